```python
import math
import jax, jax.numpy as jnp
from jax import lax
import numpy as np

D_MODEL = 1024
BATCH = 8
SEQ = 8192
DEPTH = 1

D_MIX = D_MODEL
ATT_HEADS = 8
HEAD_DIM = 64
D_ATT = ATT_HEADS * HEAD_DIM
D_CONV = D_MIX - D_ATT
DILATED_PATTERNS = ((128, 1), (512, 4), (2048, 16))
ROPE_THETA = 500000.0
ROT_DIM = HEAD_DIM // 4
CONV_WIDTH = 31
N_MEM = 256
XATT_HEADS = 4
XATT_HEAD_DIM = D_MODEL // XATT_HEADS
D_FF = 4 * D_MODEL
D_IN = 3 * D_ATT + 2 * D_CONV
EPS = 1e-6
NEG_INF = -1e30

kernel_name = "hybrid_dilated_swa_conformer_encoder"


def rmsnorm(x, g):
    xf = x.astype(jnp.float32)
    var = jnp.mean(xf * xf, axis=-1, keepdims=True)
    return (xf * lax.rsqrt(var + EPS) * g.astype(jnp.float32)).astype(x.dtype)


def layernorm(x, g, b):
    xf = x.astype(jnp.float32)
    mu = jnp.mean(xf, axis=-1, keepdims=True)
    var = jnp.mean(jnp.square(xf - mu), axis=-1, keepdims=True)
    y = (xf - mu) * lax.rsqrt(var + EPS) * g.astype(jnp.float32) + b.astype(jnp.float32)
    return y.astype(x.dtype)


def partial_rotary(t):
    S = t.shape[1]
    half = ROT_DIM // 2
    freqs = ROPE_THETA ** (-jnp.arange(0, ROT_DIM, 2, dtype=jnp.float32) / ROT_DIM)
    ang = jnp.arange(S, dtype=jnp.float32)[:, None] * freqs[None, :]
    cos = jnp.cos(ang)[None, :, None, :]
    sin = jnp.sin(ang)[None, :, None, :]
    tf = t.astype(jnp.float32)
    x1, x2, rest = tf[..., :half], tf[..., half:ROT_DIM], tf[..., ROT_DIM:]
    rot = jnp.concatenate([x1 * cos - x2 * sin, x2 * cos + x1 * sin, rest], axis=-1)
    return rot.astype(t.dtype)


def to_strided(t, d):
    B, S = t.shape[:2]
    rest = t.shape[2:]
    t = jnp.moveaxis(t.reshape(B, S // d, d, *rest), 2, 1)
    return t.reshape(B * d, S // d, *rest)


def from_strided(t, d, B):
    N, L = t.shape[:2]
    rest = t.shape[2:]
    t = jnp.moveaxis(t.reshape(B, d, L, *rest), 1, 2)
    return t.reshape(B, L * d, *rest)


def banded_attention(q, k, v, half):
    N, L, H, Dh = q.shape
    blk = half
    nb = -(-L // blk)
    Lp = nb * blk
    pad = Lp - L
    qb = jnp.pad(q, ((0, 0), (0, pad), (0, 0), (0, 0))).reshape(N, nb, blk, H, Dh)
    kp = jnp.pad(k, ((0, 0), (blk, blk + pad), (0, 0), (0, 0)))
    vp = jnp.pad(v, ((0, 0), (blk, blk + pad), (0, 0), (0, 0)))

    def window(t):
        return jnp.concatenate(
            [t[:, i * blk:i * blk + Lp].reshape(N, nb, blk, H, Dh) for i in range(3)], axis=2)

    kb, vb = window(kp), window(vp)
    s = jnp.einsum('nbqhd,nbkhd->nbhqk', qb, kb).astype(jnp.float32) * (Dh ** -0.5)
    qpos = jnp.arange(nb)[:, None] * blk + jnp.arange(blk)[None, :]
    kpos = jnp.arange(nb)[:, None] * blk - blk + jnp.arange(3 * blk)[None, :]
    valid = ((jnp.abs(kpos[:, None, :] - qpos[:, :, None]) <= half)
             & (kpos >= 0)[:, None, :] & (kpos < L)[:, None, :])
    s = jnp.where(valid[None, :, None], s, NEG_INF)
    m = jnp.max(s, axis=-1, keepdims=True)
    p = jnp.exp(s - m)
    den = jnp.sum(p, axis=-1, keepdims=True)
    o = jnp.einsum('nbhqk,nbkhd->nbqhd', (p / den).astype(v.dtype), vb)
    lse = (m + jnp.log(den))[..., 0]
    o = o.reshape(N, Lp, H, Dh)[:, :L]
    lse = jnp.transpose(lse, (0, 1, 3, 2)).reshape(N, Lp, H)[:, :L]
    return o, lse


def dilated_sliding_attention(q, k, v):
    B = q.shape[0]
    outs, lses = [], []
    for window, d in DILATED_PATTERNS:
        half = window // (2 * d)
        o, lse = banded_attention(to_strided(q, d), to_strided(k, d), to_strided(v, d), half)
        outs.append(from_strided(o, d, B))
        lses.append(from_strided(lse, d, B))
    w = jax.nn.softmax(jnp.stack(lses, axis=0), axis=0)
    o = jnp.sum(w[..., None] * jnp.stack(outs, axis=0).astype(jnp.float32), axis=0)
    return o.astype(q.dtype)


def conformer_conv(a, g, conv_w, conv_b, ln_g, ln_b):
    u = a * jax.nn.sigmoid(g)
    C = u.shape[-1]
    u = lax.conv_general_dilated(
        u, conv_w.reshape(CONV_WIDTH, 1, C).astype(u.dtype),
        window_strides=(1,), padding=[((CONV_WIDTH - 1) // 2, (CONV_WIDTH - 1) // 2)],
        dimension_numbers=('NWC', 'WIO', 'NWC'), feature_group_count=C) + conv_b
    u = layernorm(u, ln_g, ln_b)
    return jax.nn.silu(u)


def _fwd_setup_inputs(seed: int = 0) -> dict:
    key = jax.random.key(seed)
    ks = jax.random.split(key, 20)
    f32 = jnp.float32

    def w(k, shape, fan_in):
        return jax.random.normal(k, shape, f32) * (fan_in ** -0.5)

    def gain(k, shape):
        return 1.0 + 0.02 * jax.random.normal(k, shape, f32)

    return {
        "x": jax.random.normal(ks[0], (BATCH, SEQ, D_MODEL), f32),
        "mem": jax.random.normal(ks[1], (BATCH, N_MEM, D_MODEL), f32),
        "norm_mix_g": gain(ks[2], (DEPTH, D_MODEL)),
        "w_in": w(ks[3], (DEPTH, D_MODEL, D_IN), D_MODEL),
        "conv_w": w(ks[4], (DEPTH, CONV_WIDTH, D_CONV), CONV_WIDTH),
        "conv_b": 0.02 * jax.random.normal(ks[5], (DEPTH, D_CONV), f32),
        "conv_ln_g": gain(ks[6], (DEPTH, D_CONV)),
        "conv_ln_b": 0.02 * jax.random.normal(ks[7], (DEPTH, D_CONV), f32),
        "w_out": w(ks[8], (DEPTH, D_MIX, D_MODEL), D_MIX),
        "norm_x_g": gain(ks[9], (DEPTH, D_MODEL)),
        "norm_mem_g": gain(ks[10], (DEPTH, D_MODEL)),
        "w_xq": w(ks[11], (DEPTH, D_MODEL, D_MODEL), D_MODEL),
        "w_xk": w(ks[12], (DEPTH, D_MODEL, D_MODEL), D_MODEL),
        "w_xv": w(ks[13], (DEPTH, D_MODEL, D_MODEL), D_MODEL),
        "w_xo": w(ks[14], (DEPTH, D_MODEL, D_MODEL), D_MODEL),
        "norm_mlp_g": gain(ks[15], (DEPTH, D_MODEL)),
        "w_up": w(ks[16], (DEPTH, D_MODEL, D_FF), D_MODEL),
        "w_down": w(ks[17], (DEPTH, D_FF, D_MODEL), D_FF),
        "norm_final_g": gain(ks[18], (D_MODEL,)),
    }


def _fwd_reference(x, mem, norm_mix_g, w_in, conv_w, conv_b, conv_ln_g, conv_ln_b, w_out,
              norm_x_g, norm_mem_g, w_xq, w_xk, w_xv, w_xo, norm_mlp_g, w_up, w_down,
              norm_final_g):
    B, S, _ = x.shape
    M = mem.shape[1]
    h = x
    for l in range(DEPTH):
        y = rmsnorm(h, norm_mix_g[l]) @ w_in[l]
        q = partial_rotary(y[..., 0:D_ATT].reshape(B, S, ATT_HEADS, HEAD_DIM))
        k = partial_rotary(y[..., D_ATT:2 * D_ATT].reshape(B, S, ATT_HEADS, HEAD_DIM))
        v = y[..., 2 * D_ATT:3 * D_ATT].reshape(B, S, ATT_HEADS, HEAD_DIM)
        att = dilated_sliding_attention(q, k, v).reshape(B, S, D_ATT)
        c0 = 3 * D_ATT
        conv = conformer_conv(y[..., c0:c0 + D_CONV], y[..., c0 + D_CONV:c0 + 2 * D_CONV],
                              conv_w[l], conv_b[l], conv_ln_g[l], conv_ln_b[l])
        h = h + jnp.concatenate([att, conv], axis=-1) @ w_out[l]

        xq = (rmsnorm(h, norm_x_g[l]) @ w_xq[l]).reshape(B, S, XATT_HEADS, XATT_HEAD_DIM)
        mn = rmsnorm(mem, norm_mem_g[l])
        xk = (mn @ w_xk[l]).reshape(B, M, XATT_HEADS, XATT_HEAD_DIM)
        xv = (mn @ w_xv[l]).reshape(B, M, XATT_HEADS, XATT_HEAD_DIM)
        sc = jnp.einsum('bshd,bmhd->bhsm', xq, xk).astype(jnp.float32) * (XATT_HEAD_DIM ** -0.5)
        pr = jax.nn.softmax(sc, axis=-1).astype(xv.dtype)
        xo = jnp.einsum('bhsm,bmhd->bshd', pr, xv).reshape(B, S, D_MODEL)
        h = h + xo @ w_xo[l]

        u = rmsnorm(h, norm_mlp_g[l]) @ w_up[l]
        h = h + jnp.square(jax.nn.relu(u)) @ w_down[l]
    return rmsnorm(h, norm_final_g)


import jax as _jax
import jax.numpy as _jnp

TWIN_FORMAT = 'train_step'
FWD_PARAMS = ['x', 'mem', 'norm_mix_g', 'w_in', 'conv_w', 'conv_b', 'conv_ln_g', 'conv_ln_b', 'w_out', 'norm_x_g', 'norm_mem_g', 'w_xq', 'w_xk', 'w_xv', 'w_xo', 'norm_mlp_g', 'w_up', 'w_down', 'norm_final_g']
TWIN_WEIGHTS = ['norm_mix_g', 'w_in', 'conv_w', 'conv_b', 'conv_ln_g', 'conv_ln_b', 'w_out', 'norm_x_g', 'norm_mem_g', 'w_xq', 'w_xk', 'w_xv', 'w_xo', 'norm_mlp_g', 'w_up', 'w_down', 'norm_final_g']
TWIN_DIFF_INPUT = 'x'
TWIN_INPUTS = ['x', 'mem', 'norm_mix_g', 'w_in', 'conv_w', 'conv_b', 'conv_ln_g', 'conv_ln_b', 'w_out', 'norm_x_g', 'norm_mem_g', 'w_xq', 'w_xk', 'w_xv', 'w_xo', 'norm_mlp_g', 'w_up', 'w_down', 'norm_final_g', 'loss_target', 'm_norm_mix_g', 'm_w_in', 'm_conv_w', 'm_conv_b', 'm_conv_ln_g', 'm_conv_ln_b', 'm_w_out', 'm_norm_x_g', 'm_norm_mem_g', 'm_w_xq', 'm_w_xk', 'm_w_xv', 'm_w_xo', 'm_norm_mlp_g', 'm_w_up', 'm_w_down', 'm_norm_final_g', 'v_norm_mix_g', 'v_w_in', 'v_conv_w', 'v_conv_b', 'v_conv_ln_g', 'v_conv_ln_b', 'v_w_out', 'v_norm_x_g', 'v_norm_mem_g', 'v_w_xq', 'v_w_xk', 'v_w_xv', 'v_w_xo', 'v_norm_mlp_g', 'v_w_up', 'v_w_down', 'v_norm_final_g']
TWIN_OUTPUTS = ['loss', 'grad_x', 'grad_norm_mix_g', 'grad_w_in', 'grad_conv_w', 'grad_conv_b', 'grad_conv_ln_g', 'grad_conv_ln_b', 'grad_w_out', 'grad_norm_x_g', 'grad_norm_mem_g', 'grad_w_xq', 'grad_w_xk', 'grad_w_xv', 'grad_w_xo', 'grad_norm_mlp_g', 'grad_w_up', 'grad_w_down', 'grad_norm_final_g', 'delta_norm_mix_g', 'delta_w_in', 'delta_conv_w', 'delta_conv_b', 'delta_conv_ln_g', 'delta_conv_ln_b', 'delta_w_out', 'delta_norm_x_g', 'delta_norm_mem_g', 'delta_w_xq', 'delta_w_xk', 'delta_w_xv', 'delta_w_xo', 'delta_norm_mlp_g', 'delta_w_up', 'delta_w_down', 'delta_norm_final_g', 'new_m_norm_mix_g', 'new_m_w_in', 'new_m_conv_w', 'new_m_conv_b', 'new_m_conv_ln_g', 'new_m_conv_ln_b', 'new_m_w_out', 'new_m_norm_x_g', 'new_m_norm_mem_g', 'new_m_w_xq', 'new_m_w_xk', 'new_m_w_xv', 'new_m_w_xo', 'new_m_norm_mlp_g', 'new_m_w_up', 'new_m_w_down', 'new_m_norm_final_g', 'new_v_norm_mix_g', 'new_v_w_in', 'new_v_conv_w', 'new_v_conv_b', 'new_v_conv_ln_g', 'new_v_conv_ln_b', 'new_v_w_out', 'new_v_norm_x_g', 'new_v_norm_mem_g', 'new_v_w_xq', 'new_v_w_xk', 'new_v_w_xv', 'new_v_w_xo', 'new_v_norm_mlp_g', 'new_v_w_up', 'new_v_w_down', 'new_v_norm_final_g']
TWIN_LEAF_KINDS = {'loss': 'loss', 'grad_x': 'grad_x', 'grad_norm_mix_g': 'grad_w', 'grad_w_in': 'grad_w', 'grad_conv_w': 'grad_w', 'grad_conv_b': 'grad_w', 'grad_conv_ln_g': 'grad_w', 'grad_conv_ln_b': 'grad_w', 'grad_w_out': 'grad_w', 'grad_norm_x_g': 'grad_w', 'grad_norm_mem_g': 'grad_w', 'grad_w_xq': 'grad_w', 'grad_w_xk': 'grad_w', 'grad_w_xv': 'grad_w', 'grad_w_xo': 'grad_w', 'grad_norm_mlp_g': 'grad_w', 'grad_w_up': 'grad_w', 'grad_w_down': 'grad_w', 'grad_norm_final_g': 'grad_w', 'delta_norm_mix_g': 'delta_w', 'delta_w_in': 'delta_w', 'delta_conv_w': 'delta_w', 'delta_conv_b': 'delta_w', 'delta_conv_ln_g': 'delta_w', 'delta_conv_ln_b': 'delta_w', 'delta_w_out': 'delta_w', 'delta_norm_x_g': 'delta_w', 'delta_norm_mem_g': 'delta_w', 'delta_w_xq': 'delta_w', 'delta_w_xk': 'delta_w', 'delta_w_xv': 'delta_w', 'delta_w_xo': 'delta_w', 'delta_norm_mlp_g': 'delta_w', 'delta_w_up': 'delta_w', 'delta_w_down': 'delta_w', 'delta_norm_final_g': 'delta_w', 'new_m_norm_mix_g': 'new_m', 'new_m_w_in': 'new_m', 'new_m_conv_w': 'new_m', 'new_m_conv_b': 'new_m', 'new_m_conv_ln_g': 'new_m', 'new_m_conv_ln_b': 'new_m', 'new_m_w_out': 'new_m', 'new_m_norm_x_g': 'new_m', 'new_m_norm_mem_g': 'new_m', 'new_m_w_xq': 'new_m', 'new_m_w_xk': 'new_m', 'new_m_w_xv': 'new_m', 'new_m_w_xo': 'new_m', 'new_m_norm_mlp_g': 'new_m', 'new_m_w_up': 'new_m', 'new_m_w_down': 'new_m', 'new_m_norm_final_g': 'new_m', 'new_v_norm_mix_g': 'new_v', 'new_v_w_in': 'new_v', 'new_v_conv_w': 'new_v', 'new_v_conv_b': 'new_v', 'new_v_conv_ln_g': 'new_v', 'new_v_conv_ln_b': 'new_v', 'new_v_w_out': 'new_v', 'new_v_norm_x_g': 'new_v', 'new_v_norm_mem_g': 'new_v', 'new_v_w_xq': 'new_v', 'new_v_w_xk': 'new_v', 'new_v_w_xv': 'new_v', 'new_v_w_xo': 'new_v', 'new_v_norm_mlp_g': 'new_v', 'new_v_w_up': 'new_v', 'new_v_w_down': 'new_v', 'new_v_norm_final_g': 'new_v'}


def _forward(args):
    return _fwd_reference(*[args[k] for k in FWD_PARAMS])


def _output_shape():
    def fwd():
        inp = _fwd_setup_inputs(0)
        return _fwd_reference(*[inp[k] for k in FWD_PARAMS])
    out = _jax.eval_shape(fwd)
    return out.shape, out.dtype

N_MICROBATCH = 1
ADAM_LR = 0.001
ADAM_B1 = 0.9
ADAM_B2 = 0.999
ADAM_EPS = 1e-08
ADAM_WD = 0.01
ADAM_STEP = 10
PER_EXAMPLE_BATCH_AXIS = {'x': 0, 'mem': 0, 'loss_target': 0}
SHARED_INPUTS = []
_WEIGHT_DTYPES = {'norm_mix_g': _jnp.float32, 'w_in': _jnp.float32, 'conv_w': _jnp.float32, 'conv_b': _jnp.float32, 'conv_ln_g': _jnp.float32, 'conv_ln_b': _jnp.float32, 'w_out': _jnp.float32, 'norm_x_g': _jnp.float32, 'norm_mem_g': _jnp.float32, 'w_xq': _jnp.float32, 'w_xk': _jnp.float32, 'w_xv': _jnp.float32, 'w_xo': _jnp.float32, 'norm_mlp_g': _jnp.float32, 'w_up': _jnp.float32, 'w_down': _jnp.float32, 'norm_final_g': _jnp.float32}
MOMENT_SCALE = {'norm_mix_g': 1.269351e-01, 'w_in': 7.956738e-02, 'conv_w': 1.544645e-01, 'conv_b': 4.593084e-01, 'conv_ln_g': 2.263117e-01, 'conv_ln_b': 2.513922e-01, 'w_out': 1.179817e-01, 'norm_x_g': 2.657040e-02, 'norm_mem_g': 3.968971e-02, 'w_xq': 2.636668e-02, 'w_xk': 2.653164e-02, 'w_xv': 2.784111e-02, 'w_xo': 2.760950e-02, 'norm_mlp_g': 2.172729e-01, 'w_up': 1.095887e-01, 'w_down': 2.310493e-01, 'norm_final_g': 6.460015e+01}


def _to_microbatches(a, axis):
    t = _jnp.moveaxis(a, axis, 0)
    t = t.reshape((N_MICROBATCH, t.shape[0] // N_MICROBATCH) + t.shape[1:])
    return _jnp.moveaxis(t, 1, axis + 1)


def setup_inputs(seed: int = 0) -> dict:
    inp = _fwd_setup_inputs(seed)
    key = _jax.random.fold_in(_jax.random.key(seed), 7919)
    shape, _ = _output_shape()
    out = dict(inp)
    out["loss_target"] = _jax.random.normal(_jax.random.fold_in(key, 0), shape, _jnp.float32)
    for i, name in enumerate(TWIN_WEIGHTS):
        w = inp[name].astype(_jnp.float32)
        if MOMENT_SCALE is None:
            s = _jnp.sqrt(_jnp.mean(_jnp.square(w)) + 1e-30)
        else:
            s = MOMENT_SCALE[name]
        km, kv = _jax.random.split(_jax.random.fold_in(key, i + 1))
        out[name] = w
        out["m_" + name] = s * _jax.random.normal(km, w.shape, _jnp.float32)
        out["v_" + name] = (s * s) * _jax.random.uniform(kv, w.shape, _jnp.float32, 0.5, 1.5)
    if N_MICROBATCH > 1:
        for name, axis in PER_EXAMPLE_BATCH_AXIS.items():
            out[name] = _to_microbatches(out[name], axis)
    return {'x': out['x'], 'mem': out['mem'], 'norm_mix_g': out['norm_mix_g'], 'w_in': out['w_in'], 'conv_w': out['conv_w'], 'conv_b': out['conv_b'], 'conv_ln_g': out['conv_ln_g'], 'conv_ln_b': out['conv_ln_b'], 'w_out': out['w_out'], 'norm_x_g': out['norm_x_g'], 'norm_mem_g': out['norm_mem_g'], 'w_xq': out['w_xq'], 'w_xk': out['w_xk'], 'w_xv': out['w_xv'], 'w_xo': out['w_xo'], 'norm_mlp_g': out['norm_mlp_g'], 'w_up': out['w_up'], 'w_down': out['w_down'], 'norm_final_g': out['norm_final_g'], 'loss_target': out['loss_target'], 'm_norm_mix_g': out['m_norm_mix_g'], 'm_w_in': out['m_w_in'], 'm_conv_w': out['m_conv_w'], 'm_conv_b': out['m_conv_b'], 'm_conv_ln_g': out['m_conv_ln_g'], 'm_conv_ln_b': out['m_conv_ln_b'], 'm_w_out': out['m_w_out'], 'm_norm_x_g': out['m_norm_x_g'], 'm_norm_mem_g': out['m_norm_mem_g'], 'm_w_xq': out['m_w_xq'], 'm_w_xk': out['m_w_xk'], 'm_w_xv': out['m_w_xv'], 'm_w_xo': out['m_w_xo'], 'm_norm_mlp_g': out['m_norm_mlp_g'], 'm_w_up': out['m_w_up'], 'm_w_down': out['m_w_down'], 'm_norm_final_g': out['m_norm_final_g'], 'v_norm_mix_g': out['v_norm_mix_g'], 'v_w_in': out['v_w_in'], 'v_conv_w': out['v_conv_w'], 'v_conv_b': out['v_conv_b'], 'v_conv_ln_g': out['v_conv_ln_g'], 'v_conv_ln_b': out['v_conv_ln_b'], 'v_w_out': out['v_w_out'], 'v_norm_x_g': out['v_norm_x_g'], 'v_norm_mem_g': out['v_norm_mem_g'], 'v_w_xq': out['v_w_xq'], 'v_w_xk': out['v_w_xk'], 'v_w_xv': out['v_w_xv'], 'v_w_xo': out['v_w_xo'], 'v_norm_mlp_g': out['v_norm_mlp_g'], 'v_w_up': out['v_w_up'], 'v_w_down': out['v_w_down'], 'v_norm_final_g': out['v_norm_final_g']}


def _loss(weights, diff, rest, loss_target):
    with _jax.named_scope("forward"):
        args = {**rest, TWIN_DIFF_INPUT: diff, **{k: w.astype(_WEIGHT_DTYPES[k]) for k, w in weights.items()}}
        y = _forward(args)
    with _jax.named_scope("loss_head"):
        err = _jnp.square(y.astype(_jnp.float32) - loss_target)
        return 0.5 * _jnp.sum(_jnp.mean(err, axis=-1)) if err.ndim else 0.5 * err


def _adamw(w, g, m, v):
    m = ADAM_B1 * m + (1.0 - ADAM_B1) * g
    v = ADAM_B2 * v + (1.0 - ADAM_B2) * _jnp.square(g)
    m_hat = m / (1.0 - ADAM_B1 ** ADAM_STEP)
    v_hat = v / (1.0 - ADAM_B2 ** ADAM_STEP)
    delta = -ADAM_LR * (m_hat / (_jnp.sqrt(v_hat) + ADAM_EPS) + ADAM_WD * w)
    return delta, m, v


def reference(x, mem, norm_mix_g, w_in, conv_w, conv_b, conv_ln_g, conv_ln_b, w_out, norm_x_g, norm_mem_g, w_xq, w_xk, w_xv, w_xo, norm_mlp_g, w_up, w_down, norm_final_g, loss_target, m_norm_mix_g, m_w_in, m_conv_w, m_conv_b, m_conv_ln_g, m_conv_ln_b, m_w_out, m_norm_x_g, m_norm_mem_g, m_w_xq, m_w_xk, m_w_xv, m_w_xo, m_norm_mlp_g, m_w_up, m_w_down, m_norm_final_g, v_norm_mix_g, v_w_in, v_conv_w, v_conv_b, v_conv_ln_g, v_conv_ln_b, v_w_out, v_norm_x_g, v_norm_mem_g, v_w_xq, v_w_xk, v_w_xv, v_w_xo, v_norm_mlp_g, v_w_up, v_w_down, v_norm_final_g):
    given = dict(x=x, mem=mem, norm_mix_g=norm_mix_g, w_in=w_in, conv_w=conv_w, conv_b=conv_b, conv_ln_g=conv_ln_g, conv_ln_b=conv_ln_b, w_out=w_out, norm_x_g=norm_x_g, norm_mem_g=norm_mem_g, w_xq=w_xq, w_xk=w_xk, w_xv=w_xv, w_xo=w_xo, norm_mlp_g=norm_mlp_g, w_up=w_up, w_down=w_down, norm_final_g=norm_final_g, loss_target=loss_target, m_norm_mix_g=m_norm_mix_g, m_w_in=m_w_in, m_conv_w=m_conv_w, m_conv_b=m_conv_b, m_conv_ln_g=m_conv_ln_g, m_conv_ln_b=m_conv_ln_b, m_w_out=m_w_out, m_norm_x_g=m_norm_x_g, m_norm_mem_g=m_norm_mem_g, m_w_xq=m_w_xq, m_w_xk=m_w_xk, m_w_xv=m_w_xv, m_w_xo=m_w_xo, m_norm_mlp_g=m_norm_mlp_g, m_w_up=m_w_up, m_w_down=m_w_down, m_norm_final_g=m_norm_final_g, v_norm_mix_g=v_norm_mix_g, v_w_in=v_w_in, v_conv_w=v_conv_w, v_conv_b=v_conv_b, v_conv_ln_g=v_conv_ln_g, v_conv_ln_b=v_conv_ln_b, v_w_out=v_w_out, v_norm_x_g=v_norm_x_g, v_norm_mem_g=v_norm_mem_g, v_w_xq=v_w_xq, v_w_xk=v_w_xk, v_w_xv=v_w_xv, v_w_xo=v_w_xo, v_norm_mlp_g=v_norm_mlp_g, v_w_up=v_w_up, v_w_down=v_w_down, v_norm_final_g=v_norm_final_g)
    weights = {n: given[n] for n in TWIN_WEIGHTS}
    shared = {n: given[n] for n in SHARED_INPUTS}
    per_example = {n: given[n] for n in ['x', 'mem']}
    grad_fn = _jax.value_and_grad(_loss, argnums=(0, 1))

    def one_microbatch(ex, loss_target):
        ex = dict(ex)
        diff = ex.pop(TWIN_DIFF_INPUT)
        return grad_fn(weights, diff, {**shared, **ex}, loss_target)

    if N_MICROBATCH == 1:
        loss, (grad_w, grad_x) = one_microbatch(per_example, given["loss_target"])
    else:
        def body(carry, xs):
            loss_sum, grad_sum = carry
            l_k, (gw_k, gx_k) = one_microbatch(xs[0], xs[1])
            with _jax.named_scope("update"):
                return (loss_sum + l_k, _jax.tree.map(_jnp.add, grad_sum, gw_k)), gx_k

        init = (_jnp.zeros((), _jnp.float32), _jax.tree.map(_jnp.zeros_like, weights))
        (loss, grad_w), grad_x = _jax.lax.scan(body, init, (per_example, given["loss_target"]))
    with _jax.named_scope("update"):
        delta_w, new_m, new_v = {}, {}, {}
        for n in TWIN_WEIGHTS:
            delta_w[n], new_m[n], new_v[n] = _adamw(weights[n], grad_w[n], given["m_" + n], given["v_" + n])
    return (loss, grad_x, *[grad_w[n] for n in TWIN_WEIGHTS], *[delta_w[n] for n in TWIN_WEIGHTS],
            *[new_m[n] for n in TWIN_WEIGHTS], *[new_v[n] for n in TWIN_WEIGHTS])
```

```python
import functools
import math

import jax
import jax.numpy as jnp
from jax import lax
from jax.experimental import pallas as pl
from jax.experimental.pallas import tpu as pltpu

F32 = jnp.float32
BF16 = jnp.bfloat16
MESH = pl.DeviceIdType.MESH

D_MODEL = 1024
ATT_HEADS = 8
HEAD_DIM = 64
D_ATT = ATT_HEADS * HEAD_DIM
D_CONV = D_MODEL - D_ATT
DILATIONS = (1, 4, 16)
HALF = 64
ROPE_THETA = 500000.0
ROT_DIM = HEAD_DIM // 4
CONV_WIDTH = 31
CONV_PAD = (CONV_WIDTH - 1) // 2
XATT_HEADS = 4
XATT_HEAD_DIM = D_MODEL // XATT_HEADS
D_FF = 4 * D_MODEL
D_IN = 3 * D_ATT + 2 * D_CONV
EPS = 1e-6
NEG_INF = -1e30
N_CHIPS = 4
N_DEV = 8

ADAM_LR = 0.001
ADAM_B1 = 0.9
ADAM_B2 = 0.999
ADAM_EPS = 1e-08
ADAM_WD = 0.01
ADAM_STEP = 10

VMEM_LIMIT_V7X = 56 * 1024 * 1024
HALO = 16
ATT_BLOCK = 128
SMALL_W = 512


def _params(*sem):
    return pltpu.CompilerParams(dimension_semantics=sem, vmem_limit_bytes=VMEM_LIMIT_V7X)


def _sds(shape, dtype):
    return jax.ShapeDtypeStruct(shape, dtype)


def _mm_nn(a, w3, *, name, out_dtype=BF16, res=None, relu2=False, tm=1024, tn=None, tk=1024):
    M, K = a.shape
    nsh, _, n = w3.shape
    tm, tk = min(tm, M), min(tk, K)
    tn = tn or min(n, 1024)
    npt, nk = n // tn, K // tk
    nj, N = nsh * npt, nsh * n
    n_out = 2 if relu2 else 1

    def body(*refs):
        a_ref, w_ref = refs[0], refs[1]
        pos = 2
        res_ref = None
        if res is not None:
            res_ref = refs[pos]
            pos += 1
        outs = refs[pos:pos + n_out]
        acc_ref = refs[pos + n_out] if nk > 1 else None

        def finish(acc):
            if res_ref is not None:
                acc = acc + res_ref[...]
            if relu2:
                r = jnp.maximum(acc, 0.0)
                outs[0][...] = r.astype(outs[0].dtype)
                outs[1][...] = (r * r).astype(outs[1].dtype)
            else:
                outs[0][...] = acc.astype(outs[0].dtype)

        part = jnp.dot(a_ref[...], w_ref[...], preferred_element_type=F32)
        if nk == 1:
            finish(part)
        else:
            k = pl.program_id(2)

            @pl.when(k == 0)
            def _():
                acc_ref[...] = part

            @pl.when(k > 0)
            def _():
                acc_ref[...] += part

            @pl.when(k == nk - 1)
            def _():
                finish(acc_ref[...])

    in_specs = [pl.BlockSpec((tm, tk), lambda i, j, k: (i, k)),
                pl.BlockSpec((None, tk, tn), lambda i, j, k: (j // npt, k, j % npt))]
    args = [a, w3]
    if res is not None:
        in_specs.append(pl.BlockSpec((tm, tn), lambda i, j, k: (i, j)))
        args.append(res)
    out_spec = pl.BlockSpec((tm, tn), lambda i, j, k: (i, j))
    out = pl.pallas_call(
        body, name=name, grid=(M // tm, nj, nk), in_specs=in_specs,
        out_specs=[out_spec] * n_out, out_shape=[_sds((M, N), out_dtype)] * n_out,
        scratch_shapes=[pltpu.VMEM((tm, tn), F32)] if nk > 1 else [],
        compiler_params=_params("parallel", "parallel", "arbitrary"))(*args)
    return tuple(out) if relu2 else out[0]


def _mm_nt(dy, w3, *, name, out_dtype=F32, mul=None, tm=1024, tn=None, tko=1024):
    M, N = dy.shape
    nsh, K, n = w3.shape
    tm, tko = min(tm, M), min(tko, K)
    tn = tn or min(n, 1024)
    npt = n // tn
    nj = nsh * npt

    def body(*refs):
        dy_ref, w_ref = refs[0], refs[1]
        pos = 2
        mul_ref = None
        if mul is not None:
            mul_ref = refs[pos]
            pos += 1
        out_ref = refs[pos]
        acc_ref = refs[pos + 1] if nj > 1 else None

        def finish(acc):
            if mul_ref is not None:
                acc = acc * (2.0 * mul_ref[...].astype(F32))
            out_ref[...] = acc.astype(out_ref.dtype)

        part = lax.dot_general(dy_ref[...], w_ref[...], (((1,), (1,)), ((), ())), preferred_element_type=F32)
        if nj == 1:
            finish(part)
        else:
            j = pl.program_id(2)

            @pl.when(j == 0)
            def _():
                acc_ref[...] = part

            @pl.when(j > 0)
            def _():
                acc_ref[...] += part

            @pl.when(j == nj - 1)
            def _():
                finish(acc_ref[...])

    in_specs = [pl.BlockSpec((tm, tn), lambda i, ko, j: (i, j)),
                pl.BlockSpec((None, tko, tn), lambda i, ko, j: (j // npt, ko, j % npt))]
    args = [dy, w3]
    if mul is not None:
        in_specs.append(pl.BlockSpec((tm, tko), lambda i, ko, j: (i, ko)))
        args.append(mul)
    return pl.pallas_call(
        body, name=name, grid=(M // tm, K // tko, nj), in_specs=in_specs,
        out_specs=pl.BlockSpec((tm, tko), lambda i, ko, j: (i, ko)), out_shape=_sds((M, K), out_dtype),
        scratch_shapes=[pltpu.VMEM((tm, tko), F32)] if nj > 1 else [],
        compiler_params=_params("parallel", "parallel", "arbitrary"))(*args)


def _mm_tn(a, dy, nsh, *, name, out_dtype=BF16, tm=1024, tk=1024, tn=None):
    M, K = a.shape
    N = dy.shape[1]
    n = N // nsh
    tm, tk = min(tm, M), min(tk, K)
    tn = tn or min(n, 1024)
    npt = n // tn
    nj, nm = nsh * npt, M // tm

    def body(a_ref, dy_ref, out_ref, acc_ref):
        m = pl.program_id(2)
        part = lax.dot_general(a_ref[...], dy_ref[...], (((0,), (0,)), ((), ())), preferred_element_type=F32)

        @pl.when(m == 0)
        def _():
            acc_ref[...] = part

        @pl.when(m > 0)
        def _():
            acc_ref[...] += part

        @pl.when(m == nm - 1)
        def _():
            out_ref[...] = acc_ref[...].astype(out_ref.dtype)

    return pl.pallas_call(
        body, name=name, grid=(K // tk, nj, nm),
        in_specs=[pl.BlockSpec((tm, tk), lambda kk, j, m: (m, kk)),
                  pl.BlockSpec((tm, tn), lambda kk, j, m: (m, j))],
        out_specs=pl.BlockSpec((None, tk, tn), lambda kk, j, m: (j // npt, kk, j % npt)),
        out_shape=_sds((nsh, K, n), out_dtype),
        scratch_shapes=[pltpu.VMEM((tk, tn), F32)],
        compiler_params=_params("parallel", "parallel", "arbitrary"))(a, dy)


def _rms_fwd(x, g, *, name, tm=512):
    M, Dm = x.shape
    tm = min(tm, M)

    def body(x_ref, g_ref, o_ref):
        xf = x_ref[...]
        r = lax.rsqrt(jnp.mean(xf * xf, axis=-1, keepdims=True) + EPS)
        o_ref[...] = (xf * r * g_ref[...]).astype(o_ref.dtype)

    return pl.pallas_call(
        body, name=name, grid=(M // tm,),
        in_specs=[pl.BlockSpec((tm, Dm), lambda i: (i, 0)), pl.BlockSpec((1, Dm), lambda i: (0, 0))],
        out_specs=pl.BlockSpec((tm, Dm), lambda i: (i, 0)), out_shape=_sds((M, Dm), BF16),
        compiler_params=_params("parallel"))(x, g)


def _rms_bwd(dxn, x, g, dres, *, name, tm=512):
    M, Dm = x.shape
    tm = min(tm, M)
    has_res = dres is not None

    def body(*refs):
        dxn_ref, x_ref, g_ref = refs[:3]
        dres_ref = refs[3] if has_res else None
        dx_ref, dg_ref = refs[-2:]
        i = pl.program_id(0)
        xf = x_ref[...]
        r = lax.rsqrt(jnp.mean(xf * xf, axis=-1, keepdims=True) + EPS)
        nrm = xf * r
        dxn_f = dxn_ref[...].astype(F32)
        dn = dxn_f * g_ref[...]
        dx = r * (dn - nrm * jnp.mean(dn * nrm, axis=-1, keepdims=True))
        if has_res:
            dx = dx + dres_ref[...]
        dx_ref[...] = dx

        @pl.when(i == 0)
        def _():
            dg_ref[...] = jnp.zeros_like(dg_ref)

        dg_ref[0:1, :] += jnp.sum(dxn_f * nrm, axis=0, keepdims=True)

    row = pl.BlockSpec((tm, Dm), lambda i: (i, 0))
    in_specs = [row, row, pl.BlockSpec((1, Dm), lambda i: (0, 0))] + ([row] if has_res else [])
    args = [dxn, x, g] + ([dres] if has_res else [])
    return pl.pallas_call(
        body, name=name, grid=(M // tm,), in_specs=in_specs,
        out_specs=[row, pl.BlockSpec((8, Dm), lambda i: (0, 0))],
        out_shape=[_sds((M, Dm), F32), _sds((8, Dm), F32)],
        compiler_params=_params("arbitrary"))(*args)


def _loss_head(h, g, target, *, tm=512):
    M, Dm = h.shape
    tm = min(tm, M)

    def body(h_ref, g_ref, t_ref, dh_ref, dg_ref, loss_ref):
        i = pl.program_id(0)
        hf = h_ref[...]
        r = lax.rsqrt(jnp.mean(hf * hf, axis=-1, keepdims=True) + EPS)
        nrm = hf * r
        gv = g_ref[...]
        err = nrm * gv - t_ref[...]
        dy = err * (1.0 / Dm)
        dn = dy * gv
        dh_ref[...] = r * (dn - nrm * jnp.mean(dn * nrm, axis=-1, keepdims=True))

        @pl.when(i == 0)
        def _():
            dg_ref[...] = jnp.zeros_like(dg_ref)
            loss_ref[...] = jnp.zeros_like(loss_ref)

        dg_ref[0:1, :] += jnp.sum(dy * nrm, axis=0, keepdims=True)
        part = 0.5 * jnp.sum(jnp.mean(err * err, axis=-1, keepdims=True), axis=0, keepdims=True)
        sel = (lax.broadcasted_iota(jnp.int32, (8, 128), 0) == 0) & (lax.broadcasted_iota(jnp.int32, (8, 128), 1) == 0)
        loss_ref[...] += jnp.where(sel, part, 0.0)

    row = pl.BlockSpec((tm, Dm), lambda i: (i, 0))
    return pl.pallas_call(
        body, name="loss_head", grid=(M // tm,),
        in_specs=[row, pl.BlockSpec((1, Dm), lambda i: (0, 0)), row],
        out_specs=[row, pl.BlockSpec((8, Dm), lambda i: (0, 0)), pl.BlockSpec((8, 128), lambda i: (0, 0))],
        out_shape=[_sds((M, Dm), F32), _sds((8, Dm), F32), _sds((8, 128), F32)],
        compiler_params=_params("arbitrary"))(h, g, target)


def _rope_tables(S):
    half = ROT_DIM // 2
    freqs = ROPE_THETA ** (-jnp.arange(0, ROT_DIM, 2, dtype=F32) / ROT_DIM)
    ang = jnp.arange(S, dtype=F32)[:, None] * freqs[None, :]
    cos, sin = jnp.cos(ang), jnp.sin(ang)
    ones = jnp.ones((S, HEAD_DIM - ROT_DIM), F32)
    zeros = jnp.zeros((S, HEAD_DIM - ROT_DIM), F32)
    zh = jnp.zeros((S, half), F32)
    c = jnp.concatenate([cos, cos, ones], axis=1)
    sa = jnp.concatenate([-sin, zh, zeros], axis=1)
    sb = jnp.concatenate([zh, sin, zeros], axis=1)
    return tuple(jnp.tile(t, (1, 128 // HEAD_DIM)) for t in (c, sa, sb))


def _rope_fwd(y, tables, *, tm=512):
    S = y.shape[0]
    W = 2 * D_ATT
    tm = min(tm, S)
    half = ROT_DIM // 2

    def body(y_ref, c_ref, sa_ref, sb_ref, o_ref):
        t = y_ref[...].astype(F32)
        rep = W // 128
        c, sa, sb = (jnp.tile(r[...], (1, rep)) for r in (c_ref, sa_ref, sb_ref))
        rot = t * c + pltpu.roll(t, W - half, axis=1) * sa + pltpu.roll(t, half, axis=1) * sb
        o_ref[...] = rot.astype(o_ref.dtype)

    tab = pl.BlockSpec((tm, 128), lambda i: (i, 0))
    return pl.pallas_call(
        body, name="rope_fwd", grid=(S // tm,),
        in_specs=[pl.BlockSpec((tm, W), lambda i: (i, 0)), tab, tab, tab],
        out_specs=pl.BlockSpec((tm, W), lambda i: (i, 0)), out_shape=_sds((S, W), BF16),
        compiler_params=_params("parallel"))(y, *tables)


def _assemble_dy(dq, dk, dv, dag, tables, *, tm=512):
    S = dag.shape[0]
    tm = min(tm, S)
    half = ROT_DIM // 2
    W = D_ATT

    def body(*refs):
        dq_refs, dk_refs, dv_refs = refs[0:3], refs[3:6], refs[6:9]
        dag_ref, c_ref, sa_ref, sb_ref, o_ref = refs[9:]
        rep = W // 128
        c, sa, sb = (jnp.tile(r[...], (1, rep)) for r in (c_ref, sa_ref, sb_ref))

        def total(rs):
            return rs[0][...].astype(F32) + rs[1][...].astype(F32) + rs[2][...].astype(F32)

        def unrope(dr):
            return dr * c + pltpu.roll(dr * sa, half, axis=1) + pltpu.roll(dr * sb, W - half, axis=1)

        o_ref[:, 0:W] = unrope(total(dq_refs)).astype(o_ref.dtype)
        o_ref[:, W:2 * W] = unrope(total(dk_refs)).astype(o_ref.dtype)
        o_ref[:, 2 * W:3 * W] = total(dv_refs).astype(o_ref.dtype)
        o_ref[:, 3 * W:] = dag_ref[...]

    blk = pl.BlockSpec((tm, W), lambda i: (i, 0))
    tab = pl.BlockSpec((tm, 128), lambda i: (i, 0))
    return pl.pallas_call(
        body, name="assemble_dy", grid=(S // tm,),
        in_specs=[blk] * 9 + [pl.BlockSpec((tm, 2 * D_CONV), lambda i: (i, 0)), tab, tab, tab],
        out_specs=pl.BlockSpec((tm, D_IN), lambda i: (i, 0)), out_shape=_sds((S, D_IN), BF16),
        compiler_params=_params("parallel"))(*dq, *dk, *dv, dag, *tables)


def _att_specs(L, tq, width, col_of_r):
    per = tq // HALF
    last = L // HALF - 1
    centre = pl.BlockSpec((tq, width), lambda r, i: (i, col_of_r(r)))
    prev = pl.BlockSpec((HALF, width), lambda r, i: (jnp.maximum(i * per - 1, 0), col_of_r(r)))
    nxt = pl.BlockSpec((HALF, width), lambda r, i: (jnp.minimum((i + 1) * per, last), col_of_r(r)))
    return centre, prev, nxt


def _band_mask(i, tq, L, centre_is_query):
    if centre_is_query:
        shape = (tq, tq + 2 * HALF)
        c_idx = lax.broadcasted_iota(jnp.int32, shape, 0)
        w_idx = lax.broadcasted_iota(jnp.int32, shape, 1)
    else:
        shape = (tq + 2 * HALF, tq)
        w_idx = lax.broadcasted_iota(jnp.int32, shape, 0)
        c_idx = lax.broadcasted_iota(jnp.int32, shape, 1)
    diff = w_idx - c_idx
    wpos = i * tq - HALF + w_idx
    return (diff >= 0) & (diff <= 2 * HALF) & (wpos >= 0) & (wpos < L)


def _att_fwd(qk, y, d, *, name):
    S = qk.shape[0]
    L = S // d
    tq = min(ATT_BLOCK, L)
    qk_v = qk.reshape(L, d * 2 * D_ATT)
    y_v = y.reshape(L, d * D_IN)
    scale = HEAD_DIM ** -0.5

    def body(q_ref, kp_ref, kc_ref, kn_ref, vp_ref, vc_ref, vn_ref, o_ref, lse_ref):
        i = pl.program_id(1)
        valid = _band_mask(i, tq, L, True)
        q = q_ref[...]
        kwin = jnp.concatenate([kp_ref[...], kc_ref[...], kn_ref[...]], axis=0)
        vwin = jnp.concatenate([vp_ref[...], vc_ref[...], vn_ref[...]], axis=0)
        for h in range(ATT_HEADS):
            sl = slice(h * HEAD_DIM, (h + 1) * HEAD_DIM)
            s = lax.dot_general(q[:, sl], kwin[:, sl], (((1,), (1,)), ((), ())), preferred_element_type=F32) * scale
            s = jnp.where(valid, s, NEG_INF)
            m = jnp.max(s, axis=-1, keepdims=True)
            p = jnp.exp(s - m)
            den = jnp.sum(p, axis=-1, keepdims=True)
            o = jnp.dot((p / den).astype(BF16), vwin[:, sl], preferred_element_type=F32)
            o_ref[:, sl] = o
            lse_ref[:, sl] = jnp.broadcast_to(m + jnp.log(den), (tq, HEAD_DIM))

    qc, _, _ = _att_specs(L, tq, D_ATT, lambda r: 2 * r)
    kc, kp, kn = _att_specs(L, tq, D_ATT, lambda r: 2 * r + 1)
    vc, vp, vn = _att_specs(L, tq, D_ATT, lambda r: 5 * r + 2)
    out = pl.BlockSpec((tq, D_ATT), lambda r, i: (i, r))
    o, lse = pl.pallas_call(
        body, name=name, grid=(d, L // tq),
        in_specs=[qc, kp, kc, kn, vp, vc, vn], out_specs=[out, out],
        out_shape=[_sds((L, d * D_ATT), F32)] * 2,
        compiler_params=_params("parallel", "parallel"))(qk_v, qk_v, qk_v, qk_v, y_v, y_v, y_v)
    return o.reshape(S, D_ATT), lse.reshape(S, D_ATT)


def _att_combine(outs, lses, *, tm=512):
    S = outs[0].shape[0]
    tm = min(tm, S)

    def body(o1, o2, o3, l1, l2, l3, att_ref, lg_ref):
        a, b, c = l1[...], l2[...], l3[...]
        mx = jnp.maximum(jnp.maximum(a, b), c)
        ea, eb, ec = jnp.exp(a - mx), jnp.exp(b - mx), jnp.exp(c - mx)
        tot = ea + eb + ec
        att_ref[...] = ((ea * o1[...] + eb * o2[...] + ec * o3[...]) / tot).astype(att_ref.dtype)
        lg_ref[...] = mx + jnp.log(tot)

    blk = pl.BlockSpec((tm, D_ATT), lambda i: (i, 0))
    return pl.pallas_call(
        body, name="att_combine", grid=(S // tm,), in_specs=[blk] * 6, out_specs=[blk, blk],
        out_shape=[_sds((S, D_ATT), BF16), _sds((S, D_ATT), F32)],
        compiler_params=_params("parallel"))(*outs, *lses)


def _att_delta(dac, att, *, tm=512):
    S = att.shape[0]
    tm = min(tm, S)

    def body(do_ref, o_ref, out_ref):
        prod = do_ref[...].astype(F32) * o_ref[...].astype(F32)
        for h in range(ATT_HEADS):
            sl = slice(h * HEAD_DIM, (h + 1) * HEAD_DIM)
            out_ref[:, sl] = jnp.broadcast_to(jnp.sum(prod[:, sl], axis=-1, keepdims=True), (tm, HEAD_DIM))

    blk = pl.BlockSpec((tm, D_ATT), lambda i: (i, 0))
    return pl.pallas_call(
        body, name="att_delta", grid=(S // tm,), in_specs=[blk, blk], out_specs=blk,
        out_shape=_sds((S, D_ATT), F32), compiler_params=_params("parallel"))(dac, att)


def _att_dq(qk, y, dac, lg, delta, d, *, name):
    S = qk.shape[0]
    L = S // d
    tq = min(ATT_BLOCK, L)
    qk_v = qk.reshape(L, d * 2 * D_ATT)
    y_v = y.reshape(L, d * D_IN)
    dac_v = dac.reshape(L, d * 2 * D_ATT)
    lg_v = lg.reshape(L, d * D_ATT)
    dl_v = delta.reshape(L, d * D_ATT)
    scale = HEAD_DIM ** -0.5

    def body(q_ref, kp_ref, kc_ref, kn_ref, vp_ref, vc_ref, vn_ref, do_ref, lg_ref, dl_ref, dq_ref):
        i = pl.program_id(1)
        valid = _band_mask(i, tq, L, True)
        q, do = q_ref[...], do_ref[...]
        kwin = jnp.concatenate([kp_ref[...], kc_ref[...], kn_ref[...]], axis=0)
        vwin = jnp.concatenate([vp_ref[...], vc_ref[...], vn_ref[...]], axis=0)
        for h in range(ATT_HEADS):
            sl = slice(h * HEAD_DIM, (h + 1) * HEAD_DIM)
            c0 = h * HEAD_DIM
            s = lax.dot_general(q[:, sl], kwin[:, sl], (((1,), (1,)), ((), ())), preferred_element_type=F32) * scale
            s = jnp.where(valid, s, NEG_INF)
            p = jnp.exp(s - lg_ref[:, c0:c0 + 1])
            dp = lax.dot_general(do[:, sl], vwin[:, sl], (((1,), (1,)), ((), ())), preferred_element_type=F32)
            ds = p * (dp - dl_ref[:, c0:c0 + 1])
            dq = jnp.dot(ds.astype(BF16), kwin[:, sl], preferred_element_type=F32) * scale
            dq_ref[:, sl] = dq.astype(dq_ref.dtype)

    qc, _, _ = _att_specs(L, tq, D_ATT, lambda r: 2 * r)
    kc, kp, kn = _att_specs(L, tq, D_ATT, lambda r: 2 * r + 1)
    vc, vp, vn = _att_specs(L, tq, D_ATT, lambda r: 5 * r + 2)
    row = pl.BlockSpec((tq, D_ATT), lambda r, i: (i, r))
    dq = pl.pallas_call(
        body, name=name, grid=(d, L // tq),
        in_specs=[qc, kp, kc, kn, vp, vc, vn, qc, row, row], out_specs=row,
        out_shape=_sds((L, d * D_ATT), BF16),
        compiler_params=_params("parallel", "parallel"))(qk_v, qk_v, qk_v, qk_v, y_v, y_v, y_v, dac_v, lg_v, dl_v)
    return dq.reshape(S, D_ATT)


def _att_dkv(qk, y, dac, lg, delta, d, *, name):
    S = qk.shape[0]
    L = S // d
    tk = min(ATT_BLOCK, L)
    qk_v = qk.reshape(L, d * 2 * D_ATT)
    y_v = y.reshape(L, d * D_IN)
    dac_v = dac.reshape(L, d * 2 * D_ATT)
    lg_v = lg.reshape(L, d * D_ATT)
    dl_v = delta.reshape(L, d * D_ATT)
    scale = HEAD_DIM ** -0.5

    def body(k_ref, v_ref, qp_ref, qc_ref, qn_ref, dop_ref, doc_ref, don_ref,
             lgp_ref, lgc_ref, lgn_ref, dlp_ref, dlc_ref, dln_ref, dk_ref, dv_ref):
        i = pl.program_id(1)
        valid = _band_mask(i, tk, L, False)
        k, v = k_ref[...], v_ref[...]
        qwin = jnp.concatenate([qp_ref[...], qc_ref[...], qn_ref[...]], axis=0)
        dowin = jnp.concatenate([dop_ref[...], doc_ref[...], don_ref[...]], axis=0)
        for h in range(ATT_HEADS):
            sl = slice(h * HEAD_DIM, (h + 1) * HEAD_DIM)
            c0 = h * HEAD_DIM
            lgw = jnp.concatenate([r[:, c0:c0 + 1] for r in (lgp_ref, lgc_ref, lgn_ref)], axis=0)
            dlw = jnp.concatenate([r[:, c0:c0 + 1] for r in (dlp_ref, dlc_ref, dln_ref)], axis=0)
            s = lax.dot_general(qwin[:, sl], k[:, sl], (((1,), (1,)), ((), ())), preferred_element_type=F32) * scale
            s = jnp.where(valid, s, NEG_INF)
            p = jnp.exp(s - lgw)
            dp = lax.dot_general(dowin[:, sl], v[:, sl], (((1,), (1,)), ((), ())), preferred_element_type=F32)
            ds = p * (dp - dlw)
            dv = lax.dot_general(p.astype(BF16), dowin[:, sl], (((0,), (0,)), ((), ())), preferred_element_type=F32)
            dk = lax.dot_general(ds.astype(BF16), qwin[:, sl], (((0,), (0,)), ((), ())), preferred_element_type=F32)
            dv_ref[:, sl] = dv.astype(dv_ref.dtype)
            dk_ref[:, sl] = (dk * scale).astype(dk_ref.dtype)

    kc, _, _ = _att_specs(L, tk, D_ATT, lambda r: 2 * r + 1)
    vc, _, _ = _att_specs(L, tk, D_ATT, lambda r: 5 * r + 2)
    qc, qp, qn = _att_specs(L, tk, D_ATT, lambda r: 2 * r)
    rc, rp, rn = _att_specs(L, tk, D_ATT, lambda r: r)
    out = pl.BlockSpec((tk, D_ATT), lambda r, i: (i, r))
    dk, dv = pl.pallas_call(
        body, name=name, grid=(d, L // tk),
        in_specs=[kc, vc, qp, qc, qn, qp, qc, qn, rp, rc, rn, rp, rc, rn], out_specs=[out, out],
        out_shape=[_sds((L, d * D_ATT), BF16)] * 2,
        compiler_params=_params("parallel", "parallel"))(
            qk_v, y_v, qk_v, qk_v, qk_v, dac_v, dac_v, dac_v, lg_v, lg_v, lg_v, dl_v, dl_v, dl_v)
    return dk.reshape(S, D_ATT), dv.reshape(S, D_ATT)


def _sigmoid(x):
    return 1.0 / (1.0 + jnp.exp(-x))


def _halo_specs(S, T, width, col):
    last = S // HALO - 1
    per = T // HALO
    centre = pl.BlockSpec((T, width), lambda i: (i, col))
    prev = pl.BlockSpec((HALO, width), lambda i: (jnp.maximum(i * per - 1, 0), col))
    nxt = pl.BlockSpec((HALO, width), lambda i: (jnp.minimum((i + 1) * per, last), col))
    return prev, centre, nxt


def _conv_fwd(y, conv_w32, conv_b, ln_g, ln_b, *, T=512):
    S = y.shape[0]
    T = min(T, S)
    nblk = S // T
    C = D_CONV

    def body(ap, ac, an, gp, gc, gn, w_ref, b_ref, lg_ref, lb_ref, cv_ref, u1_ref, buf):
        i = pl.program_id(0)

        def glu(a_ref, g_ref):
            return a_ref[...].astype(F32) * _sigmoid(g_ref[...].astype(F32))

        buf[0:HALO, :] = jnp.where(i > 0, glu(ap, gp), 0.0)
        buf[HALO:HALO + T, :] = glu(ac, gc)
        buf[HALO + T:, :] = jnp.where(i < nblk - 1, glu(an, gn), 0.0)
        acc = jnp.zeros((T, C), F32)
        for k in range(CONV_WIDTH):
            off = HALO - CONV_PAD + k
            acc = acc + buf[off:off + T, :] * w_ref[k:k + 1, :]
        u1 = acc + b_ref[...]
        u1_ref[...] = u1
        mu = jnp.mean(u1, axis=-1, keepdims=True)
        xc = u1 - mu
        rstd = lax.rsqrt(jnp.mean(xc * xc, axis=-1, keepdims=True) + EPS)
        u2 = xc * rstd * lg_ref[...] + lb_ref[...]
        cv_ref[...] = (u2 * _sigmoid(u2)).astype(cv_ref.dtype)

    ap, ac, an = _halo_specs(S, T, C, 3)
    gp, gc, gn = _halo_specs(S, T, C, 4)
    vec = pl.BlockSpec((1, C), lambda i: (0, 0))
    out = pl.BlockSpec((T, C), lambda i: (i, 0))
    return pl.pallas_call(
        body, name="conv_fwd", grid=(nblk,),
        in_specs=[ap, ac, an, gp, gc, gn, pl.BlockSpec((32, C), lambda i: (0, 0)), vec, vec, vec],
        out_specs=[out, out], out_shape=[_sds((S, C), BF16), _sds((S, C), F32)],
        scratch_shapes=[pltpu.VMEM((T + 2 * HALO, C), F32)],
        compiler_params=_params("parallel"))(y, y, y, y, y, y, conv_w32, conv_b, ln_g, ln_b)


def _conv_bwd(dac, u1, y, conv_w32, ln_g, ln_b, *, T=512):
    S = y.shape[0]
    T = min(T, S)
    nblk = S // T
    C = D_CONV

    def body(dp, dc, dn, up, uc, un, ap, ac, an, gp, gc, gn, w_ref, lg_ref, lb_ref,
             dag_ref, dw_ref, dsm_ref, bufd, bufu):
        i = pl.program_id(0)
        lg = lg_ref[...]

        def du1_of(dcv_ref, u1_ref):
            u1 = u1_ref[...]
            mu = jnp.mean(u1, axis=-1, keepdims=True)
            xc = u1 - mu
            rstd = lax.rsqrt(jnp.mean(xc * xc, axis=-1, keepdims=True) + EPS)
            xhat = xc * rstd
            u2 = xhat * lg + lb_ref[...]
            sg = _sigmoid(u2)
            du2 = dcv_ref[...].astype(F32) * (sg * (1.0 + u2 * (1.0 - sg)))
            dxh = du2 * lg
            du1 = rstd * (dxh - jnp.mean(dxh, axis=-1, keepdims=True)
                          - xhat * jnp.mean(dxh * xhat, axis=-1, keepdims=True))
            return du1, du2, xhat

        def glu(a_ref, g_ref):
            return a_ref[...].astype(F32) * _sigmoid(g_ref[...].astype(F32))

        du1_c, du2_c, xhat_c = du1_of(dc, uc)
        bufd[0:HALO, :] = jnp.where(i > 0, du1_of(dp, up)[0], 0.0)
        bufd[HALO:HALO + T, :] = du1_c
        bufd[HALO + T:, :] = jnp.where(i < nblk - 1, du1_of(dn, un)[0], 0.0)
        bufu[0:HALO, :] = jnp.where(i > 0, glu(ap, gp), 0.0)
        bufu[HALO:HALO + T, :] = glu(ac, gc)
        bufu[HALO + T:, :] = jnp.where(i < nblk - 1, glu(an, gn), 0.0)

        @pl.when(i == 0)
        def _():
            dw_ref[...] = jnp.zeros_like(dw_ref)
            dsm_ref[...] = jnp.zeros_like(dsm_ref)

        du0 = jnp.zeros((T, C), F32)
        for k in range(CONV_WIDTH):
            offd = HALO + CONV_PAD - k
            du0 = du0 + bufd[offd:offd + T, :] * w_ref[k:k + 1, :]
            offu = HALO - CONV_PAD + k
            dw_ref[k:k + 1, :] += jnp.sum(du1_c * bufu[offu:offu + T, :], axis=0, keepdims=True)
        dsm_ref[0:1, :] += jnp.sum(du1_c, axis=0, keepdims=True)
        dsm_ref[1:2, :] += jnp.sum(du2_c * xhat_c, axis=0, keepdims=True)
        dsm_ref[2:3, :] += jnp.sum(du2_c, axis=0, keepdims=True)
        a = ac[...].astype(F32)
        sg = _sigmoid(gc[...].astype(F32))
        dag_ref[:, 0:C] = (du0 * sg).astype(dag_ref.dtype)
        dag_ref[:, C:] = (du0 * a * sg * (1.0 - sg)).astype(dag_ref.dtype)

    dp, dc, dn = _halo_specs(S, T, C, 1)
    up, uc, un = _halo_specs(S, T, C, 0)
    ap, ac, an = _halo_specs(S, T, C, 3)
    gp, gc, gn = _halo_specs(S, T, C, 4)
    vec = pl.BlockSpec((1, C), lambda i: (0, 0))
    return pl.pallas_call(
        body, name="conv_bwd", grid=(nblk,),
        in_specs=[dp, dc, dn, up, uc, un, ap, ac, an, gp, gc, gn,
                  pl.BlockSpec((32, C), lambda i: (0, 0)), vec, vec],
        out_specs=[pl.BlockSpec((T, 2 * C), lambda i: (i, 0)), pl.BlockSpec((32, C), lambda i: (0, 0)),
                   pl.BlockSpec((8, C), lambda i: (0, 0))],
        out_shape=[_sds((S, 2 * C), BF16), _sds((32, C), F32), _sds((8, C), F32)],
        scratch_shapes=[pltpu.VMEM((T + 2 * HALO, C), F32), pltpu.VMEM((T + 2 * HALO, C), F32)],
        compiler_params=_params("arbitrary"))(dac, dac, dac, u1, u1, u1, y, y, y, y, y, y, conv_w32, ln_g, ln_b)


def _xatt_fwd(xq, xk, xv, *, tm=512):
    S = xq.shape[0]
    M = xk.shape[0]
    tm = min(tm, S)
    scale = XATT_HEAD_DIM ** -0.5

    def body(q_ref, k_ref, v_ref, o_ref):
        for h in range(XATT_HEADS):
            sl = slice(h * XATT_HEAD_DIM, (h + 1) * XATT_HEAD_DIM)
            s = lax.dot_general(q_ref[:, sl], k_ref[:, sl], (((1,), (1,)), ((), ())), preferred_element_type=F32) * scale
            e = jnp.exp(s - jnp.max(s, axis=-1, keepdims=True))
            p = e / jnp.sum(e, axis=-1, keepdims=True)
            o_ref[:, sl] = jnp.dot(p.astype(BF16), v_ref[:, sl], preferred_element_type=F32).astype(o_ref.dtype)

    row = pl.BlockSpec((tm, D_MODEL), lambda i: (i, 0))
    full = pl.BlockSpec((M, D_MODEL), lambda i: (0, 0))
    return pl.pallas_call(
        body, name="xatt_fwd", grid=(S // tm,), in_specs=[row, full, full], out_specs=row,
        out_shape=_sds((S, D_MODEL), BF16), compiler_params=_params("parallel"))(xq, xk, xv)


def _xatt_bwd(xq, xk, xv, dxo, *, tm=512):
    S = xq.shape[0]
    M = xk.shape[0]
    tm = min(tm, S)
    scale = XATT_HEAD_DIM ** -0.5

    def body(q_ref, k_ref, v_ref, do_ref, dq_ref, dk_ref, dv_ref):
        i = pl.program_id(0)

        @pl.when(i == 0)
        def _():
            dk_ref[...] = jnp.zeros_like(dk_ref)
            dv_ref[...] = jnp.zeros_like(dv_ref)

        for h in range(XATT_HEADS):
            sl = slice(h * XATT_HEAD_DIM, (h + 1) * XATT_HEAD_DIM)
            q, k, v, do = q_ref[:, sl], k_ref[:, sl], v_ref[:, sl], do_ref[:, sl]
            s = lax.dot_general(q, k, (((1,), (1,)), ((), ())), preferred_element_type=F32) * scale
            e = jnp.exp(s - jnp.max(s, axis=-1, keepdims=True))
            p = e / jnp.sum(e, axis=-1, keepdims=True)
            dp = lax.dot_general(do, v, (((1,), (1,)), ((), ())), preferred_element_type=F32)
            ds = p * (dp - jnp.sum(dp * p, axis=-1, keepdims=True))
            dsb = ds.astype(BF16)
            dq_ref[:, sl] = (jnp.dot(dsb, k, preferred_element_type=F32) * scale).astype(dq_ref.dtype)
            dv_ref[:, sl] += lax.dot_general(p.astype(BF16), do, (((0,), (0,)), ((), ())), preferred_element_type=F32)
            dk_ref[:, sl] += lax.dot_general(dsb, q, (((0,), (0,)), ((), ())), preferred_element_type=F32) * scale

    row = pl.BlockSpec((tm, D_MODEL), lambda i: (i, 0))
    full = pl.BlockSpec((M, D_MODEL), lambda i: (0, 0))
    return pl.pallas_call(
        body, name="xatt_bwd", grid=(S // tm,), in_specs=[row, full, full, row], out_specs=[row, full, full],
        out_shape=[_sds((S, D_MODEL), BF16), _sds((M, D_MODEL), F32), _sds((M, D_MODEL), F32)],
        compiler_params=_params("arbitrary"))(xq, xk, xv, dxo)


def _row_tile(R):
    for t in (256, 128, 64, 32, 16, 8):
        if R % t == 0:
            return t
    return R


def _sum_partials(own, recv, me, *, name):
    _, R, C = own.shape
    t = _row_tile(R)

    def body(me_ref, own_ref, r_ref, o_ref):
        o_ref[...] = ((own_ref[...].astype(F32) + r_ref[0].astype(F32)) + r_ref[1].astype(F32)) + r_ref[2].astype(F32)

    return pl.pallas_call(
        body, name=name,
        grid_spec=pltpu.PrefetchScalarGridSpec(
            num_scalar_prefetch=1, grid=(R // t,),
            in_specs=[pl.BlockSpec((None, t, C), lambda i, me_ref: (me_ref[0], i, 0)),
                      pl.BlockSpec((3, t, C), lambda i, me_ref: (0, i, 0))],
            out_specs=pl.BlockSpec((t, C), lambda i, me_ref: (i, 0))),
        out_shape=_sds((R, C), F32), compiler_params=_params("parallel"))(me, own, recv)


def _adamw_math(w, g, m, v):
    m2 = ADAM_B1 * m + (1.0 - ADAM_B1) * g
    v2 = ADAM_B2 * v + (1.0 - ADAM_B2) * (g * g)
    m_hat = m2 / (1.0 - ADAM_B1 ** ADAM_STEP)
    v_hat = v2 / (1.0 - ADAM_B2 ** ADAM_STEP)
    delta = -ADAM_LR * (m_hat / (jnp.sqrt(v_hat) + ADAM_EPS) + ADAM_WD * w)
    return delta, m2, v2


def _adamw(parts, w, m, v, *, name):
    R, C = w.shape
    t = _row_tile(R)
    n = len(parts)

    def body(*refs):
        w_ref, m_ref, v_ref = refs[n:n + 3]
        g_ref, d_ref, m2_ref, v2_ref = refs[n + 3:]
        g = refs[0][...]
        for r in refs[1:n]:
            g = g + r[...]
        delta, m2, v2 = _adamw_math(w_ref[...], g, m_ref[...], v_ref[...])
        g_ref[...] = g
        d_ref[...] = delta
        m2_ref[...] = m2
        v2_ref[...] = v2

    blk = pl.BlockSpec((t, C), lambda i: (i, 0))
    return pl.pallas_call(
        body, name=name, grid=(R // t,), in_specs=[blk] * (n + 3), out_specs=[blk] * 4,
        out_shape=[_sds((R, C), F32)] * 4, compiler_params=_params("parallel"))(*parts, w, m, v)


def _sum_devices(gathered):
    _, R, C = gathered.shape

    def body(g_ref, o_ref):
        acc = g_ref[0]
        for k in range(1, N_DEV):
            acc = acc + g_ref[k]
        o_ref[...] = acc

    return pl.pallas_call(body, name="sum_devices", out_shape=_sds((R, C), F32))(gathered)


def _chip_peers():
    x, y = lax.axis_index("x"), lax.axis_index("y")
    return [(1 - x, y), (x, 1 - y), (1 - x, 1 - y)]


def _gather_weights(shards):
    n = len(shards)

    def body(*refs):
        ins, outs = refs[:n], refs[n:2 * n]
        send_sems, recv_sems, loc_sems = refs[2 * n:]
        x, y, c = lax.axis_index("x"), lax.axis_index("y"), lax.axis_index("c")
        me = 2 * x + y
        peers = _chip_peers()
        started = []
        for t in range(n):
            loc = pltpu.make_async_copy(ins[t], outs[t].at[me], loc_sems.at[t])
            loc.start()
            started.append(loc)
        sends = []
        for t in range(n):
            for k, (px, py) in enumerate(peers):
                cp = pltpu.make_async_remote_copy(
                    src_ref=ins[t], dst_ref=outs[t].at[me], send_sem=send_sems.at[t, k],
                    recv_sem=recv_sems.at[t, k], device_id=(px, py, c), device_id_type=MESH)
                cp.start()
                sends.append(cp)
        for t in range(n):
            for k, (px, py) in enumerate(peers):
                pltpu.make_async_remote_copy(
                    src_ref=ins[t], dst_ref=outs[t].at[2 * px + py], send_sem=send_sems.at[t, k],
                    recv_sem=recv_sems.at[t, k], device_id=(px, py, c), device_id_type=MESH).wait_recv()
        for cp in sends:
            cp.wait_send()
        for loc in started:
            loc.wait()

    any_spec = pl.BlockSpec(memory_space=pl.ANY)
    return pl.pallas_call(
        body, name="gather_weights", in_specs=[any_spec] * n, out_specs=[any_spec] * n,
        out_shape=[_sds((N_CHIPS,) + s.shape, s.dtype) for s in shards],
        scratch_shapes=[pltpu.SemaphoreType.DMA((n, 3)), pltpu.SemaphoreType.DMA((n, 3)),
                        pltpu.SemaphoreType.DMA((n,))])(*shards)


def _scatter_grads(grads, small):
    n = len(grads)

    def body(*refs):
        ins, small_ref = refs[:n], refs[n]
        outs, gath_ref = refs[n + 1:2 * n + 1], refs[2 * n + 1]
        send_sems, recv_sems, ssend, srecv, loc_sem = refs[2 * n + 2:]
        x, y, c = lax.axis_index("x"), lax.axis_index("y"), lax.axis_index("c")
        me = 4 * x + 2 * y + c
        peers = _chip_peers()
        flips = [(fx, fy, fc) for fx in (0, 1) for fy in (0, 1) for fc in (0, 1)][1:]

        def flipped(fx, fy, fc):
            return (1 - x if fx else x, 1 - y if fy else y, 1 - c if fc else c)

        loc = pltpu.make_async_copy(small_ref, gath_ref.at[me], loc_sem)
        loc.start()
        sends = []
        for j, (fx, fy, fc) in enumerate(flips):
            cp = pltpu.make_async_remote_copy(
                src_ref=small_ref, dst_ref=gath_ref.at[me], send_sem=ssend.at[j], recv_sem=srecv.at[j],
                device_id=flipped(fx, fy, fc), device_id_type=MESH)
            cp.start()
            sends.append(cp)
        for t in range(n):
            for k, (px, py) in enumerate(peers):
                cp = pltpu.make_async_remote_copy(
                    src_ref=ins[t].at[2 * px + py], dst_ref=outs[t].at[k], send_sem=send_sems.at[t, k],
                    recv_sem=recv_sems.at[t, k], device_id=(px, py, c), device_id_type=MESH)
                cp.start()
                sends.append(cp)
        for j, (fx, fy, fc) in enumerate(flips):
            px, py, pc = flipped(fx, fy, fc)
            pltpu.make_async_remote_copy(
                src_ref=small_ref, dst_ref=gath_ref.at[4 * px + 2 * py + pc], send_sem=ssend.at[j],
                recv_sem=srecv.at[j], device_id=(px, py, pc), device_id_type=MESH).wait_recv()
        for t in range(n):
            for k, (px, py) in enumerate(peers):
                pltpu.make_async_remote_copy(
                    src_ref=ins[t].at[2 * px + py], dst_ref=outs[t].at[k], send_sem=send_sems.at[t, k],
                    recv_sem=recv_sems.at[t, k], device_id=(px, py, c), device_id_type=MESH).wait_recv()
        for cp in sends:
            cp.wait_send()
        loc.wait()

    any_spec = pl.BlockSpec(memory_space=pl.ANY)
    out = pl.pallas_call(
        body, name="scatter_grads", in_specs=[any_spec] * (n + 1), out_specs=[any_spec] * (n + 1),
        out_shape=[_sds((3,) + g.shape[1:], g.dtype) for g in grads] + [_sds((N_DEV,) + small.shape, small.dtype)],
        scratch_shapes=[pltpu.SemaphoreType.DMA((n, 3)), pltpu.SemaphoreType.DMA((n, 3)),
                        pltpu.SemaphoreType.DMA((N_DEV - 1,)), pltpu.SemaphoreType.DMA((N_DEV - 1,)),
                        pltpu.SemaphoreType.DMA])(*grads, small)
    return out[:n], out[n]


def _swap_with_sibling(parts):
    n = len(parts)

    def body(*refs):
        ins, outs = refs[:n], refs[n:2 * n]
        send_sems, recv_sems = refs[2 * n:]
        sib = (lax.axis_index("x"), lax.axis_index("y"), 1 - lax.axis_index("c"))
        cps = []
        for t in range(n):
            cp = pltpu.make_async_remote_copy(
                src_ref=ins[t], dst_ref=outs[t], send_sem=send_sems.at[t], recv_sem=recv_sems.at[t],
                device_id=sib, device_id_type=MESH)
            cp.start()
            cps.append(cp)
        for cp in cps:
            cp.wait()

    any_spec = pl.BlockSpec(memory_space=pl.ANY)
    return pl.pallas_call(
        body, name="swap_with_sibling", in_specs=[any_spec] * n, out_specs=[any_spec] * n,
        out_shape=[_sds(p.shape, p.dtype) for p in parts],
        scratch_shapes=[pltpu.SemaphoreType.DMA((n,)), pltpu.SemaphoreType.DMA((n,))])(*parts)


BIG = ("w_in", "w_out", "w_xq", "w_xk", "w_xv", "w_xo", "w_up", "w_down")
COL_SHARDED = ("w_in", "w_up")


def _as_matrix(name, w4):
    if name in COL_SHARDED:
        return w4
    return w4.reshape(1, w4.shape[0] * w4.shape[1], w4.shape[2])


def _local_step(x, mem, target, W, conv_w32, vecs):
    S = x.shape[0]
    Wm = {k: _as_matrix(k, v) for k, v in W.items()}
    tables = _rope_tables(S)

    xn = _rms_fwd(x, vecs["norm_mix_g"], name="rms_mix")
    y = _mm_nn(xn, Wm["w_in"], name="mm_in", tn=640)
    qk = _rope_fwd(y, tables)
    outs, lses = zip(*[_att_fwd(qk, y, d, name=f"att_fwd_d{d}") for d in DILATIONS])
    att, lg = _att_combine(outs, lses)
    cv, u1 = _conv_fwd(y, conv_w32, vecs["conv_b"], vecs["conv_ln_g"], vecs["conv_ln_b"])
    mix = jnp.concatenate([att, cv], axis=1)
    h1 = _mm_nn(mix, Wm["w_out"], name="mm_out", out_dtype=F32, res=x)
    hn = _rms_fwd(h1, vecs["norm_x_g"], name="rms_x")
    xq = _mm_nn(hn, Wm["w_xq"], name="mm_xq")
    mn = _rms_fwd(mem, vecs["norm_mem_g"], name="rms_mem")
    xk = _mm_nn(mn, Wm["w_xk"], name="mm_xk")
    xv = _mm_nn(mn, Wm["w_xv"], name="mm_xv")
    xo = _xatt_fwd(xq, xk, xv)
    h2 = _mm_nn(xo, Wm["w_xo"], name="mm_xo", out_dtype=F32, res=h1)
    hm = _rms_fwd(h2, vecs["norm_mlp_g"], name="rms_mlp")
    relu_up, act = _mm_nn(hm, Wm["w_up"], name="mm_up", relu2=True)
    h3 = _mm_nn(act, Wm["w_down"], name="mm_down", out_dtype=F32, res=h2)

    dh3, dg_final, loss = _loss_head(h3, vecs["norm_final_g"], target)
    dh3b = dh3.astype(BF16)
    g = {}
    g["w_down"] = _mm_tn(act, dh3b, 1, name="dw_down")
    dup = _mm_nt(dh3b, Wm["w_down"], name="d_act", out_dtype=BF16, mul=relu_up)
    g["w_up"] = _mm_tn(hm, dup, N_CHIPS, name="dw_up")
    dhm = _mm_nt(dup, Wm["w_up"], name="d_hm")
    dh2, dg_mlp = _rms_bwd(dhm, h2, vecs["norm_mlp_g"], dh3, name="rms_bwd_mlp")
    dh2b = dh2.astype(BF16)
    g["w_xo"] = _mm_tn(xo, dh2b, 1, name="dw_xo")
    dxo = _mm_nt(dh2b, Wm["w_xo"], name="d_xo", out_dtype=BF16)
    dxq, dxk, dxv = _xatt_bwd(xq, xk, xv, dxo)
    g["w_xq"] = _mm_tn(hn, dxq, 1, name="dw_xq")
    dhn = _mm_nt(dxq, Wm["w_xq"], name="d_hn")
    dh1, dg_x = _rms_bwd(dhn, h1, vecs["norm_x_g"], dh2, name="rms_bwd_x")
    dxkb, dxvb = dxk.astype(BF16), dxv.astype(BF16)
    g["w_xk"] = _mm_tn(mn, dxkb, 1, name="dw_xk")
    g["w_xv"] = _mm_tn(mn, dxvb, 1, name="dw_xv")
    dmn = _mm_nt(jnp.concatenate([dxkb, dxvb], axis=1),
                 jnp.concatenate([Wm["w_xk"], Wm["w_xv"]], axis=2), name="d_mn")
    _, dg_mem = _rms_bwd(dmn, mem, vecs["norm_mem_g"], None, name="rms_bwd_mem")
    dh1b = dh1.astype(BF16)
    g["w_out"] = _mm_tn(mix, dh1b, 1, name="dw_out")
    dac = _mm_nt(dh1b, Wm["w_out"], name="d_mix", out_dtype=BF16)
    delta = _att_delta(dac, att)
    dq = [_att_dq(qk, y, dac, lg, delta, d, name=f"att_dq_d{d}") for d in DILATIONS]
    dk, dv = zip(*[_att_dkv(qk, y, dac, lg, delta, d, name=f"att_dkv_d{d}") for d in DILATIONS])
    dag, dconv_w, dconv_small = _conv_bwd(dac, u1, y, conv_w32, vecs["conv_ln_g"], vecs["conv_ln_b"])
    dy = _assemble_dy(dq, dk, dv, dag, tables)
    g["w_in"] = _mm_tn(xn, dy, N_CHIPS, name="dw_in", tn=640)
    dxn = _mm_nt(dy, Wm["w_in"], name="d_xn", tn=640)
    grad_x, dg_mix = _rms_bwd(dxn, x, vecs["norm_mix_g"], dh1, name="rms_bwd_mix")

    big = {k: v.reshape(N_CHIPS, v.shape[0] * v.shape[1] // N_CHIPS, v.shape[2]) if k not in COL_SHARDED else v
           for k, v in g.items()}
    small = dict(conv_w=dconv_w, conv_small=dconv_small, norm_mix_g=dg_mix, norm_x_g=dg_x, norm_mem_g=dg_mem,
                 norm_mlp_g=dg_mlp, norm_final_g=dg_final, loss=loss)
    return grad_x, big, small


SMALL_ORDER = ("conv_w", "conv_small", "norm_mix_g", "norm_x_g", "norm_mem_g", "norm_mlp_g", "norm_final_g", "loss")


def _pack_small(small):
    rows, offs, pos = [], {}, 0
    for k in SMALL_ORDER:
        a = small[k]
        a = a.reshape(a.shape[0] * a.shape[1] // SMALL_W, SMALL_W)
        pad = (-a.shape[0]) % 8
        if pad:
            a = jnp.pad(a, ((0, pad), (0, 0)))
        rows.append(a)
        offs[k] = pos
        pos += a.shape[0]
    return jnp.concatenate(rows, axis=0), offs


def kernel(x, mem, norm_mix_g, w_in, conv_w, conv_b, conv_ln_g, conv_ln_b, w_out, norm_x_g, norm_mem_g, w_xq, w_xk, w_xv, w_xo, norm_mlp_g, w_up, w_down, norm_final_g, loss_target, m_norm_mix_g, m_w_in, m_conv_w, m_conv_b, m_conv_ln_g, m_conv_ln_b, m_w_out, m_norm_x_g, m_norm_mem_g, m_w_xq, m_w_xk, m_w_xv, m_w_xo, m_norm_mlp_g, m_w_up, m_w_down, m_norm_final_g, v_norm_mix_g, v_w_in, v_conv_w, v_conv_b, v_conv_ln_g, v_conv_ln_b, v_w_out, v_norm_x_g, v_norm_mem_g, v_w_xq, v_w_xk, v_w_xv, v_w_xo, v_norm_mlp_g, v_w_up, v_w_down, v_norm_final_g):
    names = ("norm_mix_g", "w_in", "conv_w", "conv_b", "conv_ln_g", "conv_ln_b", "w_out", "norm_x_g", "norm_mem_g",
             "w_xq", "w_xk", "w_xv", "w_xo", "norm_mlp_g", "w_up", "w_down", "norm_final_g")
    wts = dict(zip(names, (norm_mix_g, w_in, conv_w, conv_b, conv_ln_g, conv_ln_b, w_out, norm_x_g, norm_mem_g,
                           w_xq, w_xk, w_xv, w_xo, norm_mlp_g, w_up, w_down, norm_final_g)))
    mom = dict(zip(names, (m_norm_mix_g, m_w_in, m_conv_w, m_conv_b, m_conv_ln_g, m_conv_ln_b, m_w_out, m_norm_x_g,
                           m_norm_mem_g, m_w_xq, m_w_xk, m_w_xv, m_w_xo, m_norm_mlp_g, m_w_up, m_w_down, m_norm_final_g)))
    var = dict(zip(names, (v_norm_mix_g, v_w_in, v_conv_w, v_conv_b, v_conv_ln_g, v_conv_ln_b, v_w_out, v_norm_x_g,
                           v_norm_mem_g, v_w_xq, v_w_xk, v_w_xv, v_w_xo, v_norm_mlp_g, v_w_up, v_w_down, v_norm_final_g)))
    chip = 2 * lax.axis_index("x") + lax.axis_index("y")

    conv_w_pad = jnp.pad(wts["conv_w"][0], ((0, 1), (0, 0)))
    shards = [wts[k][0].astype(BF16) for k in BIG] + [conv_w_pad]
    gathered = _gather_weights(shards)
    W = dict(zip(BIG, gathered[:len(BIG)]))
    conv_w32 = jnp.transpose(gathered[-1], (1, 0, 2)).reshape(32, D_CONV)

    vecs = {k: wts[k] for k in ("norm_mix_g", "conv_b", "conv_ln_g", "conv_ln_b", "norm_x_g", "norm_mem_g", "norm_mlp_g")}
    vecs["norm_final_g"] = wts["norm_final_g"].reshape(1, D_MODEL)
    grad_x, big, small = _local_step(x[0], mem[0], loss_target[0], W, conv_w32, vecs)

    packed, offs = _pack_small(small)
    recv, gath = _scatter_grads([big[k] for k in BIG], packed)
    me_arr = jnp.reshape(chip, (1,)).astype(jnp.int32)
    sums = [_sum_partials(big[k], r, me_arr, name=f"sum_{k}") for k, r in zip(BIG, recv)]
    sib = _swap_with_sibling(sums)
    tot_small = _sum_devices(gath)

    res = {}
    for k, s_mine, s_sib in zip(BIG, sums, sib):
        res[k] = _adamw([s_mine, s_sib], wts[k][0], mom[k][0], var[k][0], name=f"adamw_{k}")

    def piece(key, nrows):
        return tot_small[offs[key]:offs[key] + nrows]

    def small_update(k, gfull):
        return _adamw([gfull], wts[k].reshape(gfull.shape), mom[k].reshape(gfull.shape),
                      var[k].reshape(gfull.shape), name=f"adamw_{k}")

    dcw = piece("conv_w", 32)[:CONV_WIDTH]
    dcw_mine = lax.dynamic_slice_in_dim(dcw, chip * (D_CONV // N_CHIPS), D_CONV // N_CHIPS, axis=1)
    res["conv_w"] = small_update("conv_w", dcw_mine)
    cs = piece("conv_small", 8)
    res["conv_b"] = small_update("conv_b", cs[0:1])
    res["conv_ln_g"] = small_update("conv_ln_g", cs[1:2])
    res["conv_ln_b"] = small_update("conv_ln_b", cs[2:3])
    for k in ("norm_mix_g", "norm_x_g", "norm_mem_g", "norm_mlp_g", "norm_final_g"):
        res[k] = small_update(k, piece(k, 8 * D_MODEL // SMALL_W).reshape(8, D_MODEL)[0:1])
    loss = piece("loss", 8)[0, 0]

    def full(k, j):
        a = res[k][j]
        return a.reshape(wts[k].shape)

    outs = [loss, grad_x[None]]
    for j in range(4):
        outs += [full(k, j) for k in names]
    return tuple(outs)
```

```python
import jax
import jax.numpy as jnp
from jax import lax
from jax.experimental import pallas as pl
from jax.experimental.pallas import tpu as pltpu

F32 = jnp.float32
BF16 = jnp.bfloat16
MESH = pl.DeviceIdType.MESH

D_MODEL = 1024
ATT_HEADS = 8
HEAD_DIM = 64
D_ATT = ATT_HEADS * HEAD_DIM
D_CONV = D_MODEL - D_ATT
DILATIONS = (1, 4, 16)
HALF = 64
ROPE_THETA = 500000.0
ROT_DIM = HEAD_DIM // 4
CONV_WIDTH = 31
CONV_PAD = (CONV_WIDTH - 1) // 2
XATT_HEADS = 4
XATT_HEAD_DIM = D_MODEL // XATT_HEADS
D_FF = 4 * D_MODEL
D_IN = 3 * D_ATT + 2 * D_CONV
EPS = 1e-6
NEG_INF = -1e30
N_CHIPS = 4
N_DEV = 8

ADAM_LR = 0.001
ADAM_B1 = 0.9
ADAM_B2 = 0.999
ADAM_EPS = 1e-08
ADAM_WD = 0.01
ADAM_STEP = 10

VMEM_LIMIT_V7X = 56 * 1024 * 1024
LANES = 128
HALO = 16
CONV_ROWS = 64
ATT_BLOCK = 128
SMALL_W = 512


def _params(*sem):
    return pltpu.CompilerParams(dimension_semantics=sem, vmem_limit_bytes=VMEM_LIMIT_V7X)


def _sds(shape, dtype):
    return jax.ShapeDtypeStruct(shape, dtype)


def _mm_nn(a, w3, *, name, out_dtype=BF16, res=None, relu2=False, tm=1024, tn=None, tk=1024):
    M, K = a.shape
    nsh, _, n = w3.shape
    tm, tk = min(tm, M), min(tk, K)
    tn = tn or min(n, 1024)
    npt, nk = n // tn, K // tk
    nj, N = nsh * npt, nsh * n
    n_out = 2 if relu2 else 1

    def body(*refs):
        a_ref, w_ref = refs[0], refs[1]
        pos = 2
        res_ref = None
        if res is not None:
            res_ref = refs[pos]
            pos += 1
        outs = refs[pos:pos + n_out]
        acc_ref = refs[pos + n_out] if nk > 1 else None

        def finish(acc):
            if res_ref is not None:
                acc = acc + res_ref[...]
            if relu2:
                r = jnp.maximum(acc, 0.0)
                outs[0][...] = r.astype(outs[0].dtype)
                outs[1][...] = (r * r).astype(outs[1].dtype)
            else:
                outs[0][...] = acc.astype(outs[0].dtype)

        part = jnp.dot(a_ref[...], w_ref[...], preferred_element_type=F32)
        if nk == 1:
            finish(part)
        else:
            k = pl.program_id(2)

            @pl.when(k == 0)
            def _():
                acc_ref[...] = part

            @pl.when(k > 0)
            def _():
                acc_ref[...] += part

            @pl.when(k == nk - 1)
            def _():
                finish(acc_ref[...])

    in_specs = [pl.BlockSpec((tm, tk), lambda i, j, k: (i, k)),
                pl.BlockSpec((None, tk, tn), lambda i, j, k: (j // npt, k, j % npt))]
    args = [a, w3]
    if res is not None:
        in_specs.append(pl.BlockSpec((tm, tn), lambda i, j, k: (i, j)))
        args.append(res)
    out_spec = pl.BlockSpec((tm, tn), lambda i, j, k: (i, j))
    out = pl.pallas_call(
        body, name=name, grid=(M // tm, nj, nk), in_specs=in_specs,
        out_specs=[out_spec] * n_out, out_shape=[_sds((M, N), out_dtype)] * n_out,
        scratch_shapes=[pltpu.VMEM((tm, tn), F32)] if nk > 1 else [],
        compiler_params=_params("parallel", "parallel", "arbitrary"))(*args)
    return tuple(out) if relu2 else out[0]


def _mm_nt(dy, w3, *, name, out_dtype=F32, mul=None, tm=1024, tn=None, tko=1024):
    M, N = dy.shape
    nsh, K, n = w3.shape
    tm, tko = min(tm, M), min(tko, K)
    tn = tn or min(n, 1024)
    npt = n // tn
    nj = nsh * npt

    def body(*refs):
        dy_ref, w_ref = refs[0], refs[1]
        pos = 2
        mul_ref = None
        if mul is not None:
            mul_ref = refs[pos]
            pos += 1
        out_ref = refs[pos]
        acc_ref = refs[pos + 1] if nj > 1 else None

        def finish(acc):
            if mul_ref is not None:
                acc = acc * (2.0 * mul_ref[...].astype(F32))
            out_ref[...] = acc.astype(out_ref.dtype)

        part = lax.dot_general(dy_ref[...], w_ref[...], (((1,), (1,)), ((), ())), preferred_element_type=F32)
        if nj == 1:
            finish(part)
        else:
            j = pl.program_id(2)

            @pl.when(j == 0)
            def _():
                acc_ref[...] = part

            @pl.when(j > 0)
            def _():
                acc_ref[...] += part

            @pl.when(j == nj - 1)
            def _():
                finish(acc_ref[...])

    in_specs = [pl.BlockSpec((tm, tn), lambda i, ko, j: (i, j)),
                pl.BlockSpec((None, tko, tn), lambda i, ko, j: (j // npt, ko, j % npt))]
    args = [dy, w3]
    if mul is not None:
        in_specs.append(pl.BlockSpec((tm, tko), lambda i, ko, j: (i, ko)))
        args.append(mul)
    return pl.pallas_call(
        body, name=name, grid=(M // tm, K // tko, nj), in_specs=in_specs,
        out_specs=pl.BlockSpec((tm, tko), lambda i, ko, j: (i, ko)), out_shape=_sds((M, K), out_dtype),
        scratch_shapes=[pltpu.VMEM((tm, tko), F32)] if nj > 1 else [],
        compiler_params=_params("parallel", "parallel", "arbitrary"))(*args)


def _mm_tn(a, dy, nsh, *, name, out_dtype=BF16, tm=1024, tk=1024, tn=None):
    M, K = a.shape
    N = dy.shape[1]
    n = N // nsh
    tm, tk = min(tm, M), min(tk, K)
    tn = tn or min(n, 1024)
    npt = n // tn
    nj, nm = nsh * npt, M // tm

    def body(a_ref, dy_ref, out_ref, acc_ref):
        m = pl.program_id(2)
        part = lax.dot_general(a_ref[...], dy_ref[...], (((0,), (0,)), ((), ())), preferred_element_type=F32)

        @pl.when(m == 0)
        def _():
            acc_ref[...] = part

        @pl.when(m > 0)
        def _():
            acc_ref[...] += part

        @pl.when(m == nm - 1)
        def _():
            out_ref[...] = acc_ref[...].astype(out_ref.dtype)

    return pl.pallas_call(
        body, name=name, grid=(K // tk, nj, nm),
        in_specs=[pl.BlockSpec((tm, tk), lambda kk, j, m: (m, kk)),
                  pl.BlockSpec((tm, tn), lambda kk, j, m: (m, j))],
        out_specs=pl.BlockSpec((None, tk, tn), lambda kk, j, m: (j // npt, kk, j % npt)),
        out_shape=_sds((nsh, K, n), out_dtype),
        scratch_shapes=[pltpu.VMEM((tk, tn), F32)],
        compiler_params=_params("parallel", "parallel", "arbitrary"))(a, dy)


def _rms_fwd(x, g, *, name, tm=512):
    M, Dm = x.shape
    tm = min(tm, M)

    def body(x_ref, g_ref, o_ref):
        xf = x_ref[...]
        r = lax.rsqrt(jnp.mean(xf * xf, axis=-1, keepdims=True) + EPS)
        o_ref[...] = (xf * r * g_ref[...]).astype(o_ref.dtype)

    return pl.pallas_call(
        body, name=name, grid=(M // tm,),
        in_specs=[pl.BlockSpec((tm, Dm), lambda i: (i, 0)), pl.BlockSpec((1, Dm), lambda i: (0, 0))],
        out_specs=pl.BlockSpec((tm, Dm), lambda i: (i, 0)), out_shape=_sds((M, Dm), BF16),
        compiler_params=_params("parallel"))(x, g)


def _rms_bwd(dxn, x, g, dres, *, name, tm=512):
    M, Dm = x.shape
    tm = min(tm, M)
    has_res = dres is not None

    def body(*refs):
        dxn_ref, x_ref, g_ref = refs[:3]
        dres_ref = refs[3] if has_res else None
        dx_ref, dxb_ref, dg_ref = refs[-3:]
        i = pl.program_id(0)
        xf = x_ref[...]
        r = lax.rsqrt(jnp.mean(xf * xf, axis=-1, keepdims=True) + EPS)
        nrm = xf * r
        dxn_f = dxn_ref[...].astype(F32)
        dn = dxn_f * g_ref[...]
        dx = r * (dn - nrm * jnp.mean(dn * nrm, axis=-1, keepdims=True))
        if has_res:
            dx = dx + dres_ref[...]
        dx_ref[...] = dx
        dxb_ref[...] = dx.astype(dxb_ref.dtype)

        @pl.when(i == 0)
        def _():
            dg_ref[...] = jnp.zeros_like(dg_ref)

        dg_ref[0:1, :] += jnp.sum(dxn_f * nrm, axis=0, keepdims=True)

    row = pl.BlockSpec((tm, Dm), lambda i: (i, 0))
    in_specs = [row, row, pl.BlockSpec((1, Dm), lambda i: (0, 0))] + ([row] if has_res else [])
    args = [dxn, x, g] + ([dres] if has_res else [])
    return pl.pallas_call(
        body, name=name, grid=(M // tm,), in_specs=in_specs,
        out_specs=[row, row, pl.BlockSpec((8, Dm), lambda i: (0, 0))],
        out_shape=[_sds((M, Dm), F32), _sds((M, Dm), BF16), _sds((8, Dm), F32)],
        compiler_params=_params("arbitrary"))(*args)


def _loss_head(h, g, target, *, tm=512):
    M, Dm = h.shape
    tm = min(tm, M)

    def body(h_ref, g_ref, t_ref, dh_ref, dhb_ref, dg_ref, loss_ref):
        i = pl.program_id(0)
        hf = h_ref[...]
        r = lax.rsqrt(jnp.mean(hf * hf, axis=-1, keepdims=True) + EPS)
        nrm = hf * r
        gv = g_ref[...]
        err = nrm * gv - t_ref[...]
        dy = err * (1.0 / Dm)
        dn = dy * gv
        dh = r * (dn - nrm * jnp.mean(dn * nrm, axis=-1, keepdims=True))
        dh_ref[...] = dh
        dhb_ref[...] = dh.astype(dhb_ref.dtype)

        @pl.when(i == 0)
        def _():
            dg_ref[...] = jnp.zeros_like(dg_ref)
            loss_ref[...] = jnp.zeros_like(loss_ref)

        dg_ref[0:1, :] += jnp.sum(dy * nrm, axis=0, keepdims=True)
        part = 0.5 * jnp.sum(jnp.mean(err * err, axis=-1, keepdims=True), axis=0, keepdims=True)
        sel = (lax.broadcasted_iota(jnp.int32, (8, 128), 0) == 0) & (lax.broadcasted_iota(jnp.int32, (8, 128), 1) == 0)
        loss_ref[...] += jnp.where(sel, part, 0.0)

    row = pl.BlockSpec((tm, Dm), lambda i: (i, 0))
    return pl.pallas_call(
        body, name="loss_head", grid=(M // tm,),
        in_specs=[row, pl.BlockSpec((1, Dm), lambda i: (0, 0)), row],
        out_specs=[row, row, pl.BlockSpec((8, Dm), lambda i: (0, 0)), pl.BlockSpec((8, 128), lambda i: (0, 0))],
        out_shape=[_sds((M, Dm), F32), _sds((M, Dm), BF16), _sds((8, Dm), F32), _sds((8, 128), F32)],
        compiler_params=_params("arbitrary"))(h, g, target)


def _class_spec(tm, d, width):
    return pl.BlockSpec((d, tm // d, width), lambda i: (0, i, 0))


def _row_scratch(tm, width):
    return pltpu.VMEM((width // LANES, tm, LANES), F32)


def _fill(scr, val):
    for c in range(scr.shape[0]):
        scr[c] = val[:, c * LANES:(c + 1) * LANES]


def _to_classes(scr, out_ref, d):
    n = scr.shape[1] // d
    for r in range(d):
        for c in range(scr.shape[0]):
            out_ref[r, :, c * LANES:(c + 1) * LANES] = scr[c, pl.ds(r, n, stride=d), :].astype(out_ref.dtype)


def _from_classes(in_ref, scr, d):
    n = scr.shape[1] // d
    for r in range(d):
        blk = in_ref[r].astype(F32)
        for c in range(scr.shape[0]):
            scr[c, pl.ds(r, n, stride=d), :] = blk[:, c * LANES:(c + 1) * LANES]
    return jnp.concatenate([scr[c] for c in range(scr.shape[0])], axis=1)


def _rope_tables(S):
    half = ROT_DIM // 2
    freqs = ROPE_THETA ** (-jnp.arange(0, ROT_DIM, 2, dtype=F32) / ROT_DIM)
    ang = jnp.arange(S, dtype=F32)[:, None] * freqs[None, :]
    cos, sin = jnp.cos(ang), jnp.sin(ang)
    ones = jnp.ones((S, HEAD_DIM - ROT_DIM), F32)
    zeros = jnp.zeros((S, HEAD_DIM - ROT_DIM), F32)
    zh = jnp.zeros((S, half), F32)
    c = jnp.concatenate([cos, cos, ones], axis=1)
    sa = jnp.concatenate([-sin, zh, zeros], axis=1)
    sb = jnp.concatenate([zh, sin, zeros], axis=1)
    return tuple(jnp.tile(t, (1, LANES // HEAD_DIM)) for t in (c, sa, sb))


def _rope_fwd(y, tables, *, tm=512):
    S = y.shape[0]
    W = 2 * D_ATT
    tm = min(tm, S)
    half = ROT_DIM // 2
    dils = [d for d in DILATIONS if d > 1]

    def body(y_ref, c_ref, sa_ref, sb_ref, qk_ref, *rest):
        qk_outs, v_outs = rest[:len(dils)], rest[len(dils):2 * len(dils)]
        scr_qk, scr_v = rest[2 * len(dils):]
        t = y_ref[:, 0:W].astype(F32)
        rep = W // LANES
        c, sa, sb = (jnp.tile(r[...], (1, rep)) for r in (c_ref, sa_ref, sb_ref))
        rot = t * c + pltpu.roll(t, W - half, axis=1) * sa + pltpu.roll(t, half, axis=1) * sb
        qk_ref[...] = rot.astype(qk_ref.dtype)
        _fill(scr_qk, rot)
        _fill(scr_v, y_ref[:, W:W + D_ATT].astype(F32))
        for d, qo, vo in zip(dils, qk_outs, v_outs):
            _to_classes(scr_qk, qo, d)
            _to_classes(scr_v, vo, d)

    tab = pl.BlockSpec((tm, LANES), lambda i: (i, 0))
    out = pl.pallas_call(
        body, name="rope_fwd", grid=(S // tm,),
        in_specs=[pl.BlockSpec((tm, 3 * D_ATT), lambda i: (i, 0)), tab, tab, tab],
        out_specs=[pl.BlockSpec((tm, W), lambda i: (i, 0))] + [_class_spec(tm, d, W) for d in dils]
        + [_class_spec(tm, d, D_ATT) for d in dils],
        out_shape=[_sds((S, W), BF16)] + [_sds((d, S // d, W), BF16) for d in dils]
        + [_sds((d, S // d, D_ATT), BF16) for d in dils],
        scratch_shapes=[_row_scratch(tm, W), _row_scratch(tm, D_ATT)],
        compiler_params=_params("parallel"))(y, *tables)
    qk = [out[0]] + [o.reshape(S, W) for o in out[1:1 + len(dils)]]
    v = [None] + [o.reshape(S, D_ATT) for o in out[1 + len(dils):]]
    return qk, v


def _assemble_dy(dq, dk, dv, dag, tables, *, tm=512):
    S = dag.shape[0]
    tm = min(tm, S)
    half = ROT_DIM // 2
    W = D_ATT
    n_pat = len(DILATIONS)

    def body(*refs):
        groups = [refs[g * n_pat:(g + 1) * n_pat] for g in range(3)]
        dag_ref, c_ref, sa_ref, sb_ref, o_ref, scr = refs[3 * n_pat:]
        rep = W // LANES
        c, sa, sb = (jnp.tile(r[...], (1, rep)) for r in (c_ref, sa_ref, sb_ref))

        def total(rs):
            acc = rs[0][...].astype(F32)
            for d, r in zip(DILATIONS[1:], rs[1:]):
                acc = acc + _from_classes(r, scr, d)
            return acc

        def unrope(dr):
            return dr * c + pltpu.roll(dr * sa, half, axis=1) + pltpu.roll(dr * sb, W - half, axis=1)

        o_ref[:, 0:W] = unrope(total(groups[0])).astype(o_ref.dtype)
        o_ref[:, W:2 * W] = unrope(total(groups[1])).astype(o_ref.dtype)
        o_ref[:, 2 * W:3 * W] = total(groups[2]).astype(o_ref.dtype)
        o_ref[:, 3 * W:] = dag_ref[...]

    specs = [pl.BlockSpec((tm, W), lambda i: (i, 0))] + [_class_spec(tm, d, W) for d in DILATIONS[1:]]
    tab = pl.BlockSpec((tm, LANES), lambda i: (i, 0))
    args = [a if d == 1 else a.reshape(d, S // d, W) for grp in (dq, dk, dv) for d, a in zip(DILATIONS, grp)]
    return pl.pallas_call(
        body, name="assemble_dy", grid=(S // tm,),
        in_specs=specs * 3 + [pl.BlockSpec((tm, 2 * D_CONV), lambda i: (i, 0)), tab, tab, tab],
        out_specs=pl.BlockSpec((tm, D_IN), lambda i: (i, 0)), out_shape=_sds((S, D_IN), BF16),
        scratch_shapes=[_row_scratch(tm, W)],
        compiler_params=_params("parallel"))(*args, dag, *tables)


def _seq_specs(L, tb, col):
    nb, per, nh = L // tb, tb // HALF, L // HALF
    centre = pl.BlockSpec((tb, D_ATT), lambda r, i: (r * nb + i, col))
    prev = pl.BlockSpec((HALF, D_ATT), lambda r, i: (r * nh + jnp.maximum(i * per - 1, 0), col))
    nxt = pl.BlockSpec((HALF, D_ATT), lambda r, i: (r * nh + jnp.minimum((i + 1) * per, nh - 1), col))
    return prev, centre, nxt


def _band_mask(i, tq, L, centre_is_query):
    if centre_is_query:
        shape = (tq, tq + 2 * HALF)
        c_idx = lax.broadcasted_iota(jnp.int32, shape, 0)
        w_idx = lax.broadcasted_iota(jnp.int32, shape, 1)
    else:
        shape = (tq + 2 * HALF, tq)
        w_idx = lax.broadcasted_iota(jnp.int32, shape, 0)
        c_idx = lax.broadcasted_iota(jnp.int32, shape, 1)
    diff = w_idx - c_idx
    wpos = i * tq - HALF + w_idx
    return (diff >= 0) & (diff <= 2 * HALF) & (wpos >= 0) & (wpos < L)


def _head_pairs():
    for h in range(0, ATT_HEADS, 2):
        yield [(hh, slice(hh * HEAD_DIM, (hh + 1) * HEAD_DIM)) for hh in (h, h + 1)]


def _nt(a, b):
    return lax.dot_general(a, b, (((1,), (1,)), ((), ())), preferred_element_type=F32)


def _tn(a, b):
    return lax.dot_general(a, b, (((0,), (0,)), ((), ())), preferred_element_type=F32)


ATT_SCALE = HEAD_DIM ** -0.5


def _att_fwd(qk, v_src, d, *, name):
    S = qk.shape[0]
    L = S // d
    tq = min(ATT_BLOCK, L)
    v_arr, v_col = v_src

    def body(q_ref, kp_ref, kc_ref, kn_ref, vp_ref, vc_ref, vn_ref, o_ref, lse_ref):
        i = pl.program_id(1)
        valid = _band_mask(i, tq, L, True)
        q = q_ref[...] * ATT_SCALE
        kwin = jnp.concatenate([kp_ref[...], kc_ref[...], kn_ref[...]], axis=0)
        vwin = jnp.concatenate([vp_ref[...], vc_ref[...], vn_ref[...]], axis=0)
        for pair in _head_pairs():
            s = [jnp.where(valid, _nt(q[:, sl], kwin[:, sl]), NEG_INF) for _, sl in pair]
            m = [jnp.max(t, axis=-1, keepdims=True) for t in s]
            p = [jnp.exp(t - mm) for t, mm in zip(s, m)]
            den = [jnp.sum(t, axis=-1, keepdims=True) for t in p]
            o = [jnp.dot(t.astype(BF16), vwin[:, sl], preferred_element_type=F32) for t, (_, sl) in zip(p, pair)]
            for (_, sl), oo, mm, dd in zip(pair, o, m, den):
                o_ref[:, sl] = oo * (1.0 / dd)
                lse_ref[:, sl] = jnp.broadcast_to(mm + jnp.log(dd), (tq, HEAD_DIM))

    _, qc, _ = _seq_specs(L, tq, 0)
    kp, kc, kn = _seq_specs(L, tq, 1)
    vp, vc, vn = _seq_specs(L, tq, v_col)
    out = pl.BlockSpec((tq, D_ATT), lambda r, i: (r * (L // tq) + i, 0))
    return pl.pallas_call(
        body, name=name, grid=(d, L // tq),
        in_specs=[qc, kp, kc, kn, vp, vc, vn], out_specs=[out, out],
        out_shape=[_sds((S, D_ATT), F32)] * 2,
        compiler_params=_params("parallel", "parallel"))(qk, qk, qk, qk, v_arr, v_arr, v_arr)


def _att_combine(outs, lses, *, tm=512):
    S = outs[0].shape[0]
    tm = min(tm, S)
    dils = DILATIONS[1:]
    n_d = len(dils)

    def body(*refs):
        o_refs, l_refs = refs[0:1 + n_d], refs[1 + n_d:2 + 2 * n_d]
        att_ref, lg_ref = refs[2 + 2 * n_d:4 + 2 * n_d]
        lg_outs = refs[4 + 2 * n_d:4 + 3 * n_d]
        scr = refs[4 + 3 * n_d:]
        scr_o, scr_l, scr_lg = scr[:n_d], scr[n_d:2 * n_d], scr[2 * n_d]
        ls = [l_refs[0][...]] + [_from_classes(r, s, d) for r, s, d in zip(l_refs[1:], scr_l, dils)]
        os_ = [o_refs[0][...]] + [_from_classes(r, s, d) for r, s, d in zip(o_refs[1:], scr_o, dils)]
        mx = ls[0]
        for l in ls[1:]:
            mx = jnp.maximum(mx, l)
        es = [jnp.exp(l - mx) for l in ls]
        tot = es[0]
        num = es[0] * os_[0]
        for e, o in zip(es[1:], os_[1:]):
            tot = tot + e
            num = num + e * o
        att_ref[...] = (num / tot).astype(att_ref.dtype)
        lg = mx + jnp.log(tot)
        lg_ref[...] = lg
        _fill(scr_lg, lg)
        for d, out in zip(dils, lg_outs):
            _to_classes(scr_lg, out, d)

    nat = pl.BlockSpec((tm, D_ATT), lambda i: (i, 0))
    specs = [nat] + [_class_spec(tm, d, D_ATT) for d in dils]
    view = lambda arrs: [arrs[0]] + [a.reshape(d, S // d, D_ATT) for a, d in zip(arrs[1:], dils)]
    out = pl.pallas_call(
        body, name="att_combine", grid=(S // tm,), in_specs=specs * 2,
        out_specs=[nat, nat] + specs[1:],
        out_shape=[_sds((S, D_ATT), BF16), _sds((S, D_ATT), F32)] + [_sds((d, S // d, D_ATT), F32) for d in dils],
        scratch_shapes=[_row_scratch(tm, D_ATT)] * (2 * n_d + 1),
        compiler_params=_params("parallel"))(*view(list(outs)), *view(list(lses)))
    return out[0], [out[1]] + [o.reshape(S, D_ATT) for o in out[2:]]


def _att_delta(dac, att, *, tm=512):
    S = att.shape[0]
    tm = min(tm, S)
    dils = DILATIONS[1:]
    n_d = len(dils)

    def body(do_ref, o_ref, dl_ref, *rest):
        dl_outs, do_outs = rest[:n_d], rest[n_d:2 * n_d]
        scr_dl, scr_do = rest[2 * n_d:]
        do = do_ref[...].astype(F32)
        prod = do * o_ref[...].astype(F32)
        per_head = [jnp.broadcast_to(jnp.sum(prod[:, h * HEAD_DIM:(h + 1) * HEAD_DIM], axis=-1, keepdims=True),
                                     (tm, HEAD_DIM)) for h in range(ATT_HEADS)]
        dl = jnp.concatenate(per_head, axis=1)
        dl_ref[...] = dl
        _fill(scr_dl, dl)
        _fill(scr_do, do)
        for d, dlo, doo in zip(dils, dl_outs, do_outs):
            _to_classes(scr_dl, dlo, d)
            _to_classes(scr_do, doo, d)

    blk = pl.BlockSpec((tm, D_ATT), lambda i: (i, 0))
    out = pl.pallas_call(
        body, name="att_delta", grid=(S // tm,), in_specs=[blk, blk],
        out_specs=[blk] + [_class_spec(tm, d, D_ATT) for d in dils] * 2,
        out_shape=[_sds((S, D_ATT), F32)] + [_sds((d, S // d, D_ATT), F32) for d in dils]
        + [_sds((d, S // d, D_ATT), BF16) for d in dils],
        scratch_shapes=[_row_scratch(tm, D_ATT), _row_scratch(tm, D_ATT)],
        compiler_params=_params("parallel"))(dac, att)
    delta = [out[0]] + [o.reshape(S, D_ATT) for o in out[1:1 + n_d]]
    do = [None] + [o.reshape(S, D_ATT) for o in out[1 + n_d:]]
    return delta, do


def _att_dq(qk, v_src, do_src, lg, delta, d, *, name):
    S = qk.shape[0]
    L = S // d
    tq = min(ATT_BLOCK, L)
    (v_arr, v_col), (do_arr, do_col) = v_src, do_src

    def body(q_ref, kp_ref, kc_ref, kn_ref, vp_ref, vc_ref, vn_ref, do_ref, lg_ref, dl_ref, dq_ref):
        i = pl.program_id(1)
        valid = _band_mask(i, tq, L, True)
        q, do = q_ref[...] * ATT_SCALE, do_ref[...]
        kwin = jnp.concatenate([kp_ref[...], kc_ref[...], kn_ref[...]], axis=0)
        vwin = jnp.concatenate([vp_ref[...], vc_ref[...], vn_ref[...]], axis=0)
        for pair in _head_pairs():
            s = [jnp.where(valid, _nt(q[:, sl], kwin[:, sl]), NEG_INF) for _, sl in pair]
            p = [jnp.exp(t - lg_ref[:, sl.start:sl.start + 1]) for t, (_, sl) in zip(s, pair)]
            dp = [_nt(do[:, sl], vwin[:, sl]) for _, sl in pair]
            ds = [pp * (t - dl_ref[:, sl.start:sl.start + 1]) for pp, t, (_, sl) in zip(p, dp, pair)]
            dq = [jnp.dot(t.astype(BF16), kwin[:, sl], preferred_element_type=F32) for t, (_, sl) in zip(ds, pair)]
            for (_, sl), t in zip(pair, dq):
                dq_ref[:, sl] = (t * ATT_SCALE).astype(dq_ref.dtype)

    _, qc, _ = _seq_specs(L, tq, 0)
    kp, kc, kn = _seq_specs(L, tq, 1)
    vp, vc, vn = _seq_specs(L, tq, v_col)
    _, doc, _ = _seq_specs(L, tq, do_col)
    row = pl.BlockSpec((tq, D_ATT), lambda r, i: (r * (L // tq) + i, 0))
    return pl.pallas_call(
        body, name=name, grid=(d, L // tq),
        in_specs=[qc, kp, kc, kn, vp, vc, vn, doc, row, row], out_specs=row,
        out_shape=_sds((S, D_ATT), BF16),
        compiler_params=_params("parallel", "parallel"))(qk, qk, qk, qk, v_arr, v_arr, v_arr, do_arr, lg, delta)


def _att_dkv(qk, v_src, do_src, lg, delta, d, *, name):
    S = qk.shape[0]
    L = S // d
    tk = min(ATT_BLOCK, L)
    (v_arr, v_col), (do_arr, do_col) = v_src, do_src

    def body(k_ref, v_ref, qp_ref, qc_ref, qn_ref, dop_ref, doc_ref, don_ref,
             lgp_ref, lgc_ref, lgn_ref, dlp_ref, dlc_ref, dln_ref, dk_ref, dv_ref):
        i = pl.program_id(1)
        valid = _band_mask(i, tk, L, False)
        k, v = k_ref[...], v_ref[...]
        qwin = jnp.concatenate([qp_ref[...], qc_ref[...], qn_ref[...]], axis=0) * ATT_SCALE
        dowin = jnp.concatenate([dop_ref[...], doc_ref[...], don_ref[...]], axis=0)

        def column(refs, c0):
            return jnp.concatenate([r[:, c0:c0 + 1] for r in refs], axis=0)

        for pair in _head_pairs():
            lgw = [column((lgp_ref, lgc_ref, lgn_ref), sl.start) for _, sl in pair]
            dlw = [column((dlp_ref, dlc_ref, dln_ref), sl.start) for _, sl in pair]
            s = [jnp.where(valid, _nt(qwin[:, sl], k[:, sl]), NEG_INF) for _, sl in pair]
            p = [jnp.exp(t - l) for t, l in zip(s, lgw)]
            dp = [_nt(dowin[:, sl], v[:, sl]) for _, sl in pair]
            ds = [pp * (t - l) for pp, t, l in zip(p, dp, dlw)]
            dv = [_tn(pp.astype(BF16), dowin[:, sl]) for pp, (_, sl) in zip(p, pair)]
            dk = [_tn(t.astype(BF16), qwin[:, sl]) for t, (_, sl) in zip(ds, pair)]
            for (_, sl), a, b in zip(pair, dk, dv):
                dk_ref[:, sl] = a.astype(dk_ref.dtype)
                dv_ref[:, sl] = b.astype(dv_ref.dtype)

    _, kc, _ = _seq_specs(L, tk, 1)
    _, vc, _ = _seq_specs(L, tk, v_col)
    qp, qc, qn = _seq_specs(L, tk, 0)
    dop, doc, don = _seq_specs(L, tk, do_col)
    rp, rc, rn = _seq_specs(L, tk, 0)
    out = pl.BlockSpec((tk, D_ATT), lambda r, i: (r * (L // tk) + i, 0))
    return pl.pallas_call(
        body, name=name, grid=(d, L // tk),
        in_specs=[kc, vc, qp, qc, qn, dop, doc, don, rp, rc, rn, rp, rc, rn], out_specs=[out, out],
        out_shape=[_sds((S, D_ATT), BF16)] * 2,
        compiler_params=_params("parallel", "parallel"))(
            qk, v_arr, qk, qk, qk, do_arr, do_arr, do_arr, lg, lg, lg, delta, delta, delta)


def _sigmoid(x):
    return 1.0 / (1.0 + jnp.exp(-x))


def _halo_specs(S, T, width, col):
    last = S // HALO - 1
    per = T // HALO
    centre = pl.BlockSpec((T, width), lambda i: (i, col))
    prev = pl.BlockSpec((HALO, width), lambda i: (jnp.maximum(i * per - 1, 0), col))
    nxt = pl.BlockSpec((HALO, width), lambda i: (jnp.minimum((i + 1) * per, last), col))
    return prev, centre, nxt


def _conv_tiles(T, C):
    for r0 in range(0, T, CONV_ROWS):
        for c0 in range(0, C, LANES):
            yield r0, slice(c0, c0 + LANES)


def _depthwise(buf, w_ref, out_ref, T, C, first_off, step):
    for r0, ls in _conv_tiles(T, C):
        acc = jnp.zeros((CONV_ROWS, LANES), F32)
        for k in range(CONV_WIDTH):
            off = first_off + step * k + r0
            acc = acc + buf[off:off + CONV_ROWS, ls] * w_ref[k:k + 1, ls]
        out_ref[r0:r0 + CONV_ROWS, ls] = acc


def _conv_fwd(y, conv_w32, conv_b, ln_g, ln_b, *, T=512):
    S = y.shape[0]
    T = min(T, S)
    nblk = S // T
    C = D_CONV

    def body(ap, ac, an, gp, gc, gn, w_ref, b_ref, lg_ref, lb_ref, cv_ref, u1_ref, buf):
        i = pl.program_id(0)

        def glu(a_ref, g_ref):
            return a_ref[...].astype(F32) * _sigmoid(g_ref[...].astype(F32))

        buf[0:HALO, :] = jnp.where(i > 0, glu(ap, gp), 0.0)
        buf[HALO:HALO + T, :] = glu(ac, gc)
        buf[HALO + T:, :] = jnp.where(i < nblk - 1, glu(an, gn), 0.0)
        _depthwise(buf, w_ref, u1_ref, T, C, HALO - CONV_PAD, 1)
        u1 = u1_ref[...] + b_ref[...]
        u1_ref[...] = u1
        mu = jnp.mean(u1, axis=-1, keepdims=True)
        xc = u1 - mu
        rstd = lax.rsqrt(jnp.mean(xc * xc, axis=-1, keepdims=True) + EPS)
        u2 = xc * rstd * lg_ref[...] + lb_ref[...]
        cv_ref[...] = (u2 * _sigmoid(u2)).astype(cv_ref.dtype)

    ap, ac, an = _halo_specs(S, T, C, 3)
    gp, gc, gn = _halo_specs(S, T, C, 4)
    vec = pl.BlockSpec((1, C), lambda i: (0, 0))
    out = pl.BlockSpec((T, C), lambda i: (i, 0))
    return pl.pallas_call(
        body, name="conv_fwd", grid=(nblk,),
        in_specs=[ap, ac, an, gp, gc, gn, pl.BlockSpec((32, C), lambda i: (0, 0)), vec, vec, vec],
        out_specs=[out, out], out_shape=[_sds((S, C), BF16), _sds((S, C), F32)],
        scratch_shapes=[pltpu.VMEM((T + 2 * HALO, C), F32)],
        compiler_params=_params("parallel"))(y, y, y, y, y, y, conv_w32, conv_b, ln_g, ln_b)


def _conv_bwd(dac, u1, y, conv_w32, ln_g, ln_b, *, T=512):
    S = y.shape[0]
    T = min(T, S)
    nblk = S // T
    C = D_CONV

    def body(dp, dc, dn, up, uc, un, ap, ac, an, gp, gc, gn, w_ref, lg_ref, lb_ref,
             dag_ref, dw_ref, dsm_ref, bufd, bufu, du0_scr):
        i = pl.program_id(0)
        lg = lg_ref[...]

        def du1_of(dcv_ref, u1_ref):
            u1 = u1_ref[...]
            mu = jnp.mean(u1, axis=-1, keepdims=True)
            xc = u1 - mu
            rstd = lax.rsqrt(jnp.mean(xc * xc, axis=-1, keepdims=True) + EPS)
            xhat = xc * rstd
            u2 = xhat * lg + lb_ref[...]
            sg = _sigmoid(u2)
            du2 = dcv_ref[...].astype(F32) * (sg * (1.0 + u2 * (1.0 - sg)))
            dxh = du2 * lg
            du1 = rstd * (dxh - jnp.mean(dxh, axis=-1, keepdims=True)
                          - xhat * jnp.mean(dxh * xhat, axis=-1, keepdims=True))
            return du1, du2, xhat

        def glu(a_ref, g_ref):
            return a_ref[...].astype(F32) * _sigmoid(g_ref[...].astype(F32))

        @pl.when(i == 0)
        def _():
            dw_ref[...] = jnp.zeros_like(dw_ref)
            dsm_ref[...] = jnp.zeros_like(dsm_ref)

        du1_c, du2_c, xhat_c = du1_of(dc, uc)
        dsm_ref[0:1, :] += jnp.sum(du1_c, axis=0, keepdims=True)
        dsm_ref[1:2, :] += jnp.sum(du2_c * xhat_c, axis=0, keepdims=True)
        dsm_ref[2:3, :] += jnp.sum(du2_c, axis=0, keepdims=True)
        bufd[0:HALO, :] = jnp.where(i > 0, du1_of(dp, up)[0], 0.0)
        bufd[HALO:HALO + T, :] = du1_c
        bufd[HALO + T:, :] = jnp.where(i < nblk - 1, du1_of(dn, un)[0], 0.0)
        bufu[0:HALO, :] = jnp.where(i > 0, glu(ap, gp), 0.0)
        bufu[HALO:HALO + T, :] = glu(ac, gc)
        bufu[HALO + T:, :] = jnp.where(i < nblk - 1, glu(an, gn), 0.0)

        _depthwise(bufd, w_ref, du0_scr, T, C, HALO + CONV_PAD, -1)
        for c0 in range(0, C, LANES):
            ls = slice(c0, c0 + LANES)
            for k in range(CONV_WIDTH):
                offu = HALO - CONV_PAD + k
                acc = jnp.zeros((8, LANES), F32)
                for r0 in range(0, T, CONV_ROWS):
                    prod = bufd[HALO + r0:HALO + r0 + CONV_ROWS, ls] * bufu[offu + r0:offu + r0 + CONV_ROWS, ls]
                    for j in range(0, CONV_ROWS, 8):
                        acc = acc + prod[j:j + 8]
                dw_ref[k:k + 1, ls] += jnp.sum(acc, axis=0, keepdims=True)
        du0 = du0_scr[...]
        a = ac[...].astype(F32)
        sg = _sigmoid(gc[...].astype(F32))
        dag_ref[:, 0:C] = (du0 * sg).astype(dag_ref.dtype)
        dag_ref[:, C:] = (du0 * a * sg * (1.0 - sg)).astype(dag_ref.dtype)

    dp, dc, dn = _halo_specs(S, T, C, 1)
    up, uc, un = _halo_specs(S, T, C, 0)
    ap, ac, an = _halo_specs(S, T, C, 3)
    gp, gc, gn = _halo_specs(S, T, C, 4)
    vec = pl.BlockSpec((1, C), lambda i: (0, 0))
    return pl.pallas_call(
        body, name="conv_bwd", grid=(nblk,),
        in_specs=[dp, dc, dn, up, uc, un, ap, ac, an, gp, gc, gn,
                  pl.BlockSpec((32, C), lambda i: (0, 0)), vec, vec],
        out_specs=[pl.BlockSpec((T, 2 * C), lambda i: (i, 0)), pl.BlockSpec((32, C), lambda i: (0, 0)),
                   pl.BlockSpec((8, C), lambda i: (0, 0))],
        out_shape=[_sds((S, 2 * C), BF16), _sds((32, C), F32), _sds((8, C), F32)],
        scratch_shapes=[pltpu.VMEM((T + 2 * HALO, C), F32), pltpu.VMEM((T + 2 * HALO, C), F32),
                        pltpu.VMEM((T, C), F32)],
        compiler_params=_params("arbitrary"))(dac, dac, dac, u1, u1, u1, y, y, y, y, y, y, conv_w32, ln_g, ln_b)


def _xatt_fwd(xq, xk, xv, *, tm=512):
    S = xq.shape[0]
    M = xk.shape[0]
    tm = min(tm, S)
    scale = XATT_HEAD_DIM ** -0.5

    def body(q_ref, k_ref, v_ref, o_ref):
        for h in range(XATT_HEADS):
            sl = slice(h * XATT_HEAD_DIM, (h + 1) * XATT_HEAD_DIM)
            s = _nt(q_ref[:, sl], k_ref[:, sl]) * scale
            e = jnp.exp(s - jnp.max(s, axis=-1, keepdims=True))
            p = e / jnp.sum(e, axis=-1, keepdims=True)
            o_ref[:, sl] = jnp.dot(p.astype(BF16), v_ref[:, sl], preferred_element_type=F32).astype(o_ref.dtype)

    row = pl.BlockSpec((tm, D_MODEL), lambda i: (i, 0))
    full = pl.BlockSpec((M, D_MODEL), lambda i: (0, 0))
    return pl.pallas_call(
        body, name="xatt_fwd", grid=(S // tm,), in_specs=[row, full, full], out_specs=row,
        out_shape=_sds((S, D_MODEL), BF16), compiler_params=_params("parallel"))(xq, xk, xv)


def _xatt_bwd(xq, xk, xv, dxo, *, tm=512):
    S = xq.shape[0]
    M = xk.shape[0]
    tm = min(tm, S)
    scale = XATT_HEAD_DIM ** -0.5

    def body(q_ref, k_ref, v_ref, do_ref, dq_ref, dk_ref, dv_ref):
        i = pl.program_id(0)

        @pl.when(i == 0)
        def _():
            dk_ref[...] = jnp.zeros_like(dk_ref)
            dv_ref[...] = jnp.zeros_like(dv_ref)

        for h in range(XATT_HEADS):
            sl = slice(h * XATT_HEAD_DIM, (h + 1) * XATT_HEAD_DIM)
            q, k, v, do = q_ref[:, sl], k_ref[:, sl], v_ref[:, sl], do_ref[:, sl]
            s = _nt(q, k) * scale
            e = jnp.exp(s - jnp.max(s, axis=-1, keepdims=True))
            p = e / jnp.sum(e, axis=-1, keepdims=True)
            dp = _nt(do, v)
            ds = p * (dp - jnp.sum(dp * p, axis=-1, keepdims=True))
            dsb = ds.astype(BF16)
            dq_ref[:, sl] = (jnp.dot(dsb, k, preferred_element_type=F32) * scale).astype(dq_ref.dtype)
            dv_ref[:, sl] += _tn(p.astype(BF16), do)
            dk_ref[:, sl] += _tn(dsb, q) * scale

    row = pl.BlockSpec((tm, D_MODEL), lambda i: (i, 0))
    full = pl.BlockSpec((M, D_MODEL), lambda i: (0, 0))
    return pl.pallas_call(
        body, name="xatt_bwd", grid=(S // tm,), in_specs=[row, full, full, row], out_specs=[row, full, full],
        out_shape=[_sds((S, D_MODEL), BF16), _sds((M, D_MODEL), F32), _sds((M, D_MODEL), F32)],
        compiler_params=_params("arbitrary"))(xq, xk, xv, dxo)


def _row_tile(R):
    for t in (256, 128, 64, 32, 16, 8):
        if R % t == 0:
            return t
    return R


def _sum_partials(own, recv, me, *, name):
    _, R, C = own.shape
    t = _row_tile(R)

    def body(me_ref, own_ref, r_ref, o_ref):
        o_ref[...] = ((own_ref[...].astype(F32) + r_ref[0].astype(F32)) + r_ref[1].astype(F32)) + r_ref[2].astype(F32)

    return pl.pallas_call(
        body, name=name,
        grid_spec=pltpu.PrefetchScalarGridSpec(
            num_scalar_prefetch=1, grid=(R // t,),
            in_specs=[pl.BlockSpec((None, t, C), lambda i, me_ref: (me_ref[0], i, 0)),
                      pl.BlockSpec((3, t, C), lambda i, me_ref: (0, i, 0))],
            out_specs=pl.BlockSpec((t, C), lambda i, me_ref: (i, 0))),
        out_shape=_sds((R, C), F32), compiler_params=_params("parallel"))(me, own, recv)


def _adamw_math(w, g, m, v):
    m2 = ADAM_B1 * m + (1.0 - ADAM_B1) * g
    v2 = ADAM_B2 * v + (1.0 - ADAM_B2) * (g * g)
    m_hat = m2 / (1.0 - ADAM_B1 ** ADAM_STEP)
    v_hat = v2 / (1.0 - ADAM_B2 ** ADAM_STEP)
    delta = -ADAM_LR * (m_hat / (jnp.sqrt(v_hat) + ADAM_EPS) + ADAM_WD * w)
    return delta, m2, v2


def _adamw(parts, w, m, v, *, name):
    R, C = w.shape
    t = _row_tile(R)
    n = len(parts)

    def body(*refs):
        w_ref, m_ref, v_ref = refs[n:n + 3]
        g_ref, d_ref, m2_ref, v2_ref = refs[n + 3:]
        g = refs[0][...]
        for r in refs[1:n]:
            g = g + r[...]
        delta, m2, v2 = _adamw_math(w_ref[...], g, m_ref[...], v_ref[...])
        g_ref[...] = g
        d_ref[...] = delta
        m2_ref[...] = m2
        v2_ref[...] = v2

    blk = pl.BlockSpec((t, C), lambda i: (i, 0))
    return pl.pallas_call(
        body, name=name, grid=(R // t,), in_specs=[blk] * (n + 3), out_specs=[blk] * 4,
        out_shape=[_sds((R, C), F32)] * 4, compiler_params=_params("parallel"))(*parts, w, m, v)


def _sum_devices(gathered):
    _, R, C = gathered.shape

    def body(g_ref, o_ref):
        acc = g_ref[0]
        for k in range(1, N_DEV):
            acc = acc + g_ref[k]
        o_ref[...] = acc

    return pl.pallas_call(body, name="sum_devices", out_shape=_sds((R, C), F32))(gathered)


def _chip_peers():
    x, y = lax.axis_index("x"), lax.axis_index("y")
    return [(1 - x, y), (x, 1 - y), (1 - x, 1 - y)]


def _gather_weights(shards):
    n = len(shards)

    def body(*refs):
        ins, outs = refs[:n], refs[n:2 * n]
        send_sems, recv_sems, loc_sems = refs[2 * n:]
        x, y, c = lax.axis_index("x"), lax.axis_index("y"), lax.axis_index("c")
        me = 2 * x + y
        peers = _chip_peers()
        started = []
        for t in range(n):
            loc = pltpu.make_async_copy(ins[t], outs[t].at[me], loc_sems.at[t])
            loc.start()
            started.append(loc)
        sends = []
        for t in range(n):
            for k, (px, py) in enumerate(peers):
                cp = pltpu.make_async_remote_copy(
                    src_ref=ins[t], dst_ref=outs[t].at[me], send_sem=send_sems.at[t, k],
                    recv_sem=recv_sems.at[t, k], device_id=(px, py, c), device_id_type=MESH)
                cp.start()
                sends.append(cp)
        for t in range(n):
            for k, (px, py) in enumerate(peers):
                pltpu.make_async_remote_copy(
                    src_ref=ins[t], dst_ref=outs[t].at[2 * px + py], send_sem=send_sems.at[t, k],
                    recv_sem=recv_sems.at[t, k], device_id=(px, py, c), device_id_type=MESH).wait_recv()
        for cp in sends:
            cp.wait_send()
        for loc in started:
            loc.wait()

    any_spec = pl.BlockSpec(memory_space=pl.ANY)
    return pl.pallas_call(
        body, name="gather_weights", in_specs=[any_spec] * n, out_specs=[any_spec] * n,
        out_shape=[_sds((N_CHIPS,) + s.shape, s.dtype) for s in shards],
        scratch_shapes=[pltpu.SemaphoreType.DMA((n, 3)), pltpu.SemaphoreType.DMA((n, 3)),
                        pltpu.SemaphoreType.DMA((n,))])(*shards)


def _scatter_grads(grads, small):
    n = len(grads)

    def body(*refs):
        ins, small_ref = refs[:n], refs[n]
        outs, gath_ref = refs[n + 1:2 * n + 1], refs[2 * n + 1]
        send_sems, recv_sems, ssend, srecv, loc_sem = refs[2 * n + 2:]
        x, y, c = lax.axis_index("x"), lax.axis_index("y"), lax.axis_index("c")
        me = 4 * x + 2 * y + c
        peers = _chip_peers()
        flips = [(fx, fy, fc) for fx in (0, 1) for fy in (0, 1) for fc in (0, 1)][1:]

        def flipped(fx, fy, fc):
            return (1 - x if fx else x, 1 - y if fy else y, 1 - c if fc else c)

        loc = pltpu.make_async_copy(small_ref, gath_ref.at[me], loc_sem)
        loc.start()
        sends = []
        for j, (fx, fy, fc) in enumerate(flips):
            cp = pltpu.make_async_remote_copy(
                src_ref=small_ref, dst_ref=gath_ref.at[me], send_sem=ssend.at[j], recv_sem=srecv.at[j],
                device_id=flipped(fx, fy, fc), device_id_type=MESH)
            cp.start()
            sends.append(cp)
        for t in range(n):
            for k, (px, py) in enumerate(peers):
                cp = pltpu.make_async_remote_copy(
                    src_ref=ins[t].at[2 * px + py], dst_ref=outs[t].at[k], send_sem=send_sems.at[t, k],
                    recv_sem=recv_sems.at[t, k], device_id=(px, py, c), device_id_type=MESH)
                cp.start()
                sends.append(cp)
        for j, (fx, fy, fc) in enumerate(flips):
            px, py, pc = flipped(fx, fy, fc)
            pltpu.make_async_remote_copy(
                src_ref=small_ref, dst_ref=gath_ref.at[4 * px + 2 * py + pc], send_sem=ssend.at[j],
                recv_sem=srecv.at[j], device_id=(px, py, pc), device_id_type=MESH).wait_recv()
        for t in range(n):
            for k, (px, py) in enumerate(peers):
                pltpu.make_async_remote_copy(
                    src_ref=ins[t].at[2 * px + py], dst_ref=outs[t].at[k], send_sem=send_sems.at[t, k],
                    recv_sem=recv_sems.at[t, k], device_id=(px, py, c), device_id_type=MESH).wait_recv()
        for cp in sends:
            cp.wait_send()
        loc.wait()

    any_spec = pl.BlockSpec(memory_space=pl.ANY)
    out = pl.pallas_call(
        body, name="scatter_grads", in_specs=[any_spec] * (n + 1), out_specs=[any_spec] * (n + 1),
        out_shape=[_sds((3,) + g.shape[1:], g.dtype) for g in grads] + [_sds((N_DEV,) + small.shape, small.dtype)],
        scratch_shapes=[pltpu.SemaphoreType.DMA((n, 3)), pltpu.SemaphoreType.DMA((n, 3)),
                        pltpu.SemaphoreType.DMA((N_DEV - 1,)), pltpu.SemaphoreType.DMA((N_DEV - 1,)),
                        pltpu.SemaphoreType.DMA])(*grads, small)
    return out[:n], out[n]


def _swap_with_sibling(parts):
    n = len(parts)

    def body(*refs):
        ins, outs = refs[:n], refs[n:2 * n]
        send_sems, recv_sems = refs[2 * n:]
        sib = (lax.axis_index("x"), lax.axis_index("y"), 1 - lax.axis_index("c"))
        cps = []
        for t in range(n):
            cp = pltpu.make_async_remote_copy(
                src_ref=ins[t], dst_ref=outs[t], send_sem=send_sems.at[t], recv_sem=recv_sems.at[t],
                device_id=sib, device_id_type=MESH)
            cp.start()
            cps.append(cp)
        for cp in cps:
            cp.wait()

    any_spec = pl.BlockSpec(memory_space=pl.ANY)
    return pl.pallas_call(
        body, name="swap_with_sibling", in_specs=[any_spec] * n, out_specs=[any_spec] * n,
        out_shape=[_sds(p.shape, p.dtype) for p in parts],
        scratch_shapes=[pltpu.SemaphoreType.DMA((n,)), pltpu.SemaphoreType.DMA((n,))])(*parts)


BIG = ("w_in", "w_out", "w_xq", "w_xk", "w_xv", "w_xo", "w_up", "w_down")
COL_SHARDED = ("w_in", "w_up")


def _as_matrix(name, w4):
    if name in COL_SHARDED:
        return w4
    return w4.reshape(1, w4.shape[0] * w4.shape[1], w4.shape[2])


def _local_step(x, mem, target, W, conv_w32, vecs):
    S = x.shape[0]
    Wm = {k: _as_matrix(k, v) for k, v in W.items()}
    tables = _rope_tables(S)

    xn = _rms_fwd(x, vecs["norm_mix_g"], name="rms_mix")
    y = _mm_nn(xn, Wm["w_in"], name="mm_in", tn=640)
    qk, v_perm = _rope_fwd(y, tables)
    v_src = [(y, 2)] + [(v, 0) for v in v_perm[1:]]
    outs, lses = zip(*[_att_fwd(qk[p], v_src[p], d, name=f"att_fwd_d{d}") for p, d in enumerate(DILATIONS)])
    att, lg = _att_combine(outs, lses)
    cv, u1 = _conv_fwd(y, conv_w32, vecs["conv_b"], vecs["conv_ln_g"], vecs["conv_ln_b"])
    mix = jnp.concatenate([att, cv], axis=1)
    h1 = _mm_nn(mix, Wm["w_out"], name="mm_out", out_dtype=F32, res=x)
    hn = _rms_fwd(h1, vecs["norm_x_g"], name="rms_x")
    xq = _mm_nn(hn, Wm["w_xq"], name="mm_xq")
    mn = _rms_fwd(mem, vecs["norm_mem_g"], name="rms_mem")
    xk = _mm_nn(mn, Wm["w_xk"], name="mm_xk")
    xv = _mm_nn(mn, Wm["w_xv"], name="mm_xv")
    xo = _xatt_fwd(xq, xk, xv)
    h2 = _mm_nn(xo, Wm["w_xo"], name="mm_xo", out_dtype=F32, res=h1)
    hm = _rms_fwd(h2, vecs["norm_mlp_g"], name="rms_mlp")
    relu_up, act = _mm_nn(hm, Wm["w_up"], name="mm_up", relu2=True)
    h3 = _mm_nn(act, Wm["w_down"], name="mm_down", out_dtype=F32, res=h2)

    dh3, dh3b, dg_final, loss = _loss_head(h3, vecs["norm_final_g"], target)
    g = {}
    g["w_down"] = _mm_tn(act, dh3b, 1, name="dw_down")
    dup = _mm_nt(dh3b, Wm["w_down"], name="d_act", out_dtype=BF16, mul=relu_up)
    g["w_up"] = _mm_tn(hm, dup, N_CHIPS, name="dw_up")
    dhm = _mm_nt(dup, Wm["w_up"], name="d_hm")
    dh2, dh2b, dg_mlp = _rms_bwd(dhm, h2, vecs["norm_mlp_g"], dh3, name="rms_bwd_mlp")
    g["w_xo"] = _mm_tn(xo, dh2b, 1, name="dw_xo")
    dxo = _mm_nt(dh2b, Wm["w_xo"], name="d_xo", out_dtype=BF16)
    dxq, dxk, dxv = _xatt_bwd(xq, xk, xv, dxo)
    g["w_xq"] = _mm_tn(hn, dxq, 1, name="dw_xq")
    dhn = _mm_nt(dxq, Wm["w_xq"], name="d_hn")
    dh1, dh1b, dg_x = _rms_bwd(dhn, h1, vecs["norm_x_g"], dh2, name="rms_bwd_x")
    dxkb, dxvb = dxk.astype(BF16), dxv.astype(BF16)
    g["w_xk"] = _mm_tn(mn, dxkb, 1, name="dw_xk")
    g["w_xv"] = _mm_tn(mn, dxvb, 1, name="dw_xv")
    dmn = _mm_nt(jnp.concatenate([dxkb, dxvb], axis=1),
                 jnp.concatenate([Wm["w_xk"], Wm["w_xv"]], axis=2), name="d_mn")
    _, _, dg_mem = _rms_bwd(dmn, mem, vecs["norm_mem_g"], None, name="rms_bwd_mem")
    g["w_out"] = _mm_tn(mix, dh1b, 1, name="dw_out")
    dac = _mm_nt(dh1b, Wm["w_out"], name="d_mix", out_dtype=BF16)
    delta, do_perm = _att_delta(dac, att)
    do_src = [(dac, 0)] + [(t, 0) for t in do_perm[1:]]
    dq = [_att_dq(qk[p], v_src[p], do_src[p], lg[p], delta[p], d, name=f"att_dq_d{d}")
          for p, d in enumerate(DILATIONS)]
    dk, dv = zip(*[_att_dkv(qk[p], v_src[p], do_src[p], lg[p], delta[p], d, name=f"att_dkv_d{d}")
                   for p, d in enumerate(DILATIONS)])
    dag, dconv_w, dconv_small = _conv_bwd(dac, u1, y, conv_w32, vecs["conv_ln_g"], vecs["conv_ln_b"])
    dy = _assemble_dy(dq, dk, dv, dag, tables)
    g["w_in"] = _mm_tn(xn, dy, N_CHIPS, name="dw_in", tn=640)
    dxn = _mm_nt(dy, Wm["w_in"], name="d_xn", tn=640)
    grad_x, _, dg_mix = _rms_bwd(dxn, x, vecs["norm_mix_g"], dh1, name="rms_bwd_mix")

    big = {k: v.reshape(N_CHIPS, v.shape[0] * v.shape[1] // N_CHIPS, v.shape[2]) if k not in COL_SHARDED else v
           for k, v in g.items()}
    small = dict(conv_w=dconv_w, conv_small=dconv_small, norm_mix_g=dg_mix, norm_x_g=dg_x, norm_mem_g=dg_mem,
                 norm_mlp_g=dg_mlp, norm_final_g=dg_final, loss=loss)
    return grad_x, big, small


SMALL_ORDER = ("conv_w", "conv_small", "norm_mix_g", "norm_x_g", "norm_mem_g", "norm_mlp_g", "norm_final_g", "loss")


def _pack_small(small):
    rows, offs, pos = [], {}, 0
    for k in SMALL_ORDER:
        a = small[k]
        a = a.reshape(a.shape[0] * a.shape[1] // SMALL_W, SMALL_W)
        pad = (-a.shape[0]) % 8
        if pad:
            a = jnp.pad(a, ((0, pad), (0, 0)))
        rows.append(a)
        offs[k] = pos
        pos += a.shape[0]
    return jnp.concatenate(rows, axis=0), offs


def kernel(x, mem, norm_mix_g, w_in, conv_w, conv_b, conv_ln_g, conv_ln_b, w_out, norm_x_g, norm_mem_g, w_xq, w_xk, w_xv, w_xo, norm_mlp_g, w_up, w_down, norm_final_g, loss_target, m_norm_mix_g, m_w_in, m_conv_w, m_conv_b, m_conv_ln_g, m_conv_ln_b, m_w_out, m_norm_x_g, m_norm_mem_g, m_w_xq, m_w_xk, m_w_xv, m_w_xo, m_norm_mlp_g, m_w_up, m_w_down, m_norm_final_g, v_norm_mix_g, v_w_in, v_conv_w, v_conv_b, v_conv_ln_g, v_conv_ln_b, v_w_out, v_norm_x_g, v_norm_mem_g, v_w_xq, v_w_xk, v_w_xv, v_w_xo, v_norm_mlp_g, v_w_up, v_w_down, v_norm_final_g):
    names = ("norm_mix_g", "w_in", "conv_w", "conv_b", "conv_ln_g", "conv_ln_b", "w_out", "norm_x_g", "norm_mem_g",
             "w_xq", "w_xk", "w_xv", "w_xo", "norm_mlp_g", "w_up", "w_down", "norm_final_g")
    wts = dict(zip(names, (norm_mix_g, w_in, conv_w, conv_b, conv_ln_g, conv_ln_b, w_out, norm_x_g, norm_mem_g,
                           w_xq, w_xk, w_xv, w_xo, norm_mlp_g, w_up, w_down, norm_final_g)))
    mom = dict(zip(names, (m_norm_mix_g, m_w_in, m_conv_w, m_conv_b, m_conv_ln_g, m_conv_ln_b, m_w_out, m_norm_x_g,
                           m_norm_mem_g, m_w_xq, m_w_xk, m_w_xv, m_w_xo, m_norm_mlp_g, m_w_up, m_w_down, m_norm_final_g)))
    var = dict(zip(names, (v_norm_mix_g, v_w_in, v_conv_w, v_conv_b, v_conv_ln_g, v_conv_ln_b, v_w_out, v_norm_x_g,
                           v_norm_mem_g, v_w_xq, v_w_xk, v_w_xv, v_w_xo, v_norm_mlp_g, v_w_up, v_w_down, v_norm_final_g)))
    chip = 2 * lax.axis_index("x") + lax.axis_index("y")

    conv_w_pad = jnp.pad(wts["conv_w"][0], ((0, 1), (0, 0)))
    shards = [wts[k][0].astype(BF16) for k in BIG] + [conv_w_pad]
    gathered = _gather_weights(shards)
    W = dict(zip(BIG, gathered[:len(BIG)]))
    conv_w32 = jnp.transpose(gathered[-1], (1, 0, 2)).reshape(32, D_CONV)

    vecs = {k: wts[k] for k in ("norm_mix_g", "conv_b", "conv_ln_g", "conv_ln_b", "norm_x_g", "norm_mem_g", "norm_mlp_g")}
    vecs["norm_final_g"] = wts["norm_final_g"].reshape(1, D_MODEL)
    grad_x, big, small = _local_step(x[0], mem[0], loss_target[0], W, conv_w32, vecs)

    packed, offs = _pack_small(small)
    recv, gath = _scatter_grads([big[k] for k in BIG], packed)
    me_arr = jnp.reshape(chip, (1,)).astype(jnp.int32)
    sums = [_sum_partials(big[k], r, me_arr, name=f"sum_{k}") for k, r in zip(BIG, recv)]
    sib = _swap_with_sibling(sums)
    tot_small = _sum_devices(gath)

    res = {}
    for k, s_mine, s_sib in zip(BIG, sums, sib):
        res[k] = _adamw([s_mine, s_sib], wts[k][0], mom[k][0], var[k][0], name=f"adamw_{k}")

    def piece(key, nrows):
        return tot_small[offs[key]:offs[key] + nrows]

    def small_update(k, gfull):
        return _adamw([gfull], wts[k].reshape(gfull.shape), mom[k].reshape(gfull.shape),
                      var[k].reshape(gfull.shape), name=f"adamw_{k}")

    dcw = piece("conv_w", 32)[:CONV_WIDTH]
    dcw_mine = lax.dynamic_slice_in_dim(dcw, chip * (D_CONV // N_CHIPS), D_CONV // N_CHIPS, axis=1)
    res["conv_w"] = small_update("conv_w", dcw_mine)
    cs = piece("conv_small", 8)
    res["conv_b"] = small_update("conv_b", cs[0:1])
    res["conv_ln_g"] = small_update("conv_ln_g", cs[1:2])
    res["conv_ln_b"] = small_update("conv_ln_b", cs[2:3])
    for k in ("norm_mix_g", "norm_x_g", "norm_mem_g", "norm_mlp_g", "norm_final_g"):
        res[k] = small_update(k, piece(k, 8 * D_MODEL // SMALL_W).reshape(8, D_MODEL)[0:1])
    loss = piece("loss", 8)[0, 0]

    outs = [loss, grad_x[None]]
    for j in range(4):
        outs += [res[k][j].reshape(wts[k].shape) for k in names]
    return tuple(outs)
```

```python
import jax
import jax.numpy as jnp
from jax import lax
from jax.experimental import pallas as pl
from jax.experimental.pallas import tpu as pltpu

F32 = jnp.float32
BF16 = jnp.bfloat16
MESH = pl.DeviceIdType.MESH

D_MODEL = 1024
ATT_HEADS = 8
HEAD_DIM = 64
D_ATT = ATT_HEADS * HEAD_DIM
D_CONV = D_MODEL - D_ATT
DILATIONS = (1, 4, 16)
HALF = 64
ROPE_THETA = 500000.0
ROT_DIM = HEAD_DIM // 4
CONV_WIDTH = 31
CONV_PAD = (CONV_WIDTH - 1) // 2
XATT_HEADS = 4
XATT_HEAD_DIM = D_MODEL // XATT_HEADS
D_FF = 4 * D_MODEL
D_IN = 3 * D_ATT + 2 * D_CONV
EPS = 1e-6
NEG_INF = -1e30
N_CHIPS = 4
N_DEV = 8

ADAM_LR = 0.001
ADAM_B1 = 0.9
ADAM_B2 = 0.999
ADAM_EPS = 1e-08
ADAM_WD = 0.01
ADAM_STEP = 10

VMEM_LIMIT_V7X = 56 * 1024 * 1024
LANES = 128
HALO = 16
CONV_ROWS = 64
ATT_BLOCK = 128
SMALL_W = 512


def _params(*sem):
    return pltpu.CompilerParams(dimension_semantics=sem, vmem_limit_bytes=VMEM_LIMIT_V7X)


def _sds(shape, dtype):
    return jax.ShapeDtypeStruct(shape, dtype)


def _mm_nn(a, w3, *, name, out_dtype=BF16, res=None, relu2=False, tm=1024, tn=None, tk=1024):
    M, K = a.shape
    nsh, _, n = w3.shape
    tm, tk = min(tm, M), min(tk, K)
    tn = tn or min(n, 1024)
    npt, nk = n // tn, K // tk
    nj, N = nsh * npt, nsh * n
    n_out = 2 if relu2 else 1

    def body(*refs):
        a_ref, w_ref = refs[0], refs[1]
        pos = 2
        res_ref = None
        if res is not None:
            res_ref = refs[pos]
            pos += 1
        outs = refs[pos:pos + n_out]
        acc_ref = refs[pos + n_out] if nk > 1 else None

        def finish(acc):
            if res_ref is not None:
                acc = acc + res_ref[...]
            if relu2:
                r = jnp.maximum(acc, 0.0)
                outs[0][...] = r.astype(outs[0].dtype)
                outs[1][...] = (r * r).astype(outs[1].dtype)
            else:
                outs[0][...] = acc.astype(outs[0].dtype)

        part = jnp.dot(a_ref[...], w_ref[...], preferred_element_type=F32)
        if nk == 1:
            finish(part)
        else:
            k = pl.program_id(2)

            @pl.when(k == 0)
            def _():
                acc_ref[...] = part

            @pl.when(k > 0)
            def _():
                acc_ref[...] += part

            @pl.when(k == nk - 1)
            def _():
                finish(acc_ref[...])

    in_specs = [pl.BlockSpec((tm, tk), lambda i, j, k: (i, k)),
                pl.BlockSpec((None, tk, tn), lambda i, j, k: (j // npt, k, j % npt))]
    args = [a, w3]
    if res is not None:
        in_specs.append(pl.BlockSpec((tm, tn), lambda i, j, k: (i, j)))
        args.append(res)
    out_spec = pl.BlockSpec((tm, tn), lambda i, j, k: (i, j))
    out = pl.pallas_call(
        body, name=name, grid=(M // tm, nj, nk), in_specs=in_specs,
        out_specs=[out_spec] * n_out, out_shape=[_sds((M, N), out_dtype)] * n_out,
        scratch_shapes=[pltpu.VMEM((tm, tn), F32)] if nk > 1 else [],
        compiler_params=_params("parallel", "parallel", "arbitrary"))(*args)
    return tuple(out) if relu2 else out[0]


def _mm_nt(dy, w3, *, name, out_dtype=F32, mul=None, tm=1024, tn=None, tko=1024):
    M, N = dy.shape
    nsh, K, n = w3.shape
    tm, tko = min(tm, M), min(tko, K)
    tn = tn or min(n, 1024)
    npt = n // tn
    nj = nsh * npt

    def body(*refs):
        dy_ref, w_ref = refs[0], refs[1]
        pos = 2
        mul_ref = None
        if mul is not None:
            mul_ref = refs[pos]
            pos += 1
        out_ref = refs[pos]
        acc_ref = refs[pos + 1] if nj > 1 else None

        def finish(acc):
            if mul_ref is not None:
                acc = acc * (2.0 * mul_ref[...].astype(F32))
            out_ref[...] = acc.astype(out_ref.dtype)

        part = lax.dot_general(dy_ref[...], w_ref[...], (((1,), (1,)), ((), ())), preferred_element_type=F32)
        if nj == 1:
            finish(part)
        else:
            j = pl.program_id(2)

            @pl.when(j == 0)
            def _():
                acc_ref[...] = part

            @pl.when(j > 0)
            def _():
                acc_ref[...] += part

            @pl.when(j == nj - 1)
            def _():
                finish(acc_ref[...])

    in_specs = [pl.BlockSpec((tm, tn), lambda i, ko, j: (i, j)),
                pl.BlockSpec((None, tko, tn), lambda i, ko, j: (j // npt, ko, j % npt))]
    args = [dy, w3]
    if mul is not None:
        in_specs.append(pl.BlockSpec((tm, tko), lambda i, ko, j: (i, ko)))
        args.append(mul)
    return pl.pallas_call(
        body, name=name, grid=(M // tm, K // tko, nj), in_specs=in_specs,
        out_specs=pl.BlockSpec((tm, tko), lambda i, ko, j: (i, ko)), out_shape=_sds((M, K), out_dtype),
        scratch_shapes=[pltpu.VMEM((tm, tko), F32)] if nj > 1 else [],
        compiler_params=_params("parallel", "parallel", "arbitrary"))(*args)


def _mm_tn(a, dy, nsh, *, name, out_dtype=BF16, tm=1024, tk=1024, tn=None):
    M, K = a.shape
    N = dy.shape[1]
    n = N // nsh
    tm, tk = min(tm, M), min(tk, K)
    tn = tn or min(n, 1024)
    npt = n // tn
    nj, nm = nsh * npt, M // tm

    def body(a_ref, dy_ref, out_ref, acc_ref):
        m = pl.program_id(2)
        part = lax.dot_general(a_ref[...], dy_ref[...], (((0,), (0,)), ((), ())), preferred_element_type=F32)

        @pl.when(m == 0)
        def _():
            acc_ref[...] = part

        @pl.when(m > 0)
        def _():
            acc_ref[...] += part

        @pl.when(m == nm - 1)
        def _():
            out_ref[...] = acc_ref[...].astype(out_ref.dtype)

    return pl.pallas_call(
        body, name=name, grid=(K // tk, nj, nm),
        in_specs=[pl.BlockSpec((tm, tk), lambda kk, j, m: (m, kk)),
                  pl.BlockSpec((tm, tn), lambda kk, j, m: (m, j))],
        out_specs=pl.BlockSpec((None, tk, tn), lambda kk, j, m: (j // npt, kk, j % npt)),
        out_shape=_sds((nsh, K, n), out_dtype),
        scratch_shapes=[pltpu.VMEM((tk, tn), F32)],
        compiler_params=_params("parallel", "parallel", "arbitrary"))(a, dy)


def _rms_fwd(x, g, *, name, tm=512):
    M, Dm = x.shape
    tm = min(tm, M)

    def body(x_ref, g_ref, o_ref):
        xf = x_ref[...]
        r = lax.rsqrt(jnp.mean(xf * xf, axis=-1, keepdims=True) + EPS)
        o_ref[...] = (xf * r * g_ref[...]).astype(o_ref.dtype)

    return pl.pallas_call(
        body, name=name, grid=(M // tm,),
        in_specs=[pl.BlockSpec((tm, Dm), lambda i: (i, 0)), pl.BlockSpec((1, Dm), lambda i: (0, 0))],
        out_specs=pl.BlockSpec((tm, Dm), lambda i: (i, 0)), out_shape=_sds((M, Dm), BF16),
        compiler_params=_params("parallel"))(x, g)


def _rms_bwd(dxn, x, g, dres, *, name, tm=512):
    M, Dm = x.shape
    tm = min(tm, M)
    has_res = dres is not None

    def body(*refs):
        dxn_ref, x_ref, g_ref = refs[:3]
        dres_ref = refs[3] if has_res else None
        dx_ref, dxb_ref, dg_ref = refs[-3:]
        i = pl.program_id(0)
        xf = x_ref[...]
        r = lax.rsqrt(jnp.mean(xf * xf, axis=-1, keepdims=True) + EPS)
        nrm = xf * r
        dxn_f = dxn_ref[...].astype(F32)
        dn = dxn_f * g_ref[...]
        dx = r * (dn - nrm * jnp.mean(dn * nrm, axis=-1, keepdims=True))
        if has_res:
            dx = dx + dres_ref[...]
        dx_ref[...] = dx
        dxb_ref[...] = dx.astype(dxb_ref.dtype)

        @pl.when(i == 0)
        def _():
            dg_ref[...] = jnp.zeros_like(dg_ref)

        dg_ref[0:1, :] += jnp.sum(dxn_f * nrm, axis=0, keepdims=True)

    row = pl.BlockSpec((tm, Dm), lambda i: (i, 0))
    in_specs = [row, row, pl.BlockSpec((1, Dm), lambda i: (0, 0))] + ([row] if has_res else [])
    args = [dxn, x, g] + ([dres] if has_res else [])
    return pl.pallas_call(
        body, name=name, grid=(M // tm,), in_specs=in_specs,
        out_specs=[row, row, pl.BlockSpec((8, Dm), lambda i: (0, 0))],
        out_shape=[_sds((M, Dm), F32), _sds((M, Dm), BF16), _sds((8, Dm), F32)],
        compiler_params=_params("arbitrary"))(*args)


def _loss_head(h, g, target, *, tm=512):
    M, Dm = h.shape
    tm = min(tm, M)

    def body(h_ref, g_ref, t_ref, dh_ref, dhb_ref, dg_ref, loss_ref):
        i = pl.program_id(0)
        hf = h_ref[...]
        r = lax.rsqrt(jnp.mean(hf * hf, axis=-1, keepdims=True) + EPS)
        nrm = hf * r
        gv = g_ref[...]
        err = nrm * gv - t_ref[...]
        dy = err * (1.0 / Dm)
        dn = dy * gv
        dh = r * (dn - nrm * jnp.mean(dn * nrm, axis=-1, keepdims=True))
        dh_ref[...] = dh
        dhb_ref[...] = dh.astype(dhb_ref.dtype)

        @pl.when(i == 0)
        def _():
            dg_ref[...] = jnp.zeros_like(dg_ref)
            loss_ref[...] = jnp.zeros_like(loss_ref)

        dg_ref[0:1, :] += jnp.sum(dy * nrm, axis=0, keepdims=True)
        part = 0.5 * jnp.sum(jnp.mean(err * err, axis=-1, keepdims=True), axis=0, keepdims=True)
        sel = (lax.broadcasted_iota(jnp.int32, (8, 128), 0) == 0) & (lax.broadcasted_iota(jnp.int32, (8, 128), 1) == 0)
        loss_ref[...] += jnp.where(sel, part, 0.0)

    row = pl.BlockSpec((tm, Dm), lambda i: (i, 0))
    return pl.pallas_call(
        body, name="loss_head", grid=(M // tm,),
        in_specs=[row, pl.BlockSpec((1, Dm), lambda i: (0, 0)), row],
        out_specs=[row, row, pl.BlockSpec((8, Dm), lambda i: (0, 0)), pl.BlockSpec((8, 128), lambda i: (0, 0))],
        out_shape=[_sds((M, Dm), F32), _sds((M, Dm), BF16), _sds((8, Dm), F32), _sds((8, 128), F32)],
        compiler_params=_params("arbitrary"))(h, g, target)


def _class_spec(tm, d, width):
    return pl.BlockSpec((d, tm // d, width), lambda i: (0, i, 0))


def _row_scratch(tm, width):
    return pltpu.VMEM((width // LANES, tm, LANES), F32)


def _fill(scr, val):
    for c in range(scr.shape[0]):
        scr[c] = val[:, c * LANES:(c + 1) * LANES]


def _to_classes(scr, out_ref, d):
    n = scr.shape[1] // d
    for r in range(d):
        for c in range(scr.shape[0]):
            out_ref[r, :, c * LANES:(c + 1) * LANES] = scr[c, pl.ds(r, n, stride=d), :].astype(out_ref.dtype)


def _from_classes(in_ref, scr, d):
    n = scr.shape[1] // d
    for r in range(d):
        blk = in_ref[r].astype(F32)
        for c in range(scr.shape[0]):
            scr[c, pl.ds(r, n, stride=d), :] = blk[:, c * LANES:(c + 1) * LANES]
    return jnp.concatenate([scr[c] for c in range(scr.shape[0])], axis=1)


def _rope_tables(S):
    half = ROT_DIM // 2
    freqs = ROPE_THETA ** (-jnp.arange(0, ROT_DIM, 2, dtype=F32) / ROT_DIM)
    ang = jnp.arange(S, dtype=F32)[:, None] * freqs[None, :]
    cos, sin = jnp.cos(ang), jnp.sin(ang)
    ones = jnp.ones((S, HEAD_DIM - ROT_DIM), F32)
    zeros = jnp.zeros((S, HEAD_DIM - ROT_DIM), F32)
    zh = jnp.zeros((S, half), F32)
    c = jnp.concatenate([cos, cos, ones], axis=1)
    sa = jnp.concatenate([-sin, zh, zeros], axis=1)
    sb = jnp.concatenate([zh, sin, zeros], axis=1)
    return tuple(jnp.tile(t, (1, LANES // HEAD_DIM)) for t in (c, sa, sb))


def _rope_fwd(y, tables, *, tm=512):
    S = y.shape[0]
    W = 2 * D_ATT
    tm = min(tm, S)
    half = ROT_DIM // 2
    dils = [d for d in DILATIONS if d > 1]

    def body(y_ref, c_ref, sa_ref, sb_ref, qk_ref, *rest):
        qk_outs, v_outs = rest[:len(dils)], rest[len(dils):2 * len(dils)]
        scr_qk, scr_v = rest[2 * len(dils):]
        t = y_ref[:, 0:W].astype(F32)
        rep = W // LANES
        c, sa, sb = (jnp.tile(r[...], (1, rep)) for r in (c_ref, sa_ref, sb_ref))
        rot = t * c + pltpu.roll(t, W - half, axis=1) * sa + pltpu.roll(t, half, axis=1) * sb
        qk_ref[...] = rot.astype(qk_ref.dtype)
        _fill(scr_qk, rot)
        _fill(scr_v, y_ref[:, W:W + D_ATT].astype(F32))
        for d, qo, vo in zip(dils, qk_outs, v_outs):
            _to_classes(scr_qk, qo, d)
            _to_classes(scr_v, vo, d)

    tab = pl.BlockSpec((tm, LANES), lambda i: (i, 0))
    out = pl.pallas_call(
        body, name="rope_fwd", grid=(S // tm,),
        in_specs=[pl.BlockSpec((tm, 3 * D_ATT), lambda i: (i, 0)), tab, tab, tab],
        out_specs=[pl.BlockSpec((tm, W), lambda i: (i, 0))] + [_class_spec(tm, d, W) for d in dils]
        + [_class_spec(tm, d, D_ATT) for d in dils],
        out_shape=[_sds((S, W), BF16)] + [_sds((d, S // d, W), BF16) for d in dils]
        + [_sds((d, S // d, D_ATT), BF16) for d in dils],
        scratch_shapes=[_row_scratch(tm, W), _row_scratch(tm, D_ATT)],
        compiler_params=_params("parallel"))(y, *tables)
    qk = [out[0]] + [o.reshape(S, W) for o in out[1:1 + len(dils)]]
    v = [None] + [o.reshape(S, D_ATT) for o in out[1 + len(dils):]]
    return qk, v


def _assemble_dy(dq, dk, dv, dag, tables, *, tm=512):
    S = dag.shape[0]
    tm = min(tm, S)
    half = ROT_DIM // 2
    W = D_ATT
    n_pat = len(DILATIONS)

    def body(*refs):
        groups = [refs[g * n_pat:(g + 1) * n_pat] for g in range(3)]
        dag_ref, c_ref, sa_ref, sb_ref, o_ref, scr = refs[3 * n_pat:]
        rep = W // LANES
        c, sa, sb = (jnp.tile(r[...], (1, rep)) for r in (c_ref, sa_ref, sb_ref))

        def total(rs):
            acc = rs[0][...].astype(F32)
            for d, r in zip(DILATIONS[1:], rs[1:]):
                acc = acc + _from_classes(r, scr, d)
            return acc

        def unrope(dr):
            return dr * c + pltpu.roll(dr * sa, half, axis=1) + pltpu.roll(dr * sb, W - half, axis=1)

        o_ref[:, 0:W] = unrope(total(groups[0])).astype(o_ref.dtype)
        o_ref[:, W:2 * W] = unrope(total(groups[1])).astype(o_ref.dtype)
        o_ref[:, 2 * W:3 * W] = total(groups[2]).astype(o_ref.dtype)
        o_ref[:, 3 * W:] = dag_ref[...]

    specs = [pl.BlockSpec((tm, W), lambda i: (i, 0))] + [_class_spec(tm, d, W) for d in DILATIONS[1:]]
    tab = pl.BlockSpec((tm, LANES), lambda i: (i, 0))
    args = [a if d == 1 else a.reshape(d, S // d, W) for grp in (dq, dk, dv) for d, a in zip(DILATIONS, grp)]
    return pl.pallas_call(
        body, name="assemble_dy", grid=(S // tm,),
        in_specs=specs * 3 + [pl.BlockSpec((tm, 2 * D_CONV), lambda i: (i, 0)), tab, tab, tab],
        out_specs=pl.BlockSpec((tm, D_IN), lambda i: (i, 0)), out_shape=_sds((S, D_IN), BF16),
        scratch_shapes=[_row_scratch(tm, W)],
        compiler_params=_params("parallel"))(*args, dag, *tables)


def _seq_specs(L, tb, col):
    nb, per, nh = L // tb, tb // HALF, L // HALF
    centre = pl.BlockSpec((tb, D_ATT), lambda r, i: (r * nb + i, col))
    prev = pl.BlockSpec((HALF, D_ATT), lambda r, i: (r * nh + jnp.maximum(i * per - 1, 0), col))
    nxt = pl.BlockSpec((HALF, D_ATT), lambda r, i: (r * nh + jnp.minimum((i + 1) * per, nh - 1), col))
    return prev, centre, nxt


def _band_mask(i, tq, L, centre_is_query):
    if centre_is_query:
        shape = (tq, tq + 2 * HALF)
        c_idx = lax.broadcasted_iota(jnp.int32, shape, 0)
        w_idx = lax.broadcasted_iota(jnp.int32, shape, 1)
    else:
        shape = (tq + 2 * HALF, tq)
        w_idx = lax.broadcasted_iota(jnp.int32, shape, 0)
        c_idx = lax.broadcasted_iota(jnp.int32, shape, 1)
    diff = w_idx - c_idx
    wpos = i * tq - HALF + w_idx
    return (diff >= 0) & (diff <= 2 * HALF) & (wpos >= 0) & (wpos < L)


def _lane_groups():
    for c0 in range(0, D_ATT, LANES):
        yield slice(c0, c0 + LANES)


def _first_head(rows):
    return lax.broadcasted_iota(jnp.int32, (rows, LANES), 1) < HEAD_DIM


def _split_pair(x, first):
    zero = jnp.zeros_like(x)
    return jnp.where(first, x, zero), jnp.where(first, zero, x)


def _nt(a, b):
    return lax.dot_general(a, b, (((1,), (1,)), ((), ())), preferred_element_type=F32)


def _tn(a, b):
    return lax.dot_general(a, b, (((0,), (0,)), ((), ())), preferred_element_type=F32)


ATT_SCALE = HEAD_DIM ** -0.5


def _att_fwd(qk, v_src, d, *, name):
    S = qk.shape[0]
    L = S // d
    tq = min(ATT_BLOCK, L)
    v_arr, v_col = v_src

    def body(q_ref, kp_ref, kc_ref, kn_ref, vp_ref, vc_ref, vn_ref, o_ref, lse_ref):
        i = pl.program_id(1)
        valid = _band_mask(i, tq, L, True)
        q = q_ref[...] * ATT_SCALE
        kwin = jnp.concatenate([kp_ref[...], kc_ref[...], kn_ref[...]], axis=0)
        vwin = jnp.concatenate([vp_ref[...], vc_ref[...], vn_ref[...]], axis=0)
        first = _first_head(tq)
        for ls in _lane_groups():
            k2, v2 = kwin[:, ls], vwin[:, ls]
            s = [jnp.where(valid, _nt(t, k2), NEG_INF) for t in _split_pair(q[:, ls], first)]
            m = [jnp.max(t, axis=-1, keepdims=True) for t in s]
            p = [jnp.exp(t - mm) for t, mm in zip(s, m)]
            den = [jnp.sum(t, axis=-1, keepdims=True) for t in p]
            o = [jnp.dot(t.astype(BF16), v2, preferred_element_type=F32) * (1.0 / dd) for t, dd in zip(p, den)]
            lse = [mm + jnp.log(dd) for mm, dd in zip(m, den)]
            o_ref[:, ls] = jnp.where(first, o[0], o[1])
            lse_ref[:, ls] = jnp.where(first, lse[0], lse[1])

    _, qc, _ = _seq_specs(L, tq, 0)
    kp, kc, kn = _seq_specs(L, tq, 1)
    vp, vc, vn = _seq_specs(L, tq, v_col)
    out = pl.BlockSpec((tq, D_ATT), lambda r, i: (r * (L // tq) + i, 0))
    return pl.pallas_call(
        body, name=name, grid=(d, L // tq),
        in_specs=[qc, kp, kc, kn, vp, vc, vn], out_specs=[out, out],
        out_shape=[_sds((S, D_ATT), F32)] * 2,
        compiler_params=_params("parallel", "parallel"))(qk, qk, qk, qk, v_arr, v_arr, v_arr)


def _att_combine(outs, lses, *, tm=512):
    S = outs[0].shape[0]
    tm = min(tm, S)
    dils = DILATIONS[1:]
    n_d = len(dils)

    def body(*refs):
        o_refs, l_refs = refs[0:1 + n_d], refs[1 + n_d:2 + 2 * n_d]
        att_ref, lg_ref = refs[2 + 2 * n_d:4 + 2 * n_d]
        lg_outs = refs[4 + 2 * n_d:4 + 3 * n_d]
        scr = refs[4 + 3 * n_d:]
        scr_o, scr_l, scr_lg = scr[:n_d], scr[n_d:2 * n_d], scr[2 * n_d]
        ls = [l_refs[0][...]] + [_from_classes(r, s, d) for r, s, d in zip(l_refs[1:], scr_l, dils)]
        os_ = [o_refs[0][...]] + [_from_classes(r, s, d) for r, s, d in zip(o_refs[1:], scr_o, dils)]
        mx = ls[0]
        for l in ls[1:]:
            mx = jnp.maximum(mx, l)
        es = [jnp.exp(l - mx) for l in ls]
        tot = es[0]
        num = es[0] * os_[0]
        for e, o in zip(es[1:], os_[1:]):
            tot = tot + e
            num = num + e * o
        att_ref[...] = (num / tot).astype(att_ref.dtype)
        lg = mx + jnp.log(tot)
        lg_ref[...] = lg
        _fill(scr_lg, lg)
        for d, out in zip(dils, lg_outs):
            _to_classes(scr_lg, out, d)

    nat = pl.BlockSpec((tm, D_ATT), lambda i: (i, 0))
    specs = [nat] + [_class_spec(tm, d, D_ATT) for d in dils]
    view = lambda arrs: [arrs[0]] + [a.reshape(d, S // d, D_ATT) for a, d in zip(arrs[1:], dils)]
    out = pl.pallas_call(
        body, name="att_combine", grid=(S // tm,), in_specs=specs * 2,
        out_specs=[nat, nat] + specs[1:],
        out_shape=[_sds((S, D_ATT), BF16), _sds((S, D_ATT), F32)] + [_sds((d, S // d, D_ATT), F32) for d in dils],
        scratch_shapes=[_row_scratch(tm, D_ATT)] * (2 * n_d + 1),
        compiler_params=_params("parallel"))(*view(list(outs)), *view(list(lses)))
    return out[0], [out[1]] + [o.reshape(S, D_ATT) for o in out[2:]]


def _att_delta(dac, att, *, tm=512):
    S = att.shape[0]
    tm = min(tm, S)
    dils = DILATIONS[1:]
    n_d = len(dils)

    def body(do_ref, o_ref, dl_ref, *rest):
        dl_outs, do_outs = rest[:n_d], rest[n_d:2 * n_d]
        scr_dl, scr_do = rest[2 * n_d:]
        do = do_ref[...].astype(F32)
        prod = do * o_ref[...].astype(F32)
        per_head = [jnp.broadcast_to(jnp.sum(prod[:, h * HEAD_DIM:(h + 1) * HEAD_DIM], axis=-1, keepdims=True),
                                     (tm, HEAD_DIM)) for h in range(ATT_HEADS)]
        dl = jnp.concatenate(per_head, axis=1)
        dl_ref[...] = dl
        _fill(scr_dl, dl)
        _fill(scr_do, do)
        for d, dlo, doo in zip(dils, dl_outs, do_outs):
            _to_classes(scr_dl, dlo, d)
            _to_classes(scr_do, doo, d)

    blk = pl.BlockSpec((tm, D_ATT), lambda i: (i, 0))
    out = pl.pallas_call(
        body, name="att_delta", grid=(S // tm,), in_specs=[blk, blk],
        out_specs=[blk] + [_class_spec(tm, d, D_ATT) for d in dils] * 2,
        out_shape=[_sds((S, D_ATT), F32)] + [_sds((d, S // d, D_ATT), F32) for d in dils]
        + [_sds((d, S // d, D_ATT), BF16) for d in dils],
        scratch_shapes=[_row_scratch(tm, D_ATT), _row_scratch(tm, D_ATT)],
        compiler_params=_params("parallel"))(dac, att)
    delta = [out[0]] + [o.reshape(S, D_ATT) for o in out[1:1 + n_d]]
    do = [None] + [o.reshape(S, D_ATT) for o in out[1 + n_d:]]
    return delta, do


def _att_dq(qk, v_src, do_src, lg, delta, d, *, name):
    S = qk.shape[0]
    L = S // d
    tq = min(ATT_BLOCK, L)
    (v_arr, v_col), (do_arr, do_col) = v_src, do_src

    def body(q_ref, kp_ref, kc_ref, kn_ref, vp_ref, vc_ref, vn_ref, do_ref, lg_ref, dl_ref, dq_ref):
        i = pl.program_id(1)
        valid = _band_mask(i, tq, L, True)
        q, do = q_ref[...] * ATT_SCALE, do_ref[...]
        kwin = jnp.concatenate([kp_ref[...], kc_ref[...], kn_ref[...]], axis=0)
        vwin = jnp.concatenate([vp_ref[...], vc_ref[...], vn_ref[...]], axis=0)
        first = _first_head(tq)
        for ls in _lane_groups():
            k2, v2 = kwin[:, ls], vwin[:, ls]
            cols = (ls.start, ls.start + HEAD_DIM)
            s = [jnp.where(valid, _nt(t, k2), NEG_INF) for t in _split_pair(q[:, ls], first)]
            p = [jnp.exp(t - lg_ref[:, c:c + 1]) for t, c in zip(s, cols)]
            dp = [_nt(t, v2) for t in _split_pair(do[:, ls], first)]
            ds = [pp * (t - dl_ref[:, c:c + 1]) for pp, t, c in zip(p, dp, cols)]
            dq = [jnp.dot(t.astype(BF16), k2, preferred_element_type=F32) for t in ds]
            dq_ref[:, ls] = (jnp.where(first, dq[0], dq[1]) * ATT_SCALE).astype(dq_ref.dtype)

    _, qc, _ = _seq_specs(L, tq, 0)
    kp, kc, kn = _seq_specs(L, tq, 1)
    vp, vc, vn = _seq_specs(L, tq, v_col)
    _, doc, _ = _seq_specs(L, tq, do_col)
    row = pl.BlockSpec((tq, D_ATT), lambda r, i: (r * (L // tq) + i, 0))
    return pl.pallas_call(
        body, name=name, grid=(d, L // tq),
        in_specs=[qc, kp, kc, kn, vp, vc, vn, doc, row, row], out_specs=row,
        out_shape=_sds((S, D_ATT), BF16),
        compiler_params=_params("parallel", "parallel"))(qk, qk, qk, qk, v_arr, v_arr, v_arr, do_arr, lg, delta)


def _att_dkv(qk, v_src, do_src, lg, delta, d, *, name):
    S = qk.shape[0]
    L = S // d
    tk = min(ATT_BLOCK, L)
    (v_arr, v_col), (do_arr, do_col) = v_src, do_src

    def body(k_ref, v_ref, qp_ref, qc_ref, qn_ref, dop_ref, doc_ref, don_ref,
             lgp_ref, lgc_ref, lgn_ref, dlp_ref, dlc_ref, dln_ref, dk_ref, dv_ref):
        i = pl.program_id(1)
        valid = _band_mask(i, tk, L, False)
        k, v = k_ref[...], v_ref[...]
        qwin = jnp.concatenate([qp_ref[...], qc_ref[...], qn_ref[...]], axis=0) * ATT_SCALE
        dowin = jnp.concatenate([dop_ref[...], doc_ref[...], don_ref[...]], axis=0)

        def column(refs, c0):
            return jnp.concatenate([r[:, c0:c0 + 1] for r in refs], axis=0)

        first_w, first_k = _first_head(tk + 2 * HALF), _first_head(tk)
        for ls in _lane_groups():
            k2, v2, q2, do2 = k[:, ls], v[:, ls], qwin[:, ls], dowin[:, ls]
            cols = (ls.start, ls.start + HEAD_DIM)
            lgw = [column((lgp_ref, lgc_ref, lgn_ref), c) for c in cols]
            dlw = [column((dlp_ref, dlc_ref, dln_ref), c) for c in cols]
            s = [jnp.where(valid, _nt(t, k2), NEG_INF) for t in _split_pair(q2, first_w)]
            p = [jnp.exp(t - l) for t, l in zip(s, lgw)]
            dp = [_nt(t, v2) for t in _split_pair(do2, first_w)]
            ds = [pp * (t - l) for pp, t, l in zip(p, dp, dlw)]
            dv = [_tn(pp.astype(BF16), do2) for pp in p]
            dk = [_tn(t.astype(BF16), q2) for t in ds]
            dk_ref[:, ls] = jnp.where(first_k, dk[0], dk[1]).astype(dk_ref.dtype)
            dv_ref[:, ls] = jnp.where(first_k, dv[0], dv[1]).astype(dv_ref.dtype)

    _, kc, _ = _seq_specs(L, tk, 1)
    _, vc, _ = _seq_specs(L, tk, v_col)
    qp, qc, qn = _seq_specs(L, tk, 0)
    dop, doc, don = _seq_specs(L, tk, do_col)
    rp, rc, rn = _seq_specs(L, tk, 0)
    out = pl.BlockSpec((tk, D_ATT), lambda r, i: (r * (L // tk) + i, 0))
    return pl.pallas_call(
        body, name=name, grid=(d, L // tk),
        in_specs=[kc, vc, qp, qc, qn, dop, doc, don, rp, rc, rn, rp, rc, rn], out_specs=[out, out],
        out_shape=[_sds((S, D_ATT), BF16)] * 2,
        compiler_params=_params("parallel", "parallel"))(
            qk, v_arr, qk, qk, qk, do_arr, do_arr, do_arr, lg, lg, lg, delta, delta, delta)


def _sigmoid(x):
    return 1.0 / (1.0 + jnp.exp(-x))


def _halo_specs(S, T, width, col):
    last = S // HALO - 1
    per = T // HALO
    centre = pl.BlockSpec((T, width), lambda i: (i, col))
    prev = pl.BlockSpec((HALO, width), lambda i: (jnp.maximum(i * per - 1, 0), col))
    nxt = pl.BlockSpec((HALO, width), lambda i: (jnp.minimum((i + 1) * per, last), col))
    return prev, centre, nxt


def _window_scratch(T, C):
    return pltpu.VMEM((8, T + 2 * HALO, C), F32)


def _fill_window(buf, prev, centre, nxt, T):
    buf[0, 0:HALO, :] = prev
    buf[0, HALO:HALO + T, :] = centre
    buf[0, HALO + T:, :] = nxt
    rows = T + 2 * HALO - 8
    for s in range(1, 8):
        buf[s, 0:rows, :] = buf[0, s:s + rows, :]


def _window_rows(buf, off, r0, ls):
    s = off % 8
    return buf[s, pl.ds(off - s + r0, CONV_ROWS), ls]


def _depthwise(buf, w_ref, out_ref, T, C, first_off, step):
    def row_tile(t, carry):
        r0 = pl.multiple_of(t * CONV_ROWS, CONV_ROWS)
        for c0 in range(0, C, LANES):
            ls = slice(c0, c0 + LANES)
            acc = jnp.zeros((CONV_ROWS, LANES), F32)
            for k in range(CONV_WIDTH):
                acc = acc + _window_rows(buf, first_off + step * k, r0, ls) * w_ref[k:k + 1, ls]
            out_ref[pl.ds(r0, CONV_ROWS), ls] = acc
        return carry

    lax.fori_loop(0, T // CONV_ROWS, row_tile, 0)


def _conv_fwd(y, conv_w32, conv_b, ln_g, ln_b, *, T=512):
    S = y.shape[0]
    T = min(T, S)
    nblk = S // T
    C = D_CONV

    def body(ap, ac, an, gp, gc, gn, w_ref, b_ref, lg_ref, lb_ref, cv_ref, u1_ref, buf):
        i = pl.program_id(0)

        def glu(a_ref, g_ref):
            return a_ref[...].astype(F32) * _sigmoid(g_ref[...].astype(F32))

        _fill_window(buf, jnp.where(i > 0, glu(ap, gp), 0.0), glu(ac, gc),
                     jnp.where(i < nblk - 1, glu(an, gn), 0.0), T)
        _depthwise(buf, w_ref, u1_ref, T, C, HALO - CONV_PAD, 1)
        u1 = u1_ref[...] + b_ref[...]
        u1_ref[...] = u1
        mu = jnp.mean(u1, axis=-1, keepdims=True)
        xc = u1 - mu
        rstd = lax.rsqrt(jnp.mean(xc * xc, axis=-1, keepdims=True) + EPS)
        u2 = xc * rstd * lg_ref[...] + lb_ref[...]
        cv_ref[...] = (u2 * _sigmoid(u2)).astype(cv_ref.dtype)

    ap, ac, an = _halo_specs(S, T, C, 3)
    gp, gc, gn = _halo_specs(S, T, C, 4)
    vec = pl.BlockSpec((1, C), lambda i: (0, 0))
    out = pl.BlockSpec((T, C), lambda i: (i, 0))
    return pl.pallas_call(
        body, name="conv_fwd", grid=(nblk,),
        in_specs=[ap, ac, an, gp, gc, gn, pl.BlockSpec((32, C), lambda i: (0, 0)), vec, vec, vec],
        out_specs=[out, out], out_shape=[_sds((S, C), BF16), _sds((S, C), F32)],
        scratch_shapes=[_window_scratch(T, C)],
        compiler_params=_params("parallel"))(y, y, y, y, y, y, conv_w32, conv_b, ln_g, ln_b)


def _conv_bwd(dac, u1, y, conv_w32, ln_g, ln_b, *, T=512):
    S = y.shape[0]
    T = min(T, S)
    nblk = S // T
    C = D_CONV

    def body(dp, dc, dn, up, uc, un, ap, ac, an, gp, gc, gn, w_ref, lg_ref, lb_ref,
             dag_ref, dw_ref, dsm_ref, bufd, bufu, du0_scr, dw_acc):
        i = pl.program_id(0)
        lg = lg_ref[...]

        def du1_of(dcv_ref, u1_ref):
            u1 = u1_ref[...]
            mu = jnp.mean(u1, axis=-1, keepdims=True)
            xc = u1 - mu
            rstd = lax.rsqrt(jnp.mean(xc * xc, axis=-1, keepdims=True) + EPS)
            xhat = xc * rstd
            u2 = xhat * lg + lb_ref[...]
            sg = _sigmoid(u2)
            du2 = dcv_ref[...].astype(F32) * (sg * (1.0 + u2 * (1.0 - sg)))
            dxh = du2 * lg
            du1 = rstd * (dxh - jnp.mean(dxh, axis=-1, keepdims=True)
                          - xhat * jnp.mean(dxh * xhat, axis=-1, keepdims=True))
            return du1, du2, xhat

        def glu(a_ref, g_ref):
            return a_ref[...].astype(F32) * _sigmoid(g_ref[...].astype(F32))

        @pl.when(i == 0)
        def _():
            dw_ref[...] = jnp.zeros_like(dw_ref)
            dsm_ref[...] = jnp.zeros_like(dsm_ref)

        du1_c, du2_c, xhat_c = du1_of(dc, uc)
        dsm_ref[0:1, :] += jnp.sum(du1_c, axis=0, keepdims=True)
        dsm_ref[1:2, :] += jnp.sum(du2_c * xhat_c, axis=0, keepdims=True)
        dsm_ref[2:3, :] += jnp.sum(du2_c, axis=0, keepdims=True)
        _fill_window(bufd, jnp.where(i > 0, du1_of(dp, up)[0], 0.0), du1_c,
                     jnp.where(i < nblk - 1, du1_of(dn, un)[0], 0.0), T)
        _fill_window(bufu, jnp.where(i > 0, glu(ap, gp), 0.0), glu(ac, gc),
                     jnp.where(i < nblk - 1, glu(an, gn), 0.0), T)

        _depthwise(bufd, w_ref, du0_scr, T, C, HALO + CONV_PAD, -1)
        dw_acc[...] = jnp.zeros_like(dw_acc)

        def dw_tile(t, carry):
            r0 = pl.multiple_of(t * CONV_ROWS, CONV_ROWS)
            for c0 in range(0, C, LANES):
                ls = slice(c0, c0 + LANES)
                d = _window_rows(bufd, HALO, r0, ls)
                for k in range(CONV_WIDTH):
                    prod = d * _window_rows(bufu, HALO - CONV_PAD + k, r0, ls)
                    part = prod[0:8]
                    for j in range(8, CONV_ROWS, 8):
                        part = part + prod[j:j + 8]
                    dw_acc[k, :, ls] += part
            return carry

        lax.fori_loop(0, T // CONV_ROWS, dw_tile, 0)
        for k in range(CONV_WIDTH):
            dw_ref[k:k + 1, :] += jnp.sum(dw_acc[k], axis=0, keepdims=True)
        du0 = du0_scr[...]
        a = ac[...].astype(F32)
        sg = _sigmoid(gc[...].astype(F32))
        dag_ref[:, 0:C] = (du0 * sg).astype(dag_ref.dtype)
        dag_ref[:, C:] = (du0 * a * sg * (1.0 - sg)).astype(dag_ref.dtype)

    dp, dc, dn = _halo_specs(S, T, C, 1)
    up, uc, un = _halo_specs(S, T, C, 0)
    ap, ac, an = _halo_specs(S, T, C, 3)
    gp, gc, gn = _halo_specs(S, T, C, 4)
    vec = pl.BlockSpec((1, C), lambda i: (0, 0))
    return pl.pallas_call(
        body, name="conv_bwd", grid=(nblk,),
        in_specs=[dp, dc, dn, up, uc, un, ap, ac, an, gp, gc, gn,
                  pl.BlockSpec((32, C), lambda i: (0, 0)), vec, vec],
        out_specs=[pl.BlockSpec((T, 2 * C), lambda i: (i, 0)), pl.BlockSpec((32, C), lambda i: (0, 0)),
                   pl.BlockSpec((8, C), lambda i: (0, 0))],
        out_shape=[_sds((S, 2 * C), BF16), _sds((32, C), F32), _sds((8, C), F32)],
        scratch_shapes=[_window_scratch(T, C), _window_scratch(T, C), pltpu.VMEM((T, C), F32),
                        pltpu.VMEM((CONV_WIDTH, 8, C), F32)],
        compiler_params=_params("arbitrary"))(dac, dac, dac, u1, u1, u1, y, y, y, y, y, y, conv_w32, ln_g, ln_b)


def _xatt_fwd(xq, xk, xv, *, tm=512):
    S = xq.shape[0]
    M = xk.shape[0]
    tm = min(tm, S)
    scale = XATT_HEAD_DIM ** -0.5

    def body(q_ref, k_ref, v_ref, o_ref):
        for h in range(XATT_HEADS):
            sl = slice(h * XATT_HEAD_DIM, (h + 1) * XATT_HEAD_DIM)
            s = _nt(q_ref[:, sl], k_ref[:, sl]) * scale
            e = jnp.exp(s - jnp.max(s, axis=-1, keepdims=True))
            p = e / jnp.sum(e, axis=-1, keepdims=True)
            o_ref[:, sl] = jnp.dot(p.astype(BF16), v_ref[:, sl], preferred_element_type=F32).astype(o_ref.dtype)

    row = pl.BlockSpec((tm, D_MODEL), lambda i: (i, 0))
    full = pl.BlockSpec((M, D_MODEL), lambda i: (0, 0))
    return pl.pallas_call(
        body, name="xatt_fwd", grid=(S // tm,), in_specs=[row, full, full], out_specs=row,
        out_shape=_sds((S, D_MODEL), BF16), compiler_params=_params("parallel"))(xq, xk, xv)


def _xatt_bwd(xq, xk, xv, dxo, *, tm=512):
    S = xq.shape[0]
    M = xk.shape[0]
    tm = min(tm, S)
    scale = XATT_HEAD_DIM ** -0.5

    def body(q_ref, k_ref, v_ref, do_ref, dq_ref, dk_ref, dv_ref):
        i = pl.program_id(0)

        @pl.when(i == 0)
        def _():
            dk_ref[...] = jnp.zeros_like(dk_ref)
            dv_ref[...] = jnp.zeros_like(dv_ref)

        for h in range(XATT_HEADS):
            sl = slice(h * XATT_HEAD_DIM, (h + 1) * XATT_HEAD_DIM)
            q, k, v, do = q_ref[:, sl], k_ref[:, sl], v_ref[:, sl], do_ref[:, sl]
            s = _nt(q, k) * scale
            e = jnp.exp(s - jnp.max(s, axis=-1, keepdims=True))
            p = e / jnp.sum(e, axis=-1, keepdims=True)
            dp = _nt(do, v)
            ds = p * (dp - jnp.sum(dp * p, axis=-1, keepdims=True))
            dsb = ds.astype(BF16)
            dq_ref[:, sl] = (jnp.dot(dsb, k, preferred_element_type=F32) * scale).astype(dq_ref.dtype)
            dv_ref[:, sl] += _tn(p.astype(BF16), do)
            dk_ref[:, sl] += _tn(dsb, q) * scale

    row = pl.BlockSpec((tm, D_MODEL), lambda i: (i, 0))
    full = pl.BlockSpec((M, D_MODEL), lambda i: (0, 0))
    return pl.pallas_call(
        body, name="xatt_bwd", grid=(S // tm,), in_specs=[row, full, full, row], out_specs=[row, full, full],
        out_shape=[_sds((S, D_MODEL), BF16), _sds((M, D_MODEL), F32), _sds((M, D_MODEL), F32)],
        compiler_params=_params("arbitrary"))(xq, xk, xv, dxo)


def _row_tile(R):
    for t in (256, 128, 64, 32, 16, 8):
        if R % t == 0:
            return t
    return R


def _sum_partials(own, recv, me, *, name):
    _, R, C = own.shape
    t = _row_tile(R)

    def body(me_ref, own_ref, r_ref, o_ref):
        o_ref[...] = ((own_ref[...].astype(F32) + r_ref[0].astype(F32)) + r_ref[1].astype(F32)) + r_ref[2].astype(F32)

    return pl.pallas_call(
        body, name=name,
        grid_spec=pltpu.PrefetchScalarGridSpec(
            num_scalar_prefetch=1, grid=(R // t,),
            in_specs=[pl.BlockSpec((None, t, C), lambda i, me_ref: (me_ref[0], i, 0)),
                      pl.BlockSpec((3, t, C), lambda i, me_ref: (0, i, 0))],
            out_specs=pl.BlockSpec((t, C), lambda i, me_ref: (i, 0))),
        out_shape=_sds((R, C), F32), compiler_params=_params("parallel"))(me, own, recv)


def _adamw_math(w, g, m, v):
    m2 = ADAM_B1 * m + (1.0 - ADAM_B1) * g
    v2 = ADAM_B2 * v + (1.0 - ADAM_B2) * (g * g)
    m_hat = m2 / (1.0 - ADAM_B1 ** ADAM_STEP)
    v_hat = v2 / (1.0 - ADAM_B2 ** ADAM_STEP)
    delta = -ADAM_LR * (m_hat / (jnp.sqrt(v_hat) + ADAM_EPS) + ADAM_WD * w)
    return delta, m2, v2


def _adamw(parts, w, m, v, *, name):
    R, C = w.shape
    t = _row_tile(R)
    n = len(parts)

    def body(*refs):
        w_ref, m_ref, v_ref = refs[n:n + 3]
        g_ref, d_ref, m2_ref, v2_ref = refs[n + 3:]
        g = refs[0][...]
        for r in refs[1:n]:
            g = g + r[...]
        delta, m2, v2 = _adamw_math(w_ref[...], g, m_ref[...], v_ref[...])
        g_ref[...] = g
        d_ref[...] = delta
        m2_ref[...] = m2
        v2_ref[...] = v2

    blk = pl.BlockSpec((t, C), lambda i: (i, 0))
    return pl.pallas_call(
        body, name=name, grid=(R // t,), in_specs=[blk] * (n + 3), out_specs=[blk] * 4,
        out_shape=[_sds((R, C), F32)] * 4, compiler_params=_params("parallel"))(*parts, w, m, v)


def _sum_devices(gathered):
    _, R, C = gathered.shape

    def body(g_ref, o_ref):
        acc = g_ref[0]
        for k in range(1, N_DEV):
            acc = acc + g_ref[k]
        o_ref[...] = acc

    return pl.pallas_call(body, name="sum_devices", out_shape=_sds((R, C), F32))(gathered)


def _chip_peers():
    x, y = lax.axis_index("x"), lax.axis_index("y")
    return [(1 - x, y), (x, 1 - y), (1 - x, 1 - y)]


def _gather_weights(shards):
    n = len(shards)

    def body(*refs):
        ins, outs = refs[:n], refs[n:2 * n]
        send_sems, recv_sems, loc_sems = refs[2 * n:]
        x, y, c = lax.axis_index("x"), lax.axis_index("y"), lax.axis_index("c")
        me = 2 * x + y
        peers = _chip_peers()
        started = []
        for t in range(n):
            loc = pltpu.make_async_copy(ins[t], outs[t].at[me], loc_sems.at[t])
            loc.start()
            started.append(loc)
        sends = []
        for t in range(n):
            for k, (px, py) in enumerate(peers):
                cp = pltpu.make_async_remote_copy(
                    src_ref=ins[t], dst_ref=outs[t].at[me], send_sem=send_sems.at[t, k],
                    recv_sem=recv_sems.at[t, k], device_id=(px, py, c), device_id_type=MESH)
                cp.start()
                sends.append(cp)
        for t in range(n):
            for k, (px, py) in enumerate(peers):
                pltpu.make_async_remote_copy(
                    src_ref=ins[t], dst_ref=outs[t].at[2 * px + py], send_sem=send_sems.at[t, k],
                    recv_sem=recv_sems.at[t, k], device_id=(px, py, c), device_id_type=MESH).wait_recv()
        for cp in sends:
            cp.wait_send()
        for loc in started:
            loc.wait()

    any_spec = pl.BlockSpec(memory_space=pl.ANY)
    return pl.pallas_call(
        body, name="gather_weights", in_specs=[any_spec] * n, out_specs=[any_spec] * n,
        out_shape=[_sds((N_CHIPS,) + s.shape, s.dtype) for s in shards],
        scratch_shapes=[pltpu.SemaphoreType.DMA((n, 3)), pltpu.SemaphoreType.DMA((n, 3)),
                        pltpu.SemaphoreType.DMA((n,))])(*shards)


def _scatter_grads(grads, small):
    n = len(grads)

    def body(*refs):
        ins, small_ref = refs[:n], refs[n]
        outs, gath_ref = refs[n + 1:2 * n + 1], refs[2 * n + 1]
        send_sems, recv_sems, ssend, srecv, loc_sem = refs[2 * n + 2:]
        x, y, c = lax.axis_index("x"), lax.axis_index("y"), lax.axis_index("c")
        me = 4 * x + 2 * y + c
        peers = _chip_peers()
        flips = [(fx, fy, fc) for fx in (0, 1) for fy in (0, 1) for fc in (0, 1)][1:]

        def flipped(fx, fy, fc):
            return (1 - x if fx else x, 1 - y if fy else y, 1 - c if fc else c)

        loc = pltpu.make_async_copy(small_ref, gath_ref.at[me], loc_sem)
        loc.start()
        sends = []
        for j, (fx, fy, fc) in enumerate(flips):
            cp = pltpu.make_async_remote_copy(
                src_ref=small_ref, dst_ref=gath_ref.at[me], send_sem=ssend.at[j], recv_sem=srecv.at[j],
                device_id=flipped(fx, fy, fc), device_id_type=MESH)
            cp.start()
            sends.append(cp)
        for t in range(n):
            for k, (px, py) in enumerate(peers):
                cp = pltpu.make_async_remote_copy(
                    src_ref=ins[t].at[2 * px + py], dst_ref=outs[t].at[k], send_sem=send_sems.at[t, k],
                    recv_sem=recv_sems.at[t, k], device_id=(px, py, c), device_id_type=MESH)
                cp.start()
                sends.append(cp)
        for j, (fx, fy, fc) in enumerate(flips):
            px, py, pc = flipped(fx, fy, fc)
            pltpu.make_async_remote_copy(
                src_ref=small_ref, dst_ref=gath_ref.at[4 * px + 2 * py + pc], send_sem=ssend.at[j],
                recv_sem=srecv.at[j], device_id=(px, py, pc), device_id_type=MESH).wait_recv()
        for t in range(n):
            for k, (px, py) in enumerate(peers):
                pltpu.make_async_remote_copy(
                    src_ref=ins[t].at[2 * px + py], dst_ref=outs[t].at[k], send_sem=send_sems.at[t, k],
                    recv_sem=recv_sems.at[t, k], device_id=(px, py, c), device_id_type=MESH).wait_recv()
        for cp in sends:
            cp.wait_send()
        loc.wait()

    any_spec = pl.BlockSpec(memory_space=pl.ANY)
    out = pl.pallas_call(
        body, name="scatter_grads", in_specs=[any_spec] * (n + 1), out_specs=[any_spec] * (n + 1),
        out_shape=[_sds((3,) + g.shape[1:], g.dtype) for g in grads] + [_sds((N_DEV,) + small.shape, small.dtype)],
        scratch_shapes=[pltpu.SemaphoreType.DMA((n, 3)), pltpu.SemaphoreType.DMA((n, 3)),
                        pltpu.SemaphoreType.DMA((N_DEV - 1,)), pltpu.SemaphoreType.DMA((N_DEV - 1,)),
                        pltpu.SemaphoreType.DMA])(*grads, small)
    return out[:n], out[n]


def _swap_with_sibling(parts):
    n = len(parts)

    def body(*refs):
        ins, outs = refs[:n], refs[n:2 * n]
        send_sems, recv_sems = refs[2 * n:]
        sib = (lax.axis_index("x"), lax.axis_index("y"), 1 - lax.axis_index("c"))
        cps = []
        for t in range(n):
            cp = pltpu.make_async_remote_copy(
                src_ref=ins[t], dst_ref=outs[t], send_sem=send_sems.at[t], recv_sem=recv_sems.at[t],
                device_id=sib, device_id_type=MESH)
            cp.start()
            cps.append(cp)
        for cp in cps:
            cp.wait()

    any_spec = pl.BlockSpec(memory_space=pl.ANY)
    return pl.pallas_call(
        body, name="swap_with_sibling", in_specs=[any_spec] * n, out_specs=[any_spec] * n,
        out_shape=[_sds(p.shape, p.dtype) for p in parts],
        scratch_shapes=[pltpu.SemaphoreType.DMA((n,)), pltpu.SemaphoreType.DMA((n,))])(*parts)


BIG = ("w_in", "w_out", "w_xq", "w_xk", "w_xv", "w_xo", "w_up", "w_down")
COL_SHARDED = ("w_in", "w_up")


def _as_matrix(name, w4):
    if name in COL_SHARDED:
        return w4
    return w4.reshape(1, w4.shape[0] * w4.shape[1], w4.shape[2])


def _local_step(x, mem, target, W, conv_w32, vecs):
    S = x.shape[0]
    Wm = {k: _as_matrix(k, v) for k, v in W.items()}
    tables = _rope_tables(S)

    xn = _rms_fwd(x, vecs["norm_mix_g"], name="rms_mix")
    y = _mm_nn(xn, Wm["w_in"], name="mm_in", tn=640)
    qk, v_perm = _rope_fwd(y, tables)
    v_src = [(y, 2)] + [(v, 0) for v in v_perm[1:]]
    outs, lses = zip(*[_att_fwd(qk[p], v_src[p], d, name=f"att_fwd_d{d}") for p, d in enumerate(DILATIONS)])
    att, lg = _att_combine(outs, lses)
    cv, u1 = _conv_fwd(y, conv_w32, vecs["conv_b"], vecs["conv_ln_g"], vecs["conv_ln_b"])
    mix = jnp.concatenate([att, cv], axis=1)
    h1 = _mm_nn(mix, Wm["w_out"], name="mm_out", out_dtype=F32, res=x)
    hn = _rms_fwd(h1, vecs["norm_x_g"], name="rms_x")
    xq = _mm_nn(hn, Wm["w_xq"], name="mm_xq")
    mn = _rms_fwd(mem, vecs["norm_mem_g"], name="rms_mem")
    xk = _mm_nn(mn, Wm["w_xk"], name="mm_xk")
    xv = _mm_nn(mn, Wm["w_xv"], name="mm_xv")
    xo = _xatt_fwd(xq, xk, xv)
    h2 = _mm_nn(xo, Wm["w_xo"], name="mm_xo", out_dtype=F32, res=h1)
    hm = _rms_fwd(h2, vecs["norm_mlp_g"], name="rms_mlp")
    relu_up, act = _mm_nn(hm, Wm["w_up"], name="mm_up", relu2=True)
    h3 = _mm_nn(act, Wm["w_down"], name="mm_down", out_dtype=F32, res=h2)

    dh3, dh3b, dg_final, loss = _loss_head(h3, vecs["norm_final_g"], target)
    g = {}
    g["w_down"] = _mm_tn(act, dh3b, 1, name="dw_down")
    dup = _mm_nt(dh3b, Wm["w_down"], name="d_act", out_dtype=BF16, mul=relu_up)
    g["w_up"] = _mm_tn(hm, dup, N_CHIPS, name="dw_up")
    dhm = _mm_nt(dup, Wm["w_up"], name="d_hm")
    dh2, dh2b, dg_mlp = _rms_bwd(dhm, h2, vecs["norm_mlp_g"], dh3, name="rms_bwd_mlp")
    g["w_xo"] = _mm_tn(xo, dh2b, 1, name="dw_xo")
    dxo = _mm_nt(dh2b, Wm["w_xo"], name="d_xo", out_dtype=BF16)
    dxq, dxk, dxv = _xatt_bwd(xq, xk, xv, dxo)
    g["w_xq"] = _mm_tn(hn, dxq, 1, name="dw_xq")
    dhn = _mm_nt(dxq, Wm["w_xq"], name="d_hn")
    dh1, dh1b, dg_x = _rms_bwd(dhn, h1, vecs["norm_x_g"], dh2, name="rms_bwd_x")
    dxkb, dxvb = dxk.astype(BF16), dxv.astype(BF16)
    g["w_xk"] = _mm_tn(mn, dxkb, 1, name="dw_xk")
    g["w_xv"] = _mm_tn(mn, dxvb, 1, name="dw_xv")
    dmn = _mm_nt(jnp.concatenate([dxkb, dxvb], axis=1),
                 jnp.concatenate([Wm["w_xk"], Wm["w_xv"]], axis=2), name="d_mn")
    _, _, dg_mem = _rms_bwd(dmn, mem, vecs["norm_mem_g"], None, name="rms_bwd_mem")
    g["w_out"] = _mm_tn(mix, dh1b, 1, name="dw_out")
    dac = _mm_nt(dh1b, Wm["w_out"], name="d_mix", out_dtype=BF16)
    delta, do_perm = _att_delta(dac, att)
    do_src = [(dac, 0)] + [(t, 0) for t in do_perm[1:]]
    dq = [_att_dq(qk[p], v_src[p], do_src[p], lg[p], delta[p], d, name=f"att_dq_d{d}")
          for p, d in enumerate(DILATIONS)]
    dk, dv = zip(*[_att_dkv(qk[p], v_src[p], do_src[p], lg[p], delta[p], d, name=f"att_dkv_d{d}")
                   for p, d in enumerate(DILATIONS)])
    dag, dconv_w, dconv_small = _conv_bwd(dac, u1, y, conv_w32, vecs["conv_ln_g"], vecs["conv_ln_b"])
    dy = _assemble_dy(dq, dk, dv, dag, tables)
    g["w_in"] = _mm_tn(xn, dy, N_CHIPS, name="dw_in", tn=640)
    dxn = _mm_nt(dy, Wm["w_in"], name="d_xn", tn=640)
    grad_x, _, dg_mix = _rms_bwd(dxn, x, vecs["norm_mix_g"], dh1, name="rms_bwd_mix")

    big = {k: v.reshape(N_CHIPS, v.shape[0] * v.shape[1] // N_CHIPS, v.shape[2]) if k not in COL_SHARDED else v
           for k, v in g.items()}
    small = dict(conv_w=dconv_w, conv_small=dconv_small, norm_mix_g=dg_mix, norm_x_g=dg_x, norm_mem_g=dg_mem,
                 norm_mlp_g=dg_mlp, norm_final_g=dg_final, loss=loss)
    return grad_x, big, small


SMALL_ORDER = ("conv_w", "conv_small", "norm_mix_g", "norm_x_g", "norm_mem_g", "norm_mlp_g", "norm_final_g", "loss")


def _pack_small(small):
    rows, offs, pos = [], {}, 0
    for k in SMALL_ORDER:
        a = small[k]
        a = a.reshape(a.shape[0] * a.shape[1] // SMALL_W, SMALL_W)
        pad = (-a.shape[0]) % 8
        if pad:
            a = jnp.pad(a, ((0, pad), (0, 0)))
        rows.append(a)
        offs[k] = pos
        pos += a.shape[0]
    return jnp.concatenate(rows, axis=0), offs


def kernel(x, mem, norm_mix_g, w_in, conv_w, conv_b, conv_ln_g, conv_ln_b, w_out, norm_x_g, norm_mem_g, w_xq, w_xk, w_xv, w_xo, norm_mlp_g, w_up, w_down, norm_final_g, loss_target, m_norm_mix_g, m_w_in, m_conv_w, m_conv_b, m_conv_ln_g, m_conv_ln_b, m_w_out, m_norm_x_g, m_norm_mem_g, m_w_xq, m_w_xk, m_w_xv, m_w_xo, m_norm_mlp_g, m_w_up, m_w_down, m_norm_final_g, v_norm_mix_g, v_w_in, v_conv_w, v_conv_b, v_conv_ln_g, v_conv_ln_b, v_w_out, v_norm_x_g, v_norm_mem_g, v_w_xq, v_w_xk, v_w_xv, v_w_xo, v_norm_mlp_g, v_w_up, v_w_down, v_norm_final_g):
    names = ("norm_mix_g", "w_in", "conv_w", "conv_b", "conv_ln_g", "conv_ln_b", "w_out", "norm_x_g", "norm_mem_g",
             "w_xq", "w_xk", "w_xv", "w_xo", "norm_mlp_g", "w_up", "w_down", "norm_final_g")
    wts = dict(zip(names, (norm_mix_g, w_in, conv_w, conv_b, conv_ln_g, conv_ln_b, w_out, norm_x_g, norm_mem_g,
                           w_xq, w_xk, w_xv, w_xo, norm_mlp_g, w_up, w_down, norm_final_g)))
    mom = dict(zip(names, (m_norm_mix_g, m_w_in, m_conv_w, m_conv_b, m_conv_ln_g, m_conv_ln_b, m_w_out, m_norm_x_g,
                           m_norm_mem_g, m_w_xq, m_w_xk, m_w_xv, m_w_xo, m_norm_mlp_g, m_w_up, m_w_down, m_norm_final_g)))
    var = dict(zip(names, (v_norm_mix_g, v_w_in, v_conv_w, v_conv_b, v_conv_ln_g, v_conv_ln_b, v_w_out, v_norm_x_g,
                           v_norm_mem_g, v_w_xq, v_w_xk, v_w_xv, v_w_xo, v_norm_mlp_g, v_w_up, v_w_down, v_norm_final_g)))
    chip = 2 * lax.axis_index("x") + lax.axis_index("y")

    conv_w_pad = jnp.pad(wts["conv_w"][0], ((0, 1), (0, 0)))
    shards = [wts[k][0].astype(BF16) for k in BIG] + [conv_w_pad]
    gathered = _gather_weights(shards)
    W = dict(zip(BIG, gathered[:len(BIG)]))
    conv_w32 = jnp.transpose(gathered[-1], (1, 0, 2)).reshape(32, D_CONV)

    vecs = {k: wts[k] for k in ("norm_mix_g", "conv_b", "conv_ln_g", "conv_ln_b", "norm_x_g", "norm_mem_g", "norm_mlp_g")}
    vecs["norm_final_g"] = wts["norm_final_g"].reshape(1, D_MODEL)
    grad_x, big, small = _local_step(x[0], mem[0], loss_target[0], W, conv_w32, vecs)

    packed, offs = _pack_small(small)
    recv, gath = _scatter_grads([big[k] for k in BIG], packed)
    me_arr = jnp.reshape(chip, (1,)).astype(jnp.int32)
    sums = [_sum_partials(big[k], r, me_arr, name=f"sum_{k}") for k, r in zip(BIG, recv)]
    sib = _swap_with_sibling(sums)
    tot_small = _sum_devices(gath)

    res = {}
    for k, s_mine, s_sib in zip(BIG, sums, sib):
        res[k] = _adamw([s_mine, s_sib], wts[k][0], mom[k][0], var[k][0], name=f"adamw_{k}")

    def piece(key, nrows):
        return tot_small[offs[key]:offs[key] + nrows]

    def small_update(k, gfull):
        return _adamw([gfull], wts[k].reshape(gfull.shape), mom[k].reshape(gfull.shape),
                      var[k].reshape(gfull.shape), name=f"adamw_{k}")

    dcw = piece("conv_w", 32)[:CONV_WIDTH]
    dcw_mine = lax.dynamic_slice_in_dim(dcw, chip * (D_CONV // N_CHIPS), D_CONV // N_CHIPS, axis=1)
    res["conv_w"] = small_update("conv_w", dcw_mine)
    cs = piece("conv_small", 8)
    res["conv_b"] = small_update("conv_b", cs[0:1])
    res["conv_ln_g"] = small_update("conv_ln_g", cs[1:2])
    res["conv_ln_b"] = small_update("conv_ln_b", cs[2:3])
    for k in ("norm_mix_g", "norm_x_g", "norm_mem_g", "norm_mlp_g", "norm_final_g"):
        res[k] = small_update(k, piece(k, 8 * D_MODEL // SMALL_W).reshape(8, D_MODEL)[0:1])
    loss = piece("loss", 8)[0, 0]

    outs = [loss, grad_x[None]]
    for j in range(4):
        outs += [res[k][j].reshape(wts[k].shape) for k in names]
    return tuple(outs)
```

```python
import jax
import jax.numpy as jnp
from jax import lax
from jax.experimental import pallas as pl
from jax.experimental.pallas import tpu as pltpu

F32 = jnp.float32
BF16 = jnp.bfloat16
MESH = pl.DeviceIdType.MESH

D_MODEL = 1024
ATT_HEADS = 8
HEAD_DIM = 64
D_ATT = ATT_HEADS * HEAD_DIM
D_CONV = D_MODEL - D_ATT
DILATIONS = (1, 4, 16)
HALF = 64
ROPE_THETA = 500000.0
ROT_DIM = HEAD_DIM // 4
CONV_WIDTH = 31
CONV_PAD = (CONV_WIDTH - 1) // 2
XATT_HEADS = 4
XATT_HEAD_DIM = D_MODEL // XATT_HEADS
D_FF = 4 * D_MODEL
D_IN = 3 * D_ATT + 2 * D_CONV
EPS = 1e-6
NEG_INF = -1e30
N_CHIPS = 4
N_DEV = 8

ADAM_LR = 0.001
ADAM_B1 = 0.9
ADAM_B2 = 0.999
ADAM_EPS = 1e-08
ADAM_WD = 0.01
ADAM_STEP = 10

VMEM_LIMIT_V7X = 56 * 1024 * 1024
LANES = 128
HALO = 16
CONV_ROWS = 64
ATT_BLOCK = 128
SMALL_W = 512


def _params(*sem):
    return pltpu.CompilerParams(dimension_semantics=sem, vmem_limit_bytes=VMEM_LIMIT_V7X)


def _sds(shape, dtype):
    return jax.ShapeDtypeStruct(shape, dtype)


def _mm_nn(a, w3, *, name, out_dtype=BF16, res=None, relu2=False, tm=1024, tn=None, tk=1024):
    M, K = a.shape
    nsh, _, n = w3.shape
    tm, tk = min(tm, M), min(tk, K)
    tn = tn or min(n, 1024)
    npt, nk = n // tn, K // tk
    nj, N = nsh * npt, nsh * n
    n_out = 2 if relu2 else 1

    def body(*refs):
        a_ref, w_ref = refs[0], refs[1]
        pos = 2
        res_ref = None
        if res is not None:
            res_ref = refs[pos]
            pos += 1
        outs = refs[pos:pos + n_out]
        acc_ref = refs[pos + n_out] if nk > 1 else None

        def finish(acc):
            if res_ref is not None:
                acc = acc + res_ref[...]
            if relu2:
                r = jnp.maximum(acc, 0.0)
                outs[0][...] = r.astype(outs[0].dtype)
                outs[1][...] = (r * r).astype(outs[1].dtype)
            else:
                outs[0][...] = acc.astype(outs[0].dtype)

        part = jnp.dot(a_ref[...], w_ref[...], preferred_element_type=F32)
        if nk == 1:
            finish(part)
        else:
            k = pl.program_id(2)

            @pl.when(k == 0)
            def _():
                acc_ref[...] = part

            @pl.when(k > 0)
            def _():
                acc_ref[...] += part

            @pl.when(k == nk - 1)
            def _():
                finish(acc_ref[...])

    in_specs = [pl.BlockSpec((tm, tk), lambda i, j, k: (i, k)),
                pl.BlockSpec((None, tk, tn), lambda i, j, k: (j // npt, k, j % npt))]
    args = [a, w3]
    if res is not None:
        in_specs.append(pl.BlockSpec((tm, tn), lambda i, j, k: (i, j)))
        args.append(res)
    out_spec = pl.BlockSpec((tm, tn), lambda i, j, k: (i, j))
    out = pl.pallas_call(
        body, name=name, grid=(M // tm, nj, nk), in_specs=in_specs,
        out_specs=[out_spec] * n_out, out_shape=[_sds((M, N), out_dtype)] * n_out,
        scratch_shapes=[pltpu.VMEM((tm, tn), F32)] if nk > 1 else [],
        compiler_params=_params("parallel", "parallel", "arbitrary"))(*args)
    return tuple(out) if relu2 else out[0]


def _mm_nt(dy, w3, *, name, out_dtype=F32, mul=None, tm=1024, tn=None, tko=1024):
    M, N = dy.shape
    nsh, K, n = w3.shape
    tm, tko = min(tm, M), min(tko, K)
    tn = tn or min(n, 1024)
    npt = n // tn
    nj = nsh * npt

    def body(*refs):
        dy_ref, w_ref = refs[0], refs[1]
        pos = 2
        mul_ref = None
        if mul is not None:
            mul_ref = refs[pos]
            pos += 1
        out_ref = refs[pos]
        acc_ref = refs[pos + 1] if nj > 1 else None

        def finish(acc):
            if mul_ref is not None:
                acc = acc * (2.0 * mul_ref[...].astype(F32))
            out_ref[...] = acc.astype(out_ref.dtype)

        part = lax.dot_general(dy_ref[...], w_ref[...], (((1,), (1,)), ((), ())), preferred_element_type=F32)
        if nj == 1:
            finish(part)
        else:
            j = pl.program_id(2)

            @pl.when(j == 0)
            def _():
                acc_ref[...] = part

            @pl.when(j > 0)
            def _():
                acc_ref[...] += part

            @pl.when(j == nj - 1)
            def _():
                finish(acc_ref[...])

    in_specs = [pl.BlockSpec((tm, tn), lambda i, ko, j: (i, j)),
                pl.BlockSpec((None, tko, tn), lambda i, ko, j: (j // npt, ko, j % npt))]
    args = [dy, w3]
    if mul is not None:
        in_specs.append(pl.BlockSpec((tm, tko), lambda i, ko, j: (i, ko)))
        args.append(mul)
    return pl.pallas_call(
        body, name=name, grid=(M // tm, K // tko, nj), in_specs=in_specs,
        out_specs=pl.BlockSpec((tm, tko), lambda i, ko, j: (i, ko)), out_shape=_sds((M, K), out_dtype),
        scratch_shapes=[pltpu.VMEM((tm, tko), F32)] if nj > 1 else [],
        compiler_params=_params("parallel", "parallel", "arbitrary"))(*args)


def _mm_tn(a, dy, nsh, *, name, out_dtype=BF16, tm=1024, tk=1024, tn=None):
    M, K = a.shape
    N = dy.shape[1]
    n = N // nsh
    tm, tk = min(tm, M), min(tk, K)
    tn = tn or min(n, 1024)
    npt = n // tn
    nj, nm = nsh * npt, M // tm

    def body(a_ref, dy_ref, out_ref, acc_ref):
        m = pl.program_id(2)
        part = lax.dot_general(a_ref[...], dy_ref[...], (((0,), (0,)), ((), ())), preferred_element_type=F32)

        @pl.when(m == 0)
        def _():
            acc_ref[...] = part

        @pl.when(m > 0)
        def _():
            acc_ref[...] += part

        @pl.when(m == nm - 1)
        def _():
            out_ref[...] = acc_ref[...].astype(out_ref.dtype)

    return pl.pallas_call(
        body, name=name, grid=(K // tk, nj, nm),
        in_specs=[pl.BlockSpec((tm, tk), lambda kk, j, m: (m, kk)),
                  pl.BlockSpec((tm, tn), lambda kk, j, m: (m, j))],
        out_specs=pl.BlockSpec((None, tk, tn), lambda kk, j, m: (j // npt, kk, j % npt)),
        out_shape=_sds((nsh, K, n), out_dtype),
        scratch_shapes=[pltpu.VMEM((tk, tn), F32)],
        compiler_params=_params("parallel", "parallel", "arbitrary"))(a, dy)


def _rms_fwd(x, g, *, name, tm=512):
    M, Dm = x.shape
    tm = min(tm, M)

    def body(x_ref, g_ref, o_ref):
        xf = x_ref[...]
        r = lax.rsqrt(jnp.mean(xf * xf, axis=-1, keepdims=True) + EPS)
        o_ref[...] = (xf * r * g_ref[...]).astype(o_ref.dtype)

    return pl.pallas_call(
        body, name=name, grid=(M // tm,),
        in_specs=[pl.BlockSpec((tm, Dm), lambda i: (i, 0)), pl.BlockSpec((1, Dm), lambda i: (0, 0))],
        out_specs=pl.BlockSpec((tm, Dm), lambda i: (i, 0)), out_shape=_sds((M, Dm), BF16),
        compiler_params=_params("parallel"))(x, g)


def _rms_bwd(dxn, x, g, dres, *, name, tm=512):
    M, Dm = x.shape
    tm = min(tm, M)
    has_res = dres is not None

    def body(*refs):
        dxn_ref, x_ref, g_ref = refs[:3]
        dres_ref = refs[3] if has_res else None
        dx_ref, dxb_ref, dg_ref = refs[-3:]
        i = pl.program_id(0)
        xf = x_ref[...]
        r = lax.rsqrt(jnp.mean(xf * xf, axis=-1, keepdims=True) + EPS)
        nrm = xf * r
        dxn_f = dxn_ref[...].astype(F32)
        dn = dxn_f * g_ref[...]
        dx = r * (dn - nrm * jnp.mean(dn * nrm, axis=-1, keepdims=True))
        if has_res:
            dx = dx + dres_ref[...]
        dx_ref[...] = dx
        dxb_ref[...] = dx.astype(dxb_ref.dtype)

        @pl.when(i == 0)
        def _():
            dg_ref[...] = jnp.zeros_like(dg_ref)

        dg_ref[0:1, :] += jnp.sum(dxn_f * nrm, axis=0, keepdims=True)

    row = pl.BlockSpec((tm, Dm), lambda i: (i, 0))
    in_specs = [row, row, pl.BlockSpec((1, Dm), lambda i: (0, 0))] + ([row] if has_res else [])
    args = [dxn, x, g] + ([dres] if has_res else [])
    return pl.pallas_call(
        body, name=name, grid=(M // tm,), in_specs=in_specs,
        out_specs=[row, row, pl.BlockSpec((8, Dm), lambda i: (0, 0))],
        out_shape=[_sds((M, Dm), F32), _sds((M, Dm), BF16), _sds((8, Dm), F32)],
        compiler_params=_params("arbitrary"))(*args)


def _loss_head(h, g, target, *, tm=512):
    M, Dm = h.shape
    tm = min(tm, M)

    def body(h_ref, g_ref, t_ref, dh_ref, dhb_ref, dg_ref, loss_ref):
        i = pl.program_id(0)
        hf = h_ref[...]
        r = lax.rsqrt(jnp.mean(hf * hf, axis=-1, keepdims=True) + EPS)
        nrm = hf * r
        gv = g_ref[...]
        err = nrm * gv - t_ref[...]
        dy = err * (1.0 / Dm)
        dn = dy * gv
        dh = r * (dn - nrm * jnp.mean(dn * nrm, axis=-1, keepdims=True))
        dh_ref[...] = dh
        dhb_ref[...] = dh.astype(dhb_ref.dtype)

        @pl.when(i == 0)
        def _():
            dg_ref[...] = jnp.zeros_like(dg_ref)
            loss_ref[...] = jnp.zeros_like(loss_ref)

        dg_ref[0:1, :] += jnp.sum(dy * nrm, axis=0, keepdims=True)
        part = 0.5 * jnp.sum(jnp.mean(err * err, axis=-1, keepdims=True), axis=0, keepdims=True)
        sel = (lax.broadcasted_iota(jnp.int32, (8, 128), 0) == 0) & (lax.broadcasted_iota(jnp.int32, (8, 128), 1) == 0)
        loss_ref[...] += jnp.where(sel, part, 0.0)

    row = pl.BlockSpec((tm, Dm), lambda i: (i, 0))
    return pl.pallas_call(
        body, name="loss_head", grid=(M // tm,),
        in_specs=[row, pl.BlockSpec((1, Dm), lambda i: (0, 0)), row],
        out_specs=[row, row, pl.BlockSpec((8, Dm), lambda i: (0, 0)), pl.BlockSpec((8, 128), lambda i: (0, 0))],
        out_shape=[_sds((M, Dm), F32), _sds((M, Dm), BF16), _sds((8, Dm), F32), _sds((8, 128), F32)],
        compiler_params=_params("arbitrary"))(h, g, target)


def _class_spec(tm, d, width):
    return pl.BlockSpec((d, tm // d, width), lambda i: (0, i, 0))


def _row_scratch(tm, width):
    return pltpu.VMEM((width // LANES, tm, LANES), F32)


def _fill(scr, val):
    for c in range(scr.shape[0]):
        scr[c] = val[:, c * LANES:(c + 1) * LANES]


def _to_classes(scr, out_ref, d):
    n = scr.shape[1] // d
    for r in range(d):
        for c in range(scr.shape[0]):
            out_ref[r, :, c * LANES:(c + 1) * LANES] = scr[c, pl.ds(r, n, stride=d), :].astype(out_ref.dtype)


def _from_classes(in_ref, scr, d):
    n = scr.shape[1] // d
    for r in range(d):
        blk = in_ref[r].astype(F32)
        for c in range(scr.shape[0]):
            scr[c, pl.ds(r, n, stride=d), :] = blk[:, c * LANES:(c + 1) * LANES]
    return jnp.concatenate([scr[c] for c in range(scr.shape[0])], axis=1)


def _rope_tables(S):
    half = ROT_DIM // 2
    freqs = ROPE_THETA ** (-jnp.arange(0, ROT_DIM, 2, dtype=F32) / ROT_DIM)
    ang = jnp.arange(S, dtype=F32)[:, None] * freqs[None, :]
    cos, sin = jnp.cos(ang), jnp.sin(ang)
    ones = jnp.ones((S, HEAD_DIM - ROT_DIM), F32)
    zeros = jnp.zeros((S, HEAD_DIM - ROT_DIM), F32)
    zh = jnp.zeros((S, half), F32)
    c = jnp.concatenate([cos, cos, ones], axis=1)
    sa = jnp.concatenate([-sin, zh, zeros], axis=1)
    sb = jnp.concatenate([zh, sin, zeros], axis=1)
    return tuple(jnp.tile(t, (1, LANES // HEAD_DIM)) for t in (c, sa, sb))


def _rope_fwd(y, tables, *, tm=512):
    S = y.shape[0]
    W = 2 * D_ATT
    tm = min(tm, S)
    half = ROT_DIM // 2
    dils = [d for d in DILATIONS if d > 1]

    def body(y_ref, c_ref, sa_ref, sb_ref, qk_ref, *rest):
        qk_outs, v_outs = rest[:len(dils)], rest[len(dils):2 * len(dils)]
        scr_qk, scr_v = rest[2 * len(dils):]
        t = y_ref[:, 0:W].astype(F32)
        rep = W // LANES
        c, sa, sb = (jnp.tile(r[...], (1, rep)) for r in (c_ref, sa_ref, sb_ref))
        rot = t * c + pltpu.roll(t, W - half, axis=1) * sa + pltpu.roll(t, half, axis=1) * sb
        qk_ref[...] = rot.astype(qk_ref.dtype)
        _fill(scr_qk, rot)
        _fill(scr_v, y_ref[:, W:W + D_ATT].astype(F32))
        for d, qo, vo in zip(dils, qk_outs, v_outs):
            _to_classes(scr_qk, qo, d)
            _to_classes(scr_v, vo, d)

    tab = pl.BlockSpec((tm, LANES), lambda i: (i, 0))
    out = pl.pallas_call(
        body, name="rope_fwd", grid=(S // tm,),
        in_specs=[pl.BlockSpec((tm, 3 * D_ATT), lambda i: (i, 0)), tab, tab, tab],
        out_specs=[pl.BlockSpec((tm, W), lambda i: (i, 0))] + [_class_spec(tm, d, W) for d in dils]
        + [_class_spec(tm, d, D_ATT) for d in dils],
        out_shape=[_sds((S, W), BF16)] + [_sds((d, S // d, W), BF16) for d in dils]
        + [_sds((d, S // d, D_ATT), BF16) for d in dils],
        scratch_shapes=[_row_scratch(tm, W), _row_scratch(tm, D_ATT)],
        compiler_params=_params("parallel"))(y, *tables)
    qk = [out[0]] + [o.reshape(S, W) for o in out[1:1 + len(dils)]]
    v = [None] + [o.reshape(S, D_ATT) for o in out[1 + len(dils):]]
    return qk, v


def _assemble_dy(dq, dk, dv, dag, tables, *, tm=512):
    S = dag.shape[0]
    tm = min(tm, S)
    half = ROT_DIM // 2
    W = D_ATT
    n_pat = len(DILATIONS)

    def body(*refs):
        groups = [refs[g * n_pat:(g + 1) * n_pat] for g in range(3)]
        dag_ref, c_ref, sa_ref, sb_ref, o_ref, scr = refs[3 * n_pat:]
        rep = W // LANES
        c, sa, sb = (jnp.tile(r[...], (1, rep)) for r in (c_ref, sa_ref, sb_ref))

        def total(rs):
            acc = rs[0][...].astype(F32)
            for d, r in zip(DILATIONS[1:], rs[1:]):
                acc = acc + _from_classes(r, scr, d)
            return acc

        def unrope(dr):
            return dr * c + pltpu.roll(dr * sa, half, axis=1) + pltpu.roll(dr * sb, W - half, axis=1)

        o_ref[:, 0:W] = unrope(total(groups[0])).astype(o_ref.dtype)
        o_ref[:, W:2 * W] = unrope(total(groups[1])).astype(o_ref.dtype)
        o_ref[:, 2 * W:3 * W] = total(groups[2]).astype(o_ref.dtype)
        o_ref[:, 3 * W:] = dag_ref[...]

    specs = [pl.BlockSpec((tm, W), lambda i: (i, 0))] + [_class_spec(tm, d, W) for d in DILATIONS[1:]]
    tab = pl.BlockSpec((tm, LANES), lambda i: (i, 0))
    args = [a if d == 1 else a.reshape(d, S // d, W) for grp in (dq, dk, dv) for d, a in zip(DILATIONS, grp)]
    return pl.pallas_call(
        body, name="assemble_dy", grid=(S // tm,),
        in_specs=specs * 3 + [pl.BlockSpec((tm, 2 * D_CONV), lambda i: (i, 0)), tab, tab, tab],
        out_specs=pl.BlockSpec((tm, D_IN), lambda i: (i, 0)), out_shape=_sds((S, D_IN), BF16),
        scratch_shapes=[_row_scratch(tm, W)],
        compiler_params=_params("parallel"))(*args, dag, *tables)


def _seq_specs(L, tb, col):
    nb, per, nh = L // tb, tb // HALF, L // HALF
    centre = pl.BlockSpec((tb, D_ATT), lambda r, i: (r * nb + i, col))
    prev = pl.BlockSpec((HALF, D_ATT), lambda r, i: (r * nh + jnp.maximum(i * per - 1, 0), col))
    nxt = pl.BlockSpec((HALF, D_ATT), lambda r, i: (r * nh + jnp.minimum((i + 1) * per, nh - 1), col))
    return prev, centre, nxt


def _band_mask(i, tq, L, centre_is_query):
    if centre_is_query:
        shape = (tq, tq + 2 * HALF)
        c_idx = lax.broadcasted_iota(jnp.int32, shape, 0)
        w_idx = lax.broadcasted_iota(jnp.int32, shape, 1)
    else:
        shape = (tq + 2 * HALF, tq)
        w_idx = lax.broadcasted_iota(jnp.int32, shape, 0)
        c_idx = lax.broadcasted_iota(jnp.int32, shape, 1)
    diff = w_idx - c_idx
    wpos = i * tq - HALF + w_idx
    return (diff >= 0) & (diff <= 2 * HALF) & (wpos >= 0) & (wpos < L)


def _lane_groups():
    for c0 in range(0, D_ATT, LANES):
        yield slice(c0, c0 + LANES)


def _first_head(rows):
    return lax.broadcasted_iota(jnp.int32, (rows, LANES), 1) < HEAD_DIM


def _split_pair(x, first):
    zero = jnp.zeros_like(x)
    return jnp.where(first, x, zero), jnp.where(first, zero, x)


def _nt(a, b):
    return lax.dot_general(a, b, (((1,), (1,)), ((), ())), preferred_element_type=F32)


def _tn(a, b):
    return lax.dot_general(a, b, (((0,), (0,)), ((), ())), preferred_element_type=F32)


ATT_SCALE = HEAD_DIM ** -0.5


def _att_fwd(qk, v_src, d, *, name):
    S = qk.shape[0]
    L = S // d
    tq = min(ATT_BLOCK, L)
    v_arr, v_col = v_src

    def body(q_ref, kp_ref, kc_ref, kn_ref, vp_ref, vc_ref, vn_ref, o_ref, lse_ref):
        i = pl.program_id(1)
        valid = _band_mask(i, tq, L, True)
        q = q_ref[...] * ATT_SCALE
        kwin = jnp.concatenate([kp_ref[...], kc_ref[...], kn_ref[...]], axis=0)
        vwin = jnp.concatenate([vp_ref[...], vc_ref[...], vn_ref[...]], axis=0)
        first = _first_head(tq)
        for ls in _lane_groups():
            k2, v2 = kwin[:, ls], vwin[:, ls]
            s = [jnp.where(valid, _nt(t, k2), NEG_INF) for t in _split_pair(q[:, ls], first)]
            m = [jnp.max(t, axis=-1, keepdims=True) for t in s]
            p = [jnp.exp(t - mm) for t, mm in zip(s, m)]
            den = [jnp.sum(t, axis=-1, keepdims=True) for t in p]
            o = [jnp.dot(t.astype(BF16), v2, preferred_element_type=F32) * (1.0 / dd) for t, dd in zip(p, den)]
            lse = [mm + jnp.log(dd) for mm, dd in zip(m, den)]
            o_ref[:, ls] = jnp.where(first, o[0], o[1])
            lse_ref[:, ls] = jnp.where(first, lse[0], lse[1])

    _, qc, _ = _seq_specs(L, tq, 0)
    kp, kc, kn = _seq_specs(L, tq, 1)
    vp, vc, vn = _seq_specs(L, tq, v_col)
    out = pl.BlockSpec((tq, D_ATT), lambda r, i: (r * (L // tq) + i, 0))
    return pl.pallas_call(
        body, name=name, grid=(d, L // tq),
        in_specs=[qc, kp, kc, kn, vp, vc, vn], out_specs=[out, out],
        out_shape=[_sds((S, D_ATT), F32)] * 2,
        compiler_params=_params("parallel", "parallel"))(qk, qk, qk, qk, v_arr, v_arr, v_arr)


def _att_combine(outs, lses, *, tm=512):
    S = outs[0].shape[0]
    tm = min(tm, S)
    dils = DILATIONS[1:]
    n_d = len(dils)

    def body(*refs):
        o_refs, l_refs = refs[0:1 + n_d], refs[1 + n_d:2 + 2 * n_d]
        att_ref, lg_ref = refs[2 + 2 * n_d:4 + 2 * n_d]
        lg_outs = refs[4 + 2 * n_d:4 + 3 * n_d]
        scr = refs[4 + 3 * n_d:]
        scr_o, scr_l, scr_lg = scr[:n_d], scr[n_d:2 * n_d], scr[2 * n_d]
        ls = [l_refs[0][...]] + [_from_classes(r, s, d) for r, s, d in zip(l_refs[1:], scr_l, dils)]
        os_ = [o_refs[0][...]] + [_from_classes(r, s, d) for r, s, d in zip(o_refs[1:], scr_o, dils)]
        mx = ls[0]
        for l in ls[1:]:
            mx = jnp.maximum(mx, l)
        es = [jnp.exp(l - mx) for l in ls]
        tot = es[0]
        num = es[0] * os_[0]
        for e, o in zip(es[1:], os_[1:]):
            tot = tot + e
            num = num + e * o
        att_ref[...] = (num / tot).astype(att_ref.dtype)
        lg = mx + jnp.log(tot)
        lg_ref[...] = lg
        _fill(scr_lg, lg)
        for d, out in zip(dils, lg_outs):
            _to_classes(scr_lg, out, d)

    nat = pl.BlockSpec((tm, D_ATT), lambda i: (i, 0))
    specs = [nat] + [_class_spec(tm, d, D_ATT) for d in dils]
    view = lambda arrs: [arrs[0]] + [a.reshape(d, S // d, D_ATT) for a, d in zip(arrs[1:], dils)]
    out = pl.pallas_call(
        body, name="att_combine", grid=(S // tm,), in_specs=specs * 2,
        out_specs=[nat, nat] + specs[1:],
        out_shape=[_sds((S, D_ATT), BF16), _sds((S, D_ATT), F32)] + [_sds((d, S // d, D_ATT), F32) for d in dils],
        scratch_shapes=[_row_scratch(tm, D_ATT)] * (2 * n_d + 1),
        compiler_params=_params("parallel"))(*view(list(outs)), *view(list(lses)))
    return out[0], [out[1]] + [o.reshape(S, D_ATT) for o in out[2:]]


def _att_delta(dac, att, *, tm=512):
    S = att.shape[0]
    tm = min(tm, S)
    dils = DILATIONS[1:]
    n_d = len(dils)

    def body(do_ref, o_ref, dl_ref, *rest):
        dl_outs, do_outs = rest[:n_d], rest[n_d:2 * n_d]
        scr_dl, scr_do = rest[2 * n_d:]
        do = do_ref[...].astype(F32)
        prod = do * o_ref[...].astype(F32)
        per_head = [jnp.broadcast_to(jnp.sum(prod[:, h * HEAD_DIM:(h + 1) * HEAD_DIM], axis=-1, keepdims=True),
                                     (tm, HEAD_DIM)) for h in range(ATT_HEADS)]
        dl = jnp.concatenate(per_head, axis=1)
        dl_ref[...] = dl
        _fill(scr_dl, dl)
        _fill(scr_do, do)
        for d, dlo, doo in zip(dils, dl_outs, do_outs):
            _to_classes(scr_dl, dlo, d)
            _to_classes(scr_do, doo, d)

    blk = pl.BlockSpec((tm, D_ATT), lambda i: (i, 0))
    out = pl.pallas_call(
        body, name="att_delta", grid=(S // tm,), in_specs=[blk, blk],
        out_specs=[blk] + [_class_spec(tm, d, D_ATT) for d in dils] * 2,
        out_shape=[_sds((S, D_ATT), F32)] + [_sds((d, S // d, D_ATT), F32) for d in dils]
        + [_sds((d, S // d, D_ATT), BF16) for d in dils],
        scratch_shapes=[_row_scratch(tm, D_ATT), _row_scratch(tm, D_ATT)],
        compiler_params=_params("parallel"))(dac, att)
    delta = [out[0]] + [o.reshape(S, D_ATT) for o in out[1:1 + n_d]]
    do = [None] + [o.reshape(S, D_ATT) for o in out[1 + n_d:]]
    return delta, do


def _att_dq(qk, v_src, do_src, lg, delta, d, *, name):
    S = qk.shape[0]
    L = S // d
    tq = min(ATT_BLOCK, L)
    (v_arr, v_col), (do_arr, do_col) = v_src, do_src

    def body(q_ref, kp_ref, kc_ref, kn_ref, vp_ref, vc_ref, vn_ref, do_ref, lg_ref, dl_ref, dq_ref):
        i = pl.program_id(1)
        valid = _band_mask(i, tq, L, True)
        q, do = q_ref[...] * ATT_SCALE, do_ref[...]
        kwin = jnp.concatenate([kp_ref[...], kc_ref[...], kn_ref[...]], axis=0)
        vwin = jnp.concatenate([vp_ref[...], vc_ref[...], vn_ref[...]], axis=0)
        first = _first_head(tq)
        for ls in _lane_groups():
            k2, v2 = kwin[:, ls], vwin[:, ls]
            cols = (ls.start, ls.start + HEAD_DIM)
            s = [jnp.where(valid, _nt(t, k2), NEG_INF) for t in _split_pair(q[:, ls], first)]
            p = [jnp.exp(t - lg_ref[:, c:c + 1]) for t, c in zip(s, cols)]
            dp = [_nt(t, v2) for t in _split_pair(do[:, ls], first)]
            ds = [pp * (t - dl_ref[:, c:c + 1]) for pp, t, c in zip(p, dp, cols)]
            dq = [jnp.dot(t.astype(BF16), k2, preferred_element_type=F32) for t in ds]
            dq_ref[:, ls] = (jnp.where(first, dq[0], dq[1]) * ATT_SCALE).astype(dq_ref.dtype)

    _, qc, _ = _seq_specs(L, tq, 0)
    kp, kc, kn = _seq_specs(L, tq, 1)
    vp, vc, vn = _seq_specs(L, tq, v_col)
    _, doc, _ = _seq_specs(L, tq, do_col)
    row = pl.BlockSpec((tq, D_ATT), lambda r, i: (r * (L // tq) + i, 0))
    return pl.pallas_call(
        body, name=name, grid=(d, L // tq),
        in_specs=[qc, kp, kc, kn, vp, vc, vn, doc, row, row], out_specs=row,
        out_shape=_sds((S, D_ATT), BF16),
        compiler_params=_params("parallel", "parallel"))(qk, qk, qk, qk, v_arr, v_arr, v_arr, do_arr, lg, delta)


def _att_dkv(qk, v_src, do_src, lg, delta, d, *, name):
    S = qk.shape[0]
    L = S // d
    tk = min(ATT_BLOCK, L)
    (v_arr, v_col), (do_arr, do_col) = v_src, do_src

    def body(k_ref, v_ref, qp_ref, qc_ref, qn_ref, dop_ref, doc_ref, don_ref,
             lgp_ref, lgc_ref, lgn_ref, dlp_ref, dlc_ref, dln_ref, dk_ref, dv_ref):
        i = pl.program_id(1)
        valid = _band_mask(i, tk, L, False)
        k, v = k_ref[...], v_ref[...]
        qwin = jnp.concatenate([qp_ref[...], qc_ref[...], qn_ref[...]], axis=0) * ATT_SCALE
        dowin = jnp.concatenate([dop_ref[...], doc_ref[...], don_ref[...]], axis=0)

        def column(refs, c0):
            return jnp.concatenate([r[:, c0:c0 + 1] for r in refs], axis=0)

        first_w, first_k = _first_head(tk + 2 * HALF), _first_head(tk)
        for ls in _lane_groups():
            k2, v2, q2, do2 = k[:, ls], v[:, ls], qwin[:, ls], dowin[:, ls]
            cols = (ls.start, ls.start + HEAD_DIM)
            lgw = [column((lgp_ref, lgc_ref, lgn_ref), c) for c in cols]
            dlw = [column((dlp_ref, dlc_ref, dln_ref), c) for c in cols]
            s = [jnp.where(valid, _nt(t, k2), NEG_INF) for t in _split_pair(q2, first_w)]
            p = [jnp.exp(t - l) for t, l in zip(s, lgw)]
            dp = [_nt(t, v2) for t in _split_pair(do2, first_w)]
            ds = [pp * (t - l) for pp, t, l in zip(p, dp, dlw)]
            dv = [_tn(pp.astype(BF16), do2) for pp in p]
            dk = [_tn(t.astype(BF16), q2) for t in ds]
            dk_ref[:, ls] = jnp.where(first_k, dk[0], dk[1]).astype(dk_ref.dtype)
            dv_ref[:, ls] = jnp.where(first_k, dv[0], dv[1]).astype(dv_ref.dtype)

    _, kc, _ = _seq_specs(L, tk, 1)
    _, vc, _ = _seq_specs(L, tk, v_col)
    qp, qc, qn = _seq_specs(L, tk, 0)
    dop, doc, don = _seq_specs(L, tk, do_col)
    rp, rc, rn = _seq_specs(L, tk, 0)
    out = pl.BlockSpec((tk, D_ATT), lambda r, i: (r * (L // tk) + i, 0))
    return pl.pallas_call(
        body, name=name, grid=(d, L // tk),
        in_specs=[kc, vc, qp, qc, qn, dop, doc, don, rp, rc, rn, rp, rc, rn], out_specs=[out, out],
        out_shape=[_sds((S, D_ATT), BF16)] * 2,
        compiler_params=_params("parallel", "parallel"))(
            qk, v_arr, qk, qk, qk, do_arr, do_arr, do_arr, lg, lg, lg, delta, delta, delta)


def _sigmoid(x):
    return 1.0 / (1.0 + jnp.exp(-x))


def _halo_specs(S, T, width, col):
    last = S // HALO - 1
    per = T // HALO
    centre = pl.BlockSpec((T, width), lambda i: (i, col))
    prev = pl.BlockSpec((HALO, width), lambda i: (jnp.maximum(i * per - 1, 0), col))
    nxt = pl.BlockSpec((HALO, width), lambda i: (jnp.minimum((i + 1) * per, last), col))
    return prev, centre, nxt


def _window_scratch(T, C):
    return pltpu.VMEM((8, T + 2 * HALO, C), F32)


def _fill_window(buf, prev, centre, nxt, T):
    buf[0, 0:HALO, :] = prev
    buf[0, HALO:HALO + T, :] = centre
    buf[0, HALO + T:, :] = nxt
    rows = T + 2 * HALO - 8
    for s in range(1, 8):
        buf[s, 0:rows, :] = buf[0, s:s + rows, :]


def _window_rows(buf, off, r0, ls):
    s = off % 8
    return buf[s, pl.ds(off - s + r0, CONV_ROWS), ls]


def _depthwise(buf, w_ref, out_ref, T, C, first_off, step):
    def row_tile(t, carry):
        r0 = pl.multiple_of(t * CONV_ROWS, CONV_ROWS)
        for c0 in range(0, C, LANES):
            ls = slice(c0, c0 + LANES)
            acc = jnp.zeros((CONV_ROWS, LANES), F32)
            for k in range(CONV_WIDTH):
                acc = acc + _window_rows(buf, first_off + step * k, r0, ls) * w_ref[k:k + 1, ls]
            out_ref[pl.ds(r0, CONV_ROWS), ls] = acc
        return carry

    lax.fori_loop(0, T // CONV_ROWS, row_tile, 0)


def _conv_fwd(y, conv_w32, conv_b, ln_g, ln_b, *, T=512):
    S = y.shape[0]
    T = min(T, S)
    nblk = S // T
    C = D_CONV

    def body(ap, ac, an, gp, gc, gn, w_ref, b_ref, lg_ref, lb_ref, cv_ref, u1_ref, buf):
        i = pl.program_id(0)

        def glu(a_ref, g_ref):
            return a_ref[...].astype(F32) * _sigmoid(g_ref[...].astype(F32))

        _fill_window(buf, jnp.where(i > 0, glu(ap, gp), 0.0), glu(ac, gc),
                     jnp.where(i < nblk - 1, glu(an, gn), 0.0), T)
        _depthwise(buf, w_ref, u1_ref, T, C, HALO - CONV_PAD, 1)
        u1 = u1_ref[...] + b_ref[...]
        u1_ref[...] = u1
        mu = jnp.mean(u1, axis=-1, keepdims=True)
        xc = u1 - mu
        rstd = lax.rsqrt(jnp.mean(xc * xc, axis=-1, keepdims=True) + EPS)
        u2 = xc * rstd * lg_ref[...] + lb_ref[...]
        cv_ref[...] = (u2 * _sigmoid(u2)).astype(cv_ref.dtype)

    ap, ac, an = _halo_specs(S, T, C, 3)
    gp, gc, gn = _halo_specs(S, T, C, 4)
    vec = pl.BlockSpec((1, C), lambda i: (0, 0))
    out = pl.BlockSpec((T, C), lambda i: (i, 0))
    return pl.pallas_call(
        body, name="conv_fwd", grid=(nblk,),
        in_specs=[ap, ac, an, gp, gc, gn, pl.BlockSpec((32, C), lambda i: (0, 0)), vec, vec, vec],
        out_specs=[out, out], out_shape=[_sds((S, C), BF16), _sds((S, C), F32)],
        scratch_shapes=[_window_scratch(T, C)],
        compiler_params=_params("parallel"))(y, y, y, y, y, y, conv_w32, conv_b, ln_g, ln_b)


def _conv_bwd(dac, u1, y, conv_w32, ln_g, ln_b, *, T=512):
    S = y.shape[0]
    T = min(T, S)
    nblk = S // T
    C = D_CONV

    def body(dp, dc, dn, up, uc, un, ap, ac, an, gp, gc, gn, w_ref, lg_ref, lb_ref,
             dag_ref, dw_ref, dsm_ref, bufd, bufu, du0_scr, dw_acc):
        i = pl.program_id(0)
        lg = lg_ref[...]

        def du1_of(dcv_ref, u1_ref):
            u1 = u1_ref[...]
            mu = jnp.mean(u1, axis=-1, keepdims=True)
            xc = u1 - mu
            rstd = lax.rsqrt(jnp.mean(xc * xc, axis=-1, keepdims=True) + EPS)
            xhat = xc * rstd
            u2 = xhat * lg + lb_ref[...]
            sg = _sigmoid(u2)
            du2 = dcv_ref[...].astype(F32) * (sg * (1.0 + u2 * (1.0 - sg)))
            dxh = du2 * lg
            du1 = rstd * (dxh - jnp.mean(dxh, axis=-1, keepdims=True)
                          - xhat * jnp.mean(dxh * xhat, axis=-1, keepdims=True))
            return du1, du2, xhat

        def glu(a_ref, g_ref):
            return a_ref[...].astype(F32) * _sigmoid(g_ref[...].astype(F32))

        @pl.when(i == 0)
        def _():
            dw_ref[...] = jnp.zeros_like(dw_ref)
            dsm_ref[...] = jnp.zeros_like(dsm_ref)

        du1_c, du2_c, xhat_c = du1_of(dc, uc)
        dsm_ref[0:1, :] += jnp.sum(du1_c, axis=0, keepdims=True)
        dsm_ref[1:2, :] += jnp.sum(du2_c * xhat_c, axis=0, keepdims=True)
        dsm_ref[2:3, :] += jnp.sum(du2_c, axis=0, keepdims=True)
        _fill_window(bufd, jnp.where(i > 0, du1_of(dp, up)[0], 0.0), du1_c,
                     jnp.where(i < nblk - 1, du1_of(dn, un)[0], 0.0), T)
        _fill_window(bufu, jnp.where(i > 0, glu(ap, gp), 0.0), glu(ac, gc),
                     jnp.where(i < nblk - 1, glu(an, gn), 0.0), T)

        _depthwise(bufd, w_ref, du0_scr, T, C, HALO + CONV_PAD, -1)
        dw_acc[...] = jnp.zeros_like(dw_acc)

        def dw_tile(t, carry):
            r0 = pl.multiple_of(t * CONV_ROWS, CONV_ROWS)
            for c0 in range(0, C, LANES):
                ls = slice(c0, c0 + LANES)
                d = _window_rows(bufd, HALO, r0, ls)
                for k in range(CONV_WIDTH):
                    prod = d * _window_rows(bufu, HALO - CONV_PAD + k, r0, ls)
                    part = prod[0:8]
                    for j in range(8, CONV_ROWS, 8):
                        part = part + prod[j:j + 8]
                    dw_acc[k, :, ls] += part
            return carry

        lax.fori_loop(0, T // CONV_ROWS, dw_tile, 0)
        for k in range(CONV_WIDTH):
            dw_ref[k:k + 1, :] += jnp.sum(dw_acc[k], axis=0, keepdims=True)
        du0 = du0_scr[...]
        a = ac[...].astype(F32)
        sg = _sigmoid(gc[...].astype(F32))
        dag_ref[:, 0:C] = (du0 * sg).astype(dag_ref.dtype)
        dag_ref[:, C:] = (du0 * a * sg * (1.0 - sg)).astype(dag_ref.dtype)

    dp, dc, dn = _halo_specs(S, T, C, 1)
    up, uc, un = _halo_specs(S, T, C, 0)
    ap, ac, an = _halo_specs(S, T, C, 3)
    gp, gc, gn = _halo_specs(S, T, C, 4)
    vec = pl.BlockSpec((1, C), lambda i: (0, 0))
    return pl.pallas_call(
        body, name="conv_bwd", grid=(nblk,),
        in_specs=[dp, dc, dn, up, uc, un, ap, ac, an, gp, gc, gn,
                  pl.BlockSpec((32, C), lambda i: (0, 0)), vec, vec],
        out_specs=[pl.BlockSpec((T, 2 * C), lambda i: (i, 0)), pl.BlockSpec((32, C), lambda i: (0, 0)),
                   pl.BlockSpec((8, C), lambda i: (0, 0))],
        out_shape=[_sds((S, 2 * C), BF16), _sds((32, C), F32), _sds((8, C), F32)],
        scratch_shapes=[_window_scratch(T, C), _window_scratch(T, C), pltpu.VMEM((T, C), F32),
                        pltpu.VMEM((CONV_WIDTH, 8, C), F32)],
        compiler_params=_params("arbitrary"))(dac, dac, dac, u1, u1, u1, y, y, y, y, y, y, conv_w32, ln_g, ln_b)


def _xatt_fwd(xq, xk, xv, *, tm=512):
    S = xq.shape[0]
    M = xk.shape[0]
    tm = min(tm, S)
    scale = XATT_HEAD_DIM ** -0.5

    def body(q_ref, k_ref, v_ref, o_ref):
        for h in range(XATT_HEADS):
            sl = slice(h * XATT_HEAD_DIM, (h + 1) * XATT_HEAD_DIM)
            s = _nt(q_ref[:, sl], k_ref[:, sl]) * scale
            e = jnp.exp(s - jnp.max(s, axis=-1, keepdims=True))
            p = e / jnp.sum(e, axis=-1, keepdims=True)
            o_ref[:, sl] = jnp.dot(p.astype(BF16), v_ref[:, sl], preferred_element_type=F32).astype(o_ref.dtype)

    row = pl.BlockSpec((tm, D_MODEL), lambda i: (i, 0))
    full = pl.BlockSpec((M, D_MODEL), lambda i: (0, 0))
    return pl.pallas_call(
        body, name="xatt_fwd", grid=(S // tm,), in_specs=[row, full, full], out_specs=row,
        out_shape=_sds((S, D_MODEL), BF16), compiler_params=_params("parallel"))(xq, xk, xv)


def _xatt_bwd(xq, xk, xv, dxo, *, tm=512):
    S = xq.shape[0]
    M = xk.shape[0]
    tm = min(tm, S)
    scale = XATT_HEAD_DIM ** -0.5

    def body(q_ref, k_ref, v_ref, do_ref, dq_ref, dk_ref, dv_ref):
        i = pl.program_id(0)

        @pl.when(i == 0)
        def _():
            dk_ref[...] = jnp.zeros_like(dk_ref)
            dv_ref[...] = jnp.zeros_like(dv_ref)

        for h in range(XATT_HEADS):
            sl = slice(h * XATT_HEAD_DIM, (h + 1) * XATT_HEAD_DIM)
            q, k, v, do = q_ref[:, sl], k_ref[:, sl], v_ref[:, sl], do_ref[:, sl]
            s = _nt(q, k) * scale
            e = jnp.exp(s - jnp.max(s, axis=-1, keepdims=True))
            p = e / jnp.sum(e, axis=-1, keepdims=True)
            dp = _nt(do, v)
            ds = p * (dp - jnp.sum(dp * p, axis=-1, keepdims=True))
            dsb = ds.astype(BF16)
            dq_ref[:, sl] = (jnp.dot(dsb, k, preferred_element_type=F32) * scale).astype(dq_ref.dtype)
            dv_ref[:, sl] += _tn(p.astype(BF16), do)
            dk_ref[:, sl] += _tn(dsb, q) * scale

    row = pl.BlockSpec((tm, D_MODEL), lambda i: (i, 0))
    full = pl.BlockSpec((M, D_MODEL), lambda i: (0, 0))
    return pl.pallas_call(
        body, name="xatt_bwd", grid=(S // tm,), in_specs=[row, full, full, row], out_specs=[row, full, full],
        out_shape=[_sds((S, D_MODEL), BF16), _sds((M, D_MODEL), F32), _sds((M, D_MODEL), F32)],
        compiler_params=_params("arbitrary"))(xq, xk, xv, dxo)


def _row_tile(R):
    for t in (256, 128, 64, 32, 16, 8):
        if R % t == 0:
            return t
    return R


def _sum_partials(own, recv, me, *, name):
    _, R, C = own.shape
    t = _row_tile(R)

    def body(me_ref, own_ref, r_ref, o_ref):
        o_ref[...] = ((own_ref[...].astype(F32) + r_ref[0].astype(F32)) + r_ref[1].astype(F32)) + r_ref[2].astype(F32)

    return pl.pallas_call(
        body, name=name,
        grid_spec=pltpu.PrefetchScalarGridSpec(
            num_scalar_prefetch=1, grid=(R // t,),
            in_specs=[pl.BlockSpec((None, t, C), lambda i, me_ref: (me_ref[0], i, 0)),
                      pl.BlockSpec((3, t, C), lambda i, me_ref: (0, i, 0))],
            out_specs=pl.BlockSpec((t, C), lambda i, me_ref: (i, 0))),
        out_shape=_sds((R, C), F32), compiler_params=_params("parallel"))(me, own, recv)


def _adamw_math(w, g, m, v):
    m2 = ADAM_B1 * m + (1.0 - ADAM_B1) * g
    v2 = ADAM_B2 * v + (1.0 - ADAM_B2) * (g * g)
    m_hat = m2 / (1.0 - ADAM_B1 ** ADAM_STEP)
    v_hat = v2 / (1.0 - ADAM_B2 ** ADAM_STEP)
    delta = -ADAM_LR * (m_hat / (jnp.sqrt(v_hat) + ADAM_EPS) + ADAM_WD * w)
    return delta, m2, v2


def _adamw(parts, w, m, v, *, name):
    R, C = w.shape
    t = _row_tile(R)
    n = len(parts)

    def body(*refs):
        w_ref, m_ref, v_ref = refs[n:n + 3]
        g_ref, d_ref, m2_ref, v2_ref = refs[n + 3:]
        g = refs[0][...]
        for r in refs[1:n]:
            g = g + r[...]
        delta, m2, v2 = _adamw_math(w_ref[...], g, m_ref[...], v_ref[...])
        g_ref[...] = g
        d_ref[...] = delta
        m2_ref[...] = m2
        v2_ref[...] = v2

    blk = pl.BlockSpec((t, C), lambda i: (i, 0))
    return pl.pallas_call(
        body, name=name, grid=(R // t,), in_specs=[blk] * (n + 3), out_specs=[blk] * 4,
        out_shape=[_sds((R, C), F32)] * 4, compiler_params=_params("parallel"))(*parts, w, m, v)


def _sum_devices(gathered):
    _, R, C = gathered.shape

    def body(g_ref, o_ref):
        acc = g_ref[0]
        for k in range(1, N_DEV):
            acc = acc + g_ref[k]
        o_ref[...] = acc

    return pl.pallas_call(body, name="sum_devices", out_shape=_sds((R, C), F32))(gathered)


def _chip_peers():
    x, y = lax.axis_index("x"), lax.axis_index("y")
    return [(1 - x, y), (x, 1 - y), (1 - x, 1 - y)]


def _gather_weights(shards):
    n = len(shards)

    def body(*refs):
        ins, outs = refs[:n], refs[n:2 * n]
        send_sems, recv_sems, loc_sems = refs[2 * n:]
        x, y, c = lax.axis_index("x"), lax.axis_index("y"), lax.axis_index("c")
        me = 2 * x + y
        peers = _chip_peers()
        started = []
        for t in range(n):
            loc = pltpu.make_async_copy(ins[t], outs[t].at[me], loc_sems.at[t])
            loc.start()
            started.append(loc)
        sends = []
        for t in range(n):
            for k, (px, py) in enumerate(peers):
                cp = pltpu.make_async_remote_copy(
                    src_ref=ins[t], dst_ref=outs[t].at[me], send_sem=send_sems.at[t, k],
                    recv_sem=recv_sems.at[t, k], device_id=(px, py, c), device_id_type=MESH)
                cp.start()
                sends.append(cp)
        for t in range(n):
            for k, (px, py) in enumerate(peers):
                pltpu.make_async_remote_copy(
                    src_ref=ins[t], dst_ref=outs[t].at[2 * px + py], send_sem=send_sems.at[t, k],
                    recv_sem=recv_sems.at[t, k], device_id=(px, py, c), device_id_type=MESH).wait_recv()
        for cp in sends:
            cp.wait_send()
        for loc in started:
            loc.wait()

    any_spec = pl.BlockSpec(memory_space=pl.ANY)
    return pl.pallas_call(
        body, name="gather_weights", in_specs=[any_spec] * n, out_specs=[any_spec] * n,
        out_shape=[_sds((N_CHIPS,) + s.shape, s.dtype) for s in shards],
        scratch_shapes=[pltpu.SemaphoreType.DMA((n, 3)), pltpu.SemaphoreType.DMA((n, 3)),
                        pltpu.SemaphoreType.DMA((n,))])(*shards)


HBM_SPEC = pl.BlockSpec(memory_space=pltpu.HBM)
SEM_SPEC = pl.BlockSpec(memory_space=pltpu.SEMAPHORE)


def _exchange_start(mode, srcs, zones, *, name):
    n = len(srcs)

    def body(*refs):
        ins, lands = refs[:n], refs[n:2 * n]
        send_sems, recv_sems = refs[2 * n:3 * n], refs[3 * n:4 * n]
        token = refs[-1]
        c = lax.axis_index("c")
        mine = 2 * lax.axis_index("x") + lax.axis_index("y")
        for t in range(n):
            for k, (px, py) in enumerate(_chip_peers()):
                if mode == "gather":
                    s, d = ins[t], lands[t].at[mine]
                else:
                    s, d = ins[t].at[2 * px + py], lands[t].at[k]
                pltpu.make_async_remote_copy(src_ref=s, dst_ref=d, send_sem=send_sems[t], recv_sem=recv_sems[t],
                                             device_id=(px, py, c), device_id_type=MESH).start()
        token[...] = jnp.zeros_like(token)

    hbm = lambda a: pltpu.with_memory_space_constraint(a, pltpu.HBM)
    out = pl.pallas_call(
        body, name=name,
        in_specs=[HBM_SPEC] * (2 * n),
        out_specs=[SEM_SPEC] * (2 * n) + [HBM_SPEC] * (2 * n) + [pl.BlockSpec(memory_space=pltpu.VMEM)],
        out_shape=[pltpu.SemaphoreType.DMA(())] * (2 * n)
        + [pltpu.HBM(a.shape, a.dtype) for a in list(srcs) + list(zones)] + [_sds((8, LANES), F32)],
        input_output_aliases={i: 2 * n + i for i in range(2 * n)},
        compiler_params=pltpu.CompilerParams(has_side_effects=pltpu.SideEffectType.DATAFLOW_SIDE_EFFECTING),
    )(*[hbm(a) for a in list(srcs) + list(zones)])
    return out[:n], out[n:2 * n], out[2 * n:3 * n], out[3 * n:4 * n], out[-1]


def _exchange_wait(started, after, *, name):
    send_sems, recv_sems, srcs, zones, _ = started
    n = len(srcs)

    def body(*refs):
        lands = refs[n:2 * n]
        send_refs, recv_refs = refs[2 * n:3 * n], refs[3 * n:4 * n]
        me = (lax.axis_index("x"), lax.axis_index("y"), lax.axis_index("c"))
        for t in range(n):
            three = lands[t].at[pl.ds(0, N_CHIPS - 1)]
            cp = pltpu.make_async_remote_copy(src_ref=three, dst_ref=three, send_sem=send_refs[t],
                                              recv_sem=recv_refs[t], device_id=me, device_id_type=MESH)
            cp.wait_send()
            cp.wait_recv()

    out = pl.pallas_call(
        body, name=name,
        in_specs=[HBM_SPEC] * (2 * n) + [SEM_SPEC] * (2 * n) + [pl.BlockSpec(memory_space=pl.ANY)],
        out_specs=[HBM_SPEC] * (2 * n),
        out_shape=[pltpu.HBM(a.shape, a.dtype) for a in list(srcs) + list(zones)],
        input_output_aliases={i: i for i in range(2 * n)},
        compiler_params=pltpu.CompilerParams(has_side_effects=pltpu.SideEffectType.DATAFLOW_SIDE_EFFECTING),
    )(*srcs, *zones, *send_sems, *recv_sems, after)
    return out[:n], out[n:]


def _scatter_grads(grads, small):
    n = len(grads)

    def body(*refs):
        ins, small_ref = refs[:n], refs[n]
        outs, gath_ref = refs[n + 1:2 * n + 1], refs[2 * n + 1]
        send_sems, recv_sems, ssend, srecv, loc_sem = refs[2 * n + 2:]
        x, y, c = lax.axis_index("x"), lax.axis_index("y"), lax.axis_index("c")
        me = 4 * x + 2 * y + c
        peers = _chip_peers()
        flips = [(fx, fy, fc) for fx in (0, 1) for fy in (0, 1) for fc in (0, 1)][1:]

        def flipped(fx, fy, fc):
            return (1 - x if fx else x, 1 - y if fy else y, 1 - c if fc else c)

        loc = pltpu.make_async_copy(small_ref, gath_ref.at[me], loc_sem)
        loc.start()
        sends = []
        for j, (fx, fy, fc) in enumerate(flips):
            cp = pltpu.make_async_remote_copy(
                src_ref=small_ref, dst_ref=gath_ref.at[me], send_sem=ssend.at[j], recv_sem=srecv.at[j],
                device_id=flipped(fx, fy, fc), device_id_type=MESH)
            cp.start()
            sends.append(cp)
        for t in range(n):
            for k, (px, py) in enumerate(peers):
                cp = pltpu.make_async_remote_copy(
                    src_ref=ins[t].at[2 * px + py], dst_ref=outs[t].at[k], send_sem=send_sems.at[t, k],
                    recv_sem=recv_sems.at[t, k], device_id=(px, py, c), device_id_type=MESH)
                cp.start()
                sends.append(cp)
        for j, (fx, fy, fc) in enumerate(flips):
            px, py, pc = flipped(fx, fy, fc)
            pltpu.make_async_remote_copy(
                src_ref=small_ref, dst_ref=gath_ref.at[4 * px + 2 * py + pc], send_sem=ssend.at[j],
                recv_sem=srecv.at[j], device_id=(px, py, pc), device_id_type=MESH).wait_recv()
        for t in range(n):
            for k, (px, py) in enumerate(peers):
                pltpu.make_async_remote_copy(
                    src_ref=ins[t].at[2 * px + py], dst_ref=outs[t].at[k], send_sem=send_sems.at[t, k],
                    recv_sem=recv_sems.at[t, k], device_id=(px, py, c), device_id_type=MESH).wait_recv()
        for cp in sends:
            cp.wait_send()
        loc.wait()

    any_spec = pl.BlockSpec(memory_space=pl.ANY)
    out = pl.pallas_call(
        body, name="scatter_grads", in_specs=[any_spec] * (n + 1), out_specs=[any_spec] * (n + 1),
        out_shape=[_sds((3,) + g.shape[1:], g.dtype) for g in grads] + [_sds((N_DEV,) + small.shape, small.dtype)],
        scratch_shapes=[pltpu.SemaphoreType.DMA((n, 3)), pltpu.SemaphoreType.DMA((n, 3)),
                        pltpu.SemaphoreType.DMA((N_DEV - 1,)), pltpu.SemaphoreType.DMA((N_DEV - 1,)),
                        pltpu.SemaphoreType.DMA])(*grads, small)
    return out[:n], out[n]


def _swap_with_sibling(parts):
    n = len(parts)

    def body(*refs):
        ins, outs = refs[:n], refs[n:2 * n]
        send_sems, recv_sems = refs[2 * n:]
        sib = (lax.axis_index("x"), lax.axis_index("y"), 1 - lax.axis_index("c"))
        cps = []
        for t in range(n):
            cp = pltpu.make_async_remote_copy(
                src_ref=ins[t], dst_ref=outs[t], send_sem=send_sems.at[t], recv_sem=recv_sems.at[t],
                device_id=sib, device_id_type=MESH)
            cp.start()
            cps.append(cp)
        for cp in cps:
            cp.wait()

    any_spec = pl.BlockSpec(memory_space=pl.ANY)
    return pl.pallas_call(
        body, name="swap_with_sibling", in_specs=[any_spec] * n, out_specs=[any_spec] * n,
        out_shape=[_sds(p.shape, p.dtype) for p in parts],
        scratch_shapes=[pltpu.SemaphoreType.DMA((n,)), pltpu.SemaphoreType.DMA((n,))])(*parts)


BIG = ("w_in", "w_out", "w_xq", "w_xk", "w_xv", "w_xo", "w_up", "w_down")
COL_SHARDED = ("w_in", "w_up")


def _as_matrix(name, w4):
    if name in COL_SHARDED:
        return w4
    return w4.reshape(1, w4.shape[0] * w4.shape[1], w4.shape[2])


def _shard_layout(name, g):
    if name in COL_SHARDED:
        return g
    return g.reshape(N_CHIPS, g.shape[0] * g.shape[1] // N_CHIPS, g.shape[2])


def _local_step(x, mem, target, w_in, conv_w32, vecs, comm):
    S = x.shape[0]
    tables = _rope_tables(S)

    xn = _rms_fwd(x, vecs["norm_mix_g"], name="rms_mix")
    y = _mm_nn(xn, w_in, name="mm_in", tn=640)
    qk, v_perm = _rope_fwd(y, tables)
    v_src = [(y, 2)] + [(v, 0) for v in v_perm[1:]]
    outs, lses = zip(*[_att_fwd(qk[p], v_src[p], d, name=f"att_fwd_d{d}") for p, d in enumerate(DILATIONS)])
    att, lg = _att_combine(outs, lses)
    cv, u1 = _conv_fwd(y, conv_w32, vecs["conv_b"], vecs["conv_ln_g"], vecs["conv_ln_b"])
    mix = jnp.concatenate([att, cv], axis=1)
    Wm = {k: _as_matrix(k, v) for k, v in comm["rest"](mix).items()}
    Wm["w_in"] = w_in
    h1 = _mm_nn(mix, Wm["w_out"], name="mm_out", out_dtype=F32, res=x)
    hn = _rms_fwd(h1, vecs["norm_x_g"], name="rms_x")
    xq = _mm_nn(hn, Wm["w_xq"], name="mm_xq")
    mn = _rms_fwd(mem, vecs["norm_mem_g"], name="rms_mem")
    xk = _mm_nn(mn, Wm["w_xk"], name="mm_xk")
    xv = _mm_nn(mn, Wm["w_xv"], name="mm_xv")
    xo = _xatt_fwd(xq, xk, xv)
    h2 = _mm_nn(xo, Wm["w_xo"], name="mm_xo", out_dtype=F32, res=h1)
    hm = _rms_fwd(h2, vecs["norm_mlp_g"], name="rms_mlp")
    relu_up, act = _mm_nn(hm, Wm["w_up"], name="mm_up", relu2=True)
    h3 = _mm_nn(act, Wm["w_down"], name="mm_down", out_dtype=F32, res=h2)

    dh3, dh3b, dg_final, loss = _loss_head(h3, vecs["norm_final_g"], target)
    g = {}
    g["w_down"] = _mm_tn(act, dh3b, 1, name="dw_down")
    dup = _mm_nt(dh3b, Wm["w_down"], name="d_act", out_dtype=BF16, mul=relu_up)
    g["w_up"] = _mm_tn(hm, dup, N_CHIPS, name="dw_up")
    sent = comm["send_mlp"]({k: _shard_layout(k, g[k]) for k in ("w_down", "w_up")})
    dhm = _mm_nt(dup, Wm["w_up"], name="d_hm")
    dh2, dh2b, dg_mlp = _rms_bwd(dhm, h2, vecs["norm_mlp_g"] + sent[0:1, 0:1], dh3, name="rms_bwd_mlp")
    g["w_xo"] = _mm_tn(xo, dh2b, 1, name="dw_xo")
    dxo = _mm_nt(dh2b, Wm["w_xo"], name="d_xo", out_dtype=BF16)
    dxq, dxk, dxv = _xatt_bwd(xq, xk, xv, dxo)
    g["w_xq"] = _mm_tn(hn, dxq, 1, name="dw_xq")
    dhn = _mm_nt(dxq, Wm["w_xq"], name="d_hn")
    dh1, dh1b, dg_x = _rms_bwd(dhn, h1, vecs["norm_x_g"], dh2, name="rms_bwd_x")
    dxkb, dxvb = dxk.astype(BF16), dxv.astype(BF16)
    g["w_xk"] = _mm_tn(mn, dxkb, 1, name="dw_xk")
    g["w_xv"] = _mm_tn(mn, dxvb, 1, name="dw_xv")
    dmn = _mm_nt(jnp.concatenate([dxkb, dxvb], axis=1),
                 jnp.concatenate([Wm["w_xk"], Wm["w_xv"]], axis=2), name="d_mn")
    _, _, dg_mem = _rms_bwd(dmn, mem, vecs["norm_mem_g"], None, name="rms_bwd_mem")
    g["w_out"] = _mm_tn(mix, dh1b, 1, name="dw_out")
    sent = comm["send_att"]({k: _shard_layout(k, g[k]) for k in ("w_out", "w_xq", "w_xk", "w_xv", "w_xo")})
    dac = _mm_nt(dh1b, Wm["w_out"], name="d_mix", out_dtype=BF16)
    dag, dconv_w, dconv_small = _conv_bwd(dac, u1, y, conv_w32, vecs["conv_ln_g"] + sent[0:1, 0:1],
                                          vecs["conv_ln_b"])
    delta, do_perm = _att_delta(dac, att)
    do_src = [(dac, 0)] + [(t, 0) for t in do_perm[1:]]
    dq = [_att_dq(qk[p], v_src[p], do_src[p], lg[p], delta[p], d, name=f"att_dq_d{d}")
          for p, d in enumerate(DILATIONS)]
    dk, dv = zip(*[_att_dkv(qk[p], v_src[p], do_src[p], lg[p], delta[p], d, name=f"att_dkv_d{d}")
                   for p, d in enumerate(DILATIONS)])
    dy = _assemble_dy(dq, dk, dv, dag, tables)
    g_in = _mm_tn(xn, dy, N_CHIPS, name="dw_in", tn=640)
    dxn = _mm_nt(dy, Wm["w_in"], name="d_xn", tn=640)
    grad_x, _, dg_mix = _rms_bwd(dxn, x, vecs["norm_mix_g"], dh1, name="rms_bwd_mix")

    small = dict(conv_w=dconv_w, conv_small=dconv_small, norm_mix_g=dg_mix, norm_x_g=dg_x, norm_mem_g=dg_mem,
                 norm_mlp_g=dg_mlp, norm_final_g=dg_final, loss=loss)
    return grad_x, g_in, small


SMALL_ORDER = ("conv_w", "conv_small", "norm_mix_g", "norm_x_g", "norm_mem_g", "norm_mlp_g", "norm_final_g", "loss")


def _pack_small(small):
    rows, offs, pos = [], {}, 0
    for k in SMALL_ORDER:
        a = small[k]
        a = a.reshape(a.shape[0] * a.shape[1] // SMALL_W, SMALL_W)
        pad = (-a.shape[0]) % 8
        if pad:
            a = jnp.pad(a, ((0, pad), (0, 0)))
        rows.append(a)
        offs[k] = pos
        pos += a.shape[0]
    return jnp.concatenate(rows, axis=0), offs


def kernel(x, mem, norm_mix_g, w_in, conv_w, conv_b, conv_ln_g, conv_ln_b, w_out, norm_x_g, norm_mem_g, w_xq, w_xk, w_xv, w_xo, norm_mlp_g, w_up, w_down, norm_final_g, loss_target, m_norm_mix_g, m_w_in, m_conv_w, m_conv_b, m_conv_ln_g, m_conv_ln_b, m_w_out, m_norm_x_g, m_norm_mem_g, m_w_xq, m_w_xk, m_w_xv, m_w_xo, m_norm_mlp_g, m_w_up, m_w_down, m_norm_final_g, v_norm_mix_g, v_w_in, v_conv_w, v_conv_b, v_conv_ln_g, v_conv_ln_b, v_w_out, v_norm_x_g, v_norm_mem_g, v_w_xq, v_w_xk, v_w_xv, v_w_xo, v_norm_mlp_g, v_w_up, v_w_down, v_norm_final_g):
    names = ("norm_mix_g", "w_in", "conv_w", "conv_b", "conv_ln_g", "conv_ln_b", "w_out", "norm_x_g", "norm_mem_g",
             "w_xq", "w_xk", "w_xv", "w_xo", "norm_mlp_g", "w_up", "w_down", "norm_final_g")
    wts = dict(zip(names, (norm_mix_g, w_in, conv_w, conv_b, conv_ln_g, conv_ln_b, w_out, norm_x_g, norm_mem_g,
                           w_xq, w_xk, w_xv, w_xo, norm_mlp_g, w_up, w_down, norm_final_g)))
    mom = dict(zip(names, (m_norm_mix_g, m_w_in, m_conv_w, m_conv_b, m_conv_ln_g, m_conv_ln_b, m_w_out, m_norm_x_g,
                           m_norm_mem_g, m_w_xq, m_w_xk, m_w_xv, m_w_xo, m_norm_mlp_g, m_w_up, m_w_down, m_norm_final_g)))
    var = dict(zip(names, (v_norm_mix_g, v_w_in, v_conv_w, v_conv_b, v_conv_ln_g, v_conv_ln_b, v_w_out, v_norm_x_g,
                           v_norm_mem_g, v_w_xq, v_w_xk, v_w_xv, v_w_xo, v_norm_mlp_g, v_w_up, v_w_down, v_norm_final_g)))
    chip = 2 * lax.axis_index("x") + lax.axis_index("y")

    conv_w_pad = jnp.pad(wts["conv_w"][0], ((0, 1), (0, 0)))
    w_in_all, conv_w_all = _gather_weights([wts["w_in"][0].astype(BF16), conv_w_pad])
    conv_w32 = jnp.transpose(conv_w_all, (1, 0, 2)).reshape(32, D_CONV)
    rest = tuple(k for k in BIG if k != "w_in")
    rest_shards = [wts[k][0].astype(BF16) for k in rest]
    behind_first = lax.bitcast_convert_type(w_in_all[0, 0, 0], jnp.uint16).astype(jnp.int32) & 0
    rest_zones = [lax.dynamic_update_slice(lax.empty((N_CHIPS,) + s.shape, s.dtype), s[None],
                                           (chip + behind_first, 0, 0)) for s in rest_shards]
    gathering = _exchange_start("gather", rest_shards, rest_zones, name="gather_rest_start")
    sending = {}

    def wait_rest(after):
        _, zones = _exchange_wait(gathering, after, name="gather_rest_wait")
        return dict(zip(rest, zones))

    def send(group, grads):
        keys = tuple(grads)
        zones = [lax.empty((N_CHIPS - 1,) + grads[k].shape[1:], grads[k].dtype) for k in keys]
        sending[group] = (keys, _exchange_start("scatter", [grads[k] for k in keys], zones,
                                                name=f"scatter_{group}_start"))
        return sending[group][1][4]

    comm = dict(rest=wait_rest, send_mlp=lambda grads: send("mlp", grads), send_att=lambda grads: send("att", grads))
    vecs = {k: wts[k] for k in ("conv_b", "conv_ln_g", "conv_ln_b", "norm_x_g", "norm_mem_g", "norm_mlp_g")}
    vecs["norm_mix_g"] = wts["norm_mix_g"] + gathering[4][0:1, 0:1]
    vecs["norm_final_g"] = wts["norm_final_g"].reshape(1, D_MODEL)
    grad_x, g_in, small = _local_step(x[0], mem[0], loss_target[0], w_in_all, conv_w32, vecs, comm)

    packed, offs = _pack_small(small)
    recv_in, gath = _scatter_grads([g_in], packed)
    big, recv = {"w_in": g_in}, {"w_in": recv_in[0]}
    for group in ("mlp", "att"):
        keys, started = sending[group]
        srcs, zones = _exchange_wait(started, recv_in[0], name=f"scatter_{group}_wait")
        big.update(zip(keys, srcs))
        recv.update(zip(keys, zones))
    me_arr = jnp.reshape(chip, (1,)).astype(jnp.int32)
    sums = [_sum_partials(big[k], recv[k], me_arr, name=f"sum_{k}") for k in BIG]
    sib = _swap_with_sibling(sums)
    tot_small = _sum_devices(gath)

    res = {}
    for k, s_mine, s_sib in zip(BIG, sums, sib):
        res[k] = _adamw([s_mine, s_sib], wts[k][0], mom[k][0], var[k][0], name=f"adamw_{k}")

    def piece(key, nrows):
        return tot_small[offs[key]:offs[key] + nrows]

    def small_update(k, gfull):
        return _adamw([gfull], wts[k].reshape(gfull.shape), mom[k].reshape(gfull.shape),
                      var[k].reshape(gfull.shape), name=f"adamw_{k}")

    dcw = piece("conv_w", 32)[:CONV_WIDTH]
    dcw_mine = lax.dynamic_slice_in_dim(dcw, chip * (D_CONV // N_CHIPS), D_CONV // N_CHIPS, axis=1)
    res["conv_w"] = small_update("conv_w", dcw_mine)
    cs = piece("conv_small", 8)
    res["conv_b"] = small_update("conv_b", cs[0:1])
    res["conv_ln_g"] = small_update("conv_ln_g", cs[1:2])
    res["conv_ln_b"] = small_update("conv_ln_b", cs[2:3])
    for k in ("norm_mix_g", "norm_x_g", "norm_mem_g", "norm_mlp_g", "norm_final_g"):
        res[k] = small_update(k, piece(k, 8 * D_MODEL // SMALL_W).reshape(8, D_MODEL)[0:1])
    loss = piece("loss", 8)[0, 0]

    outs = [loss, grad_x[None]]
    for j in range(4):
        outs += [res[k][j].reshape(wts[k].shape) for k in names]
    return tuple(outs)
```

```python
import jax
import jax.numpy as jnp
from jax import lax
from jax.experimental import pallas as pl
from jax.experimental.pallas import tpu as pltpu

F32 = jnp.float32
BF16 = jnp.bfloat16
MESH = pl.DeviceIdType.MESH

D_MODEL = 1024
ATT_HEADS = 8
HEAD_DIM = 64
D_ATT = ATT_HEADS * HEAD_DIM
D_CONV = D_MODEL - D_ATT
DILATIONS = (1, 4, 16)
HALF = 64
ROPE_THETA = 500000.0
ROT_DIM = HEAD_DIM // 4
CONV_WIDTH = 31
CONV_PAD = (CONV_WIDTH - 1) // 2
XATT_HEADS = 4
XATT_HEAD_DIM = D_MODEL // XATT_HEADS
D_FF = 4 * D_MODEL
D_IN = 3 * D_ATT + 2 * D_CONV
EPS = 1e-6
NEG_INF = -1e30
N_CHIPS = 4
N_DEV = 8

ADAM_LR = 0.001
ADAM_B1 = 0.9
ADAM_B2 = 0.999
ADAM_EPS = 1e-08
ADAM_WD = 0.01
ADAM_STEP = 10

VMEM_LIMIT_V7X = 56 * 1024 * 1024
LANES = 128
HALO = 16
CONV_ROWS = 64
ATT_BLOCK = 128
SMALL_W = 512


def _params(*sem):
    return pltpu.CompilerParams(dimension_semantics=sem, vmem_limit_bytes=VMEM_LIMIT_V7X)


def _sds(shape, dtype):
    return jax.ShapeDtypeStruct(shape, dtype)


def _mm_nn(a, w3, *, name, out_dtype=BF16, res=None, relu2=False, tm=1024, tn=None, tk=1024):
    M, K = a.shape
    nsh, _, n = w3.shape
    tm, tk = min(tm, M), min(tk, K)
    tn = tn or min(n, 1024)
    npt, nk = n // tn, K // tk
    nj, N = nsh * npt, nsh * n
    n_out = 2 if relu2 else 1

    def body(*refs):
        a_ref, w_ref = refs[0], refs[1]
        pos = 2
        res_ref = None
        if res is not None:
            res_ref = refs[pos]
            pos += 1
        outs = refs[pos:pos + n_out]
        acc_ref = refs[pos + n_out] if nk > 1 else None

        def finish(acc):
            if res_ref is not None:
                acc = acc + res_ref[...]
            if relu2:
                r = jnp.maximum(acc, 0.0)
                outs[0][...] = r.astype(outs[0].dtype)
                outs[1][...] = (r * r).astype(outs[1].dtype)
            else:
                outs[0][...] = acc.astype(outs[0].dtype)

        part = jnp.dot(a_ref[...], w_ref[...], preferred_element_type=F32)
        if nk == 1:
            finish(part)
        else:
            k = pl.program_id(2)

            @pl.when(k == 0)
            def _():
                acc_ref[...] = part

            @pl.when(k > 0)
            def _():
                acc_ref[...] += part

            @pl.when(k == nk - 1)
            def _():
                finish(acc_ref[...])

    in_specs = [pl.BlockSpec((tm, tk), lambda i, j, k: (i, k)),
                pl.BlockSpec((None, tk, tn), lambda i, j, k: (j // npt, k, j % npt))]
    args = [a, w3]
    if res is not None:
        in_specs.append(pl.BlockSpec((tm, tn), lambda i, j, k: (i, j)))
        args.append(res)
    out_spec = pl.BlockSpec((tm, tn), lambda i, j, k: (i, j))
    out = pl.pallas_call(
        body, name=name, grid=(M // tm, nj, nk), in_specs=in_specs,
        out_specs=[out_spec] * n_out, out_shape=[_sds((M, N), out_dtype)] * n_out,
        scratch_shapes=[pltpu.VMEM((tm, tn), F32)] if nk > 1 else [],
        compiler_params=_params("parallel", "parallel", "arbitrary"))(*args)
    return tuple(out) if relu2 else out[0]


def _mm_nt(dy, w3, *, name, out_dtype=F32, mul=None, tm=1024, tn=None, tko=1024):
    M, N = dy.shape
    nsh, K, n = w3.shape
    tm, tko = min(tm, M), min(tko, K)
    tn = tn or min(n, 1024)
    npt = n // tn
    nj = nsh * npt

    def body(*refs):
        dy_ref, w_ref = refs[0], refs[1]
        pos = 2
        mul_ref = None
        if mul is not None:
            mul_ref = refs[pos]
            pos += 1
        out_ref = refs[pos]
        acc_ref = refs[pos + 1] if nj > 1 else None

        def finish(acc):
            if mul_ref is not None:
                acc = acc * (2.0 * mul_ref[...].astype(F32))
            out_ref[...] = acc.astype(out_ref.dtype)

        part = lax.dot_general(dy_ref[...], w_ref[...], (((1,), (1,)), ((), ())), preferred_element_type=F32)
        if nj == 1:
            finish(part)
        else:
            j = pl.program_id(2)

            @pl.when(j == 0)
            def _():
                acc_ref[...] = part

            @pl.when(j > 0)
            def _():
                acc_ref[...] += part

            @pl.when(j == nj - 1)
            def _():
                finish(acc_ref[...])

    in_specs = [pl.BlockSpec((tm, tn), lambda i, ko, j: (i, j)),
                pl.BlockSpec((None, tko, tn), lambda i, ko, j: (j // npt, ko, j % npt))]
    args = [dy, w3]
    if mul is not None:
        in_specs.append(pl.BlockSpec((tm, tko), lambda i, ko, j: (i, ko)))
        args.append(mul)
    return pl.pallas_call(
        body, name=name, grid=(M // tm, K // tko, nj), in_specs=in_specs,
        out_specs=pl.BlockSpec((tm, tko), lambda i, ko, j: (i, ko)), out_shape=_sds((M, K), out_dtype),
        scratch_shapes=[pltpu.VMEM((tm, tko), F32)] if nj > 1 else [],
        compiler_params=_params("parallel", "parallel", "arbitrary"))(*args)


def _mm_tn(a, dy, nsh, *, name, out_dtype=BF16, tm=2048, tk=1024, tn=None):
    M, K = a.shape
    N = dy.shape[1]
    n = N // nsh
    tm, tk = min(tm, M), min(tk, K)
    tn = tn or min(n, 1024)
    npt = n // tn
    nj, nm = nsh * npt, M // tm

    def body(a_ref, dy_ref, out_ref, acc_ref):
        m = pl.program_id(2)
        part = lax.dot_general(a_ref[...], dy_ref[...], (((0,), (0,)), ((), ())), preferred_element_type=F32)

        @pl.when(m == 0)
        def _():
            acc_ref[...] = part

        @pl.when(m > 0)
        def _():
            acc_ref[...] += part

        @pl.when(m == nm - 1)
        def _():
            out_ref[...] = acc_ref[...].astype(out_ref.dtype)

    return pl.pallas_call(
        body, name=name, grid=(K // tk, nj, nm),
        in_specs=[pl.BlockSpec((tm, tk), lambda kk, j, m: (m, kk)),
                  pl.BlockSpec((tm, tn), lambda kk, j, m: (m, j))],
        out_specs=pl.BlockSpec((None, tk, tn), lambda kk, j, m: (j // npt, kk, j % npt)),
        out_shape=_sds((nsh, K, n), out_dtype),
        scratch_shapes=[pltpu.VMEM((tk, tn), F32)],
        compiler_params=_params("parallel", "parallel", "arbitrary"))(a, dy)


def _rms_fwd(x, g, *, name, tm=512):
    M, Dm = x.shape
    tm = min(tm, M)

    def body(x_ref, g_ref, o_ref):
        xf = x_ref[...]
        r = lax.rsqrt(jnp.mean(xf * xf, axis=-1, keepdims=True) + EPS)
        o_ref[...] = (xf * r * g_ref[...]).astype(o_ref.dtype)

    return pl.pallas_call(
        body, name=name, grid=(M // tm,),
        in_specs=[pl.BlockSpec((tm, Dm), lambda i: (i, 0)), pl.BlockSpec((1, Dm), lambda i: (0, 0))],
        out_specs=pl.BlockSpec((tm, Dm), lambda i: (i, 0)), out_shape=_sds((M, Dm), BF16),
        compiler_params=_params("parallel"))(x, g)


def _rms_bwd(dxn, x, g, dres, *, name, tm=512):
    M, Dm = x.shape
    tm = min(tm, M)
    has_res = dres is not None

    def body(*refs):
        dxn_ref, x_ref, g_ref = refs[:3]
        dres_ref = refs[3] if has_res else None
        dx_ref, dxb_ref, dg_ref = refs[-3:]
        i = pl.program_id(0)
        xf = x_ref[...]
        r = lax.rsqrt(jnp.mean(xf * xf, axis=-1, keepdims=True) + EPS)
        nrm = xf * r
        dxn_f = dxn_ref[...].astype(F32)
        dn = dxn_f * g_ref[...]
        dx = r * (dn - nrm * jnp.mean(dn * nrm, axis=-1, keepdims=True))
        if has_res:
            dx = dx + dres_ref[...]
        dx_ref[...] = dx
        dxb_ref[...] = dx.astype(dxb_ref.dtype)

        @pl.when(i == 0)
        def _():
            dg_ref[...] = jnp.zeros_like(dg_ref)

        dg_ref[0:1, :] += jnp.sum(dxn_f * nrm, axis=0, keepdims=True)

    row = pl.BlockSpec((tm, Dm), lambda i: (i, 0))
    in_specs = [row, row, pl.BlockSpec((1, Dm), lambda i: (0, 0))] + ([row] if has_res else [])
    args = [dxn, x, g] + ([dres] if has_res else [])
    return pl.pallas_call(
        body, name=name, grid=(M // tm,), in_specs=in_specs,
        out_specs=[row, row, pl.BlockSpec((8, Dm), lambda i: (0, 0))],
        out_shape=[_sds((M, Dm), F32), _sds((M, Dm), BF16), _sds((8, Dm), F32)],
        compiler_params=_params("arbitrary"))(*args)


def _loss_head(h, g, target, *, tm=512):
    M, Dm = h.shape
    tm = min(tm, M)

    def body(h_ref, g_ref, t_ref, dh_ref, dhb_ref, dg_ref, loss_ref):
        i = pl.program_id(0)
        hf = h_ref[...]
        r = lax.rsqrt(jnp.mean(hf * hf, axis=-1, keepdims=True) + EPS)
        nrm = hf * r
        gv = g_ref[...]
        err = nrm * gv - t_ref[...]
        dy = err * (1.0 / Dm)
        dn = dy * gv
        dh = r * (dn - nrm * jnp.mean(dn * nrm, axis=-1, keepdims=True))
        dh_ref[...] = dh
        dhb_ref[...] = dh.astype(dhb_ref.dtype)

        @pl.when(i == 0)
        def _():
            dg_ref[...] = jnp.zeros_like(dg_ref)
            loss_ref[...] = jnp.zeros_like(loss_ref)

        dg_ref[0:1, :] += jnp.sum(dy * nrm, axis=0, keepdims=True)
        part = 0.5 * jnp.sum(jnp.mean(err * err, axis=-1, keepdims=True), axis=0, keepdims=True)
        sel = (lax.broadcasted_iota(jnp.int32, (8, 128), 0) == 0) & (lax.broadcasted_iota(jnp.int32, (8, 128), 1) == 0)
        loss_ref[...] += jnp.where(sel, part, 0.0)

    row = pl.BlockSpec((tm, Dm), lambda i: (i, 0))
    return pl.pallas_call(
        body, name="loss_head", grid=(M // tm,),
        in_specs=[row, pl.BlockSpec((1, Dm), lambda i: (0, 0)), row],
        out_specs=[row, row, pl.BlockSpec((8, Dm), lambda i: (0, 0)), pl.BlockSpec((8, 128), lambda i: (0, 0))],
        out_shape=[_sds((M, Dm), F32), _sds((M, Dm), BF16), _sds((8, Dm), F32), _sds((8, 128), F32)],
        compiler_params=_params("arbitrary"))(h, g, target)


def _class_spec(tm, d, width):
    return pl.BlockSpec((d, tm // d, width), lambda i: (0, i, 0))


def _row_scratch(tm, width):
    return pltpu.VMEM((width // LANES, tm, LANES), F32)


def _fill(scr, val):
    for c in range(scr.shape[0]):
        scr[c] = val[:, c * LANES:(c + 1) * LANES]


def _to_classes(scr, out_ref, d):
    n = scr.shape[1] // d
    for r in range(d):
        for c in range(scr.shape[0]):
            out_ref[r, :, c * LANES:(c + 1) * LANES] = scr[c, pl.ds(r, n, stride=d), :].astype(out_ref.dtype)


def _from_classes(in_ref, scr, d):
    n = scr.shape[1] // d
    for r in range(d):
        blk = in_ref[r].astype(F32)
        for c in range(scr.shape[0]):
            scr[c, pl.ds(r, n, stride=d), :] = blk[:, c * LANES:(c + 1) * LANES]
    return jnp.concatenate([scr[c] for c in range(scr.shape[0])], axis=1)


def _rope_tables(S):
    half = ROT_DIM // 2
    freqs = ROPE_THETA ** (-jnp.arange(0, ROT_DIM, 2, dtype=F32) / ROT_DIM)
    ang = jnp.arange(S, dtype=F32)[:, None] * freqs[None, :]
    cos, sin = jnp.cos(ang), jnp.sin(ang)
    ones = jnp.ones((S, HEAD_DIM - ROT_DIM), F32)
    zeros = jnp.zeros((S, HEAD_DIM - ROT_DIM), F32)
    zh = jnp.zeros((S, half), F32)
    c = jnp.concatenate([cos, cos, ones], axis=1)
    sa = jnp.concatenate([-sin, zh, zeros], axis=1)
    sb = jnp.concatenate([zh, sin, zeros], axis=1)
    return tuple(jnp.tile(t, (1, LANES // HEAD_DIM)) for t in (c, sa, sb))


def _rope_fwd(y, tables, *, tm=512):
    S = y.shape[0]
    W = 2 * D_ATT
    tm = min(tm, S)
    half = ROT_DIM // 2
    dils = [d for d in DILATIONS if d > 1]

    def body(y_ref, c_ref, sa_ref, sb_ref, qk_ref, *rest):
        qk_outs, v_outs = rest[:len(dils)], rest[len(dils):2 * len(dils)]
        scr_qk, scr_v = rest[2 * len(dils):]
        t = y_ref[:, 0:W].astype(F32)
        rep = W // LANES
        c, sa, sb = (jnp.tile(r[...], (1, rep)) for r in (c_ref, sa_ref, sb_ref))
        rot = t * c + pltpu.roll(t, W - half, axis=1) * sa + pltpu.roll(t, half, axis=1) * sb
        qk_ref[...] = rot.astype(qk_ref.dtype)
        _fill(scr_qk, rot)
        _fill(scr_v, y_ref[:, W:W + D_ATT].astype(F32))
        for d, qo, vo in zip(dils, qk_outs, v_outs):
            _to_classes(scr_qk, qo, d)
            _to_classes(scr_v, vo, d)

    tab = pl.BlockSpec((tm, LANES), lambda i: (i, 0))
    out = pl.pallas_call(
        body, name="rope_fwd", grid=(S // tm,),
        in_specs=[pl.BlockSpec((tm, 3 * D_ATT), lambda i: (i, 0)), tab, tab, tab],
        out_specs=[pl.BlockSpec((tm, W), lambda i: (i, 0))] + [_class_spec(tm, d, W) for d in dils]
        + [_class_spec(tm, d, D_ATT) for d in dils],
        out_shape=[_sds((S, W), BF16)] + [_sds((d, S // d, W), BF16) for d in dils]
        + [_sds((d, S // d, D_ATT), BF16) for d in dils],
        scratch_shapes=[_row_scratch(tm, W), _row_scratch(tm, D_ATT)],
        compiler_params=_params("parallel"))(y, *tables)
    qk = [out[0]] + [o.reshape(S, W) for o in out[1:1 + len(dils)]]
    v = [None] + [o.reshape(S, D_ATT) for o in out[1 + len(dils):]]
    return qk, v


def _assemble_dy(dq, dk, dv, dag, tables, *, tm=512):
    S = dag.shape[0]
    tm = min(tm, S)
    half = ROT_DIM // 2
    W = D_ATT
    n_pat = len(DILATIONS)

    def body(*refs):
        groups = [refs[g * n_pat:(g + 1) * n_pat] for g in range(3)]
        dag_ref, c_ref, sa_ref, sb_ref, o_ref, scr = refs[3 * n_pat:]
        rep = W // LANES
        c, sa, sb = (jnp.tile(r[...], (1, rep)) for r in (c_ref, sa_ref, sb_ref))

        def total(rs):
            acc = rs[0][...].astype(F32)
            for d, r in zip(DILATIONS[1:], rs[1:]):
                acc = acc + _from_classes(r, scr, d)
            return acc

        def unrope(dr):
            return dr * c + pltpu.roll(dr * sa, half, axis=1) + pltpu.roll(dr * sb, W - half, axis=1)

        o_ref[:, 0:W] = unrope(total(groups[0])).astype(o_ref.dtype)
        o_ref[:, W:2 * W] = unrope(total(groups[1])).astype(o_ref.dtype)
        o_ref[:, 2 * W:3 * W] = total(groups[2]).astype(o_ref.dtype)
        o_ref[:, 3 * W:] = dag_ref[...]

    specs = [pl.BlockSpec((tm, W), lambda i: (i, 0))] + [_class_spec(tm, d, W) for d in DILATIONS[1:]]
    tab = pl.BlockSpec((tm, LANES), lambda i: (i, 0))
    args = [a if d == 1 else a.reshape(d, S // d, W) for grp in (dq, dk, dv) for d, a in zip(DILATIONS, grp)]
    return pl.pallas_call(
        body, name="assemble_dy", grid=(S // tm,),
        in_specs=specs * 3 + [pl.BlockSpec((tm, 2 * D_CONV), lambda i: (i, 0)), tab, tab, tab],
        out_specs=pl.BlockSpec((tm, D_IN), lambda i: (i, 0)), out_shape=_sds((S, D_IN), BF16),
        scratch_shapes=[_row_scratch(tm, W)],
        compiler_params=_params("parallel"))(*args, dag, *tables)


def _seq_specs(L, tb, col):
    nb, per, nh = L // tb, tb // HALF, L // HALF
    centre = pl.BlockSpec((tb, D_ATT), lambda r, i: (r * nb + i, col))
    prev = pl.BlockSpec((HALF, D_ATT), lambda r, i: (r * nh + jnp.maximum(i * per - 1, 0), col))
    nxt = pl.BlockSpec((HALF, D_ATT), lambda r, i: (r * nh + jnp.minimum((i + 1) * per, nh - 1), col))
    return prev, centre, nxt


def _band_mask(i, tq, L, centre_is_query):
    if centre_is_query:
        shape = (tq, tq + 2 * HALF)
        c_idx = lax.broadcasted_iota(jnp.int32, shape, 0)
        w_idx = lax.broadcasted_iota(jnp.int32, shape, 1)
    else:
        shape = (tq + 2 * HALF, tq)
        w_idx = lax.broadcasted_iota(jnp.int32, shape, 0)
        c_idx = lax.broadcasted_iota(jnp.int32, shape, 1)
    diff = w_idx - c_idx
    wpos = i * tq - HALF + w_idx
    return (diff >= 0) & (diff <= 2 * HALF) & (wpos >= 0) & (wpos < L)


def _lane_groups():
    for c0 in range(0, D_ATT, LANES):
        yield slice(c0, c0 + LANES)


def _first_head(rows):
    return lax.broadcasted_iota(jnp.int32, (rows, LANES), 1) < HEAD_DIM


def _split_pair(x, first):
    zero = jnp.zeros_like(x)
    return jnp.where(first, x, zero), jnp.where(first, zero, x)


def _nt(a, b):
    return lax.dot_general(a, b, (((1,), (1,)), ((), ())), preferred_element_type=F32)


def _tn(a, b):
    return lax.dot_general(a, b, (((0,), (0,)), ((), ())), preferred_element_type=F32)


ATT_SCALE = HEAD_DIM ** -0.5


def _att_fwd(qk, v_src, d, *, name):
    S = qk.shape[0]
    L = S // d
    tq = min(ATT_BLOCK, L)
    v_arr, v_col = v_src

    def body(q_ref, kp_ref, kc_ref, kn_ref, vp_ref, vc_ref, vn_ref, o_ref, lse_ref):
        i = pl.program_id(1)
        valid = _band_mask(i, tq, L, True)
        q = q_ref[...] * ATT_SCALE
        kwin = jnp.concatenate([kp_ref[...], kc_ref[...], kn_ref[...]], axis=0)
        vwin = jnp.concatenate([vp_ref[...], vc_ref[...], vn_ref[...]], axis=0)
        first = _first_head(tq)
        for ls in _lane_groups():
            k2, v2 = kwin[:, ls], vwin[:, ls]
            s = [jnp.where(valid, _nt(t, k2), NEG_INF) for t in _split_pair(q[:, ls], first)]
            m = [jnp.max(t, axis=-1, keepdims=True) for t in s]
            p = [jnp.exp(t - mm) for t, mm in zip(s, m)]
            den = [jnp.sum(t, axis=-1, keepdims=True) for t in p]
            o = [jnp.dot(t.astype(BF16), v2, preferred_element_type=F32) * (1.0 / dd) for t, dd in zip(p, den)]
            lse = [mm + jnp.log(dd) for mm, dd in zip(m, den)]
            o_ref[:, ls] = jnp.where(first, o[0], o[1])
            lse_ref[:, ls] = jnp.where(first, lse[0], lse[1])

    _, qc, _ = _seq_specs(L, tq, 0)
    kp, kc, kn = _seq_specs(L, tq, 1)
    vp, vc, vn = _seq_specs(L, tq, v_col)
    out = pl.BlockSpec((tq, D_ATT), lambda r, i: (r * (L // tq) + i, 0))
    return pl.pallas_call(
        body, name=name, grid=(d, L // tq),
        in_specs=[qc, kp, kc, kn, vp, vc, vn], out_specs=[out, out],
        out_shape=[_sds((S, D_ATT), F32)] * 2,
        compiler_params=_params("parallel", "parallel"))(qk, qk, qk, qk, v_arr, v_arr, v_arr)


def _att_combine(outs, lses, *, tm=512):
    S = outs[0].shape[0]
    tm = min(tm, S)
    dils = DILATIONS[1:]
    n_d = len(dils)

    def body(*refs):
        o_refs, l_refs = refs[0:1 + n_d], refs[1 + n_d:2 + 2 * n_d]
        att_ref, lg_ref = refs[2 + 2 * n_d:4 + 2 * n_d]
        lg_outs = refs[4 + 2 * n_d:4 + 3 * n_d]
        scr = refs[4 + 3 * n_d:]
        scr_o, scr_l, scr_lg = scr[:n_d], scr[n_d:2 * n_d], scr[2 * n_d]
        ls = [l_refs[0][...]] + [_from_classes(r, s, d) for r, s, d in zip(l_refs[1:], scr_l, dils)]
        os_ = [o_refs[0][...]] + [_from_classes(r, s, d) for r, s, d in zip(o_refs[1:], scr_o, dils)]
        mx = ls[0]
        for l in ls[1:]:
            mx = jnp.maximum(mx, l)
        es = [jnp.exp(l - mx) for l in ls]
        tot = es[0]
        num = es[0] * os_[0]
        for e, o in zip(es[1:], os_[1:]):
            tot = tot + e
            num = num + e * o
        att_ref[...] = (num / tot).astype(att_ref.dtype)
        lg = mx + jnp.log(tot)
        lg_ref[...] = lg
        _fill(scr_lg, lg)
        for d, out in zip(dils, lg_outs):
            _to_classes(scr_lg, out, d)

    nat = pl.BlockSpec((tm, D_ATT), lambda i: (i, 0))
    specs = [nat] + [_class_spec(tm, d, D_ATT) for d in dils]
    view = lambda arrs: [arrs[0]] + [a.reshape(d, S // d, D_ATT) for a, d in zip(arrs[1:], dils)]
    out = pl.pallas_call(
        body, name="att_combine", grid=(S // tm,), in_specs=specs * 2,
        out_specs=[nat, nat] + specs[1:],
        out_shape=[_sds((S, D_ATT), BF16), _sds((S, D_ATT), F32)] + [_sds((d, S // d, D_ATT), F32) for d in dils],
        scratch_shapes=[_row_scratch(tm, D_ATT)] * (2 * n_d + 1),
        compiler_params=_params("parallel"))(*view(list(outs)), *view(list(lses)))
    return out[0], [out[1]] + [o.reshape(S, D_ATT) for o in out[2:]]


def _att_delta(dac, att, *, tm=512):
    S = att.shape[0]
    tm = min(tm, S)
    dils = DILATIONS[1:]
    n_d = len(dils)

    def body(do_ref, o_ref, dl_ref, *rest):
        dl_outs, do_outs = rest[:n_d], rest[n_d:2 * n_d]
        scr_dl, scr_do = rest[2 * n_d:]
        do = do_ref[...].astype(F32)
        prod = do * o_ref[...].astype(F32)
        per_head = [jnp.broadcast_to(jnp.sum(prod[:, h * HEAD_DIM:(h + 1) * HEAD_DIM], axis=-1, keepdims=True),
                                     (tm, HEAD_DIM)) for h in range(ATT_HEADS)]
        dl = jnp.concatenate(per_head, axis=1)
        dl_ref[...] = dl
        _fill(scr_dl, dl)
        _fill(scr_do, do)
        for d, dlo, doo in zip(dils, dl_outs, do_outs):
            _to_classes(scr_dl, dlo, d)
            _to_classes(scr_do, doo, d)

    blk = pl.BlockSpec((tm, D_ATT), lambda i: (i, 0))
    out = pl.pallas_call(
        body, name="att_delta", grid=(S // tm,), in_specs=[blk, blk],
        out_specs=[blk] + [_class_spec(tm, d, D_ATT) for d in dils] * 2,
        out_shape=[_sds((S, D_ATT), F32)] + [_sds((d, S // d, D_ATT), F32) for d in dils]
        + [_sds((d, S // d, D_ATT), BF16) for d in dils],
        scratch_shapes=[_row_scratch(tm, D_ATT), _row_scratch(tm, D_ATT)],
        compiler_params=_params("parallel"))(dac, att)
    delta = [out[0]] + [o.reshape(S, D_ATT) for o in out[1:1 + n_d]]
    do = [None] + [o.reshape(S, D_ATT) for o in out[1 + n_d:]]
    return delta, do


def _att_dq(qk, v_src, do_src, lg, delta, d, *, name):
    S = qk.shape[0]
    L = S // d
    tq = min(ATT_BLOCK, L)
    (v_arr, v_col), (do_arr, do_col) = v_src, do_src

    def body(q_ref, kp_ref, kc_ref, kn_ref, vp_ref, vc_ref, vn_ref, do_ref, lg_ref, dl_ref, dq_ref):
        i = pl.program_id(1)
        valid = _band_mask(i, tq, L, True)
        q, do = q_ref[...] * ATT_SCALE, do_ref[...]
        kwin = jnp.concatenate([kp_ref[...], kc_ref[...], kn_ref[...]], axis=0)
        vwin = jnp.concatenate([vp_ref[...], vc_ref[...], vn_ref[...]], axis=0)
        first = _first_head(tq)
        for ls in _lane_groups():
            k2, v2 = kwin[:, ls], vwin[:, ls]
            cols = (ls.start, ls.start + HEAD_DIM)
            s = [jnp.where(valid, _nt(t, k2), NEG_INF) for t in _split_pair(q[:, ls], first)]
            p = [jnp.exp(t - lg_ref[:, c:c + 1]) for t, c in zip(s, cols)]
            dp = [_nt(t, v2) for t in _split_pair(do[:, ls], first)]
            ds = [pp * (t - dl_ref[:, c:c + 1]) for pp, t, c in zip(p, dp, cols)]
            dq = [jnp.dot(t.astype(BF16), k2, preferred_element_type=F32) for t in ds]
            dq_ref[:, ls] = (jnp.where(first, dq[0], dq[1]) * ATT_SCALE).astype(dq_ref.dtype)

    _, qc, _ = _seq_specs(L, tq, 0)
    kp, kc, kn = _seq_specs(L, tq, 1)
    vp, vc, vn = _seq_specs(L, tq, v_col)
    _, doc, _ = _seq_specs(L, tq, do_col)
    row = pl.BlockSpec((tq, D_ATT), lambda r, i: (r * (L // tq) + i, 0))
    return pl.pallas_call(
        body, name=name, grid=(d, L // tq),
        in_specs=[qc, kp, kc, kn, vp, vc, vn, doc, row, row], out_specs=row,
        out_shape=_sds((S, D_ATT), BF16),
        compiler_params=_params("parallel", "parallel"))(qk, qk, qk, qk, v_arr, v_arr, v_arr, do_arr, lg, delta)


def _att_dkv(qk, v_src, do_src, lg, delta, d, *, name):
    S = qk.shape[0]
    L = S // d
    tk = min(ATT_BLOCK, L)
    (v_arr, v_col), (do_arr, do_col) = v_src, do_src

    def body(k_ref, v_ref, qp_ref, qc_ref, qn_ref, dop_ref, doc_ref, don_ref,
             lgp_ref, lgc_ref, lgn_ref, dlp_ref, dlc_ref, dln_ref, dk_ref, dv_ref):
        i = pl.program_id(1)
        valid = _band_mask(i, tk, L, False)
        k, v = k_ref[...], v_ref[...]
        qwin = jnp.concatenate([qp_ref[...], qc_ref[...], qn_ref[...]], axis=0) * ATT_SCALE
        dowin = jnp.concatenate([dop_ref[...], doc_ref[...], don_ref[...]], axis=0)

        def column(refs, c0):
            return jnp.concatenate([r[:, c0:c0 + 1] for r in refs], axis=0)

        first_w, first_k = _first_head(tk + 2 * HALF), _first_head(tk)
        for ls in _lane_groups():
            k2, v2, q2, do2 = k[:, ls], v[:, ls], qwin[:, ls], dowin[:, ls]
            cols = (ls.start, ls.start + HEAD_DIM)
            lgw = [column((lgp_ref, lgc_ref, lgn_ref), c) for c in cols]
            dlw = [column((dlp_ref, dlc_ref, dln_ref), c) for c in cols]
            s = [jnp.where(valid, _nt(t, k2), NEG_INF) for t in _split_pair(q2, first_w)]
            p = [jnp.exp(t - l) for t, l in zip(s, lgw)]
            dp = [_nt(t, v2) for t in _split_pair(do2, first_w)]
            ds = [pp * (t - l) for pp, t, l in zip(p, dp, dlw)]
            dv = [_tn(pp.astype(BF16), do2) for pp in p]
            dk = [_tn(t.astype(BF16), q2) for t in ds]
            dk_ref[:, ls] = jnp.where(first_k, dk[0], dk[1]).astype(dk_ref.dtype)
            dv_ref[:, ls] = jnp.where(first_k, dv[0], dv[1]).astype(dv_ref.dtype)

    _, kc, _ = _seq_specs(L, tk, 1)
    _, vc, _ = _seq_specs(L, tk, v_col)
    qp, qc, qn = _seq_specs(L, tk, 0)
    dop, doc, don = _seq_specs(L, tk, do_col)
    rp, rc, rn = _seq_specs(L, tk, 0)
    out = pl.BlockSpec((tk, D_ATT), lambda r, i: (r * (L // tk) + i, 0))
    return pl.pallas_call(
        body, name=name, grid=(d, L // tk),
        in_specs=[kc, vc, qp, qc, qn, dop, doc, don, rp, rc, rn, rp, rc, rn], out_specs=[out, out],
        out_shape=[_sds((S, D_ATT), BF16)] * 2,
        compiler_params=_params("parallel", "parallel"))(
            qk, v_arr, qk, qk, qk, do_arr, do_arr, do_arr, lg, lg, lg, delta, delta, delta)


def _sigmoid(x):
    return 1.0 / (1.0 + jnp.exp(-x))


def _halo_specs(S, T, width, col):
    last = S // HALO - 1
    per = T // HALO
    centre = pl.BlockSpec((T, width), lambda i: (i, col))
    prev = pl.BlockSpec((HALO, width), lambda i: (jnp.maximum(i * per - 1, 0), col))
    nxt = pl.BlockSpec((HALO, width), lambda i: (jnp.minimum((i + 1) * per, last), col))
    return prev, centre, nxt


def _window_scratch(T, C):
    return pltpu.VMEM((8, T + 2 * HALO, C), F32)


def _fill_window(buf, prev, centre, nxt, T):
    buf[0, 0:HALO, :] = prev
    buf[0, HALO:HALO + T, :] = centre
    buf[0, HALO + T:, :] = nxt
    rows = T + 2 * HALO - 8
    for s in range(1, 8):
        buf[s, 0:rows, :] = buf[0, s:s + rows, :]


def _tap_reads(buf, first_off, step, r0, ls):
    by_slab = {}
    for k in range(CONV_WIDTH):
        off = first_off + step * k
        by_slab.setdefault(off % 8, []).append((k, off - off % 8))
    for s, taps in by_slab.items():
        lo = min(a for _, a in taps)
        hi = max(a for _, a in taps)
        rows = buf[s, pl.ds(lo + r0, CONV_ROWS + hi - lo), ls]
        for k, a in taps:
            yield k, rows[a - lo:a - lo + CONV_ROWS]


def _depthwise(buf, w_ref, out_ref, T, C, first_off, step):
    def row_tile(t, carry):
        r0 = pl.multiple_of(t * CONV_ROWS, CONV_ROWS)
        for c0 in range(0, C, LANES):
            ls = slice(c0, c0 + LANES)
            acc = jnp.zeros((CONV_ROWS, LANES), F32)
            for k, rows in _tap_reads(buf, first_off, step, r0, ls):
                acc = acc + rows * w_ref[k:k + 1, ls]
            out_ref[pl.ds(r0, CONV_ROWS), ls] = acc
        return carry

    lax.fori_loop(0, T // CONV_ROWS, row_tile, 0)


def _conv_fwd(y, conv_w32, conv_b, ln_g, ln_b, *, T=512):
    S = y.shape[0]
    T = min(T, S)
    nblk = S // T
    C = D_CONV

    def body(ap, ac, an, gp, gc, gn, w_ref, b_ref, lg_ref, lb_ref, cv_ref, u1_ref, buf):
        i = pl.program_id(0)

        def glu(a_ref, g_ref):
            return a_ref[...].astype(F32) * _sigmoid(g_ref[...].astype(F32))

        _fill_window(buf, jnp.where(i > 0, glu(ap, gp), 0.0), glu(ac, gc),
                     jnp.where(i < nblk - 1, glu(an, gn), 0.0), T)
        _depthwise(buf, w_ref, u1_ref, T, C, HALO - CONV_PAD, 1)
        u1 = u1_ref[...] + b_ref[...]
        u1_ref[...] = u1
        mu = jnp.mean(u1, axis=-1, keepdims=True)
        xc = u1 - mu
        rstd = lax.rsqrt(jnp.mean(xc * xc, axis=-1, keepdims=True) + EPS)
        u2 = xc * rstd * lg_ref[...] + lb_ref[...]
        cv_ref[...] = (u2 * _sigmoid(u2)).astype(cv_ref.dtype)

    ap, ac, an = _halo_specs(S, T, C, 3)
    gp, gc, gn = _halo_specs(S, T, C, 4)
    vec = pl.BlockSpec((1, C), lambda i: (0, 0))
    out = pl.BlockSpec((T, C), lambda i: (i, 0))
    return pl.pallas_call(
        body, name="conv_fwd", grid=(nblk,),
        in_specs=[ap, ac, an, gp, gc, gn, pl.BlockSpec((32, C), lambda i: (0, 0)), vec, vec, vec],
        out_specs=[out, out], out_shape=[_sds((S, C), BF16), _sds((S, C), F32)],
        scratch_shapes=[_window_scratch(T, C)],
        compiler_params=_params("parallel"))(y, y, y, y, y, y, conv_w32, conv_b, ln_g, ln_b)


def _conv_bwd(dac, u1, y, conv_w32, ln_g, ln_b, *, T=512):
    S = y.shape[0]
    T = min(T, S)
    nblk = S // T
    C = D_CONV

    def body(dp, dc, dn, up, uc, un, ap, ac, an, gp, gc, gn, w_ref, lg_ref, lb_ref,
             dag_ref, dw_ref, dsm_ref, bufd, bufu, du0_scr, dw_acc):
        i = pl.program_id(0)
        lg = lg_ref[...]

        def du1_of(dcv_ref, u1_ref):
            u1 = u1_ref[...]
            mu = jnp.mean(u1, axis=-1, keepdims=True)
            xc = u1 - mu
            rstd = lax.rsqrt(jnp.mean(xc * xc, axis=-1, keepdims=True) + EPS)
            xhat = xc * rstd
            u2 = xhat * lg + lb_ref[...]
            sg = _sigmoid(u2)
            du2 = dcv_ref[...].astype(F32) * (sg * (1.0 + u2 * (1.0 - sg)))
            dxh = du2 * lg
            du1 = rstd * (dxh - jnp.mean(dxh, axis=-1, keepdims=True)
                          - xhat * jnp.mean(dxh * xhat, axis=-1, keepdims=True))
            return du1, du2, xhat

        def glu(a_ref, g_ref):
            return a_ref[...].astype(F32) * _sigmoid(g_ref[...].astype(F32))

        @pl.when(i == 0)
        def _():
            dw_ref[...] = jnp.zeros_like(dw_ref)
            dsm_ref[...] = jnp.zeros_like(dsm_ref)

        du1_c, du2_c, xhat_c = du1_of(dc, uc)
        dsm_ref[0:1, :] += jnp.sum(du1_c, axis=0, keepdims=True)
        dsm_ref[1:2, :] += jnp.sum(du2_c * xhat_c, axis=0, keepdims=True)
        dsm_ref[2:3, :] += jnp.sum(du2_c, axis=0, keepdims=True)
        _fill_window(bufd, jnp.where(i > 0, du1_of(dp, up)[0], 0.0), du1_c,
                     jnp.where(i < nblk - 1, du1_of(dn, un)[0], 0.0), T)
        _fill_window(bufu, jnp.where(i > 0, glu(ap, gp), 0.0), glu(ac, gc),
                     jnp.where(i < nblk - 1, glu(an, gn), 0.0), T)

        _depthwise(bufd, w_ref, du0_scr, T, C, HALO + CONV_PAD, -1)
        dw_acc[...] = jnp.zeros_like(dw_acc)

        def dw_tile(t, carry):
            r0 = pl.multiple_of(t * CONV_ROWS, CONV_ROWS)
            for c0 in range(0, C, LANES):
                ls = slice(c0, c0 + LANES)
                d = bufd[0, pl.ds(HALO + r0, CONV_ROWS), ls]
                for k, rows in _tap_reads(bufu, HALO - CONV_PAD, 1, r0, ls):
                    prod = d * rows
                    part = prod[0:8]
                    for j in range(8, CONV_ROWS, 8):
                        part = part + prod[j:j + 8]
                    dw_acc[k, :, ls] += part
            return carry

        lax.fori_loop(0, T // CONV_ROWS, dw_tile, 0)
        for k in range(CONV_WIDTH):
            dw_ref[k:k + 1, :] += jnp.sum(dw_acc[k], axis=0, keepdims=True)
        du0 = du0_scr[...]
        a = ac[...].astype(F32)
        sg = _sigmoid(gc[...].astype(F32))
        dag_ref[:, 0:C] = (du0 * sg).astype(dag_ref.dtype)
        dag_ref[:, C:] = (du0 * a * sg * (1.0 - sg)).astype(dag_ref.dtype)

    dp, dc, dn = _halo_specs(S, T, C, 1)
    up, uc, un = _halo_specs(S, T, C, 0)
    ap, ac, an = _halo_specs(S, T, C, 3)
    gp, gc, gn = _halo_specs(S, T, C, 4)
    vec = pl.BlockSpec((1, C), lambda i: (0, 0))
    return pl.pallas_call(
        body, name="conv_bwd", grid=(nblk,),
        in_specs=[dp, dc, dn, up, uc, un, ap, ac, an, gp, gc, gn,
                  pl.BlockSpec((32, C), lambda i: (0, 0)), vec, vec],
        out_specs=[pl.BlockSpec((T, 2 * C), lambda i: (i, 0)), pl.BlockSpec((32, C), lambda i: (0, 0)),
                   pl.BlockSpec((8, C), lambda i: (0, 0))],
        out_shape=[_sds((S, 2 * C), BF16), _sds((32, C), F32), _sds((8, C), F32)],
        scratch_shapes=[_window_scratch(T, C), _window_scratch(T, C), pltpu.VMEM((T, C), F32),
                        pltpu.VMEM((CONV_WIDTH, 8, C), F32)],
        compiler_params=_params("arbitrary"))(dac, dac, dac, u1, u1, u1, y, y, y, y, y, y, conv_w32, ln_g, ln_b)


def _xatt_fwd(xq, xk, xv, *, tm=512):
    S = xq.shape[0]
    M = xk.shape[0]
    tm = min(tm, S)
    scale = XATT_HEAD_DIM ** -0.5

    def body(q_ref, k_ref, v_ref, o_ref):
        for h in range(XATT_HEADS):
            sl = slice(h * XATT_HEAD_DIM, (h + 1) * XATT_HEAD_DIM)
            s = _nt(q_ref[:, sl], k_ref[:, sl]) * scale
            e = jnp.exp(s - jnp.max(s, axis=-1, keepdims=True))
            p = e / jnp.sum(e, axis=-1, keepdims=True)
            o_ref[:, sl] = jnp.dot(p.astype(BF16), v_ref[:, sl], preferred_element_type=F32).astype(o_ref.dtype)

    row = pl.BlockSpec((tm, D_MODEL), lambda i: (i, 0))
    full = pl.BlockSpec((M, D_MODEL), lambda i: (0, 0))
    return pl.pallas_call(
        body, name="xatt_fwd", grid=(S // tm,), in_specs=[row, full, full], out_specs=row,
        out_shape=_sds((S, D_MODEL), BF16), compiler_params=_params("parallel"))(xq, xk, xv)


def _xatt_bwd(xq, xk, xv, dxo, *, tm=512):
    S = xq.shape[0]
    M = xk.shape[0]
    tm = min(tm, S)
    scale = XATT_HEAD_DIM ** -0.5

    def body(q_ref, k_ref, v_ref, do_ref, dq_ref, dk_ref, dv_ref):
        i = pl.program_id(0)

        @pl.when(i == 0)
        def _():
            dk_ref[...] = jnp.zeros_like(dk_ref)
            dv_ref[...] = jnp.zeros_like(dv_ref)

        for h in range(XATT_HEADS):
            sl = slice(h * XATT_HEAD_DIM, (h + 1) * XATT_HEAD_DIM)
            q, k, v, do = q_ref[:, sl], k_ref[:, sl], v_ref[:, sl], do_ref[:, sl]
            s = _nt(q, k) * scale
            e = jnp.exp(s - jnp.max(s, axis=-1, keepdims=True))
            p = e / jnp.sum(e, axis=-1, keepdims=True)
            dp = _nt(do, v)
            ds = p * (dp - jnp.sum(dp * p, axis=-1, keepdims=True))
            dsb = ds.astype(BF16)
            dq_ref[:, sl] = (jnp.dot(dsb, k, preferred_element_type=F32) * scale).astype(dq_ref.dtype)
            dv_ref[:, sl] += _tn(p.astype(BF16), do)
            dk_ref[:, sl] += _tn(dsb, q) * scale

    row = pl.BlockSpec((tm, D_MODEL), lambda i: (i, 0))
    full = pl.BlockSpec((M, D_MODEL), lambda i: (0, 0))
    return pl.pallas_call(
        body, name="xatt_bwd", grid=(S // tm,), in_specs=[row, full, full, row], out_specs=[row, full, full],
        out_shape=[_sds((S, D_MODEL), BF16), _sds((M, D_MODEL), F32), _sds((M, D_MODEL), F32)],
        compiler_params=_params("arbitrary"))(xq, xk, xv, dxo)


def _row_tile(R):
    for t in (256, 128, 64, 32, 16, 8):
        if R % t == 0:
            return t
    return R


def _sum_partials(own, recv, me, *, name):
    _, R, C = own.shape
    t = _row_tile(R)

    def body(me_ref, own_ref, r_ref, o_ref):
        o_ref[...] = ((own_ref[...].astype(F32) + r_ref[0].astype(F32)) + r_ref[1].astype(F32)) + r_ref[2].astype(F32)

    return pl.pallas_call(
        body, name=name,
        grid_spec=pltpu.PrefetchScalarGridSpec(
            num_scalar_prefetch=1, grid=(R // t,),
            in_specs=[pl.BlockSpec((None, t, C), lambda i, me_ref: (me_ref[0], i, 0)),
                      pl.BlockSpec((3, t, C), lambda i, me_ref: (0, i, 0))],
            out_specs=pl.BlockSpec((t, C), lambda i, me_ref: (i, 0))),
        out_shape=_sds((R, C), F32), compiler_params=_params("parallel"))(me, own, recv)


def _adamw_math(w, g, m, v):
    m2 = ADAM_B1 * m + (1.0 - ADAM_B1) * g
    v2 = ADAM_B2 * v + (1.0 - ADAM_B2) * (g * g)
    m_hat = m2 / (1.0 - ADAM_B1 ** ADAM_STEP)
    v_hat = v2 / (1.0 - ADAM_B2 ** ADAM_STEP)
    delta = -ADAM_LR * (m_hat / (jnp.sqrt(v_hat) + ADAM_EPS) + ADAM_WD * w)
    return delta, m2, v2


def _adamw(parts, w, m, v, *, name):
    R, C = w.shape
    t = _row_tile(R)
    n = len(parts)

    def body(*refs):
        w_ref, m_ref, v_ref = refs[n:n + 3]
        g_ref, d_ref, m2_ref, v2_ref = refs[n + 3:]
        g = refs[0][...]
        for r in refs[1:n]:
            g = g + r[...]
        delta, m2, v2 = _adamw_math(w_ref[...], g, m_ref[...], v_ref[...])
        g_ref[...] = g
        d_ref[...] = delta
        m2_ref[...] = m2
        v2_ref[...] = v2

    blk = pl.BlockSpec((t, C), lambda i: (i, 0))
    return pl.pallas_call(
        body, name=name, grid=(R // t,), in_specs=[blk] * (n + 3), out_specs=[blk] * 4,
        out_shape=[_sds((R, C), F32)] * 4, compiler_params=_params("parallel"))(*parts, w, m, v)


def _sum_devices(gathered):
    _, R, C = gathered.shape

    def body(g_ref, o_ref):
        acc = g_ref[0]
        for k in range(1, N_DEV):
            acc = acc + g_ref[k]
        o_ref[...] = acc

    return pl.pallas_call(body, name="sum_devices", out_shape=_sds((R, C), F32))(gathered)


def _chip_peers():
    x, y = lax.axis_index("x"), lax.axis_index("y")
    return [(1 - x, y), (x, 1 - y), (1 - x, 1 - y)]


def _gather_weights(shards):
    n = len(shards)

    def body(*refs):
        ins, outs = refs[:n], refs[n:2 * n]
        send_sems, recv_sems, loc_sems = refs[2 * n:]
        x, y, c = lax.axis_index("x"), lax.axis_index("y"), lax.axis_index("c")
        me = 2 * x + y
        peers = _chip_peers()
        started = []
        for t in range(n):
            loc = pltpu.make_async_copy(ins[t], outs[t].at[me], loc_sems.at[t])
            loc.start()
            started.append(loc)
        sends = []
        for t in range(n):
            for k, (px, py) in enumerate(peers):
                cp = pltpu.make_async_remote_copy(
                    src_ref=ins[t], dst_ref=outs[t].at[me], send_sem=send_sems.at[t, k],
                    recv_sem=recv_sems.at[t, k], device_id=(px, py, c), device_id_type=MESH)
                cp.start()
                sends.append(cp)
        for t in range(n):
            for k, (px, py) in enumerate(peers):
                pltpu.make_async_remote_copy(
                    src_ref=ins[t], dst_ref=outs[t].at[2 * px + py], send_sem=send_sems.at[t, k],
                    recv_sem=recv_sems.at[t, k], device_id=(px, py, c), device_id_type=MESH).wait_recv()
        for cp in sends:
            cp.wait_send()
        for loc in started:
            loc.wait()

    any_spec = pl.BlockSpec(memory_space=pl.ANY)
    return pl.pallas_call(
        body, name="gather_weights", in_specs=[any_spec] * n, out_specs=[any_spec] * n,
        out_shape=[_sds((N_CHIPS,) + s.shape, s.dtype) for s in shards],
        scratch_shapes=[pltpu.SemaphoreType.DMA((n, 3)), pltpu.SemaphoreType.DMA((n, 3)),
                        pltpu.SemaphoreType.DMA((n,))])(*shards)


HBM_SPEC = pl.BlockSpec(memory_space=pltpu.HBM)
SEM_SPEC = pl.BlockSpec(memory_space=pltpu.SEMAPHORE)


def _exchange_start(mode, srcs, zones, *, name):
    n = len(srcs)

    def body(*refs):
        ins, lands = refs[:n], refs[n:2 * n]
        send_sems, recv_sems = refs[2 * n:3 * n], refs[3 * n:4 * n]
        token = refs[-1]
        c = lax.axis_index("c")
        mine = 2 * lax.axis_index("x") + lax.axis_index("y")
        for t in range(n):
            for k, (px, py) in enumerate(_chip_peers()):
                if mode == "gather":
                    s, d = ins[t], lands[t].at[mine]
                else:
                    s, d = ins[t].at[2 * px + py], lands[t].at[k]
                pltpu.make_async_remote_copy(src_ref=s, dst_ref=d, send_sem=send_sems[t], recv_sem=recv_sems[t],
                                             device_id=(px, py, c), device_id_type=MESH).start()
        token[...] = jnp.zeros_like(token)

    hbm = lambda a: pltpu.with_memory_space_constraint(a, pltpu.HBM)
    out = pl.pallas_call(
        body, name=name,
        in_specs=[HBM_SPEC] * (2 * n),
        out_specs=[SEM_SPEC] * (2 * n) + [HBM_SPEC] * (2 * n) + [pl.BlockSpec(memory_space=pltpu.VMEM)],
        out_shape=[pltpu.SemaphoreType.DMA(())] * (2 * n)
        + [pltpu.HBM(a.shape, a.dtype) for a in list(srcs) + list(zones)] + [_sds((8, LANES), F32)],
        input_output_aliases={i: 2 * n + i for i in range(2 * n)},
        compiler_params=pltpu.CompilerParams(has_side_effects=pltpu.SideEffectType.DATAFLOW_SIDE_EFFECTING),
    )(*[hbm(a) for a in list(srcs) + list(zones)])
    return out[:n], out[n:2 * n], out[2 * n:3 * n], out[3 * n:4 * n], out[-1]


def _exchange_wait(started, after, *, name):
    send_sems, recv_sems, srcs, zones, _ = started
    n = len(srcs)

    def body(*refs):
        lands = refs[n:2 * n]
        send_refs, recv_refs = refs[2 * n:3 * n], refs[3 * n:4 * n]
        me = (lax.axis_index("x"), lax.axis_index("y"), lax.axis_index("c"))
        for t in range(n):
            three = lands[t].at[pl.ds(0, N_CHIPS - 1)]
            cp = pltpu.make_async_remote_copy(src_ref=three, dst_ref=three, send_sem=send_refs[t],
                                              recv_sem=recv_refs[t], device_id=me, device_id_type=MESH)
            cp.wait_send()
            cp.wait_recv()

    out = pl.pallas_call(
        body, name=name,
        in_specs=[HBM_SPEC] * (2 * n) + [SEM_SPEC] * (2 * n) + [pl.BlockSpec(memory_space=pl.ANY)],
        out_specs=[HBM_SPEC] * (2 * n),
        out_shape=[pltpu.HBM(a.shape, a.dtype) for a in list(srcs) + list(zones)],
        input_output_aliases={i: i for i in range(2 * n)},
        compiler_params=pltpu.CompilerParams(has_side_effects=pltpu.SideEffectType.DATAFLOW_SIDE_EFFECTING),
    )(*srcs, *zones, *send_sems, *recv_sems, after)
    return out[:n], out[n:]


def _scatter_grads(grads, small):
    n = len(grads)

    def body(*refs):
        ins, small_ref = refs[:n], refs[n]
        outs, gath_ref = refs[n + 1:2 * n + 1], refs[2 * n + 1]
        send_sems, recv_sems, ssend, srecv, loc_sem = refs[2 * n + 2:]
        x, y, c = lax.axis_index("x"), lax.axis_index("y"), lax.axis_index("c")
        me = 4 * x + 2 * y + c
        peers = _chip_peers()
        flips = [(fx, fy, fc) for fx in (0, 1) for fy in (0, 1) for fc in (0, 1)][1:]

        def flipped(fx, fy, fc):
            return (1 - x if fx else x, 1 - y if fy else y, 1 - c if fc else c)

        loc = pltpu.make_async_copy(small_ref, gath_ref.at[me], loc_sem)
        loc.start()
        sends = []
        for j, (fx, fy, fc) in enumerate(flips):
            cp = pltpu.make_async_remote_copy(
                src_ref=small_ref, dst_ref=gath_ref.at[me], send_sem=ssend.at[j], recv_sem=srecv.at[j],
                device_id=flipped(fx, fy, fc), device_id_type=MESH)
            cp.start()
            sends.append(cp)
        for t in range(n):
            for k, (px, py) in enumerate(peers):
                cp = pltpu.make_async_remote_copy(
                    src_ref=ins[t].at[2 * px + py], dst_ref=outs[t].at[k], send_sem=send_sems.at[t, k],
                    recv_sem=recv_sems.at[t, k], device_id=(px, py, c), device_id_type=MESH)
                cp.start()
                sends.append(cp)
        for j, (fx, fy, fc) in enumerate(flips):
            px, py, pc = flipped(fx, fy, fc)
            pltpu.make_async_remote_copy(
                src_ref=small_ref, dst_ref=gath_ref.at[4 * px + 2 * py + pc], send_sem=ssend.at[j],
                recv_sem=srecv.at[j], device_id=(px, py, pc), device_id_type=MESH).wait_recv()
        for t in range(n):
            for k, (px, py) in enumerate(peers):
                pltpu.make_async_remote_copy(
                    src_ref=ins[t].at[2 * px + py], dst_ref=outs[t].at[k], send_sem=send_sems.at[t, k],
                    recv_sem=recv_sems.at[t, k], device_id=(px, py, c), device_id_type=MESH).wait_recv()
        for cp in sends:
            cp.wait_send()
        loc.wait()

    any_spec = pl.BlockSpec(memory_space=pl.ANY)
    out = pl.pallas_call(
        body, name="scatter_grads", in_specs=[any_spec] * (n + 1), out_specs=[any_spec] * (n + 1),
        out_shape=[_sds((3,) + g.shape[1:], g.dtype) for g in grads] + [_sds((N_DEV,) + small.shape, small.dtype)],
        scratch_shapes=[pltpu.SemaphoreType.DMA((n, 3)), pltpu.SemaphoreType.DMA((n, 3)),
                        pltpu.SemaphoreType.DMA((N_DEV - 1,)), pltpu.SemaphoreType.DMA((N_DEV - 1,)),
                        pltpu.SemaphoreType.DMA])(*grads, small)
    return out[:n], out[n]


def _swap_with_sibling(parts):
    n = len(parts)

    def body(*refs):
        ins, outs = refs[:n], refs[n:2 * n]
        send_sems, recv_sems = refs[2 * n:]
        sib = (lax.axis_index("x"), lax.axis_index("y"), 1 - lax.axis_index("c"))
        cps = []
        for t in range(n):
            cp = pltpu.make_async_remote_copy(
                src_ref=ins[t], dst_ref=outs[t], send_sem=send_sems.at[t], recv_sem=recv_sems.at[t],
                device_id=sib, device_id_type=MESH)
            cp.start()
            cps.append(cp)
        for cp in cps:
            cp.wait()

    any_spec = pl.BlockSpec(memory_space=pl.ANY)
    return pl.pallas_call(
        body, name="swap_with_sibling", in_specs=[any_spec] * n, out_specs=[any_spec] * n,
        out_shape=[_sds(p.shape, p.dtype) for p in parts],
        scratch_shapes=[pltpu.SemaphoreType.DMA((n,)), pltpu.SemaphoreType.DMA((n,))])(*parts)


BIG = ("w_in", "w_out", "w_xq", "w_xk", "w_xv", "w_xo", "w_up", "w_down")
COL_SHARDED = ("w_in", "w_up")


def _as_matrix(name, w4):
    if name in COL_SHARDED:
        return w4
    return w4.reshape(1, w4.shape[0] * w4.shape[1], w4.shape[2])


def _transposed(w3):
    nsh, K, n = w3.shape
    return jnp.swapaxes(w3, 1, 2).reshape(1, nsh * n, K)


def _shard_layout(name, g):
    if name in COL_SHARDED:
        return g
    return g.reshape(N_CHIPS, g.shape[0] * g.shape[1] // N_CHIPS, g.shape[2])


def _local_step(x, mem, target, w_in, conv_w32, vecs, comm):
    S = x.shape[0]
    tables = _rope_tables(S)

    xn = _rms_fwd(x, vecs["norm_mix_g"], name="rms_mix")
    y = _mm_nn(xn, w_in, name="mm_in", tn=640)
    qk, v_perm = _rope_fwd(y, tables)
    v_src = [(y, 2)] + [(v, 0) for v in v_perm[1:]]
    outs, lses = zip(*[_att_fwd(qk[p], v_src[p], d, name=f"att_fwd_d{d}") for p, d in enumerate(DILATIONS)])
    att, lg = _att_combine(outs, lses)
    cv, u1 = _conv_fwd(y, conv_w32, vecs["conv_b"], vecs["conv_ln_g"], vecs["conv_ln_b"])
    mix = jnp.concatenate([att, cv], axis=1)
    Wm = {k: _as_matrix(k, v) for k, v in comm["rest"](mix).items()}
    Wm["w_in"] = w_in
    h1 = _mm_nn(mix, Wm["w_out"], name="mm_out", out_dtype=F32, res=x)
    hn = _rms_fwd(h1, vecs["norm_x_g"], name="rms_x")
    xq = _mm_nn(hn, Wm["w_xq"], name="mm_xq")
    mn = _rms_fwd(mem, vecs["norm_mem_g"], name="rms_mem")
    xk = _mm_nn(mn, Wm["w_xk"], name="mm_xk")
    xv = _mm_nn(mn, Wm["w_xv"], name="mm_xv")
    xo = _xatt_fwd(xq, xk, xv)
    h2 = _mm_nn(xo, Wm["w_xo"], name="mm_xo", out_dtype=F32, res=h1)
    hm = _rms_fwd(h2, vecs["norm_mlp_g"], name="rms_mlp")
    relu_up, act = _mm_nn(hm, Wm["w_up"], name="mm_up", relu2=True)
    h3 = _mm_nn(act, Wm["w_down"], name="mm_down", out_dtype=F32, res=h2, tm=512, tk=D_FF)

    dh3, dh3b, dg_final, loss = _loss_head(h3, vecs["norm_final_g"], target)
    g = {}
    g["w_down"] = _mm_tn(act, dh3b, 1, name="dw_down")
    dup = _mm_nt(dh3b, Wm["w_down"], name="d_act", out_dtype=BF16, mul=relu_up)
    g["w_up"] = _mm_tn(hm, dup, N_CHIPS, name="dw_up")
    sent = comm["send_mlp"]({k: _shard_layout(k, g[k]) for k in ("w_down", "w_up")})
    dhm = _mm_nn(dup, _transposed(Wm["w_up"]), name="d_hm", tm=512, tk=D_FF)
    dh2, dh2b, dg_mlp = _rms_bwd(dhm, h2, vecs["norm_mlp_g"] + sent[0:1, 0:1], dh3, name="rms_bwd_mlp")
    g["w_xo"] = _mm_tn(xo, dh2b, 1, name="dw_xo")
    dxo = _mm_nt(dh2b, Wm["w_xo"], name="d_xo", out_dtype=BF16)
    dxq, dxk, dxv = _xatt_bwd(xq, xk, xv, dxo)
    g["w_xq"] = _mm_tn(hn, dxq, 1, name="dw_xq")
    dhn = _mm_nt(dxq, Wm["w_xq"], name="d_hn", out_dtype=BF16)
    dh1, dh1b, dg_x = _rms_bwd(dhn, h1, vecs["norm_x_g"], dh2, name="rms_bwd_x")
    dxkb, dxvb = dxk.astype(BF16), dxv.astype(BF16)
    g["w_xk"] = _mm_tn(mn, dxkb, 1, name="dw_xk")
    g["w_xv"] = _mm_tn(mn, dxvb, 1, name="dw_xv")
    dmn = _mm_nt(jnp.concatenate([dxkb, dxvb], axis=1),
                 jnp.concatenate([Wm["w_xk"], Wm["w_xv"]], axis=2), name="d_mn", out_dtype=BF16)
    _, _, dg_mem = _rms_bwd(dmn, mem, vecs["norm_mem_g"], None, name="rms_bwd_mem")
    g["w_out"] = _mm_tn(mix, dh1b, 1, name="dw_out")
    sent = comm["send_att"]({k: _shard_layout(k, g[k]) for k in ("w_out", "w_xq", "w_xk", "w_xv", "w_xo")})
    dac = _mm_nt(dh1b, Wm["w_out"], name="d_mix", out_dtype=BF16)
    dag, dconv_w, dconv_small = _conv_bwd(dac, u1, y, conv_w32, vecs["conv_ln_g"] + sent[0:1, 0:1],
                                          vecs["conv_ln_b"])
    delta, do_perm = _att_delta(dac, att)
    do_src = [(dac, 0)] + [(t, 0) for t in do_perm[1:]]
    dq = [_att_dq(qk[p], v_src[p], do_src[p], lg[p], delta[p], d, name=f"att_dq_d{d}")
          for p, d in enumerate(DILATIONS)]
    dk, dv = zip(*[_att_dkv(qk[p], v_src[p], do_src[p], lg[p], delta[p], d, name=f"att_dkv_d{d}")
                   for p, d in enumerate(DILATIONS)])
    dy = _assemble_dy(dq, dk, dv, dag, tables)
    g_in = _mm_tn(xn, dy, N_CHIPS, name="dw_in", tn=640)
    dxn = _mm_nn(dy, _transposed(Wm["w_in"]), name="d_xn", tm=512, tk=D_IN)
    grad_x, _, dg_mix = _rms_bwd(dxn, x, vecs["norm_mix_g"], dh1, name="rms_bwd_mix")

    small = dict(conv_w=dconv_w, conv_small=dconv_small, norm_mix_g=dg_mix, norm_x_g=dg_x, norm_mem_g=dg_mem,
                 norm_mlp_g=dg_mlp, norm_final_g=dg_final, loss=loss)
    return grad_x, g_in, small


SMALL_ORDER = ("conv_w", "conv_small", "norm_mix_g", "norm_x_g", "norm_mem_g", "norm_mlp_g", "norm_final_g", "loss")


def _pack_small(small):
    rows, offs, pos = [], {}, 0
    for k in SMALL_ORDER:
        a = small[k]
        a = a.reshape(a.shape[0] * a.shape[1] // SMALL_W, SMALL_W)
        pad = (-a.shape[0]) % 8
        if pad:
            a = jnp.pad(a, ((0, pad), (0, 0)))
        rows.append(a)
        offs[k] = pos
        pos += a.shape[0]
    return jnp.concatenate(rows, axis=0), offs


def kernel(x, mem, norm_mix_g, w_in, conv_w, conv_b, conv_ln_g, conv_ln_b, w_out, norm_x_g, norm_mem_g, w_xq, w_xk, w_xv, w_xo, norm_mlp_g, w_up, w_down, norm_final_g, loss_target, m_norm_mix_g, m_w_in, m_conv_w, m_conv_b, m_conv_ln_g, m_conv_ln_b, m_w_out, m_norm_x_g, m_norm_mem_g, m_w_xq, m_w_xk, m_w_xv, m_w_xo, m_norm_mlp_g, m_w_up, m_w_down, m_norm_final_g, v_norm_mix_g, v_w_in, v_conv_w, v_conv_b, v_conv_ln_g, v_conv_ln_b, v_w_out, v_norm_x_g, v_norm_mem_g, v_w_xq, v_w_xk, v_w_xv, v_w_xo, v_norm_mlp_g, v_w_up, v_w_down, v_norm_final_g):
    names = ("norm_mix_g", "w_in", "conv_w", "conv_b", "conv_ln_g", "conv_ln_b", "w_out", "norm_x_g", "norm_mem_g",
             "w_xq", "w_xk", "w_xv", "w_xo", "norm_mlp_g", "w_up", "w_down", "norm_final_g")
    wts = dict(zip(names, (norm_mix_g, w_in, conv_w, conv_b, conv_ln_g, conv_ln_b, w_out, norm_x_g, norm_mem_g,
                           w_xq, w_xk, w_xv, w_xo, norm_mlp_g, w_up, w_down, norm_final_g)))
    mom = dict(zip(names, (m_norm_mix_g, m_w_in, m_conv_w, m_conv_b, m_conv_ln_g, m_conv_ln_b, m_w_out, m_norm_x_g,
                           m_norm_mem_g, m_w_xq, m_w_xk, m_w_xv, m_w_xo, m_norm_mlp_g, m_w_up, m_w_down, m_norm_final_g)))
    var = dict(zip(names, (v_norm_mix_g, v_w_in, v_conv_w, v_conv_b, v_conv_ln_g, v_conv_ln_b, v_w_out, v_norm_x_g,
                           v_norm_mem_g, v_w_xq, v_w_xk, v_w_xv, v_w_xo, v_norm_mlp_g, v_w_up, v_w_down, v_norm_final_g)))
    chip = 2 * lax.axis_index("x") + lax.axis_index("y")

    conv_w_pad = jnp.pad(wts["conv_w"][0], ((0, 1), (0, 0)))
    w_in_all, conv_w_all = _gather_weights([wts["w_in"][0].astype(BF16), conv_w_pad])
    conv_w32 = jnp.transpose(conv_w_all, (1, 0, 2)).reshape(32, D_CONV)
    rest = tuple(k for k in BIG if k != "w_in")
    rest_shards = [wts[k][0].astype(BF16) for k in rest]
    behind_first = lax.bitcast_convert_type(w_in_all[0, 0, 0], jnp.uint16).astype(jnp.int32) & 0
    rest_zones = [lax.dynamic_update_slice(lax.empty((N_CHIPS,) + s.shape, s.dtype), s[None],
                                           (chip + behind_first, 0, 0)) for s in rest_shards]
    gathering = _exchange_start("gather", rest_shards, rest_zones, name="gather_rest_start")
    sending = {}

    def wait_rest(after):
        _, zones = _exchange_wait(gathering, after, name="gather_rest_wait")
        return dict(zip(rest, zones))

    def send(group, grads):
        keys = tuple(grads)
        zones = [lax.empty((N_CHIPS - 1,) + grads[k].shape[1:], grads[k].dtype) for k in keys]
        sending[group] = (keys, _exchange_start("scatter", [grads[k] for k in keys], zones,
                                                name=f"scatter_{group}_start"))
        return sending[group][1][4]

    comm = dict(rest=wait_rest, send_mlp=lambda grads: send("mlp", grads), send_att=lambda grads: send("att", grads))
    vecs = {k: wts[k] for k in ("conv_b", "conv_ln_g", "conv_ln_b", "norm_x_g", "norm_mem_g", "norm_mlp_g")}
    vecs["norm_mix_g"] = wts["norm_mix_g"] + gathering[4][0:1, 0:1]
    vecs["norm_final_g"] = wts["norm_final_g"].reshape(1, D_MODEL)
    grad_x, g_in, small = _local_step(x[0], mem[0], loss_target[0], w_in_all, conv_w32, vecs, comm)

    packed, offs = _pack_small(small)
    recv_in, gath = _scatter_grads([g_in], packed)
    big, recv = {"w_in": g_in}, {"w_in": recv_in[0]}
    for group in ("mlp", "att"):
        keys, started = sending[group]
        srcs, zones = _exchange_wait(started, recv_in[0], name=f"scatter_{group}_wait")
        big.update(zip(keys, srcs))
        recv.update(zip(keys, zones))
    me_arr = jnp.reshape(chip, (1,)).astype(jnp.int32)
    sums = [_sum_partials(big[k], recv[k], me_arr, name=f"sum_{k}") for k in BIG]
    sib = _swap_with_sibling(sums)
    tot_small = _sum_devices(gath)

    res = {}
    for k, s_mine, s_sib in zip(BIG, sums, sib):
        res[k] = _adamw([s_mine, s_sib], wts[k][0], mom[k][0], var[k][0], name=f"adamw_{k}")

    def piece(key, nrows):
        return tot_small[offs[key]:offs[key] + nrows]

    def small_update(k, gfull):
        return _adamw([gfull], wts[k].reshape(gfull.shape), mom[k].reshape(gfull.shape),
                      var[k].reshape(gfull.shape), name=f"adamw_{k}")

    dcw = piece("conv_w", 32)[:CONV_WIDTH]
    dcw_mine = lax.dynamic_slice_in_dim(dcw, chip * (D_CONV // N_CHIPS), D_CONV // N_CHIPS, axis=1)
    res["conv_w"] = small_update("conv_w", dcw_mine)
    cs = piece("conv_small", 8)
    res["conv_b"] = small_update("conv_b", cs[0:1])
    res["conv_ln_g"] = small_update("conv_ln_g", cs[1:2])
    res["conv_ln_b"] = small_update("conv_ln_b", cs[2:3])
    for k in ("norm_mix_g", "norm_x_g", "norm_mem_g", "norm_mlp_g", "norm_final_g"):
        res[k] = small_update(k, piece(k, 8 * D_MODEL // SMALL_W).reshape(8, D_MODEL)[0:1])
    loss = piece("loss", 8)[0, 0]

    outs = [loss, grad_x[None]]
    for j in range(4):
        outs += [res[k][j].reshape(wts[k].shape) for k in names]
    return tuple(outs)
```

```python
import jax
import jax.numpy as jnp
from jax import lax
from jax.experimental import pallas as pl
from jax.experimental.pallas import tpu as pltpu

F32 = jnp.float32
BF16 = jnp.bfloat16
MESH = pl.DeviceIdType.MESH

D_MODEL = 1024
ATT_HEADS = 8
HEAD_DIM = 64
D_ATT = ATT_HEADS * HEAD_DIM
D_CONV = D_MODEL - D_ATT
DILATIONS = (1, 4, 16)
HALF = 64
ROPE_THETA = 500000.0
ROT_DIM = HEAD_DIM // 4
CONV_WIDTH = 31
CONV_PAD = (CONV_WIDTH - 1) // 2
XATT_HEADS = 4
XATT_HEAD_DIM = D_MODEL // XATT_HEADS
D_FF = 4 * D_MODEL
D_IN = 3 * D_ATT + 2 * D_CONV
EPS = 1e-6
NEG_INF = -1e30
N_CHIPS = 4
N_DEV = 8

ADAM_LR = 0.001
ADAM_B1 = 0.9
ADAM_B2 = 0.999
ADAM_EPS = 1e-08
ADAM_WD = 0.01
ADAM_STEP = 10

VMEM_LIMIT_V7X = 56 * 1024 * 1024
LANES = 128
HALO = 16
CONV_ROWS = 64
ATT_BLOCK = 128
SMALL_W = 512


def _params(*sem):
    return pltpu.CompilerParams(dimension_semantics=sem, vmem_limit_bytes=VMEM_LIMIT_V7X)


def _sds(shape, dtype):
    return jax.ShapeDtypeStruct(shape, dtype)


def _mm_nn(a, w3, *, name, out_dtype=BF16, res=None, relu2=False, tm=1024, tn=None, tk=1024):
    M, K = a.shape
    nsh, _, n = w3.shape
    tm, tk = min(tm, M), min(tk, K)
    tn = tn or min(n, 1024)
    npt, nk = n // tn, K // tk
    nj, N = nsh * npt, nsh * n
    n_out = 2 if relu2 else 1

    def body(*refs):
        a_ref, w_ref = refs[0], refs[1]
        pos = 2
        res_ref = None
        if res is not None:
            res_ref = refs[pos]
            pos += 1
        outs = refs[pos:pos + n_out]
        acc_ref = refs[pos + n_out] if nk > 1 else None

        def finish(acc):
            if res_ref is not None:
                acc = acc + res_ref[...]
            if relu2:
                r = jnp.maximum(acc, 0.0)
                outs[0][...] = r.astype(outs[0].dtype)
                outs[1][...] = (r * r).astype(outs[1].dtype)
            else:
                outs[0][...] = acc.astype(outs[0].dtype)

        part = jnp.dot(a_ref[...], w_ref[...], preferred_element_type=F32)
        if nk == 1:
            finish(part)
        else:
            k = pl.program_id(2)

            @pl.when(k == 0)
            def _():
                acc_ref[...] = part

            @pl.when(k > 0)
            def _():
                acc_ref[...] += part

            @pl.when(k == nk - 1)
            def _():
                finish(acc_ref[...])

    in_specs = [pl.BlockSpec((tm, tk), lambda i, j, k: (i, k)),
                pl.BlockSpec((None, tk, tn), lambda i, j, k: (j // npt, k, j % npt))]
    args = [a, w3]
    if res is not None:
        in_specs.append(pl.BlockSpec((tm, tn), lambda i, j, k: (i, j)))
        args.append(res)
    out_spec = pl.BlockSpec((tm, tn), lambda i, j, k: (i, j))
    out = pl.pallas_call(
        body, name=name, grid=(M // tm, nj, nk), in_specs=in_specs,
        out_specs=[out_spec] * n_out, out_shape=[_sds((M, N), out_dtype)] * n_out,
        scratch_shapes=[pltpu.VMEM((tm, tn), F32)] if nk > 1 else [],
        compiler_params=_params("parallel", "parallel", "arbitrary"))(*args)
    return tuple(out) if relu2 else out[0]


def _mm_nt(dy, w3, *, name, out_dtype=F32, mul=None, tm=1024, tn=None, tko=1024):
    M, N = dy.shape
    nsh, K, n = w3.shape
    tm, tko = min(tm, M), min(tko, K)
    tn = tn or min(n, 1024)
    npt = n // tn
    nj = nsh * npt

    def body(*refs):
        dy_ref, w_ref = refs[0], refs[1]
        pos = 2
        mul_ref = None
        if mul is not None:
            mul_ref = refs[pos]
            pos += 1
        out_ref = refs[pos]
        acc_ref = refs[pos + 1] if nj > 1 else None

        def finish(acc):
            if mul_ref is not None:
                acc = acc * (2.0 * mul_ref[...].astype(F32))
            out_ref[...] = acc.astype(out_ref.dtype)

        part = lax.dot_general(dy_ref[...], w_ref[...], (((1,), (1,)), ((), ())), preferred_element_type=F32)
        if nj == 1:
            finish(part)
        else:
            j = pl.program_id(2)

            @pl.when(j == 0)
            def _():
                acc_ref[...] = part

            @pl.when(j > 0)
            def _():
                acc_ref[...] += part

            @pl.when(j == nj - 1)
            def _():
                finish(acc_ref[...])

    in_specs = [pl.BlockSpec((tm, tn), lambda i, ko, j: (i, j)),
                pl.BlockSpec((None, tko, tn), lambda i, ko, j: (j // npt, ko, j % npt))]
    args = [dy, w3]
    if mul is not None:
        in_specs.append(pl.BlockSpec((tm, tko), lambda i, ko, j: (i, ko)))
        args.append(mul)
    return pl.pallas_call(
        body, name=name, grid=(M // tm, K // tko, nj), in_specs=in_specs,
        out_specs=pl.BlockSpec((tm, tko), lambda i, ko, j: (i, ko)), out_shape=_sds((M, K), out_dtype),
        scratch_shapes=[pltpu.VMEM((tm, tko), F32)] if nj > 1 else [],
        compiler_params=_params("parallel", "parallel", "arbitrary"))(*args)


def _mm_tn(a, dy, nsh, *, name, out_dtype=BF16, tm=2048, tk=1024, tn=None):
    M, K = a.shape
    N = dy.shape[1]
    n = N // nsh
    tm, tk = min(tm, M), min(tk, K)
    tn = tn or min(n, 1024)
    npt = n // tn
    nj, nm = nsh * npt, M // tm

    def body(a_ref, dy_ref, out_ref, acc_ref):
        m = pl.program_id(2)
        part = lax.dot_general(a_ref[...], dy_ref[...], (((0,), (0,)), ((), ())), preferred_element_type=F32)

        @pl.when(m == 0)
        def _():
            acc_ref[...] = part

        @pl.when(m > 0)
        def _():
            acc_ref[...] += part

        @pl.when(m == nm - 1)
        def _():
            out_ref[...] = acc_ref[...].astype(out_ref.dtype)

    return pl.pallas_call(
        body, name=name, grid=(K // tk, nj, nm),
        in_specs=[pl.BlockSpec((tm, tk), lambda kk, j, m: (m, kk)),
                  pl.BlockSpec((tm, tn), lambda kk, j, m: (m, j))],
        out_specs=pl.BlockSpec((None, tk, tn), lambda kk, j, m: (j // npt, kk, j % npt)),
        out_shape=_sds((nsh, K, n), out_dtype),
        scratch_shapes=[pltpu.VMEM((tk, tn), F32)],
        compiler_params=_params("parallel", "parallel", "arbitrary"))(a, dy)


def _rms_fwd(x, g, *, name, tm=512):
    M, Dm = x.shape
    tm = min(tm, M)

    def body(x_ref, g_ref, o_ref):
        xf = x_ref[...]
        r = lax.rsqrt(jnp.mean(xf * xf, axis=-1, keepdims=True) + EPS)
        o_ref[...] = (xf * r * g_ref[...]).astype(o_ref.dtype)

    return pl.pallas_call(
        body, name=name, grid=(M // tm,),
        in_specs=[pl.BlockSpec((tm, Dm), lambda i: (i, 0)), pl.BlockSpec((1, Dm), lambda i: (0, 0))],
        out_specs=pl.BlockSpec((tm, Dm), lambda i: (i, 0)), out_shape=_sds((M, Dm), BF16),
        compiler_params=_params("parallel"))(x, g)


def _rms_bwd(dxn, x, g, dres, *, name, tm=512):
    M, Dm = x.shape
    tm = min(tm, M)
    has_res = dres is not None

    def body(*refs):
        dxn_ref, x_ref, g_ref = refs[:3]
        dres_ref = refs[3] if has_res else None
        dx_ref, dxb_ref, dg_ref = refs[-3:]
        i = pl.program_id(0)
        xf = x_ref[...]
        r = lax.rsqrt(jnp.mean(xf * xf, axis=-1, keepdims=True) + EPS)
        nrm = xf * r
        dxn_f = dxn_ref[...].astype(F32)
        dn = dxn_f * g_ref[...]
        dx = r * (dn - nrm * jnp.mean(dn * nrm, axis=-1, keepdims=True))
        if has_res:
            dx = dx + dres_ref[...]
        dx_ref[...] = dx
        dxb_ref[...] = dx.astype(dxb_ref.dtype)

        @pl.when(i == 0)
        def _():
            dg_ref[...] = jnp.zeros_like(dg_ref)

        dg_ref[0:1, :] += jnp.sum(dxn_f * nrm, axis=0, keepdims=True)

    row = pl.BlockSpec((tm, Dm), lambda i: (i, 0))
    in_specs = [row, row, pl.BlockSpec((1, Dm), lambda i: (0, 0))] + ([row] if has_res else [])
    args = [dxn, x, g] + ([dres] if has_res else [])
    return pl.pallas_call(
        body, name=name, grid=(M // tm,), in_specs=in_specs,
        out_specs=[row, row, pl.BlockSpec((8, Dm), lambda i: (0, 0))],
        out_shape=[_sds((M, Dm), F32), _sds((M, Dm), BF16), _sds((8, Dm), F32)],
        compiler_params=_params("arbitrary"))(*args)


def _loss_head(h, g, target, *, tm=512):
    M, Dm = h.shape
    tm = min(tm, M)

    def body(h_ref, g_ref, t_ref, dh_ref, dhb_ref, dg_ref, loss_ref):
        i = pl.program_id(0)
        hf = h_ref[...]
        r = lax.rsqrt(jnp.mean(hf * hf, axis=-1, keepdims=True) + EPS)
        nrm = hf * r
        gv = g_ref[...]
        err = nrm * gv - t_ref[...]
        dy = err * (1.0 / Dm)
        dn = dy * gv
        dh = r * (dn - nrm * jnp.mean(dn * nrm, axis=-1, keepdims=True))
        dh_ref[...] = dh
        dhb_ref[...] = dh.astype(dhb_ref.dtype)

        @pl.when(i == 0)
        def _():
            dg_ref[...] = jnp.zeros_like(dg_ref)
            loss_ref[...] = jnp.zeros_like(loss_ref)

        dg_ref[0:1, :] += jnp.sum(dy * nrm, axis=0, keepdims=True)
        part = 0.5 * jnp.sum(jnp.mean(err * err, axis=-1, keepdims=True), axis=0, keepdims=True)
        sel = (lax.broadcasted_iota(jnp.int32, (8, 128), 0) == 0) & (lax.broadcasted_iota(jnp.int32, (8, 128), 1) == 0)
        loss_ref[...] += jnp.where(sel, part, 0.0)

    row = pl.BlockSpec((tm, Dm), lambda i: (i, 0))
    return pl.pallas_call(
        body, name="loss_head", grid=(M // tm,),
        in_specs=[row, pl.BlockSpec((1, Dm), lambda i: (0, 0)), row],
        out_specs=[row, row, pl.BlockSpec((8, Dm), lambda i: (0, 0)), pl.BlockSpec((8, 128), lambda i: (0, 0))],
        out_shape=[_sds((M, Dm), F32), _sds((M, Dm), BF16), _sds((8, Dm), F32), _sds((8, 128), F32)],
        compiler_params=_params("arbitrary"))(h, g, target)


def _class_spec(tm, d, width):
    return pl.BlockSpec((d, tm // d, width), lambda i: (0, i, 0))


def _row_scratch(tm, width):
    return pltpu.VMEM((width // LANES, tm, LANES), F32)


def _fill(scr, val):
    for c in range(scr.shape[0]):
        scr[c] = val[:, c * LANES:(c + 1) * LANES]


def _to_classes(scr, out_ref, d):
    n = scr.shape[1] // d
    for r in range(d):
        for c in range(scr.shape[0]):
            out_ref[r, :, c * LANES:(c + 1) * LANES] = scr[c, pl.ds(r, n, stride=d), :].astype(out_ref.dtype)


def _from_classes(in_ref, scr, d):
    n = scr.shape[1] // d
    for r in range(d):
        blk = in_ref[r].astype(F32)
        for c in range(scr.shape[0]):
            scr[c, pl.ds(r, n, stride=d), :] = blk[:, c * LANES:(c + 1) * LANES]
    return jnp.concatenate([scr[c] for c in range(scr.shape[0])], axis=1)


def _rope_tables(S):
    half = ROT_DIM // 2
    freqs = ROPE_THETA ** (-jnp.arange(0, ROT_DIM, 2, dtype=F32) / ROT_DIM)
    ang = jnp.arange(S, dtype=F32)[:, None] * freqs[None, :]
    cos, sin = jnp.cos(ang), jnp.sin(ang)
    ones = jnp.ones((S, HEAD_DIM - ROT_DIM), F32)
    zeros = jnp.zeros((S, HEAD_DIM - ROT_DIM), F32)
    zh = jnp.zeros((S, half), F32)
    c = jnp.concatenate([cos, cos, ones], axis=1)
    sa = jnp.concatenate([-sin, zh, zeros], axis=1)
    sb = jnp.concatenate([zh, sin, zeros], axis=1)
    return tuple(jnp.tile(t, (1, LANES // HEAD_DIM)) for t in (c, sa, sb))


def _rope_fwd(y, tables, *, tm=512):
    S = y.shape[0]
    W = 2 * D_ATT
    tm = min(tm, S)
    half = ROT_DIM // 2
    dils = [d for d in DILATIONS if d > 1]

    def body(y_ref, c_ref, sa_ref, sb_ref, qk_ref, *rest):
        qk_outs, v_outs = rest[:len(dils)], rest[len(dils):2 * len(dils)]
        scr_qk, scr_v = rest[2 * len(dils):]
        t = y_ref[:, 0:W].astype(F32)
        rep = W // LANES
        c, sa, sb = (jnp.tile(r[...], (1, rep)) for r in (c_ref, sa_ref, sb_ref))
        rot = t * c + pltpu.roll(t, W - half, axis=1) * sa + pltpu.roll(t, half, axis=1) * sb
        qk_ref[...] = rot.astype(qk_ref.dtype)
        _fill(scr_qk, rot)
        _fill(scr_v, y_ref[:, W:W + D_ATT].astype(F32))
        for d, qo, vo in zip(dils, qk_outs, v_outs):
            _to_classes(scr_qk, qo, d)
            _to_classes(scr_v, vo, d)

    tab = pl.BlockSpec((tm, LANES), lambda i: (i, 0))
    out = pl.pallas_call(
        body, name="rope_fwd", grid=(S // tm,),
        in_specs=[pl.BlockSpec((tm, 3 * D_ATT), lambda i: (i, 0)), tab, tab, tab],
        out_specs=[pl.BlockSpec((tm, W), lambda i: (i, 0))] + [_class_spec(tm, d, W) for d in dils]
        + [_class_spec(tm, d, D_ATT) for d in dils],
        out_shape=[_sds((S, W), BF16)] + [_sds((d, S // d, W), BF16) for d in dils]
        + [_sds((d, S // d, D_ATT), BF16) for d in dils],
        scratch_shapes=[_row_scratch(tm, W), _row_scratch(tm, D_ATT)],
        compiler_params=_params("parallel"))(y, *tables)
    qk = [out[0]] + [o.reshape(S, W) for o in out[1:1 + len(dils)]]
    v = [None] + [o.reshape(S, D_ATT) for o in out[1 + len(dils):]]
    return qk, v


def _assemble_dy(dq, dk, dv, dag, tables, *, tm=512):
    S = dag.shape[0]
    tm = min(tm, S)
    half = ROT_DIM // 2
    W = D_ATT
    n_pat = len(DILATIONS)

    def body(*refs):
        groups = [refs[g * n_pat:(g + 1) * n_pat] for g in range(3)]
        dag_ref, c_ref, sa_ref, sb_ref, o_ref, scr = refs[3 * n_pat:]
        rep = W // LANES
        c, sa, sb = (jnp.tile(r[...], (1, rep)) for r in (c_ref, sa_ref, sb_ref))

        def total(rs):
            acc = rs[0][...].astype(F32)
            for d, r in zip(DILATIONS[1:], rs[1:]):
                acc = acc + _from_classes(r, scr, d)
            return acc

        def unrope(dr):
            return dr * c + pltpu.roll(dr * sa, half, axis=1) + pltpu.roll(dr * sb, W - half, axis=1)

        o_ref[:, 0:W] = unrope(total(groups[0])).astype(o_ref.dtype)
        o_ref[:, W:2 * W] = unrope(total(groups[1])).astype(o_ref.dtype)
        o_ref[:, 2 * W:3 * W] = total(groups[2]).astype(o_ref.dtype)
        o_ref[:, 3 * W:] = dag_ref[...]

    specs = [pl.BlockSpec((tm, W), lambda i: (i, 0))] + [_class_spec(tm, d, W) for d in DILATIONS[1:]]
    tab = pl.BlockSpec((tm, LANES), lambda i: (i, 0))
    args = [a if d == 1 else a.reshape(d, S // d, W) for grp in (dq, dk, dv) for d, a in zip(DILATIONS, grp)]
    return pl.pallas_call(
        body, name="assemble_dy", grid=(S // tm,),
        in_specs=specs * 3 + [pl.BlockSpec((tm, 2 * D_CONV), lambda i: (i, 0)), tab, tab, tab],
        out_specs=pl.BlockSpec((tm, D_IN), lambda i: (i, 0)), out_shape=_sds((S, D_IN), BF16),
        scratch_shapes=[_row_scratch(tm, W)],
        compiler_params=_params("parallel"))(*args, dag, *tables)


def _seq_specs(L, tb, col):
    nb, per, nh = L // tb, tb // HALF, L // HALF
    centre = pl.BlockSpec((tb, D_ATT), lambda r, i: (r * nb + i, col))
    prev = pl.BlockSpec((HALF, D_ATT), lambda r, i: (r * nh + jnp.maximum(i * per - 1, 0), col))
    nxt = pl.BlockSpec((HALF, D_ATT), lambda r, i: (r * nh + jnp.minimum((i + 1) * per, nh - 1), col))
    return prev, centre, nxt


def _band_mask(i, tq, L, centre_is_query):
    if centre_is_query:
        shape = (tq, tq + 2 * HALF)
        c_idx = lax.broadcasted_iota(jnp.int32, shape, 0)
        w_idx = lax.broadcasted_iota(jnp.int32, shape, 1)
    else:
        shape = (tq + 2 * HALF, tq)
        w_idx = lax.broadcasted_iota(jnp.int32, shape, 0)
        c_idx = lax.broadcasted_iota(jnp.int32, shape, 1)
    diff = w_idx - c_idx
    wpos = i * tq - HALF + w_idx
    return (diff >= 0) & (diff <= 2 * HALF) & (wpos >= 0) & (wpos < L)


def _lane_groups():
    for c0 in range(0, D_ATT, LANES):
        yield slice(c0, c0 + LANES)


def _first_head(rows):
    return lax.broadcasted_iota(jnp.int32, (rows, LANES), 1) < HEAD_DIM


def _split_pair(x, first):
    zero = jnp.zeros_like(x)
    return jnp.where(first, x, zero), jnp.where(first, zero, x)


def _nt(a, b):
    return lax.dot_general(a, b, (((1,), (1,)), ((), ())), preferred_element_type=F32)


def _tn(a, b):
    return lax.dot_general(a, b, (((0,), (0,)), ((), ())), preferred_element_type=F32)


ATT_SCALE = HEAD_DIM ** -0.5


def _att_fwd(qk, v_src, d, *, name):
    S = qk.shape[0]
    L = S // d
    tq = min(ATT_BLOCK, L)
    v_arr, v_col = v_src

    def body(q_ref, kp_ref, kc_ref, kn_ref, vp_ref, vc_ref, vn_ref, o_ref, lse_ref):
        i = pl.program_id(1)
        valid = _band_mask(i, tq, L, True)
        q = q_ref[...] * ATT_SCALE
        kwin = jnp.concatenate([kp_ref[...], kc_ref[...], kn_ref[...]], axis=0)
        vwin = jnp.concatenate([vp_ref[...], vc_ref[...], vn_ref[...]], axis=0)
        first = _first_head(tq)
        groups = list(_lane_groups())
        heads = [(ls, t) for ls in groups for t in _split_pair(q[:, ls], first)]
        s = [jnp.where(valid, _nt(t, kwin[:, ls]), NEG_INF) for ls, t in heads]
        m = [jnp.max(t, axis=-1, keepdims=True) for t in s]
        p = [jnp.exp(t - mm) for t, mm in zip(s, m)]
        den = [jnp.sum(t, axis=-1, keepdims=True) for t in p]
        o = [jnp.dot(t.astype(BF16), vwin[:, ls], preferred_element_type=F32) * (1.0 / dd)
             for t, dd, (ls, _) in zip(p, den, heads)]
        lse = [mm + jnp.log(dd) for mm, dd in zip(m, den)]
        for g, ls in enumerate(groups):
            o_ref[:, ls] = jnp.where(first, o[2 * g], o[2 * g + 1])
            lse_ref[:, ls] = jnp.where(first, lse[2 * g], lse[2 * g + 1])

    _, qc, _ = _seq_specs(L, tq, 0)
    kp, kc, kn = _seq_specs(L, tq, 1)
    vp, vc, vn = _seq_specs(L, tq, v_col)
    out = pl.BlockSpec((tq, D_ATT), lambda r, i: (r * (L // tq) + i, 0))
    return pl.pallas_call(
        body, name=name, grid=(d, L // tq),
        in_specs=[qc, kp, kc, kn, vp, vc, vn], out_specs=[out, out],
        out_shape=[_sds((S, D_ATT), F32)] * 2,
        compiler_params=_params("parallel", "parallel"))(qk, qk, qk, qk, v_arr, v_arr, v_arr)


def _att_combine(outs, lses, *, tm=512):
    S = outs[0].shape[0]
    tm = min(tm, S)
    dils = DILATIONS[1:]
    n_d = len(dils)

    def body(*refs):
        o_refs, l_refs = refs[0:1 + n_d], refs[1 + n_d:2 + 2 * n_d]
        att_ref, lg_ref = refs[2 + 2 * n_d:4 + 2 * n_d]
        lg_outs = refs[4 + 2 * n_d:4 + 3 * n_d]
        scr = refs[4 + 3 * n_d:]
        scr_o, scr_l, scr_lg = scr[:n_d], scr[n_d:2 * n_d], scr[2 * n_d]
        ls = [l_refs[0][...]] + [_from_classes(r, s, d) for r, s, d in zip(l_refs[1:], scr_l, dils)]
        os_ = [o_refs[0][...]] + [_from_classes(r, s, d) for r, s, d in zip(o_refs[1:], scr_o, dils)]
        mx = ls[0]
        for l in ls[1:]:
            mx = jnp.maximum(mx, l)
        es = [jnp.exp(l - mx) for l in ls]
        tot = es[0]
        num = es[0] * os_[0]
        for e, o in zip(es[1:], os_[1:]):
            tot = tot + e
            num = num + e * o
        att_ref[...] = (num / tot).astype(att_ref.dtype)
        lg = mx + jnp.log(tot)
        lg_ref[...] = lg
        _fill(scr_lg, lg)
        for d, out in zip(dils, lg_outs):
            _to_classes(scr_lg, out, d)

    nat = pl.BlockSpec((tm, D_ATT), lambda i: (i, 0))
    specs = [nat] + [_class_spec(tm, d, D_ATT) for d in dils]
    view = lambda arrs: [arrs[0]] + [a.reshape(d, S // d, D_ATT) for a, d in zip(arrs[1:], dils)]
    out = pl.pallas_call(
        body, name="att_combine", grid=(S // tm,), in_specs=specs * 2,
        out_specs=[nat, nat] + specs[1:],
        out_shape=[_sds((S, D_ATT), BF16), _sds((S, D_ATT), F32)] + [_sds((d, S // d, D_ATT), F32) for d in dils],
        scratch_shapes=[_row_scratch(tm, D_ATT)] * (2 * n_d + 1),
        compiler_params=_params("parallel"))(*view(list(outs)), *view(list(lses)))
    return out[0], [out[1]] + [o.reshape(S, D_ATT) for o in out[2:]]


def _att_delta(dac, att, *, tm=512):
    S = att.shape[0]
    tm = min(tm, S)
    dils = DILATIONS[1:]
    n_d = len(dils)

    def body(do_ref, o_ref, dl_ref, *rest):
        dl_outs, do_outs = rest[:n_d], rest[n_d:2 * n_d]
        scr_dl, scr_do = rest[2 * n_d:]
        do = do_ref[...].astype(F32)
        prod = do * o_ref[...].astype(F32)
        per_head = [jnp.broadcast_to(jnp.sum(prod[:, h * HEAD_DIM:(h + 1) * HEAD_DIM], axis=-1, keepdims=True),
                                     (tm, HEAD_DIM)) for h in range(ATT_HEADS)]
        dl = jnp.concatenate(per_head, axis=1)
        dl_ref[...] = dl
        _fill(scr_dl, dl)
        _fill(scr_do, do)
        for d, dlo, doo in zip(dils, dl_outs, do_outs):
            _to_classes(scr_dl, dlo, d)
            _to_classes(scr_do, doo, d)

    blk = pl.BlockSpec((tm, D_ATT), lambda i: (i, 0))
    out = pl.pallas_call(
        body, name="att_delta", grid=(S // tm,), in_specs=[blk, blk],
        out_specs=[blk] + [_class_spec(tm, d, D_ATT) for d in dils] * 2,
        out_shape=[_sds((S, D_ATT), F32)] + [_sds((d, S // d, D_ATT), F32) for d in dils]
        + [_sds((d, S // d, D_ATT), BF16) for d in dils],
        scratch_shapes=[_row_scratch(tm, D_ATT), _row_scratch(tm, D_ATT)],
        compiler_params=_params("parallel"))(dac, att)
    delta = [out[0]] + [o.reshape(S, D_ATT) for o in out[1:1 + n_d]]
    do = [None] + [o.reshape(S, D_ATT) for o in out[1 + n_d:]]
    return delta, do


def _att_dq(qk, v_src, do_src, lg, delta, d, *, name):
    S = qk.shape[0]
    L = S // d
    tq = min(ATT_BLOCK, L)
    (v_arr, v_col), (do_arr, do_col) = v_src, do_src

    def body(q_ref, kp_ref, kc_ref, kn_ref, vp_ref, vc_ref, vn_ref, do_ref, lg_ref, dl_ref, dq_ref):
        i = pl.program_id(1)
        valid = _band_mask(i, tq, L, True)
        q, do = q_ref[...] * ATT_SCALE, do_ref[...]
        kwin = jnp.concatenate([kp_ref[...], kc_ref[...], kn_ref[...]], axis=0)
        vwin = jnp.concatenate([vp_ref[...], vc_ref[...], vn_ref[...]], axis=0)
        first = _first_head(tq)
        groups = list(_lane_groups())
        cols = [c for ls in groups for c in (ls.start, ls.start + HEAD_DIM)]
        lanes = [ls for ls in groups for _ in range(2)]
        qh = [t for ls in groups for t in _split_pair(q[:, ls], first)]
        doh = [t for ls in groups for t in _split_pair(do[:, ls], first)]
        s = [jnp.where(valid, _nt(t, kwin[:, ls]), NEG_INF) for t, ls in zip(qh, lanes)]
        dp = [_nt(t, vwin[:, ls]) for t, ls in zip(doh, lanes)]
        p = [jnp.exp(t - lg_ref[:, c:c + 1]) for t, c in zip(s, cols)]
        ds = [pp * (t - dl_ref[:, c:c + 1]) for pp, t, c in zip(p, dp, cols)]
        dq = [jnp.dot(t.astype(BF16), kwin[:, ls], preferred_element_type=F32) for t, ls in zip(ds, lanes)]
        for g, ls in enumerate(groups):
            dq_ref[:, ls] = (jnp.where(first, dq[2 * g], dq[2 * g + 1]) * ATT_SCALE).astype(dq_ref.dtype)

    _, qc, _ = _seq_specs(L, tq, 0)
    kp, kc, kn = _seq_specs(L, tq, 1)
    vp, vc, vn = _seq_specs(L, tq, v_col)
    _, doc, _ = _seq_specs(L, tq, do_col)
    row = pl.BlockSpec((tq, D_ATT), lambda r, i: (r * (L // tq) + i, 0))
    return pl.pallas_call(
        body, name=name, grid=(d, L // tq),
        in_specs=[qc, kp, kc, kn, vp, vc, vn, doc, row, row], out_specs=row,
        out_shape=_sds((S, D_ATT), BF16),
        compiler_params=_params("parallel", "parallel"))(qk, qk, qk, qk, v_arr, v_arr, v_arr, do_arr, lg, delta)


def _att_dkv(qk, v_src, do_src, lg, delta, d, *, name):
    S = qk.shape[0]
    L = S // d
    tk = min(ATT_BLOCK, L)
    (v_arr, v_col), (do_arr, do_col) = v_src, do_src

    def body(k_ref, v_ref, qp_ref, qc_ref, qn_ref, dop_ref, doc_ref, don_ref,
             lgp_ref, lgc_ref, lgn_ref, dlp_ref, dlc_ref, dln_ref, dk_ref, dv_ref):
        i = pl.program_id(1)
        valid = _band_mask(i, tk, L, False)
        k, v = k_ref[...], v_ref[...]
        qwin = jnp.concatenate([qp_ref[...], qc_ref[...], qn_ref[...]], axis=0) * ATT_SCALE
        dowin = jnp.concatenate([dop_ref[...], doc_ref[...], don_ref[...]], axis=0)

        def column(refs, c0):
            return jnp.concatenate([r[:, c0:c0 + 1] for r in refs], axis=0)

        first_w, first_k = _first_head(tk + 2 * HALF), _first_head(tk)
        groups = list(_lane_groups())
        cols = [c for ls in groups for c in (ls.start, ls.start + HEAD_DIM)]
        lanes = [ls for ls in groups for _ in range(2)]
        qh = [t for ls in groups for t in _split_pair(qwin[:, ls], first_w)]
        doh = [t for ls in groups for t in _split_pair(dowin[:, ls], first_w)]
        lgw = [column((lgp_ref, lgc_ref, lgn_ref), c) for c in cols]
        dlw = [column((dlp_ref, dlc_ref, dln_ref), c) for c in cols]
        s = [jnp.where(valid, _nt(t, k[:, ls]), NEG_INF) for t, ls in zip(qh, lanes)]
        dp = [_nt(t, v[:, ls]) for t, ls in zip(doh, lanes)]
        p = [jnp.exp(t - l) for t, l in zip(s, lgw)]
        ds = [pp * (t - l) for pp, t, l in zip(p, dp, dlw)]
        dv = [_tn(pp.astype(BF16), dowin[:, ls]) for pp, ls in zip(p, lanes)]
        dk = [_tn(t.astype(BF16), qwin[:, ls]) for t, ls in zip(ds, lanes)]
        for g, ls in enumerate(groups):
            dk_ref[:, ls] = jnp.where(first_k, dk[2 * g], dk[2 * g + 1]).astype(dk_ref.dtype)
            dv_ref[:, ls] = jnp.where(first_k, dv[2 * g], dv[2 * g + 1]).astype(dv_ref.dtype)

    _, kc, _ = _seq_specs(L, tk, 1)
    _, vc, _ = _seq_specs(L, tk, v_col)
    qp, qc, qn = _seq_specs(L, tk, 0)
    dop, doc, don = _seq_specs(L, tk, do_col)
    rp, rc, rn = _seq_specs(L, tk, 0)
    out = pl.BlockSpec((tk, D_ATT), lambda r, i: (r * (L // tk) + i, 0))
    return pl.pallas_call(
        body, name=name, grid=(d, L // tk),
        in_specs=[kc, vc, qp, qc, qn, dop, doc, don, rp, rc, rn, rp, rc, rn], out_specs=[out, out],
        out_shape=[_sds((S, D_ATT), BF16)] * 2,
        compiler_params=_params("parallel", "parallel"))(
            qk, v_arr, qk, qk, qk, do_arr, do_arr, do_arr, lg, lg, lg, delta, delta, delta)


def _sigmoid(x):
    return 1.0 / (1.0 + jnp.exp(-x))


def _halo_specs(S, T, width, col):
    last = S // HALO - 1
    per = T // HALO
    centre = pl.BlockSpec((T, width), lambda i: (i, col))
    prev = pl.BlockSpec((HALO, width), lambda i: (jnp.maximum(i * per - 1, 0), col))
    nxt = pl.BlockSpec((HALO, width), lambda i: (jnp.minimum((i + 1) * per, last), col))
    return prev, centre, nxt


def _window_scratch(T, C):
    return pltpu.VMEM((8, T + 2 * HALO, C), F32)


def _fill_window(buf, prev, centre, nxt, T):
    buf[0, 0:HALO, :] = prev
    buf[0, HALO:HALO + T, :] = centre
    buf[0, HALO + T:, :] = nxt
    rows = T + 2 * HALO - 8
    for s in range(1, 8):
        buf[s, 0:rows, :] = buf[0, s:s + rows, :]


def _tap_reads(buf, first_off, step, r0, ls):
    by_slab = {}
    for k in range(CONV_WIDTH):
        off = first_off + step * k
        by_slab.setdefault(off % 8, []).append((k, off - off % 8))
    for s, taps in by_slab.items():
        lo = min(a for _, a in taps)
        hi = max(a for _, a in taps)
        rows = buf[s, pl.ds(lo + r0, CONV_ROWS + hi - lo), ls]
        for k, a in taps:
            yield k, rows[a - lo:a - lo + CONV_ROWS]


def _depthwise(buf, w_ref, out_ref, T, C, first_off, step):
    def row_tile(t, carry):
        r0 = pl.multiple_of(t * CONV_ROWS, CONV_ROWS)
        for c0 in range(0, C, LANES):
            ls = slice(c0, c0 + LANES)
            acc = jnp.zeros((CONV_ROWS, LANES), F32)
            for k, rows in _tap_reads(buf, first_off, step, r0, ls):
                acc = acc + rows * w_ref[k:k + 1, ls]
            out_ref[pl.ds(r0, CONV_ROWS), ls] = acc
        return carry

    lax.fori_loop(0, T // CONV_ROWS, row_tile, 0)


def _conv_fwd(y, conv_w32, conv_b, ln_g, ln_b, *, T=512):
    S = y.shape[0]
    T = min(T, S)
    nblk = S // T
    C = D_CONV

    def body(ap, ac, an, gp, gc, gn, w_ref, b_ref, lg_ref, lb_ref, cv_ref, u1_ref, buf):
        i = pl.program_id(0)

        def glu(a_ref, g_ref):
            return a_ref[...].astype(F32) * _sigmoid(g_ref[...].astype(F32))

        _fill_window(buf, jnp.where(i > 0, glu(ap, gp), 0.0), glu(ac, gc),
                     jnp.where(i < nblk - 1, glu(an, gn), 0.0), T)
        _depthwise(buf, w_ref, u1_ref, T, C, HALO - CONV_PAD, 1)
        u1 = u1_ref[...] + b_ref[...]
        u1_ref[...] = u1
        mu = jnp.mean(u1, axis=-1, keepdims=True)
        xc = u1 - mu
        rstd = lax.rsqrt(jnp.mean(xc * xc, axis=-1, keepdims=True) + EPS)
        u2 = xc * rstd * lg_ref[...] + lb_ref[...]
        cv_ref[...] = (u2 * _sigmoid(u2)).astype(cv_ref.dtype)

    ap, ac, an = _halo_specs(S, T, C, 3)
    gp, gc, gn = _halo_specs(S, T, C, 4)
    vec = pl.BlockSpec((1, C), lambda i: (0, 0))
    out = pl.BlockSpec((T, C), lambda i: (i, 0))
    return pl.pallas_call(
        body, name="conv_fwd", grid=(nblk,),
        in_specs=[ap, ac, an, gp, gc, gn, pl.BlockSpec((32, C), lambda i: (0, 0)), vec, vec, vec],
        out_specs=[out, out], out_shape=[_sds((S, C), BF16), _sds((S, C), F32)],
        scratch_shapes=[_window_scratch(T, C)],
        compiler_params=_params("parallel"))(y, y, y, y, y, y, conv_w32, conv_b, ln_g, ln_b)


def _conv_bwd(dac, u1, y, conv_w32, ln_g, ln_b, *, T=512):
    S = y.shape[0]
    T = min(T, S)
    nblk = S // T
    C = D_CONV

    def body(dp, dc, dn, up, uc, un, ap, ac, an, gp, gc, gn, w_ref, lg_ref, lb_ref,
             dag_ref, dw_ref, dsm_ref, bufd, bufu, du0_scr, dw_acc):
        i = pl.program_id(0)
        lg = lg_ref[...]

        def du1_of(dcv_ref, u1_ref):
            u1 = u1_ref[...]
            mu = jnp.mean(u1, axis=-1, keepdims=True)
            xc = u1 - mu
            rstd = lax.rsqrt(jnp.mean(xc * xc, axis=-1, keepdims=True) + EPS)
            xhat = xc * rstd
            u2 = xhat * lg + lb_ref[...]
            sg = _sigmoid(u2)
            du2 = dcv_ref[...].astype(F32) * (sg * (1.0 + u2 * (1.0 - sg)))
            dxh = du2 * lg
            du1 = rstd * (dxh - jnp.mean(dxh, axis=-1, keepdims=True)
                          - xhat * jnp.mean(dxh * xhat, axis=-1, keepdims=True))
            return du1, du2, xhat

        def glu(a_ref, g_ref):
            return a_ref[...].astype(F32) * _sigmoid(g_ref[...].astype(F32))

        @pl.when(i == 0)
        def _():
            dw_ref[...] = jnp.zeros_like(dw_ref)
            dsm_ref[...] = jnp.zeros_like(dsm_ref)

        du1_c, du2_c, xhat_c = du1_of(dc, uc)
        dsm_ref[0:1, :] += jnp.sum(du1_c, axis=0, keepdims=True)
        dsm_ref[1:2, :] += jnp.sum(du2_c * xhat_c, axis=0, keepdims=True)
        dsm_ref[2:3, :] += jnp.sum(du2_c, axis=0, keepdims=True)
        _fill_window(bufd, jnp.where(i > 0, du1_of(dp, up)[0], 0.0), du1_c,
                     jnp.where(i < nblk - 1, du1_of(dn, un)[0], 0.0), T)
        _fill_window(bufu, jnp.where(i > 0, glu(ap, gp), 0.0), glu(ac, gc),
                     jnp.where(i < nblk - 1, glu(an, gn), 0.0), T)

        _depthwise(bufd, w_ref, du0_scr, T, C, HALO + CONV_PAD, -1)
        dw_acc[...] = jnp.zeros_like(dw_acc)

        def dw_tile(t, carry):
            r0 = pl.multiple_of(t * CONV_ROWS, CONV_ROWS)
            for c0 in range(0, C, LANES):
                ls = slice(c0, c0 + LANES)
                d = bufd[0, pl.ds(HALO + r0, CONV_ROWS), ls]
                for k, rows in _tap_reads(bufu, HALO - CONV_PAD, 1, r0, ls):
                    prod = d * rows
                    part = prod[0:8]
                    for j in range(8, CONV_ROWS, 8):
                        part = part + prod[j:j + 8]
                    dw_acc[k, :, ls] += part
            return carry

        lax.fori_loop(0, T // CONV_ROWS, dw_tile, 0)
        for k in range(CONV_WIDTH):
            dw_ref[k:k + 1, :] += jnp.sum(dw_acc[k], axis=0, keepdims=True)
        du0 = du0_scr[...]
        a = ac[...].astype(F32)
        sg = _sigmoid(gc[...].astype(F32))
        dag_ref[:, 0:C] = (du0 * sg).astype(dag_ref.dtype)
        dag_ref[:, C:] = (du0 * a * sg * (1.0 - sg)).astype(dag_ref.dtype)

    dp, dc, dn = _halo_specs(S, T, C, 1)
    up, uc, un = _halo_specs(S, T, C, 0)
    ap, ac, an = _halo_specs(S, T, C, 3)
    gp, gc, gn = _halo_specs(S, T, C, 4)
    vec = pl.BlockSpec((1, C), lambda i: (0, 0))
    return pl.pallas_call(
        body, name="conv_bwd", grid=(nblk,),
        in_specs=[dp, dc, dn, up, uc, un, ap, ac, an, gp, gc, gn,
                  pl.BlockSpec((32, C), lambda i: (0, 0)), vec, vec],
        out_specs=[pl.BlockSpec((T, 2 * C), lambda i: (i, 0)), pl.BlockSpec((32, C), lambda i: (0, 0)),
                   pl.BlockSpec((8, C), lambda i: (0, 0))],
        out_shape=[_sds((S, 2 * C), BF16), _sds((32, C), F32), _sds((8, C), F32)],
        scratch_shapes=[_window_scratch(T, C), _window_scratch(T, C), pltpu.VMEM((T, C), F32),
                        pltpu.VMEM((CONV_WIDTH, 8, C), F32)],
        compiler_params=_params("arbitrary"))(dac, dac, dac, u1, u1, u1, y, y, y, y, y, y, conv_w32, ln_g, ln_b)


def _xatt_fwd(xq, xk, xv, *, tm=512):
    S = xq.shape[0]
    M = xk.shape[0]
    tm = min(tm, S)
    scale = XATT_HEAD_DIM ** -0.5

    def body(q_ref, k_ref, v_ref, o_ref):
        for h in range(XATT_HEADS):
            sl = slice(h * XATT_HEAD_DIM, (h + 1) * XATT_HEAD_DIM)
            s = _nt(q_ref[:, sl], k_ref[:, sl]) * scale
            e = jnp.exp(s - jnp.max(s, axis=-1, keepdims=True))
            p = e / jnp.sum(e, axis=-1, keepdims=True)
            o_ref[:, sl] = jnp.dot(p.astype(BF16), v_ref[:, sl], preferred_element_type=F32).astype(o_ref.dtype)

    row = pl.BlockSpec((tm, D_MODEL), lambda i: (i, 0))
    full = pl.BlockSpec((M, D_MODEL), lambda i: (0, 0))
    return pl.pallas_call(
        body, name="xatt_fwd", grid=(S // tm,), in_specs=[row, full, full], out_specs=row,
        out_shape=_sds((S, D_MODEL), BF16), compiler_params=_params("parallel"))(xq, xk, xv)


def _xatt_bwd(xq, xk, xv, dxo, *, tm=512):
    S = xq.shape[0]
    M = xk.shape[0]
    tm = min(tm, S)
    scale = XATT_HEAD_DIM ** -0.5

    def body(q_ref, k_ref, v_ref, do_ref, dq_ref, dk_ref, dv_ref):
        i = pl.program_id(0)

        @pl.when(i == 0)
        def _():
            dk_ref[...] = jnp.zeros_like(dk_ref)
            dv_ref[...] = jnp.zeros_like(dv_ref)

        for h in range(XATT_HEADS):
            sl = slice(h * XATT_HEAD_DIM, (h + 1) * XATT_HEAD_DIM)
            q, k, v, do = q_ref[:, sl], k_ref[:, sl], v_ref[:, sl], do_ref[:, sl]
            s = _nt(q, k) * scale
            e = jnp.exp(s - jnp.max(s, axis=-1, keepdims=True))
            p = e / jnp.sum(e, axis=-1, keepdims=True)
            dp = _nt(do, v)
            ds = p * (dp - jnp.sum(dp * p, axis=-1, keepdims=True))
            dsb = ds.astype(BF16)
            dq_ref[:, sl] = (jnp.dot(dsb, k, preferred_element_type=F32) * scale).astype(dq_ref.dtype)
            dv_ref[:, sl] += _tn(p.astype(BF16), do)
            dk_ref[:, sl] += _tn(dsb, q) * scale

    row = pl.BlockSpec((tm, D_MODEL), lambda i: (i, 0))
    full = pl.BlockSpec((M, D_MODEL), lambda i: (0, 0))
    return pl.pallas_call(
        body, name="xatt_bwd", grid=(S // tm,), in_specs=[row, full, full, row], out_specs=[row, full, full],
        out_shape=[_sds((S, D_MODEL), BF16), _sds((M, D_MODEL), F32), _sds((M, D_MODEL), F32)],
        compiler_params=_params("arbitrary"))(xq, xk, xv, dxo)


def _row_tile(R):
    for t in (256, 128, 64, 32, 16, 8):
        if R % t == 0:
            return t
    return R


def _sum_partials(own, recv, me, *, name):
    _, R, C = own.shape
    t = _row_tile(R)

    def body(me_ref, own_ref, r_ref, o_ref):
        o_ref[...] = ((own_ref[...].astype(F32) + r_ref[0].astype(F32)) + r_ref[1].astype(F32)) + r_ref[2].astype(F32)

    return pl.pallas_call(
        body, name=name,
        grid_spec=pltpu.PrefetchScalarGridSpec(
            num_scalar_prefetch=1, grid=(R // t,),
            in_specs=[pl.BlockSpec((None, t, C), lambda i, me_ref: (me_ref[0], i, 0)),
                      pl.BlockSpec((3, t, C), lambda i, me_ref: (0, i, 0))],
            out_specs=pl.BlockSpec((t, C), lambda i, me_ref: (i, 0))),
        out_shape=_sds((R, C), F32), compiler_params=_params("parallel"))(me, own, recv)


def _adamw_math(w, g, m, v):
    m2 = ADAM_B1 * m + (1.0 - ADAM_B1) * g
    v2 = ADAM_B2 * v + (1.0 - ADAM_B2) * (g * g)
    m_hat = m2 / (1.0 - ADAM_B1 ** ADAM_STEP)
    v_hat = v2 / (1.0 - ADAM_B2 ** ADAM_STEP)
    delta = -ADAM_LR * (m_hat / (jnp.sqrt(v_hat) + ADAM_EPS) + ADAM_WD * w)
    return delta, m2, v2


def _adamw(parts, w, m, v, *, name):
    R, C = w.shape
    t = _row_tile(R)
    n = len(parts)

    def body(*refs):
        w_ref, m_ref, v_ref = refs[n:n + 3]
        g_ref, d_ref, m2_ref, v2_ref = refs[n + 3:]
        g = refs[0][...]
        for r in refs[1:n]:
            g = g + r[...]
        delta, m2, v2 = _adamw_math(w_ref[...], g, m_ref[...], v_ref[...])
        g_ref[...] = g
        d_ref[...] = delta
        m2_ref[...] = m2
        v2_ref[...] = v2

    blk = pl.BlockSpec((t, C), lambda i: (i, 0))
    return pl.pallas_call(
        body, name=name, grid=(R // t,), in_specs=[blk] * (n + 3), out_specs=[blk] * 4,
        out_shape=[_sds((R, C), F32)] * 4, compiler_params=_params("parallel"))(*parts, w, m, v)


def _sum_devices(gathered):
    _, R, C = gathered.shape

    def body(g_ref, o_ref):
        acc = g_ref[0]
        for k in range(1, N_DEV):
            acc = acc + g_ref[k]
        o_ref[...] = acc

    return pl.pallas_call(body, name="sum_devices", out_shape=_sds((R, C), F32))(gathered)


def _chip_peers():
    x, y = lax.axis_index("x"), lax.axis_index("y")
    return [(1 - x, y), (x, 1 - y), (1 - x, 1 - y)]


HBM_SPEC = pl.BlockSpec(memory_space=pltpu.HBM)
SEM_SPEC = pl.BlockSpec(memory_space=pltpu.SEMAPHORE)


def _exchange_start(mode, srcs, zones, *, name):
    n = len(srcs)

    def body(*refs):
        ins, lands = refs[:n], refs[n:2 * n]
        send_sems, recv_sems = refs[2 * n:3 * n], refs[3 * n:4 * n]
        token = refs[-1]
        c = lax.axis_index("c")
        mine = 2 * lax.axis_index("x") + lax.axis_index("y")
        for t in range(n):
            for k, (px, py) in enumerate(_chip_peers()):
                if mode == "gather":
                    s, d = ins[t], lands[t].at[mine]
                else:
                    s, d = ins[t].at[2 * px + py], lands[t].at[k]
                pltpu.make_async_remote_copy(src_ref=s, dst_ref=d, send_sem=send_sems[t], recv_sem=recv_sems[t],
                                             device_id=(px, py, c), device_id_type=MESH).start()
        token[...] = jnp.zeros_like(token)

    hbm = lambda a: pltpu.with_memory_space_constraint(a, pltpu.HBM)
    out = pl.pallas_call(
        body, name=name,
        in_specs=[HBM_SPEC] * (2 * n),
        out_specs=[SEM_SPEC] * (2 * n) + [HBM_SPEC] * (2 * n) + [pl.BlockSpec(memory_space=pltpu.VMEM)],
        out_shape=[pltpu.SemaphoreType.DMA(())] * (2 * n)
        + [pltpu.HBM(a.shape, a.dtype) for a in list(srcs) + list(zones)] + [_sds((8, LANES), F32)],
        input_output_aliases={i: 2 * n + i for i in range(2 * n)},
        compiler_params=pltpu.CompilerParams(has_side_effects=pltpu.SideEffectType.DATAFLOW_SIDE_EFFECTING),
    )(*[hbm(a) for a in list(srcs) + list(zones)])
    return out[:n], out[n:2 * n], out[2 * n:3 * n], out[3 * n:4 * n], out[-1]


def _exchange_wait(started, after, *, name):
    send_sems, recv_sems, srcs, zones, _ = started
    n = len(srcs)

    def body(*refs):
        lands = refs[n:2 * n]
        send_refs, recv_refs = refs[2 * n:3 * n], refs[3 * n:4 * n]
        me = (lax.axis_index("x"), lax.axis_index("y"), lax.axis_index("c"))
        for t in range(n):
            three = lands[t].at[pl.ds(0, N_CHIPS - 1)]
            cp = pltpu.make_async_remote_copy(src_ref=three, dst_ref=three, send_sem=send_refs[t],
                                              recv_sem=recv_refs[t], device_id=me, device_id_type=MESH)
            cp.wait_send()
            cp.wait_recv()

    out = pl.pallas_call(
        body, name=name,
        in_specs=[HBM_SPEC] * (2 * n) + [SEM_SPEC] * (2 * n) + [pl.BlockSpec(memory_space=pl.ANY)],
        out_specs=[HBM_SPEC] * (2 * n),
        out_shape=[pltpu.HBM(a.shape, a.dtype) for a in list(srcs) + list(zones)],
        input_output_aliases={i: i for i in range(2 * n)},
        compiler_params=pltpu.CompilerParams(has_side_effects=pltpu.SideEffectType.DATAFLOW_SIDE_EFFECTING),
    )(*srcs, *zones, *send_sems, *recv_sems, after)
    return out[:n], out[n:]


def _allgather_small(small):
    def body(small_ref, gath_ref, send_sems, recv_sems, loc_sem):
        x, y, c = lax.axis_index("x"), lax.axis_index("y"), lax.axis_index("c")
        me = 4 * x + 2 * y + c
        flips = [(fx, fy, fc) for fx in (0, 1) for fy in (0, 1) for fc in (0, 1)][1:]

        def flipped(fx, fy, fc):
            return (1 - x if fx else x, 1 - y if fy else y, 1 - c if fc else c)

        loc = pltpu.make_async_copy(small_ref, gath_ref.at[me], loc_sem)
        loc.start()
        sends = []
        for j, flip in enumerate(flips):
            cp = pltpu.make_async_remote_copy(
                src_ref=small_ref, dst_ref=gath_ref.at[me], send_sem=send_sems.at[j], recv_sem=recv_sems.at[j],
                device_id=flipped(*flip), device_id_type=MESH)
            cp.start()
            sends.append(cp)
        for j, flip in enumerate(flips):
            px, py, pc = flipped(*flip)
            pltpu.make_async_remote_copy(
                src_ref=small_ref, dst_ref=gath_ref.at[4 * px + 2 * py + pc], send_sem=send_sems.at[j],
                recv_sem=recv_sems.at[j], device_id=(px, py, pc), device_id_type=MESH).wait_recv()
        for cp in sends:
            cp.wait_send()
        loc.wait()

    any_spec = pl.BlockSpec(memory_space=pl.ANY)
    return pl.pallas_call(
        body, name="allgather_small", in_specs=[any_spec], out_specs=any_spec,
        out_shape=_sds((N_DEV,) + small.shape, small.dtype),
        scratch_shapes=[pltpu.SemaphoreType.DMA((N_DEV - 1,)), pltpu.SemaphoreType.DMA((N_DEV - 1,)),
                        pltpu.SemaphoreType.DMA])(small)


def _swap_with_sibling(parts):
    n = len(parts)

    def body(*refs):
        ins, outs = refs[:n], refs[n:2 * n]
        send_sems, recv_sems = refs[2 * n:]
        sib = (lax.axis_index("x"), lax.axis_index("y"), 1 - lax.axis_index("c"))
        cps = []
        for t in range(n):
            cp = pltpu.make_async_remote_copy(
                src_ref=ins[t], dst_ref=outs[t], send_sem=send_sems.at[t], recv_sem=recv_sems.at[t],
                device_id=sib, device_id_type=MESH)
            cp.start()
            cps.append(cp)
        for cp in cps:
            cp.wait()

    any_spec = pl.BlockSpec(memory_space=pl.ANY)
    return pl.pallas_call(
        body, name="swap_with_sibling", in_specs=[any_spec] * n, out_specs=[any_spec] * n,
        out_shape=[_sds(p.shape, p.dtype) for p in parts],
        scratch_shapes=[pltpu.SemaphoreType.DMA((n,)), pltpu.SemaphoreType.DMA((n,))])(*parts)


BIG = ("w_in", "w_out", "w_xq", "w_xk", "w_xv", "w_xo", "w_up", "w_down")
COL_SHARDED = ("w_in", "w_up")


def _as_matrix(name, w4):
    if name in COL_SHARDED:
        return w4
    return w4.reshape(1, w4.shape[0] * w4.shape[1], w4.shape[2])


def _transposed(w3):
    nsh, K, n = w3.shape
    return jnp.swapaxes(w3, 1, 2).reshape(1, nsh * n, K)


def _shard_layout(name, g):
    if name in COL_SHARDED:
        return g
    return g.reshape(N_CHIPS, g.shape[0] * g.shape[1] // N_CHIPS, g.shape[2])


def _local_step(x, mem, target, vecs, comm):
    S = x.shape[0]
    tables = _rope_tables(S)

    xn = _rms_fwd(x, vecs["norm_mix_g"], name="rms_mix")
    w_in, conv_w32 = comm["first"](xn)
    y = _mm_nn(xn, w_in, name="mm_in", tn=640)
    qk, v_perm = _rope_fwd(y, tables)
    v_src = [(y, 2)] + [(v, 0) for v in v_perm[1:]]
    outs, lses = zip(*[_att_fwd(qk[p], v_src[p], d, name=f"att_fwd_d{d}") for p, d in enumerate(DILATIONS)])
    att, lg = _att_combine(outs, lses)
    cv, u1 = _conv_fwd(y, conv_w32, vecs["conv_b"], vecs["conv_ln_g"], vecs["conv_ln_b"])
    mix = jnp.concatenate([att, cv], axis=1)
    Wm = {k: _as_matrix(k, v) for k, v in comm["rest"](mix).items()}
    Wm["w_in"] = w_in
    h1 = _mm_nn(mix, Wm["w_out"], name="mm_out", out_dtype=F32, res=x)
    hn = _rms_fwd(h1, vecs["norm_x_g"], name="rms_x")
    xq = _mm_nn(hn, Wm["w_xq"], name="mm_xq")
    mn = _rms_fwd(mem, vecs["norm_mem_g"], name="rms_mem")
    xk = _mm_nn(mn, Wm["w_xk"], name="mm_xk")
    xv = _mm_nn(mn, Wm["w_xv"], name="mm_xv")
    xo = _xatt_fwd(xq, xk, xv)
    h2 = _mm_nn(xo, Wm["w_xo"], name="mm_xo", out_dtype=F32, res=h1)
    hm = _rms_fwd(h2, vecs["norm_mlp_g"], name="rms_mlp")
    relu_up, act = _mm_nn(hm, Wm["w_up"], name="mm_up", relu2=True)
    h3 = _mm_nn(act, Wm["w_down"], name="mm_down", out_dtype=F32, res=h2, tm=512, tk=D_FF)

    dh3, dh3b, dg_final, loss = _loss_head(h3, vecs["norm_final_g"], target)
    g = {}
    g["w_down"] = _mm_tn(act, dh3b, 1, name="dw_down")
    dup = _mm_nt(dh3b, Wm["w_down"], name="d_act", out_dtype=BF16, mul=relu_up)
    g["w_up"] = _mm_tn(hm, dup, N_CHIPS, name="dw_up")
    sent = comm["send_mlp"]({k: _shard_layout(k, g[k]) for k in ("w_down", "w_up")})
    dhm = _mm_nn(dup, _transposed(Wm["w_up"]), name="d_hm", tm=512, tk=D_FF)
    dh2, dh2b, dg_mlp = _rms_bwd(dhm, h2, vecs["norm_mlp_g"] + sent[0:1, 0:1], dh3, name="rms_bwd_mlp")
    g["w_xo"] = _mm_tn(xo, dh2b, 1, name="dw_xo")
    dxo = _mm_nt(dh2b, Wm["w_xo"], name="d_xo", out_dtype=BF16)
    dxq, dxk, dxv = _xatt_bwd(xq, xk, xv, dxo)
    g["w_xq"] = _mm_tn(hn, dxq, 1, name="dw_xq")
    dhn = _mm_nt(dxq, Wm["w_xq"], name="d_hn", out_dtype=BF16)
    dh1, dh1b, dg_x = _rms_bwd(dhn, h1, vecs["norm_x_g"], dh2, name="rms_bwd_x")
    dxkb, dxvb = dxk.astype(BF16), dxv.astype(BF16)
    g["w_xk"] = _mm_tn(mn, dxkb, 1, name="dw_xk")
    g["w_xv"] = _mm_tn(mn, dxvb, 1, name="dw_xv")
    dmn = _mm_nt(jnp.concatenate([dxkb, dxvb], axis=1),
                 jnp.concatenate([Wm["w_xk"], Wm["w_xv"]], axis=2), name="d_mn", out_dtype=BF16)
    _, _, dg_mem = _rms_bwd(dmn, mem, vecs["norm_mem_g"], None, name="rms_bwd_mem")
    g["w_out"] = _mm_tn(mix, dh1b, 1, name="dw_out")
    sent = comm["send_att"]({k: _shard_layout(k, g[k]) for k in ("w_out", "w_xq", "w_xk", "w_xv", "w_xo")})
    dac = _mm_nt(dh1b, Wm["w_out"], name="d_mix", out_dtype=BF16)
    dag, dconv_w, dconv_small = _conv_bwd(dac, u1, y, conv_w32, vecs["conv_ln_g"] + sent[0:1, 0:1],
                                          vecs["conv_ln_b"])
    delta, do_perm = _att_delta(dac, att)
    do_src = [(dac, 0)] + [(t, 0) for t in do_perm[1:]]
    dq = [_att_dq(qk[p], v_src[p], do_src[p], lg[p], delta[p], d, name=f"att_dq_d{d}")
          for p, d in enumerate(DILATIONS)]
    dk, dv = zip(*[_att_dkv(qk[p], v_src[p], do_src[p], lg[p], delta[p], d, name=f"att_dkv_d{d}")
                   for p, d in enumerate(DILATIONS)])
    dy = _assemble_dy(dq, dk, dv, dag, tables)
    sent = comm["send_in"]({"w_in": _mm_tn(xn, dy, N_CHIPS, name="dw_in", tn=640)})
    dxn = _mm_nn(dy, _transposed(Wm["w_in"]), name="d_xn", tm=512, tk=D_IN)
    grad_x, _, dg_mix = _rms_bwd(dxn, x, vecs["norm_mix_g"] + sent[0:1, 0:1], dh1, name="rms_bwd_mix")

    small = dict(conv_w=dconv_w, conv_small=dconv_small, norm_mix_g=dg_mix, norm_x_g=dg_x, norm_mem_g=dg_mem,
                 norm_mlp_g=dg_mlp, norm_final_g=dg_final, loss=loss)
    return grad_x, small


SMALL_ORDER = ("conv_w", "conv_small", "norm_mix_g", "norm_x_g", "norm_mem_g", "norm_mlp_g", "norm_final_g", "loss")


def _pack_small(small):
    rows, offs, pos = [], {}, 0
    for k in SMALL_ORDER:
        a = small[k]
        a = a.reshape(a.shape[0] * a.shape[1] // SMALL_W, SMALL_W)
        pad = (-a.shape[0]) % 8
        if pad:
            a = jnp.pad(a, ((0, pad), (0, 0)))
        rows.append(a)
        offs[k] = pos
        pos += a.shape[0]
    return jnp.concatenate(rows, axis=0), offs


def kernel(x, mem, norm_mix_g, w_in, conv_w, conv_b, conv_ln_g, conv_ln_b, w_out, norm_x_g, norm_mem_g, w_xq, w_xk, w_xv, w_xo, norm_mlp_g, w_up, w_down, norm_final_g, loss_target, m_norm_mix_g, m_w_in, m_conv_w, m_conv_b, m_conv_ln_g, m_conv_ln_b, m_w_out, m_norm_x_g, m_norm_mem_g, m_w_xq, m_w_xk, m_w_xv, m_w_xo, m_norm_mlp_g, m_w_up, m_w_down, m_norm_final_g, v_norm_mix_g, v_w_in, v_conv_w, v_conv_b, v_conv_ln_g, v_conv_ln_b, v_w_out, v_norm_x_g, v_norm_mem_g, v_w_xq, v_w_xk, v_w_xv, v_w_xo, v_norm_mlp_g, v_w_up, v_w_down, v_norm_final_g):
    names = ("norm_mix_g", "w_in", "conv_w", "conv_b", "conv_ln_g", "conv_ln_b", "w_out", "norm_x_g", "norm_mem_g",
             "w_xq", "w_xk", "w_xv", "w_xo", "norm_mlp_g", "w_up", "w_down", "norm_final_g")
    wts = dict(zip(names, (norm_mix_g, w_in, conv_w, conv_b, conv_ln_g, conv_ln_b, w_out, norm_x_g, norm_mem_g,
                           w_xq, w_xk, w_xv, w_xo, norm_mlp_g, w_up, w_down, norm_final_g)))
    mom = dict(zip(names, (m_norm_mix_g, m_w_in, m_conv_w, m_conv_b, m_conv_ln_g, m_conv_ln_b, m_w_out, m_norm_x_g,
                           m_norm_mem_g, m_w_xq, m_w_xk, m_w_xv, m_w_xo, m_norm_mlp_g, m_w_up, m_w_down, m_norm_final_g)))
    var = dict(zip(names, (v_norm_mix_g, v_w_in, v_conv_w, v_conv_b, v_conv_ln_g, v_conv_ln_b, v_w_out, v_norm_x_g,
                           v_norm_mem_g, v_w_xq, v_w_xk, v_w_xv, v_w_xo, v_norm_mlp_g, v_w_up, v_w_down, v_norm_final_g)))
    chip = 2 * lax.axis_index("x") + lax.axis_index("y")

    def own_slot(shard, slot):
        return lax.dynamic_update_slice(lax.empty((N_CHIPS,) + shard.shape, shard.dtype), shard[None], (slot, 0, 0))

    conv_w_pad = jnp.pad(wts["conv_w"][0], ((0, 1), (0, 0)))
    first_shards = [wts["w_in"][0].astype(BF16), conv_w_pad]
    gathering_first = _exchange_start("gather", first_shards, [own_slot(s, chip) for s in first_shards],
                                      name="gather_first_start")
    rest = tuple(k for k in BIG if k != "w_in")
    rest_shards = [wts[k][0].astype(BF16) for k in rest]
    behind_first = gathering_first[4][0, 0].astype(jnp.int32)
    gathering = _exchange_start("gather", rest_shards, [own_slot(s, chip + behind_first) for s in rest_shards],
                                name="gather_rest_start")
    sending = {}

    def wait_first(after):
        _, (w_in_all, conv_w_all) = _exchange_wait(gathering_first, after, name="gather_first_wait")
        return w_in_all, jnp.transpose(conv_w_all, (1, 0, 2)).reshape(32, D_CONV)

    def wait_rest(after):
        _, zones = _exchange_wait(gathering, after, name="gather_rest_wait")
        return dict(zip(rest, zones))

    def send(group, grads):
        keys = tuple(grads)
        zones = [lax.empty((N_CHIPS - 1,) + grads[k].shape[1:], grads[k].dtype) for k in keys]
        sending[group] = (keys, _exchange_start("scatter", [grads[k] for k in keys], zones,
                                                name=f"scatter_{group}_start"))
        return sending[group][1][4]

    comm = dict(first=wait_first, rest=wait_rest, send_mlp=lambda grads: send("mlp", grads),
                send_att=lambda grads: send("att", grads), send_in=lambda grads: send("in", grads))
    vecs = {k: wts[k] for k in ("conv_b", "conv_ln_g", "conv_ln_b", "norm_x_g", "norm_mem_g", "norm_mlp_g")}
    vecs["norm_mix_g"] = wts["norm_mix_g"] + gathering[4][0:1, 0:1]
    vecs["norm_final_g"] = wts["norm_final_g"].reshape(1, D_MODEL)
    grad_x, small = _local_step(x[0], mem[0], loss_target[0], vecs, comm)

    packed, offs = _pack_small(small)
    gath = _allgather_small(packed)
    big, recv = {}, {}
    for group in ("mlp", "att", "in"):
        keys, started = sending[group]
        srcs, zones = _exchange_wait(started, gath, name=f"scatter_{group}_wait")
        big.update(zip(keys, srcs))
        recv.update(zip(keys, zones))
    me_arr = jnp.reshape(chip, (1,)).astype(jnp.int32)
    sums = [_sum_partials(big[k], recv[k], me_arr, name=f"sum_{k}") for k in BIG]
    sib = _swap_with_sibling(sums)
    tot_small = _sum_devices(gath)

    res = {}
    for k, s_mine, s_sib in zip(BIG, sums, sib):
        res[k] = _adamw([s_mine, s_sib], wts[k][0], mom[k][0], var[k][0], name=f"adamw_{k}")

    def piece(key, nrows):
        return tot_small[offs[key]:offs[key] + nrows]

    def small_update(k, gfull):
        return _adamw([gfull], wts[k].reshape(gfull.shape), mom[k].reshape(gfull.shape),
                      var[k].reshape(gfull.shape), name=f"adamw_{k}")

    dcw = piece("conv_w", 32)[:CONV_WIDTH]
    dcw_mine = lax.dynamic_slice_in_dim(dcw, chip * (D_CONV // N_CHIPS), D_CONV // N_CHIPS, axis=1)
    res["conv_w"] = small_update("conv_w", dcw_mine)
    cs = piece("conv_small", 8)
    res["conv_b"] = small_update("conv_b", cs[0:1])
    res["conv_ln_g"] = small_update("conv_ln_g", cs[1:2])
    res["conv_ln_b"] = small_update("conv_ln_b", cs[2:3])
    for k in ("norm_mix_g", "norm_x_g", "norm_mem_g", "norm_mlp_g", "norm_final_g"):
        res[k] = small_update(k, piece(k, 8 * D_MODEL // SMALL_W).reshape(8, D_MODEL)[0:1])
    loss = piece("loss", 8)[0, 0]

    outs = [loss, grad_x[None]]
    for j in range(4):
        outs += [res[k][j].reshape(wts[k].shape) for k in names]
    return tuple(outs)
```

```python
import jax
import jax.numpy as jnp
from jax import lax
from jax.experimental import pallas as pl
from jax.experimental.pallas import tpu as pltpu

F32 = jnp.float32
BF16 = jnp.bfloat16
MESH = pl.DeviceIdType.MESH

D_MODEL = 1024
ATT_HEADS = 8
HEAD_DIM = 64
D_ATT = ATT_HEADS * HEAD_DIM
D_CONV = D_MODEL - D_ATT
DILATIONS = (1, 4, 16)
HALF = 64
ROPE_THETA = 500000.0
ROT_DIM = HEAD_DIM // 4
CONV_WIDTH = 31
CONV_PAD = (CONV_WIDTH - 1) // 2
XATT_HEADS = 4
XATT_HEAD_DIM = D_MODEL // XATT_HEADS
D_FF = 4 * D_MODEL
D_IN = 3 * D_ATT + 2 * D_CONV
EPS = 1e-6
NEG_INF = -1e30
N_CHIPS = 4
N_DEV = 8

ADAM_LR = 0.001
ADAM_B1 = 0.9
ADAM_B2 = 0.999
ADAM_EPS = 1e-08
ADAM_WD = 0.01
ADAM_STEP = 10

VMEM_LIMIT_V7X = 56 * 1024 * 1024
LANES = 128
HALO = 16
CONV_ROWS = 64
ATT_BLOCK = 128
SMALL_W = 512


def _params(*sem):
    return pltpu.CompilerParams(dimension_semantics=sem, vmem_limit_bytes=VMEM_LIMIT_V7X)


def _sds(shape, dtype):
    return jax.ShapeDtypeStruct(shape, dtype)


def _mm_nn(a, w3, *, name, out_dtype=BF16, res=None, relu2=False, tm=1024, tn=None, tk=1024):
    M, K = a.shape
    nsh, _, n = w3.shape
    tm, tk = min(tm, M), min(tk, K)
    tn = tn or min(n, 1024)
    npt, nk = n // tn, K // tk
    nj, N = nsh * npt, nsh * n
    n_out = 2 if relu2 else 1

    def body(*refs):
        a_ref, w_ref = refs[0], refs[1]
        pos = 2
        res_ref = None
        if res is not None:
            res_ref = refs[pos]
            pos += 1
        outs = refs[pos:pos + n_out]
        acc_ref = refs[pos + n_out] if nk > 1 else None

        def finish(acc):
            if res_ref is not None:
                acc = acc + res_ref[...]
            if relu2:
                r = jnp.maximum(acc, 0.0)
                outs[0][...] = r.astype(outs[0].dtype)
                outs[1][...] = (r * r).astype(outs[1].dtype)
            else:
                outs[0][...] = acc.astype(outs[0].dtype)

        part = jnp.dot(a_ref[...], w_ref[...], preferred_element_type=F32)
        if nk == 1:
            finish(part)
        else:
            k = pl.program_id(2)

            @pl.when(k == 0)
            def _():
                acc_ref[...] = part

            @pl.when(k > 0)
            def _():
                acc_ref[...] += part

            @pl.when(k == nk - 1)
            def _():
                finish(acc_ref[...])

    in_specs = [pl.BlockSpec((tm, tk), lambda i, j, k: (i, k)),
                pl.BlockSpec((None, tk, tn), lambda i, j, k: (j // npt, k, j % npt))]
    args = [a, w3]
    if res is not None:
        in_specs.append(pl.BlockSpec((tm, tn), lambda i, j, k: (i, j)))
        args.append(res)
    out_spec = pl.BlockSpec((tm, tn), lambda i, j, k: (i, j))
    out = pl.pallas_call(
        body, name=name, grid=(M // tm, nj, nk), in_specs=in_specs,
        out_specs=[out_spec] * n_out, out_shape=[_sds((M, N), out_dtype)] * n_out,
        scratch_shapes=[pltpu.VMEM((tm, tn), F32)] if nk > 1 else [],
        compiler_params=_params("parallel", "parallel", "arbitrary"))(*args)
    return tuple(out) if relu2 else out[0]


def _mm_nt(dy, w3, *, name, out_dtype=F32, mul=None, tm=1024, tn=None, tko=1024):
    M, N = dy.shape
    nsh, K, n = w3.shape
    tm, tko = min(tm, M), min(tko, K)
    tn = tn or min(n, 1024)
    npt = n // tn
    nj = nsh * npt

    def body(*refs):
        dy_ref, w_ref = refs[0], refs[1]
        pos = 2
        mul_ref = None
        if mul is not None:
            mul_ref = refs[pos]
            pos += 1
        out_ref = refs[pos]
        acc_ref = refs[pos + 1] if nj > 1 else None

        def finish(acc):
            if mul_ref is not None:
                acc = acc * (2.0 * mul_ref[...].astype(F32))
            out_ref[...] = acc.astype(out_ref.dtype)

        part = lax.dot_general(dy_ref[...], w_ref[...], (((1,), (1,)), ((), ())), preferred_element_type=F32)
        if nj == 1:
            finish(part)
        else:
            j = pl.program_id(2)

            @pl.when(j == 0)
            def _():
                acc_ref[...] = part

            @pl.when(j > 0)
            def _():
                acc_ref[...] += part

            @pl.when(j == nj - 1)
            def _():
                finish(acc_ref[...])

    in_specs = [pl.BlockSpec((tm, tn), lambda i, ko, j: (i, j)),
                pl.BlockSpec((None, tko, tn), lambda i, ko, j: (j // npt, ko, j % npt))]
    args = [dy, w3]
    if mul is not None:
        in_specs.append(pl.BlockSpec((tm, tko), lambda i, ko, j: (i, ko)))
        args.append(mul)
    return pl.pallas_call(
        body, name=name, grid=(M // tm, K // tko, nj), in_specs=in_specs,
        out_specs=pl.BlockSpec((tm, tko), lambda i, ko, j: (i, ko)), out_shape=_sds((M, K), out_dtype),
        scratch_shapes=[pltpu.VMEM((tm, tko), F32)] if nj > 1 else [],
        compiler_params=_params("parallel", "parallel", "arbitrary"))(*args)


def _mm_tn(a, dy, nsh, *, name, out_dtype=BF16, tm=2048, tk=1024, tn=None):
    M, K = a.shape
    N = dy.shape[1]
    n = N // nsh
    tm, tk = min(tm, M), min(tk, K)
    tn = tn or min(n, 1024)
    npt = n // tn
    nj, nm = nsh * npt, M // tm

    def body(a_ref, dy_ref, out_ref, acc_ref):
        m = pl.program_id(2)
        part = lax.dot_general(a_ref[...], dy_ref[...], (((0,), (0,)), ((), ())), preferred_element_type=F32)

        @pl.when(m == 0)
        def _():
            acc_ref[...] = part

        @pl.when(m > 0)
        def _():
            acc_ref[...] += part

        @pl.when(m == nm - 1)
        def _():
            out_ref[...] = acc_ref[...].astype(out_ref.dtype)

    return pl.pallas_call(
        body, name=name, grid=(K // tk, nj, nm),
        in_specs=[pl.BlockSpec((tm, tk), lambda kk, j, m: (m, kk)),
                  pl.BlockSpec((tm, tn), lambda kk, j, m: (m, j))],
        out_specs=pl.BlockSpec((None, tk, tn), lambda kk, j, m: (j // npt, kk, j % npt)),
        out_shape=_sds((nsh, K, n), out_dtype),
        scratch_shapes=[pltpu.VMEM((tk, tn), F32)],
        compiler_params=_params("parallel", "parallel", "arbitrary"))(a, dy)


def _rms_fwd(x, g, *, name, tm=512):
    M, Dm = x.shape
    tm = min(tm, M)

    def body(x_ref, g_ref, o_ref):
        xf = x_ref[...]
        r = lax.rsqrt(jnp.mean(xf * xf, axis=-1, keepdims=True) + EPS)
        o_ref[...] = (xf * r * g_ref[...]).astype(o_ref.dtype)

    return pl.pallas_call(
        body, name=name, grid=(M // tm,),
        in_specs=[pl.BlockSpec((tm, Dm), lambda i: (i, 0)), pl.BlockSpec((1, Dm), lambda i: (0, 0))],
        out_specs=pl.BlockSpec((tm, Dm), lambda i: (i, 0)), out_shape=_sds((M, Dm), BF16),
        compiler_params=_params("parallel"))(x, g)


def _rms_bwd(dxn, x, g, dres, *, name, tm=512):
    M, Dm = x.shape
    tm = min(tm, M)
    has_res = dres is not None

    def body(*refs):
        dxn_ref, x_ref, g_ref = refs[:3]
        dres_ref = refs[3] if has_res else None
        dx_ref, dxb_ref, dg_ref = refs[-3:]
        i = pl.program_id(0)
        xf = x_ref[...]
        r = lax.rsqrt(jnp.mean(xf * xf, axis=-1, keepdims=True) + EPS)
        nrm = xf * r
        dxn_f = dxn_ref[...].astype(F32)
        dn = dxn_f * g_ref[...]
        dx = r * (dn - nrm * jnp.mean(dn * nrm, axis=-1, keepdims=True))
        if has_res:
            dx = dx + dres_ref[...]
        dx_ref[...] = dx
        dxb_ref[...] = dx.astype(dxb_ref.dtype)

        @pl.when(i == 0)
        def _():
            dg_ref[...] = jnp.zeros_like(dg_ref)

        dg_ref[0:1, :] += jnp.sum(dxn_f * nrm, axis=0, keepdims=True)

    row = pl.BlockSpec((tm, Dm), lambda i: (i, 0))
    in_specs = [row, row, pl.BlockSpec((1, Dm), lambda i: (0, 0))] + ([row] if has_res else [])
    args = [dxn, x, g] + ([dres] if has_res else [])
    return pl.pallas_call(
        body, name=name, grid=(M // tm,), in_specs=in_specs,
        out_specs=[row, row, pl.BlockSpec((8, Dm), lambda i: (0, 0))],
        out_shape=[_sds((M, Dm), F32), _sds((M, Dm), BF16), _sds((8, Dm), F32)],
        compiler_params=_params("arbitrary"))(*args)


def _loss_head(h, g, target, *, tm=512):
    M, Dm = h.shape
    tm = min(tm, M)

    def body(h_ref, g_ref, t_ref, dh_ref, dhb_ref, dg_ref, loss_ref):
        i = pl.program_id(0)
        hf = h_ref[...]
        r = lax.rsqrt(jnp.mean(hf * hf, axis=-1, keepdims=True) + EPS)
        nrm = hf * r
        gv = g_ref[...]
        err = nrm * gv - t_ref[...]
        dy = err * (1.0 / Dm)
        dn = dy * gv
        dh = r * (dn - nrm * jnp.mean(dn * nrm, axis=-1, keepdims=True))
        dh_ref[...] = dh
        dhb_ref[...] = dh.astype(dhb_ref.dtype)

        @pl.when(i == 0)
        def _():
            dg_ref[...] = jnp.zeros_like(dg_ref)
            loss_ref[...] = jnp.zeros_like(loss_ref)

        dg_ref[0:1, :] += jnp.sum(dy * nrm, axis=0, keepdims=True)
        part = 0.5 * jnp.sum(jnp.mean(err * err, axis=-1, keepdims=True), axis=0, keepdims=True)
        sel = (lax.broadcasted_iota(jnp.int32, (8, 128), 0) == 0) & (lax.broadcasted_iota(jnp.int32, (8, 128), 1) == 0)
        loss_ref[...] += jnp.where(sel, part, 0.0)

    row = pl.BlockSpec((tm, Dm), lambda i: (i, 0))
    return pl.pallas_call(
        body, name="loss_head", grid=(M // tm,),
        in_specs=[row, pl.BlockSpec((1, Dm), lambda i: (0, 0)), row],
        out_specs=[row, row, pl.BlockSpec((8, Dm), lambda i: (0, 0)), pl.BlockSpec((8, 128), lambda i: (0, 0))],
        out_shape=[_sds((M, Dm), F32), _sds((M, Dm), BF16), _sds((8, Dm), F32), _sds((8, 128), F32)],
        compiler_params=_params("arbitrary"))(h, g, target)


def _class_spec(tm, d, width):
    return pl.BlockSpec((d, tm // d, width), lambda i: (0, i, 0))


def _row_scratch(tm, width):
    return pltpu.VMEM((width // LANES, tm, LANES), F32)


def _fill(scr, val):
    for c in range(scr.shape[0]):
        scr[c] = val[:, c * LANES:(c + 1) * LANES]


def _to_classes(scr, out_ref, d):
    n = scr.shape[1] // d
    for r in range(d):
        for c in range(scr.shape[0]):
            out_ref[r, :, c * LANES:(c + 1) * LANES] = scr[c, pl.ds(r, n, stride=d), :].astype(out_ref.dtype)


def _from_classes(in_ref, scr, d):
    n = scr.shape[1] // d
    for r in range(d):
        blk = in_ref[r].astype(F32)
        for c in range(scr.shape[0]):
            scr[c, pl.ds(r, n, stride=d), :] = blk[:, c * LANES:(c + 1) * LANES]
    return jnp.concatenate([scr[c] for c in range(scr.shape[0])], axis=1)


def _rope_tables(S):
    half = ROT_DIM // 2
    freqs = ROPE_THETA ** (-jnp.arange(0, ROT_DIM, 2, dtype=F32) / ROT_DIM)
    ang = jnp.arange(S, dtype=F32)[:, None] * freqs[None, :]
    cos, sin = jnp.cos(ang), jnp.sin(ang)
    ones = jnp.ones((S, HEAD_DIM - ROT_DIM), F32)
    zeros = jnp.zeros((S, HEAD_DIM - ROT_DIM), F32)
    zh = jnp.zeros((S, half), F32)
    c = jnp.concatenate([cos, cos, ones], axis=1)
    sa = jnp.concatenate([-sin, zh, zeros], axis=1)
    sb = jnp.concatenate([zh, sin, zeros], axis=1)
    return tuple(jnp.tile(t, (1, LANES // HEAD_DIM)) for t in (c, sa, sb))


def _rope_fwd(y, tables, *, tm=512):
    S = y.shape[0]
    W = 2 * D_ATT
    tm = min(tm, S)
    half = ROT_DIM // 2
    dils = [d for d in DILATIONS if d > 1]

    def body(y_ref, c_ref, sa_ref, sb_ref, qk_ref, *rest):
        qk_outs, v_outs = rest[:len(dils)], rest[len(dils):2 * len(dils)]
        scr_qk, scr_v = rest[2 * len(dils):]
        t = y_ref[:, 0:W].astype(F32)
        rep = W // LANES
        c, sa, sb = (jnp.tile(r[...], (1, rep)) for r in (c_ref, sa_ref, sb_ref))
        rot = t * c + pltpu.roll(t, W - half, axis=1) * sa + pltpu.roll(t, half, axis=1) * sb
        qk_ref[...] = rot.astype(qk_ref.dtype)
        _fill(scr_qk, rot)
        _fill(scr_v, y_ref[:, W:W + D_ATT].astype(F32))
        for d, qo, vo in zip(dils, qk_outs, v_outs):
            _to_classes(scr_qk, qo, d)
            _to_classes(scr_v, vo, d)

    tab = pl.BlockSpec((tm, LANES), lambda i: (i, 0))
    out = pl.pallas_call(
        body, name="rope_fwd", grid=(S // tm,),
        in_specs=[pl.BlockSpec((tm, 3 * D_ATT), lambda i: (i, 0)), tab, tab, tab],
        out_specs=[pl.BlockSpec((tm, W), lambda i: (i, 0))] + [_class_spec(tm, d, W) for d in dils]
        + [_class_spec(tm, d, D_ATT) for d in dils],
        out_shape=[_sds((S, W), BF16)] + [_sds((d, S // d, W), BF16) for d in dils]
        + [_sds((d, S // d, D_ATT), BF16) for d in dils],
        scratch_shapes=[_row_scratch(tm, W), _row_scratch(tm, D_ATT)],
        compiler_params=_params("parallel"))(y, *tables)
    qk = [out[0]] + [o.reshape(S, W) for o in out[1:1 + len(dils)]]
    v = [None] + [o.reshape(S, D_ATT) for o in out[1 + len(dils):]]
    return qk, v


def _assemble_dy(dq, dk, dv, dag, tables, *, tm=512):
    S = dag.shape[0]
    tm = min(tm, S)
    half = ROT_DIM // 2
    W = D_ATT
    n_pat = len(DILATIONS)

    def body(*refs):
        groups = [refs[g * n_pat:(g + 1) * n_pat] for g in range(3)]
        dag_ref, c_ref, sa_ref, sb_ref, o_ref, scr = refs[3 * n_pat:]
        rep = W // LANES
        c, sa, sb = (jnp.tile(r[...], (1, rep)) for r in (c_ref, sa_ref, sb_ref))

        def total(rs):
            acc = rs[0][...].astype(F32)
            for d, r in zip(DILATIONS[1:], rs[1:]):
                acc = acc + _from_classes(r, scr, d)
            return acc

        def unrope(dr):
            return dr * c + pltpu.roll(dr * sa, half, axis=1) + pltpu.roll(dr * sb, W - half, axis=1)

        o_ref[:, 0:W] = unrope(total(groups[0])).astype(o_ref.dtype)
        o_ref[:, W:2 * W] = unrope(total(groups[1])).astype(o_ref.dtype)
        o_ref[:, 2 * W:3 * W] = total(groups[2]).astype(o_ref.dtype)
        o_ref[:, 3 * W:] = dag_ref[...]

    specs = [pl.BlockSpec((tm, W), lambda i: (i, 0))] + [_class_spec(tm, d, W) for d in DILATIONS[1:]]
    tab = pl.BlockSpec((tm, LANES), lambda i: (i, 0))
    args = [a if d == 1 else a.reshape(d, S // d, W) for grp in (dq, dk, dv) for d, a in zip(DILATIONS, grp)]
    return pl.pallas_call(
        body, name="assemble_dy", grid=(S // tm,),
        in_specs=specs * 3 + [pl.BlockSpec((tm, 2 * D_CONV), lambda i: (i, 0)), tab, tab, tab],
        out_specs=pl.BlockSpec((tm, D_IN), lambda i: (i, 0)), out_shape=_sds((S, D_IN), BF16),
        scratch_shapes=[_row_scratch(tm, W)],
        compiler_params=_params("parallel"))(*args, dag, *tables)


def _seq_specs(L, tb, col):
    nb, per, nh = L // tb, tb // HALF, L // HALF
    centre = pl.BlockSpec((tb, D_ATT), lambda r, i: (r * nb + i, col))
    prev = pl.BlockSpec((HALF, D_ATT), lambda r, i: (r * nh + jnp.maximum(i * per - 1, 0), col))
    nxt = pl.BlockSpec((HALF, D_ATT), lambda r, i: (r * nh + jnp.minimum((i + 1) * per, nh - 1), col))
    return prev, centre, nxt


def _band_mask(i, tq, L, centre_is_query):
    if centre_is_query:
        shape = (tq, tq + 2 * HALF)
        c_idx = lax.broadcasted_iota(jnp.int32, shape, 0)
        w_idx = lax.broadcasted_iota(jnp.int32, shape, 1)
    else:
        shape = (tq + 2 * HALF, tq)
        w_idx = lax.broadcasted_iota(jnp.int32, shape, 0)
        c_idx = lax.broadcasted_iota(jnp.int32, shape, 1)
    diff = w_idx - c_idx
    wpos = i * tq - HALF + w_idx
    return (diff >= 0) & (diff <= 2 * HALF) & (wpos >= 0) & (wpos < L)


def _lane_groups():
    for c0 in range(0, D_ATT, LANES):
        yield slice(c0, c0 + LANES)


def _first_head(rows):
    return lax.broadcasted_iota(jnp.int32, (rows, LANES), 1) < HEAD_DIM


def _split_pair(x, first):
    zero = jnp.zeros_like(x)
    return jnp.where(first, x, zero), jnp.where(first, zero, x)


def _nt(a, b):
    return lax.dot_general(a, b, (((1,), (1,)), ((), ())), preferred_element_type=F32)


def _tn(a, b):
    return lax.dot_general(a, b, (((0,), (0,)), ((), ())), preferred_element_type=F32)


ATT_SCALE = HEAD_DIM ** -0.5


def _att_fwd(qk, v_src, d, *, name):
    S = qk.shape[0]
    L = S // d
    tq = min(ATT_BLOCK, L)
    v_arr, v_col = v_src

    def body(q_ref, kp_ref, kc_ref, kn_ref, vp_ref, vc_ref, vn_ref, o_ref, lse_ref):
        i = pl.program_id(1)
        valid = _band_mask(i, tq, L, True)
        q = q_ref[...] * ATT_SCALE
        kwin = jnp.concatenate([kp_ref[...], kc_ref[...], kn_ref[...]], axis=0)
        vwin = jnp.concatenate([vp_ref[...], vc_ref[...], vn_ref[...]], axis=0)
        first = _first_head(tq)
        groups = list(_lane_groups())
        heads = [(ls, t) for ls in groups for t in _split_pair(q[:, ls], first)]
        s = [jnp.where(valid, _nt(t, kwin[:, ls]), NEG_INF) for ls, t in heads]
        m = [jnp.max(t, axis=-1, keepdims=True) for t in s]
        p = [jnp.exp(t - mm) for t, mm in zip(s, m)]
        den = [jnp.sum(t, axis=-1, keepdims=True) for t in p]
        o = [jnp.dot(t.astype(BF16), vwin[:, ls], preferred_element_type=F32) * (1.0 / dd)
             for t, dd, (ls, _) in zip(p, den, heads)]
        lse = [mm + jnp.log(dd) for mm, dd in zip(m, den)]
        for g, ls in enumerate(groups):
            o_ref[:, ls] = jnp.where(first, o[2 * g], o[2 * g + 1])
            lse_ref[:, ls] = jnp.where(first, lse[2 * g], lse[2 * g + 1])

    _, qc, _ = _seq_specs(L, tq, 0)
    kp, kc, kn = _seq_specs(L, tq, 1)
    vp, vc, vn = _seq_specs(L, tq, v_col)
    out = pl.BlockSpec((tq, D_ATT), lambda r, i: (r * (L // tq) + i, 0))
    return pl.pallas_call(
        body, name=name, grid=(d, L // tq),
        in_specs=[qc, kp, kc, kn, vp, vc, vn], out_specs=[out, out],
        out_shape=[_sds((S, D_ATT), F32)] * 2,
        compiler_params=_params("parallel", "parallel"))(qk, qk, qk, qk, v_arr, v_arr, v_arr)


def _att_combine(outs, lses, *, tm=512):
    S = outs[0].shape[0]
    tm = min(tm, S)
    dils = DILATIONS[1:]
    n_d = len(dils)

    def body(*refs):
        o_refs, l_refs = refs[0:1 + n_d], refs[1 + n_d:2 + 2 * n_d]
        att_ref, lg_ref = refs[2 + 2 * n_d:4 + 2 * n_d]
        lg_outs = refs[4 + 2 * n_d:4 + 3 * n_d]
        scr = refs[4 + 3 * n_d:]
        scr_o, scr_l, scr_lg = scr[:n_d], scr[n_d:2 * n_d], scr[2 * n_d]
        ls = [l_refs[0][...]] + [_from_classes(r, s, d) for r, s, d in zip(l_refs[1:], scr_l, dils)]
        os_ = [o_refs[0][...]] + [_from_classes(r, s, d) for r, s, d in zip(o_refs[1:], scr_o, dils)]
        mx = ls[0]
        for l in ls[1:]:
            mx = jnp.maximum(mx, l)
        es = [jnp.exp(l - mx) for l in ls]
        tot = es[0]
        num = es[0] * os_[0]
        for e, o in zip(es[1:], os_[1:]):
            tot = tot + e
            num = num + e * o
        att_ref[...] = (num / tot).astype(att_ref.dtype)
        lg = mx + jnp.log(tot)
        lg_ref[...] = lg
        _fill(scr_lg, lg)
        for d, out in zip(dils, lg_outs):
            _to_classes(scr_lg, out, d)

    nat = pl.BlockSpec((tm, D_ATT), lambda i: (i, 0))
    specs = [nat] + [_class_spec(tm, d, D_ATT) for d in dils]
    view = lambda arrs: [arrs[0]] + [a.reshape(d, S // d, D_ATT) for a, d in zip(arrs[1:], dils)]
    out = pl.pallas_call(
        body, name="att_combine", grid=(S // tm,), in_specs=specs * 2,
        out_specs=[nat, nat] + specs[1:],
        out_shape=[_sds((S, D_ATT), BF16), _sds((S, D_ATT), F32)] + [_sds((d, S // d, D_ATT), F32) for d in dils],
        scratch_shapes=[_row_scratch(tm, D_ATT)] * (2 * n_d + 1),
        compiler_params=_params("parallel"))(*view(list(outs)), *view(list(lses)))
    return out[0], [out[1]] + [o.reshape(S, D_ATT) for o in out[2:]]


def _att_delta(dac, att, *, tm=512):
    S = att.shape[0]
    tm = min(tm, S)
    dils = DILATIONS[1:]
    n_d = len(dils)

    def body(do_ref, o_ref, dl_ref, *rest):
        dl_outs, do_outs = rest[:n_d], rest[n_d:2 * n_d]
        scr_dl, scr_do = rest[2 * n_d:]
        do = do_ref[...].astype(F32)
        prod = do * o_ref[...].astype(F32)
        per_head = [jnp.broadcast_to(jnp.sum(prod[:, h * HEAD_DIM:(h + 1) * HEAD_DIM], axis=-1, keepdims=True),
                                     (tm, HEAD_DIM)) for h in range(ATT_HEADS)]
        dl = jnp.concatenate(per_head, axis=1)
        dl_ref[...] = dl
        _fill(scr_dl, dl)
        _fill(scr_do, do)
        for d, dlo, doo in zip(dils, dl_outs, do_outs):
            _to_classes(scr_dl, dlo, d)
            _to_classes(scr_do, doo, d)

    blk = pl.BlockSpec((tm, D_ATT), lambda i: (i, 0))
    out = pl.pallas_call(
        body, name="att_delta", grid=(S // tm,), in_specs=[blk, blk],
        out_specs=[blk] + [_class_spec(tm, d, D_ATT) for d in dils] * 2,
        out_shape=[_sds((S, D_ATT), F32)] + [_sds((d, S // d, D_ATT), F32) for d in dils]
        + [_sds((d, S // d, D_ATT), BF16) for d in dils],
        scratch_shapes=[_row_scratch(tm, D_ATT), _row_scratch(tm, D_ATT)],
        compiler_params=_params("parallel"))(dac, att)
    delta = [out[0]] + [o.reshape(S, D_ATT) for o in out[1:1 + n_d]]
    do = [None] + [o.reshape(S, D_ATT) for o in out[1 + n_d:]]
    return delta, do


def _att_bwd(qk, v_src, do_src, lg, delta, d, *, name):
    S = qk.shape[0]
    L = S // d
    tq = min(ATT_BLOCK, L)
    nb, per, nh = L // tq, tq // HALF, L // HALF
    win = tq + 2 * HALF
    lead = tq - HALF
    acc_rows = lead + win
    (v_arr, v_col), (do_arr, do_col) = v_src, do_src

    def body(q_ref, kp_ref, kc_ref, kn_ref, vp_ref, vc_ref, vn_ref, do_ref, lg_ref, dl_ref,
             dq_ref, dk_ref, dv_ref, acc_k, acc_v):
        i = pl.program_id(1)

        @pl.when(i == 0)
        def _():
            acc_k[...] = jnp.zeros_like(acc_k)
            acc_v[...] = jnp.zeros_like(acc_v)

        @pl.when(i < nb)
        def _():
            valid = _band_mask(i, tq, L, True)
            q, do = q_ref[...] * ATT_SCALE, do_ref[...]
            kwin = jnp.concatenate([kp_ref[...], kc_ref[...], kn_ref[...]], axis=0)
            vwin = jnp.concatenate([vp_ref[...], vc_ref[...], vn_ref[...]], axis=0)
            first, first_w = _first_head(tq), _first_head(win)
            groups = list(_lane_groups())
            cols = [c for ls in groups for c in (ls.start, ls.start + HEAD_DIM)]
            lanes = [ls for ls in groups for _ in range(2)]
            qh = [t for ls in groups for t in _split_pair(q[:, ls], first)]
            doh = [t for ls in groups for t in _split_pair(do[:, ls], first)]
            s = [jnp.where(valid, _nt(t, kwin[:, ls]), NEG_INF) for t, ls in zip(qh, lanes)]
            dp = [_nt(t, vwin[:, ls]) for t, ls in zip(doh, lanes)]
            p = [jnp.exp(t - lg_ref[:, c:c + 1]) for t, c in zip(s, cols)]
            ds = [(pp * (t - dl_ref[:, c:c + 1])).astype(BF16) for pp, t, c in zip(p, dp, cols)]
            dq = [jnp.dot(t, kwin[:, ls], preferred_element_type=F32) for t, ls in zip(ds, lanes)]
            dk = [_tn(t, q[:, ls]) for t, ls in zip(ds, lanes)]
            dv = [_tn(pp.astype(BF16), do[:, ls]) for pp, ls in zip(p, lanes)]
            for g, ls in enumerate(groups):
                dq_ref[:, ls] = (jnp.where(first, dq[2 * g], dq[2 * g + 1]) * ATT_SCALE).astype(dq_ref.dtype)
                acc_k[lead:, ls] += jnp.where(first_w, dk[2 * g], dk[2 * g + 1])
                acc_v[lead:, ls] += jnp.where(first_w, dv[2 * g], dv[2 * g + 1])

        for acc, out in ((acc_k, dk_ref), (acc_v, dv_ref)):
            out[...] = acc[0:tq, :].astype(out.dtype)
            kept = acc[tq:, :]
            acc[0:acc_rows - tq, :] = kept
            acc[acc_rows - tq:, :] = jnp.zeros((tq, D_ATT), F32)

    def seq(col):
        blk = lambda i: jnp.minimum(i, nb - 1)
        centre = pl.BlockSpec((tq, D_ATT), lambda r, i: (r * nb + blk(i), col))
        prev = pl.BlockSpec((HALF, D_ATT), lambda r, i: (r * nh + jnp.maximum(blk(i) * per - 1, 0), col))
        nxt = pl.BlockSpec((HALF, D_ATT), lambda r, i: (r * nh + jnp.minimum((blk(i) + 1) * per, nh - 1), col))
        return prev, centre, nxt

    _, qc, _ = seq(0)
    kp, kc, kn = seq(1)
    vp, vc, vn = seq(v_col)
    _, doc, _ = seq(do_col)
    late = pl.BlockSpec((tq, D_ATT), lambda r, i: (r * nb + jnp.maximum(i - 1, 0), 0))
    return pl.pallas_call(
        body, name=name, grid=(d, nb + 1),
        in_specs=[qc, kp, kc, kn, vp, vc, vn, doc, qc, qc], out_specs=[qc, late, late],
        out_shape=[_sds((S, D_ATT), BF16)] * 3,
        scratch_shapes=[pltpu.VMEM((acc_rows, D_ATT), F32), pltpu.VMEM((acc_rows, D_ATT), F32)],
        compiler_params=_params("arbitrary", "arbitrary"))(qk, qk, qk, qk, v_arr, v_arr, v_arr, do_arr, lg, delta)


def _sigmoid(x):
    return 1.0 / (1.0 + jnp.exp(-x))


def _halo_specs(S, T, width, col):
    last = S // HALO - 1
    per = T // HALO
    centre = pl.BlockSpec((T, width), lambda i: (i, col))
    prev = pl.BlockSpec((HALO, width), lambda i: (jnp.maximum(i * per - 1, 0), col))
    nxt = pl.BlockSpec((HALO, width), lambda i: (jnp.minimum((i + 1) * per, last), col))
    return prev, centre, nxt


def _window_scratch(T, C):
    return pltpu.VMEM((8, T + 2 * HALO, C), F32)


def _fill_window(buf, prev, centre, nxt, T):
    buf[0, 0:HALO, :] = prev
    buf[0, HALO:HALO + T, :] = centre
    buf[0, HALO + T:, :] = nxt
    rows = T + 2 * HALO - 8
    for s in range(1, 8):
        buf[s, 0:rows, :] = buf[0, s:s + rows, :]


def _tap_reads(buf, first_off, step, r0, ls):
    by_slab = {}
    for k in range(CONV_WIDTH):
        off = first_off + step * k
        by_slab.setdefault(off % 8, []).append((k, off - off % 8))
    for s, taps in by_slab.items():
        lo = min(a for _, a in taps)
        hi = max(a for _, a in taps)
        rows = buf[s, pl.ds(lo + r0, CONV_ROWS + hi - lo), ls]
        for k, a in taps:
            yield k, rows[a - lo:a - lo + CONV_ROWS]


def _depthwise(buf, w_ref, out_ref, T, C, first_off, step):
    def row_tile(t, carry):
        r0 = pl.multiple_of(t * CONV_ROWS, CONV_ROWS)
        for c0 in range(0, C, LANES):
            ls = slice(c0, c0 + LANES)
            acc = jnp.zeros((CONV_ROWS, LANES), F32)
            for k, rows in _tap_reads(buf, first_off, step, r0, ls):
                acc = acc + rows * w_ref[k:k + 1, ls]
            out_ref[pl.ds(r0, CONV_ROWS), ls] = acc
        return carry

    lax.fori_loop(0, T // CONV_ROWS, row_tile, 0)


def _conv_fwd(y, conv_w32, conv_b, ln_g, ln_b, *, T=512):
    S = y.shape[0]
    T = min(T, S)
    nblk = S // T
    C = D_CONV

    def body(ap, ac, an, gp, gc, gn, w_ref, b_ref, lg_ref, lb_ref, cv_ref, u1_ref, buf):
        i = pl.program_id(0)

        def glu(a_ref, g_ref):
            return a_ref[...].astype(F32) * _sigmoid(g_ref[...].astype(F32))

        _fill_window(buf, jnp.where(i > 0, glu(ap, gp), 0.0), glu(ac, gc),
                     jnp.where(i < nblk - 1, glu(an, gn), 0.0), T)
        _depthwise(buf, w_ref, u1_ref, T, C, HALO - CONV_PAD, 1)
        u1 = u1_ref[...] + b_ref[...]
        u1_ref[...] = u1
        mu = jnp.mean(u1, axis=-1, keepdims=True)
        xc = u1 - mu
        rstd = lax.rsqrt(jnp.mean(xc * xc, axis=-1, keepdims=True) + EPS)
        u2 = xc * rstd * lg_ref[...] + lb_ref[...]
        cv_ref[...] = (u2 * _sigmoid(u2)).astype(cv_ref.dtype)

    ap, ac, an = _halo_specs(S, T, C, 3)
    gp, gc, gn = _halo_specs(S, T, C, 4)
    vec = pl.BlockSpec((1, C), lambda i: (0, 0))
    out = pl.BlockSpec((T, C), lambda i: (i, 0))
    return pl.pallas_call(
        body, name="conv_fwd", grid=(nblk,),
        in_specs=[ap, ac, an, gp, gc, gn, pl.BlockSpec((32, C), lambda i: (0, 0)), vec, vec, vec],
        out_specs=[out, out], out_shape=[_sds((S, C), BF16), _sds((S, C), F32)],
        scratch_shapes=[_window_scratch(T, C)],
        compiler_params=_params("parallel"))(y, y, y, y, y, y, conv_w32, conv_b, ln_g, ln_b)


def _conv_bwd(dac, u1, y, conv_w32, ln_g, ln_b, *, T=512):
    S = y.shape[0]
    T = min(T, S)
    nblk = S // T
    C = D_CONV

    def body(dp, dc, dn, up, uc, un, ap, ac, an, gp, gc, gn, w_ref, lg_ref, lb_ref,
             dag_ref, dw_ref, dsm_ref, bufd, bufu, du0_scr, dw_acc):
        i = pl.program_id(0)
        lg = lg_ref[...]

        def du1_of(dcv_ref, u1_ref):
            u1 = u1_ref[...]
            mu = jnp.mean(u1, axis=-1, keepdims=True)
            xc = u1 - mu
            rstd = lax.rsqrt(jnp.mean(xc * xc, axis=-1, keepdims=True) + EPS)
            xhat = xc * rstd
            u2 = xhat * lg + lb_ref[...]
            sg = _sigmoid(u2)
            du2 = dcv_ref[...].astype(F32) * (sg * (1.0 + u2 * (1.0 - sg)))
            dxh = du2 * lg
            du1 = rstd * (dxh - jnp.mean(dxh, axis=-1, keepdims=True)
                          - xhat * jnp.mean(dxh * xhat, axis=-1, keepdims=True))
            return du1, du2, xhat

        def glu(a_ref, g_ref):
            return a_ref[...].astype(F32) * _sigmoid(g_ref[...].astype(F32))

        @pl.when(i == 0)
        def _():
            dw_ref[...] = jnp.zeros_like(dw_ref)
            dsm_ref[...] = jnp.zeros_like(dsm_ref)

        du1_c, du2_c, xhat_c = du1_of(dc, uc)
        dsm_ref[0:1, :] += jnp.sum(du1_c, axis=0, keepdims=True)
        dsm_ref[1:2, :] += jnp.sum(du2_c * xhat_c, axis=0, keepdims=True)
        dsm_ref[2:3, :] += jnp.sum(du2_c, axis=0, keepdims=True)
        _fill_window(bufd, jnp.where(i > 0, du1_of(dp, up)[0], 0.0), du1_c,
                     jnp.where(i < nblk - 1, du1_of(dn, un)[0], 0.0), T)
        _fill_window(bufu, jnp.where(i > 0, glu(ap, gp), 0.0), glu(ac, gc),
                     jnp.where(i < nblk - 1, glu(an, gn), 0.0), T)

        _depthwise(bufd, w_ref, du0_scr, T, C, HALO + CONV_PAD, -1)
        dw_acc[...] = jnp.zeros_like(dw_acc)

        def dw_tile(t, carry):
            r0 = pl.multiple_of(t * CONV_ROWS, CONV_ROWS)
            for c0 in range(0, C, LANES):
                ls = slice(c0, c0 + LANES)
                d = bufd[0, pl.ds(HALO + r0, CONV_ROWS), ls]
                for k, rows in _tap_reads(bufu, HALO - CONV_PAD, 1, r0, ls):
                    prod = d * rows
                    part = prod[0:8]
                    for j in range(8, CONV_ROWS, 8):
                        part = part + prod[j:j + 8]
                    dw_acc[k, :, ls] += part
            return carry

        lax.fori_loop(0, T // CONV_ROWS, dw_tile, 0)
        for k in range(CONV_WIDTH):
            dw_ref[k:k + 1, :] += jnp.sum(dw_acc[k], axis=0, keepdims=True)
        du0 = du0_scr[...]
        a = ac[...].astype(F32)
        sg = _sigmoid(gc[...].astype(F32))
        dag_ref[:, 0:C] = (du0 * sg).astype(dag_ref.dtype)
        dag_ref[:, C:] = (du0 * a * sg * (1.0 - sg)).astype(dag_ref.dtype)

    dp, dc, dn = _halo_specs(S, T, C, 1)
    up, uc, un = _halo_specs(S, T, C, 0)
    ap, ac, an = _halo_specs(S, T, C, 3)
    gp, gc, gn = _halo_specs(S, T, C, 4)
    vec = pl.BlockSpec((1, C), lambda i: (0, 0))
    return pl.pallas_call(
        body, name="conv_bwd", grid=(nblk,),
        in_specs=[dp, dc, dn, up, uc, un, ap, ac, an, gp, gc, gn,
                  pl.BlockSpec((32, C), lambda i: (0, 0)), vec, vec],
        out_specs=[pl.BlockSpec((T, 2 * C), lambda i: (i, 0)), pl.BlockSpec((32, C), lambda i: (0, 0)),
                   pl.BlockSpec((8, C), lambda i: (0, 0))],
        out_shape=[_sds((S, 2 * C), BF16), _sds((32, C), F32), _sds((8, C), F32)],
        scratch_shapes=[_window_scratch(T, C), _window_scratch(T, C), pltpu.VMEM((T, C), F32),
                        pltpu.VMEM((CONV_WIDTH, 8, C), F32)],
        compiler_params=_params("arbitrary"))(dac, dac, dac, u1, u1, u1, y, y, y, y, y, y, conv_w32, ln_g, ln_b)


def _xatt_fwd(xq, xk, xv, *, tm=512):
    S = xq.shape[0]
    M = xk.shape[0]
    tm = min(tm, S)
    scale = XATT_HEAD_DIM ** -0.5

    def body(q_ref, k_ref, v_ref, o_ref):
        for h in range(XATT_HEADS):
            sl = slice(h * XATT_HEAD_DIM, (h + 1) * XATT_HEAD_DIM)
            s = _nt(q_ref[:, sl], k_ref[:, sl]) * scale
            e = jnp.exp(s - jnp.max(s, axis=-1, keepdims=True))
            p = e / jnp.sum(e, axis=-1, keepdims=True)
            o_ref[:, sl] = jnp.dot(p.astype(BF16), v_ref[:, sl], preferred_element_type=F32).astype(o_ref.dtype)

    row = pl.BlockSpec((tm, D_MODEL), lambda i: (i, 0))
    full = pl.BlockSpec((M, D_MODEL), lambda i: (0, 0))
    return pl.pallas_call(
        body, name="xatt_fwd", grid=(S // tm,), in_specs=[row, full, full], out_specs=row,
        out_shape=_sds((S, D_MODEL), BF16), compiler_params=_params("parallel"))(xq, xk, xv)


def _xatt_bwd(xq, xk, xv, dxo, *, tm=512):
    S = xq.shape[0]
    M = xk.shape[0]
    tm = min(tm, S)
    scale = XATT_HEAD_DIM ** -0.5

    def body(q_ref, k_ref, v_ref, do_ref, dq_ref, dk_ref, dv_ref):
        i = pl.program_id(0)

        @pl.when(i == 0)
        def _():
            dk_ref[...] = jnp.zeros_like(dk_ref)
            dv_ref[...] = jnp.zeros_like(dv_ref)

        for h in range(XATT_HEADS):
            sl = slice(h * XATT_HEAD_DIM, (h + 1) * XATT_HEAD_DIM)
            q, k, v, do = q_ref[:, sl], k_ref[:, sl], v_ref[:, sl], do_ref[:, sl]
            s = _nt(q, k) * scale
            e = jnp.exp(s - jnp.max(s, axis=-1, keepdims=True))
            p = e / jnp.sum(e, axis=-1, keepdims=True)
            dp = _nt(do, v)
            ds = p * (dp - jnp.sum(dp * p, axis=-1, keepdims=True))
            dsb = ds.astype(BF16)
            dq_ref[:, sl] = (jnp.dot(dsb, k, preferred_element_type=F32) * scale).astype(dq_ref.dtype)
            dv_ref[:, sl] += _tn(p.astype(BF16), do)
            dk_ref[:, sl] += _tn(dsb, q) * scale

    row = pl.BlockSpec((tm, D_MODEL), lambda i: (i, 0))
    full = pl.BlockSpec((M, D_MODEL), lambda i: (0, 0))
    return pl.pallas_call(
        body, name="xatt_bwd", grid=(S // tm,), in_specs=[row, full, full, row], out_specs=[row, full, full],
        out_shape=[_sds((S, D_MODEL), BF16), _sds((M, D_MODEL), F32), _sds((M, D_MODEL), F32)],
        compiler_params=_params("arbitrary"))(xq, xk, xv, dxo)


def _row_tile(R):
    for t in (256, 128, 64, 32, 16, 8):
        if R % t == 0:
            return t
    return R


def _sum_partials(own, recv, me, *, name):
    _, R, C = own.shape
    t = _row_tile(R)

    def body(me_ref, own_ref, r_ref, o_ref):
        o_ref[...] = ((own_ref[...].astype(F32) + r_ref[0].astype(F32)) + r_ref[1].astype(F32)) + r_ref[2].astype(F32)

    return pl.pallas_call(
        body, name=name,
        grid_spec=pltpu.PrefetchScalarGridSpec(
            num_scalar_prefetch=1, grid=(R // t,),
            in_specs=[pl.BlockSpec((None, t, C), lambda i, me_ref: (me_ref[0], i, 0)),
                      pl.BlockSpec((3, t, C), lambda i, me_ref: (0, i, 0))],
            out_specs=pl.BlockSpec((t, C), lambda i, me_ref: (i, 0))),
        out_shape=_sds((R, C), F32), compiler_params=_params("parallel"))(me, own, recv)


def _adamw_math(w, g, m, v):
    m2 = ADAM_B1 * m + (1.0 - ADAM_B1) * g
    v2 = ADAM_B2 * v + (1.0 - ADAM_B2) * (g * g)
    m_hat = m2 / (1.0 - ADAM_B1 ** ADAM_STEP)
    v_hat = v2 / (1.0 - ADAM_B2 ** ADAM_STEP)
    delta = -ADAM_LR * (m_hat / (jnp.sqrt(v_hat) + ADAM_EPS) + ADAM_WD * w)
    return delta, m2, v2


def _adamw(parts, w, m, v, *, name):
    R, C = w.shape
    t = _row_tile(R)
    n = len(parts)

    def body(*refs):
        w_ref, m_ref, v_ref = refs[n:n + 3]
        g_ref, d_ref, m2_ref, v2_ref = refs[n + 3:]
        g = refs[0][...]
        for r in refs[1:n]:
            g = g + r[...]
        delta, m2, v2 = _adamw_math(w_ref[...], g, m_ref[...], v_ref[...])
        g_ref[...] = g
        d_ref[...] = delta
        m2_ref[...] = m2
        v2_ref[...] = v2

    blk = pl.BlockSpec((t, C), lambda i: (i, 0))
    return pl.pallas_call(
        body, name=name, grid=(R // t,), in_specs=[blk] * (n + 3), out_specs=[blk] * 4,
        out_shape=[_sds((R, C), F32)] * 4, compiler_params=_params("parallel"))(*parts, w, m, v)


def _sum_devices(gathered):
    _, R, C = gathered.shape

    def body(g_ref, o_ref):
        acc = g_ref[0]
        for k in range(1, N_DEV):
            acc = acc + g_ref[k]
        o_ref[...] = acc

    return pl.pallas_call(body, name="sum_devices", out_shape=_sds((R, C), F32))(gathered)


def _chip_peers():
    x, y = lax.axis_index("x"), lax.axis_index("y")
    return [(1 - x, y), (x, 1 - y), (1 - x, 1 - y)]


HBM_SPEC = pl.BlockSpec(memory_space=pltpu.HBM)
SEM_SPEC = pl.BlockSpec(memory_space=pltpu.SEMAPHORE)


def _exchange_start(mode, srcs, zones, *, name):
    n = len(srcs)

    def body(*refs):
        ins, lands = refs[:n], refs[n:2 * n]
        send_sems, recv_sems = refs[2 * n:3 * n], refs[3 * n:4 * n]
        token = refs[-1]
        c = lax.axis_index("c")
        mine = 2 * lax.axis_index("x") + lax.axis_index("y")
        for t in range(n):
            for k, (px, py) in enumerate(_chip_peers()):
                if mode == "gather":
                    s, d = ins[t], lands[t].at[mine]
                else:
                    s, d = ins[t].at[2 * px + py], lands[t].at[k]
                pltpu.make_async_remote_copy(src_ref=s, dst_ref=d, send_sem=send_sems[t], recv_sem=recv_sems[t],
                                             device_id=(px, py, c), device_id_type=MESH).start()
        token[...] = jnp.zeros_like(token)

    hbm = lambda a: pltpu.with_memory_space_constraint(a, pltpu.HBM)
    out = pl.pallas_call(
        body, name=name,
        in_specs=[HBM_SPEC] * (2 * n),
        out_specs=[SEM_SPEC] * (2 * n) + [HBM_SPEC] * (2 * n) + [pl.BlockSpec(memory_space=pltpu.VMEM)],
        out_shape=[pltpu.SemaphoreType.DMA(())] * (2 * n)
        + [pltpu.HBM(a.shape, a.dtype) for a in list(srcs) + list(zones)] + [_sds((8, LANES), F32)],
        input_output_aliases={i: 2 * n + i for i in range(2 * n)},
        compiler_params=pltpu.CompilerParams(has_side_effects=pltpu.SideEffectType.DATAFLOW_SIDE_EFFECTING),
    )(*[hbm(a) for a in list(srcs) + list(zones)])
    return out[:n], out[n:2 * n], out[2 * n:3 * n], out[3 * n:4 * n], out[-1]


def _exchange_wait(started, after, *, name):
    send_sems, recv_sems, srcs, zones, _ = started
    n = len(srcs)

    def body(*refs):
        lands = refs[n:2 * n]
        send_refs, recv_refs = refs[2 * n:3 * n], refs[3 * n:4 * n]
        me = (lax.axis_index("x"), lax.axis_index("y"), lax.axis_index("c"))
        for t in range(n):
            three = lands[t].at[pl.ds(0, N_CHIPS - 1)]
            cp = pltpu.make_async_remote_copy(src_ref=three, dst_ref=three, send_sem=send_refs[t],
                                              recv_sem=recv_refs[t], device_id=me, device_id_type=MESH)
            cp.wait_send()
            cp.wait_recv()

    out = pl.pallas_call(
        body, name=name,
        in_specs=[HBM_SPEC] * (2 * n) + [SEM_SPEC] * (2 * n) + [pl.BlockSpec(memory_space=pl.ANY)],
        out_specs=[HBM_SPEC] * (2 * n),
        out_shape=[pltpu.HBM(a.shape, a.dtype) for a in list(srcs) + list(zones)],
        input_output_aliases={i: i for i in range(2 * n)},
        compiler_params=pltpu.CompilerParams(has_side_effects=pltpu.SideEffectType.DATAFLOW_SIDE_EFFECTING),
    )(*srcs, *zones, *send_sems, *recv_sems, after)
    return out[:n], out[n:]


def _allgather_small(small):
    def body(small_ref, gath_ref, send_sems, recv_sems, loc_sem):
        x, y, c = lax.axis_index("x"), lax.axis_index("y"), lax.axis_index("c")
        me = 4 * x + 2 * y + c
        flips = [(fx, fy, fc) for fx in (0, 1) for fy in (0, 1) for fc in (0, 1)][1:]

        def flipped(fx, fy, fc):
            return (1 - x if fx else x, 1 - y if fy else y, 1 - c if fc else c)

        loc = pltpu.make_async_copy(small_ref, gath_ref.at[me], loc_sem)
        loc.start()
        sends = []
        for j, flip in enumerate(flips):
            cp = pltpu.make_async_remote_copy(
                src_ref=small_ref, dst_ref=gath_ref.at[me], send_sem=send_sems.at[j], recv_sem=recv_sems.at[j],
                device_id=flipped(*flip), device_id_type=MESH)
            cp.start()
            sends.append(cp)
        for j, flip in enumerate(flips):
            px, py, pc = flipped(*flip)
            pltpu.make_async_remote_copy(
                src_ref=small_ref, dst_ref=gath_ref.at[4 * px + 2 * py + pc], send_sem=send_sems.at[j],
                recv_sem=recv_sems.at[j], device_id=(px, py, pc), device_id_type=MESH).wait_recv()
        for cp in sends:
            cp.wait_send()
        loc.wait()

    any_spec = pl.BlockSpec(memory_space=pl.ANY)
    return pl.pallas_call(
        body, name="allgather_small", in_specs=[any_spec], out_specs=any_spec,
        out_shape=_sds((N_DEV,) + small.shape, small.dtype),
        scratch_shapes=[pltpu.SemaphoreType.DMA((N_DEV - 1,)), pltpu.SemaphoreType.DMA((N_DEV - 1,)),
                        pltpu.SemaphoreType.DMA])(small)


def _swap_with_sibling(parts):
    n = len(parts)

    def body(*refs):
        ins, outs = refs[:n], refs[n:2 * n]
        send_sems, recv_sems = refs[2 * n:]
        sib = (lax.axis_index("x"), lax.axis_index("y"), 1 - lax.axis_index("c"))
        cps = []
        for t in range(n):
            cp = pltpu.make_async_remote_copy(
                src_ref=ins[t], dst_ref=outs[t], send_sem=send_sems.at[t], recv_sem=recv_sems.at[t],
                device_id=sib, device_id_type=MESH)
            cp.start()
            cps.append(cp)
        for cp in cps:
            cp.wait()

    any_spec = pl.BlockSpec(memory_space=pl.ANY)
    return pl.pallas_call(
        body, name="swap_with_sibling", in_specs=[any_spec] * n, out_specs=[any_spec] * n,
        out_shape=[_sds(p.shape, p.dtype) for p in parts],
        scratch_shapes=[pltpu.SemaphoreType.DMA((n,)), pltpu.SemaphoreType.DMA((n,))])(*parts)


BIG = ("w_in", "w_out", "w_xq", "w_xk", "w_xv", "w_xo", "w_up", "w_down")
COL_SHARDED = ("w_in", "w_up")


def _as_matrix(name, w4):
    if name in COL_SHARDED:
        return w4
    return w4.reshape(1, w4.shape[0] * w4.shape[1], w4.shape[2])


def _transposed(w3):
    nsh, K, n = w3.shape
    return jnp.swapaxes(w3, 1, 2).reshape(1, nsh * n, K)


def _shard_layout(name, g):
    if name in COL_SHARDED:
        return g
    return g.reshape(N_CHIPS, g.shape[0] * g.shape[1] // N_CHIPS, g.shape[2])


def _local_step(x, mem, target, vecs, comm):
    S = x.shape[0]
    tables = _rope_tables(S)

    xn = _rms_fwd(x, vecs["norm_mix_g"], name="rms_mix")
    w_in, conv_w32 = comm["first"](xn)
    y = _mm_nn(xn, w_in, name="mm_in", tm=2048, tn=640)
    qk, v_perm = _rope_fwd(y, tables)
    v_src = [(y, 2)] + [(v, 0) for v in v_perm[1:]]
    outs, lses = zip(*[_att_fwd(qk[p], v_src[p], d, name=f"att_fwd_d{d}") for p, d in enumerate(DILATIONS)])
    att, lg = _att_combine(outs, lses)
    cv, u1 = _conv_fwd(y, conv_w32, vecs["conv_b"], vecs["conv_ln_g"], vecs["conv_ln_b"])
    mix = jnp.concatenate([att, cv], axis=1)
    Wm = {k: _as_matrix(k, v) for k, v in comm["rest"](mix).items()}
    Wm["w_in"] = w_in
    h1 = _mm_nn(mix, Wm["w_out"], name="mm_out", out_dtype=F32, res=x)
    hn = _rms_fwd(h1, vecs["norm_x_g"], name="rms_x")
    xq = _mm_nn(hn, Wm["w_xq"], name="mm_xq")
    mn = _rms_fwd(mem, vecs["norm_mem_g"], name="rms_mem")
    xk = _mm_nn(mn, Wm["w_xk"], name="mm_xk")
    xv = _mm_nn(mn, Wm["w_xv"], name="mm_xv")
    xo = _xatt_fwd(xq, xk, xv)
    h2 = _mm_nn(xo, Wm["w_xo"], name="mm_xo", out_dtype=F32, res=h1)
    hm = _rms_fwd(h2, vecs["norm_mlp_g"], name="rms_mlp")
    relu_up, act = _mm_nn(hm, Wm["w_up"], name="mm_up", relu2=True, tm=2048)
    h3 = _mm_nn(act, Wm["w_down"], name="mm_down", out_dtype=F32, res=h2, tm=512, tk=D_FF)

    dh3, dh3b, dg_final, loss = _loss_head(h3, vecs["norm_final_g"], target)
    g = {}
    g["w_down"] = _mm_tn(act, dh3b, 1, name="dw_down")
    dup = _mm_nt(dh3b, Wm["w_down"], name="d_act", out_dtype=BF16, mul=relu_up, tm=2048)
    g["w_up"] = _mm_tn(hm, dup, N_CHIPS, name="dw_up")
    sent = comm["send_mlp"]({k: _shard_layout(k, g[k]) for k in ("w_down", "w_up")})
    dhm = _mm_nn(dup, _transposed(Wm["w_up"]), name="d_hm", tm=512, tk=D_FF)
    dh2, dh2b, dg_mlp = _rms_bwd(dhm, h2, vecs["norm_mlp_g"] + sent[0:1, 0:1], dh3, name="rms_bwd_mlp")
    g["w_xo"] = _mm_tn(xo, dh2b, 1, name="dw_xo")
    dxo = _mm_nt(dh2b, Wm["w_xo"], name="d_xo", out_dtype=BF16)
    dxq, dxk, dxv = _xatt_bwd(xq, xk, xv, dxo)
    g["w_xq"] = _mm_tn(hn, dxq, 1, name="dw_xq")
    dhn = _mm_nt(dxq, Wm["w_xq"], name="d_hn", out_dtype=BF16)
    dh1, dh1b, dg_x = _rms_bwd(dhn, h1, vecs["norm_x_g"], dh2, name="rms_bwd_x")
    dxkb, dxvb = dxk.astype(BF16), dxv.astype(BF16)
    g["w_xk"] = _mm_tn(mn, dxkb, 1, name="dw_xk")
    g["w_xv"] = _mm_tn(mn, dxvb, 1, name="dw_xv")
    dmn = _mm_nt(jnp.concatenate([dxkb, dxvb], axis=1),
                 jnp.concatenate([Wm["w_xk"], Wm["w_xv"]], axis=2), name="d_mn", out_dtype=BF16)
    _, _, dg_mem = _rms_bwd(dmn, mem, vecs["norm_mem_g"], None, name="rms_bwd_mem")
    g["w_out"] = _mm_tn(mix, dh1b, 1, name="dw_out")
    sent = comm["send_att"]({k: _shard_layout(k, g[k]) for k in ("w_out", "w_xq", "w_xk", "w_xv", "w_xo")})
    dac = _mm_nt(dh1b, Wm["w_out"], name="d_mix", out_dtype=BF16)
    dag, dconv_w, dconv_small = _conv_bwd(dac, u1, y, conv_w32, vecs["conv_ln_g"] + sent[0:1, 0:1],
                                          vecs["conv_ln_b"])
    delta, do_perm = _att_delta(dac, att)
    do_src = [(dac, 0)] + [(t, 0) for t in do_perm[1:]]
    dq, dk, dv = zip(*[_att_bwd(qk[p], v_src[p], do_src[p], lg[p], delta[p], d, name=f"att_bwd_d{d}")
                       for p, d in enumerate(DILATIONS)])
    dy = _assemble_dy(dq, dk, dv, dag, tables)
    sent = comm["send_in"]({"w_in": _mm_tn(xn, dy, N_CHIPS, name="dw_in", tn=640)})
    dxn = _mm_nn(dy, _transposed(Wm["w_in"]), name="d_xn", tm=512, tk=D_IN)
    grad_x, _, dg_mix = _rms_bwd(dxn, x, vecs["norm_mix_g"] + sent[0:1, 0:1], dh1, name="rms_bwd_mix")

    small = dict(conv_w=dconv_w, conv_small=dconv_small, norm_mix_g=dg_mix, norm_x_g=dg_x, norm_mem_g=dg_mem,
                 norm_mlp_g=dg_mlp, norm_final_g=dg_final, loss=loss)
    return grad_x, small


SMALL_ORDER = ("conv_w", "conv_small", "norm_mix_g", "norm_x_g", "norm_mem_g", "norm_mlp_g", "norm_final_g", "loss")


def _pack_small(small):
    rows, offs, pos = [], {}, 0
    for k in SMALL_ORDER:
        a = small[k]
        a = a.reshape(a.shape[0] * a.shape[1] // SMALL_W, SMALL_W)
        pad = (-a.shape[0]) % 8
        if pad:
            a = jnp.pad(a, ((0, pad), (0, 0)))
        rows.append(a)
        offs[k] = pos
        pos += a.shape[0]
    return jnp.concatenate(rows, axis=0), offs


def kernel(x, mem, norm_mix_g, w_in, conv_w, conv_b, conv_ln_g, conv_ln_b, w_out, norm_x_g, norm_mem_g, w_xq, w_xk, w_xv, w_xo, norm_mlp_g, w_up, w_down, norm_final_g, loss_target, m_norm_mix_g, m_w_in, m_conv_w, m_conv_b, m_conv_ln_g, m_conv_ln_b, m_w_out, m_norm_x_g, m_norm_mem_g, m_w_xq, m_w_xk, m_w_xv, m_w_xo, m_norm_mlp_g, m_w_up, m_w_down, m_norm_final_g, v_norm_mix_g, v_w_in, v_conv_w, v_conv_b, v_conv_ln_g, v_conv_ln_b, v_w_out, v_norm_x_g, v_norm_mem_g, v_w_xq, v_w_xk, v_w_xv, v_w_xo, v_norm_mlp_g, v_w_up, v_w_down, v_norm_final_g):
    names = ("norm_mix_g", "w_in", "conv_w", "conv_b", "conv_ln_g", "conv_ln_b", "w_out", "norm_x_g", "norm_mem_g",
             "w_xq", "w_xk", "w_xv", "w_xo", "norm_mlp_g", "w_up", "w_down", "norm_final_g")
    wts = dict(zip(names, (norm_mix_g, w_in, conv_w, conv_b, conv_ln_g, conv_ln_b, w_out, norm_x_g, norm_mem_g,
                           w_xq, w_xk, w_xv, w_xo, norm_mlp_g, w_up, w_down, norm_final_g)))
    mom = dict(zip(names, (m_norm_mix_g, m_w_in, m_conv_w, m_conv_b, m_conv_ln_g, m_conv_ln_b, m_w_out, m_norm_x_g,
                           m_norm_mem_g, m_w_xq, m_w_xk, m_w_xv, m_w_xo, m_norm_mlp_g, m_w_up, m_w_down, m_norm_final_g)))
    var = dict(zip(names, (v_norm_mix_g, v_w_in, v_conv_w, v_conv_b, v_conv_ln_g, v_conv_ln_b, v_w_out, v_norm_x_g,
                           v_norm_mem_g, v_w_xq, v_w_xk, v_w_xv, v_w_xo, v_norm_mlp_g, v_w_up, v_w_down, v_norm_final_g)))
    chip = 2 * lax.axis_index("x") + lax.axis_index("y")

    def own_slot(shard, slot):
        return lax.dynamic_update_slice(lax.empty((N_CHIPS,) + shard.shape, shard.dtype), shard[None], (slot, 0, 0))

    conv_w_pad = jnp.pad(wts["conv_w"][0], ((0, 1), (0, 0)))
    first_shards = [wts["w_in"][0].astype(BF16), conv_w_pad]
    gathering_first = _exchange_start("gather", first_shards, [own_slot(s, chip) for s in first_shards],
                                      name="gather_first_start")
    rest = tuple(k for k in BIG if k != "w_in")
    rest_shards = [wts[k][0].astype(BF16) for k in rest]
    behind_first = gathering_first[4][0, 0].astype(jnp.int32)
    gathering = _exchange_start("gather", rest_shards, [own_slot(s, chip + behind_first) for s in rest_shards],
                                name="gather_rest_start")
    sending = {}

    def wait_first(after):
        _, (w_in_all, conv_w_all) = _exchange_wait(gathering_first, after, name="gather_first_wait")
        return w_in_all, jnp.transpose(conv_w_all, (1, 0, 2)).reshape(32, D_CONV)

    def wait_rest(after):
        _, zones = _exchange_wait(gathering, after, name="gather_rest_wait")
        return dict(zip(rest, zones))

    def send(group, grads):
        keys = tuple(grads)
        zones = [lax.empty((N_CHIPS - 1,) + grads[k].shape[1:], grads[k].dtype) for k in keys]
        sending[group] = (keys, _exchange_start("scatter", [grads[k] for k in keys], zones,
                                                name=f"scatter_{group}_start"))
        return sending[group][1][4]

    comm = dict(first=wait_first, rest=wait_rest, send_mlp=lambda grads: send("mlp", grads),
                send_att=lambda grads: send("att", grads), send_in=lambda grads: send("in", grads))
    vecs = {k: wts[k] for k in ("conv_b", "conv_ln_g", "conv_ln_b", "norm_x_g", "norm_mem_g", "norm_mlp_g")}
    vecs["norm_mix_g"] = wts["norm_mix_g"] + gathering[4][0:1, 0:1]
    vecs["norm_final_g"] = wts["norm_final_g"].reshape(1, D_MODEL)
    grad_x, small = _local_step(x[0], mem[0], loss_target[0], vecs, comm)

    packed, offs = _pack_small(small)
    gath = _allgather_small(packed)
    big, recv = {}, {}
    for group in ("mlp", "att", "in"):
        keys, started = sending[group]
        srcs, zones = _exchange_wait(started, gath, name=f"scatter_{group}_wait")
        big.update(zip(keys, srcs))
        recv.update(zip(keys, zones))
    me_arr = jnp.reshape(chip, (1,)).astype(jnp.int32)
    sums = [_sum_partials(big[k], recv[k], me_arr, name=f"sum_{k}") for k in BIG]
    sib = _swap_with_sibling(sums)
    tot_small = _sum_devices(gath)

    res = {}
    for k, s_mine, s_sib in zip(BIG, sums, sib):
        res[k] = _adamw([s_mine, s_sib], wts[k][0], mom[k][0], var[k][0], name=f"adamw_{k}")

    def piece(key, nrows):
        return tot_small[offs[key]:offs[key] + nrows]

    def small_update(k, gfull):
        return _adamw([gfull], wts[k].reshape(gfull.shape), mom[k].reshape(gfull.shape),
                      var[k].reshape(gfull.shape), name=f"adamw_{k}")

    dcw = piece("conv_w", 32)[:CONV_WIDTH]
    dcw_mine = lax.dynamic_slice_in_dim(dcw, chip * (D_CONV // N_CHIPS), D_CONV // N_CHIPS, axis=1)
    res["conv_w"] = small_update("conv_w", dcw_mine)
    cs = piece("conv_small", 8)
    res["conv_b"] = small_update("conv_b", cs[0:1])
    res["conv_ln_g"] = small_update("conv_ln_g", cs[1:2])
    res["conv_ln_b"] = small_update("conv_ln_b", cs[2:3])
    for k in ("norm_mix_g", "norm_x_g", "norm_mem_g", "norm_mlp_g", "norm_final_g"):
        res[k] = small_update(k, piece(k, 8 * D_MODEL // SMALL_W).reshape(8, D_MODEL)[0:1])
    loss = piece("loss", 8)[0, 0]

    outs = [loss, grad_x[None]]
    for j in range(4):
        outs += [res[k][j].reshape(wts[k].shape) for k in names]
    return tuple(outs)
```

```python
import jax
import jax.numpy as jnp
from jax import lax
from jax.experimental import pallas as pl
from jax.experimental.pallas import tpu as pltpu

F32 = jnp.float32
BF16 = jnp.bfloat16
MESH = pl.DeviceIdType.MESH

D_MODEL = 1024
ATT_HEADS = 8
HEAD_DIM = 64
D_ATT = ATT_HEADS * HEAD_DIM
D_CONV = D_MODEL - D_ATT
DILATIONS = (1, 4, 16)
HALF = 64
ROPE_THETA = 500000.0
ROT_DIM = HEAD_DIM // 4
CONV_WIDTH = 31
CONV_PAD = (CONV_WIDTH - 1) // 2
XATT_HEADS = 4
XATT_HEAD_DIM = D_MODEL // XATT_HEADS
D_FF = 4 * D_MODEL
D_IN = 3 * D_ATT + 2 * D_CONV
EPS = 1e-6
NEG_INF = -1e30
N_CHIPS = 4
N_DEV = 8

ADAM_LR = 0.001
ADAM_B1 = 0.9
ADAM_B2 = 0.999
ADAM_EPS = 1e-08
ADAM_WD = 0.01
ADAM_STEP = 10

VMEM_LIMIT_V7X = 56 * 1024 * 1024
LANES = 128
HALO = 16
CONV_ROWS = 64
ATT_BLOCK = 128
SMALL_W = 512


def _params(*sem):
    return pltpu.CompilerParams(dimension_semantics=sem, vmem_limit_bytes=VMEM_LIMIT_V7X)


def _sds(shape, dtype):
    return jax.ShapeDtypeStruct(shape, dtype)


def _squared(a):
    af = a.astype(F32)
    return (af * af).astype(BF16)


def _mm_nn(a, w3, *, name, out_dtype=BF16, res=None, relu=False, a_squared=False, tm=1024, tn=None, tk=1024):
    M, K = a.shape
    nsh, _, n = w3.shape
    tm, tk = min(tm, M), min(tk, K)
    tn = tn or min(n, 1024)
    npt, nk = n // tn, K // tk
    nj, N = nsh * npt, nsh * n
    n_out = 1

    def body(*refs):
        a_ref, w_ref = refs[0], refs[1]
        pos = 2
        res_ref = None
        if res is not None:
            res_ref = refs[pos]
            pos += 1
        outs = refs[pos:pos + n_out]
        acc_ref = refs[pos + n_out] if nk > 1 else None

        def finish(acc):
            if res_ref is not None:
                acc = acc + res_ref[...]
            if relu:
                acc = jnp.maximum(acc, 0.0)
            outs[0][...] = acc.astype(outs[0].dtype)

        a_val = _squared(a_ref[...]) if a_squared else a_ref[...]
        part = jnp.dot(a_val, w_ref[...], preferred_element_type=F32)
        if nk == 1:
            finish(part)
        else:
            k = pl.program_id(2)

            @pl.when(k == 0)
            def _():
                acc_ref[...] = part

            @pl.when(k > 0)
            def _():
                acc_ref[...] += part

            @pl.when(k == nk - 1)
            def _():
                finish(acc_ref[...])

    in_specs = [pl.BlockSpec((tm, tk), lambda i, j, k: (i, k)),
                pl.BlockSpec((None, tk, tn), lambda i, j, k: (j // npt, k, j % npt))]
    args = [a, w3]
    if res is not None:
        in_specs.append(pl.BlockSpec((tm, tn), lambda i, j, k: (i, j)))
        args.append(res)
    out_spec = pl.BlockSpec((tm, tn), lambda i, j, k: (i, j))
    out = pl.pallas_call(
        body, name=name, grid=(M // tm, nj, nk), in_specs=in_specs,
        out_specs=[out_spec] * n_out, out_shape=[_sds((M, N), out_dtype)] * n_out,
        scratch_shapes=[pltpu.VMEM((tm, tn), F32)] if nk > 1 else [],
        compiler_params=_params("parallel", "parallel", "arbitrary"))(*args)
    return out[0]


def _mm_nt(dy, w3, *, name, out_dtype=F32, mul=None, tm=1024, tn=None, tko=1024):
    M, N = dy.shape
    nsh, K, n = w3.shape
    tm, tko = min(tm, M), min(tko, K)
    tn = tn or min(n, 1024)
    npt = n // tn
    nj = nsh * npt

    def body(*refs):
        dy_ref, w_ref = refs[0], refs[1]
        pos = 2
        mul_ref = None
        if mul is not None:
            mul_ref = refs[pos]
            pos += 1
        out_ref = refs[pos]
        acc_ref = refs[pos + 1] if nj > 1 else None

        def finish(acc):
            if mul_ref is not None:
                acc = acc * (2.0 * mul_ref[...].astype(F32))
            out_ref[...] = acc.astype(out_ref.dtype)

        part = lax.dot_general(dy_ref[...], w_ref[...], (((1,), (1,)), ((), ())), preferred_element_type=F32)
        if nj == 1:
            finish(part)
        else:
            j = pl.program_id(2)

            @pl.when(j == 0)
            def _():
                acc_ref[...] = part

            @pl.when(j > 0)
            def _():
                acc_ref[...] += part

            @pl.when(j == nj - 1)
            def _():
                finish(acc_ref[...])

    in_specs = [pl.BlockSpec((tm, tn), lambda i, ko, j: (i, j)),
                pl.BlockSpec((None, tko, tn), lambda i, ko, j: (j // npt, ko, j % npt))]
    args = [dy, w3]
    if mul is not None:
        in_specs.append(pl.BlockSpec((tm, tko), lambda i, ko, j: (i, ko)))
        args.append(mul)
    return pl.pallas_call(
        body, name=name, grid=(M // tm, K // tko, nj), in_specs=in_specs,
        out_specs=pl.BlockSpec((tm, tko), lambda i, ko, j: (i, ko)), out_shape=_sds((M, K), out_dtype),
        scratch_shapes=[pltpu.VMEM((tm, tko), F32)] if nj > 1 else [],
        compiler_params=_params("parallel", "parallel", "arbitrary"))(*args)


def _mm_tn(a, dy, nsh, *, name, out_dtype=BF16, a_squared=False, tm=2048, tk=1024, tn=None):
    M, K = a.shape
    N = dy.shape[1]
    n = N // nsh
    tm, tk = min(tm, M), min(tk, K)
    tn = tn or min(n, 1024)
    npt = n // tn
    nj, nm = nsh * npt, M // tm

    def body(a_ref, dy_ref, out_ref, acc_ref):
        m = pl.program_id(2)
        a_val = _squared(a_ref[...]) if a_squared else a_ref[...]
        part = lax.dot_general(a_val, dy_ref[...], (((0,), (0,)), ((), ())), preferred_element_type=F32)

        @pl.when(m == 0)
        def _():
            acc_ref[...] = part

        @pl.when(m > 0)
        def _():
            acc_ref[...] += part

        @pl.when(m == nm - 1)
        def _():
            out_ref[...] = acc_ref[...].astype(out_ref.dtype)

    return pl.pallas_call(
        body, name=name, grid=(K // tk, nj, nm),
        in_specs=[pl.BlockSpec((tm, tk), lambda kk, j, m: (m, kk)),
                  pl.BlockSpec((tm, tn), lambda kk, j, m: (m, j))],
        out_specs=pl.BlockSpec((None, tk, tn), lambda kk, j, m: (j // npt, kk, j % npt)),
        out_shape=_sds((nsh, K, n), out_dtype),
        scratch_shapes=[pltpu.VMEM((tk, tn), F32)],
        compiler_params=_params("parallel", "parallel", "arbitrary"))(a, dy)


def _rms_fwd(x, g, *, name, tm=512):
    M, Dm = x.shape
    tm = min(tm, M)

    def body(x_ref, g_ref, o_ref):
        xf = x_ref[...]
        r = lax.rsqrt(jnp.mean(xf * xf, axis=-1, keepdims=True) + EPS)
        o_ref[...] = (xf * r * g_ref[...]).astype(o_ref.dtype)

    return pl.pallas_call(
        body, name=name, grid=(M // tm,),
        in_specs=[pl.BlockSpec((tm, Dm), lambda i: (i, 0)), pl.BlockSpec((1, Dm), lambda i: (0, 0))],
        out_specs=pl.BlockSpec((tm, Dm), lambda i: (i, 0)), out_shape=_sds((M, Dm), BF16),
        compiler_params=_params("parallel"))(x, g)


def _rms_bwd(dxn, x, g, dres, *, name, bf16_copy=True, tm=512):
    M, Dm = x.shape
    tm = min(tm, M)
    has_res = dres is not None

    def body(*refs):
        dxn_ref, x_ref, g_ref = refs[:3]
        dres_ref = refs[3] if has_res else None
        dx_ref, dg_ref = refs[-1 - 1 - bf16_copy], refs[-1]
        dxb_ref = refs[-2] if bf16_copy else None
        i = pl.program_id(0)
        xf = x_ref[...]
        r = lax.rsqrt(jnp.mean(xf * xf, axis=-1, keepdims=True) + EPS)
        nrm = xf * r
        dxn_f = dxn_ref[...].astype(F32)
        dn = dxn_f * g_ref[...]
        dx = r * (dn - nrm * jnp.mean(dn * nrm, axis=-1, keepdims=True))
        if has_res:
            dx = dx + dres_ref[...]
        dx_ref[...] = dx
        if bf16_copy:
            dxb_ref[...] = dx.astype(dxb_ref.dtype)

        @pl.when(i == 0)
        def _():
            dg_ref[...] = jnp.zeros_like(dg_ref)

        dg_ref[0:1, :] += jnp.sum(dxn_f * nrm, axis=0, keepdims=True)

    row = pl.BlockSpec((tm, Dm), lambda i: (i, 0))
    in_specs = [row, row, pl.BlockSpec((1, Dm), lambda i: (0, 0))] + ([row] if has_res else [])
    args = [dxn, x, g] + ([dres] if has_res else [])
    out = pl.pallas_call(
        body, name=name, grid=(M // tm,), in_specs=in_specs,
        out_specs=[row] * (1 + bf16_copy) + [pl.BlockSpec((8, Dm), lambda i: (0, 0))],
        out_shape=[_sds((M, Dm), F32)] + [_sds((M, Dm), BF16)] * bf16_copy + [_sds((8, Dm), F32)],
        compiler_params=_params("arbitrary"))(*args)
    return out[0], (out[1] if bf16_copy else None), out[-1]


def _loss_head(h, g, target, *, tm=512):
    M, Dm = h.shape
    tm = min(tm, M)

    def body(h_ref, g_ref, t_ref, dh_ref, dhb_ref, dg_ref, loss_ref):
        i = pl.program_id(0)
        hf = h_ref[...]
        r = lax.rsqrt(jnp.mean(hf * hf, axis=-1, keepdims=True) + EPS)
        nrm = hf * r
        gv = g_ref[...]
        err = nrm * gv - t_ref[...]
        dy = err * (1.0 / Dm)
        dn = dy * gv
        dh = r * (dn - nrm * jnp.mean(dn * nrm, axis=-1, keepdims=True))
        dh_ref[...] = dh
        dhb_ref[...] = dh.astype(dhb_ref.dtype)

        @pl.when(i == 0)
        def _():
            dg_ref[...] = jnp.zeros_like(dg_ref)
            loss_ref[...] = jnp.zeros_like(loss_ref)

        dg_ref[0:1, :] += jnp.sum(dy * nrm, axis=0, keepdims=True)
        part = 0.5 * jnp.sum(jnp.mean(err * err, axis=-1, keepdims=True), axis=0, keepdims=True)
        sel = (lax.broadcasted_iota(jnp.int32, (8, 128), 0) == 0) & (lax.broadcasted_iota(jnp.int32, (8, 128), 1) == 0)
        loss_ref[...] += jnp.where(sel, part, 0.0)

    row = pl.BlockSpec((tm, Dm), lambda i: (i, 0))
    return pl.pallas_call(
        body, name="loss_head", grid=(M // tm,),
        in_specs=[row, pl.BlockSpec((1, Dm), lambda i: (0, 0)), row],
        out_specs=[row, row, pl.BlockSpec((8, Dm), lambda i: (0, 0)), pl.BlockSpec((8, 128), lambda i: (0, 0))],
        out_shape=[_sds((M, Dm), F32), _sds((M, Dm), BF16), _sds((8, Dm), F32), _sds((8, 128), F32)],
        compiler_params=_params("arbitrary"))(h, g, target)


def _class_spec(tm, d, width):
    return pl.BlockSpec((d, tm // d, width), lambda i: (0, i, 0))


def _row_scratch(tm, width):
    return pltpu.VMEM((width // LANES, tm, LANES), F32)


def _fill(scr, val):
    for c in range(scr.shape[0]):
        scr[c] = val[:, c * LANES:(c + 1) * LANES]


def _to_classes(scr, out_ref, d):
    n = scr.shape[1] // d
    for r in range(d):
        for c in range(scr.shape[0]):
            out_ref[r, :, c * LANES:(c + 1) * LANES] = scr[c, pl.ds(r, n, stride=d), :].astype(out_ref.dtype)


def _from_classes(in_ref, scr, d):
    n = scr.shape[1] // d
    for r in range(d):
        blk = in_ref[r].astype(F32)
        for c in range(scr.shape[0]):
            scr[c, pl.ds(r, n, stride=d), :] = blk[:, c * LANES:(c + 1) * LANES]
    return jnp.concatenate([scr[c] for c in range(scr.shape[0])], axis=1)


def _rope_tables(S):
    half = ROT_DIM // 2
    freqs = ROPE_THETA ** (-jnp.arange(0, ROT_DIM, 2, dtype=F32) / ROT_DIM)
    ang = jnp.arange(S, dtype=F32)[:, None] * freqs[None, :]
    cos, sin = jnp.cos(ang), jnp.sin(ang)
    ones = jnp.ones((S, HEAD_DIM - ROT_DIM), F32)
    zeros = jnp.zeros((S, HEAD_DIM - ROT_DIM), F32)
    zh = jnp.zeros((S, half), F32)
    c = jnp.concatenate([cos, cos, ones], axis=1)
    sa = jnp.concatenate([-sin, zh, zeros], axis=1)
    sb = jnp.concatenate([zh, sin, zeros], axis=1)
    return tuple(jnp.tile(t, (1, LANES // HEAD_DIM)) for t in (c, sa, sb))


def _rope_fwd(y, tables, *, tm=512):
    S = y.shape[0]
    W = 2 * D_ATT
    tm = min(tm, S)
    half = ROT_DIM // 2
    dils = [d for d in DILATIONS if d > 1]

    def body(y_ref, c_ref, sa_ref, sb_ref, qk_ref, *rest):
        qk_outs, v_outs = rest[:len(dils)], rest[len(dils):2 * len(dils)]
        scr_qk, scr_v = rest[2 * len(dils):]
        t = y_ref[:, 0:W].astype(F32)
        rep = W // LANES
        c, sa, sb = (jnp.tile(r[...], (1, rep)) for r in (c_ref, sa_ref, sb_ref))
        rot = t * c + pltpu.roll(t, W - half, axis=1) * sa + pltpu.roll(t, half, axis=1) * sb
        qk_ref[...] = rot.astype(qk_ref.dtype)
        _fill(scr_qk, rot)
        _fill(scr_v, y_ref[:, W:W + D_ATT].astype(F32))
        for d, qo, vo in zip(dils, qk_outs, v_outs):
            _to_classes(scr_qk, qo, d)
            _to_classes(scr_v, vo, d)

    tab = pl.BlockSpec((tm, LANES), lambda i: (i, 0))
    out = pl.pallas_call(
        body, name="rope_fwd", grid=(S // tm,),
        in_specs=[pl.BlockSpec((tm, 3 * D_ATT), lambda i: (i, 0)), tab, tab, tab],
        out_specs=[pl.BlockSpec((tm, W), lambda i: (i, 0))] + [_class_spec(tm, d, W) for d in dils]
        + [_class_spec(tm, d, D_ATT) for d in dils],
        out_shape=[_sds((S, W), BF16)] + [_sds((d, S // d, W), BF16) for d in dils]
        + [_sds((d, S // d, D_ATT), BF16) for d in dils],
        scratch_shapes=[_row_scratch(tm, W), _row_scratch(tm, D_ATT)],
        compiler_params=_params("parallel"))(y, *tables)
    qk = [out[0]] + [o.reshape(S, W) for o in out[1:1 + len(dils)]]
    v = [None] + [o.reshape(S, D_ATT) for o in out[1 + len(dils):]]
    return qk, v


def _assemble_dy(dq, dk, dv, dag, tables, *, tm=512):
    S = dag.shape[0]
    tm = min(tm, S)
    half = ROT_DIM // 2
    W = D_ATT
    n_pat = len(DILATIONS)

    def body(*refs):
        groups = [refs[g * n_pat:(g + 1) * n_pat] for g in range(3)]
        dag_ref, c_ref, sa_ref, sb_ref, o_ref, scr = refs[3 * n_pat:]
        rep = W // LANES
        c, sa, sb = (jnp.tile(r[...], (1, rep)) for r in (c_ref, sa_ref, sb_ref))

        def total(rs):
            acc = rs[0][...].astype(F32)
            for d, r in zip(DILATIONS[1:], rs[1:]):
                acc = acc + _from_classes(r, scr, d)
            return acc

        def unrope(dr):
            return dr * c + pltpu.roll(dr * sa, half, axis=1) + pltpu.roll(dr * sb, W - half, axis=1)

        o_ref[:, 0:W] = unrope(total(groups[0])).astype(o_ref.dtype)
        o_ref[:, W:2 * W] = unrope(total(groups[1])).astype(o_ref.dtype)
        o_ref[:, 2 * W:3 * W] = total(groups[2]).astype(o_ref.dtype)
        o_ref[:, 3 * W:] = dag_ref[...]

    specs = [pl.BlockSpec((tm, W), lambda i: (i, 0))] + [_class_spec(tm, d, W) for d in DILATIONS[1:]]
    tab = pl.BlockSpec((tm, LANES), lambda i: (i, 0))
    args = [a if d == 1 else a.reshape(d, S // d, W) for grp in (dq, dk, dv) for d, a in zip(DILATIONS, grp)]
    return pl.pallas_call(
        body, name="assemble_dy", grid=(S // tm,),
        in_specs=specs * 3 + [pl.BlockSpec((tm, 2 * D_CONV), lambda i: (i, 0)), tab, tab, tab],
        out_specs=pl.BlockSpec((tm, D_IN), lambda i: (i, 0)), out_shape=_sds((S, D_IN), BF16),
        scratch_shapes=[_row_scratch(tm, W)],
        compiler_params=_params("parallel"))(*args, dag, *tables)


def _seq_specs(L, tb, col):
    nb, per, nh = L // tb, tb // HALF, L // HALF
    centre = pl.BlockSpec((tb, D_ATT), lambda r, i: (r * nb + i, col))
    prev = pl.BlockSpec((HALF, D_ATT), lambda r, i: (r * nh + jnp.maximum(i * per - 1, 0), col))
    nxt = pl.BlockSpec((HALF, D_ATT), lambda r, i: (r * nh + jnp.minimum((i + 1) * per, nh - 1), col))
    return prev, centre, nxt


def _band_mask(i, tq, L):
    shape = (tq, tq + 2 * HALF)
    c_idx = lax.broadcasted_iota(jnp.int32, shape, 0)
    w_idx = lax.broadcasted_iota(jnp.int32, shape, 1)
    diff = w_idx - c_idx
    wpos = i * tq - HALF + w_idx
    return (diff >= 0) & (diff <= 2 * HALF) & (wpos >= 0) & (wpos < L)


def _lane_groups():
    for c0 in range(0, D_ATT, LANES):
        yield slice(c0, c0 + LANES)


def _first_head(rows):
    return lax.broadcasted_iota(jnp.int32, (rows, LANES), 1) < HEAD_DIM


def _split_pair(x, first):
    zero = jnp.zeros_like(x)
    return jnp.where(first, x, zero), jnp.where(first, zero, x)


def _nt(a, b):
    return lax.dot_general(a, b, (((1,), (1,)), ((), ())), preferred_element_type=F32)


def _tn(a, b):
    return lax.dot_general(a, b, (((0,), (0,)), ((), ())), preferred_element_type=F32)


ATT_SCALE = HEAD_DIM ** -0.5


def _att_fwd(qk, v_src, d, *, name):
    S = qk.shape[0]
    L = S // d
    tq = min(ATT_BLOCK, L)
    v_arr, v_col = v_src

    def body(q_ref, kp_ref, kc_ref, kn_ref, vp_ref, vc_ref, vn_ref, o_ref, lse_ref):
        i = pl.program_id(1)
        valid = _band_mask(i, tq, L)
        q = q_ref[...] * ATT_SCALE
        kwin = jnp.concatenate([kp_ref[...], kc_ref[...], kn_ref[...]], axis=0)
        vwin = jnp.concatenate([vp_ref[...], vc_ref[...], vn_ref[...]], axis=0)
        first = _first_head(tq)
        groups = list(_lane_groups())
        heads = [(ls, t) for ls in groups for t in _split_pair(q[:, ls], first)]
        s = [jnp.where(valid, _nt(t, kwin[:, ls]), NEG_INF) for ls, t in heads]
        m = [jnp.max(t, axis=-1, keepdims=True) for t in s]
        p = [jnp.exp(t - mm) for t, mm in zip(s, m)]
        den = [jnp.sum(t, axis=-1, keepdims=True) for t in p]
        o = [jnp.dot(t.astype(BF16), vwin[:, ls], preferred_element_type=F32) * (1.0 / dd)
             for t, dd, (ls, _) in zip(p, den, heads)]
        lse = [mm + jnp.log(dd) for mm, dd in zip(m, den)]
        for g, ls in enumerate(groups):
            o_ref[:, ls] = jnp.where(first, o[2 * g], o[2 * g + 1]).astype(o_ref.dtype)
            lse_ref[:, ls] = jnp.where(first, lse[2 * g], lse[2 * g + 1])

    _, qc, _ = _seq_specs(L, tq, 0)
    kp, kc, kn = _seq_specs(L, tq, 1)
    vp, vc, vn = _seq_specs(L, tq, v_col)
    out = pl.BlockSpec((tq, D_ATT), lambda r, i: (r * (L // tq) + i, 0))
    return pl.pallas_call(
        body, name=name, grid=(d, L // tq),
        in_specs=[qc, kp, kc, kn, vp, vc, vn], out_specs=[out, out],
        out_shape=[_sds((S, D_ATT), BF16), _sds((S, D_ATT), F32)],
        compiler_params=_params("parallel", "parallel"))(qk, qk, qk, qk, v_arr, v_arr, v_arr)


def _att_combine(outs, lses, *, tm=512):
    S = outs[0].shape[0]
    tm = min(tm, S)
    dils = DILATIONS[1:]
    n_d = len(dils)

    def body(*refs):
        o_refs, l_refs = refs[0:1 + n_d], refs[1 + n_d:2 + 2 * n_d]
        att_ref, lg_ref = refs[2 + 2 * n_d:4 + 2 * n_d]
        lg_outs = refs[4 + 2 * n_d:4 + 3 * n_d]
        scr = refs[4 + 3 * n_d:]
        scr_o, scr_l, scr_lg = scr[:n_d], scr[n_d:2 * n_d], scr[2 * n_d]
        ls = [l_refs[0][...]] + [_from_classes(r, s, d) for r, s, d in zip(l_refs[1:], scr_l, dils)]
        os_ = [o_refs[0][...].astype(F32)] + [_from_classes(r, s, d) for r, s, d in zip(o_refs[1:], scr_o, dils)]
        mx = ls[0]
        for l in ls[1:]:
            mx = jnp.maximum(mx, l)
        es = [jnp.exp(l - mx) for l in ls]
        tot = es[0]
        num = es[0] * os_[0]
        for e, o in zip(es[1:], os_[1:]):
            tot = tot + e
            num = num + e * o
        att_ref[...] = (num / tot).astype(att_ref.dtype)
        lg = mx + jnp.log(tot)
        lg_ref[...] = lg
        _fill(scr_lg, lg)
        for d, out in zip(dils, lg_outs):
            _to_classes(scr_lg, out, d)

    nat = pl.BlockSpec((tm, D_ATT), lambda i: (i, 0))
    specs = [nat] + [_class_spec(tm, d, D_ATT) for d in dils]
    view = lambda arrs: [arrs[0]] + [a.reshape(d, S // d, D_ATT) for a, d in zip(arrs[1:], dils)]
    out = pl.pallas_call(
        body, name="att_combine", grid=(S // tm,), in_specs=specs * 2,
        out_specs=[nat, nat] + specs[1:],
        out_shape=[_sds((S, D_ATT), BF16), _sds((S, D_ATT), F32)] + [_sds((d, S // d, D_ATT), F32) for d in dils],
        scratch_shapes=[_row_scratch(tm, D_ATT)] * (2 * n_d + 1),
        compiler_params=_params("parallel"))(*view(list(outs)), *view(list(lses)))
    return out[0], [out[1]] + [o.reshape(S, D_ATT) for o in out[2:]]


def _att_delta(dac, att, *, tm=512):
    S = att.shape[0]
    tm = min(tm, S)
    dils = DILATIONS[1:]
    n_d = len(dils)

    def body(do_ref, o_ref, dl_ref, *rest):
        dl_outs, do_outs = rest[:n_d], rest[n_d:2 * n_d]
        scr_dl, scr_do = rest[2 * n_d:]
        do = do_ref[...].astype(F32)
        prod = do * o_ref[...].astype(F32)
        per_head = [jnp.broadcast_to(jnp.sum(prod[:, h * HEAD_DIM:(h + 1) * HEAD_DIM], axis=-1, keepdims=True),
                                     (tm, HEAD_DIM)) for h in range(ATT_HEADS)]
        dl = jnp.concatenate(per_head, axis=1)
        dl_ref[...] = dl
        _fill(scr_dl, dl)
        _fill(scr_do, do)
        for d, dlo, doo in zip(dils, dl_outs, do_outs):
            _to_classes(scr_dl, dlo, d)
            _to_classes(scr_do, doo, d)

    blk = pl.BlockSpec((tm, D_ATT), lambda i: (i, 0))
    out = pl.pallas_call(
        body, name="att_delta", grid=(S // tm,), in_specs=[blk, blk],
        out_specs=[blk] + [_class_spec(tm, d, D_ATT) for d in dils] * 2,
        out_shape=[_sds((S, D_ATT), F32)] + [_sds((d, S // d, D_ATT), F32) for d in dils]
        + [_sds((d, S // d, D_ATT), BF16) for d in dils],
        scratch_shapes=[_row_scratch(tm, D_ATT), _row_scratch(tm, D_ATT)],
        compiler_params=_params("parallel"))(dac, att)
    delta = [out[0]] + [o.reshape(S, D_ATT) for o in out[1:1 + n_d]]
    do = [None] + [o.reshape(S, D_ATT) for o in out[1 + n_d:]]
    return delta, do


def _att_bwd(qk, v_src, do_src, lg, delta, d, *, name):
    S = qk.shape[0]
    L = S // d
    tq = min(ATT_BLOCK, L)
    nb, per, nh = L // tq, tq // HALF, L // HALF
    n_blocks = d * nb
    win = tq + 2 * HALF
    lead = tq - HALF
    acc_rows = lead + win
    (v_arr, v_col), (do_arr, do_col) = v_src, do_src

    def body(q_ref, kp_ref, kc_ref, kn_ref, vp_ref, vc_ref, vn_ref, do_ref, lg_ref, dl_ref,
             dq_ref, dk_ref, dv_ref, acc_k, acc_v):
        b = pl.program_id(0)
        i = lax.rem(jnp.minimum(b, n_blocks - 1), nb)

        @pl.when(b == 0)
        def _():
            acc_k[...] = jnp.zeros_like(acc_k)
            acc_v[...] = jnp.zeros_like(acc_v)

        @pl.when(b < n_blocks)
        def _():
            valid = _band_mask(i, tq, L)
            q, do = q_ref[...] * ATT_SCALE, do_ref[...]
            kwin = jnp.concatenate([kp_ref[...], kc_ref[...], kn_ref[...]], axis=0)
            vwin = jnp.concatenate([vp_ref[...], vc_ref[...], vn_ref[...]], axis=0)
            first, first_w = _first_head(tq), _first_head(win)
            groups = list(_lane_groups())
            cols = [c for ls in groups for c in (ls.start, ls.start + HEAD_DIM)]
            lanes = [ls for ls in groups for _ in range(2)]
            qh = [t for ls in groups for t in _split_pair(q[:, ls], first)]
            doh = [t for ls in groups for t in _split_pair(do[:, ls], first)]
            s = [jnp.where(valid, _nt(t, kwin[:, ls]), NEG_INF) for t, ls in zip(qh, lanes)]
            dp = [_nt(t, vwin[:, ls]) for t, ls in zip(doh, lanes)]
            p = [jnp.exp(t - lg_ref[:, c:c + 1]) for t, c in zip(s, cols)]
            ds = [(pp * (t - dl_ref[:, c:c + 1])).astype(BF16) for pp, t, c in zip(p, dp, cols)]
            dq = [jnp.dot(t, kwin[:, ls], preferred_element_type=F32) for t, ls in zip(ds, lanes)]
            dk = [_tn(t, q[:, ls]) for t, ls in zip(ds, lanes)]
            dv = [_tn(pp.astype(BF16), do[:, ls]) for pp, ls in zip(p, lanes)]
            for g, ls in enumerate(groups):
                dq_ref[:, ls] = (jnp.where(first, dq[2 * g], dq[2 * g + 1]) * ATT_SCALE).astype(dq_ref.dtype)
                acc_k[lead:, ls] += jnp.where(first_w, dk[2 * g], dk[2 * g + 1])
                acc_v[lead:, ls] += jnp.where(first_w, dv[2 * g], dv[2 * g + 1])

        for acc, out in ((acc_k, dk_ref), (acc_v, dv_ref)):
            out[...] = acc[0:tq, :].astype(out.dtype)
            kept = acc[tq:, :]
            acc[0:acc_rows - tq, :] = kept
            acc[acc_rows - tq:, :] = jnp.zeros((tq, D_ATT), F32)

    def seq(col):
        blk = lambda b: jnp.minimum(b, n_blocks - 1)
        cls = lambda b: (blk(b) // nb) * nh
        centre = pl.BlockSpec((tq, D_ATT), lambda b: (blk(b), col))
        prev = pl.BlockSpec((HALF, D_ATT), lambda b: (cls(b) + jnp.maximum((blk(b) % nb) * per - 1, 0), col))
        nxt = pl.BlockSpec((HALF, D_ATT), lambda b: (cls(b) + jnp.minimum((blk(b) % nb + 1) * per, nh - 1), col))
        return prev, centre, nxt

    _, qc, _ = seq(0)
    kp, kc, kn = seq(1)
    vp, vc, vn = seq(v_col)
    _, doc, _ = seq(do_col)
    late = pl.BlockSpec((tq, D_ATT), lambda b: (jnp.maximum(b - 1, 0), 0))
    return pl.pallas_call(
        body, name=name, grid=(n_blocks + 1,),
        in_specs=[qc, kp, kc, kn, vp, vc, vn, doc, qc, qc], out_specs=[qc, late, late],
        out_shape=[_sds((S, D_ATT), BF16)] * 3,
        scratch_shapes=[pltpu.VMEM((acc_rows, D_ATT), F32), pltpu.VMEM((acc_rows, D_ATT), F32)],
        compiler_params=_params("arbitrary"))(qk, qk, qk, qk, v_arr, v_arr, v_arr, do_arr, lg, delta)


def _sigmoid(x):
    return 1.0 / (1.0 + jnp.exp(-x))


def _halo_specs(S, T, width, col):
    last = S // HALO - 1
    per = T // HALO
    centre = pl.BlockSpec((T, width), lambda i: (i, col))
    prev = pl.BlockSpec((HALO, width), lambda i: (jnp.maximum(i * per - 1, 0), col))
    nxt = pl.BlockSpec((HALO, width), lambda i: (jnp.minimum((i + 1) * per, last), col))
    return prev, centre, nxt


def _window_scratch(T, C):
    return pltpu.VMEM((8, T + 2 * HALO, C), F32)


def _fill_window(buf, prev, centre, nxt, T):
    buf[0, 0:HALO, :] = prev
    buf[0, HALO:HALO + T, :] = centre
    buf[0, HALO + T:, :] = nxt
    rows = T + 2 * HALO - 8
    for s in range(1, 8):
        buf[s, 0:rows, :] = buf[0, s:s + rows, :]


def _tap_reads(buf, first_off, step, r0, ls):
    by_slab = {}
    for k in range(CONV_WIDTH):
        off = first_off + step * k
        by_slab.setdefault(off % 8, []).append((k, off - off % 8))
    for s, taps in by_slab.items():
        lo = min(a for _, a in taps)
        hi = max(a for _, a in taps)
        rows = buf[s, pl.ds(lo + r0, CONV_ROWS + hi - lo), ls]
        for k, a in taps:
            yield k, rows[a - lo:a - lo + CONV_ROWS]


def _depthwise(buf, w_ref, out_ref, T, C, first_off, step):
    def row_tile(t, carry):
        r0 = pl.multiple_of(t * CONV_ROWS, CONV_ROWS)
        for c0 in range(0, C, LANES):
            ls = slice(c0, c0 + LANES)
            acc = jnp.zeros((CONV_ROWS, LANES), F32)
            for k, rows in _tap_reads(buf, first_off, step, r0, ls):
                acc = acc + rows * w_ref[k:k + 1, ls]
            out_ref[pl.ds(r0, CONV_ROWS), ls] = acc
        return carry

    lax.fori_loop(0, T // CONV_ROWS, row_tile, 0)


def _conv_fwd(y, conv_w32, conv_b, ln_g, ln_b, *, T=512):
    S = y.shape[0]
    T = min(T, S)
    nblk = S // T
    C = D_CONV

    def body(ap, ac, an, gp, gc, gn, w_ref, b_ref, lg_ref, lb_ref, cv_ref, u1_ref, buf):
        i = pl.program_id(0)

        def glu(a_ref, g_ref):
            return a_ref[...].astype(F32) * _sigmoid(g_ref[...].astype(F32))

        _fill_window(buf, jnp.where(i > 0, glu(ap, gp), 0.0), glu(ac, gc),
                     jnp.where(i < nblk - 1, glu(an, gn), 0.0), T)
        _depthwise(buf, w_ref, u1_ref, T, C, HALO - CONV_PAD, 1)
        u1 = u1_ref[...] + b_ref[...]
        u1_ref[...] = u1
        mu = jnp.mean(u1, axis=-1, keepdims=True)
        xc = u1 - mu
        rstd = lax.rsqrt(jnp.mean(xc * xc, axis=-1, keepdims=True) + EPS)
        u2 = xc * rstd * lg_ref[...] + lb_ref[...]
        cv_ref[...] = (u2 * _sigmoid(u2)).astype(cv_ref.dtype)

    ap, ac, an = _halo_specs(S, T, C, 3)
    gp, gc, gn = _halo_specs(S, T, C, 4)
    vec = pl.BlockSpec((1, C), lambda i: (0, 0))
    out = pl.BlockSpec((T, C), lambda i: (i, 0))
    return pl.pallas_call(
        body, name="conv_fwd", grid=(nblk,),
        in_specs=[ap, ac, an, gp, gc, gn, pl.BlockSpec((32, C), lambda i: (0, 0)), vec, vec, vec],
        out_specs=[out, out], out_shape=[_sds((S, C), BF16), _sds((S, C), F32)],
        scratch_shapes=[_window_scratch(T, C)],
        compiler_params=_params("parallel"))(y, y, y, y, y, y, conv_w32, conv_b, ln_g, ln_b)


def _conv_bwd(dac, u1, y, conv_w32, ln_g, ln_b, *, T=512):
    S = y.shape[0]
    T = min(T, S)
    nblk = S // T
    C = D_CONV

    def body(dp, dc, dn, up, uc, un, ap, ac, an, gp, gc, gn, w_ref, lg_ref, lb_ref,
             dag_ref, dw_ref, dsm_ref, bufd, bufu, du0_scr, dw_acc):
        i = pl.program_id(0)
        lg = lg_ref[...]

        def du1_of(dcv_ref, u1_ref):
            u1 = u1_ref[...]
            mu = jnp.mean(u1, axis=-1, keepdims=True)
            xc = u1 - mu
            rstd = lax.rsqrt(jnp.mean(xc * xc, axis=-1, keepdims=True) + EPS)
            xhat = xc * rstd
            u2 = xhat * lg + lb_ref[...]
            sg = _sigmoid(u2)
            du2 = dcv_ref[...].astype(F32) * (sg * (1.0 + u2 * (1.0 - sg)))
            dxh = du2 * lg
            du1 = rstd * (dxh - jnp.mean(dxh, axis=-1, keepdims=True)
                          - xhat * jnp.mean(dxh * xhat, axis=-1, keepdims=True))
            return du1, du2, xhat

        def glu(a_ref, g_ref):
            return a_ref[...].astype(F32) * _sigmoid(g_ref[...].astype(F32))

        @pl.when(i == 0)
        def _():
            dw_ref[...] = jnp.zeros_like(dw_ref)
            dsm_ref[...] = jnp.zeros_like(dsm_ref)

        du1_c, du2_c, xhat_c = du1_of(dc, uc)
        dsm_ref[0:1, :] += jnp.sum(du1_c, axis=0, keepdims=True)
        dsm_ref[1:2, :] += jnp.sum(du2_c * xhat_c, axis=0, keepdims=True)
        dsm_ref[2:3, :] += jnp.sum(du2_c, axis=0, keepdims=True)
        _fill_window(bufd, jnp.where(i > 0, du1_of(dp, up)[0], 0.0), du1_c,
                     jnp.where(i < nblk - 1, du1_of(dn, un)[0], 0.0), T)
        _fill_window(bufu, jnp.where(i > 0, glu(ap, gp), 0.0), glu(ac, gc),
                     jnp.where(i < nblk - 1, glu(an, gn), 0.0), T)

        _depthwise(bufd, w_ref, du0_scr, T, C, HALO + CONV_PAD, -1)
        dw_acc[...] = jnp.zeros_like(dw_acc)

        def dw_tile(t, carry):
            r0 = pl.multiple_of(t * CONV_ROWS, CONV_ROWS)
            for c0 in range(0, C, LANES):
                ls = slice(c0, c0 + LANES)
                d = bufd[0, pl.ds(HALO + r0, CONV_ROWS), ls]
                for k, rows in _tap_reads(bufu, HALO - CONV_PAD, 1, r0, ls):
                    prod = d * rows
                    part = prod[0:8]
                    for j in range(8, CONV_ROWS, 8):
                        part = part + prod[j:j + 8]
                    dw_acc[k, :, ls] += part
            return carry

        lax.fori_loop(0, T // CONV_ROWS, dw_tile, 0)
        for k in range(CONV_WIDTH):
            dw_ref[k:k + 1, :] += jnp.sum(dw_acc[k], axis=0, keepdims=True)
        du0 = du0_scr[...]
        a = ac[...].astype(F32)
        sg = _sigmoid(gc[...].astype(F32))
        dag_ref[:, 0:C] = (du0 * sg).astype(dag_ref.dtype)
        dag_ref[:, C:] = (du0 * a * sg * (1.0 - sg)).astype(dag_ref.dtype)

    dp, dc, dn = _halo_specs(S, T, C, 1)
    up, uc, un = _halo_specs(S, T, C, 0)
    ap, ac, an = _halo_specs(S, T, C, 3)
    gp, gc, gn = _halo_specs(S, T, C, 4)
    vec = pl.BlockSpec((1, C), lambda i: (0, 0))
    return pl.pallas_call(
        body, name="conv_bwd", grid=(nblk,),
        in_specs=[dp, dc, dn, up, uc, un, ap, ac, an, gp, gc, gn,
                  pl.BlockSpec((32, C), lambda i: (0, 0)), vec, vec],
        out_specs=[pl.BlockSpec((T, 2 * C), lambda i: (i, 0)), pl.BlockSpec((32, C), lambda i: (0, 0)),
                   pl.BlockSpec((8, C), lambda i: (0, 0))],
        out_shape=[_sds((S, 2 * C), BF16), _sds((32, C), F32), _sds((8, C), F32)],
        scratch_shapes=[_window_scratch(T, C), _window_scratch(T, C), pltpu.VMEM((T, C), F32),
                        pltpu.VMEM((CONV_WIDTH, 8, C), F32)],
        compiler_params=_params("arbitrary"))(dac, dac, dac, u1, u1, u1, y, y, y, y, y, y, conv_w32, ln_g, ln_b)


def _xatt_fwd(xq, xk, xv, *, tm=512):
    S = xq.shape[0]
    M = xk.shape[0]
    tm = min(tm, S)
    scale = XATT_HEAD_DIM ** -0.5

    def body(q_ref, k_ref, v_ref, o_ref):
        heads = [slice(h * XATT_HEAD_DIM, (h + 1) * XATT_HEAD_DIM) for h in range(XATT_HEADS)]
        s = [_nt(q_ref[:, sl], k_ref[:, sl]) * scale for sl in heads]
        e = [jnp.exp(t - jnp.max(t, axis=-1, keepdims=True)) for t in s]
        p = [t * (1.0 / jnp.sum(t, axis=-1, keepdims=True)) for t in e]
        for sl, t in zip(heads, p):
            o_ref[:, sl] = jnp.dot(t.astype(BF16), v_ref[:, sl], preferred_element_type=F32).astype(o_ref.dtype)

    row = pl.BlockSpec((tm, D_MODEL), lambda i: (i, 0))
    full = pl.BlockSpec((M, D_MODEL), lambda i: (0, 0))
    return pl.pallas_call(
        body, name="xatt_fwd", grid=(S // tm,), in_specs=[row, full, full], out_specs=row,
        out_shape=_sds((S, D_MODEL), BF16), compiler_params=_params("parallel"))(xq, xk, xv)


def _xatt_bwd(xq, xk, xv, dxo, *, tm=512):
    S = xq.shape[0]
    M = xk.shape[0]
    tm = min(tm, S)
    scale = XATT_HEAD_DIM ** -0.5

    def body(q_ref, k_ref, v_ref, do_ref, dq_ref, dk_ref, dv_ref):
        i = pl.program_id(0)

        @pl.when(i == 0)
        def _():
            dk_ref[...] = jnp.zeros_like(dk_ref)
            dv_ref[...] = jnp.zeros_like(dv_ref)

        heads = [slice(h * XATT_HEAD_DIM, (h + 1) * XATT_HEAD_DIM) for h in range(XATT_HEADS)]
        s = [_nt(q_ref[:, sl], k_ref[:, sl]) * scale for sl in heads]
        dp = [_nt(do_ref[:, sl], v_ref[:, sl]) for sl in heads]
        e = [jnp.exp(t - jnp.max(t, axis=-1, keepdims=True)) for t in s]
        p = [t * (1.0 / jnp.sum(t, axis=-1, keepdims=True)) for t in e]
        ds = [(pp * (t - jnp.sum(t * pp, axis=-1, keepdims=True))).astype(BF16) for pp, t in zip(p, dp)]
        for sl, pp, t in zip(heads, p, ds):
            dq_ref[:, sl] = (jnp.dot(t, k_ref[:, sl], preferred_element_type=F32) * scale).astype(dq_ref.dtype)
            dv_ref[:, sl] += _tn(pp.astype(BF16), do_ref[:, sl])
            dk_ref[:, sl] += _tn(t, q_ref[:, sl]) * scale

    row = pl.BlockSpec((tm, D_MODEL), lambda i: (i, 0))
    full = pl.BlockSpec((M, D_MODEL), lambda i: (0, 0))
    return pl.pallas_call(
        body, name="xatt_bwd", grid=(S // tm,), in_specs=[row, full, full, row], out_specs=[row, full, full],
        out_shape=[_sds((S, D_MODEL), BF16), _sds((M, D_MODEL), F32), _sds((M, D_MODEL), F32)],
        compiler_params=_params("arbitrary"))(xq, xk, xv, dxo)


def _row_tile(R):
    for t in (256, 128, 64, 32, 16, 8):
        if R % t == 0:
            return t
    return R


def _sum_partials(own, recv, me, *, name):
    _, R, C = own.shape
    t = _row_tile(R)

    def body(me_ref, own_ref, r_ref, o_ref):
        o_ref[...] = ((own_ref[...].astype(F32) + r_ref[0].astype(F32)) + r_ref[1].astype(F32)) + r_ref[2].astype(F32)

    return pl.pallas_call(
        body, name=name,
        grid_spec=pltpu.PrefetchScalarGridSpec(
            num_scalar_prefetch=1, grid=(R // t,),
            in_specs=[pl.BlockSpec((None, t, C), lambda i, me_ref: (me_ref[0], i, 0)),
                      pl.BlockSpec((3, t, C), lambda i, me_ref: (0, i, 0))],
            out_specs=pl.BlockSpec((t, C), lambda i, me_ref: (i, 0))),
        out_shape=_sds((R, C), F32), compiler_params=_params("parallel"))(me, own, recv)


def _adamw_math(w, g, m, v):
    m2 = ADAM_B1 * m + (1.0 - ADAM_B1) * g
    v2 = ADAM_B2 * v + (1.0 - ADAM_B2) * (g * g)
    m_hat = m2 / (1.0 - ADAM_B1 ** ADAM_STEP)
    v_hat = v2 / (1.0 - ADAM_B2 ** ADAM_STEP)
    delta = -ADAM_LR * (m_hat / (jnp.sqrt(v_hat) + ADAM_EPS) + ADAM_WD * w)
    return delta, m2, v2


def _adamw(parts, w, m, v, *, name):
    R, C = w.shape
    t = _row_tile(R)
    n = len(parts)

    def body(*refs):
        w_ref, m_ref, v_ref = refs[n:n + 3]
        g_ref, d_ref, m2_ref, v2_ref = refs[n + 3:]
        g = refs[0][...]
        for r in refs[1:n]:
            g = g + r[...]
        delta, m2, v2 = _adamw_math(w_ref[...], g, m_ref[...], v_ref[...])
        g_ref[...] = g
        d_ref[...] = delta
        m2_ref[...] = m2
        v2_ref[...] = v2

    blk = pl.BlockSpec((t, C), lambda i: (i, 0))
    return pl.pallas_call(
        body, name=name, grid=(R // t,), in_specs=[blk] * (n + 3), out_specs=[blk] * 4,
        out_shape=[_sds((R, C), F32)] * 4, compiler_params=_params("parallel"))(*parts, w, m, v)


def _sum_devices(gathered):
    _, R, C = gathered.shape

    def body(g_ref, o_ref):
        acc = g_ref[0]
        for k in range(1, N_DEV):
            acc = acc + g_ref[k]
        o_ref[...] = acc

    return pl.pallas_call(body, name="sum_devices", out_shape=_sds((R, C), F32))(gathered)


def _chip_peers():
    x, y = lax.axis_index("x"), lax.axis_index("y")
    return [(1 - x, y), (x, 1 - y), (1 - x, 1 - y)]


HBM_SPEC = pl.BlockSpec(memory_space=pltpu.HBM)
SEM_SPEC = pl.BlockSpec(memory_space=pltpu.SEMAPHORE)


def _exchange_start(mode, srcs, zones, *, name):
    n = len(srcs)

    def body(*refs):
        ins, lands = refs[:n], refs[n:2 * n]
        send_sems, recv_sems = refs[2 * n:3 * n], refs[3 * n:4 * n]
        token = refs[-1]
        c = lax.axis_index("c")
        mine = 2 * lax.axis_index("x") + lax.axis_index("y")
        for t in range(n):
            for k, (px, py) in enumerate(_chip_peers()):
                if mode == "gather":
                    s, d = ins[t], lands[t].at[mine]
                else:
                    s, d = ins[t].at[2 * px + py], lands[t].at[k]
                pltpu.make_async_remote_copy(src_ref=s, dst_ref=d, send_sem=send_sems[t], recv_sem=recv_sems[t],
                                             device_id=(px, py, c), device_id_type=MESH).start()
        token[...] = jnp.zeros_like(token)

    hbm = lambda a: pltpu.with_memory_space_constraint(a, pltpu.HBM)
    out = pl.pallas_call(
        body, name=name,
        in_specs=[HBM_SPEC] * (2 * n),
        out_specs=[SEM_SPEC] * (2 * n) + [HBM_SPEC] * (2 * n) + [pl.BlockSpec(memory_space=pltpu.VMEM)],
        out_shape=[pltpu.SemaphoreType.DMA(())] * (2 * n)
        + [pltpu.HBM(a.shape, a.dtype) for a in list(srcs) + list(zones)] + [_sds((8, LANES), F32)],
        input_output_aliases={i: 2 * n + i for i in range(2 * n)},
        compiler_params=pltpu.CompilerParams(has_side_effects=pltpu.SideEffectType.DATAFLOW_SIDE_EFFECTING),
    )(*[hbm(a) for a in list(srcs) + list(zones)])
    return out[:n], out[n:2 * n], out[2 * n:3 * n], out[3 * n:4 * n], out[-1]


def _exchange_wait(started, after, *, name):
    send_sems, recv_sems, srcs, zones, _ = started
    n = len(srcs)

    def body(*refs):
        lands = refs[n:2 * n]
        send_refs, recv_refs = refs[2 * n:3 * n], refs[3 * n:4 * n]
        me = (lax.axis_index("x"), lax.axis_index("y"), lax.axis_index("c"))
        for t in range(n):
            three = lands[t].at[pl.ds(0, N_CHIPS - 1)]
            cp = pltpu.make_async_remote_copy(src_ref=three, dst_ref=three, send_sem=send_refs[t],
                                              recv_sem=recv_refs[t], device_id=me, device_id_type=MESH)
            cp.wait_send()
            cp.wait_recv()

    out = pl.pallas_call(
        body, name=name,
        in_specs=[HBM_SPEC] * (2 * n) + [SEM_SPEC] * (2 * n) + [pl.BlockSpec(memory_space=pl.ANY)],
        out_specs=[HBM_SPEC] * (2 * n),
        out_shape=[pltpu.HBM(a.shape, a.dtype) for a in list(srcs) + list(zones)],
        input_output_aliases={i: i for i in range(2 * n)},
        compiler_params=pltpu.CompilerParams(has_side_effects=pltpu.SideEffectType.DATAFLOW_SIDE_EFFECTING),
    )(*srcs, *zones, *send_sems, *recv_sems, after)
    return out[:n], out[n:]


def _allgather_small(small):
    def body(small_ref, gath_ref, send_sems, recv_sems, loc_sem):
        x, y, c = lax.axis_index("x"), lax.axis_index("y"), lax.axis_index("c")
        me = 4 * x + 2 * y + c
        flips = [(fx, fy, fc) for fx in (0, 1) for fy in (0, 1) for fc in (0, 1)][1:]

        def flipped(fx, fy, fc):
            return (1 - x if fx else x, 1 - y if fy else y, 1 - c if fc else c)

        loc = pltpu.make_async_copy(small_ref, gath_ref.at[me], loc_sem)
        loc.start()
        sends = []
        for j, flip in enumerate(flips):
            cp = pltpu.make_async_remote_copy(
                src_ref=small_ref, dst_ref=gath_ref.at[me], send_sem=send_sems.at[j], recv_sem=recv_sems.at[j],
                device_id=flipped(*flip), device_id_type=MESH)
            cp.start()
            sends.append(cp)
        for j, flip in enumerate(flips):
            px, py, pc = flipped(*flip)
            pltpu.make_async_remote_copy(
                src_ref=small_ref, dst_ref=gath_ref.at[4 * px + 2 * py + pc], send_sem=send_sems.at[j],
                recv_sem=recv_sems.at[j], device_id=(px, py, pc), device_id_type=MESH).wait_recv()
        for cp in sends:
            cp.wait_send()
        loc.wait()

    any_spec = pl.BlockSpec(memory_space=pl.ANY)
    return pl.pallas_call(
        body, name="allgather_small", in_specs=[any_spec], out_specs=any_spec,
        out_shape=_sds((N_DEV,) + small.shape, small.dtype),
        scratch_shapes=[pltpu.SemaphoreType.DMA((N_DEV - 1,)), pltpu.SemaphoreType.DMA((N_DEV - 1,)),
                        pltpu.SemaphoreType.DMA])(small)


def _swap_with_sibling(parts):
    n = len(parts)

    def body(*refs):
        ins, outs = refs[:n], refs[n:2 * n]
        send_sems, recv_sems = refs[2 * n:]
        sib = (lax.axis_index("x"), lax.axis_index("y"), 1 - lax.axis_index("c"))
        cps = []
        for t in range(n):
            cp = pltpu.make_async_remote_copy(
                src_ref=ins[t], dst_ref=outs[t], send_sem=send_sems.at[t], recv_sem=recv_sems.at[t],
                device_id=sib, device_id_type=MESH)
            cp.start()
            cps.append(cp)
        for cp in cps:
            cp.wait()

    any_spec = pl.BlockSpec(memory_space=pl.ANY)
    return pl.pallas_call(
        body, name="swap_with_sibling", in_specs=[any_spec] * n, out_specs=[any_spec] * n,
        out_shape=[_sds(p.shape, p.dtype) for p in parts],
        scratch_shapes=[pltpu.SemaphoreType.DMA((n,)), pltpu.SemaphoreType.DMA((n,))])(*parts)


BIG = ("w_in", "w_out", "w_xq", "w_xk", "w_xv", "w_xo", "w_up", "w_down")
COL_SHARDED = ("w_in", "w_up")


def _as_matrix(name, w4):
    if name in COL_SHARDED:
        return w4
    return w4.reshape(1, w4.shape[0] * w4.shape[1], w4.shape[2])


def _transposed(w3):
    nsh, K, n = w3.shape
    return jnp.swapaxes(w3, 1, 2).reshape(1, nsh * n, K)


def _shard_layout(name, g):
    if name in COL_SHARDED:
        return g
    return g.reshape(N_CHIPS, g.shape[0] * g.shape[1] // N_CHIPS, g.shape[2])


def _local_step(x, mem, target, vecs, comm):
    S = x.shape[0]
    tables = _rope_tables(S)

    xn = _rms_fwd(x, vecs["norm_mix_g"], name="rms_mix")
    w_in, conv_w32 = comm["first"](xn)
    y = _mm_nn(xn, w_in, name="mm_in", tm=2048, tn=640)
    qk, v_perm = _rope_fwd(y, tables)
    v_src = [(y, 2)] + [(v, 0) for v in v_perm[1:]]
    outs, lses = zip(*[_att_fwd(qk[p], v_src[p], d, name=f"att_fwd_d{d}") for p, d in enumerate(DILATIONS)])
    att, lg = _att_combine(outs, lses)
    cv, u1 = _conv_fwd(y, conv_w32, vecs["conv_b"], vecs["conv_ln_g"], vecs["conv_ln_b"])
    mix = jnp.concatenate([att, cv], axis=1)
    Wm = {k: _as_matrix(k, v) for k, v in comm["rest"](mix).items()}
    Wm["w_in"] = w_in
    h1 = _mm_nn(mix, Wm["w_out"], name="mm_out", out_dtype=F32, res=x)
    hn = _rms_fwd(h1, vecs["norm_x_g"], name="rms_x")
    xq = _mm_nn(hn, Wm["w_xq"], name="mm_xq")
    mn = _rms_fwd(mem, vecs["norm_mem_g"], name="rms_mem")
    xk = _mm_nn(mn, Wm["w_xk"], name="mm_xk")
    xv = _mm_nn(mn, Wm["w_xv"], name="mm_xv")
    xo = _xatt_fwd(xq, xk, xv)
    h2 = _mm_nn(xo, Wm["w_xo"], name="mm_xo", out_dtype=F32, res=h1)
    hm = _rms_fwd(h2, vecs["norm_mlp_g"], name="rms_mlp")
    relu_up = _mm_nn(hm, Wm["w_up"], name="mm_up", relu=True, tm=2048)
    h3 = _mm_nn(relu_up, Wm["w_down"], name="mm_down", out_dtype=F32, res=h2, a_squared=True, tm=512, tk=D_FF)

    dh3, dh3b, dg_final, loss = _loss_head(h3, vecs["norm_final_g"], target)
    g = {}
    g["w_down"] = _mm_tn(relu_up, dh3b, 1, name="dw_down", a_squared=True)
    dup = _mm_nt(dh3b, Wm["w_down"], name="d_act", out_dtype=BF16, mul=relu_up, tm=2048)
    g["w_up"] = _mm_tn(hm, dup, N_CHIPS, name="dw_up")
    sent = comm["send_mlp"]({k: _shard_layout(k, g[k]) for k in ("w_down", "w_up")})
    dhm = _mm_nn(dup, _transposed(Wm["w_up"]), name="d_hm", tm=512, tk=D_FF)
    dh2, dh2b, dg_mlp = _rms_bwd(dhm, h2, vecs["norm_mlp_g"] + sent[0:1, 0:1], dh3, name="rms_bwd_mlp")
    g["w_xo"] = _mm_tn(xo, dh2b, 1, name="dw_xo")
    dxo = _mm_nt(dh2b, Wm["w_xo"], name="d_xo", out_dtype=BF16)
    dxq, dxk, dxv = _xatt_bwd(xq, xk, xv, dxo)
    g["w_xq"] = _mm_tn(hn, dxq, 1, name="dw_xq")
    dhn = _mm_nt(dxq, Wm["w_xq"], name="d_hn", out_dtype=BF16)
    dh1, dh1b, dg_x = _rms_bwd(dhn, h1, vecs["norm_x_g"], dh2, name="rms_bwd_x")
    dxkb, dxvb = dxk.astype(BF16), dxv.astype(BF16)
    g["w_xk"] = _mm_tn(mn, dxkb, 1, name="dw_xk")
    g["w_xv"] = _mm_tn(mn, dxvb, 1, name="dw_xv")
    dmn = _mm_nt(jnp.concatenate([dxkb, dxvb], axis=1),
                 jnp.concatenate([Wm["w_xk"], Wm["w_xv"]], axis=2), name="d_mn", out_dtype=BF16)
    _, _, dg_mem = _rms_bwd(dmn, mem, vecs["norm_mem_g"], None, name="rms_bwd_mem", bf16_copy=False)
    g["w_out"] = _mm_tn(mix, dh1b, 1, name="dw_out")
    sent = comm["send_att"]({k: _shard_layout(k, g[k]) for k in ("w_out", "w_xq", "w_xk", "w_xv", "w_xo")})
    dac = _mm_nt(dh1b, Wm["w_out"], name="d_mix", out_dtype=BF16)
    dag, dconv_w, dconv_small = _conv_bwd(dac, u1, y, conv_w32, vecs["conv_ln_g"] + sent[0:1, 0:1],
                                          vecs["conv_ln_b"])
    delta, do_perm = _att_delta(dac, att)
    do_src = [(dac, 0)] + [(t, 0) for t in do_perm[1:]]
    dq, dk, dv = zip(*[_att_bwd(qk[p], v_src[p], do_src[p], lg[p], delta[p], d, name=f"att_bwd_d{d}")
                       for p, d in enumerate(DILATIONS)])
    dy = _assemble_dy(dq, dk, dv, dag, tables)
    sent = comm["send_in"]({"w_in": _mm_tn(xn, dy, N_CHIPS, name="dw_in", tn=640)})
    dxn = _mm_nn(dy, _transposed(Wm["w_in"]), name="d_xn", tm=512, tk=D_IN)
    grad_x, _, dg_mix = _rms_bwd(dxn, x, vecs["norm_mix_g"] + sent[0:1, 0:1], dh1, name="rms_bwd_mix",
                                 bf16_copy=False)

    small = dict(conv_w=dconv_w, conv_small=dconv_small, norm_mix_g=dg_mix, norm_x_g=dg_x, norm_mem_g=dg_mem,
                 norm_mlp_g=dg_mlp, norm_final_g=dg_final, loss=loss)
    return grad_x, small


SMALL_ORDER = ("conv_w", "conv_small", "norm_mix_g", "norm_x_g", "norm_mem_g", "norm_mlp_g", "norm_final_g", "loss")


def _pack_small(small):
    rows, offs, pos = [], {}, 0
    for k in SMALL_ORDER:
        a = small[k]
        a = a.reshape(a.shape[0] * a.shape[1] // SMALL_W, SMALL_W)
        pad = (-a.shape[0]) % 8
        if pad:
            a = jnp.pad(a, ((0, pad), (0, 0)))
        rows.append(a)
        offs[k] = pos
        pos += a.shape[0]
    return jnp.concatenate(rows, axis=0), offs


def kernel(x, mem, norm_mix_g, w_in, conv_w, conv_b, conv_ln_g, conv_ln_b, w_out, norm_x_g, norm_mem_g, w_xq, w_xk, w_xv, w_xo, norm_mlp_g, w_up, w_down, norm_final_g, loss_target, m_norm_mix_g, m_w_in, m_conv_w, m_conv_b, m_conv_ln_g, m_conv_ln_b, m_w_out, m_norm_x_g, m_norm_mem_g, m_w_xq, m_w_xk, m_w_xv, m_w_xo, m_norm_mlp_g, m_w_up, m_w_down, m_norm_final_g, v_norm_mix_g, v_w_in, v_conv_w, v_conv_b, v_conv_ln_g, v_conv_ln_b, v_w_out, v_norm_x_g, v_norm_mem_g, v_w_xq, v_w_xk, v_w_xv, v_w_xo, v_norm_mlp_g, v_w_up, v_w_down, v_norm_final_g):
    names = ("norm_mix_g", "w_in", "conv_w", "conv_b", "conv_ln_g", "conv_ln_b", "w_out", "norm_x_g", "norm_mem_g",
             "w_xq", "w_xk", "w_xv", "w_xo", "norm_mlp_g", "w_up", "w_down", "norm_final_g")
    wts = dict(zip(names, (norm_mix_g, w_in, conv_w, conv_b, conv_ln_g, conv_ln_b, w_out, norm_x_g, norm_mem_g,
                           w_xq, w_xk, w_xv, w_xo, norm_mlp_g, w_up, w_down, norm_final_g)))
    mom = dict(zip(names, (m_norm_mix_g, m_w_in, m_conv_w, m_conv_b, m_conv_ln_g, m_conv_ln_b, m_w_out, m_norm_x_g,
                           m_norm_mem_g, m_w_xq, m_w_xk, m_w_xv, m_w_xo, m_norm_mlp_g, m_w_up, m_w_down, m_norm_final_g)))
    var = dict(zip(names, (v_norm_mix_g, v_w_in, v_conv_w, v_conv_b, v_conv_ln_g, v_conv_ln_b, v_w_out, v_norm_x_g,
                           v_norm_mem_g, v_w_xq, v_w_xk, v_w_xv, v_w_xo, v_norm_mlp_g, v_w_up, v_w_down, v_norm_final_g)))
    chip = 2 * lax.axis_index("x") + lax.axis_index("y")

    def own_slot(shard, slot):
        return lax.dynamic_update_slice(lax.empty((N_CHIPS,) + shard.shape, shard.dtype), shard[None], (slot, 0, 0))

    conv_w_pad = jnp.pad(wts["conv_w"][0], ((0, 1), (0, 0)))
    first_shards = [wts["w_in"][0].astype(BF16), conv_w_pad]
    gathering_first = _exchange_start("gather", first_shards, [own_slot(s, chip) for s in first_shards],
                                      name="gather_first_start")
    rest = tuple(k for k in BIG if k != "w_in")
    rest_shards = [wts[k][0].astype(BF16) for k in rest]
    behind_first = gathering_first[4][0, 0].astype(jnp.int32)
    gathering = _exchange_start("gather", rest_shards, [own_slot(s, chip + behind_first) for s in rest_shards],
                                name="gather_rest_start")
    sending = {}

    def wait_first(after):
        _, (w_in_all, conv_w_all) = _exchange_wait(gathering_first, after, name="gather_first_wait")
        return w_in_all, jnp.transpose(conv_w_all, (1, 0, 2)).reshape(32, D_CONV)

    def wait_rest(after):
        _, zones = _exchange_wait(gathering, after, name="gather_rest_wait")
        return dict(zip(rest, zones))

    def send(group, grads):
        keys = tuple(grads)
        zones = [lax.empty((N_CHIPS - 1,) + grads[k].shape[1:], grads[k].dtype) for k in keys]
        sending[group] = (keys, _exchange_start("scatter", [grads[k] for k in keys], zones,
                                                name=f"scatter_{group}_start"))
        return sending[group][1][4]

    comm = dict(first=wait_first, rest=wait_rest, send_mlp=lambda grads: send("mlp", grads),
                send_att=lambda grads: send("att", grads), send_in=lambda grads: send("in", grads))
    vecs = {k: wts[k] for k in ("conv_b", "conv_ln_g", "conv_ln_b", "norm_x_g", "norm_mem_g", "norm_mlp_g")}
    vecs["norm_mix_g"] = wts["norm_mix_g"] + gathering[4][0:1, 0:1]
    vecs["norm_final_g"] = wts["norm_final_g"].reshape(1, D_MODEL)
    grad_x, small = _local_step(x[0], mem[0], loss_target[0], vecs, comm)

    packed, offs = _pack_small(small)
    gath = _allgather_small(packed)
    big, recv = {}, {}
    for group in ("mlp", "att", "in"):
        keys, started = sending[group]
        srcs, zones = _exchange_wait(started, gath, name=f"scatter_{group}_wait")
        big.update(zip(keys, srcs))
        recv.update(zip(keys, zones))
    me_arr = jnp.reshape(chip, (1,)).astype(jnp.int32)
    sums = [_sum_partials(big[k], recv[k], me_arr, name=f"sum_{k}") for k in BIG]
    sib = _swap_with_sibling(sums)
    tot_small = _sum_devices(gath)

    res = {}
    for k, s_mine, s_sib in zip(BIG, sums, sib):
        res[k] = _adamw([s_mine, s_sib], wts[k][0], mom[k][0], var[k][0], name=f"adamw_{k}")

    def piece(key, nrows):
        return tot_small[offs[key]:offs[key] + nrows]

    def small_update(k, gfull):
        return _adamw([gfull], wts[k].reshape(gfull.shape), mom[k].reshape(gfull.shape),
                      var[k].reshape(gfull.shape), name=f"adamw_{k}")

    dcw = piece("conv_w", 32)[:CONV_WIDTH]
    dcw_mine = lax.dynamic_slice_in_dim(dcw, chip * (D_CONV // N_CHIPS), D_CONV // N_CHIPS, axis=1)
    res["conv_w"] = small_update("conv_w", dcw_mine)
    cs = piece("conv_small", 8)
    res["conv_b"] = small_update("conv_b", cs[0:1])
    res["conv_ln_g"] = small_update("conv_ln_g", cs[1:2])
    res["conv_ln_b"] = small_update("conv_ln_b", cs[2:3])
    for k in ("norm_mix_g", "norm_x_g", "norm_mem_g", "norm_mlp_g", "norm_final_g"):
        res[k] = small_update(k, piece(k, 8 * D_MODEL // SMALL_W).reshape(8, D_MODEL)[0:1])
    loss = piece("loss", 8)[0, 0]

    outs = [loss, grad_x[None]]
    for j in range(4):
        outs += [res[k][j].reshape(wts[k].shape) for k in names]
    return tuple(outs)
```

```python
import jax
import jax.numpy as jnp
from jax import lax
from jax.experimental import pallas as pl
from jax.experimental.pallas import tpu as pltpu

F32 = jnp.float32
BF16 = jnp.bfloat16
MESH = pl.DeviceIdType.MESH

D_MODEL = 1024
ATT_HEADS = 8
HEAD_DIM = 64
D_ATT = ATT_HEADS * HEAD_DIM
D_CONV = D_MODEL - D_ATT
DILATIONS = (1, 4, 16)
HALF = 64
ROPE_THETA = 500000.0
ROT_DIM = HEAD_DIM // 4
CONV_WIDTH = 31
CONV_PAD = (CONV_WIDTH - 1) // 2
XATT_HEADS = 4
XATT_HEAD_DIM = D_MODEL // XATT_HEADS
D_FF = 4 * D_MODEL
D_IN = 3 * D_ATT + 2 * D_CONV
EPS = 1e-6
NEG_INF = -1e30
N_CHIPS = 4
N_DEV = 8

ADAM_LR = 0.001
ADAM_B1 = 0.9
ADAM_B2 = 0.999
ADAM_EPS = 1e-08
ADAM_WD = 0.01
ADAM_STEP = 10

VMEM_LIMIT_V7X = 56 * 1024 * 1024
LANES = 128
HALO = 16
CONV_ROWS = 64
ATT_BLOCK = 128
SMALL_W = 512


def _params(*sem):
    return pltpu.CompilerParams(dimension_semantics=sem, vmem_limit_bytes=VMEM_LIMIT_V7X)


def _sds(shape, dtype):
    return jax.ShapeDtypeStruct(shape, dtype)


def _squared(a):
    af = a.astype(F32)
    return (af * af).astype(BF16)


def _mm_nn(a, w3, *, name, out_dtype=BF16, res=None, relu=False, a_squared=False, tm=1024, tn=None, tk=1024):
    M, K = a.shape
    nsh, _, n = w3.shape
    tm, tk = min(tm, M), min(tk, K)
    tn = tn or min(n, 1024)
    npt, nk = n // tn, K // tk
    nj, N = nsh * npt, nsh * n
    n_out = 1

    def body(*refs):
        a_ref, w_ref = refs[0], refs[1]
        pos = 2
        res_ref = None
        if res is not None:
            res_ref = refs[pos]
            pos += 1
        outs = refs[pos:pos + n_out]
        acc_ref = refs[pos + n_out] if nk > 1 else None

        def finish(acc):
            if res_ref is not None:
                acc = acc + res_ref[...]
            if relu:
                acc = jnp.maximum(acc, 0.0)
            outs[0][...] = acc.astype(outs[0].dtype)

        a_val = _squared(a_ref[...]) if a_squared else a_ref[...]
        part = jnp.dot(a_val, w_ref[...], preferred_element_type=F32)
        if nk == 1:
            finish(part)
        else:
            k = pl.program_id(2)

            @pl.when(k == 0)
            def _():
                acc_ref[...] = part

            @pl.when(k > 0)
            def _():
                acc_ref[...] += part

            @pl.when(k == nk - 1)
            def _():
                finish(acc_ref[...])

    in_specs = [pl.BlockSpec((tm, tk), lambda i, j, k: (i, k)),
                pl.BlockSpec((None, tk, tn), lambda i, j, k: (j // npt, k, j % npt))]
    args = [a, w3]
    if res is not None:
        in_specs.append(pl.BlockSpec((tm, tn), lambda i, j, k: (i, j)))
        args.append(res)
    out_spec = pl.BlockSpec((tm, tn), lambda i, j, k: (i, j))
    out = pl.pallas_call(
        body, name=name, grid=(M // tm, nj, nk), in_specs=in_specs,
        out_specs=[out_spec] * n_out, out_shape=[_sds((M, N), out_dtype)] * n_out,
        scratch_shapes=[pltpu.VMEM((tm, tn), F32)] if nk > 1 else [],
        compiler_params=_params("parallel", "parallel", "arbitrary"))(*args)
    return out[0]


def _mm_nt(dy, w3, *, name, out_dtype=F32, mul=None, tm=1024, tn=None, tko=1024):
    M, N = dy.shape
    nsh, K, n = w3.shape
    tm, tko = min(tm, M), min(tko, K)
    tn = tn or min(n, 1024)
    npt = n // tn
    nj = nsh * npt

    def body(*refs):
        dy_ref, w_ref = refs[0], refs[1]
        pos = 2
        mul_ref = None
        if mul is not None:
            mul_ref = refs[pos]
            pos += 1
        out_ref = refs[pos]
        acc_ref = refs[pos + 1] if nj > 1 else None

        def finish(acc):
            if mul_ref is not None:
                acc = acc * (2.0 * mul_ref[...].astype(F32))
            out_ref[...] = acc.astype(out_ref.dtype)

        part = lax.dot_general(dy_ref[...], w_ref[...], (((1,), (1,)), ((), ())), preferred_element_type=F32)
        if nj == 1:
            finish(part)
        else:
            j = pl.program_id(2)

            @pl.when(j == 0)
            def _():
                acc_ref[...] = part

            @pl.when(j > 0)
            def _():
                acc_ref[...] += part

            @pl.when(j == nj - 1)
            def _():
                finish(acc_ref[...])

    in_specs = [pl.BlockSpec((tm, tn), lambda i, ko, j: (i, j)),
                pl.BlockSpec((None, tko, tn), lambda i, ko, j: (j // npt, ko, j % npt))]
    args = [dy, w3]
    if mul is not None:
        in_specs.append(pl.BlockSpec((tm, tko), lambda i, ko, j: (i, ko)))
        args.append(mul)
    return pl.pallas_call(
        body, name=name, grid=(M // tm, K // tko, nj), in_specs=in_specs,
        out_specs=pl.BlockSpec((tm, tko), lambda i, ko, j: (i, ko)), out_shape=_sds((M, K), out_dtype),
        scratch_shapes=[pltpu.VMEM((tm, tko), F32)] if nj > 1 else [],
        compiler_params=_params("parallel", "parallel", "arbitrary"))(*args)


def _mm_tn(a, dy, nsh, *, name, out_dtype=BF16, a_squared=False, tm=2048, tk=1024, tn=None):
    M, K = a.shape
    N = dy.shape[1]
    n = N // nsh
    tm, tk = min(tm, M), min(tk, K)
    tn = tn or min(n, 1024)
    npt = n // tn
    nj, nm = nsh * npt, M // tm

    def body(a_ref, dy_ref, out_ref, acc_ref):
        m = pl.program_id(2)
        a_val = _squared(a_ref[...]) if a_squared else a_ref[...]
        part = lax.dot_general(a_val, dy_ref[...], (((0,), (0,)), ((), ())), preferred_element_type=F32)

        @pl.when(m == 0)
        def _():
            acc_ref[...] = part

        @pl.when(m > 0)
        def _():
            acc_ref[...] += part

        @pl.when(m == nm - 1)
        def _():
            out_ref[...] = acc_ref[...].astype(out_ref.dtype)

    return pl.pallas_call(
        body, name=name, grid=(K // tk, nj, nm),
        in_specs=[pl.BlockSpec((tm, tk), lambda kk, j, m: (m, kk)),
                  pl.BlockSpec((tm, tn), lambda kk, j, m: (m, j))],
        out_specs=pl.BlockSpec((None, tk, tn), lambda kk, j, m: (j // npt, kk, j % npt)),
        out_shape=_sds((nsh, K, n), out_dtype),
        scratch_shapes=[pltpu.VMEM((tk, tn), F32)],
        compiler_params=_params("parallel", "parallel", "arbitrary"))(a, dy)


def _rms_fwd(x, g, *, name, tm=512):
    M, Dm = x.shape
    tm = min(tm, M)

    def body(x_ref, g_ref, o_ref):
        xf = x_ref[...]
        r = lax.rsqrt(jnp.mean(xf * xf, axis=-1, keepdims=True) + EPS)
        o_ref[...] = (xf * r * g_ref[...]).astype(o_ref.dtype)

    return pl.pallas_call(
        body, name=name, grid=(M // tm,),
        in_specs=[pl.BlockSpec((tm, Dm), lambda i: (i, 0)), pl.BlockSpec((1, Dm), lambda i: (0, 0))],
        out_specs=pl.BlockSpec((tm, Dm), lambda i: (i, 0)), out_shape=_sds((M, Dm), BF16),
        compiler_params=_params("parallel"))(x, g)


def _rms_bwd(dxn, x, g, dres, *, name, bf16_copy=True, tm=512):
    M, Dm = x.shape
    tm = min(tm, M)
    has_res = dres is not None

    def body(*refs):
        dxn_ref, x_ref, g_ref = refs[:3]
        dres_ref = refs[3] if has_res else None
        dx_ref, dg_ref = refs[-1 - 1 - bf16_copy], refs[-1]
        dxb_ref = refs[-2] if bf16_copy else None
        i = pl.program_id(0)
        xf = x_ref[...]
        r = lax.rsqrt(jnp.mean(xf * xf, axis=-1, keepdims=True) + EPS)
        nrm = xf * r
        dxn_f = dxn_ref[...].astype(F32)
        dn = dxn_f * g_ref[...]
        dx = r * (dn - nrm * jnp.mean(dn * nrm, axis=-1, keepdims=True))
        if has_res:
            dx = dx + dres_ref[...]
        dx_ref[...] = dx
        if bf16_copy:
            dxb_ref[...] = dx.astype(dxb_ref.dtype)

        @pl.when(i == 0)
        def _():
            dg_ref[...] = jnp.zeros_like(dg_ref)

        dg_ref[0:1, :] += jnp.sum(dxn_f * nrm, axis=0, keepdims=True)

    row = pl.BlockSpec((tm, Dm), lambda i: (i, 0))
    in_specs = [row, row, pl.BlockSpec((1, Dm), lambda i: (0, 0))] + ([row] if has_res else [])
    args = [dxn, x, g] + ([dres] if has_res else [])
    out = pl.pallas_call(
        body, name=name, grid=(M // tm,), in_specs=in_specs,
        out_specs=[row] * (1 + bf16_copy) + [pl.BlockSpec((8, Dm), lambda i: (0, 0))],
        out_shape=[_sds((M, Dm), F32)] + [_sds((M, Dm), BF16)] * bf16_copy + [_sds((8, Dm), F32)],
        compiler_params=_params("arbitrary"))(*args)
    return out[0], (out[1] if bf16_copy else None), out[-1]


def _loss_head(h, g, target, *, tm=512):
    M, Dm = h.shape
    tm = min(tm, M)

    def body(h_ref, g_ref, t_ref, dh_ref, dhb_ref, dg_ref, loss_ref):
        i = pl.program_id(0)
        hf = h_ref[...]
        r = lax.rsqrt(jnp.mean(hf * hf, axis=-1, keepdims=True) + EPS)
        nrm = hf * r
        gv = g_ref[...]
        err = nrm * gv - t_ref[...]
        dy = err * (1.0 / Dm)
        dn = dy * gv
        dh = r * (dn - nrm * jnp.mean(dn * nrm, axis=-1, keepdims=True))
        dh_ref[...] = dh
        dhb_ref[...] = dh.astype(dhb_ref.dtype)

        @pl.when(i == 0)
        def _():
            dg_ref[...] = jnp.zeros_like(dg_ref)
            loss_ref[...] = jnp.zeros_like(loss_ref)

        dg_ref[0:1, :] += jnp.sum(dy * nrm, axis=0, keepdims=True)
        part = 0.5 * jnp.sum(jnp.mean(err * err, axis=-1, keepdims=True), axis=0, keepdims=True)
        sel = (lax.broadcasted_iota(jnp.int32, (8, 128), 0) == 0) & (lax.broadcasted_iota(jnp.int32, (8, 128), 1) == 0)
        loss_ref[...] += jnp.where(sel, part, 0.0)

    row = pl.BlockSpec((tm, Dm), lambda i: (i, 0))
    return pl.pallas_call(
        body, name="loss_head", grid=(M // tm,),
        in_specs=[row, pl.BlockSpec((1, Dm), lambda i: (0, 0)), row],
        out_specs=[row, row, pl.BlockSpec((8, Dm), lambda i: (0, 0)), pl.BlockSpec((8, 128), lambda i: (0, 0))],
        out_shape=[_sds((M, Dm), F32), _sds((M, Dm), BF16), _sds((8, Dm), F32), _sds((8, 128), F32)],
        compiler_params=_params("arbitrary"))(h, g, target)


def _class_spec(tm, d, width):
    return pl.BlockSpec((d, tm // d, width), lambda i: (0, i, 0))


def _row_scratch(tm, width):
    return pltpu.VMEM((width // LANES, tm, LANES), F32)


def _fill(scr, val):
    for c in range(scr.shape[0]):
        scr[c] = val[:, c * LANES:(c + 1) * LANES]


def _to_classes(scr, out_ref, d):
    n = scr.shape[1] // d
    for r in range(d):
        for c in range(scr.shape[0]):
            out_ref[r, :, c * LANES:(c + 1) * LANES] = scr[c, pl.ds(r, n, stride=d), :].astype(out_ref.dtype)


def _from_classes(in_ref, scr, d):
    n = scr.shape[1] // d
    for r in range(d):
        blk = in_ref[r].astype(F32)
        for c in range(scr.shape[0]):
            scr[c, pl.ds(r, n, stride=d), :] = blk[:, c * LANES:(c + 1) * LANES]
    return jnp.concatenate([scr[c] for c in range(scr.shape[0])], axis=1)


def _rope_tables(S):
    half = ROT_DIM // 2
    freqs = ROPE_THETA ** (-jnp.arange(0, ROT_DIM, 2, dtype=F32) / ROT_DIM)
    ang = jnp.arange(S, dtype=F32)[:, None] * freqs[None, :]
    cos, sin = jnp.cos(ang), jnp.sin(ang)
    ones = jnp.ones((S, HEAD_DIM - ROT_DIM), F32)
    zeros = jnp.zeros((S, HEAD_DIM - ROT_DIM), F32)
    zh = jnp.zeros((S, half), F32)
    c = jnp.concatenate([cos, cos, ones], axis=1)
    sa = jnp.concatenate([-sin, zh, zeros], axis=1)
    sb = jnp.concatenate([zh, sin, zeros], axis=1)
    return tuple(jnp.tile(t, (1, LANES // HEAD_DIM)) for t in (c, sa, sb))


def _rope_fwd(y, tables, *, tm=512):
    S = y.shape[0]
    W = 2 * D_ATT
    tm = min(tm, S)
    half = ROT_DIM // 2
    dils = [d for d in DILATIONS if d > 1]

    def body(y_ref, c_ref, sa_ref, sb_ref, qk_ref, *rest):
        qk_outs, v_outs = rest[:len(dils)], rest[len(dils):2 * len(dils)]
        scr_qk, scr_v = rest[2 * len(dils):]
        t = y_ref[:, 0:W].astype(F32)
        rep = W // LANES
        c, sa, sb = (jnp.tile(r[...], (1, rep)) for r in (c_ref, sa_ref, sb_ref))
        rot = t * c + pltpu.roll(t, W - half, axis=1) * sa + pltpu.roll(t, half, axis=1) * sb
        qk_ref[...] = rot.astype(qk_ref.dtype)
        _fill(scr_qk, rot)
        _fill(scr_v, y_ref[:, W:W + D_ATT].astype(F32))
        for d, qo, vo in zip(dils, qk_outs, v_outs):
            _to_classes(scr_qk, qo, d)
            _to_classes(scr_v, vo, d)

    tab = pl.BlockSpec((tm, LANES), lambda i: (i, 0))
    out = pl.pallas_call(
        body, name="rope_fwd", grid=(S // tm,),
        in_specs=[pl.BlockSpec((tm, 3 * D_ATT), lambda i: (i, 0)), tab, tab, tab],
        out_specs=[pl.BlockSpec((tm, W), lambda i: (i, 0))] + [_class_spec(tm, d, W) for d in dils]
        + [_class_spec(tm, d, D_ATT) for d in dils],
        out_shape=[_sds((S, W), BF16)] + [_sds((d, S // d, W), BF16) for d in dils]
        + [_sds((d, S // d, D_ATT), BF16) for d in dils],
        scratch_shapes=[_row_scratch(tm, W), _row_scratch(tm, D_ATT)],
        compiler_params=_params("parallel"))(y, *tables)
    qk = [out[0]] + [o.reshape(S, W) for o in out[1:1 + len(dils)]]
    v = [None] + [o.reshape(S, D_ATT) for o in out[1 + len(dils):]]
    return qk, v


def _assemble_dy(dq, dk, dv, dag, tables, *, tm=512):
    S = dag.shape[0]
    tm = min(tm, S)
    half = ROT_DIM // 2
    W = D_ATT
    n_pat = len(DILATIONS)

    def body(*refs):
        groups = [refs[g * n_pat:(g + 1) * n_pat] for g in range(3)]
        dag_ref, c_ref, sa_ref, sb_ref, o_ref, scr = refs[3 * n_pat:]
        rep = W // LANES
        c, sa, sb = (jnp.tile(r[...], (1, rep)) for r in (c_ref, sa_ref, sb_ref))

        def total(rs):
            acc = rs[0][...].astype(F32)
            for d, r in zip(DILATIONS[1:], rs[1:]):
                acc = acc + _from_classes(r, scr, d)
            return acc

        def unrope(dr):
            return dr * c + pltpu.roll(dr * sa, half, axis=1) + pltpu.roll(dr * sb, W - half, axis=1)

        o_ref[:, 0:W] = unrope(total(groups[0])).astype(o_ref.dtype)
        o_ref[:, W:2 * W] = unrope(total(groups[1])).astype(o_ref.dtype)
        o_ref[:, 2 * W:3 * W] = total(groups[2]).astype(o_ref.dtype)
        o_ref[:, 3 * W:] = dag_ref[...]

    specs = [pl.BlockSpec((tm, W), lambda i: (i, 0))] + [_class_spec(tm, d, W) for d in DILATIONS[1:]]
    tab = pl.BlockSpec((tm, LANES), lambda i: (i, 0))
    args = [a if d == 1 else a.reshape(d, S // d, W) for grp in (dq, dk, dv) for d, a in zip(DILATIONS, grp)]
    return pl.pallas_call(
        body, name="assemble_dy", grid=(S // tm,),
        in_specs=specs * 3 + [pl.BlockSpec((tm, 2 * D_CONV), lambda i: (i, 0)), tab, tab, tab],
        out_specs=pl.BlockSpec((tm, D_IN), lambda i: (i, 0)), out_shape=_sds((S, D_IN), BF16),
        scratch_shapes=[_row_scratch(tm, W)],
        compiler_params=_params("parallel"))(*args, dag, *tables)


def _seq_specs(L, tb, col):
    nb, per, nh = L // tb, tb // HALF, L // HALF
    centre = pl.BlockSpec((tb, D_ATT), lambda r, i: (r * nb + i, col))
    prev = pl.BlockSpec((HALF, D_ATT), lambda r, i: (r * nh + jnp.maximum(i * per - 1, 0), col))
    nxt = pl.BlockSpec((HALF, D_ATT), lambda r, i: (r * nh + jnp.minimum((i + 1) * per, nh - 1), col))
    return prev, centre, nxt


def _band_mask(i, tq, L):
    shape = (tq, tq + 2 * HALF)
    c_idx = lax.broadcasted_iota(jnp.int32, shape, 0)
    w_idx = lax.broadcasted_iota(jnp.int32, shape, 1)
    diff = w_idx - c_idx
    wpos = i * tq - HALF + w_idx
    return (diff >= 0) & (diff <= 2 * HALF) & (wpos >= 0) & (wpos < L)


def _lane_groups():
    for c0 in range(0, D_ATT, LANES):
        yield slice(c0, c0 + LANES)


def _first_head(rows):
    return lax.broadcasted_iota(jnp.int32, (rows, LANES), 1) < HEAD_DIM


def _split_pair(x, first):
    zero = jnp.zeros_like(x)
    return jnp.where(first, x, zero), jnp.where(first, zero, x)


def _nt(a, b):
    return lax.dot_general(a, b, (((1,), (1,)), ((), ())), preferred_element_type=F32)


def _tn(a, b):
    return lax.dot_general(a, b, (((0,), (0,)), ((), ())), preferred_element_type=F32)


ATT_SCALE = HEAD_DIM ** -0.5


def _att_fwd(qk, v_src, d, *, name):
    S = qk.shape[0]
    L = S // d
    tq = min(ATT_BLOCK, L)
    v_arr, v_col = v_src

    def body(q_ref, kp_ref, kc_ref, kn_ref, vp_ref, vc_ref, vn_ref, o_ref, lse_ref):
        i = pl.program_id(1)
        valid = _band_mask(i, tq, L)
        q = q_ref[...] * ATT_SCALE
        kwin = jnp.concatenate([kp_ref[...], kc_ref[...], kn_ref[...]], axis=0)
        vwin = jnp.concatenate([vp_ref[...], vc_ref[...], vn_ref[...]], axis=0)
        first = _first_head(tq)
        groups = list(_lane_groups())
        heads = [(ls, t) for ls in groups for t in _split_pair(q[:, ls], first)]
        s = [jnp.where(valid, _nt(t, kwin[:, ls]), NEG_INF) for ls, t in heads]
        m = [jnp.max(t, axis=-1, keepdims=True) for t in s]
        p = [jnp.exp(t - mm) for t, mm in zip(s, m)]
        den = [jnp.sum(t, axis=-1, keepdims=True) for t in p]
        o = [jnp.dot(t.astype(BF16), vwin[:, ls], preferred_element_type=F32) * (1.0 / dd)
             for t, dd, (ls, _) in zip(p, den, heads)]
        lse = [mm + jnp.log(dd) for mm, dd in zip(m, den)]
        for g, ls in enumerate(groups):
            o_ref[:, ls] = jnp.where(first, o[2 * g], o[2 * g + 1]).astype(o_ref.dtype)
            lse_ref[:, ls] = jnp.where(first, lse[2 * g], lse[2 * g + 1])

    _, qc, _ = _seq_specs(L, tq, 0)
    kp, kc, kn = _seq_specs(L, tq, 1)
    vp, vc, vn = _seq_specs(L, tq, v_col)
    out = pl.BlockSpec((tq, D_ATT), lambda r, i: (r * (L // tq) + i, 0))
    return pl.pallas_call(
        body, name=name, grid=(d, L // tq),
        in_specs=[qc, kp, kc, kn, vp, vc, vn], out_specs=[out, out],
        out_shape=[_sds((S, D_ATT), BF16), _sds((S, D_ATT), F32)],
        compiler_params=_params("parallel", "parallel"))(qk, qk, qk, qk, v_arr, v_arr, v_arr)


def _att_combine(outs, lses, *, tm=512):
    S = outs[0].shape[0]
    tm = min(tm, S)
    dils = DILATIONS[1:]
    n_d = len(dils)

    def body(*refs):
        o_refs, l_refs = refs[0:1 + n_d], refs[1 + n_d:2 + 2 * n_d]
        att_ref, lg_ref = refs[2 + 2 * n_d:4 + 2 * n_d]
        lg_outs = refs[4 + 2 * n_d:4 + 3 * n_d]
        scr = refs[4 + 3 * n_d:]
        scr_o, scr_l, scr_lg = scr[:n_d], scr[n_d:2 * n_d], scr[2 * n_d]
        ls = [l_refs[0][...]] + [_from_classes(r, s, d) for r, s, d in zip(l_refs[1:], scr_l, dils)]
        os_ = [o_refs[0][...].astype(F32)] + [_from_classes(r, s, d) for r, s, d in zip(o_refs[1:], scr_o, dils)]
        mx = ls[0]
        for l in ls[1:]:
            mx = jnp.maximum(mx, l)
        es = [jnp.exp(l - mx) for l in ls]
        tot = es[0]
        num = es[0] * os_[0]
        for e, o in zip(es[1:], os_[1:]):
            tot = tot + e
            num = num + e * o
        att_ref[...] = (num / tot).astype(att_ref.dtype)
        lg = mx + jnp.log(tot)
        lg_ref[...] = lg
        _fill(scr_lg, lg)
        for d, out in zip(dils, lg_outs):
            _to_classes(scr_lg, out, d)

    nat = pl.BlockSpec((tm, D_ATT), lambda i: (i, 0))
    specs = [nat] + [_class_spec(tm, d, D_ATT) for d in dils]
    view = lambda arrs: [arrs[0]] + [a.reshape(d, S // d, D_ATT) for a, d in zip(arrs[1:], dils)]
    out = pl.pallas_call(
        body, name="att_combine", grid=(S // tm,), in_specs=specs * 2,
        out_specs=[nat, nat] + specs[1:],
        out_shape=[_sds((S, D_ATT), BF16), _sds((S, D_ATT), F32)] + [_sds((d, S // d, D_ATT), F32) for d in dils],
        scratch_shapes=[_row_scratch(tm, D_ATT)] * (2 * n_d + 1),
        compiler_params=_params("parallel"))(*view(list(outs)), *view(list(lses)))
    return out[0], [out[1]] + [o.reshape(S, D_ATT) for o in out[2:]]


def _att_delta(dac, att, *, tm=512):
    S = att.shape[0]
    tm = min(tm, S)
    dils = DILATIONS[1:]
    n_d = len(dils)

    def body(do_ref, o_ref, dl_ref, *rest):
        dl_outs, do_outs = rest[:n_d], rest[n_d:2 * n_d]
        scr_dl, scr_do = rest[2 * n_d:]
        do = do_ref[...].astype(F32)
        prod = do * o_ref[...].astype(F32)
        per_head = [jnp.broadcast_to(jnp.sum(prod[:, h * HEAD_DIM:(h + 1) * HEAD_DIM], axis=-1, keepdims=True),
                                     (tm, HEAD_DIM)) for h in range(ATT_HEADS)]
        dl = jnp.concatenate(per_head, axis=1)
        dl_ref[...] = dl
        _fill(scr_dl, dl)
        _fill(scr_do, do)
        for d, dlo, doo in zip(dils, dl_outs, do_outs):
            _to_classes(scr_dl, dlo, d)
            _to_classes(scr_do, doo, d)

    blk = pl.BlockSpec((tm, D_ATT), lambda i: (i, 0))
    out = pl.pallas_call(
        body, name="att_delta", grid=(S // tm,), in_specs=[blk, blk],
        out_specs=[blk] + [_class_spec(tm, d, D_ATT) for d in dils] * 2,
        out_shape=[_sds((S, D_ATT), F32)] + [_sds((d, S // d, D_ATT), F32) for d in dils]
        + [_sds((d, S // d, D_ATT), BF16) for d in dils],
        scratch_shapes=[_row_scratch(tm, D_ATT), _row_scratch(tm, D_ATT)],
        compiler_params=_params("parallel"))(dac, att)
    delta = [out[0]] + [o.reshape(S, D_ATT) for o in out[1:1 + n_d]]
    do = [None] + [o.reshape(S, D_ATT) for o in out[1 + n_d:]]
    return delta, do


def _att_bwd(qk, v_src, do_src, lg, delta, d, *, name):
    S = qk.shape[0]
    L = S // d
    tq = min(ATT_BLOCK, L)
    nb, per, nh = L // tq, tq // HALF, L // HALF
    n_blocks = d * nb
    win = tq + 2 * HALF
    lead = tq - HALF
    acc_rows = lead + win
    (v_arr, v_col), (do_arr, do_col) = v_src, do_src

    def body(q_ref, kp_ref, kc_ref, kn_ref, vp_ref, vc_ref, vn_ref, do_ref, lg_ref, dl_ref,
             dq_ref, dk_ref, dv_ref, acc_k, acc_v):
        b = pl.program_id(0)
        i = lax.rem(jnp.minimum(b, n_blocks - 1), nb)

        @pl.when(b == 0)
        def _():
            acc_k[...] = jnp.zeros_like(acc_k)
            acc_v[...] = jnp.zeros_like(acc_v)

        @pl.when(b < n_blocks)
        def _():
            valid = _band_mask(i, tq, L)
            q, do = q_ref[...] * ATT_SCALE, do_ref[...]
            kwin = jnp.concatenate([kp_ref[...], kc_ref[...], kn_ref[...]], axis=0)
            vwin = jnp.concatenate([vp_ref[...], vc_ref[...], vn_ref[...]], axis=0)
            first, first_w = _first_head(tq), _first_head(win)
            groups = list(_lane_groups())
            cols = [c for ls in groups for c in (ls.start, ls.start + HEAD_DIM)]
            lanes = [ls for ls in groups for _ in range(2)]
            qh = [t for ls in groups for t in _split_pair(q[:, ls], first)]
            doh = [t for ls in groups for t in _split_pair(do[:, ls], first)]
            s = [jnp.where(valid, _nt(t, kwin[:, ls]), NEG_INF) for t, ls in zip(qh, lanes)]
            dp = [_nt(t, vwin[:, ls]) for t, ls in zip(doh, lanes)]
            p = [jnp.exp(t - lg_ref[:, c:c + 1]) for t, c in zip(s, cols)]
            ds = [(pp * (t - dl_ref[:, c:c + 1])).astype(BF16) for pp, t, c in zip(p, dp, cols)]
            dq = [jnp.dot(t, kwin[:, ls], preferred_element_type=F32) for t, ls in zip(ds, lanes)]
            dk = [_tn(t, q[:, ls]) for t, ls in zip(ds, lanes)]
            dv = [_tn(pp.astype(BF16), do[:, ls]) for pp, ls in zip(p, lanes)]
            for g, ls in enumerate(groups):
                dq_ref[:, ls] = (jnp.where(first, dq[2 * g], dq[2 * g + 1]) * ATT_SCALE).astype(dq_ref.dtype)
                acc_k[lead:, ls] += jnp.where(first_w, dk[2 * g], dk[2 * g + 1])
                acc_v[lead:, ls] += jnp.where(first_w, dv[2 * g], dv[2 * g + 1])

        for acc, out in ((acc_k, dk_ref), (acc_v, dv_ref)):
            out[...] = acc[0:tq, :].astype(out.dtype)
            kept = acc[tq:, :]
            acc[0:acc_rows - tq, :] = kept
            acc[acc_rows - tq:, :] = jnp.zeros((tq, D_ATT), F32)

    def seq(col):
        blk = lambda b: jnp.minimum(b, n_blocks - 1)
        cls = lambda b: (blk(b) // nb) * nh
        centre = pl.BlockSpec((tq, D_ATT), lambda b: (blk(b), col))
        prev = pl.BlockSpec((HALF, D_ATT), lambda b: (cls(b) + jnp.maximum((blk(b) % nb) * per - 1, 0), col))
        nxt = pl.BlockSpec((HALF, D_ATT), lambda b: (cls(b) + jnp.minimum((blk(b) % nb + 1) * per, nh - 1), col))
        return prev, centre, nxt

    _, qc, _ = seq(0)
    kp, kc, kn = seq(1)
    vp, vc, vn = seq(v_col)
    _, doc, _ = seq(do_col)
    late = pl.BlockSpec((tq, D_ATT), lambda b: (jnp.maximum(b - 1, 0), 0))
    return pl.pallas_call(
        body, name=name, grid=(n_blocks + 1,),
        in_specs=[qc, kp, kc, kn, vp, vc, vn, doc, qc, qc], out_specs=[qc, late, late],
        out_shape=[_sds((S, D_ATT), BF16)] * 3,
        scratch_shapes=[pltpu.VMEM((acc_rows, D_ATT), F32), pltpu.VMEM((acc_rows, D_ATT), F32)],
        compiler_params=_params("arbitrary"))(qk, qk, qk, qk, v_arr, v_arr, v_arr, do_arr, lg, delta)


def _sigmoid(x):
    return 1.0 / (1.0 + jnp.exp(-x))


def _halo_specs(S, T, width, col):
    last = S // HALO - 1
    per = T // HALO
    centre = pl.BlockSpec((T, width), lambda i: (i, col))
    prev = pl.BlockSpec((HALO, width), lambda i: (jnp.maximum(i * per - 1, 0), col))
    nxt = pl.BlockSpec((HALO, width), lambda i: (jnp.minimum((i + 1) * per, last), col))
    return prev, centre, nxt


def _window_scratch(T, C):
    return pltpu.VMEM((8, T + 2 * HALO, C), F32)


def _fill_window(buf, prev, centre, nxt, T):
    buf[0, 0:HALO, :] = prev
    buf[0, HALO:HALO + T, :] = centre
    buf[0, HALO + T:, :] = nxt
    rows = T + 2 * HALO - 8
    for s in range(1, 8):
        buf[s, 0:rows, :] = buf[0, s:s + rows, :]


def _tap_reads(buf, first_off, step, r0, ls):
    by_slab = {}
    for k in range(CONV_WIDTH):
        off = first_off + step * k
        by_slab.setdefault(off % 8, []).append((k, off - off % 8))
    for s, taps in by_slab.items():
        lo = min(a for _, a in taps)
        hi = max(a for _, a in taps)
        rows = buf[s, pl.ds(lo + r0, CONV_ROWS + hi - lo), ls]
        for k, a in taps:
            yield k, rows[a - lo:a - lo + CONV_ROWS]


def _depthwise(buf, w_ref, out_ref, T, C, first_off, step):
    def row_tile(t, carry):
        r0 = pl.multiple_of(t * CONV_ROWS, CONV_ROWS)
        for c0 in range(0, C, LANES):
            ls = slice(c0, c0 + LANES)
            acc = jnp.zeros((CONV_ROWS, LANES), F32)
            for k, rows in _tap_reads(buf, first_off, step, r0, ls):
                acc = acc + rows * w_ref[k:k + 1, ls]
            out_ref[pl.ds(r0, CONV_ROWS), ls] = acc
        return carry

    lax.fori_loop(0, T // CONV_ROWS, row_tile, 0)


def _conv_fwd(y, conv_w32, conv_b, ln_g, ln_b, *, T=512):
    S = y.shape[0]
    T = min(T, S)
    nblk = S // T
    C = D_CONV

    def body(ap, ac, an, gp, gc, gn, w_ref, b_ref, lg_ref, lb_ref, cv_ref, u1_ref, buf):
        i = pl.program_id(0)

        def glu(a_ref, g_ref):
            return a_ref[...].astype(F32) * _sigmoid(g_ref[...].astype(F32))

        _fill_window(buf, jnp.where(i > 0, glu(ap, gp), 0.0), glu(ac, gc),
                     jnp.where(i < nblk - 1, glu(an, gn), 0.0), T)
        _depthwise(buf, w_ref, u1_ref, T, C, HALO - CONV_PAD, 1)
        u1 = u1_ref[...] + b_ref[...]
        u1_ref[...] = u1
        mu = jnp.mean(u1, axis=-1, keepdims=True)
        xc = u1 - mu
        rstd = lax.rsqrt(jnp.mean(xc * xc, axis=-1, keepdims=True) + EPS)
        u2 = xc * rstd * lg_ref[...] + lb_ref[...]
        cv_ref[...] = (u2 * _sigmoid(u2)).astype(cv_ref.dtype)

    ap, ac, an = _halo_specs(S, T, C, 3)
    gp, gc, gn = _halo_specs(S, T, C, 4)
    vec = pl.BlockSpec((1, C), lambda i: (0, 0))
    out = pl.BlockSpec((T, C), lambda i: (i, 0))
    return pl.pallas_call(
        body, name="conv_fwd", grid=(nblk,),
        in_specs=[ap, ac, an, gp, gc, gn, pl.BlockSpec((32, C), lambda i: (0, 0)), vec, vec, vec],
        out_specs=[out, out], out_shape=[_sds((S, C), BF16), _sds((S, C), F32)],
        scratch_shapes=[_window_scratch(T, C)],
        compiler_params=_params("parallel"))(y, y, y, y, y, y, conv_w32, conv_b, ln_g, ln_b)


def _conv_bwd(dac, u1, y, conv_w32, ln_g, ln_b, *, T=512):
    S = y.shape[0]
    T = min(T, S)
    nblk = S // T
    C = D_CONV

    def body(dp, dc, dn, up, uc, un, ap, ac, an, gp, gc, gn, w_ref, lg_ref, lb_ref,
             dag_ref, dw_ref, dsm_ref, bufd, bufu, du0_scr, dw_acc):
        i = pl.program_id(0)
        lg = lg_ref[...]

        def du1_of(dcv_ref, u1_ref):
            u1 = u1_ref[...]
            mu = jnp.mean(u1, axis=-1, keepdims=True)
            xc = u1 - mu
            rstd = lax.rsqrt(jnp.mean(xc * xc, axis=-1, keepdims=True) + EPS)
            xhat = xc * rstd
            u2 = xhat * lg + lb_ref[...]
            sg = _sigmoid(u2)
            du2 = dcv_ref[...].astype(F32) * (sg * (1.0 + u2 * (1.0 - sg)))
            dxh = du2 * lg
            du1 = rstd * (dxh - jnp.mean(dxh, axis=-1, keepdims=True)
                          - xhat * jnp.mean(dxh * xhat, axis=-1, keepdims=True))
            return du1, du2, xhat

        def glu(a_ref, g_ref):
            return a_ref[...].astype(F32) * _sigmoid(g_ref[...].astype(F32))

        @pl.when(i == 0)
        def _():
            dw_ref[...] = jnp.zeros_like(dw_ref)
            dsm_ref[...] = jnp.zeros_like(dsm_ref)

        du1_c, du2_c, xhat_c = du1_of(dc, uc)
        dsm_ref[0:1, :] += jnp.sum(du1_c, axis=0, keepdims=True)
        dsm_ref[1:2, :] += jnp.sum(du2_c * xhat_c, axis=0, keepdims=True)
        dsm_ref[2:3, :] += jnp.sum(du2_c, axis=0, keepdims=True)
        _fill_window(bufd, jnp.where(i > 0, du1_of(dp, up)[0], 0.0), du1_c,
                     jnp.where(i < nblk - 1, du1_of(dn, un)[0], 0.0), T)
        _fill_window(bufu, jnp.where(i > 0, glu(ap, gp), 0.0), glu(ac, gc),
                     jnp.where(i < nblk - 1, glu(an, gn), 0.0), T)

        _depthwise(bufd, w_ref, du0_scr, T, C, HALO + CONV_PAD, -1)
        dw_acc[...] = jnp.zeros_like(dw_acc)

        def dw_tile(t, carry):
            r0 = pl.multiple_of(t * CONV_ROWS, CONV_ROWS)
            for c0 in range(0, C, LANES):
                ls = slice(c0, c0 + LANES)
                d = bufd[0, pl.ds(HALO + r0, CONV_ROWS), ls]
                for k, rows in _tap_reads(bufu, HALO - CONV_PAD, 1, r0, ls):
                    prod = d * rows
                    part = prod[0:8]
                    for j in range(8, CONV_ROWS, 8):
                        part = part + prod[j:j + 8]
                    dw_acc[k, :, ls] += part
            return carry

        lax.fori_loop(0, T // CONV_ROWS, dw_tile, 0)
        for k in range(CONV_WIDTH):
            dw_ref[k:k + 1, :] += jnp.sum(dw_acc[k], axis=0, keepdims=True)
        du0 = du0_scr[...]
        a = ac[...].astype(F32)
        sg = _sigmoid(gc[...].astype(F32))
        dag_ref[:, 0:C] = (du0 * sg).astype(dag_ref.dtype)
        dag_ref[:, C:] = (du0 * a * sg * (1.0 - sg)).astype(dag_ref.dtype)

    dp, dc, dn = _halo_specs(S, T, C, 1)
    up, uc, un = _halo_specs(S, T, C, 0)
    ap, ac, an = _halo_specs(S, T, C, 3)
    gp, gc, gn = _halo_specs(S, T, C, 4)
    vec = pl.BlockSpec((1, C), lambda i: (0, 0))
    return pl.pallas_call(
        body, name="conv_bwd", grid=(nblk,),
        in_specs=[dp, dc, dn, up, uc, un, ap, ac, an, gp, gc, gn,
                  pl.BlockSpec((32, C), lambda i: (0, 0)), vec, vec],
        out_specs=[pl.BlockSpec((T, 2 * C), lambda i: (i, 0)), pl.BlockSpec((32, C), lambda i: (0, 0)),
                   pl.BlockSpec((8, C), lambda i: (0, 0))],
        out_shape=[_sds((S, 2 * C), BF16), _sds((32, C), F32), _sds((8, C), F32)],
        scratch_shapes=[_window_scratch(T, C), _window_scratch(T, C), pltpu.VMEM((T, C), F32),
                        pltpu.VMEM((CONV_WIDTH, 8, C), F32)],
        compiler_params=_params("arbitrary"))(dac, dac, dac, u1, u1, u1, y, y, y, y, y, y, conv_w32, ln_g, ln_b)


def _xatt_fwd(xq, xk, xv, *, tm=512):
    S = xq.shape[0]
    M = xk.shape[0]
    tm = min(tm, S)
    scale = XATT_HEAD_DIM ** -0.5

    def body(q_ref, k_ref, v_ref, o_ref):
        heads = [slice(h * XATT_HEAD_DIM, (h + 1) * XATT_HEAD_DIM) for h in range(XATT_HEADS)]
        s = [_nt(q_ref[:, sl], k_ref[:, sl]) * scale for sl in heads]
        e = [jnp.exp(t - jnp.max(t, axis=-1, keepdims=True)) for t in s]
        p = [t * (1.0 / jnp.sum(t, axis=-1, keepdims=True)) for t in e]
        for sl, t in zip(heads, p):
            o_ref[:, sl] = jnp.dot(t.astype(BF16), v_ref[:, sl], preferred_element_type=F32).astype(o_ref.dtype)

    row = pl.BlockSpec((tm, D_MODEL), lambda i: (i, 0))
    full = pl.BlockSpec((M, D_MODEL), lambda i: (0, 0))
    return pl.pallas_call(
        body, name="xatt_fwd", grid=(S // tm,), in_specs=[row, full, full], out_specs=row,
        out_shape=_sds((S, D_MODEL), BF16), compiler_params=_params("parallel"))(xq, xk, xv)


def _xatt_bwd(xq, xk, xv, dxo, *, tm=512):
    S = xq.shape[0]
    M = xk.shape[0]
    tm = min(tm, S)
    scale = XATT_HEAD_DIM ** -0.5

    def body(q_ref, k_ref, v_ref, do_ref, dq_ref, dk_ref, dv_ref):
        i = pl.program_id(0)

        @pl.when(i == 0)
        def _():
            dk_ref[...] = jnp.zeros_like(dk_ref)
            dv_ref[...] = jnp.zeros_like(dv_ref)

        heads = [slice(h * XATT_HEAD_DIM, (h + 1) * XATT_HEAD_DIM) for h in range(XATT_HEADS)]
        s = [_nt(q_ref[:, sl], k_ref[:, sl]) * scale for sl in heads]
        dp = [_nt(do_ref[:, sl], v_ref[:, sl]) for sl in heads]
        e = [jnp.exp(t - jnp.max(t, axis=-1, keepdims=True)) for t in s]
        p = [t * (1.0 / jnp.sum(t, axis=-1, keepdims=True)) for t in e]
        ds = [(pp * (t - jnp.sum(t * pp, axis=-1, keepdims=True))).astype(BF16) for pp, t in zip(p, dp)]
        for sl, pp, t in zip(heads, p, ds):
            dq_ref[:, sl] = (jnp.dot(t, k_ref[:, sl], preferred_element_type=F32) * scale).astype(dq_ref.dtype)
            dv_ref[:, sl] += _tn(pp.astype(BF16), do_ref[:, sl])
            dk_ref[:, sl] += _tn(t, q_ref[:, sl]) * scale

    row = pl.BlockSpec((tm, D_MODEL), lambda i: (i, 0))
    full = pl.BlockSpec((M, D_MODEL), lambda i: (0, 0))
    return pl.pallas_call(
        body, name="xatt_bwd", grid=(S // tm,), in_specs=[row, full, full, row], out_specs=[row, full, full],
        out_shape=[_sds((S, D_MODEL), BF16), _sds((M, D_MODEL), F32), _sds((M, D_MODEL), F32)],
        compiler_params=_params("arbitrary"))(xq, xk, xv, dxo)


def _row_tile(R):
    for t in (256, 128, 64, 32, 16, 8):
        if R % t == 0:
            return t
    return R


def _sum_partials(own, recv, me, *, name):
    _, R, C = own.shape
    t = _row_tile(R)

    def body(me_ref, own_ref, r_ref, o_ref):
        o_ref[...] = ((own_ref[...].astype(F32) + r_ref[0].astype(F32)) + r_ref[1].astype(F32)) + r_ref[2].astype(F32)

    return pl.pallas_call(
        body, name=name,
        grid_spec=pltpu.PrefetchScalarGridSpec(
            num_scalar_prefetch=1, grid=(R // t,),
            in_specs=[pl.BlockSpec((None, t, C), lambda i, me_ref: (me_ref[0], i, 0)),
                      pl.BlockSpec((3, t, C), lambda i, me_ref: (0, i, 0))],
            out_specs=pl.BlockSpec((t, C), lambda i, me_ref: (i, 0))),
        out_shape=_sds((R, C), F32), compiler_params=_params("parallel"))(me, own, recv)


def _adamw_math(w, g, m, v):
    m2 = ADAM_B1 * m + (1.0 - ADAM_B1) * g
    v2 = ADAM_B2 * v + (1.0 - ADAM_B2) * (g * g)
    m_hat = m2 / (1.0 - ADAM_B1 ** ADAM_STEP)
    v_hat = v2 / (1.0 - ADAM_B2 ** ADAM_STEP)
    delta = -ADAM_LR * (m_hat / (jnp.sqrt(v_hat) + ADAM_EPS) + ADAM_WD * w)
    return delta, m2, v2


def _adamw(parts, w, m, v, *, name):
    R, C = w.shape
    t = _row_tile(R)
    n = len(parts)

    def body(*refs):
        w_ref, m_ref, v_ref = refs[n:n + 3]
        g_ref, d_ref, m2_ref, v2_ref = refs[n + 3:]
        g = refs[0][...]
        for r in refs[1:n]:
            g = g + r[...]
        delta, m2, v2 = _adamw_math(w_ref[...], g, m_ref[...], v_ref[...])
        g_ref[...] = g
        d_ref[...] = delta
        m2_ref[...] = m2
        v2_ref[...] = v2

    blk = pl.BlockSpec((t, C), lambda i: (i, 0))
    return pl.pallas_call(
        body, name=name, grid=(R // t,), in_specs=[blk] * (n + 3), out_specs=[blk] * 4,
        out_shape=[_sds((R, C), F32)] * 4, compiler_params=_params("parallel"))(*parts, w, m, v)


def _adamw_small(gathered, chip, entries):
    _, R, C = gathered.shape
    n = len(entries)
    group = D_CONV // N_CHIPS

    def body(chip_ref, g_ref, *refs):
        ins, outs, tot_ref = refs[:3 * n], refs[3 * n:7 * n], refs[7 * n]
        tot = g_ref[0]
        for k in range(1, N_DEV):
            tot = tot + g_ref[k]
        tot_ref[...] = tot
        for e, ((kind, r), _, _, _) in enumerate(entries):
            if kind == "row":
                g = tot_ref[r:r + 1, :]
            elif kind == "gain":
                g = jnp.concatenate([tot_ref[r:r + 1, :], tot_ref[r + 1:r + 2, :]], axis=1)
            else:
                g = tot_ref[r:r + CONV_WIDTH, 0:group]
                for j in range(1, N_CHIPS):
                    g = jnp.where(chip_ref[0] == j, tot_ref[r:r + CONV_WIDTH, j * group:(j + 1) * group], g)
            delta, m2, v2 = _adamw_math(ins[3 * e][...], g, ins[3 * e + 1][...], ins[3 * e + 2][...])
            for o, val in zip(outs[4 * e:4 * e + 4], (g, delta, m2, v2)):
                o[...] = val

    whole = lambda a: pl.BlockSpec(a.shape, lambda i, c: (0,) * a.ndim)
    arrays = [a for _, w, m, v in entries for a in (w, m, v)]
    out_like = [w for _, w, _, _ in entries for _ in range(4)]
    tot_like = _sds((R, C), F32)
    out = pl.pallas_call(
        body, name="adamw_small",
        grid_spec=pltpu.PrefetchScalarGridSpec(
            num_scalar_prefetch=1, grid=(1,),
            in_specs=[whole(gathered)] + [whole(a) for a in arrays],
            out_specs=[whole(a) for a in out_like] + [whole(tot_like)]),
        out_shape=[_sds(a.shape, F32) for a in out_like] + [tot_like],
        compiler_params=_params("arbitrary"))(chip, gathered, *arrays)
    return out[-1], [tuple(out[4 * e:4 * e + 4]) for e in range(n)]


def _chip_peers():
    x, y = lax.axis_index("x"), lax.axis_index("y")
    return [(1 - x, y), (x, 1 - y), (1 - x, 1 - y)]


HBM_SPEC = pl.BlockSpec(memory_space=pltpu.HBM)
SEM_SPEC = pl.BlockSpec(memory_space=pltpu.SEMAPHORE)


def _exchange_start(mode, srcs, zones, *, name):
    n = len(srcs)

    def body(*refs):
        ins, lands = refs[:n], refs[n:2 * n]
        send_sems, recv_sems = refs[2 * n:3 * n], refs[3 * n:4 * n]
        token = refs[-1]
        c = lax.axis_index("c")
        mine = 2 * lax.axis_index("x") + lax.axis_index("y")
        for t in range(n):
            for k, (px, py) in enumerate(_chip_peers()):
                if mode == "gather":
                    s, d = ins[t], lands[t].at[mine]
                else:
                    s, d = ins[t].at[2 * px + py], lands[t].at[k]
                pltpu.make_async_remote_copy(src_ref=s, dst_ref=d, send_sem=send_sems[t], recv_sem=recv_sems[t],
                                             device_id=(px, py, c), device_id_type=MESH).start()
            if mode == "gather":
                pltpu.make_async_copy(ins[t], lands[t].at[mine], send_sems[t]).start()
        token[...] = jnp.zeros_like(token)

    hbm = lambda a: pltpu.with_memory_space_constraint(a, pltpu.HBM)
    out = pl.pallas_call(
        body, name=name,
        in_specs=[HBM_SPEC] * (2 * n),
        out_specs=[SEM_SPEC] * (2 * n) + [HBM_SPEC] * (2 * n) + [pl.BlockSpec(memory_space=pltpu.VMEM)],
        out_shape=[pltpu.SemaphoreType.DMA(())] * (2 * n)
        + [pltpu.HBM(a.shape, a.dtype) for a in list(srcs) + list(zones)] + [_sds((8, LANES), F32)],
        input_output_aliases={i: 2 * n + i for i in range(2 * n)},
        compiler_params=pltpu.CompilerParams(has_side_effects=pltpu.SideEffectType.DATAFLOW_SIDE_EFFECTING),
    )(*[hbm(a) for a in list(srcs) + list(zones)])
    return out[:n], out[n:2 * n], out[2 * n:3 * n], out[3 * n:4 * n], out[-1]


def _exchange_wait(mode, started, after, *, name):
    send_sems, recv_sems, srcs, zones, _ = started
    n = len(srcs)

    def body(*refs):
        lands = refs[n:2 * n]
        send_refs, recv_refs = refs[2 * n:3 * n], refs[3 * n:4 * n]
        me = (lax.axis_index("x"), lax.axis_index("y"), lax.axis_index("c"))
        for t in range(n):
            three = lands[t].at[pl.ds(0, N_CHIPS - 1)]
            sent = lands[t] if mode == "gather" else three
            pltpu.make_async_remote_copy(src_ref=sent, dst_ref=sent, send_sem=send_refs[t], recv_sem=recv_refs[t],
                                         device_id=me, device_id_type=MESH).wait_send()
            pltpu.make_async_remote_copy(src_ref=three, dst_ref=three, send_sem=send_refs[t], recv_sem=recv_refs[t],
                                         device_id=me, device_id_type=MESH).wait_recv()

    out = pl.pallas_call(
        body, name=name,
        in_specs=[HBM_SPEC] * (2 * n) + [SEM_SPEC] * (2 * n) + [pl.BlockSpec(memory_space=pl.ANY)],
        out_specs=[HBM_SPEC] * (2 * n),
        out_shape=[pltpu.HBM(a.shape, a.dtype) for a in list(srcs) + list(zones)],
        input_output_aliases={i: i for i in range(2 * n)},
        compiler_params=pltpu.CompilerParams(has_side_effects=pltpu.SideEffectType.DATAFLOW_SIDE_EFFECTING),
    )(*srcs, *zones, *send_sems, *recv_sems, after)
    return out[:n], out[n:]


def _allgather_small(small):
    def body(small_ref, gath_ref, send_sems, recv_sems, loc_sem):
        x, y, c = lax.axis_index("x"), lax.axis_index("y"), lax.axis_index("c")
        me = 4 * x + 2 * y + c
        flips = [(fx, fy, fc) for fx in (0, 1) for fy in (0, 1) for fc in (0, 1)][1:]

        def flipped(fx, fy, fc):
            return (1 - x if fx else x, 1 - y if fy else y, 1 - c if fc else c)

        loc = pltpu.make_async_copy(small_ref, gath_ref.at[me], loc_sem)
        loc.start()
        sends = []
        for j, flip in enumerate(flips):
            cp = pltpu.make_async_remote_copy(
                src_ref=small_ref, dst_ref=gath_ref.at[me], send_sem=send_sems.at[j], recv_sem=recv_sems.at[j],
                device_id=flipped(*flip), device_id_type=MESH)
            cp.start()
            sends.append(cp)
        for j, flip in enumerate(flips):
            px, py, pc = flipped(*flip)
            pltpu.make_async_remote_copy(
                src_ref=small_ref, dst_ref=gath_ref.at[4 * px + 2 * py + pc], send_sem=send_sems.at[j],
                recv_sem=recv_sems.at[j], device_id=(px, py, pc), device_id_type=MESH).wait_recv()
        for cp in sends:
            cp.wait_send()
        loc.wait()

    any_spec = pl.BlockSpec(memory_space=pl.ANY)
    return pl.pallas_call(
        body, name="allgather_small", in_specs=[any_spec], out_specs=any_spec,
        out_shape=_sds((N_DEV,) + small.shape, small.dtype),
        scratch_shapes=[pltpu.SemaphoreType.DMA((N_DEV - 1,)), pltpu.SemaphoreType.DMA((N_DEV - 1,)),
                        pltpu.SemaphoreType.DMA])(small)


def _swap_with_sibling(parts):
    n = len(parts)

    def body(*refs):
        ins, outs = refs[:n], refs[n:2 * n]
        send_sems, recv_sems = refs[2 * n:]
        sib = (lax.axis_index("x"), lax.axis_index("y"), 1 - lax.axis_index("c"))
        cps = []
        for t in range(n):
            cp = pltpu.make_async_remote_copy(
                src_ref=ins[t], dst_ref=outs[t], send_sem=send_sems.at[t], recv_sem=recv_sems.at[t],
                device_id=sib, device_id_type=MESH)
            cp.start()
            cps.append(cp)
        for cp in cps:
            cp.wait()

    any_spec = pl.BlockSpec(memory_space=pl.ANY)
    return pl.pallas_call(
        body, name="swap_with_sibling", in_specs=[any_spec] * n, out_specs=[any_spec] * n,
        out_shape=[_sds(p.shape, p.dtype) for p in parts],
        scratch_shapes=[pltpu.SemaphoreType.DMA((n,)), pltpu.SemaphoreType.DMA((n,))])(*parts)


BIG = ("w_in", "w_out", "w_xq", "w_xk", "w_xv", "w_xo", "w_up", "w_down")
COL_SHARDED = ("w_in", "w_up")


def _as_matrix(name, w4):
    if name in COL_SHARDED:
        return w4
    return w4.reshape(1, w4.shape[0] * w4.shape[1], w4.shape[2])


def _transposed(w3):
    nsh, K, n = w3.shape
    return jnp.swapaxes(w3, 1, 2).reshape(1, nsh * n, K)


def _shard_layout(name, g):
    if name in COL_SHARDED:
        return g
    return g.reshape(N_CHIPS, g.shape[0] * g.shape[1] // N_CHIPS, g.shape[2])


def _local_step(x, mem, target, vecs, comm):
    S = x.shape[0]
    tables = _rope_tables(S)

    xn = _rms_fwd(x, vecs["norm_mix_g"], name="rms_mix")
    w_in, conv_w32 = comm["first"](xn)
    y = _mm_nn(xn, w_in, name="mm_in", tm=2048, tn=640)
    qk, v_perm = _rope_fwd(y, tables)
    v_src = [(y, 2)] + [(v, 0) for v in v_perm[1:]]
    outs, lses = zip(*[_att_fwd(qk[p], v_src[p], d, name=f"att_fwd_d{d}") for p, d in enumerate(DILATIONS)])
    att, lg = _att_combine(outs, lses)
    cv, u1 = _conv_fwd(y, conv_w32, vecs["conv_b"], vecs["conv_ln_g"], vecs["conv_ln_b"])
    mix = jnp.concatenate([att, cv], axis=1)
    Wm = {k: _as_matrix(k, v) for k, v in comm["rest"](mix).items()}
    Wm["w_in"] = w_in
    h1 = _mm_nn(mix, Wm["w_out"], name="mm_out", out_dtype=F32, res=x)
    hn = _rms_fwd(h1, vecs["norm_x_g"], name="rms_x")
    xq = _mm_nn(hn, Wm["w_xq"], name="mm_xq")
    mn = _rms_fwd(mem, vecs["norm_mem_g"], name="rms_mem")
    xk = _mm_nn(mn, Wm["w_xk"], name="mm_xk")
    xv = _mm_nn(mn, Wm["w_xv"], name="mm_xv")
    xo = _xatt_fwd(xq, xk, xv)
    h2 = _mm_nn(xo, Wm["w_xo"], name="mm_xo", out_dtype=F32, res=h1)
    hm = _rms_fwd(h2, vecs["norm_mlp_g"], name="rms_mlp")
    relu_up = _mm_nn(hm, Wm["w_up"], name="mm_up", relu=True, tm=2048)
    h3 = _mm_nn(relu_up, Wm["w_down"], name="mm_down", out_dtype=F32, res=h2, a_squared=True, tm=512, tk=D_FF)

    dh3, dh3b, dg_final, loss = _loss_head(h3, vecs["norm_final_g"], target)
    g = {}
    g["w_down"] = _mm_tn(relu_up, dh3b, 1, name="dw_down", a_squared=True)
    dup = _mm_nt(dh3b, Wm["w_down"], name="d_act", out_dtype=BF16, mul=relu_up, tm=2048)
    g["w_up"] = _mm_tn(hm, dup, N_CHIPS, name="dw_up")
    sent = comm["send_mlp"]({k: _shard_layout(k, g[k]) for k in ("w_down", "w_up")})
    dhm = _mm_nn(dup, _transposed(Wm["w_up"]), name="d_hm", tm=512, tk=D_FF)
    dh2, dh2b, dg_mlp = _rms_bwd(dhm, h2, vecs["norm_mlp_g"] + sent[0:1, 0:1], dh3, name="rms_bwd_mlp")
    g["w_xo"] = _mm_tn(xo, dh2b, 1, name="dw_xo")
    dxo = _mm_nt(dh2b, Wm["w_xo"], name="d_xo", out_dtype=BF16)
    dxq, dxk, dxv = _xatt_bwd(xq, xk, xv, dxo)
    g["w_xq"] = _mm_tn(hn, dxq, 1, name="dw_xq")
    dhn = _mm_nt(dxq, Wm["w_xq"], name="d_hn", out_dtype=BF16)
    dh1, dh1b, dg_x = _rms_bwd(dhn, h1, vecs["norm_x_g"], dh2, name="rms_bwd_x")
    dxkb, dxvb = dxk.astype(BF16), dxv.astype(BF16)
    g["w_xk"] = _mm_tn(mn, dxkb, 1, name="dw_xk")
    g["w_xv"] = _mm_tn(mn, dxvb, 1, name="dw_xv")
    dmn = _mm_nt(jnp.concatenate([dxkb, dxvb], axis=1),
                 jnp.concatenate([Wm["w_xk"], Wm["w_xv"]], axis=2), name="d_mn", out_dtype=BF16)
    _, _, dg_mem = _rms_bwd(dmn, mem, vecs["norm_mem_g"], None, name="rms_bwd_mem", bf16_copy=False)
    g["w_out"] = _mm_tn(mix, dh1b, 1, name="dw_out")
    sent = comm["send_att"]({k: _shard_layout(k, g[k]) for k in ("w_out", "w_xq", "w_xk", "w_xv", "w_xo")})
    dac = _mm_nt(dh1b, Wm["w_out"], name="d_mix", out_dtype=BF16)
    dag, dconv_w, dconv_small = _conv_bwd(dac, u1, y, conv_w32, vecs["conv_ln_g"] + sent[0:1, 0:1],
                                          vecs["conv_ln_b"])
    delta, do_perm = _att_delta(dac, att)
    do_src = [(dac, 0)] + [(t, 0) for t in do_perm[1:]]
    dq, dk, dv = zip(*[_att_bwd(qk[p], v_src[p], do_src[p], lg[p], delta[p], d, name=f"att_bwd_d{d}")
                       for p, d in enumerate(DILATIONS)])
    dy = _assemble_dy(dq, dk, dv, dag, tables)
    sent = comm["send_in"]({"w_in": _mm_tn(xn, dy, N_CHIPS, name="dw_in", tn=640)})
    dxn = _mm_nn(dy, _transposed(Wm["w_in"]), name="d_xn", tm=512, tk=D_IN)
    grad_x, _, dg_mix = _rms_bwd(dxn, x, vecs["norm_mix_g"] + sent[0:1, 0:1], dh1, name="rms_bwd_mix",
                                 bf16_copy=False)

    small = dict(conv_w=dconv_w, conv_small=dconv_small, norm_mix_g=dg_mix, norm_x_g=dg_x, norm_mem_g=dg_mem,
                 norm_mlp_g=dg_mlp, norm_final_g=dg_final, loss=loss)
    return grad_x, small


SMALL_ORDER = ("conv_w", "conv_small", "norm_mix_g", "norm_x_g", "norm_mem_g", "norm_mlp_g", "norm_final_g", "loss")


def _pack_small(small):
    rows, offs, pos = [], {}, 0
    for k in SMALL_ORDER:
        a = small[k]
        a = a.reshape(a.shape[0] * a.shape[1] // SMALL_W, SMALL_W)
        pad = (-a.shape[0]) % 8
        if pad:
            a = jnp.pad(a, ((0, pad), (0, 0)))
        rows.append(a)
        offs[k] = pos
        pos += a.shape[0]
    return jnp.concatenate(rows, axis=0), offs


def kernel(x, mem, norm_mix_g, w_in, conv_w, conv_b, conv_ln_g, conv_ln_b, w_out, norm_x_g, norm_mem_g, w_xq, w_xk, w_xv, w_xo, norm_mlp_g, w_up, w_down, norm_final_g, loss_target, m_norm_mix_g, m_w_in, m_conv_w, m_conv_b, m_conv_ln_g, m_conv_ln_b, m_w_out, m_norm_x_g, m_norm_mem_g, m_w_xq, m_w_xk, m_w_xv, m_w_xo, m_norm_mlp_g, m_w_up, m_w_down, m_norm_final_g, v_norm_mix_g, v_w_in, v_conv_w, v_conv_b, v_conv_ln_g, v_conv_ln_b, v_w_out, v_norm_x_g, v_norm_mem_g, v_w_xq, v_w_xk, v_w_xv, v_w_xo, v_norm_mlp_g, v_w_up, v_w_down, v_norm_final_g):
    names = ("norm_mix_g", "w_in", "conv_w", "conv_b", "conv_ln_g", "conv_ln_b", "w_out", "norm_x_g", "norm_mem_g",
             "w_xq", "w_xk", "w_xv", "w_xo", "norm_mlp_g", "w_up", "w_down", "norm_final_g")
    wts = dict(zip(names, (norm_mix_g, w_in, conv_w, conv_b, conv_ln_g, conv_ln_b, w_out, norm_x_g, norm_mem_g,
                           w_xq, w_xk, w_xv, w_xo, norm_mlp_g, w_up, w_down, norm_final_g)))
    mom = dict(zip(names, (m_norm_mix_g, m_w_in, m_conv_w, m_conv_b, m_conv_ln_g, m_conv_ln_b, m_w_out, m_norm_x_g,
                           m_norm_mem_g, m_w_xq, m_w_xk, m_w_xv, m_w_xo, m_norm_mlp_g, m_w_up, m_w_down, m_norm_final_g)))
    var = dict(zip(names, (v_norm_mix_g, v_w_in, v_conv_w, v_conv_b, v_conv_ln_g, v_conv_ln_b, v_w_out, v_norm_x_g,
                           v_norm_mem_g, v_w_xq, v_w_xk, v_w_xv, v_w_xo, v_norm_mlp_g, v_w_up, v_w_down, v_norm_final_g)))
    chip = 2 * lax.axis_index("x") + lax.axis_index("y")

    def zone(shard):
        return lax.empty((N_CHIPS,) + shard.shape, shard.dtype)

    conv_w_pad = jnp.pad(wts["conv_w"][0], ((0, 1), (0, 0)))
    first_shards = [wts["w_in"][0].astype(BF16), conv_w_pad]
    gathering_first = _exchange_start("gather", first_shards, [zone(s) for s in first_shards],
                                      name="gather_first_start")
    rest = tuple(k for k in BIG if k != "w_in")
    behind_first = gathering_first[4][0, 0]
    rest_shards = [(wts[k][0] + behind_first).astype(BF16) for k in rest]
    gathering = _exchange_start("gather", rest_shards, [zone(s) for s in rest_shards], name="gather_rest_start")
    sending = {}

    def wait_first(after):
        _, (w_in_all, conv_w_all) = _exchange_wait("gather", gathering_first, after, name="gather_first_wait")
        return w_in_all, jnp.transpose(conv_w_all, (1, 0, 2)).reshape(32, D_CONV)

    def wait_rest(after):
        _, zones = _exchange_wait("gather", gathering, after, name="gather_rest_wait")
        return dict(zip(rest, zones))

    def send(group, grads):
        keys = tuple(grads)
        zones = [lax.empty((N_CHIPS - 1,) + grads[k].shape[1:], grads[k].dtype) for k in keys]
        sending[group] = (keys, _exchange_start("scatter", [grads[k] for k in keys], zones,
                                                name=f"scatter_{group}_start"))
        return sending[group][1][4]

    comm = dict(first=wait_first, rest=wait_rest, send_mlp=lambda grads: send("mlp", grads),
                send_att=lambda grads: send("att", grads), send_in=lambda grads: send("in", grads))
    vecs = {k: wts[k] for k in ("conv_b", "conv_ln_g", "conv_ln_b", "norm_x_g", "norm_mem_g", "norm_mlp_g")}
    vecs["norm_mix_g"] = wts["norm_mix_g"] + gathering[4][0:1, 0:1]
    vecs["norm_final_g"] = wts["norm_final_g"].reshape(1, D_MODEL)
    grad_x, small = _local_step(x[0], mem[0], loss_target[0], vecs, comm)

    packed, offs = _pack_small(small)
    gath = _allgather_small(packed)
    big, recv = {}, {}
    for group in ("mlp", "att", "in"):
        keys, started = sending[group]
        srcs, zones = _exchange_wait("scatter", started, gath, name=f"scatter_{group}_wait")
        big.update(zip(keys, srcs))
        recv.update(zip(keys, zones))
    me_arr = jnp.reshape(chip, (1,)).astype(jnp.int32)
    sums = [_sum_partials(big[k], recv[k], me_arr, name=f"sum_{k}") for k in BIG]
    sib = _swap_with_sibling(sums)

    res = {}
    for k, s_mine, s_sib in zip(BIG, sums, sib):
        res[k] = _adamw([s_mine, s_sib], wts[k][0], mom[k][0], var[k][0], name=f"adamw_{k}")

    where = {"conv_w": ("conv_w", offs["conv_w"]), "conv_b": ("row", offs["conv_small"]),
             "conv_ln_g": ("row", offs["conv_small"] + 1), "conv_ln_b": ("row", offs["conv_small"] + 2)}
    where.update({k: ("gain", offs[k]) for k in ("norm_mix_g", "norm_x_g", "norm_mem_g", "norm_mlp_g", "norm_final_g")})
    as_2d = lambda a: a.reshape(a.shape[-2] if a.ndim > 1 else 1, a.shape[-1])
    tot_small, updates = _adamw_small(gath, me_arr, [(where[k], as_2d(wts[k]), as_2d(mom[k]), as_2d(var[k]))
                                                     for k in where])
    res.update(zip(where, updates))
    loss = tot_small[offs["loss"], 0]

    outs = [loss, grad_x[None]]
    for j in range(4):
        outs += [res[k][j].reshape(wts[k].shape) for k in names]
    return tuple(outs)
```

```python
import jax
import jax.numpy as jnp
from jax import lax
from jax.experimental import pallas as pl
from jax.experimental.pallas import tpu as pltpu

F32 = jnp.float32
BF16 = jnp.bfloat16
MESH = pl.DeviceIdType.MESH

D_MODEL = 1024
ATT_HEADS = 8
HEAD_DIM = 64
D_ATT = ATT_HEADS * HEAD_DIM
D_CONV = D_MODEL - D_ATT
DILATIONS = (1, 4, 16)
HALF = 64
ROPE_THETA = 500000.0
ROT_DIM = HEAD_DIM // 4
CONV_WIDTH = 31
CONV_PAD = (CONV_WIDTH - 1) // 2
XATT_HEADS = 4
XATT_HEAD_DIM = D_MODEL // XATT_HEADS
D_FF = 4 * D_MODEL
D_IN = 3 * D_ATT + 2 * D_CONV
EPS = 1e-6
NEG_INF = -1e30
N_CHIPS = 4
N_DEV = 8

ADAM_LR = 0.001
ADAM_B1 = 0.9
ADAM_B2 = 0.999
ADAM_EPS = 1e-08
ADAM_WD = 0.01
ADAM_STEP = 10

VMEM_LIMIT_V7X = 56 * 1024 * 1024
LANES = 128
HALO = 16
CONV_ROWS = 64
ATT_BLOCK = 128
SMALL_W = 512


def _params(*sem):
    return pltpu.CompilerParams(dimension_semantics=sem, vmem_limit_bytes=VMEM_LIMIT_V7X)


def _sds(shape, dtype):
    return jax.ShapeDtypeStruct(shape, dtype)


def _squared(a):
    af = a.astype(F32)
    return (af * af).astype(BF16)


def _mm_nn(a, w3, *, name, out_dtype=BF16, res=None, relu=False, a_squared=False, tm=1024, tn=None, tk=1024):
    M, K = a.shape
    nsh, _, n = w3.shape
    tm, tk = min(tm, M), min(tk, K)
    tn = tn or min(n, 1024)
    npt, nk = n // tn, K // tk
    nj, N = nsh * npt, nsh * n
    n_out = 1

    def body(*refs):
        a_ref, w_ref = refs[0], refs[1]
        pos = 2
        res_ref = None
        if res is not None:
            res_ref = refs[pos]
            pos += 1
        outs = refs[pos:pos + n_out]
        acc_ref = refs[pos + n_out] if nk > 1 else None

        def finish(acc):
            if res_ref is not None:
                acc = acc + res_ref[...]
            if relu:
                acc = jnp.maximum(acc, 0.0)
            outs[0][...] = acc.astype(outs[0].dtype)

        a_val = _squared(a_ref[...]) if a_squared else a_ref[...]
        part = jnp.dot(a_val, w_ref[...], preferred_element_type=F32)
        if nk == 1:
            finish(part)
        else:
            k = pl.program_id(2)

            @pl.when(k == 0)
            def _():
                acc_ref[...] = part

            @pl.when(k > 0)
            def _():
                acc_ref[...] += part

            @pl.when(k == nk - 1)
            def _():
                finish(acc_ref[...])

    in_specs = [pl.BlockSpec((tm, tk), lambda i, j, k: (i, k)),
                pl.BlockSpec((None, tk, tn), lambda i, j, k: (j // npt, k, j % npt))]
    args = [a, w3]
    if res is not None:
        in_specs.append(pl.BlockSpec((tm, tn), lambda i, j, k: (i, j)))
        args.append(res)
    out_spec = pl.BlockSpec((tm, tn), lambda i, j, k: (i, j))
    out = pl.pallas_call(
        body, name=name, grid=(M // tm, nj, nk), in_specs=in_specs,
        out_specs=[out_spec] * n_out, out_shape=[_sds((M, N), out_dtype)] * n_out,
        scratch_shapes=[pltpu.VMEM((tm, tn), F32)] if nk > 1 else [],
        compiler_params=_params("parallel", "parallel", "arbitrary"))(*args)
    return out[0]


def _mm_nt(dy, w3, *, name, out_dtype=F32, mul=None, tm=1024, tn=None, tko=1024):
    M, N = dy.shape
    nsh, K, n = w3.shape
    tm, tko = min(tm, M), min(tko, K)
    tn = tn or min(n, 1024)
    npt = n // tn
    nj = nsh * npt

    def body(*refs):
        dy_ref, w_ref = refs[0], refs[1]
        pos = 2
        mul_ref = None
        if mul is not None:
            mul_ref = refs[pos]
            pos += 1
        out_ref = refs[pos]
        acc_ref = refs[pos + 1] if nj > 1 else None

        def finish(acc):
            if mul_ref is not None:
                acc = acc * (2.0 * mul_ref[...].astype(F32))
            out_ref[...] = acc.astype(out_ref.dtype)

        part = lax.dot_general(dy_ref[...], w_ref[...], (((1,), (1,)), ((), ())), preferred_element_type=F32)
        if nj == 1:
            finish(part)
        else:
            j = pl.program_id(2)

            @pl.when(j == 0)
            def _():
                acc_ref[...] = part

            @pl.when(j > 0)
            def _():
                acc_ref[...] += part

            @pl.when(j == nj - 1)
            def _():
                finish(acc_ref[...])

    in_specs = [pl.BlockSpec((tm, tn), lambda i, ko, j: (i, j)),
                pl.BlockSpec((None, tko, tn), lambda i, ko, j: (j // npt, ko, j % npt))]
    args = [dy, w3]
    if mul is not None:
        in_specs.append(pl.BlockSpec((tm, tko), lambda i, ko, j: (i, ko)))
        args.append(mul)
    return pl.pallas_call(
        body, name=name, grid=(M // tm, K // tko, nj), in_specs=in_specs,
        out_specs=pl.BlockSpec((tm, tko), lambda i, ko, j: (i, ko)), out_shape=_sds((M, K), out_dtype),
        scratch_shapes=[pltpu.VMEM((tm, tko), F32)] if nj > 1 else [],
        compiler_params=_params("parallel", "parallel", "arbitrary"))(*args)


def _mm_tn(a, dy, nsh, *, name, out_dtype=BF16, a_squared=False, tm=2048, tk=1024, tn=None):
    M, K = a.shape
    N = dy.shape[1]
    n = N // nsh
    tm, tk = min(tm, M), min(tk, K)
    tn = tn or min(n, 1024)
    npt = n // tn
    nj, nm = nsh * npt, M // tm

    def body(a_ref, dy_ref, out_ref, acc_ref):
        m = pl.program_id(2)
        a_val = _squared(a_ref[...]) if a_squared else a_ref[...]
        part = lax.dot_general(a_val, dy_ref[...], (((0,), (0,)), ((), ())), preferred_element_type=F32)

        @pl.when(m == 0)
        def _():
            acc_ref[...] = part

        @pl.when(m > 0)
        def _():
            acc_ref[...] += part

        @pl.when(m == nm - 1)
        def _():
            out_ref[...] = acc_ref[...].astype(out_ref.dtype)

    return pl.pallas_call(
        body, name=name, grid=(K // tk, nj, nm),
        in_specs=[pl.BlockSpec((tm, tk), lambda kk, j, m: (m, kk)),
                  pl.BlockSpec((tm, tn), lambda kk, j, m: (m, j))],
        out_specs=pl.BlockSpec((None, tk, tn), lambda kk, j, m: (j // npt, kk, j % npt)),
        out_shape=_sds((nsh, K, n), out_dtype),
        scratch_shapes=[pltpu.VMEM((tk, tn), F32)],
        compiler_params=_params("parallel", "parallel", "arbitrary"))(a, dy)


def _rms_fwd(x, g, *, name, tm=512):
    M, Dm = x.shape
    tm = min(tm, M)

    def body(x_ref, g_ref, o_ref):
        xf = x_ref[...]
        r = lax.rsqrt(jnp.mean(xf * xf, axis=-1, keepdims=True) + EPS)
        o_ref[...] = (xf * r * g_ref[...]).astype(o_ref.dtype)

    return pl.pallas_call(
        body, name=name, grid=(M // tm,),
        in_specs=[pl.BlockSpec((tm, Dm), lambda i: (i, 0)), pl.BlockSpec((1, Dm), lambda i: (0, 0))],
        out_specs=pl.BlockSpec((tm, Dm), lambda i: (i, 0)), out_shape=_sds((M, Dm), BF16),
        compiler_params=_params("parallel"))(x, g)


def _rms_bwd(dxn, x, g, dres, *, name, bf16_copy=True, tm=512):
    M, Dm = x.shape
    tm = min(tm, M)
    has_res = dres is not None

    def body(*refs):
        dxn_ref, x_ref, g_ref = refs[:3]
        dres_ref = refs[3] if has_res else None
        dx_ref, dg_ref = refs[-1 - 1 - bf16_copy], refs[-1]
        dxb_ref = refs[-2] if bf16_copy else None
        i = pl.program_id(0)
        xf = x_ref[...]
        r = lax.rsqrt(jnp.mean(xf * xf, axis=-1, keepdims=True) + EPS)
        nrm = xf * r
        dxn_f = dxn_ref[...].astype(F32)
        dn = dxn_f * g_ref[...]
        dx = r * (dn - nrm * jnp.mean(dn * nrm, axis=-1, keepdims=True))
        if has_res:
            dx = dx + dres_ref[...]
        dx_ref[...] = dx
        if bf16_copy:
            dxb_ref[...] = dx.astype(dxb_ref.dtype)

        @pl.when(i == 0)
        def _():
            dg_ref[...] = jnp.zeros_like(dg_ref)

        dg_ref[0:1, :] += jnp.sum(dxn_f * nrm, axis=0, keepdims=True)

    row = pl.BlockSpec((tm, Dm), lambda i: (i, 0))
    in_specs = [row, row, pl.BlockSpec((1, Dm), lambda i: (0, 0))] + ([row] if has_res else [])
    args = [dxn, x, g] + ([dres] if has_res else [])
    out = pl.pallas_call(
        body, name=name, grid=(M // tm,), in_specs=in_specs,
        out_specs=[row] * (1 + bf16_copy) + [pl.BlockSpec((8, Dm), lambda i: (0, 0))],
        out_shape=[_sds((M, Dm), F32)] + [_sds((M, Dm), BF16)] * bf16_copy + [_sds((8, Dm), F32)],
        compiler_params=_params("arbitrary"))(*args)
    return out[0], (out[1] if bf16_copy else None), out[-1]


def _mm_rows(a, w3, tail, *, name, rows_in=(), vecs_in=(), rows_out=(), sums_out=(), a_squared=False, tm=512):
    M, K = a.shape
    N = w3.shape[2]
    tm = min(tm, M)
    n_ri, n_vi, n_ro = len(rows_in), len(vecs_in), len(rows_out)

    def body(a_ref, w_ref, *refs):
        rin, vin = refs[:n_ri], refs[n_ri:n_ri + n_vi]
        rout, sout = refs[n_ri + n_vi:n_ri + n_vi + n_ro], refs[n_ri + n_vi + n_ro:]

        @pl.when(pl.program_id(0) == 0)
        def _():
            for s in sout:
                s[...] = jnp.zeros_like(s)

        a_val = _squared(a_ref[...]) if a_squared else a_ref[...]
        tail(jnp.dot(a_val, w_ref[0], preferred_element_type=F32), rin, vin, rout, sout)

    row = pl.BlockSpec((tm, N), lambda i: (i, 0))
    once = lambda shape: pl.BlockSpec(shape, lambda i: (0,) * len(shape))
    return pl.pallas_call(
        body, name=name, grid=(M // tm,),
        in_specs=[pl.BlockSpec((tm, K), lambda i: (i, 0)), once((1, K, N))] + [row] * n_ri + [once((1, N))] * n_vi,
        out_specs=[row] * n_ro + [once(s) for s in sums_out],
        out_shape=[_sds((M, N), dt) for dt in rows_out] + [_sds(s, F32) for s in sums_out],
        compiler_params=_params("arbitrary"))(a, w3, *rows_in, *vecs_in)


def _rms_bwd_tail(bf16_copy):
    def tail(dxn, rows_in, vecs_in, rows_out, sums_out):
        xf = rows_in[0][...]
        r = lax.rsqrt(jnp.mean(xf * xf, axis=-1, keepdims=True) + EPS)
        nrm = xf * r
        dn = dxn * vecs_in[0][...]
        dx = r * (dn - nrm * jnp.mean(dn * nrm, axis=-1, keepdims=True)) + rows_in[1][...]
        rows_out[0][...] = dx
        if bf16_copy:
            rows_out[1][...] = dx.astype(BF16)
        sums_out[0][0:1, :] += jnp.sum(dxn * nrm, axis=0, keepdims=True)

    return tail


def _loss_tail(prod, rows_in, vecs_in, rows_out, sums_out):
    hf = prod + rows_in[0][...]
    r = lax.rsqrt(jnp.mean(hf * hf, axis=-1, keepdims=True) + EPS)
    nrm = hf * r
    gv = vecs_in[0][...]
    err = nrm * gv - rows_in[1][...]
    dy = err * (1.0 / hf.shape[-1])
    dn = dy * gv
    dh = r * (dn - nrm * jnp.mean(dn * nrm, axis=-1, keepdims=True))
    rows_out[0][...] = dh
    rows_out[1][...] = dh.astype(BF16)
    sums_out[0][0:1, :] += jnp.sum(dy * nrm, axis=0, keepdims=True)
    part = 0.5 * jnp.sum(jnp.mean(err * err, axis=-1, keepdims=True), axis=0, keepdims=True)
    sel = (lax.broadcasted_iota(jnp.int32, (8, 128), 0) == 0) & (lax.broadcasted_iota(jnp.int32, (8, 128), 1) == 0)
    sums_out[1][...] += jnp.where(sel, part, 0.0)


def _class_spec(tm, d, width):
    return pl.BlockSpec((d, tm // d, width), lambda i: (0, i, 0))


def _row_scratch(tm, width):
    return pltpu.VMEM((width // LANES, tm, LANES), F32)


def _fill(scr, val):
    for c in range(scr.shape[0]):
        scr[c] = val[:, c * LANES:(c + 1) * LANES]


def _to_classes(scr, out_ref, d):
    n = scr.shape[1] // d
    for r in range(d):
        for c in range(scr.shape[0]):
            out_ref[r, :, c * LANES:(c + 1) * LANES] = scr[c, pl.ds(r, n, stride=d), :].astype(out_ref.dtype)


def _from_classes(in_ref, scr, d):
    n = scr.shape[1] // d
    for r in range(d):
        blk = in_ref[r].astype(F32)
        for c in range(scr.shape[0]):
            scr[c, pl.ds(r, n, stride=d), :] = blk[:, c * LANES:(c + 1) * LANES]
    return jnp.concatenate([scr[c] for c in range(scr.shape[0])], axis=1)


def _rope_tables(S):
    half = ROT_DIM // 2
    freqs = ROPE_THETA ** (-jnp.arange(0, ROT_DIM, 2, dtype=F32) / ROT_DIM)
    ang = jnp.arange(S, dtype=F32)[:, None] * freqs[None, :]
    cos, sin = jnp.cos(ang), jnp.sin(ang)
    ones = jnp.ones((S, HEAD_DIM - ROT_DIM), F32)
    zeros = jnp.zeros((S, HEAD_DIM - ROT_DIM), F32)
    zh = jnp.zeros((S, half), F32)
    c = jnp.concatenate([cos, cos, ones], axis=1)
    sa = jnp.concatenate([-sin, zh, zeros], axis=1)
    sb = jnp.concatenate([zh, sin, zeros], axis=1)
    return tuple(jnp.tile(t, (1, LANES // HEAD_DIM)) for t in (c, sa, sb))


def _rope_fwd(y, tables, *, tm=512):
    S = y.shape[0]
    W = 2 * D_ATT
    tm = min(tm, S)
    half = ROT_DIM // 2
    dils = [d for d in DILATIONS if d > 1]

    def body(y_ref, c_ref, sa_ref, sb_ref, qk_ref, *rest):
        qk_outs, v_outs = rest[:len(dils)], rest[len(dils):2 * len(dils)]
        scr_qk, scr_v = rest[2 * len(dils):]
        t = y_ref[:, 0:W].astype(F32)
        rep = W // LANES
        c, sa, sb = (jnp.tile(r[...], (1, rep)) for r in (c_ref, sa_ref, sb_ref))
        rot = t * c + pltpu.roll(t, W - half, axis=1) * sa + pltpu.roll(t, half, axis=1) * sb
        qk_ref[...] = rot.astype(qk_ref.dtype)
        _fill(scr_qk, rot)
        _fill(scr_v, y_ref[:, W:W + D_ATT].astype(F32))
        for d, qo, vo in zip(dils, qk_outs, v_outs):
            _to_classes(scr_qk, qo, d)
            _to_classes(scr_v, vo, d)

    tab = pl.BlockSpec((tm, LANES), lambda i: (i, 0))
    out = pl.pallas_call(
        body, name="rope_fwd", grid=(S // tm,),
        in_specs=[pl.BlockSpec((tm, 3 * D_ATT), lambda i: (i, 0)), tab, tab, tab],
        out_specs=[pl.BlockSpec((tm, W), lambda i: (i, 0))] + [_class_spec(tm, d, W) for d in dils]
        + [_class_spec(tm, d, D_ATT) for d in dils],
        out_shape=[_sds((S, W), BF16)] + [_sds((d, S // d, W), BF16) for d in dils]
        + [_sds((d, S // d, D_ATT), BF16) for d in dils],
        scratch_shapes=[_row_scratch(tm, W), _row_scratch(tm, D_ATT)],
        compiler_params=_params("parallel"))(y, *tables)
    qk = [out[0]] + [o.reshape(S, W) for o in out[1:1 + len(dils)]]
    v = [None] + [o.reshape(S, D_ATT) for o in out[1 + len(dils):]]
    return qk, v


def _assemble_dy(dq, dk, dv, dag, tables, *, tm=512):
    S = dag.shape[0]
    tm = min(tm, S)
    half = ROT_DIM // 2
    W = D_ATT
    n_pat = len(DILATIONS)

    def body(*refs):
        groups = [refs[g * n_pat:(g + 1) * n_pat] for g in range(3)]
        dag_ref, c_ref, sa_ref, sb_ref, o_ref, scr = refs[3 * n_pat:]
        rep = W // LANES
        c, sa, sb = (jnp.tile(r[...], (1, rep)) for r in (c_ref, sa_ref, sb_ref))

        def total(rs):
            acc = rs[0][...].astype(F32)
            for d, r in zip(DILATIONS[1:], rs[1:]):
                acc = acc + _from_classes(r, scr, d)
            return acc

        def unrope(dr):
            return dr * c + pltpu.roll(dr * sa, half, axis=1) + pltpu.roll(dr * sb, W - half, axis=1)

        o_ref[:, 0:W] = unrope(total(groups[0])).astype(o_ref.dtype)
        o_ref[:, W:2 * W] = unrope(total(groups[1])).astype(o_ref.dtype)
        o_ref[:, 2 * W:3 * W] = total(groups[2]).astype(o_ref.dtype)
        o_ref[:, 3 * W:] = dag_ref[...]

    specs = [pl.BlockSpec((tm, W), lambda i: (i, 0))] + [_class_spec(tm, d, W) for d in DILATIONS[1:]]
    tab = pl.BlockSpec((tm, LANES), lambda i: (i, 0))
    args = [a if d == 1 else a.reshape(d, S // d, W) for grp in (dq, dk, dv) for d, a in zip(DILATIONS, grp)]
    return pl.pallas_call(
        body, name="assemble_dy", grid=(S // tm,),
        in_specs=specs * 3 + [pl.BlockSpec((tm, 2 * D_CONV), lambda i: (i, 0)), tab, tab, tab],
        out_specs=pl.BlockSpec((tm, D_IN), lambda i: (i, 0)), out_shape=_sds((S, D_IN), BF16),
        scratch_shapes=[_row_scratch(tm, W)],
        compiler_params=_params("parallel"))(*args, dag, *tables)


def _seq_specs(L, tb, col):
    nb, per, nh = L // tb, tb // HALF, L // HALF
    centre = pl.BlockSpec((tb, D_ATT), lambda r, i: (r * nb + i, col))
    prev = pl.BlockSpec((HALF, D_ATT), lambda r, i: (r * nh + jnp.maximum(i * per - 1, 0), col))
    nxt = pl.BlockSpec((HALF, D_ATT), lambda r, i: (r * nh + jnp.minimum((i + 1) * per, nh - 1), col))
    return prev, centre, nxt


def _band_mask(i, tq, L):
    shape = (tq, tq + 2 * HALF)
    c_idx = lax.broadcasted_iota(jnp.int32, shape, 0)
    w_idx = lax.broadcasted_iota(jnp.int32, shape, 1)
    diff = w_idx - c_idx
    wpos = i * tq - HALF + w_idx
    return (diff >= 0) & (diff <= 2 * HALF) & (wpos >= 0) & (wpos < L)


def _lane_groups():
    for c0 in range(0, D_ATT, LANES):
        yield slice(c0, c0 + LANES)


def _first_head(rows):
    return lax.broadcasted_iota(jnp.int32, (rows, LANES), 1) < HEAD_DIM


def _split_pair(x, first):
    zero = jnp.zeros_like(x)
    return jnp.where(first, x, zero), jnp.where(first, zero, x)


def _nt(a, b):
    return lax.dot_general(a, b, (((1,), (1,)), ((), ())), preferred_element_type=F32)


def _tn(a, b):
    return lax.dot_general(a, b, (((0,), (0,)), ((), ())), preferred_element_type=F32)


ATT_SCALE = HEAD_DIM ** -0.5


def _att_fwd(qk, v_src, d, *, name):
    S = qk.shape[0]
    L = S // d
    tq = min(ATT_BLOCK, L)
    v_arr, v_col = v_src

    def body(q_ref, kp_ref, kc_ref, kn_ref, vp_ref, vc_ref, vn_ref, o_ref, lse_ref):
        i = pl.program_id(1)
        valid = _band_mask(i, tq, L)
        q = q_ref[...] * ATT_SCALE
        kwin = jnp.concatenate([kp_ref[...], kc_ref[...], kn_ref[...]], axis=0)
        vwin = jnp.concatenate([vp_ref[...], vc_ref[...], vn_ref[...]], axis=0)
        first = _first_head(tq)
        groups = list(_lane_groups())
        heads = [(ls, t) for ls in groups for t in _split_pair(q[:, ls], first)]
        s = [jnp.where(valid, _nt(t, kwin[:, ls]), NEG_INF) for ls, t in heads]
        m = [jnp.max(t, axis=-1, keepdims=True) for t in s]
        p = [jnp.exp(t - mm) for t, mm in zip(s, m)]
        den = [jnp.sum(t, axis=-1, keepdims=True) for t in p]
        o = [jnp.dot(t.astype(BF16), vwin[:, ls], preferred_element_type=F32) * (1.0 / dd)
             for t, dd, (ls, _) in zip(p, den, heads)]
        lse = [mm + jnp.log(dd) for mm, dd in zip(m, den)]
        for g, ls in enumerate(groups):
            o_ref[:, ls] = jnp.where(first, o[2 * g], o[2 * g + 1]).astype(o_ref.dtype)
            lse_ref[:, ls] = jnp.where(first, lse[2 * g], lse[2 * g + 1])

    _, qc, _ = _seq_specs(L, tq, 0)
    kp, kc, kn = _seq_specs(L, tq, 1)
    vp, vc, vn = _seq_specs(L, tq, v_col)
    out = pl.BlockSpec((tq, D_ATT), lambda r, i: (r * (L // tq) + i, 0))
    return pl.pallas_call(
        body, name=name, grid=(d, L // tq),
        in_specs=[qc, kp, kc, kn, vp, vc, vn], out_specs=[out, out],
        out_shape=[_sds((S, D_ATT), BF16), _sds((S, D_ATT), F32)],
        compiler_params=_params("parallel", "parallel"))(qk, qk, qk, qk, v_arr, v_arr, v_arr)


def _att_combine(outs, lses, *, tm=512):
    S = outs[0].shape[0]
    tm = min(tm, S)
    dils = DILATIONS[1:]
    n_d = len(dils)

    def body(*refs):
        o_refs, l_refs = refs[0:1 + n_d], refs[1 + n_d:2 + 2 * n_d]
        att_ref, lg_ref = refs[2 + 2 * n_d:4 + 2 * n_d]
        lg_outs = refs[4 + 2 * n_d:4 + 3 * n_d]
        scr = refs[4 + 3 * n_d:]
        scr_o, scr_l, scr_lg = scr[:n_d], scr[n_d:2 * n_d], scr[2 * n_d]
        ls = [l_refs[0][...]] + [_from_classes(r, s, d) for r, s, d in zip(l_refs[1:], scr_l, dils)]
        os_ = [o_refs[0][...].astype(F32)] + [_from_classes(r, s, d) for r, s, d in zip(o_refs[1:], scr_o, dils)]
        mx = ls[0]
        for l in ls[1:]:
            mx = jnp.maximum(mx, l)
        es = [jnp.exp(l - mx) for l in ls]
        tot = es[0]
        num = es[0] * os_[0]
        for e, o in zip(es[1:], os_[1:]):
            tot = tot + e
            num = num + e * o
        att_ref[...] = (num / tot).astype(att_ref.dtype)
        lg = mx + jnp.log(tot)
        lg_ref[...] = lg
        _fill(scr_lg, lg)
        for d, out in zip(dils, lg_outs):
            _to_classes(scr_lg, out, d)

    nat = pl.BlockSpec((tm, D_ATT), lambda i: (i, 0))
    specs = [nat] + [_class_spec(tm, d, D_ATT) for d in dils]
    view = lambda arrs: [arrs[0]] + [a.reshape(d, S // d, D_ATT) for a, d in zip(arrs[1:], dils)]
    out = pl.pallas_call(
        body, name="att_combine", grid=(S // tm,), in_specs=specs * 2,
        out_specs=[nat, nat] + specs[1:],
        out_shape=[_sds((S, D_ATT), BF16), _sds((S, D_ATT), F32)] + [_sds((d, S // d, D_ATT), F32) for d in dils],
        scratch_shapes=[_row_scratch(tm, D_ATT)] * (2 * n_d + 1),
        compiler_params=_params("parallel"))(*view(list(outs)), *view(list(lses)))
    return out[0], [out[1]] + [o.reshape(S, D_ATT) for o in out[2:]]


def _att_delta(dac, att, *, tm=512):
    S = att.shape[0]
    tm = min(tm, S)
    dils = DILATIONS[1:]
    n_d = len(dils)

    def body(do_ref, o_ref, dl_ref, *rest):
        dl_outs, do_outs = rest[:n_d], rest[n_d:2 * n_d]
        scr_dl, scr_do = rest[2 * n_d:]
        do = do_ref[...].astype(F32)
        prod = do * o_ref[...].astype(F32)
        per_head = [jnp.broadcast_to(jnp.sum(prod[:, h * HEAD_DIM:(h + 1) * HEAD_DIM], axis=-1, keepdims=True),
                                     (tm, HEAD_DIM)) for h in range(ATT_HEADS)]
        dl = jnp.concatenate(per_head, axis=1)
        dl_ref[...] = dl
        _fill(scr_dl, dl)
        _fill(scr_do, do)
        for d, dlo, doo in zip(dils, dl_outs, do_outs):
            _to_classes(scr_dl, dlo, d)
            _to_classes(scr_do, doo, d)

    blk = pl.BlockSpec((tm, D_ATT), lambda i: (i, 0))
    out = pl.pallas_call(
        body, name="att_delta", grid=(S // tm,), in_specs=[blk, blk],
        out_specs=[blk] + [_class_spec(tm, d, D_ATT) for d in dils] * 2,
        out_shape=[_sds((S, D_ATT), F32)] + [_sds((d, S // d, D_ATT), F32) for d in dils]
        + [_sds((d, S // d, D_ATT), BF16) for d in dils],
        scratch_shapes=[_row_scratch(tm, D_ATT), _row_scratch(tm, D_ATT)],
        compiler_params=_params("parallel"))(dac, att)
    delta = [out[0]] + [o.reshape(S, D_ATT) for o in out[1:1 + n_d]]
    do = [None] + [o.reshape(S, D_ATT) for o in out[1 + n_d:]]
    return delta, do


def _att_bwd(qk, v_src, do_src, lg, delta, d, *, name):
    S = qk.shape[0]
    L = S // d
    tq = min(ATT_BLOCK, L)
    nb, per, nh = L // tq, tq // HALF, L // HALF
    n_blocks = d * nb
    win = tq + 2 * HALF
    lead = tq - HALF
    acc_rows = lead + win
    (v_arr, v_col), (do_arr, do_col) = v_src, do_src

    def body(q_ref, kp_ref, kc_ref, kn_ref, vp_ref, vc_ref, vn_ref, do_ref, lg_ref, dl_ref,
             dq_ref, dk_ref, dv_ref, acc_k, acc_v):
        b = pl.program_id(0)
        i = lax.rem(jnp.minimum(b, n_blocks - 1), nb)

        @pl.when(b == 0)
        def _():
            acc_k[...] = jnp.zeros_like(acc_k)
            acc_v[...] = jnp.zeros_like(acc_v)

        @pl.when(b < n_blocks)
        def _():
            valid = _band_mask(i, tq, L)
            q, do = q_ref[...] * ATT_SCALE, do_ref[...]
            kwin = jnp.concatenate([kp_ref[...], kc_ref[...], kn_ref[...]], axis=0)
            vwin = jnp.concatenate([vp_ref[...], vc_ref[...], vn_ref[...]], axis=0)
            first, first_w = _first_head(tq), _first_head(win)
            groups = list(_lane_groups())
            cols = [c for ls in groups for c in (ls.start, ls.start + HEAD_DIM)]
            lanes = [ls for ls in groups for _ in range(2)]
            qh = [t for ls in groups for t in _split_pair(q[:, ls], first)]
            doh = [t for ls in groups for t in _split_pair(do[:, ls], first)]
            s = [jnp.where(valid, _nt(t, kwin[:, ls]), NEG_INF) for t, ls in zip(qh, lanes)]
            dp = [_nt(t, vwin[:, ls]) for t, ls in zip(doh, lanes)]
            p = [jnp.exp(t - lg_ref[:, c:c + 1]) for t, c in zip(s, cols)]
            ds = [(pp * (t - dl_ref[:, c:c + 1])).astype(BF16) for pp, t, c in zip(p, dp, cols)]
            dq = [jnp.dot(t, kwin[:, ls], preferred_element_type=F32) for t, ls in zip(ds, lanes)]
            dk = [_tn(t, q[:, ls]) for t, ls in zip(ds, lanes)]
            dv = [_tn(pp.astype(BF16), do[:, ls]) for pp, ls in zip(p, lanes)]
            for g, ls in enumerate(groups):
                dq_ref[:, ls] = (jnp.where(first, dq[2 * g], dq[2 * g + 1]) * ATT_SCALE).astype(dq_ref.dtype)
                acc_k[lead:, ls] += jnp.where(first_w, dk[2 * g], dk[2 * g + 1])
                acc_v[lead:, ls] += jnp.where(first_w, dv[2 * g], dv[2 * g + 1])

        for acc, out in ((acc_k, dk_ref), (acc_v, dv_ref)):
            out[...] = acc[0:tq, :].astype(out.dtype)
            kept = acc[tq:, :]
            acc[0:acc_rows - tq, :] = kept
            acc[acc_rows - tq:, :] = jnp.zeros((tq, D_ATT), F32)

    def seq(col):
        blk = lambda b: jnp.minimum(b, n_blocks - 1)
        cls = lambda b: (blk(b) // nb) * nh
        centre = pl.BlockSpec((tq, D_ATT), lambda b: (blk(b), col))
        prev = pl.BlockSpec((HALF, D_ATT), lambda b: (cls(b) + jnp.maximum((blk(b) % nb) * per - 1, 0), col))
        nxt = pl.BlockSpec((HALF, D_ATT), lambda b: (cls(b) + jnp.minimum((blk(b) % nb + 1) * per, nh - 1), col))
        return prev, centre, nxt

    _, qc, _ = seq(0)
    kp, kc, kn = seq(1)
    vp, vc, vn = seq(v_col)
    _, doc, _ = seq(do_col)
    late = pl.BlockSpec((tq, D_ATT), lambda b: (jnp.maximum(b - 1, 0), 0))
    return pl.pallas_call(
        body, name=name, grid=(n_blocks + 1,),
        in_specs=[qc, kp, kc, kn, vp, vc, vn, doc, qc, qc], out_specs=[qc, late, late],
        out_shape=[_sds((S, D_ATT), BF16)] * 3,
        scratch_shapes=[pltpu.VMEM((acc_rows, D_ATT), F32), pltpu.VMEM((acc_rows, D_ATT), F32)],
        compiler_params=_params("arbitrary"))(qk, qk, qk, qk, v_arr, v_arr, v_arr, do_arr, lg, delta)


def _sigmoid(x):
    return 1.0 / (1.0 + jnp.exp(-x))


def _halo_specs(S, T, width, col):
    last = S // HALO - 1
    per = T // HALO
    centre = pl.BlockSpec((T, width), lambda i: (i, col))
    prev = pl.BlockSpec((HALO, width), lambda i: (jnp.maximum(i * per - 1, 0), col))
    nxt = pl.BlockSpec((HALO, width), lambda i: (jnp.minimum((i + 1) * per, last), col))
    return prev, centre, nxt


def _window_scratch(T, C):
    return pltpu.VMEM((8, T + 2 * HALO, C), F32)


def _fill_window(buf, prev, centre, nxt, T):
    buf[0, 0:HALO, :] = prev
    buf[0, HALO:HALO + T, :] = centre
    buf[0, HALO + T:, :] = nxt
    rows = T + 2 * HALO - 8
    for s in range(1, 8):
        buf[s, 0:rows, :] = buf[0, s:s + rows, :]


def _tap_reads(buf, first_off, step, r0, ls):
    by_slab = {}
    for k in range(CONV_WIDTH):
        off = first_off + step * k
        by_slab.setdefault(off % 8, []).append((k, off - off % 8))
    for s, taps in by_slab.items():
        lo = min(a for _, a in taps)
        hi = max(a for _, a in taps)
        rows = buf[s, pl.ds(lo + r0, CONV_ROWS + hi - lo), ls]
        for k, a in taps:
            yield k, rows[a - lo:a - lo + CONV_ROWS]


def _depthwise(buf, w_ref, out_ref, T, C, first_off, step):
    def row_tile(t, carry):
        r0 = pl.multiple_of(t * CONV_ROWS, CONV_ROWS)
        for c0 in range(0, C, LANES):
            ls = slice(c0, c0 + LANES)
            acc = jnp.zeros((CONV_ROWS, LANES), F32)
            for k, rows in _tap_reads(buf, first_off, step, r0, ls):
                acc = acc + rows * w_ref[k:k + 1, ls]
            out_ref[pl.ds(r0, CONV_ROWS), ls] = acc
        return carry

    lax.fori_loop(0, T // CONV_ROWS, row_tile, 0)


def _conv_fwd(y, conv_w32, conv_b, ln_g, ln_b, *, T=512):
    S = y.shape[0]
    T = min(T, S)
    nblk = S // T
    C = D_CONV

    def body(ap, ac, an, gp, gc, gn, w_ref, b_ref, lg_ref, lb_ref, cv_ref, u1_ref, buf):
        i = pl.program_id(0)

        def glu(a_ref, g_ref):
            return a_ref[...].astype(F32) * _sigmoid(g_ref[...].astype(F32))

        _fill_window(buf, jnp.where(i > 0, glu(ap, gp), 0.0), glu(ac, gc),
                     jnp.where(i < nblk - 1, glu(an, gn), 0.0), T)
        _depthwise(buf, w_ref, u1_ref, T, C, HALO - CONV_PAD, 1)
        u1 = u1_ref[...] + b_ref[...]
        u1_ref[...] = u1
        mu = jnp.mean(u1, axis=-1, keepdims=True)
        xc = u1 - mu
        rstd = lax.rsqrt(jnp.mean(xc * xc, axis=-1, keepdims=True) + EPS)
        u2 = xc * rstd * lg_ref[...] + lb_ref[...]
        cv_ref[...] = (u2 * _sigmoid(u2)).astype(cv_ref.dtype)

    ap, ac, an = _halo_specs(S, T, C, 3)
    gp, gc, gn = _halo_specs(S, T, C, 4)
    vec = pl.BlockSpec((1, C), lambda i: (0, 0))
    out = pl.BlockSpec((T, C), lambda i: (i, 0))
    return pl.pallas_call(
        body, name="conv_fwd", grid=(nblk,),
        in_specs=[ap, ac, an, gp, gc, gn, pl.BlockSpec((32, C), lambda i: (0, 0)), vec, vec, vec],
        out_specs=[out, out], out_shape=[_sds((S, C), BF16), _sds((S, C), F32)],
        scratch_shapes=[_window_scratch(T, C)],
        compiler_params=_params("parallel"))(y, y, y, y, y, y, conv_w32, conv_b, ln_g, ln_b)


def _conv_bwd(dac, u1, y, conv_w32, ln_g, ln_b, *, T=512):
    S = y.shape[0]
    T = min(T, S)
    nblk = S // T
    C = D_CONV

    def body(dp, dc, dn, up, uc, un, ap, ac, an, gp, gc, gn, w_ref, lg_ref, lb_ref,
             dag_ref, dw_ref, dsm_ref, bufd, bufu, du0_scr, dw_acc):
        i = pl.program_id(0)
        lg = lg_ref[...]

        def du1_of(dcv_ref, u1_ref):
            u1 = u1_ref[...]
            mu = jnp.mean(u1, axis=-1, keepdims=True)
            xc = u1 - mu
            rstd = lax.rsqrt(jnp.mean(xc * xc, axis=-1, keepdims=True) + EPS)
            xhat = xc * rstd
            u2 = xhat * lg + lb_ref[...]
            sg = _sigmoid(u2)
            du2 = dcv_ref[...].astype(F32) * (sg * (1.0 + u2 * (1.0 - sg)))
            dxh = du2 * lg
            du1 = rstd * (dxh - jnp.mean(dxh, axis=-1, keepdims=True)
                          - xhat * jnp.mean(dxh * xhat, axis=-1, keepdims=True))
            return du1, du2, xhat

        def glu(a_ref, g_ref):
            return a_ref[...].astype(F32) * _sigmoid(g_ref[...].astype(F32))

        @pl.when(i == 0)
        def _():
            dw_ref[...] = jnp.zeros_like(dw_ref)
            dsm_ref[...] = jnp.zeros_like(dsm_ref)

        du1_c, du2_c, xhat_c = du1_of(dc, uc)
        dsm_ref[0:1, :] += jnp.sum(du1_c, axis=0, keepdims=True)
        dsm_ref[1:2, :] += jnp.sum(du2_c * xhat_c, axis=0, keepdims=True)
        dsm_ref[2:3, :] += jnp.sum(du2_c, axis=0, keepdims=True)
        _fill_window(bufd, jnp.where(i > 0, du1_of(dp, up)[0], 0.0), du1_c,
                     jnp.where(i < nblk - 1, du1_of(dn, un)[0], 0.0), T)
        _fill_window(bufu, jnp.where(i > 0, glu(ap, gp), 0.0), glu(ac, gc),
                     jnp.where(i < nblk - 1, glu(an, gn), 0.0), T)

        _depthwise(bufd, w_ref, du0_scr, T, C, HALO + CONV_PAD, -1)
        dw_acc[...] = jnp.zeros_like(dw_acc)

        def dw_tile(t, carry):
            r0 = pl.multiple_of(t * CONV_ROWS, CONV_ROWS)
            for c0 in range(0, C, LANES):
                ls = slice(c0, c0 + LANES)
                d = bufd[0, pl.ds(HALO + r0, CONV_ROWS), ls]
                for k, rows in _tap_reads(bufu, HALO - CONV_PAD, 1, r0, ls):
                    prod = d * rows
                    part = prod[0:8]
                    for j in range(8, CONV_ROWS, 8):
                        part = part + prod[j:j + 8]
                    dw_acc[k, :, ls] += part
            return carry

        lax.fori_loop(0, T // CONV_ROWS, dw_tile, 0)
        for k in range(CONV_WIDTH):
            dw_ref[k:k + 1, :] += jnp.sum(dw_acc[k], axis=0, keepdims=True)
        du0 = du0_scr[...]
        a = ac[...].astype(F32)
        sg = _sigmoid(gc[...].astype(F32))
        dag_ref[:, 0:C] = (du0 * sg).astype(dag_ref.dtype)
        dag_ref[:, C:] = (du0 * a * sg * (1.0 - sg)).astype(dag_ref.dtype)

    dp, dc, dn = _halo_specs(S, T, C, 1)
    up, uc, un = _halo_specs(S, T, C, 0)
    ap, ac, an = _halo_specs(S, T, C, 3)
    gp, gc, gn = _halo_specs(S, T, C, 4)
    vec = pl.BlockSpec((1, C), lambda i: (0, 0))
    return pl.pallas_call(
        body, name="conv_bwd", grid=(nblk,),
        in_specs=[dp, dc, dn, up, uc, un, ap, ac, an, gp, gc, gn,
                  pl.BlockSpec((32, C), lambda i: (0, 0)), vec, vec],
        out_specs=[pl.BlockSpec((T, 2 * C), lambda i: (i, 0)), pl.BlockSpec((32, C), lambda i: (0, 0)),
                   pl.BlockSpec((8, C), lambda i: (0, 0))],
        out_shape=[_sds((S, 2 * C), BF16), _sds((32, C), F32), _sds((8, C), F32)],
        scratch_shapes=[_window_scratch(T, C), _window_scratch(T, C), pltpu.VMEM((T, C), F32),
                        pltpu.VMEM((CONV_WIDTH, 8, C), F32)],
        compiler_params=_params("arbitrary"))(dac, dac, dac, u1, u1, u1, y, y, y, y, y, y, conv_w32, ln_g, ln_b)


def _xatt_fwd(xq, xk, xv, *, tm=512):
    S = xq.shape[0]
    M = xk.shape[0]
    tm = min(tm, S)
    scale = XATT_HEAD_DIM ** -0.5

    def body(q_ref, k_ref, v_ref, o_ref):
        heads = [slice(h * XATT_HEAD_DIM, (h + 1) * XATT_HEAD_DIM) for h in range(XATT_HEADS)]
        s = [_nt(q_ref[:, sl], k_ref[:, sl]) * scale for sl in heads]
        e = [jnp.exp(t - jnp.max(t, axis=-1, keepdims=True)) for t in s]
        p = [t * (1.0 / jnp.sum(t, axis=-1, keepdims=True)) for t in e]
        for sl, t in zip(heads, p):
            o_ref[:, sl] = jnp.dot(t.astype(BF16), v_ref[:, sl], preferred_element_type=F32).astype(o_ref.dtype)

    row = pl.BlockSpec((tm, D_MODEL), lambda i: (i, 0))
    full = pl.BlockSpec((M, D_MODEL), lambda i: (0, 0))
    return pl.pallas_call(
        body, name="xatt_fwd", grid=(S // tm,), in_specs=[row, full, full], out_specs=row,
        out_shape=_sds((S, D_MODEL), BF16), compiler_params=_params("parallel"))(xq, xk, xv)


def _xatt_bwd(xq, xk, xv, dxo, *, tm=512):
    S = xq.shape[0]
    M = xk.shape[0]
    tm = min(tm, S)
    scale = XATT_HEAD_DIM ** -0.5

    def body(q_ref, k_ref, v_ref, do_ref, dq_ref, dk_ref, dv_ref):
        i = pl.program_id(0)

        @pl.when(i == 0)
        def _():
            dk_ref[...] = jnp.zeros_like(dk_ref)
            dv_ref[...] = jnp.zeros_like(dv_ref)

        heads = [slice(h * XATT_HEAD_DIM, (h + 1) * XATT_HEAD_DIM) for h in range(XATT_HEADS)]
        s = [_nt(q_ref[:, sl], k_ref[:, sl]) * scale for sl in heads]
        dp = [_nt(do_ref[:, sl], v_ref[:, sl]) for sl in heads]
        e = [jnp.exp(t - jnp.max(t, axis=-1, keepdims=True)) for t in s]
        p = [t * (1.0 / jnp.sum(t, axis=-1, keepdims=True)) for t in e]
        ds = [(pp * (t - jnp.sum(t * pp, axis=-1, keepdims=True))).astype(BF16) for pp, t in zip(p, dp)]
        for sl, pp, t in zip(heads, p, ds):
            dq_ref[:, sl] = (jnp.dot(t, k_ref[:, sl], preferred_element_type=F32) * scale).astype(dq_ref.dtype)
            dv_ref[:, sl] += _tn(pp.astype(BF16), do_ref[:, sl])
            dk_ref[:, sl] += _tn(t, q_ref[:, sl]) * scale

    row = pl.BlockSpec((tm, D_MODEL), lambda i: (i, 0))
    full = pl.BlockSpec((M, D_MODEL), lambda i: (0, 0))
    return pl.pallas_call(
        body, name="xatt_bwd", grid=(S // tm,), in_specs=[row, full, full, row], out_specs=[row, full, full],
        out_shape=[_sds((S, D_MODEL), BF16), _sds((M, D_MODEL), F32), _sds((M, D_MODEL), F32)],
        compiler_params=_params("arbitrary"))(xq, xk, xv, dxo)


def _row_tile(R):
    for t in (256, 128, 64, 32, 16, 8):
        if R % t == 0:
            return t
    return R


def _sum_partials(own, recv, me, *, name):
    _, R, C = own.shape
    t = _row_tile(R)

    def body(me_ref, own_ref, r_ref, o_ref):
        o_ref[...] = ((own_ref[...].astype(F32) + r_ref[0].astype(F32)) + r_ref[1].astype(F32)) + r_ref[2].astype(F32)

    return pl.pallas_call(
        body, name=name,
        grid_spec=pltpu.PrefetchScalarGridSpec(
            num_scalar_prefetch=1, grid=(R // t,),
            in_specs=[pl.BlockSpec((None, t, C), lambda i, me_ref: (me_ref[0], i, 0)),
                      pl.BlockSpec((3, t, C), lambda i, me_ref: (0, i, 0))],
            out_specs=pl.BlockSpec((t, C), lambda i, me_ref: (i, 0))),
        out_shape=_sds((R, C), F32), compiler_params=_params("parallel"))(me, own, recv)


def _adamw_math(w, g, m, v):
    m2 = ADAM_B1 * m + (1.0 - ADAM_B1) * g
    v2 = ADAM_B2 * v + (1.0 - ADAM_B2) * (g * g)
    m_hat = m2 / (1.0 - ADAM_B1 ** ADAM_STEP)
    v_hat = v2 / (1.0 - ADAM_B2 ** ADAM_STEP)
    delta = -ADAM_LR * (m_hat / (jnp.sqrt(v_hat) + ADAM_EPS) + ADAM_WD * w)
    return delta, m2, v2


def _adamw(parts, w, m, v, *, name):
    R, C = w.shape
    t = _row_tile(R)
    n = len(parts)

    def body(*refs):
        w_ref, m_ref, v_ref = refs[n:n + 3]
        g_ref, d_ref, m2_ref, v2_ref = refs[n + 3:]
        g = refs[0][...]
        for r in refs[1:n]:
            g = g + r[...]
        delta, m2, v2 = _adamw_math(w_ref[...], g, m_ref[...], v_ref[...])
        g_ref[...] = g
        d_ref[...] = delta
        m2_ref[...] = m2
        v2_ref[...] = v2

    blk = pl.BlockSpec((t, C), lambda i: (i, 0))
    return pl.pallas_call(
        body, name=name, grid=(R // t,), in_specs=[blk] * (n + 3), out_specs=[blk] * 4,
        out_shape=[_sds((R, C), F32)] * 4, compiler_params=_params("parallel"))(*parts, w, m, v)


def _adamw_small(gathered, chip, entries):
    _, R, C = gathered.shape
    n = len(entries)
    group = D_CONV // N_CHIPS

    def body(chip_ref, g_ref, *refs):
        ins, outs, tot_ref = refs[:3 * n], refs[3 * n:7 * n], refs[7 * n]
        tot = g_ref[0]
        for k in range(1, N_DEV):
            tot = tot + g_ref[k]
        tot_ref[...] = tot
        for e, ((kind, r), _, _, _) in enumerate(entries):
            if kind == "row":
                g = tot_ref[r:r + 1, :]
            elif kind == "gain":
                g = jnp.concatenate([tot_ref[r:r + 1, :], tot_ref[r + 1:r + 2, :]], axis=1)
            else:
                g = tot_ref[r:r + CONV_WIDTH, 0:group]
                for j in range(1, N_CHIPS):
                    g = jnp.where(chip_ref[0] == j, tot_ref[r:r + CONV_WIDTH, j * group:(j + 1) * group], g)
            delta, m2, v2 = _adamw_math(ins[3 * e][...], g, ins[3 * e + 1][...], ins[3 * e + 2][...])
            for o, val in zip(outs[4 * e:4 * e + 4], (g, delta, m2, v2)):
                o[...] = val

    whole = lambda a: pl.BlockSpec(a.shape, lambda i, c: (0,) * a.ndim)
    arrays = [a for _, w, m, v in entries for a in (w, m, v)]
    out_like = [w for _, w, _, _ in entries for _ in range(4)]
    tot_like = _sds((R, C), F32)
    out = pl.pallas_call(
        body, name="adamw_small",
        grid_spec=pltpu.PrefetchScalarGridSpec(
            num_scalar_prefetch=1, grid=(1,),
            in_specs=[whole(gathered)] + [whole(a) for a in arrays],
            out_specs=[whole(a) for a in out_like] + [whole(tot_like)]),
        out_shape=[_sds(a.shape, F32) for a in out_like] + [tot_like],
        compiler_params=_params("arbitrary"))(chip, gathered, *arrays)
    return out[-1], [tuple(out[4 * e:4 * e + 4]) for e in range(n)]


def _chip_peers():
    x, y = lax.axis_index("x"), lax.axis_index("y")
    return [(1 - x, y), (x, 1 - y), (1 - x, 1 - y)]


HBM_SPEC = pl.BlockSpec(memory_space=pltpu.HBM)
SEM_SPEC = pl.BlockSpec(memory_space=pltpu.SEMAPHORE)


def _exchange_start(mode, srcs, zones, *, name):
    n = len(srcs)

    def body(*refs):
        ins, lands = refs[:n], refs[n:2 * n]
        send_sems, recv_sems = refs[2 * n:3 * n], refs[3 * n:4 * n]
        token = refs[-1]
        c = lax.axis_index("c")
        mine = 2 * lax.axis_index("x") + lax.axis_index("y")
        for t in range(n):
            for k, (px, py) in enumerate(_chip_peers()):
                if mode == "gather":
                    s, d = ins[t], lands[t].at[mine]
                else:
                    s, d = ins[t].at[2 * px + py], lands[t].at[k]
                pltpu.make_async_remote_copy(src_ref=s, dst_ref=d, send_sem=send_sems[t], recv_sem=recv_sems[t],
                                             device_id=(px, py, c), device_id_type=MESH).start()
            if mode == "gather":
                pltpu.make_async_copy(ins[t], lands[t].at[mine], send_sems[t]).start()
        token[...] = jnp.zeros_like(token)

    hbm = lambda a: pltpu.with_memory_space_constraint(a, pltpu.HBM)
    out = pl.pallas_call(
        body, name=name,
        in_specs=[HBM_SPEC] * (2 * n),
        out_specs=[SEM_SPEC] * (2 * n) + [HBM_SPEC] * (2 * n) + [pl.BlockSpec(memory_space=pltpu.VMEM)],
        out_shape=[pltpu.SemaphoreType.DMA(())] * (2 * n)
        + [pltpu.HBM(a.shape, a.dtype) for a in list(srcs) + list(zones)] + [_sds((8, LANES), F32)],
        input_output_aliases={i: 2 * n + i for i in range(2 * n)},
        compiler_params=pltpu.CompilerParams(has_side_effects=pltpu.SideEffectType.DATAFLOW_SIDE_EFFECTING),
    )(*[hbm(a) for a in list(srcs) + list(zones)])
    return out[:n], out[n:2 * n], out[2 * n:3 * n], out[3 * n:4 * n], out[-1]


def _exchange_wait(mode, started, after, *, name):
    send_sems, recv_sems, srcs, zones, _ = started
    n = len(srcs)

    def body(*refs):
        lands = refs[n:2 * n]
        send_refs, recv_refs = refs[2 * n:3 * n], refs[3 * n:4 * n]
        me = (lax.axis_index("x"), lax.axis_index("y"), lax.axis_index("c"))
        for t in range(n):
            three = lands[t].at[pl.ds(0, N_CHIPS - 1)]
            sent = lands[t] if mode == "gather" else three
            pltpu.make_async_remote_copy(src_ref=sent, dst_ref=sent, send_sem=send_refs[t], recv_sem=recv_refs[t],
                                         device_id=me, device_id_type=MESH).wait_send()
            pltpu.make_async_remote_copy(src_ref=three, dst_ref=three, send_sem=send_refs[t], recv_sem=recv_refs[t],
                                         device_id=me, device_id_type=MESH).wait_recv()

    out = pl.pallas_call(
        body, name=name,
        in_specs=[HBM_SPEC] * (2 * n) + [SEM_SPEC] * (2 * n) + [pl.BlockSpec(memory_space=pl.ANY)],
        out_specs=[HBM_SPEC] * (2 * n),
        out_shape=[pltpu.HBM(a.shape, a.dtype) for a in list(srcs) + list(zones)],
        input_output_aliases={i: i for i in range(2 * n)},
        compiler_params=pltpu.CompilerParams(has_side_effects=pltpu.SideEffectType.DATAFLOW_SIDE_EFFECTING),
    )(*srcs, *zones, *send_sems, *recv_sems, after)
    return out[:n], out[n:]


def _allgather_small(small):
    def body(small_ref, gath_ref, send_sems, recv_sems, loc_sem):
        x, y, c = lax.axis_index("x"), lax.axis_index("y"), lax.axis_index("c")
        me = 4 * x + 2 * y + c
        flips = [(fx, fy, fc) for fx in (0, 1) for fy in (0, 1) for fc in (0, 1)][1:]

        def flipped(fx, fy, fc):
            return (1 - x if fx else x, 1 - y if fy else y, 1 - c if fc else c)

        loc = pltpu.make_async_copy(small_ref, gath_ref.at[me], loc_sem)
        loc.start()
        sends = []
        for j, flip in enumerate(flips):
            cp = pltpu.make_async_remote_copy(
                src_ref=small_ref, dst_ref=gath_ref.at[me], send_sem=send_sems.at[j], recv_sem=recv_sems.at[j],
                device_id=flipped(*flip), device_id_type=MESH)
            cp.start()
            sends.append(cp)
        for j, flip in enumerate(flips):
            px, py, pc = flipped(*flip)
            pltpu.make_async_remote_copy(
                src_ref=small_ref, dst_ref=gath_ref.at[4 * px + 2 * py + pc], send_sem=send_sems.at[j],
                recv_sem=recv_sems.at[j], device_id=(px, py, pc), device_id_type=MESH).wait_recv()
        for cp in sends:
            cp.wait_send()
        loc.wait()

    any_spec = pl.BlockSpec(memory_space=pl.ANY)
    return pl.pallas_call(
        body, name="allgather_small", in_specs=[any_spec], out_specs=any_spec,
        out_shape=_sds((N_DEV,) + small.shape, small.dtype),
        scratch_shapes=[pltpu.SemaphoreType.DMA((N_DEV - 1,)), pltpu.SemaphoreType.DMA((N_DEV - 1,)),
                        pltpu.SemaphoreType.DMA])(small)


def _swap_with_sibling(parts):
    n = len(parts)

    def body(*refs):
        ins, outs = refs[:n], refs[n:2 * n]
        send_sems, recv_sems = refs[2 * n:]
        sib = (lax.axis_index("x"), lax.axis_index("y"), 1 - lax.axis_index("c"))
        cps = []
        for t in range(n):
            cp = pltpu.make_async_remote_copy(
                src_ref=ins[t], dst_ref=outs[t], send_sem=send_sems.at[t], recv_sem=recv_sems.at[t],
                device_id=sib, device_id_type=MESH)
            cp.start()
            cps.append(cp)
        for cp in cps:
            cp.wait()

    any_spec = pl.BlockSpec(memory_space=pl.ANY)
    return pl.pallas_call(
        body, name="swap_with_sibling", in_specs=[any_spec] * n, out_specs=[any_spec] * n,
        out_shape=[_sds(p.shape, p.dtype) for p in parts],
        scratch_shapes=[pltpu.SemaphoreType.DMA((n,)), pltpu.SemaphoreType.DMA((n,))])(*parts)


BIG = ("w_in", "w_out", "w_xq", "w_xk", "w_xv", "w_xo", "w_up", "w_down")
COL_SHARDED = ("w_in", "w_up")


def _as_matrix(name, w4):
    if name in COL_SHARDED:
        return w4
    return w4.reshape(1, w4.shape[0] * w4.shape[1], w4.shape[2])


def _transposed(w3):
    nsh, K, n = w3.shape
    return jnp.swapaxes(w3, 1, 2).reshape(1, nsh * n, K)


def _shard_layout(name, g):
    if name in COL_SHARDED:
        return g
    return g.reshape(N_CHIPS, g.shape[0] * g.shape[1] // N_CHIPS, g.shape[2])


def _local_step(x, mem, target, vecs, comm):
    S = x.shape[0]
    tables = _rope_tables(S)

    xn = _rms_fwd(x, vecs["norm_mix_g"], name="rms_mix")
    w_in, conv_w32 = comm["first"](xn)
    y = _mm_nn(xn, w_in, name="mm_in", tm=2048, tn=640)
    qk, v_perm = _rope_fwd(y, tables)
    v_src = [(y, 2)] + [(v, 0) for v in v_perm[1:]]
    outs, lses = zip(*[_att_fwd(qk[p], v_src[p], d, name=f"att_fwd_d{d}") for p, d in enumerate(DILATIONS)])
    att, lg = _att_combine(outs, lses)
    cv, u1 = _conv_fwd(y, conv_w32, vecs["conv_b"], vecs["conv_ln_g"], vecs["conv_ln_b"])
    mix = jnp.concatenate([att, cv], axis=1)
    Wm = {k: _as_matrix(k, v) for k, v in comm["rest"](mix).items()}
    Wm["w_in"] = w_in
    h1 = _mm_nn(mix, Wm["w_out"], name="mm_out", out_dtype=F32, res=x)
    hn = _rms_fwd(h1, vecs["norm_x_g"], name="rms_x")
    xq = _mm_nn(hn, Wm["w_xq"], name="mm_xq")
    mn = _rms_fwd(mem, vecs["norm_mem_g"], name="rms_mem")
    xk = _mm_nn(mn, Wm["w_xk"], name="mm_xk")
    xv = _mm_nn(mn, Wm["w_xv"], name="mm_xv")
    xo = _xatt_fwd(xq, xk, xv)
    h2 = _mm_nn(xo, Wm["w_xo"], name="mm_xo", out_dtype=F32, res=h1)
    hm = _rms_fwd(h2, vecs["norm_mlp_g"], name="rms_mlp")
    relu_up = _mm_nn(hm, Wm["w_up"], name="mm_up", relu=True, tm=2048)
    sums = ((8, D_MODEL),)
    dh3, dh3b, dg_final, loss = _mm_rows(
        relu_up, Wm["w_down"], _loss_tail, name="mm_down_loss", rows_in=(h2, target), vecs_in=(vecs["norm_final_g"],),
        rows_out=(F32, BF16), sums_out=sums + ((8, LANES),), a_squared=True, tm=256)
    g = {}
    g["w_down"] = _mm_tn(relu_up, dh3b, 1, name="dw_down", a_squared=True)
    dup = _mm_nt(dh3b, Wm["w_down"], name="d_act", out_dtype=BF16, mul=relu_up, tm=2048)
    g["w_up"] = _mm_tn(hm, dup, N_CHIPS, name="dw_up")
    sent = comm["send_mlp"]({k: _shard_layout(k, g[k]) for k in ("w_down", "w_up")})
    dh2, dh2b, dg_mlp = _mm_rows(
        dup, _transposed(Wm["w_up"]), _rms_bwd_tail(True), name="d_hm_rms", rows_in=(h2, dh3),
        vecs_in=(vecs["norm_mlp_g"] + sent[0:1, 0:1],), rows_out=(F32, BF16), sums_out=sums, tm=256)
    g["w_xo"] = _mm_tn(xo, dh2b, 1, name="dw_xo")
    dxo = _mm_nt(dh2b, Wm["w_xo"], name="d_xo", out_dtype=BF16)
    dxq, dxk, dxv = _xatt_bwd(xq, xk, xv, dxo)
    g["w_xq"] = _mm_tn(hn, dxq, 1, name="dw_xq")
    dh1, dh1b, dg_x = _mm_rows(
        dxq, _transposed(Wm["w_xq"]), _rms_bwd_tail(True), name="d_hn_rms", rows_in=(h1, dh2),
        vecs_in=(vecs["norm_x_g"],), rows_out=(F32, BF16), sums_out=sums)
    dxkb, dxvb = dxk.astype(BF16), dxv.astype(BF16)
    g["w_xk"] = _mm_tn(mn, dxkb, 1, name="dw_xk")
    g["w_xv"] = _mm_tn(mn, dxvb, 1, name="dw_xv")
    dmn = _mm_nt(jnp.concatenate([dxkb, dxvb], axis=1),
                 jnp.concatenate([Wm["w_xk"], Wm["w_xv"]], axis=2), name="d_mn", out_dtype=BF16)
    _, _, dg_mem = _rms_bwd(dmn, mem, vecs["norm_mem_g"], None, name="rms_bwd_mem", bf16_copy=False)
    g["w_out"] = _mm_tn(mix, dh1b, 1, name="dw_out")
    sent = comm["send_att"]({k: _shard_layout(k, g[k]) for k in ("w_out", "w_xq", "w_xk", "w_xv", "w_xo")})
    dac = _mm_nt(dh1b, Wm["w_out"], name="d_mix", out_dtype=BF16)
    dag, dconv_w, dconv_small = _conv_bwd(dac, u1, y, conv_w32, vecs["conv_ln_g"] + sent[0:1, 0:1],
                                          vecs["conv_ln_b"])
    delta, do_perm = _att_delta(dac, att)
    do_src = [(dac, 0)] + [(t, 0) for t in do_perm[1:]]
    dq, dk, dv = zip(*[_att_bwd(qk[p], v_src[p], do_src[p], lg[p], delta[p], d, name=f"att_bwd_d{d}")
                       for p, d in enumerate(DILATIONS)])
    dy = _assemble_dy(dq, dk, dv, dag, tables)
    sent = comm["send_in"]({"w_in": _mm_tn(xn, dy, N_CHIPS, name="dw_in", tn=640)})
    grad_x, dg_mix = _mm_rows(
        dy, _transposed(Wm["w_in"]), _rms_bwd_tail(False), name="d_xn_rms", rows_in=(x, dh1),
        vecs_in=(vecs["norm_mix_g"] + sent[0:1, 0:1],), rows_out=(F32,), sums_out=sums)

    small = dict(conv_w=dconv_w, conv_small=dconv_small, norm_mix_g=dg_mix, norm_x_g=dg_x, norm_mem_g=dg_mem,
                 norm_mlp_g=dg_mlp, norm_final_g=dg_final, loss=loss)
    return grad_x, small


SMALL_ORDER = ("conv_w", "conv_small", "norm_mix_g", "norm_x_g", "norm_mem_g", "norm_mlp_g", "norm_final_g", "loss")


def _pack_small(small):
    rows, offs, pos = [], {}, 0
    for k in SMALL_ORDER:
        a = small[k]
        a = a.reshape(a.shape[0] * a.shape[1] // SMALL_W, SMALL_W)
        pad = (-a.shape[0]) % 8
        if pad:
            a = jnp.pad(a, ((0, pad), (0, 0)))
        rows.append(a)
        offs[k] = pos
        pos += a.shape[0]
    return jnp.concatenate(rows, axis=0), offs


def kernel(x, mem, norm_mix_g, w_in, conv_w, conv_b, conv_ln_g, conv_ln_b, w_out, norm_x_g, norm_mem_g, w_xq, w_xk, w_xv, w_xo, norm_mlp_g, w_up, w_down, norm_final_g, loss_target, m_norm_mix_g, m_w_in, m_conv_w, m_conv_b, m_conv_ln_g, m_conv_ln_b, m_w_out, m_norm_x_g, m_norm_mem_g, m_w_xq, m_w_xk, m_w_xv, m_w_xo, m_norm_mlp_g, m_w_up, m_w_down, m_norm_final_g, v_norm_mix_g, v_w_in, v_conv_w, v_conv_b, v_conv_ln_g, v_conv_ln_b, v_w_out, v_norm_x_g, v_norm_mem_g, v_w_xq, v_w_xk, v_w_xv, v_w_xo, v_norm_mlp_g, v_w_up, v_w_down, v_norm_final_g):
    names = ("norm_mix_g", "w_in", "conv_w", "conv_b", "conv_ln_g", "conv_ln_b", "w_out", "norm_x_g", "norm_mem_g",
             "w_xq", "w_xk", "w_xv", "w_xo", "norm_mlp_g", "w_up", "w_down", "norm_final_g")
    wts = dict(zip(names, (norm_mix_g, w_in, conv_w, conv_b, conv_ln_g, conv_ln_b, w_out, norm_x_g, norm_mem_g,
                           w_xq, w_xk, w_xv, w_xo, norm_mlp_g, w_up, w_down, norm_final_g)))
    mom = dict(zip(names, (m_norm_mix_g, m_w_in, m_conv_w, m_conv_b, m_conv_ln_g, m_conv_ln_b, m_w_out, m_norm_x_g,
                           m_norm_mem_g, m_w_xq, m_w_xk, m_w_xv, m_w_xo, m_norm_mlp_g, m_w_up, m_w_down, m_norm_final_g)))
    var = dict(zip(names, (v_norm_mix_g, v_w_in, v_conv_w, v_conv_b, v_conv_ln_g, v_conv_ln_b, v_w_out, v_norm_x_g,
                           v_norm_mem_g, v_w_xq, v_w_xk, v_w_xv, v_w_xo, v_norm_mlp_g, v_w_up, v_w_down, v_norm_final_g)))
    chip = 2 * lax.axis_index("x") + lax.axis_index("y")

    def zone(shard):
        return lax.empty((N_CHIPS,) + shard.shape, shard.dtype)

    conv_w_pad = jnp.pad(wts["conv_w"][0], ((0, 1), (0, 0)))
    first_shards = [wts["w_in"][0].astype(BF16), conv_w_pad]
    gathering_first = _exchange_start("gather", first_shards, [zone(s) for s in first_shards],
                                      name="gather_first_start")
    rest = tuple(k for k in BIG if k != "w_in")
    behind_first = gathering_first[4][0, 0]
    rest_shards = [(wts[k][0] + behind_first).astype(BF16) for k in rest]
    gathering = _exchange_start("gather", rest_shards, [zone(s) for s in rest_shards], name="gather_rest_start")
    sending = {}

    def wait_first(after):
        _, (w_in_all, conv_w_all) = _exchange_wait("gather", gathering_first, after, name="gather_first_wait")
        return w_in_all, jnp.transpose(conv_w_all, (1, 0, 2)).reshape(32, D_CONV)

    def wait_rest(after):
        _, zones = _exchange_wait("gather", gathering, after, name="gather_rest_wait")
        return dict(zip(rest, zones))

    def send(group, grads):
        keys = tuple(grads)
        zones = [lax.empty((N_CHIPS - 1,) + grads[k].shape[1:], grads[k].dtype) for k in keys]
        sending[group] = (keys, _exchange_start("scatter", [grads[k] for k in keys], zones,
                                                name=f"scatter_{group}_start"))
        return sending[group][1][4]

    comm = dict(first=wait_first, rest=wait_rest, send_mlp=lambda grads: send("mlp", grads),
                send_att=lambda grads: send("att", grads), send_in=lambda grads: send("in", grads))
    vecs = {k: wts[k] for k in ("conv_b", "conv_ln_g", "conv_ln_b", "norm_x_g", "norm_mem_g", "norm_mlp_g")}
    vecs["norm_mix_g"] = wts["norm_mix_g"] + gathering[4][0:1, 0:1]
    vecs["norm_final_g"] = wts["norm_final_g"].reshape(1, D_MODEL)
    grad_x, small = _local_step(x[0], mem[0], loss_target[0], vecs, comm)

    packed, offs = _pack_small(small)
    gath = _allgather_small(packed)
    big, recv = {}, {}
    for group in ("mlp", "att", "in"):
        keys, started = sending[group]
        srcs, zones = _exchange_wait("scatter", started, gath, name=f"scatter_{group}_wait")
        big.update(zip(keys, srcs))
        recv.update(zip(keys, zones))
    me_arr = jnp.reshape(chip, (1,)).astype(jnp.int32)
    sums = [_sum_partials(big[k], recv[k], me_arr, name=f"sum_{k}") for k in BIG]
    sib = _swap_with_sibling(sums)

    res = {}
    for k, s_mine, s_sib in zip(BIG, sums, sib):
        res[k] = _adamw([s_mine, s_sib], wts[k][0], mom[k][0], var[k][0], name=f"adamw_{k}")

    where = {"conv_w": ("conv_w", offs["conv_w"]), "conv_b": ("row", offs["conv_small"]),
             "conv_ln_g": ("row", offs["conv_small"] + 1), "conv_ln_b": ("row", offs["conv_small"] + 2)}
    where.update({k: ("gain", offs[k]) for k in ("norm_mix_g", "norm_x_g", "norm_mem_g", "norm_mlp_g", "norm_final_g")})
    as_2d = lambda a: a.reshape(a.shape[-2] if a.ndim > 1 else 1, a.shape[-1])
    tot_small, updates = _adamw_small(gath, me_arr, [(where[k], as_2d(wts[k]), as_2d(mom[k]), as_2d(var[k]))
                                                     for k in where])
    res.update(zip(where, updates))
    loss = tot_small[offs["loss"], 0]

    outs = [loss, grad_x[None]]
    for j in range(4):
        outs += [res[k][j].reshape(wts[k].shape) for k in names]
    return tuple(outs)
```

```python
import jax
import jax.numpy as jnp
from jax import lax
from jax.experimental import pallas as pl
from jax.experimental.pallas import tpu as pltpu

F32 = jnp.float32
BF16 = jnp.bfloat16
MESH = pl.DeviceIdType.MESH

D_MODEL = 1024
ATT_HEADS = 8
HEAD_DIM = 64
D_ATT = ATT_HEADS * HEAD_DIM
D_CONV = D_MODEL - D_ATT
DILATIONS = (1, 4, 16)
HALF = 64
ROPE_THETA = 500000.0
ROT_DIM = HEAD_DIM // 4
CONV_WIDTH = 31
CONV_PAD = (CONV_WIDTH - 1) // 2
XATT_HEADS = 4
XATT_HEAD_DIM = D_MODEL // XATT_HEADS
D_FF = 4 * D_MODEL
D_IN = 3 * D_ATT + 2 * D_CONV
EPS = 1e-6
NEG_INF = -1e30
N_CHIPS = 4
N_DEV = 8

ADAM_LR = 0.001
ADAM_B1 = 0.9
ADAM_B2 = 0.999
ADAM_EPS = 1e-08
ADAM_WD = 0.01
ADAM_STEP = 10

VMEM_LIMIT_V7X = 56 * 1024 * 1024
LANES = 128
HALO = 16
CONV_ROWS = 64
ATT_BLOCK = 128
SMALL_W = 512


def _params(*sem):
    return pltpu.CompilerParams(dimension_semantics=sem, vmem_limit_bytes=VMEM_LIMIT_V7X)


def _sds(shape, dtype):
    return jax.ShapeDtypeStruct(shape, dtype)


def _squared(a):
    af = a.astype(F32)
    return (af * af).astype(BF16)


def _mm_nn(a, w3, *, name, out_dtype=BF16, res=None, relu=False, a_squared=False, tm=1024, tn=None, tk=1024):
    M, K = a.shape
    nsh, _, n = w3.shape
    tm, tk = min(tm, M), min(tk, K)
    tn = tn or min(n, 1024)
    npt, nk = n // tn, K // tk
    nj, N = nsh * npt, nsh * n
    n_out = 1

    def body(*refs):
        a_ref, w_ref = refs[0], refs[1]
        pos = 2
        res_ref = None
        if res is not None:
            res_ref = refs[pos]
            pos += 1
        outs = refs[pos:pos + n_out]
        acc_ref = refs[pos + n_out] if nk > 1 else None

        def finish(acc):
            if res_ref is not None:
                acc = acc + res_ref[...]
            if relu:
                acc = jnp.maximum(acc, 0.0)
            outs[0][...] = acc.astype(outs[0].dtype)

        a_val = _squared(a_ref[...]) if a_squared else a_ref[...]
        part = jnp.dot(a_val, w_ref[...], preferred_element_type=F32)
        if nk == 1:
            finish(part)
        else:
            k = pl.program_id(2)

            @pl.when(k == 0)
            def _():
                acc_ref[...] = part

            @pl.when(k > 0)
            def _():
                acc_ref[...] += part

            @pl.when(k == nk - 1)
            def _():
                finish(acc_ref[...])

    in_specs = [pl.BlockSpec((tm, tk), lambda i, j, k: (i, k)),
                pl.BlockSpec((None, tk, tn), lambda i, j, k: (j // npt, k, j % npt))]
    args = [a, w3]
    if res is not None:
        in_specs.append(pl.BlockSpec((tm, tn), lambda i, j, k: (i, j)))
        args.append(res)
    out_spec = pl.BlockSpec((tm, tn), lambda i, j, k: (i, j))
    out = pl.pallas_call(
        body, name=name, grid=(M // tm, nj, nk), in_specs=in_specs,
        out_specs=[out_spec] * n_out, out_shape=[_sds((M, N), out_dtype)] * n_out,
        scratch_shapes=[pltpu.VMEM((tm, tn), F32)] if nk > 1 else [],
        compiler_params=_params("parallel", "parallel", "arbitrary"))(*args)
    return out[0]


def _mm_nt(dy, w3, *, name, out_dtype=F32, mul=None, tm=1024, tn=None, tko=1024):
    M, N = dy.shape
    nsh, K, n = w3.shape
    tm, tko = min(tm, M), min(tko, K)
    tn = tn or min(n, 1024)
    npt = n // tn
    nj = nsh * npt

    def body(*refs):
        dy_ref, w_ref = refs[0], refs[1]
        pos = 2
        mul_ref = None
        if mul is not None:
            mul_ref = refs[pos]
            pos += 1
        out_ref = refs[pos]
        acc_ref = refs[pos + 1] if nj > 1 else None

        def finish(acc):
            if mul_ref is not None:
                acc = acc * (2.0 * mul_ref[...].astype(F32))
            out_ref[...] = acc.astype(out_ref.dtype)

        part = lax.dot_general(dy_ref[...], w_ref[...], (((1,), (1,)), ((), ())), preferred_element_type=F32)
        if nj == 1:
            finish(part)
        else:
            j = pl.program_id(2)

            @pl.when(j == 0)
            def _():
                acc_ref[...] = part

            @pl.when(j > 0)
            def _():
                acc_ref[...] += part

            @pl.when(j == nj - 1)
            def _():
                finish(acc_ref[...])

    in_specs = [pl.BlockSpec((tm, tn), lambda i, ko, j: (i, j)),
                pl.BlockSpec((None, tko, tn), lambda i, ko, j: (j // npt, ko, j % npt))]
    args = [dy, w3]
    if mul is not None:
        in_specs.append(pl.BlockSpec((tm, tko), lambda i, ko, j: (i, ko)))
        args.append(mul)
    return pl.pallas_call(
        body, name=name, grid=(M // tm, K // tko, nj), in_specs=in_specs,
        out_specs=pl.BlockSpec((tm, tko), lambda i, ko, j: (i, ko)), out_shape=_sds((M, K), out_dtype),
        scratch_shapes=[pltpu.VMEM((tm, tko), F32)] if nj > 1 else [],
        compiler_params=_params("parallel", "parallel", "arbitrary"))(*args)


def _mm_tn(a, dy, nsh, *, name, out_dtype=BF16, a_squared=False, tm=2048, tk=1024, tn=None):
    M, K = a.shape
    N = dy.shape[1]
    n = N // nsh
    tm, tk = min(tm, M), min(tk, K)
    tn = tn or min(n, 1024)
    npt = n // tn
    nj, nm = nsh * npt, M // tm

    def body(a_ref, dy_ref, out_ref, acc_ref):
        m = pl.program_id(2)
        a_val = _squared(a_ref[...]) if a_squared else a_ref[...]
        part = lax.dot_general(a_val, dy_ref[...], (((0,), (0,)), ((), ())), preferred_element_type=F32)

        @pl.when(m == 0)
        def _():
            acc_ref[...] = part

        @pl.when(m > 0)
        def _():
            acc_ref[...] += part

        @pl.when(m == nm - 1)
        def _():
            out_ref[...] = acc_ref[...].astype(out_ref.dtype)

    return pl.pallas_call(
        body, name=name, grid=(K // tk, nj, nm),
        in_specs=[pl.BlockSpec((tm, tk), lambda kk, j, m: (m, kk)),
                  pl.BlockSpec((tm, tn), lambda kk, j, m: (m, j))],
        out_specs=pl.BlockSpec((None, tk, tn), lambda kk, j, m: (j // npt, kk, j % npt)),
        out_shape=_sds((nsh, K, n), out_dtype),
        scratch_shapes=[pltpu.VMEM((tk, tn), F32)],
        compiler_params=_params("parallel", "parallel", "arbitrary"))(a, dy)


def _rms_fwd(x, g, *, name, tm=512):
    M, Dm = x.shape
    tm = min(tm, M)

    def body(x_ref, g_ref, o_ref):
        xf = x_ref[...]
        r = lax.rsqrt(jnp.mean(xf * xf, axis=-1, keepdims=True) + EPS)
        o_ref[...] = (xf * r * g_ref[...]).astype(o_ref.dtype)

    return pl.pallas_call(
        body, name=name, grid=(M // tm,),
        in_specs=[pl.BlockSpec((tm, Dm), lambda i: (i, 0)), pl.BlockSpec((1, Dm), lambda i: (0, 0))],
        out_specs=pl.BlockSpec((tm, Dm), lambda i: (i, 0)), out_shape=_sds((M, Dm), BF16),
        compiler_params=_params("parallel"))(x, g)


def _rms_bwd(dxn, x, g, dres, *, name, bf16_copy=True, tm=512):
    M, Dm = x.shape
    tm = min(tm, M)
    has_res = dres is not None

    def body(*refs):
        dxn_ref, x_ref, g_ref = refs[:3]
        dres_ref = refs[3] if has_res else None
        dx_ref, dg_ref = refs[-1 - 1 - bf16_copy], refs[-1]
        dxb_ref = refs[-2] if bf16_copy else None
        i = pl.program_id(0)
        xf = x_ref[...]
        r = lax.rsqrt(jnp.mean(xf * xf, axis=-1, keepdims=True) + EPS)
        nrm = xf * r
        dxn_f = dxn_ref[...].astype(F32)
        dn = dxn_f * g_ref[...]
        dx = r * (dn - nrm * jnp.mean(dn * nrm, axis=-1, keepdims=True))
        if has_res:
            dx = dx + dres_ref[...]
        dx_ref[...] = dx
        if bf16_copy:
            dxb_ref[...] = dx.astype(dxb_ref.dtype)

        @pl.when(i == 0)
        def _():
            dg_ref[...] = jnp.zeros_like(dg_ref)

        dg_ref[0:1, :] += jnp.sum(dxn_f * nrm, axis=0, keepdims=True)

    row = pl.BlockSpec((tm, Dm), lambda i: (i, 0))
    in_specs = [row, row, pl.BlockSpec((1, Dm), lambda i: (0, 0))] + ([row] if has_res else [])
    args = [dxn, x, g] + ([dres] if has_res else [])
    out = pl.pallas_call(
        body, name=name, grid=(M // tm,), in_specs=in_specs,
        out_specs=[row] * (1 + bf16_copy) + [pl.BlockSpec((8, Dm), lambda i: (0, 0))],
        out_shape=[_sds((M, Dm), F32)] + [_sds((M, Dm), BF16)] * bf16_copy + [_sds((8, Dm), F32)],
        compiler_params=_params("arbitrary"))(*args)
    return out[0], (out[1] if bf16_copy else None), out[-1]


def _mm_rows(a, w3, tail, *, name, rows_in=(), vecs_in=(), rows_out=(), sums_out=(), a_squared=False, tm=512):
    parts = a if isinstance(a, (tuple, list)) else (a,)
    M = parts[0].shape[0]
    K, N = w3.shape[1], w3.shape[2]
    tm = min(tm, M)
    n_a, n_ri, n_vi, n_ro = len(parts), len(rows_in), len(vecs_in), len(rows_out)

    def body(*refs):
        a_refs, w_ref, refs = refs[:n_a], refs[n_a], refs[n_a + 1:]
        rin, vin = refs[:n_ri], refs[n_ri:n_ri + n_vi]
        rout, sout = refs[n_ri + n_vi:n_ri + n_vi + n_ro], refs[n_ri + n_vi + n_ro:]

        @pl.when(pl.program_id(0) == 0)
        def _():
            for s in sout:
                s[...] = jnp.zeros_like(s)

        a_val = a_refs[0][...] if n_a == 1 else jnp.concatenate([r[...] for r in a_refs], axis=1)
        if a_squared:
            a_val = _squared(a_val)
        tail(jnp.dot(a_val, w_ref[0], preferred_element_type=F32), rin, vin, rout, sout)

    row = pl.BlockSpec((tm, N), lambda i: (i, 0))
    once = lambda shape: pl.BlockSpec(shape, lambda i: (0,) * len(shape))
    return pl.pallas_call(
        body, name=name, grid=(M // tm,),
        in_specs=[pl.BlockSpec((tm, p.shape[1]), lambda i: (i, 0)) for p in parts] + [once((1, K, N))]
        + [row] * n_ri + [once((1, N))] * n_vi,
        out_specs=[row] * n_ro + [once(s) for s in sums_out],
        out_shape=[_sds((M, N), dt) for dt in rows_out] + [_sds(s, F32) for s in sums_out],
        compiler_params=_params("arbitrary"))(*parts, w3, *rows_in, *vecs_in)


def _residual_norm_tail(prod, rows_in, vecs_in, rows_out, sums_out):
    hf = prod + rows_in[0][...]
    rows_out[0][...] = hf
    r = lax.rsqrt(jnp.mean(hf * hf, axis=-1, keepdims=True) + EPS)
    rows_out[1][...] = (hf * r * vecs_in[0][...]).astype(BF16)


def _rms_bwd_tail(bf16_copy):
    def tail(dxn, rows_in, vecs_in, rows_out, sums_out):
        xf = rows_in[0][...]
        r = lax.rsqrt(jnp.mean(xf * xf, axis=-1, keepdims=True) + EPS)
        nrm = xf * r
        dn = dxn * vecs_in[0][...]
        dx = r * (dn - nrm * jnp.mean(dn * nrm, axis=-1, keepdims=True)) + rows_in[1][...]
        rows_out[0][...] = dx
        if bf16_copy:
            rows_out[1][...] = dx.astype(BF16)
        sums_out[0][0:1, :] += jnp.sum(dxn * nrm, axis=0, keepdims=True)

    return tail


def _loss_tail(prod, rows_in, vecs_in, rows_out, sums_out):
    hf = prod + rows_in[0][...]
    r = lax.rsqrt(jnp.mean(hf * hf, axis=-1, keepdims=True) + EPS)
    nrm = hf * r
    gv = vecs_in[0][...]
    err = nrm * gv - rows_in[1][...]
    dy = err * (1.0 / hf.shape[-1])
    dn = dy * gv
    dh = r * (dn - nrm * jnp.mean(dn * nrm, axis=-1, keepdims=True))
    rows_out[0][...] = dh
    rows_out[1][...] = dh.astype(BF16)
    sums_out[0][0:1, :] += jnp.sum(dy * nrm, axis=0, keepdims=True)
    part = 0.5 * jnp.sum(jnp.mean(err * err, axis=-1, keepdims=True), axis=0, keepdims=True)
    sel = (lax.broadcasted_iota(jnp.int32, (8, 128), 0) == 0) & (lax.broadcasted_iota(jnp.int32, (8, 128), 1) == 0)
    sums_out[1][...] += jnp.where(sel, part, 0.0)


def _class_spec(tm, d, width):
    return pl.BlockSpec((d, tm // d, width), lambda i: (0, i, 0))


def _row_scratch(tm, width):
    return pltpu.VMEM((width // LANES, tm, LANES), F32)


def _fill(scr, val):
    for c in range(scr.shape[0]):
        scr[c] = val[:, c * LANES:(c + 1) * LANES]


def _to_classes(scr, out_ref, d):
    n = scr.shape[1] // d
    for r in range(d):
        for c in range(scr.shape[0]):
            out_ref[r, :, c * LANES:(c + 1) * LANES] = scr[c, pl.ds(r, n, stride=d), :].astype(out_ref.dtype)


def _from_classes(in_ref, scr, d):
    n = scr.shape[1] // d
    for r in range(d):
        blk = in_ref[r].astype(F32)
        for c in range(scr.shape[0]):
            scr[c, pl.ds(r, n, stride=d), :] = blk[:, c * LANES:(c + 1) * LANES]
    return jnp.concatenate([scr[c] for c in range(scr.shape[0])], axis=1)


def _rope_tables(S):
    half = ROT_DIM // 2
    freqs = ROPE_THETA ** (-jnp.arange(0, ROT_DIM, 2, dtype=F32) / ROT_DIM)
    ang = jnp.arange(S, dtype=F32)[:, None] * freqs[None, :]
    cos, sin = jnp.cos(ang), jnp.sin(ang)
    ones = jnp.ones((S, HEAD_DIM - ROT_DIM), F32)
    zeros = jnp.zeros((S, HEAD_DIM - ROT_DIM), F32)
    zh = jnp.zeros((S, half), F32)
    c = jnp.concatenate([cos, cos, ones], axis=1)
    sa = jnp.concatenate([-sin, zh, zeros], axis=1)
    sb = jnp.concatenate([zh, sin, zeros], axis=1)
    return tuple(jnp.tile(t, (1, LANES // HEAD_DIM)) for t in (c, sa, sb))


def _rope_fwd(y, tables, *, tm=512):
    S = y.shape[0]
    W = 2 * D_ATT
    tm = min(tm, S)
    half = ROT_DIM // 2
    dils = [d for d in DILATIONS if d > 1]

    def body(y_ref, c_ref, sa_ref, sb_ref, qk_ref, *rest):
        qk_outs, v_outs = rest[:len(dils)], rest[len(dils):2 * len(dils)]
        scr_qk, scr_v = rest[2 * len(dils):]
        t = y_ref[:, 0:W].astype(F32)
        rep = W // LANES
        c, sa, sb = (jnp.tile(r[...], (1, rep)) for r in (c_ref, sa_ref, sb_ref))
        rot = t * c + pltpu.roll(t, W - half, axis=1) * sa + pltpu.roll(t, half, axis=1) * sb
        qk_ref[...] = rot.astype(qk_ref.dtype)
        _fill(scr_qk, rot)
        _fill(scr_v, y_ref[:, W:W + D_ATT].astype(F32))
        for d, qo, vo in zip(dils, qk_outs, v_outs):
            _to_classes(scr_qk, qo, d)
            _to_classes(scr_v, vo, d)

    tab = pl.BlockSpec((tm, LANES), lambda i: (i, 0))
    out = pl.pallas_call(
        body, name="rope_fwd", grid=(S // tm,),
        in_specs=[pl.BlockSpec((tm, 3 * D_ATT), lambda i: (i, 0)), tab, tab, tab],
        out_specs=[pl.BlockSpec((tm, W), lambda i: (i, 0))] + [_class_spec(tm, d, W) for d in dils]
        + [_class_spec(tm, d, D_ATT) for d in dils],
        out_shape=[_sds((S, W), BF16)] + [_sds((d, S // d, W), BF16) for d in dils]
        + [_sds((d, S // d, D_ATT), BF16) for d in dils],
        scratch_shapes=[_row_scratch(tm, W), _row_scratch(tm, D_ATT)],
        compiler_params=_params("parallel"))(y, *tables)
    qk = [out[0]] + [o.reshape(S, W) for o in out[1:1 + len(dils)]]
    v = [None] + [o.reshape(S, D_ATT) for o in out[1 + len(dils):]]
    return qk, v


def _assemble_dy(dq, dk, dv, dag, tables, *, tm=512):
    S = dag.shape[0]
    tm = min(tm, S)
    half = ROT_DIM // 2
    W = D_ATT
    n_pat = len(DILATIONS)

    def body(*refs):
        groups = [refs[g * n_pat:(g + 1) * n_pat] for g in range(3)]
        dag_ref, c_ref, sa_ref, sb_ref, o_ref, scr = refs[3 * n_pat:]
        rep = W // LANES
        c, sa, sb = (jnp.tile(r[...], (1, rep)) for r in (c_ref, sa_ref, sb_ref))

        def total(rs):
            acc = rs[0][...].astype(F32)
            for d, r in zip(DILATIONS[1:], rs[1:]):
                acc = acc + _from_classes(r, scr, d)
            return acc

        def unrope(dr):
            return dr * c + pltpu.roll(dr * sa, half, axis=1) + pltpu.roll(dr * sb, W - half, axis=1)

        o_ref[:, 0:W] = unrope(total(groups[0])).astype(o_ref.dtype)
        o_ref[:, W:2 * W] = unrope(total(groups[1])).astype(o_ref.dtype)
        o_ref[:, 2 * W:3 * W] = total(groups[2]).astype(o_ref.dtype)
        o_ref[:, 3 * W:] = dag_ref[...]

    specs = [pl.BlockSpec((tm, W), lambda i: (i, 0))] + [_class_spec(tm, d, W) for d in DILATIONS[1:]]
    tab = pl.BlockSpec((tm, LANES), lambda i: (i, 0))
    args = [a if d == 1 else a.reshape(d, S // d, W) for grp in (dq, dk, dv) for d, a in zip(DILATIONS, grp)]
    return pl.pallas_call(
        body, name="assemble_dy", grid=(S // tm,),
        in_specs=specs * 3 + [pl.BlockSpec((tm, 2 * D_CONV), lambda i: (i, 0)), tab, tab, tab],
        out_specs=pl.BlockSpec((tm, D_IN), lambda i: (i, 0)), out_shape=_sds((S, D_IN), BF16),
        scratch_shapes=[_row_scratch(tm, W)],
        compiler_params=_params("parallel"))(*args, dag, *tables)


def _seq_specs(L, tb, col):
    nb, per, nh = L // tb, tb // HALF, L // HALF
    centre = pl.BlockSpec((tb, D_ATT), lambda r, i: (r * nb + i, col))
    prev = pl.BlockSpec((HALF, D_ATT), lambda r, i: (r * nh + jnp.maximum(i * per - 1, 0), col))
    nxt = pl.BlockSpec((HALF, D_ATT), lambda r, i: (r * nh + jnp.minimum((i + 1) * per, nh - 1), col))
    return prev, centre, nxt


def _band_mask(i, tq, L):
    shape = (tq, tq + 2 * HALF)
    c_idx = lax.broadcasted_iota(jnp.int32, shape, 0)
    w_idx = lax.broadcasted_iota(jnp.int32, shape, 1)
    diff = w_idx - c_idx
    wpos = i * tq - HALF + w_idx
    return (diff >= 0) & (diff <= 2 * HALF) & (wpos >= 0) & (wpos < L)


def _lane_groups():
    for c0 in range(0, D_ATT, LANES):
        yield slice(c0, c0 + LANES)


def _first_head(rows):
    return lax.broadcasted_iota(jnp.int32, (rows, LANES), 1) < HEAD_DIM


def _split_pair(x, first):
    zero = jnp.zeros_like(x)
    return jnp.where(first, x, zero), jnp.where(first, zero, x)


def _nt(a, b):
    return lax.dot_general(a, b, (((1,), (1,)), ((), ())), preferred_element_type=F32)


def _tn(a, b):
    return lax.dot_general(a, b, (((0,), (0,)), ((), ())), preferred_element_type=F32)


ATT_SCALE = HEAD_DIM ** -0.5


def _att_fwd(qk, v_src, d, *, name):
    S = qk.shape[0]
    L = S // d
    tq = min(ATT_BLOCK, L)
    v_arr, v_col = v_src

    def body(q_ref, kp_ref, kc_ref, kn_ref, vp_ref, vc_ref, vn_ref, o_ref, lse_ref):
        i = pl.program_id(1)
        valid = _band_mask(i, tq, L)
        q = q_ref[...] * ATT_SCALE
        kwin = jnp.concatenate([kp_ref[...], kc_ref[...], kn_ref[...]], axis=0)
        vwin = jnp.concatenate([vp_ref[...], vc_ref[...], vn_ref[...]], axis=0)
        first = _first_head(tq)
        groups = list(_lane_groups())
        heads = [(ls, t) for ls in groups for t in _split_pair(q[:, ls], first)]
        s = [jnp.where(valid, _nt(t, kwin[:, ls]), NEG_INF) for ls, t in heads]
        m = [jnp.max(t, axis=-1, keepdims=True) for t in s]
        p = [jnp.exp(t - mm) for t, mm in zip(s, m)]
        den = [jnp.sum(t, axis=-1, keepdims=True) for t in p]
        o = [jnp.dot(t.astype(BF16), vwin[:, ls], preferred_element_type=F32) * (1.0 / dd)
             for t, dd, (ls, _) in zip(p, den, heads)]
        lse = [mm + jnp.log(dd) for mm, dd in zip(m, den)]
        for g, ls in enumerate(groups):
            o_ref[:, ls] = jnp.where(first, o[2 * g], o[2 * g + 1]).astype(o_ref.dtype)
            lse_ref[:, ls] = jnp.where(first, lse[2 * g], lse[2 * g + 1])

    _, qc, _ = _seq_specs(L, tq, 0)
    kp, kc, kn = _seq_specs(L, tq, 1)
    vp, vc, vn = _seq_specs(L, tq, v_col)
    out = pl.BlockSpec((tq, D_ATT), lambda r, i: (r * (L // tq) + i, 0))
    return pl.pallas_call(
        body, name=name, grid=(d, L // tq),
        in_specs=[qc, kp, kc, kn, vp, vc, vn], out_specs=[out, out],
        out_shape=[_sds((S, D_ATT), BF16), _sds((S, D_ATT), F32)],
        compiler_params=_params("parallel", "parallel"))(qk, qk, qk, qk, v_arr, v_arr, v_arr)


def _att_combine(outs, lses, *, tm=512):
    S = outs[0].shape[0]
    tm = min(tm, S)
    dils = DILATIONS[1:]
    n_d = len(dils)

    def body(*refs):
        o_refs, l_refs = refs[0:1 + n_d], refs[1 + n_d:2 + 2 * n_d]
        att_ref, lg_ref = refs[2 + 2 * n_d:4 + 2 * n_d]
        lg_outs = refs[4 + 2 * n_d:4 + 3 * n_d]
        scr = refs[4 + 3 * n_d:]
        scr_o, scr_l, scr_lg = scr[:n_d], scr[n_d:2 * n_d], scr[2 * n_d]
        ls = [l_refs[0][...]] + [_from_classes(r, s, d) for r, s, d in zip(l_refs[1:], scr_l, dils)]
        os_ = [o_refs[0][...].astype(F32)] + [_from_classes(r, s, d) for r, s, d in zip(o_refs[1:], scr_o, dils)]
        mx = ls[0]
        for l in ls[1:]:
            mx = jnp.maximum(mx, l)
        es = [jnp.exp(l - mx) for l in ls]
        tot = es[0]
        num = es[0] * os_[0]
        for e, o in zip(es[1:], os_[1:]):
            tot = tot + e
            num = num + e * o
        att_ref[...] = (num / tot).astype(att_ref.dtype)
        lg = mx + jnp.log(tot)
        lg_ref[...] = lg
        _fill(scr_lg, lg)
        for d, out in zip(dils, lg_outs):
            _to_classes(scr_lg, out, d)

    nat = pl.BlockSpec((tm, D_ATT), lambda i: (i, 0))
    specs = [nat] + [_class_spec(tm, d, D_ATT) for d in dils]
    view = lambda arrs: [arrs[0]] + [a.reshape(d, S // d, D_ATT) for a, d in zip(arrs[1:], dils)]
    out = pl.pallas_call(
        body, name="att_combine", grid=(S // tm,), in_specs=specs * 2,
        out_specs=[nat, nat] + specs[1:],
        out_shape=[_sds((S, D_ATT), BF16), _sds((S, D_ATT), F32)] + [_sds((d, S // d, D_ATT), F32) for d in dils],
        scratch_shapes=[_row_scratch(tm, D_ATT)] * (2 * n_d + 1),
        compiler_params=_params("parallel"))(*view(list(outs)), *view(list(lses)))
    return out[0], [out[1]] + [o.reshape(S, D_ATT) for o in out[2:]]


def _att_delta(dac, att, *, tm=512):
    S = att.shape[0]
    tm = min(tm, S)
    dils = DILATIONS[1:]
    n_d = len(dils)

    def body(do_ref, o_ref, dl_ref, *rest):
        dl_outs, do_outs = rest[:n_d], rest[n_d:2 * n_d]
        scr_dl, scr_do = rest[2 * n_d:]
        do = do_ref[...].astype(F32)
        prod = do * o_ref[...].astype(F32)
        per_head = [jnp.broadcast_to(jnp.sum(prod[:, h * HEAD_DIM:(h + 1) * HEAD_DIM], axis=-1, keepdims=True),
                                     (tm, HEAD_DIM)) for h in range(ATT_HEADS)]
        dl = jnp.concatenate(per_head, axis=1)
        dl_ref[...] = dl
        _fill(scr_dl, dl)
        _fill(scr_do, do)
        for d, dlo, doo in zip(dils, dl_outs, do_outs):
            _to_classes(scr_dl, dlo, d)
            _to_classes(scr_do, doo, d)

    blk = pl.BlockSpec((tm, D_ATT), lambda i: (i, 0))
    out = pl.pallas_call(
        body, name="att_delta", grid=(S // tm,), in_specs=[blk, blk],
        out_specs=[blk] + [_class_spec(tm, d, D_ATT) for d in dils] * 2,
        out_shape=[_sds((S, D_ATT), F32)] + [_sds((d, S // d, D_ATT), F32) for d in dils]
        + [_sds((d, S // d, D_ATT), BF16) for d in dils],
        scratch_shapes=[_row_scratch(tm, D_ATT), _row_scratch(tm, D_ATT)],
        compiler_params=_params("parallel"))(dac, att)
    delta = [out[0]] + [o.reshape(S, D_ATT) for o in out[1:1 + n_d]]
    do = [None] + [o.reshape(S, D_ATT) for o in out[1 + n_d:]]
    return delta, do


def _att_bwd(qk, v_src, do_src, lg, delta, d, *, name):
    S = qk.shape[0]
    L = S // d
    tq = min(ATT_BLOCK, L)
    nb, per, nh = L // tq, tq // HALF, L // HALF
    n_blocks = d * nb
    win = tq + 2 * HALF
    lead = tq - HALF
    acc_rows = lead + win
    (v_arr, v_col), (do_arr, do_col) = v_src, do_src

    def body(q_ref, kp_ref, kc_ref, kn_ref, vp_ref, vc_ref, vn_ref, do_ref, lg_ref, dl_ref,
             dq_ref, dk_ref, dv_ref, acc_k, acc_v):
        b = pl.program_id(0)
        i = lax.rem(jnp.minimum(b, n_blocks - 1), nb)

        @pl.when(b == 0)
        def _():
            acc_k[...] = jnp.zeros_like(acc_k)
            acc_v[...] = jnp.zeros_like(acc_v)

        @pl.when(b < n_blocks)
        def _():
            valid = _band_mask(i, tq, L)
            q, do = q_ref[...] * ATT_SCALE, do_ref[...]
            kwin = jnp.concatenate([kp_ref[...], kc_ref[...], kn_ref[...]], axis=0)
            vwin = jnp.concatenate([vp_ref[...], vc_ref[...], vn_ref[...]], axis=0)
            first, first_w = _first_head(tq), _first_head(win)
            groups = list(_lane_groups())
            cols = [c for ls in groups for c in (ls.start, ls.start + HEAD_DIM)]
            lanes = [ls for ls in groups for _ in range(2)]
            qh = [t for ls in groups for t in _split_pair(q[:, ls], first)]
            doh = [t for ls in groups for t in _split_pair(do[:, ls], first)]
            s = [jnp.where(valid, _nt(t, kwin[:, ls]), NEG_INF) for t, ls in zip(qh, lanes)]
            dp = [_nt(t, vwin[:, ls]) for t, ls in zip(doh, lanes)]
            p = [jnp.exp(t - lg_ref[:, c:c + 1]) for t, c in zip(s, cols)]
            ds = [(pp * (t - dl_ref[:, c:c + 1])).astype(BF16) for pp, t, c in zip(p, dp, cols)]
            dq = [jnp.dot(t, kwin[:, ls], preferred_element_type=F32) for t, ls in zip(ds, lanes)]
            dk = [_tn(t, q[:, ls]) for t, ls in zip(ds, lanes)]
            dv = [_tn(pp.astype(BF16), do[:, ls]) for pp, ls in zip(p, lanes)]
            for g, ls in enumerate(groups):
                dq_ref[:, ls] = (jnp.where(first, dq[2 * g], dq[2 * g + 1]) * ATT_SCALE).astype(dq_ref.dtype)
                acc_k[lead:, ls] += jnp.where(first_w, dk[2 * g], dk[2 * g + 1])
                acc_v[lead:, ls] += jnp.where(first_w, dv[2 * g], dv[2 * g + 1])

        for acc, out in ((acc_k, dk_ref), (acc_v, dv_ref)):
            out[...] = acc[0:tq, :].astype(out.dtype)
            kept = acc[tq:, :]
            acc[0:acc_rows - tq, :] = kept
            acc[acc_rows - tq:, :] = jnp.zeros((tq, D_ATT), F32)

    def seq(col):
        blk = lambda b: jnp.minimum(b, n_blocks - 1)
        cls = lambda b: (blk(b) // nb) * nh
        centre = pl.BlockSpec((tq, D_ATT), lambda b: (blk(b), col))
        prev = pl.BlockSpec((HALF, D_ATT), lambda b: (cls(b) + jnp.maximum((blk(b) % nb) * per - 1, 0), col))
        nxt = pl.BlockSpec((HALF, D_ATT), lambda b: (cls(b) + jnp.minimum((blk(b) % nb + 1) * per, nh - 1), col))
        return prev, centre, nxt

    _, qc, _ = seq(0)
    kp, kc, kn = seq(1)
    vp, vc, vn = seq(v_col)
    _, doc, _ = seq(do_col)
    late = pl.BlockSpec((tq, D_ATT), lambda b: (jnp.maximum(b - 1, 0), 0))
    return pl.pallas_call(
        body, name=name, grid=(n_blocks + 1,),
        in_specs=[qc, kp, kc, kn, vp, vc, vn, doc, qc, qc], out_specs=[qc, late, late],
        out_shape=[_sds((S, D_ATT), BF16)] * 3,
        scratch_shapes=[pltpu.VMEM((acc_rows, D_ATT), F32), pltpu.VMEM((acc_rows, D_ATT), F32)],
        compiler_params=_params("arbitrary"))(qk, qk, qk, qk, v_arr, v_arr, v_arr, do_arr, lg, delta)


def _sigmoid(x):
    return 1.0 / (1.0 + jnp.exp(-x))


def _halo_specs(S, T, width, col):
    last = S // HALO - 1
    per = T // HALO
    centre = pl.BlockSpec((T, width), lambda i: (i, col))
    prev = pl.BlockSpec((HALO, width), lambda i: (jnp.maximum(i * per - 1, 0), col))
    nxt = pl.BlockSpec((HALO, width), lambda i: (jnp.minimum((i + 1) * per, last), col))
    return prev, centre, nxt


def _window_scratch(T, C):
    return pltpu.VMEM((8, T + 2 * HALO, C), F32)


def _fill_window(buf, prev, centre, nxt, T):
    buf[0, 0:HALO, :] = prev
    buf[0, HALO:HALO + T, :] = centre
    buf[0, HALO + T:, :] = nxt
    rows = T + 2 * HALO - 8
    for s in range(1, 8):
        buf[s, 0:rows, :] = buf[0, s:s + rows, :]


def _tap_reads(buf, first_off, step, r0, ls):
    by_slab = {}
    for k in range(CONV_WIDTH):
        off = first_off + step * k
        by_slab.setdefault(off % 8, []).append((k, off - off % 8))
    for s, taps in by_slab.items():
        lo = min(a for _, a in taps)
        hi = max(a for _, a in taps)
        rows = buf[s, pl.ds(lo + r0, CONV_ROWS + hi - lo), ls]
        for k, a in taps:
            yield k, rows[a - lo:a - lo + CONV_ROWS]


def _depthwise(buf, w_ref, out_ref, T, C, first_off, step):
    def row_tile(t, carry):
        r0 = pl.multiple_of(t * CONV_ROWS, CONV_ROWS)
        for c0 in range(0, C, LANES):
            ls = slice(c0, c0 + LANES)
            acc = jnp.zeros((CONV_ROWS, LANES), F32)
            for k, rows in _tap_reads(buf, first_off, step, r0, ls):
                acc = acc + rows * w_ref[k:k + 1, ls]
            out_ref[pl.ds(r0, CONV_ROWS), ls] = acc
        return carry

    lax.fori_loop(0, T // CONV_ROWS, row_tile, 0)


def _conv_fwd(y, conv_w32, conv_b, ln_g, ln_b, *, T=512):
    S = y.shape[0]
    T = min(T, S)
    nblk = S // T
    C = D_CONV

    def body(ap, ac, an, gp, gc, gn, w_ref, b_ref, lg_ref, lb_ref, cv_ref, u1_ref, buf):
        i = pl.program_id(0)

        def glu(a_ref, g_ref):
            return a_ref[...].astype(F32) * _sigmoid(g_ref[...].astype(F32))

        _fill_window(buf, jnp.where(i > 0, glu(ap, gp), 0.0), glu(ac, gc),
                     jnp.where(i < nblk - 1, glu(an, gn), 0.0), T)
        _depthwise(buf, w_ref, u1_ref, T, C, HALO - CONV_PAD, 1)
        u1 = u1_ref[...] + b_ref[...]
        u1_ref[...] = u1
        mu = jnp.mean(u1, axis=-1, keepdims=True)
        xc = u1 - mu
        rstd = lax.rsqrt(jnp.mean(xc * xc, axis=-1, keepdims=True) + EPS)
        u2 = xc * rstd * lg_ref[...] + lb_ref[...]
        cv_ref[...] = (u2 * _sigmoid(u2)).astype(cv_ref.dtype)

    ap, ac, an = _halo_specs(S, T, C, 3)
    gp, gc, gn = _halo_specs(S, T, C, 4)
    vec = pl.BlockSpec((1, C), lambda i: (0, 0))
    out = pl.BlockSpec((T, C), lambda i: (i, 0))
    return pl.pallas_call(
        body, name="conv_fwd", grid=(nblk,),
        in_specs=[ap, ac, an, gp, gc, gn, pl.BlockSpec((32, C), lambda i: (0, 0)), vec, vec, vec],
        out_specs=[out, out], out_shape=[_sds((S, C), BF16), _sds((S, C), F32)],
        scratch_shapes=[_window_scratch(T, C)],
        compiler_params=_params("parallel"))(y, y, y, y, y, y, conv_w32, conv_b, ln_g, ln_b)


def _conv_bwd(dac, u1, y, conv_w32, ln_g, ln_b, *, T=512):
    S = y.shape[0]
    T = min(T, S)
    nblk = S // T
    C = D_CONV

    def body(dp, dc, dn, up, uc, un, ap, ac, an, gp, gc, gn, w_ref, lg_ref, lb_ref,
             dag_ref, dw_ref, dsm_ref, bufd, bufu, du0_scr, dw_acc):
        i = pl.program_id(0)
        lg = lg_ref[...]

        def du1_of(dcv_ref, u1_ref):
            u1 = u1_ref[...]
            mu = jnp.mean(u1, axis=-1, keepdims=True)
            xc = u1 - mu
            rstd = lax.rsqrt(jnp.mean(xc * xc, axis=-1, keepdims=True) + EPS)
            xhat = xc * rstd
            u2 = xhat * lg + lb_ref[...]
            sg = _sigmoid(u2)
            du2 = dcv_ref[...].astype(F32) * (sg * (1.0 + u2 * (1.0 - sg)))
            dxh = du2 * lg
            du1 = rstd * (dxh - jnp.mean(dxh, axis=-1, keepdims=True)
                          - xhat * jnp.mean(dxh * xhat, axis=-1, keepdims=True))
            return du1, du2, xhat

        def glu(a_ref, g_ref):
            return a_ref[...].astype(F32) * _sigmoid(g_ref[...].astype(F32))

        @pl.when(i == 0)
        def _():
            dw_ref[...] = jnp.zeros_like(dw_ref)
            dsm_ref[...] = jnp.zeros_like(dsm_ref)

        du1_c, du2_c, xhat_c = du1_of(dc, uc)
        dsm_ref[0:1, :] += jnp.sum(du1_c, axis=0, keepdims=True)
        dsm_ref[1:2, :] += jnp.sum(du2_c * xhat_c, axis=0, keepdims=True)
        dsm_ref[2:3, :] += jnp.sum(du2_c, axis=0, keepdims=True)
        _fill_window(bufd, jnp.where(i > 0, du1_of(dp, up)[0], 0.0), du1_c,
                     jnp.where(i < nblk - 1, du1_of(dn, un)[0], 0.0), T)
        _fill_window(bufu, jnp.where(i > 0, glu(ap, gp), 0.0), glu(ac, gc),
                     jnp.where(i < nblk - 1, glu(an, gn), 0.0), T)

        _depthwise(bufd, w_ref, du0_scr, T, C, HALO + CONV_PAD, -1)
        dw_acc[...] = jnp.zeros_like(dw_acc)

        def dw_tile(t, carry):
            r0 = pl.multiple_of(t * CONV_ROWS, CONV_ROWS)
            for c0 in range(0, C, LANES):
                ls = slice(c0, c0 + LANES)
                d = bufd[0, pl.ds(HALO + r0, CONV_ROWS), ls]
                for k, rows in _tap_reads(bufu, HALO - CONV_PAD, 1, r0, ls):
                    prod = d * rows
                    part = prod[0:8]
                    for j in range(8, CONV_ROWS, 8):
                        part = part + prod[j:j + 8]
                    dw_acc[k, :, ls] += part
            return carry

        lax.fori_loop(0, T // CONV_ROWS, dw_tile, 0)
        for k in range(CONV_WIDTH):
            dw_ref[k:k + 1, :] += jnp.sum(dw_acc[k], axis=0, keepdims=True)
        du0 = du0_scr[...]
        a = ac[...].astype(F32)
        sg = _sigmoid(gc[...].astype(F32))
        dag_ref[:, 0:C] = (du0 * sg).astype(dag_ref.dtype)
        dag_ref[:, C:] = (du0 * a * sg * (1.0 - sg)).astype(dag_ref.dtype)

    dp, dc, dn = _halo_specs(S, T, C, 1)
    up, uc, un = _halo_specs(S, T, C, 0)
    ap, ac, an = _halo_specs(S, T, C, 3)
    gp, gc, gn = _halo_specs(S, T, C, 4)
    vec = pl.BlockSpec((1, C), lambda i: (0, 0))
    return pl.pallas_call(
        body, name="conv_bwd", grid=(nblk,),
        in_specs=[dp, dc, dn, up, uc, un, ap, ac, an, gp, gc, gn,
                  pl.BlockSpec((32, C), lambda i: (0, 0)), vec, vec],
        out_specs=[pl.BlockSpec((T, 2 * C), lambda i: (i, 0)), pl.BlockSpec((32, C), lambda i: (0, 0)),
                   pl.BlockSpec((8, C), lambda i: (0, 0))],
        out_shape=[_sds((S, 2 * C), BF16), _sds((32, C), F32), _sds((8, C), F32)],
        scratch_shapes=[_window_scratch(T, C), _window_scratch(T, C), pltpu.VMEM((T, C), F32),
                        pltpu.VMEM((CONV_WIDTH, 8, C), F32)],
        compiler_params=_params("arbitrary"))(dac, dac, dac, u1, u1, u1, y, y, y, y, y, y, conv_w32, ln_g, ln_b)


def _xatt_fwd(xq, xk, xv, *, tm=512):
    S = xq.shape[0]
    M = xk.shape[0]
    tm = min(tm, S)
    scale = XATT_HEAD_DIM ** -0.5

    def body(q_ref, k_ref, v_ref, o_ref):
        heads = [slice(h * XATT_HEAD_DIM, (h + 1) * XATT_HEAD_DIM) for h in range(XATT_HEADS)]
        s = [_nt(q_ref[:, sl], k_ref[:, sl]) * scale for sl in heads]
        e = [jnp.exp(t - jnp.max(t, axis=-1, keepdims=True)) for t in s]
        p = [t * (1.0 / jnp.sum(t, axis=-1, keepdims=True)) for t in e]
        for sl, t in zip(heads, p):
            o_ref[:, sl] = jnp.dot(t.astype(BF16), v_ref[:, sl], preferred_element_type=F32).astype(o_ref.dtype)

    row = pl.BlockSpec((tm, D_MODEL), lambda i: (i, 0))
    full = pl.BlockSpec((M, D_MODEL), lambda i: (0, 0))
    return pl.pallas_call(
        body, name="xatt_fwd", grid=(S // tm,), in_specs=[row, full, full], out_specs=row,
        out_shape=_sds((S, D_MODEL), BF16), compiler_params=_params("parallel"))(xq, xk, xv)


def _xatt_bwd(xq, xk, xv, dxo, *, tm=512):
    S = xq.shape[0]
    M = xk.shape[0]
    tm = min(tm, S)
    scale = XATT_HEAD_DIM ** -0.5

    def body(q_ref, k_ref, v_ref, do_ref, dq_ref, dk_ref, dv_ref):
        i = pl.program_id(0)

        @pl.when(i == 0)
        def _():
            dk_ref[...] = jnp.zeros_like(dk_ref)
            dv_ref[...] = jnp.zeros_like(dv_ref)

        heads = [slice(h * XATT_HEAD_DIM, (h + 1) * XATT_HEAD_DIM) for h in range(XATT_HEADS)]
        s = [_nt(q_ref[:, sl], k_ref[:, sl]) * scale for sl in heads]
        dp = [_nt(do_ref[:, sl], v_ref[:, sl]) for sl in heads]
        e = [jnp.exp(t - jnp.max(t, axis=-1, keepdims=True)) for t in s]
        p = [t * (1.0 / jnp.sum(t, axis=-1, keepdims=True)) for t in e]
        ds = [(pp * (t - jnp.sum(t * pp, axis=-1, keepdims=True))).astype(BF16) for pp, t in zip(p, dp)]
        for sl, pp, t in zip(heads, p, ds):
            dq_ref[:, sl] = (jnp.dot(t, k_ref[:, sl], preferred_element_type=F32) * scale).astype(dq_ref.dtype)
            dv_ref[:, sl] += _tn(pp.astype(BF16), do_ref[:, sl])
            dk_ref[:, sl] += _tn(t, q_ref[:, sl]) * scale

    row = pl.BlockSpec((tm, D_MODEL), lambda i: (i, 0))
    full = pl.BlockSpec((M, D_MODEL), lambda i: (0, 0))
    return pl.pallas_call(
        body, name="xatt_bwd", grid=(S // tm,), in_specs=[row, full, full, row], out_specs=[row, full, full],
        out_shape=[_sds((S, D_MODEL), BF16), _sds((M, D_MODEL), F32), _sds((M, D_MODEL), F32)],
        compiler_params=_params("arbitrary"))(xq, xk, xv, dxo)


def _row_tile(R):
    for t in (256, 128, 64, 32, 16, 8):
        if R % t == 0:
            return t
    return R


def _sum_partials(own, recv, me, *, name):
    _, R, C = own.shape
    t = _row_tile(R)

    def body(me_ref, own_ref, r_ref, o_ref):
        o_ref[...] = ((own_ref[...].astype(F32) + r_ref[0].astype(F32)) + r_ref[1].astype(F32)) + r_ref[2].astype(F32)

    return pl.pallas_call(
        body, name=name,
        grid_spec=pltpu.PrefetchScalarGridSpec(
            num_scalar_prefetch=1, grid=(R // t,),
            in_specs=[pl.BlockSpec((None, t, C), lambda i, me_ref: (me_ref[0], i, 0)),
                      pl.BlockSpec((3, t, C), lambda i, me_ref: (0, i, 0))],
            out_specs=pl.BlockSpec((t, C), lambda i, me_ref: (i, 0))),
        out_shape=_sds((R, C), F32), compiler_params=_params("parallel"))(me, own, recv)


def _adamw_math(w, g, m, v):
    m2 = ADAM_B1 * m + (1.0 - ADAM_B1) * g
    v2 = ADAM_B2 * v + (1.0 - ADAM_B2) * (g * g)
    m_hat = m2 / (1.0 - ADAM_B1 ** ADAM_STEP)
    v_hat = v2 / (1.0 - ADAM_B2 ** ADAM_STEP)
    delta = -ADAM_LR * (m_hat / (jnp.sqrt(v_hat) + ADAM_EPS) + ADAM_WD * w)
    return delta, m2, v2


def _adamw(parts, w, m, v, *, name):
    R, C = w.shape
    t = _row_tile(R)
    n = len(parts)

    def body(*refs):
        w_ref, m_ref, v_ref = refs[n:n + 3]
        g_ref, d_ref, m2_ref, v2_ref = refs[n + 3:]
        g = refs[0][...]
        for r in refs[1:n]:
            g = g + r[...]
        delta, m2, v2 = _adamw_math(w_ref[...], g, m_ref[...], v_ref[...])
        g_ref[...] = g
        d_ref[...] = delta
        m2_ref[...] = m2
        v2_ref[...] = v2

    blk = pl.BlockSpec((t, C), lambda i: (i, 0))
    return pl.pallas_call(
        body, name=name, grid=(R // t,), in_specs=[blk] * (n + 3), out_specs=[blk] * 4,
        out_shape=[_sds((R, C), F32)] * 4, compiler_params=_params("parallel"))(*parts, w, m, v)


def _adamw_small(gathered, chip, entries):
    _, R, C = gathered.shape
    n = len(entries)
    group = D_CONV // N_CHIPS

    def body(chip_ref, g_ref, *refs):
        ins, outs, tot_ref = refs[:3 * n], refs[3 * n:7 * n], refs[7 * n]
        tot = g_ref[0]
        for k in range(1, N_DEV):
            tot = tot + g_ref[k]
        tot_ref[...] = tot
        for e, ((kind, r), _, _, _) in enumerate(entries):
            if kind == "row":
                g = tot_ref[r:r + 1, :]
            elif kind == "gain":
                g = jnp.concatenate([tot_ref[r:r + 1, :], tot_ref[r + 1:r + 2, :]], axis=1)
            else:
                g = tot_ref[r:r + CONV_WIDTH, 0:group]
                for j in range(1, N_CHIPS):
                    g = jnp.where(chip_ref[0] == j, tot_ref[r:r + CONV_WIDTH, j * group:(j + 1) * group], g)
            delta, m2, v2 = _adamw_math(ins[3 * e][...], g, ins[3 * e + 1][...], ins[3 * e + 2][...])
            for o, val in zip(outs[4 * e:4 * e + 4], (g, delta, m2, v2)):
                o[...] = val

    whole = lambda a: pl.BlockSpec(a.shape, lambda i, c: (0,) * a.ndim)
    arrays = [a for _, w, m, v in entries for a in (w, m, v)]
    out_like = [w for _, w, _, _ in entries for _ in range(4)]
    tot_like = _sds((R, C), F32)
    out = pl.pallas_call(
        body, name="adamw_small",
        grid_spec=pltpu.PrefetchScalarGridSpec(
            num_scalar_prefetch=1, grid=(1,),
            in_specs=[whole(gathered)] + [whole(a) for a in arrays],
            out_specs=[whole(a) for a in out_like] + [whole(tot_like)]),
        out_shape=[_sds(a.shape, F32) for a in out_like] + [tot_like],
        compiler_params=_params("arbitrary"))(chip, gathered, *arrays)
    return out[-1], [tuple(out[4 * e:4 * e + 4]) for e in range(n)]


def _chip_peers():
    x, y = lax.axis_index("x"), lax.axis_index("y")
    return [(1 - x, y), (x, 1 - y), (1 - x, 1 - y)]


HBM_SPEC = pl.BlockSpec(memory_space=pltpu.HBM)
SEM_SPEC = pl.BlockSpec(memory_space=pltpu.SEMAPHORE)


def _exchange_start(mode, srcs, zones, *, name):
    n = len(srcs)

    def body(*refs):
        ins, lands = refs[:n], refs[n:2 * n]
        send_sems, recv_sems = refs[2 * n:3 * n], refs[3 * n:4 * n]
        token = refs[-1]
        c = lax.axis_index("c")
        mine = 2 * lax.axis_index("x") + lax.axis_index("y")
        for t in range(n):
            for k, (px, py) in enumerate(_chip_peers()):
                if mode == "gather":
                    s, d = ins[t], lands[t].at[mine]
                else:
                    s, d = ins[t].at[2 * px + py], lands[t].at[k]
                pltpu.make_async_remote_copy(src_ref=s, dst_ref=d, send_sem=send_sems[t], recv_sem=recv_sems[t],
                                             device_id=(px, py, c), device_id_type=MESH).start()
            if mode == "gather":
                pltpu.make_async_copy(ins[t], lands[t].at[mine], send_sems[t]).start()
        token[...] = jnp.zeros_like(token)

    hbm = lambda a: pltpu.with_memory_space_constraint(a, pltpu.HBM)
    out = pl.pallas_call(
        body, name=name,
        in_specs=[HBM_SPEC] * (2 * n),
        out_specs=[SEM_SPEC] * (2 * n) + [HBM_SPEC] * (2 * n) + [pl.BlockSpec(memory_space=pltpu.VMEM)],
        out_shape=[pltpu.SemaphoreType.DMA(())] * (2 * n)
        + [pltpu.HBM(a.shape, a.dtype) for a in list(srcs) + list(zones)] + [_sds((8, LANES), F32)],
        input_output_aliases={i: 2 * n + i for i in range(2 * n)},
        compiler_params=pltpu.CompilerParams(has_side_effects=pltpu.SideEffectType.DATAFLOW_SIDE_EFFECTING),
    )(*[hbm(a) for a in list(srcs) + list(zones)])
    return out[:n], out[n:2 * n], out[2 * n:3 * n], out[3 * n:4 * n], out[-1]


def _exchange_wait(mode, started, after, *, name):
    send_sems, recv_sems, srcs, zones, _ = started
    n = len(srcs)

    def body(*refs):
        lands = refs[n:2 * n]
        send_refs, recv_refs = refs[2 * n:3 * n], refs[3 * n:4 * n]
        me = (lax.axis_index("x"), lax.axis_index("y"), lax.axis_index("c"))
        for t in range(n):
            three = lands[t].at[pl.ds(0, N_CHIPS - 1)]
            sent = lands[t] if mode == "gather" else three
            pltpu.make_async_remote_copy(src_ref=sent, dst_ref=sent, send_sem=send_refs[t], recv_sem=recv_refs[t],
                                         device_id=me, device_id_type=MESH).wait_send()
            pltpu.make_async_remote_copy(src_ref=three, dst_ref=three, send_sem=send_refs[t], recv_sem=recv_refs[t],
                                         device_id=me, device_id_type=MESH).wait_recv()

    out = pl.pallas_call(
        body, name=name,
        in_specs=[HBM_SPEC] * (2 * n) + [SEM_SPEC] * (2 * n) + [pl.BlockSpec(memory_space=pl.ANY)],
        out_specs=[HBM_SPEC] * (2 * n),
        out_shape=[pltpu.HBM(a.shape, a.dtype) for a in list(srcs) + list(zones)],
        input_output_aliases={i: i for i in range(2 * n)},
        compiler_params=pltpu.CompilerParams(has_side_effects=pltpu.SideEffectType.DATAFLOW_SIDE_EFFECTING),
    )(*srcs, *zones, *send_sems, *recv_sems, after)
    return out[:n], out[n:]


def _allgather_small(small):
    def body(small_ref, gath_ref, send_sems, recv_sems, loc_sem):
        x, y, c = lax.axis_index("x"), lax.axis_index("y"), lax.axis_index("c")
        me = 4 * x + 2 * y + c
        flips = [(fx, fy, fc) for fx in (0, 1) for fy in (0, 1) for fc in (0, 1)][1:]

        def flipped(fx, fy, fc):
            return (1 - x if fx else x, 1 - y if fy else y, 1 - c if fc else c)

        loc = pltpu.make_async_copy(small_ref, gath_ref.at[me], loc_sem)
        loc.start()
        sends = []
        for j, flip in enumerate(flips):
            cp = pltpu.make_async_remote_copy(
                src_ref=small_ref, dst_ref=gath_ref.at[me], send_sem=send_sems.at[j], recv_sem=recv_sems.at[j],
                device_id=flipped(*flip), device_id_type=MESH)
            cp.start()
            sends.append(cp)
        for j, flip in enumerate(flips):
            px, py, pc = flipped(*flip)
            pltpu.make_async_remote_copy(
                src_ref=small_ref, dst_ref=gath_ref.at[4 * px + 2 * py + pc], send_sem=send_sems.at[j],
                recv_sem=recv_sems.at[j], device_id=(px, py, pc), device_id_type=MESH).wait_recv()
        for cp in sends:
            cp.wait_send()
        loc.wait()

    any_spec = pl.BlockSpec(memory_space=pl.ANY)
    return pl.pallas_call(
        body, name="allgather_small", in_specs=[any_spec], out_specs=any_spec,
        out_shape=_sds((N_DEV,) + small.shape, small.dtype),
        scratch_shapes=[pltpu.SemaphoreType.DMA((N_DEV - 1,)), pltpu.SemaphoreType.DMA((N_DEV - 1,)),
                        pltpu.SemaphoreType.DMA])(small)


def _swap_with_sibling(parts):
    n = len(parts)

    def body(*refs):
        ins, outs = refs[:n], refs[n:2 * n]
        send_sems, recv_sems = refs[2 * n:]
        sib = (lax.axis_index("x"), lax.axis_index("y"), 1 - lax.axis_index("c"))
        cps = []
        for t in range(n):
            cp = pltpu.make_async_remote_copy(
                src_ref=ins[t], dst_ref=outs[t], send_sem=send_sems.at[t], recv_sem=recv_sems.at[t],
                device_id=sib, device_id_type=MESH)
            cp.start()
            cps.append(cp)
        for cp in cps:
            cp.wait()

    any_spec = pl.BlockSpec(memory_space=pl.ANY)
    return pl.pallas_call(
        body, name="swap_with_sibling", in_specs=[any_spec] * n, out_specs=[any_spec] * n,
        out_shape=[_sds(p.shape, p.dtype) for p in parts],
        scratch_shapes=[pltpu.SemaphoreType.DMA((n,)), pltpu.SemaphoreType.DMA((n,))])(*parts)


BIG = ("w_in", "w_out", "w_xq", "w_xk", "w_xv", "w_xo", "w_up", "w_down")
COL_SHARDED = ("w_in", "w_up")


def _as_matrix(name, w4):
    if name in COL_SHARDED:
        return w4
    return w4.reshape(1, w4.shape[0] * w4.shape[1], w4.shape[2])


def _transposed(w3):
    nsh, K, n = w3.shape
    return jnp.swapaxes(w3, 1, 2).reshape(1, nsh * n, K)


def _shard_layout(name, g):
    if name in COL_SHARDED:
        return g
    return g.reshape(N_CHIPS, g.shape[0] * g.shape[1] // N_CHIPS, g.shape[2])


def _local_step(x, mem, target, vecs, comm):
    S = x.shape[0]
    tables = _rope_tables(S)

    xn = _rms_fwd(x, vecs["norm_mix_g"], name="rms_mix")
    w_in, conv_w32 = comm["first"](xn)
    y = _mm_nn(xn, w_in, name="mm_in", tm=2048, tn=640)
    qk, v_perm = _rope_fwd(y, tables)
    v_src = [(y, 2)] + [(v, 0) for v in v_perm[1:]]
    outs, lses = zip(*[_att_fwd(qk[p], v_src[p], d, name=f"att_fwd_d{d}") for p, d in enumerate(DILATIONS)])
    att, lg = _att_combine(outs, lses)
    cv, u1 = _conv_fwd(y, conv_w32, vecs["conv_b"], vecs["conv_ln_g"], vecs["conv_ln_b"])
    Wm = {k: _as_matrix(k, v) for k, v in comm["rest"](cv).items()}
    Wm["w_in"] = w_in
    h1, hn = _mm_rows((att, cv), Wm["w_out"], _residual_norm_tail, name="mm_out_rms", rows_in=(x,),
                      vecs_in=(vecs["norm_x_g"],), rows_out=(F32, BF16))
    xq = _mm_nn(hn, Wm["w_xq"], name="mm_xq")
    mn = _rms_fwd(mem, vecs["norm_mem_g"], name="rms_mem")
    xk = _mm_nn(mn, Wm["w_xk"], name="mm_xk")
    xv = _mm_nn(mn, Wm["w_xv"], name="mm_xv")
    xo = _xatt_fwd(xq, xk, xv)
    h2, hm = _mm_rows(xo, Wm["w_xo"], _residual_norm_tail, name="mm_xo_rms", rows_in=(h1,),
                      vecs_in=(vecs["norm_mlp_g"],), rows_out=(F32, BF16))
    relu_up = _mm_nn(hm, Wm["w_up"], name="mm_up", relu=True, tm=2048)
    sums = ((8, D_MODEL),)
    dh3, dh3b, dg_final, loss = _mm_rows(
        relu_up, Wm["w_down"], _loss_tail, name="mm_down_loss", rows_in=(h2, target), vecs_in=(vecs["norm_final_g"],),
        rows_out=(F32, BF16), sums_out=sums + ((8, LANES),), a_squared=True, tm=256)
    g = {}
    g["w_down"] = _mm_tn(relu_up, dh3b, 1, name="dw_down", a_squared=True)
    dup = _mm_nt(dh3b, Wm["w_down"], name="d_act", out_dtype=BF16, mul=relu_up, tm=2048)
    g["w_up"] = _mm_tn(hm, dup, N_CHIPS, name="dw_up")
    sent = comm["send_mlp"]({k: _shard_layout(k, g[k]) for k in ("w_down", "w_up")})
    dh2, dh2b, dg_mlp = _mm_rows(
        dup, _transposed(Wm["w_up"]), _rms_bwd_tail(True), name="d_hm_rms", rows_in=(h2, dh3),
        vecs_in=(vecs["norm_mlp_g"] + sent[0:1, 0:1],), rows_out=(F32, BF16), sums_out=sums, tm=256)
    g["w_xo"] = _mm_tn(xo, dh2b, 1, name="dw_xo")
    dxo = _mm_nt(dh2b, Wm["w_xo"], name="d_xo", out_dtype=BF16)
    dxq, dxk, dxv = _xatt_bwd(xq, xk, xv, dxo)
    g["w_xq"] = _mm_tn(hn, dxq, 1, name="dw_xq")
    dh1, dh1b, dg_x = _mm_rows(
        dxq, _transposed(Wm["w_xq"]), _rms_bwd_tail(True), name="d_hn_rms", rows_in=(h1, dh2),
        vecs_in=(vecs["norm_x_g"],), rows_out=(F32, BF16), sums_out=sums)
    dxkb, dxvb = dxk.astype(BF16), dxv.astype(BF16)
    g["w_xk"] = _mm_tn(mn, dxkb, 1, name="dw_xk")
    g["w_xv"] = _mm_tn(mn, dxvb, 1, name="dw_xv")
    dmn = _mm_nt(jnp.concatenate([dxkb, dxvb], axis=1),
                 jnp.concatenate([Wm["w_xk"], Wm["w_xv"]], axis=2), name="d_mn", out_dtype=BF16)
    _, _, dg_mem = _rms_bwd(dmn, mem, vecs["norm_mem_g"], None, name="rms_bwd_mem", bf16_copy=False)
    g["w_out"] = jnp.concatenate([_mm_tn(att, dh1b, 1, name="dw_out_att"), _mm_tn(cv, dh1b, 1, name="dw_out_conv")],
                                 axis=1)
    sent = comm["send_att"]({k: _shard_layout(k, g[k]) for k in ("w_out", "w_xq", "w_xk", "w_xv", "w_xo")})
    dac = _mm_nt(dh1b, Wm["w_out"], name="d_mix", out_dtype=BF16)
    dag, dconv_w, dconv_small = _conv_bwd(dac, u1, y, conv_w32, vecs["conv_ln_g"] + sent[0:1, 0:1],
                                          vecs["conv_ln_b"])
    delta, do_perm = _att_delta(dac, att)
    do_src = [(dac, 0)] + [(t, 0) for t in do_perm[1:]]
    dq, dk, dv = zip(*[_att_bwd(qk[p], v_src[p], do_src[p], lg[p], delta[p], d, name=f"att_bwd_d{d}")
                       for p, d in enumerate(DILATIONS)])
    dy = _assemble_dy(dq, dk, dv, dag, tables)
    sent = comm["send_in"]({"w_in": _mm_tn(xn, dy, N_CHIPS, name="dw_in", tn=640)})
    grad_x, dg_mix = _mm_rows(
        dy, _transposed(Wm["w_in"]), _rms_bwd_tail(False), name="d_xn_rms", rows_in=(x, dh1),
        vecs_in=(vecs["norm_mix_g"] + sent[0:1, 0:1],), rows_out=(F32,), sums_out=sums)

    small = dict(conv_w=dconv_w, conv_small=dconv_small, norm_mix_g=dg_mix, norm_x_g=dg_x, norm_mem_g=dg_mem,
                 norm_mlp_g=dg_mlp, norm_final_g=dg_final, loss=loss)
    return grad_x, small


SMALL_ORDER = ("conv_w", "conv_small", "norm_mix_g", "norm_x_g", "norm_mem_g", "norm_mlp_g", "norm_final_g", "loss")


def _pack_small(small):
    rows, offs, pos = [], {}, 0
    for k in SMALL_ORDER:
        a = small[k]
        a = a.reshape(a.shape[0] * a.shape[1] // SMALL_W, SMALL_W)
        pad = (-a.shape[0]) % 8
        if pad:
            a = jnp.pad(a, ((0, pad), (0, 0)))
        rows.append(a)
        offs[k] = pos
        pos += a.shape[0]
    return jnp.concatenate(rows, axis=0), offs


def kernel(x, mem, norm_mix_g, w_in, conv_w, conv_b, conv_ln_g, conv_ln_b, w_out, norm_x_g, norm_mem_g, w_xq, w_xk, w_xv, w_xo, norm_mlp_g, w_up, w_down, norm_final_g, loss_target, m_norm_mix_g, m_w_in, m_conv_w, m_conv_b, m_conv_ln_g, m_conv_ln_b, m_w_out, m_norm_x_g, m_norm_mem_g, m_w_xq, m_w_xk, m_w_xv, m_w_xo, m_norm_mlp_g, m_w_up, m_w_down, m_norm_final_g, v_norm_mix_g, v_w_in, v_conv_w, v_conv_b, v_conv_ln_g, v_conv_ln_b, v_w_out, v_norm_x_g, v_norm_mem_g, v_w_xq, v_w_xk, v_w_xv, v_w_xo, v_norm_mlp_g, v_w_up, v_w_down, v_norm_final_g):
    names = ("norm_mix_g", "w_in", "conv_w", "conv_b", "conv_ln_g", "conv_ln_b", "w_out", "norm_x_g", "norm_mem_g",
             "w_xq", "w_xk", "w_xv", "w_xo", "norm_mlp_g", "w_up", "w_down", "norm_final_g")
    wts = dict(zip(names, (norm_mix_g, w_in, conv_w, conv_b, conv_ln_g, conv_ln_b, w_out, norm_x_g, norm_mem_g,
                           w_xq, w_xk, w_xv, w_xo, norm_mlp_g, w_up, w_down, norm_final_g)))
    mom = dict(zip(names, (m_norm_mix_g, m_w_in, m_conv_w, m_conv_b, m_conv_ln_g, m_conv_ln_b, m_w_out, m_norm_x_g,
                           m_norm_mem_g, m_w_xq, m_w_xk, m_w_xv, m_w_xo, m_norm_mlp_g, m_w_up, m_w_down, m_norm_final_g)))
    var = dict(zip(names, (v_norm_mix_g, v_w_in, v_conv_w, v_conv_b, v_conv_ln_g, v_conv_ln_b, v_w_out, v_norm_x_g,
                           v_norm_mem_g, v_w_xq, v_w_xk, v_w_xv, v_w_xo, v_norm_mlp_g, v_w_up, v_w_down, v_norm_final_g)))
    chip = 2 * lax.axis_index("x") + lax.axis_index("y")

    def zone(shard):
        return lax.empty((N_CHIPS,) + shard.shape, shard.dtype)

    conv_w_pad = jnp.pad(wts["conv_w"][0], ((0, 1), (0, 0)))
    first_shards = [wts["w_in"][0].astype(BF16), conv_w_pad]
    gathering_first = _exchange_start("gather", first_shards, [zone(s) for s in first_shards],
                                      name="gather_first_start")
    rest = tuple(k for k in BIG if k != "w_in")
    behind_first = gathering_first[4][0, 0]
    rest_shards = [(wts[k][0] + behind_first).astype(BF16) for k in rest]
    gathering = _exchange_start("gather", rest_shards, [zone(s) for s in rest_shards], name="gather_rest_start")
    sending = {}

    def wait_first(after):
        _, (w_in_all, conv_w_all) = _exchange_wait("gather", gathering_first, after, name="gather_first_wait")
        return w_in_all, jnp.transpose(conv_w_all, (1, 0, 2)).reshape(32, D_CONV)

    def wait_rest(after):
        _, zones = _exchange_wait("gather", gathering, after, name="gather_rest_wait")
        return dict(zip(rest, zones))

    def send(group, grads):
        keys = tuple(grads)
        zones = [lax.empty((N_CHIPS - 1,) + grads[k].shape[1:], grads[k].dtype) for k in keys]
        sending[group] = (keys, _exchange_start("scatter", [grads[k] for k in keys], zones,
                                                name=f"scatter_{group}_start"))
        return sending[group][1][4]

    comm = dict(first=wait_first, rest=wait_rest, send_mlp=lambda grads: send("mlp", grads),
                send_att=lambda grads: send("att", grads), send_in=lambda grads: send("in", grads))
    vecs = {k: wts[k] for k in ("conv_b", "conv_ln_g", "conv_ln_b", "norm_x_g", "norm_mem_g", "norm_mlp_g")}
    vecs["norm_mix_g"] = wts["norm_mix_g"] + gathering[4][0:1, 0:1]
    vecs["norm_final_g"] = wts["norm_final_g"].reshape(1, D_MODEL)
    grad_x, small = _local_step(x[0], mem[0], loss_target[0], vecs, comm)

    packed, offs = _pack_small(small)
    gath = _allgather_small(packed)
    big, recv = {}, {}
    for group in ("mlp", "att", "in"):
        keys, started = sending[group]
        srcs, zones = _exchange_wait("scatter", started, gath, name=f"scatter_{group}_wait")
        big.update(zip(keys, srcs))
        recv.update(zip(keys, zones))
    me_arr = jnp.reshape(chip, (1,)).astype(jnp.int32)
    sums = [_sum_partials(big[k], recv[k], me_arr, name=f"sum_{k}") for k in BIG]
    sib = _swap_with_sibling(sums)

    res = {}
    for k, s_mine, s_sib in zip(BIG, sums, sib):
        res[k] = _adamw([s_mine, s_sib], wts[k][0], mom[k][0], var[k][0], name=f"adamw_{k}")

    where = {"conv_w": ("conv_w", offs["conv_w"]), "conv_b": ("row", offs["conv_small"]),
             "conv_ln_g": ("row", offs["conv_small"] + 1), "conv_ln_b": ("row", offs["conv_small"] + 2)}
    where.update({k: ("gain", offs[k]) for k in ("norm_mix_g", "norm_x_g", "norm_mem_g", "norm_mlp_g", "norm_final_g")})
    as_2d = lambda a: a.reshape(a.shape[-2] if a.ndim > 1 else 1, a.shape[-1])
    tot_small, updates = _adamw_small(gath, me_arr, [(where[k], as_2d(wts[k]), as_2d(mom[k]), as_2d(var[k]))
                                                     for k in where])
    res.update(zip(where, updates))
    loss = tot_small[offs["loss"], 0]

    outs = [loss, grad_x[None]]
    for j in range(4):
        outs += [res[k][j].reshape(wts[k].shape) for k in names]
    return tuple(outs)
```

```python
import jax
import jax.numpy as jnp
from jax import lax
from jax.experimental import pallas as pl
from jax.experimental.pallas import tpu as pltpu

F32 = jnp.float32
BF16 = jnp.bfloat16
MESH = pl.DeviceIdType.MESH

D_MODEL = 1024
ATT_HEADS = 8
HEAD_DIM = 64
D_ATT = ATT_HEADS * HEAD_DIM
D_CONV = D_MODEL - D_ATT
DILATIONS = (1, 4, 16)
HALF = 64
ROPE_THETA = 500000.0
ROT_DIM = HEAD_DIM // 4
CONV_WIDTH = 31
CONV_PAD = (CONV_WIDTH - 1) // 2
XATT_HEADS = 4
XATT_HEAD_DIM = D_MODEL // XATT_HEADS
D_FF = 4 * D_MODEL
D_IN = 3 * D_ATT + 2 * D_CONV
EPS = 1e-6
NEG_INF = -1e30
N_CHIPS = 4
N_DEV = 8

ADAM_LR = 0.001
ADAM_B1 = 0.9
ADAM_B2 = 0.999
ADAM_EPS = 1e-08
ADAM_WD = 0.01
ADAM_STEP = 10

VMEM_LIMIT_V7X = 56 * 1024 * 1024
LANES = 128
HALO = 16
CONV_ROWS = 64
ATT_BLOCK = 128
SMALL_W = 512


def _params(*sem):
    return pltpu.CompilerParams(dimension_semantics=sem, vmem_limit_bytes=VMEM_LIMIT_V7X)


def _sds(shape, dtype):
    return jax.ShapeDtypeStruct(shape, dtype)


def _squared(a):
    af = a.astype(F32)
    return (af * af).astype(BF16)


def _mm_nn(a, w3, *, name, out_dtype=BF16, res=None, relu=False, a_squared=False, tm=1024, tn=None, tk=1024):
    M, K = a.shape
    nsh, _, n = w3.shape
    tm, tk = min(tm, M), min(tk, K)
    tn = tn or min(n, 1024)
    npt, nk = n // tn, K // tk
    nj, N = nsh * npt, nsh * n
    n_out = 1

    def body(*refs):
        a_ref, w_ref = refs[0], refs[1]
        pos = 2
        res_ref = None
        if res is not None:
            res_ref = refs[pos]
            pos += 1
        outs = refs[pos:pos + n_out]
        acc_ref = refs[pos + n_out] if nk > 1 else None

        def finish(acc):
            if res_ref is not None:
                acc = acc + res_ref[...]
            if relu:
                acc = jnp.maximum(acc, 0.0)
            outs[0][...] = acc.astype(outs[0].dtype)

        a_val = _squared(a_ref[...]) if a_squared else a_ref[...]
        part = jnp.dot(a_val, w_ref[...], preferred_element_type=F32)
        if nk == 1:
            finish(part)
        else:
            k = pl.program_id(2)

            @pl.when(k == 0)
            def _():
                acc_ref[...] = part

            @pl.when(k > 0)
            def _():
                acc_ref[...] += part

            @pl.when(k == nk - 1)
            def _():
                finish(acc_ref[...])

    in_specs = [pl.BlockSpec((tm, tk), lambda i, j, k: (i, k)),
                pl.BlockSpec((None, tk, tn), lambda i, j, k: (j // npt, k, j % npt))]
    args = [a, w3]
    if res is not None:
        in_specs.append(pl.BlockSpec((tm, tn), lambda i, j, k: (i, j)))
        args.append(res)
    out_spec = pl.BlockSpec((tm, tn), lambda i, j, k: (i, j))
    out = pl.pallas_call(
        body, name=name, grid=(M // tm, nj, nk), in_specs=in_specs,
        out_specs=[out_spec] * n_out, out_shape=[_sds((M, N), out_dtype)] * n_out,
        scratch_shapes=[pltpu.VMEM((tm, tn), F32)] if nk > 1 else [],
        compiler_params=_params("parallel", "parallel", "arbitrary"))(*args)
    return out[0]


def _mm_nt(dy, w3, *, name, out_dtype=F32, mul=None, tm=1024, tn=None, tko=1024):
    M, N = dy.shape
    nsh, K, n = w3.shape
    tm, tko = min(tm, M), min(tko, K)
    tn = tn or min(n, 1024)
    npt = n // tn
    nj = nsh * npt

    def body(*refs):
        dy_ref, w_ref = refs[0], refs[1]
        pos = 2
        mul_ref = None
        if mul is not None:
            mul_ref = refs[pos]
            pos += 1
        out_ref = refs[pos]
        acc_ref = refs[pos + 1] if nj > 1 else None

        def finish(acc):
            if mul_ref is not None:
                acc = acc * (2.0 * mul_ref[...].astype(F32))
            out_ref[...] = acc.astype(out_ref.dtype)

        part = lax.dot_general(dy_ref[...], w_ref[...], (((1,), (1,)), ((), ())), preferred_element_type=F32)
        if nj == 1:
            finish(part)
        else:
            j = pl.program_id(2)

            @pl.when(j == 0)
            def _():
                acc_ref[...] = part

            @pl.when(j > 0)
            def _():
                acc_ref[...] += part

            @pl.when(j == nj - 1)
            def _():
                finish(acc_ref[...])

    in_specs = [pl.BlockSpec((tm, tn), lambda i, ko, j: (i, j)),
                pl.BlockSpec((None, tko, tn), lambda i, ko, j: (j // npt, ko, j % npt))]
    args = [dy, w3]
    if mul is not None:
        in_specs.append(pl.BlockSpec((tm, tko), lambda i, ko, j: (i, ko)))
        args.append(mul)
    return pl.pallas_call(
        body, name=name, grid=(M // tm, K // tko, nj), in_specs=in_specs,
        out_specs=pl.BlockSpec((tm, tko), lambda i, ko, j: (i, ko)), out_shape=_sds((M, K), out_dtype),
        scratch_shapes=[pltpu.VMEM((tm, tko), F32)] if nj > 1 else [],
        compiler_params=_params("parallel", "parallel", "arbitrary"))(*args)


def _mm_tn(a, dy, nsh, *, name, out_dtype=BF16, a_squared=False, tm=2048, tk=1024, tn=None):
    M, K = a.shape
    N = dy.shape[1]
    n = N // nsh
    tm, tk = min(tm, M), min(tk, K)
    tn = tn or min(n, 1024)
    npt = n // tn
    nj, nm = nsh * npt, M // tm

    def body(a_ref, dy_ref, out_ref, acc_ref):
        m = pl.program_id(2)
        a_val = _squared(a_ref[...]) if a_squared else a_ref[...]
        part = lax.dot_general(a_val, dy_ref[...], (((0,), (0,)), ((), ())), preferred_element_type=F32)

        @pl.when(m == 0)
        def _():
            acc_ref[...] = part

        @pl.when(m > 0)
        def _():
            acc_ref[...] += part

        @pl.when(m == nm - 1)
        def _():
            out_ref[...] = acc_ref[...].astype(out_ref.dtype)

    return pl.pallas_call(
        body, name=name, grid=(K // tk, nj, nm),
        in_specs=[pl.BlockSpec((tm, tk), lambda kk, j, m: (m, kk)),
                  pl.BlockSpec((tm, tn), lambda kk, j, m: (m, j))],
        out_specs=pl.BlockSpec((None, tk, tn), lambda kk, j, m: (j // npt, kk, j % npt)),
        out_shape=_sds((nsh, K, n), out_dtype),
        scratch_shapes=[pltpu.VMEM((tk, tn), F32)],
        compiler_params=_params("parallel", "parallel", "arbitrary"))(a, dy)


def _rms_fwd(x, g, *, name, tm=512):
    M, Dm = x.shape
    tm = min(tm, M)

    def body(x_ref, g_ref, o_ref):
        xf = x_ref[...]
        r = lax.rsqrt(jnp.mean(xf * xf, axis=-1, keepdims=True) + EPS)
        o_ref[...] = (xf * r * g_ref[...]).astype(o_ref.dtype)

    return pl.pallas_call(
        body, name=name, grid=(M // tm,),
        in_specs=[pl.BlockSpec((tm, Dm), lambda i: (i, 0)), pl.BlockSpec((1, Dm), lambda i: (0, 0))],
        out_specs=pl.BlockSpec((tm, Dm), lambda i: (i, 0)), out_shape=_sds((M, Dm), BF16),
        compiler_params=_params("parallel"))(x, g)


def _rms_bwd(dxn, x, g, dres, *, name, bf16_copy=True, tm=512):
    M, Dm = x.shape
    tm = min(tm, M)
    has_res = dres is not None

    def body(*refs):
        dxn_ref, x_ref, g_ref = refs[:3]
        dres_ref = refs[3] if has_res else None
        dx_ref, dg_ref = refs[-1 - 1 - bf16_copy], refs[-1]
        dxb_ref = refs[-2] if bf16_copy else None
        i = pl.program_id(0)
        xf = x_ref[...]
        r = lax.rsqrt(jnp.mean(xf * xf, axis=-1, keepdims=True) + EPS)
        nrm = xf * r
        dxn_f = dxn_ref[...].astype(F32)
        dn = dxn_f * g_ref[...]
        dx = r * (dn - nrm * jnp.mean(dn * nrm, axis=-1, keepdims=True))
        if has_res:
            dx = dx + dres_ref[...]
        dx_ref[...] = dx
        if bf16_copy:
            dxb_ref[...] = dx.astype(dxb_ref.dtype)

        @pl.when(i == 0)
        def _():
            dg_ref[...] = jnp.zeros_like(dg_ref)

        dg_ref[0:1, :] += jnp.sum(dxn_f * nrm, axis=0, keepdims=True)

    row = pl.BlockSpec((tm, Dm), lambda i: (i, 0))
    in_specs = [row, row, pl.BlockSpec((1, Dm), lambda i: (0, 0))] + ([row] if has_res else [])
    args = [dxn, x, g] + ([dres] if has_res else [])
    out = pl.pallas_call(
        body, name=name, grid=(M // tm,), in_specs=in_specs,
        out_specs=[row] * (1 + bf16_copy) + [pl.BlockSpec((8, Dm), lambda i: (0, 0))],
        out_shape=[_sds((M, Dm), F32)] + [_sds((M, Dm), BF16)] * bf16_copy + [_sds((8, Dm), F32)],
        compiler_params=_params("arbitrary"))(*args)
    return out[0], (out[1] if bf16_copy else None), out[-1]


def _mm_rows(a, w3, tail, *, name, rows_in=(), vecs_in=(), rows_out=(), sums_out=(), a_squared=False, tm=512):
    parts = a if isinstance(a, (tuple, list)) else (a,)
    M = parts[0].shape[0]
    K, N = w3.shape[1], w3.shape[2]
    tm = min(tm, M)
    n_a, n_ri, n_vi, n_ro = len(parts), len(rows_in), len(vecs_in), len(rows_out)

    def body(*refs):
        a_refs, w_ref, refs = refs[:n_a], refs[n_a], refs[n_a + 1:]
        rin, vin = refs[:n_ri], refs[n_ri:n_ri + n_vi]
        rout, sout = refs[n_ri + n_vi:n_ri + n_vi + n_ro], refs[n_ri + n_vi + n_ro:]

        @pl.when(pl.program_id(0) == 0)
        def _():
            for s in sout:
                s[...] = jnp.zeros_like(s)

        a_val = a_refs[0][...] if n_a == 1 else jnp.concatenate([r[...] for r in a_refs], axis=1)
        if a_squared:
            a_val = _squared(a_val)
        tail(jnp.dot(a_val, w_ref[0], preferred_element_type=F32), rin, vin, rout, sout)

    row = pl.BlockSpec((tm, N), lambda i: (i, 0))
    once = lambda shape: pl.BlockSpec(shape, lambda i: (0,) * len(shape))
    return pl.pallas_call(
        body, name=name, grid=(M // tm,),
        in_specs=[pl.BlockSpec((tm, p.shape[1]), lambda i: (i, 0)) for p in parts] + [once((1, K, N))]
        + [row] * n_ri + [once((1, N))] * n_vi,
        out_specs=[row] * n_ro + [once(s) for s in sums_out],
        out_shape=[_sds((M, N), dt) for dt in rows_out] + [_sds(s, F32) for s in sums_out],
        compiler_params=_params("arbitrary"))(*parts, w3, *rows_in, *vecs_in)


def _residual_norm_tail(prod, rows_in, vecs_in, rows_out, sums_out):
    hf = prod + rows_in[0][...]
    rows_out[0][...] = hf
    r = lax.rsqrt(jnp.mean(hf * hf, axis=-1, keepdims=True) + EPS)
    rows_out[1][...] = (hf * r * vecs_in[0][...]).astype(BF16)


def _rms_bwd_tail(bf16_copy):
    def tail(dxn, rows_in, vecs_in, rows_out, sums_out):
        xf = rows_in[0][...]
        r = lax.rsqrt(jnp.mean(xf * xf, axis=-1, keepdims=True) + EPS)
        nrm = xf * r
        dn = dxn * vecs_in[0][...]
        dx = r * (dn - nrm * jnp.mean(dn * nrm, axis=-1, keepdims=True)) + rows_in[1][...]
        rows_out[0][...] = dx
        if bf16_copy:
            rows_out[1][...] = dx.astype(BF16)
        sums_out[0][0:1, :] += jnp.sum(dxn * nrm, axis=0, keepdims=True)

    return tail


def _loss_tail(prod, rows_in, vecs_in, rows_out, sums_out):
    hf = prod + rows_in[0][...]
    r = lax.rsqrt(jnp.mean(hf * hf, axis=-1, keepdims=True) + EPS)
    nrm = hf * r
    gv = vecs_in[0][...]
    err = nrm * gv - rows_in[1][...]
    dy = err * (1.0 / hf.shape[-1])
    dn = dy * gv
    dh = r * (dn - nrm * jnp.mean(dn * nrm, axis=-1, keepdims=True))
    rows_out[0][...] = dh
    rows_out[1][...] = dh.astype(BF16)
    sums_out[0][0:1, :] += jnp.sum(dy * nrm, axis=0, keepdims=True)
    part = 0.5 * jnp.sum(jnp.mean(err * err, axis=-1, keepdims=True), axis=0, keepdims=True)
    sel = (lax.broadcasted_iota(jnp.int32, (8, 128), 0) == 0) & (lax.broadcasted_iota(jnp.int32, (8, 128), 1) == 0)
    sums_out[1][...] += jnp.where(sel, part, 0.0)


def _class_spec(tm, d, width):
    return pl.BlockSpec((d, tm // d, width), lambda i: (0, i, 0))


def _row_scratch(tm, width):
    return pltpu.VMEM((width // LANES, tm, LANES), F32)


def _fill(scr, val):
    for c in range(scr.shape[0]):
        scr[c] = val[:, c * LANES:(c + 1) * LANES]


def _to_classes(scr, out_ref, d):
    n = scr.shape[1] // d
    for r in range(d):
        for c in range(scr.shape[0]):
            out_ref[r, :, c * LANES:(c + 1) * LANES] = scr[c, pl.ds(r, n, stride=d), :].astype(out_ref.dtype)


def _from_classes(in_ref, scr, d):
    n = scr.shape[1] // d
    for r in range(d):
        blk = in_ref[r].astype(F32)
        for c in range(scr.shape[0]):
            scr[c, pl.ds(r, n, stride=d), :] = blk[:, c * LANES:(c + 1) * LANES]
    return jnp.concatenate([scr[c] for c in range(scr.shape[0])], axis=1)


def _rope_tables(S):
    half = ROT_DIM // 2
    freqs = ROPE_THETA ** (-jnp.arange(0, ROT_DIM, 2, dtype=F32) / ROT_DIM)
    ang = jnp.arange(S, dtype=F32)[:, None] * freqs[None, :]
    cos, sin = jnp.cos(ang), jnp.sin(ang)
    ones = jnp.ones((S, HEAD_DIM - ROT_DIM), F32)
    zeros = jnp.zeros((S, HEAD_DIM - ROT_DIM), F32)
    zh = jnp.zeros((S, half), F32)
    c = jnp.concatenate([cos, cos, ones], axis=1)
    sa = jnp.concatenate([-sin, zh, zeros], axis=1)
    sb = jnp.concatenate([zh, sin, zeros], axis=1)
    return tuple(jnp.tile(t, (1, LANES // HEAD_DIM)) for t in (c, sa, sb))


def _rope_fwd(y, tables, *, tm=512):
    S = y.shape[0]
    W = 2 * D_ATT
    tm = min(tm, S)
    half = ROT_DIM // 2
    dils = [d for d in DILATIONS if d > 1]

    def body(y_ref, c_ref, sa_ref, sb_ref, qk_ref, *rest):
        qk_outs, v_outs = rest[:len(dils)], rest[len(dils):2 * len(dils)]
        scr_qk, scr_v = rest[2 * len(dils):]
        t = y_ref[:, 0:W].astype(F32)
        rep = W // LANES
        c, sa, sb = (jnp.tile(r[...], (1, rep)) for r in (c_ref, sa_ref, sb_ref))
        rot = t * c + pltpu.roll(t, W - half, axis=1) * sa + pltpu.roll(t, half, axis=1) * sb
        qk_ref[...] = rot.astype(qk_ref.dtype)
        _fill(scr_qk, rot)
        _fill(scr_v, y_ref[:, W:W + D_ATT].astype(F32))
        for d, qo, vo in zip(dils, qk_outs, v_outs):
            _to_classes(scr_qk, qo, d)
            _to_classes(scr_v, vo, d)

    tab = pl.BlockSpec((tm, LANES), lambda i: (i, 0))
    out = pl.pallas_call(
        body, name="rope_fwd", grid=(S // tm,),
        in_specs=[pl.BlockSpec((tm, 3 * D_ATT), lambda i: (i, 0)), tab, tab, tab],
        out_specs=[pl.BlockSpec((tm, W), lambda i: (i, 0))] + [_class_spec(tm, d, W) for d in dils]
        + [_class_spec(tm, d, D_ATT) for d in dils],
        out_shape=[_sds((S, W), BF16)] + [_sds((d, S // d, W), BF16) for d in dils]
        + [_sds((d, S // d, D_ATT), BF16) for d in dils],
        scratch_shapes=[_row_scratch(tm, W), _row_scratch(tm, D_ATT)],
        compiler_params=_params("parallel"))(y, *tables)
    qk = [out[0]] + [o.reshape(S, W) for o in out[1:1 + len(dils)]]
    v = [None] + [o.reshape(S, D_ATT) for o in out[1 + len(dils):]]
    return qk, v


def _assemble_dy(dq, dk, dv, dag, tables, *, tm=512):
    S = dag.shape[0]
    tm = min(tm, S)
    half = ROT_DIM // 2
    W = D_ATT
    n_pat = len(DILATIONS)

    def body(*refs):
        groups = [refs[g * n_pat:(g + 1) * n_pat] for g in range(3)]
        dag_ref, c_ref, sa_ref, sb_ref, o_ref, scr = refs[3 * n_pat:]
        rep = W // LANES
        c, sa, sb = (jnp.tile(r[...], (1, rep)) for r in (c_ref, sa_ref, sb_ref))

        def total(rs):
            acc = rs[0][...].astype(F32)
            for d, r in zip(DILATIONS[1:], rs[1:]):
                acc = acc + _from_classes(r, scr, d)
            return acc

        def unrope(dr):
            return dr * c + pltpu.roll(dr * sa, half, axis=1) + pltpu.roll(dr * sb, W - half, axis=1)

        o_ref[:, 0:W] = unrope(total(groups[0])).astype(o_ref.dtype)
        o_ref[:, W:2 * W] = unrope(total(groups[1])).astype(o_ref.dtype)
        o_ref[:, 2 * W:3 * W] = total(groups[2]).astype(o_ref.dtype)
        o_ref[:, 3 * W:] = dag_ref[...]

    specs = [pl.BlockSpec((tm, W), lambda i: (i, 0))] + [_class_spec(tm, d, W) for d in DILATIONS[1:]]
    tab = pl.BlockSpec((tm, LANES), lambda i: (i, 0))
    args = [a if d == 1 else a.reshape(d, S // d, W) for grp in (dq, dk, dv) for d, a in zip(DILATIONS, grp)]
    return pl.pallas_call(
        body, name="assemble_dy", grid=(S // tm,),
        in_specs=specs * 3 + [pl.BlockSpec((tm, 2 * D_CONV), lambda i: (i, 0)), tab, tab, tab],
        out_specs=pl.BlockSpec((tm, D_IN), lambda i: (i, 0)), out_shape=_sds((S, D_IN), BF16),
        scratch_shapes=[_row_scratch(tm, W)],
        compiler_params=_params("parallel"))(*args, dag, *tables)


def _seq_specs(L, tb, col):
    nb, per, nh = L // tb, tb // HALF, L // HALF
    centre = pl.BlockSpec((tb, D_ATT), lambda r, i: (r * nb + i, col))
    prev = pl.BlockSpec((HALF, D_ATT), lambda r, i: (r * nh + jnp.maximum(i * per - 1, 0), col))
    nxt = pl.BlockSpec((HALF, D_ATT), lambda r, i: (r * nh + jnp.minimum((i + 1) * per, nh - 1), col))
    return prev, centre, nxt


def _band_mask(i, tq, L):
    shape = (tq, tq + 2 * HALF)
    c_idx = lax.broadcasted_iota(jnp.int32, shape, 0)
    w_idx = lax.broadcasted_iota(jnp.int32, shape, 1)
    diff = w_idx - c_idx
    wpos = i * tq - HALF + w_idx
    return (diff >= 0) & (diff <= 2 * HALF) & (wpos >= 0) & (wpos < L)


def _lane_groups():
    for c0 in range(0, D_ATT, LANES):
        yield slice(c0, c0 + LANES)


def _first_head(rows):
    return lax.broadcasted_iota(jnp.int32, (rows, LANES), 1) < HEAD_DIM


def _split_pair(x, first):
    zero = jnp.zeros_like(x)
    return jnp.where(first, x, zero), jnp.where(first, zero, x)


def _nt(a, b):
    return lax.dot_general(a, b, (((1,), (1,)), ((), ())), preferred_element_type=F32)


def _tn(a, b):
    return lax.dot_general(a, b, (((0,), (0,)), ((), ())), preferred_element_type=F32)


ATT_SCALE = HEAD_DIM ** -0.5


def _att_fwd(qk, v_src, d, *, name):
    S = qk.shape[0]
    L = S // d
    tq = min(ATT_BLOCK, L)
    v_arr, v_col = v_src

    def body(q_ref, kp_ref, kc_ref, kn_ref, vp_ref, vc_ref, vn_ref, o_ref, lse_ref):
        i = pl.program_id(1)
        valid = _band_mask(i, tq, L)
        q = q_ref[...] * ATT_SCALE
        kwin = jnp.concatenate([kp_ref[...], kc_ref[...], kn_ref[...]], axis=0)
        vwin = jnp.concatenate([vp_ref[...], vc_ref[...], vn_ref[...]], axis=0)
        first = _first_head(tq)
        groups = list(_lane_groups())
        heads = [(ls, t) for ls in groups for t in _split_pair(q[:, ls], first)]
        s = [jnp.where(valid, _nt(t, kwin[:, ls]), NEG_INF) for ls, t in heads]
        m = [jnp.max(t, axis=-1, keepdims=True) for t in s]
        p = [jnp.exp(t - mm) for t, mm in zip(s, m)]
        den = [jnp.sum(t, axis=-1, keepdims=True) for t in p]
        o = [jnp.dot(t.astype(BF16), vwin[:, ls], preferred_element_type=F32) * (1.0 / dd)
             for t, dd, (ls, _) in zip(p, den, heads)]
        lse = [mm + jnp.log(dd) for mm, dd in zip(m, den)]
        for g, ls in enumerate(groups):
            o_ref[:, ls] = jnp.where(first, o[2 * g], o[2 * g + 1]).astype(o_ref.dtype)
            lse_ref[:, ls] = jnp.where(first, lse[2 * g], lse[2 * g + 1])

    _, qc, _ = _seq_specs(L, tq, 0)
    kp, kc, kn = _seq_specs(L, tq, 1)
    vp, vc, vn = _seq_specs(L, tq, v_col)
    out = pl.BlockSpec((tq, D_ATT), lambda r, i: (r * (L // tq) + i, 0))
    return pl.pallas_call(
        body, name=name, grid=(d, L // tq),
        in_specs=[qc, kp, kc, kn, vp, vc, vn], out_specs=[out, out],
        out_shape=[_sds((S, D_ATT), BF16), _sds((S, D_ATT), F32)],
        compiler_params=_params("parallel", "parallel"))(qk, qk, qk, qk, v_arr, v_arr, v_arr)


def _att_combine(outs, lses, *, tm=512):
    S = outs[0].shape[0]
    tm = min(tm, S)
    dils = DILATIONS[1:]
    n_d = len(dils)

    def body(*refs):
        o_refs, l_refs = refs[0:1 + n_d], refs[1 + n_d:2 + 2 * n_d]
        att_ref, lg_ref = refs[2 + 2 * n_d:4 + 2 * n_d]
        lg_outs = refs[4 + 2 * n_d:4 + 3 * n_d]
        scr = refs[4 + 3 * n_d:]
        scr_o, scr_l, scr_lg = scr[:n_d], scr[n_d:2 * n_d], scr[2 * n_d]
        ls = [l_refs[0][...]] + [_from_classes(r, s, d) for r, s, d in zip(l_refs[1:], scr_l, dils)]
        os_ = [o_refs[0][...].astype(F32)] + [_from_classes(r, s, d) for r, s, d in zip(o_refs[1:], scr_o, dils)]
        mx = ls[0]
        for l in ls[1:]:
            mx = jnp.maximum(mx, l)
        es = [jnp.exp(l - mx) for l in ls]
        tot = es[0]
        num = es[0] * os_[0]
        for e, o in zip(es[1:], os_[1:]):
            tot = tot + e
            num = num + e * o
        att_ref[...] = (num / tot).astype(att_ref.dtype)
        lg = mx + jnp.log(tot)
        lg_ref[...] = lg
        _fill(scr_lg, lg)
        for d, out in zip(dils, lg_outs):
            _to_classes(scr_lg, out, d)

    nat = pl.BlockSpec((tm, D_ATT), lambda i: (i, 0))
    specs = [nat] + [_class_spec(tm, d, D_ATT) for d in dils]
    view = lambda arrs: [arrs[0]] + [a.reshape(d, S // d, D_ATT) for a, d in zip(arrs[1:], dils)]
    out = pl.pallas_call(
        body, name="att_combine", grid=(S // tm,), in_specs=specs * 2,
        out_specs=[nat, nat] + specs[1:],
        out_shape=[_sds((S, D_ATT), BF16), _sds((S, D_ATT), F32)] + [_sds((d, S // d, D_ATT), F32) for d in dils],
        scratch_shapes=[_row_scratch(tm, D_ATT)] * (2 * n_d + 1),
        compiler_params=_params("parallel"))(*view(list(outs)), *view(list(lses)))
    return out[0], [out[1]] + [o.reshape(S, D_ATT) for o in out[2:]]


def _att_delta(dac, att, *, tm=512):
    S = att.shape[0]
    tm = min(tm, S)
    dils = DILATIONS[1:]
    n_d = len(dils)

    def body(do_ref, o_ref, dl_ref, *rest):
        dl_outs, do_outs = rest[:n_d], rest[n_d:2 * n_d]
        scr_dl, scr_do = rest[2 * n_d:]
        do = do_ref[...].astype(F32)
        prod = do * o_ref[...].astype(F32)
        per_head = [jnp.broadcast_to(jnp.sum(prod[:, h * HEAD_DIM:(h + 1) * HEAD_DIM], axis=-1, keepdims=True),
                                     (tm, HEAD_DIM)) for h in range(ATT_HEADS)]
        dl = jnp.concatenate(per_head, axis=1)
        dl_ref[...] = dl
        _fill(scr_dl, dl)
        _fill(scr_do, do)
        for d, dlo, doo in zip(dils, dl_outs, do_outs):
            _to_classes(scr_dl, dlo, d)
            _to_classes(scr_do, doo, d)

    blk = pl.BlockSpec((tm, D_ATT), lambda i: (i, 0))
    out = pl.pallas_call(
        body, name="att_delta", grid=(S // tm,), in_specs=[blk, blk],
        out_specs=[blk] + [_class_spec(tm, d, D_ATT) for d in dils] * 2,
        out_shape=[_sds((S, D_ATT), F32)] + [_sds((d, S // d, D_ATT), F32) for d in dils]
        + [_sds((d, S // d, D_ATT), BF16) for d in dils],
        scratch_shapes=[_row_scratch(tm, D_ATT), _row_scratch(tm, D_ATT)],
        compiler_params=_params("parallel"))(dac, att)
    delta = [out[0]] + [o.reshape(S, D_ATT) for o in out[1:1 + n_d]]
    do = [None] + [o.reshape(S, D_ATT) for o in out[1 + n_d:]]
    return delta, do


def _att_bwd(qk, v_src, do_src, lg, delta, d, *, name):
    S = qk.shape[0]
    L = S // d
    tq = min(ATT_BLOCK, L)
    nb, per, nh = L // tq, tq // HALF, L // HALF
    n_blocks = d * nb
    win = tq + 2 * HALF
    lead = tq - HALF
    acc_rows = lead + win
    (v_arr, v_col), (do_arr, do_col) = v_src, do_src

    def body(q_ref, kp_ref, kc_ref, kn_ref, vp_ref, vc_ref, vn_ref, do_ref, lg_ref, dl_ref,
             dq_ref, dk_ref, dv_ref, acc_k, acc_v):
        b = pl.program_id(0)
        i = lax.rem(jnp.minimum(b, n_blocks - 1), nb)

        @pl.when(b == 0)
        def _():
            acc_k[...] = jnp.zeros_like(acc_k)
            acc_v[...] = jnp.zeros_like(acc_v)

        @pl.when(b < n_blocks)
        def _():
            valid = _band_mask(i, tq, L)
            q, do = q_ref[...] * ATT_SCALE, do_ref[...]
            kwin = jnp.concatenate([kp_ref[...], kc_ref[...], kn_ref[...]], axis=0)
            vwin = jnp.concatenate([vp_ref[...], vc_ref[...], vn_ref[...]], axis=0)
            first, first_w = _first_head(tq), _first_head(win)
            groups = list(_lane_groups())
            cols = [c for ls in groups for c in (ls.start, ls.start + HEAD_DIM)]
            lanes = [ls for ls in groups for _ in range(2)]
            qh = [t for ls in groups for t in _split_pair(q[:, ls], first)]
            doh = [t for ls in groups for t in _split_pair(do[:, ls], first)]
            s = [jnp.where(valid, _nt(t, kwin[:, ls]), NEG_INF) for t, ls in zip(qh, lanes)]
            dp = [_nt(t, vwin[:, ls]) for t, ls in zip(doh, lanes)]
            p = [jnp.exp(t - lg_ref[:, c:c + 1]) for t, c in zip(s, cols)]
            ds = [(pp * (t - dl_ref[:, c:c + 1])).astype(BF16) for pp, t, c in zip(p, dp, cols)]
            dq = [jnp.dot(t, kwin[:, ls], preferred_element_type=F32) for t, ls in zip(ds, lanes)]
            dk = [_tn(t, q[:, ls]) for t, ls in zip(ds, lanes)]
            dv = [_tn(pp.astype(BF16), do[:, ls]) for pp, ls in zip(p, lanes)]
            for g, ls in enumerate(groups):
                dq_ref[:, ls] = (jnp.where(first, dq[2 * g], dq[2 * g + 1]) * ATT_SCALE).astype(dq_ref.dtype)
                acc_k[lead:, ls] += jnp.where(first_w, dk[2 * g], dk[2 * g + 1])
                acc_v[lead:, ls] += jnp.where(first_w, dv[2 * g], dv[2 * g + 1])

        for acc, out in ((acc_k, dk_ref), (acc_v, dv_ref)):
            out[...] = acc[0:tq, :].astype(out.dtype)
            kept = acc[tq:, :]
            acc[0:acc_rows - tq, :] = kept
            acc[acc_rows - tq:, :] = jnp.zeros((tq, D_ATT), F32)

    def seq(col):
        blk = lambda b: jnp.minimum(b, n_blocks - 1)
        cls = lambda b: (blk(b) // nb) * nh
        centre = pl.BlockSpec((tq, D_ATT), lambda b: (blk(b), col))
        prev = pl.BlockSpec((HALF, D_ATT), lambda b: (cls(b) + jnp.maximum((blk(b) % nb) * per - 1, 0), col))
        nxt = pl.BlockSpec((HALF, D_ATT), lambda b: (cls(b) + jnp.minimum((blk(b) % nb + 1) * per, nh - 1), col))
        return prev, centre, nxt

    _, qc, _ = seq(0)
    kp, kc, kn = seq(1)
    vp, vc, vn = seq(v_col)
    _, doc, _ = seq(do_col)
    late = pl.BlockSpec((tq, D_ATT), lambda b: (jnp.maximum(b - 1, 0), 0))
    return pl.pallas_call(
        body, name=name, grid=(n_blocks + 1,),
        in_specs=[qc, kp, kc, kn, vp, vc, vn, doc, qc, qc], out_specs=[qc, late, late],
        out_shape=[_sds((S, D_ATT), BF16)] * 3,
        scratch_shapes=[pltpu.VMEM((acc_rows, D_ATT), F32), pltpu.VMEM((acc_rows, D_ATT), F32)],
        compiler_params=_params("arbitrary"))(qk, qk, qk, qk, v_arr, v_arr, v_arr, do_arr, lg, delta)


def _sigmoid(x):
    return 1.0 / (1.0 + jnp.exp(-x))


def _halo_specs(S, T, width, col):
    last = S // HALO - 1
    per = T // HALO
    centre = pl.BlockSpec((T, width), lambda i: (i, col))
    prev = pl.BlockSpec((HALO, width), lambda i: (jnp.maximum(i * per - 1, 0), col))
    nxt = pl.BlockSpec((HALO, width), lambda i: (jnp.minimum((i + 1) * per, last), col))
    return prev, centre, nxt


def _window_scratch(T, C):
    return pltpu.VMEM((8, T + 2 * HALO, C), F32)


def _fill_window(buf, prev, centre, nxt, T):
    buf[0, 0:HALO, :] = prev
    buf[0, HALO:HALO + T, :] = centre
    buf[0, HALO + T:, :] = nxt
    rows = T + 2 * HALO - 8
    for s in range(1, 8):
        buf[s, 0:rows, :] = buf[0, s:s + rows, :]


def _tap_reads(buf, first_off, step, r0, ls):
    by_slab = {}
    for k in range(CONV_WIDTH):
        off = first_off + step * k
        by_slab.setdefault(off % 8, []).append((k, off - off % 8))
    for s, taps in by_slab.items():
        lo = min(a for _, a in taps)
        hi = max(a for _, a in taps)
        rows = buf[s, pl.ds(lo + r0, CONV_ROWS + hi - lo), ls]
        for k, a in taps:
            yield k, rows[a - lo:a - lo + CONV_ROWS]


def _depthwise(buf, w_ref, out_ref, T, C, first_off, step):
    def row_tile(t, carry):
        r0 = pl.multiple_of(t * CONV_ROWS, CONV_ROWS)
        for c0 in range(0, C, LANES):
            ls = slice(c0, c0 + LANES)
            acc = jnp.zeros((CONV_ROWS, LANES), F32)
            for k, rows in _tap_reads(buf, first_off, step, r0, ls):
                acc = acc + rows * w_ref[k:k + 1, ls]
            out_ref[pl.ds(r0, CONV_ROWS), ls] = acc
        return carry

    lax.fori_loop(0, T // CONV_ROWS, row_tile, 0)


def _conv_fwd(y, conv_w32, conv_b, ln_g, ln_b, *, T=512):
    S = y.shape[0]
    T = min(T, S)
    nblk = S // T
    C = D_CONV

    def body(ap, ac, an, gp, gc, gn, w_ref, b_ref, lg_ref, lb_ref, cv_ref, u1_ref, buf):
        i = pl.program_id(0)

        def glu(a_ref, g_ref):
            return a_ref[...].astype(F32) * _sigmoid(g_ref[...].astype(F32))

        _fill_window(buf, jnp.where(i > 0, glu(ap, gp), 0.0), glu(ac, gc),
                     jnp.where(i < nblk - 1, glu(an, gn), 0.0), T)
        _depthwise(buf, w_ref, u1_ref, T, C, HALO - CONV_PAD, 1)
        u1 = u1_ref[...] + b_ref[...]
        u1_ref[...] = u1
        mu = jnp.mean(u1, axis=-1, keepdims=True)
        xc = u1 - mu
        rstd = lax.rsqrt(jnp.mean(xc * xc, axis=-1, keepdims=True) + EPS)
        u2 = xc * rstd * lg_ref[...] + lb_ref[...]
        cv_ref[...] = (u2 * _sigmoid(u2)).astype(cv_ref.dtype)

    ap, ac, an = _halo_specs(S, T, C, 3)
    gp, gc, gn = _halo_specs(S, T, C, 4)
    vec = pl.BlockSpec((1, C), lambda i: (0, 0))
    out = pl.BlockSpec((T, C), lambda i: (i, 0))
    return pl.pallas_call(
        body, name="conv_fwd", grid=(nblk,),
        in_specs=[ap, ac, an, gp, gc, gn, pl.BlockSpec((32, C), lambda i: (0, 0)), vec, vec, vec],
        out_specs=[out, out], out_shape=[_sds((S, C), BF16), _sds((S, C), F32)],
        scratch_shapes=[_window_scratch(T, C)],
        compiler_params=_params("parallel"))(y, y, y, y, y, y, conv_w32, conv_b, ln_g, ln_b)


def _conv_bwd(dac, u1, y, conv_w32, ln_g, ln_b, *, T=512):
    S = y.shape[0]
    T = min(T, S)
    nblk = S // T
    C = D_CONV

    def body(dp, dc, dn, up, uc, un, ap, ac, an, gp, gc, gn, w_ref, lg_ref, lb_ref,
             dag_ref, dw_ref, dsm_ref, bufd, bufu, du0_scr, dw_acc):
        i = pl.program_id(0)
        lg = lg_ref[...]

        def du1_of(dcv_ref, u1_ref):
            u1 = u1_ref[...]
            mu = jnp.mean(u1, axis=-1, keepdims=True)
            xc = u1 - mu
            rstd = lax.rsqrt(jnp.mean(xc * xc, axis=-1, keepdims=True) + EPS)
            xhat = xc * rstd
            u2 = xhat * lg + lb_ref[...]
            sg = _sigmoid(u2)
            du2 = dcv_ref[...].astype(F32) * (sg * (1.0 + u2 * (1.0 - sg)))
            dxh = du2 * lg
            du1 = rstd * (dxh - jnp.mean(dxh, axis=-1, keepdims=True)
                          - xhat * jnp.mean(dxh * xhat, axis=-1, keepdims=True))
            return du1, du2, xhat

        def glu(a_ref, g_ref):
            return a_ref[...].astype(F32) * _sigmoid(g_ref[...].astype(F32))

        @pl.when(i == 0)
        def _():
            dw_ref[...] = jnp.zeros_like(dw_ref)
            dsm_ref[...] = jnp.zeros_like(dsm_ref)

        du1_c, du2_c, xhat_c = du1_of(dc, uc)
        dsm_ref[0:1, :] += jnp.sum(du1_c, axis=0, keepdims=True)
        dsm_ref[1:2, :] += jnp.sum(du2_c * xhat_c, axis=0, keepdims=True)
        dsm_ref[2:3, :] += jnp.sum(du2_c, axis=0, keepdims=True)
        _fill_window(bufd, jnp.where(i > 0, du1_of(dp, up)[0], 0.0), du1_c,
                     jnp.where(i < nblk - 1, du1_of(dn, un)[0], 0.0), T)
        _fill_window(bufu, jnp.where(i > 0, glu(ap, gp), 0.0), glu(ac, gc),
                     jnp.where(i < nblk - 1, glu(an, gn), 0.0), T)

        _depthwise(bufd, w_ref, du0_scr, T, C, HALO + CONV_PAD, -1)
        dw_acc[...] = jnp.zeros_like(dw_acc)

        def dw_tile(t, carry):
            r0 = pl.multiple_of(t * CONV_ROWS, CONV_ROWS)
            for c0 in range(0, C, LANES):
                ls = slice(c0, c0 + LANES)
                d = bufd[0, pl.ds(HALO + r0, CONV_ROWS), ls]
                for k, rows in _tap_reads(bufu, HALO - CONV_PAD, 1, r0, ls):
                    prod = d * rows
                    part = prod[0:8]
                    for j in range(8, CONV_ROWS, 8):
                        part = part + prod[j:j + 8]
                    dw_acc[k, :, ls] += part
            return carry

        lax.fori_loop(0, T // CONV_ROWS, dw_tile, 0)
        for k in range(CONV_WIDTH):
            dw_ref[k:k + 1, :] += jnp.sum(dw_acc[k], axis=0, keepdims=True)
        du0 = du0_scr[...]
        a = ac[...].astype(F32)
        sg = _sigmoid(gc[...].astype(F32))
        dag_ref[:, 0:C] = (du0 * sg).astype(dag_ref.dtype)
        dag_ref[:, C:] = (du0 * a * sg * (1.0 - sg)).astype(dag_ref.dtype)

    dp, dc, dn = _halo_specs(S, T, C, 1)
    up, uc, un = _halo_specs(S, T, C, 0)
    ap, ac, an = _halo_specs(S, T, C, 3)
    gp, gc, gn = _halo_specs(S, T, C, 4)
    vec = pl.BlockSpec((1, C), lambda i: (0, 0))
    return pl.pallas_call(
        body, name="conv_bwd", grid=(nblk,),
        in_specs=[dp, dc, dn, up, uc, un, ap, ac, an, gp, gc, gn,
                  pl.BlockSpec((32, C), lambda i: (0, 0)), vec, vec],
        out_specs=[pl.BlockSpec((T, 2 * C), lambda i: (i, 0)), pl.BlockSpec((32, C), lambda i: (0, 0)),
                   pl.BlockSpec((8, C), lambda i: (0, 0))],
        out_shape=[_sds((S, 2 * C), BF16), _sds((32, C), F32), _sds((8, C), F32)],
        scratch_shapes=[_window_scratch(T, C), _window_scratch(T, C), pltpu.VMEM((T, C), F32),
                        pltpu.VMEM((CONV_WIDTH, 8, C), F32)],
        compiler_params=_params("arbitrary"))(dac, dac, dac, u1, u1, u1, y, y, y, y, y, y, conv_w32, ln_g, ln_b)


def _xatt_fwd(xq, xk, xv, *, tm=512):
    S = xq.shape[0]
    M = xk.shape[0]
    tm = min(tm, S)
    scale = XATT_HEAD_DIM ** -0.5

    def body(q_ref, k_ref, v_ref, o_ref):
        heads = [slice(h * XATT_HEAD_DIM, (h + 1) * XATT_HEAD_DIM) for h in range(XATT_HEADS)]
        s = [_nt(q_ref[:, sl], k_ref[:, sl]) * scale for sl in heads]
        e = [jnp.exp(t - jnp.max(t, axis=-1, keepdims=True)) for t in s]
        p = [t * (1.0 / jnp.sum(t, axis=-1, keepdims=True)) for t in e]
        for sl, t in zip(heads, p):
            o_ref[:, sl] = jnp.dot(t.astype(BF16), v_ref[:, sl], preferred_element_type=F32).astype(o_ref.dtype)

    row = pl.BlockSpec((tm, D_MODEL), lambda i: (i, 0))
    full = pl.BlockSpec((M, D_MODEL), lambda i: (0, 0))
    return pl.pallas_call(
        body, name="xatt_fwd", grid=(S // tm,), in_specs=[row, full, full], out_specs=row,
        out_shape=_sds((S, D_MODEL), BF16), compiler_params=_params("parallel"))(xq, xk, xv)


def _xatt_bwd(xq, xk, xv, dxo, *, tm=512):
    S = xq.shape[0]
    M = xk.shape[0]
    tm = min(tm, S)
    scale = XATT_HEAD_DIM ** -0.5

    def body(q_ref, k_ref, v_ref, do_ref, dq_ref, dk_ref, dv_ref):
        i = pl.program_id(0)

        @pl.when(i == 0)
        def _():
            dk_ref[...] = jnp.zeros_like(dk_ref)
            dv_ref[...] = jnp.zeros_like(dv_ref)

        heads = [slice(h * XATT_HEAD_DIM, (h + 1) * XATT_HEAD_DIM) for h in range(XATT_HEADS)]
        s = [_nt(q_ref[:, sl], k_ref[:, sl]) * scale for sl in heads]
        dp = [_nt(do_ref[:, sl], v_ref[:, sl]) for sl in heads]
        e = [jnp.exp(t - jnp.max(t, axis=-1, keepdims=True)) for t in s]
        p = [t * (1.0 / jnp.sum(t, axis=-1, keepdims=True)) for t in e]
        ds = [(pp * (t - jnp.sum(t * pp, axis=-1, keepdims=True))).astype(BF16) for pp, t in zip(p, dp)]
        for sl, pp, t in zip(heads, p, ds):
            dq_ref[:, sl] = (jnp.dot(t, k_ref[:, sl], preferred_element_type=F32) * scale).astype(dq_ref.dtype)
            dv_ref[:, sl] += _tn(pp.astype(BF16), do_ref[:, sl])
            dk_ref[:, sl] += _tn(t, q_ref[:, sl]) * scale

    row = pl.BlockSpec((tm, D_MODEL), lambda i: (i, 0))
    full = pl.BlockSpec((M, D_MODEL), lambda i: (0, 0))
    return pl.pallas_call(
        body, name="xatt_bwd", grid=(S // tm,), in_specs=[row, full, full, row], out_specs=[row, full, full],
        out_shape=[_sds((S, D_MODEL), BF16), _sds((M, D_MODEL), F32), _sds((M, D_MODEL), F32)],
        compiler_params=_params("arbitrary"))(xq, xk, xv, dxo)


def _row_tile(R):
    for t in (256, 128, 64, 32, 16, 8):
        if R % t == 0:
            return t
    return R


def _sum_partials(own, recv, me, *, name):
    _, R, C = own.shape
    t = _row_tile(R)

    def body(me_ref, own_ref, r_ref, o_ref):
        o_ref[...] = ((own_ref[...].astype(F32) + r_ref[0].astype(F32)) + r_ref[1].astype(F32)) + r_ref[2].astype(F32)

    return pl.pallas_call(
        body, name=name,
        grid_spec=pltpu.PrefetchScalarGridSpec(
            num_scalar_prefetch=1, grid=(R // t,),
            in_specs=[pl.BlockSpec((None, t, C), lambda i, me_ref: (me_ref[0], i, 0)),
                      pl.BlockSpec((3, t, C), lambda i, me_ref: (0, i, 0))],
            out_specs=pl.BlockSpec((t, C), lambda i, me_ref: (i, 0))),
        out_shape=_sds((R, C), F32), compiler_params=_params("parallel"))(me, own, recv)


def _adamw_math(w, g, m, v):
    m2 = ADAM_B1 * m + (1.0 - ADAM_B1) * g
    v2 = ADAM_B2 * v + (1.0 - ADAM_B2) * (g * g)
    m_hat = m2 / (1.0 - ADAM_B1 ** ADAM_STEP)
    v_hat = v2 / (1.0 - ADAM_B2 ** ADAM_STEP)
    delta = -ADAM_LR * (m_hat / (jnp.sqrt(v_hat) + ADAM_EPS) + ADAM_WD * w)
    return delta, m2, v2


def _adamw(parts, w, m, v, *, name):
    R, C = w.shape
    t = _row_tile(R)
    n = len(parts)

    def body(*refs):
        w_ref, m_ref, v_ref = refs[n:n + 3]
        g_ref, d_ref, m2_ref, v2_ref = refs[n + 3:]
        g = refs[0][...]
        for r in refs[1:n]:
            g = g + r[...]
        delta, m2, v2 = _adamw_math(w_ref[...], g, m_ref[...], v_ref[...])
        g_ref[...] = g
        d_ref[...] = delta
        m2_ref[...] = m2
        v2_ref[...] = v2

    blk = pl.BlockSpec((t, C), lambda i: (i, 0))
    return pl.pallas_call(
        body, name=name, grid=(R // t,), in_specs=[blk] * (n + 3), out_specs=[blk] * 4,
        out_shape=[_sds((R, C), F32)] * 4, compiler_params=_params("parallel"))(*parts, w, m, v)


def _adamw_small(gathered, chip, entries):
    _, R, C = gathered.shape
    n = len(entries)
    group = D_CONV // N_CHIPS

    def body(chip_ref, g_ref, *refs):
        ins, outs, tot_ref = refs[:3 * n], refs[3 * n:7 * n], refs[7 * n]
        tot = g_ref[0]
        for k in range(1, N_DEV):
            tot = tot + g_ref[k]
        tot_ref[...] = tot
        for e, ((kind, r), _, _, _) in enumerate(entries):
            if kind == "row":
                g = tot_ref[r:r + 1, :]
            elif kind == "gain":
                g = jnp.concatenate([tot_ref[r:r + 1, :], tot_ref[r + 1:r + 2, :]], axis=1)
            else:
                g = tot_ref[r:r + CONV_WIDTH, 0:group]
                for j in range(1, N_CHIPS):
                    g = jnp.where(chip_ref[0] == j, tot_ref[r:r + CONV_WIDTH, j * group:(j + 1) * group], g)
            delta, m2, v2 = _adamw_math(ins[3 * e][...], g, ins[3 * e + 1][...], ins[3 * e + 2][...])
            for o, val in zip(outs[4 * e:4 * e + 4], (g, delta, m2, v2)):
                o[...] = val

    whole = lambda a: pl.BlockSpec(a.shape, lambda i, c: (0,) * a.ndim)
    arrays = [a for _, w, m, v in entries for a in (w, m, v)]
    out_like = [w for _, w, _, _ in entries for _ in range(4)]
    tot_like = _sds((R, C), F32)
    out = pl.pallas_call(
        body, name="adamw_small",
        grid_spec=pltpu.PrefetchScalarGridSpec(
            num_scalar_prefetch=1, grid=(1,),
            in_specs=[whole(gathered)] + [whole(a) for a in arrays],
            out_specs=[whole(a) for a in out_like] + [whole(tot_like)]),
        out_shape=[_sds(a.shape, F32) for a in out_like] + [tot_like],
        compiler_params=_params("arbitrary"))(chip, gathered, *arrays)
    return out[-1], [tuple(out[4 * e:4 * e + 4]) for e in range(n)]


def _chip_peers():
    x, y = lax.axis_index("x"), lax.axis_index("y")
    return [(1 - x, y), (x, 1 - y), (1 - x, 1 - y)]


HBM_SPEC = pl.BlockSpec(memory_space=pltpu.HBM)
SEM_SPEC = pl.BlockSpec(memory_space=pltpu.SEMAPHORE)


def _exchange_start(mode, srcs, zones, *, name):
    n = len(srcs)

    def body(*refs):
        ins, lands = refs[:n], refs[n:2 * n]
        send_sems, recv_sems = refs[2 * n:3 * n], refs[3 * n:4 * n]
        token = refs[-1]
        c = lax.axis_index("c")
        mine = 2 * lax.axis_index("x") + lax.axis_index("y")
        for t in range(n):
            for k, (px, py) in enumerate(_chip_peers()):
                if mode == "gather":
                    s, d = ins[t], lands[t].at[mine]
                else:
                    s, d = ins[t].at[2 * px + py], lands[t].at[k]
                pltpu.make_async_remote_copy(src_ref=s, dst_ref=d, send_sem=send_sems[t], recv_sem=recv_sems[t],
                                             device_id=(px, py, c), device_id_type=MESH).start()
            if mode == "gather":
                pltpu.make_async_copy(ins[t], lands[t].at[mine], send_sems[t]).start()
        token[...] = jnp.zeros_like(token)

    hbm = lambda a: pltpu.with_memory_space_constraint(a, pltpu.HBM)
    out = pl.pallas_call(
        body, name=name,
        in_specs=[HBM_SPEC] * (2 * n),
        out_specs=[SEM_SPEC] * (2 * n) + [HBM_SPEC] * (2 * n) + [pl.BlockSpec(memory_space=pltpu.VMEM)],
        out_shape=[pltpu.SemaphoreType.DMA(())] * (2 * n)
        + [pltpu.HBM(a.shape, a.dtype) for a in list(srcs) + list(zones)] + [_sds((8, LANES), F32)],
        input_output_aliases={i: 2 * n + i for i in range(2 * n)},
        compiler_params=pltpu.CompilerParams(has_side_effects=pltpu.SideEffectType.DATAFLOW_SIDE_EFFECTING),
    )(*[hbm(a) for a in list(srcs) + list(zones)])
    return out[:n], out[n:2 * n], out[2 * n:3 * n], out[3 * n:4 * n], out[-1]


def _exchange_wait(mode, started, after, *, name):
    send_sems, recv_sems, srcs, zones, _ = started
    n = len(srcs)
    afters = tuple(after) if isinstance(after, (tuple, list)) else (after,)

    def body(*refs):
        lands = refs[n:2 * n]
        send_refs, recv_refs = refs[2 * n:3 * n], refs[3 * n:4 * n]
        me = (lax.axis_index("x"), lax.axis_index("y"), lax.axis_index("c"))
        for t in range(n):
            three = lands[t].at[pl.ds(0, N_CHIPS - 1)]
            sent = lands[t] if mode == "gather" else three
            pltpu.make_async_remote_copy(src_ref=sent, dst_ref=sent, send_sem=send_refs[t], recv_sem=recv_refs[t],
                                         device_id=me, device_id_type=MESH).wait_send()
            pltpu.make_async_remote_copy(src_ref=three, dst_ref=three, send_sem=send_refs[t], recv_sem=recv_refs[t],
                                         device_id=me, device_id_type=MESH).wait_recv()

    out = pl.pallas_call(
        body, name=name,
        in_specs=[HBM_SPEC] * (2 * n) + [SEM_SPEC] * (2 * n) + [pl.BlockSpec(memory_space=pl.ANY)] * len(afters),
        out_specs=[HBM_SPEC] * (2 * n),
        out_shape=[pltpu.HBM(a.shape, a.dtype) for a in list(srcs) + list(zones)],
        input_output_aliases={i: i for i in range(2 * n)},
        compiler_params=pltpu.CompilerParams(has_side_effects=pltpu.SideEffectType.DATAFLOW_SIDE_EFFECTING),
    )(*srcs, *zones, *send_sems, *recv_sems, *afters)
    return out[:n], out[n:]


def _allgather_small(small):
    def body(small_ref, gath_ref, send_sems, recv_sems, loc_sem):
        x, y, c = lax.axis_index("x"), lax.axis_index("y"), lax.axis_index("c")
        me = 4 * x + 2 * y + c
        flips = [(fx, fy, fc) for fx in (0, 1) for fy in (0, 1) for fc in (0, 1)][1:]

        def flipped(fx, fy, fc):
            return (1 - x if fx else x, 1 - y if fy else y, 1 - c if fc else c)

        loc = pltpu.make_async_copy(small_ref, gath_ref.at[me], loc_sem)
        loc.start()
        sends = []
        for j, flip in enumerate(flips):
            cp = pltpu.make_async_remote_copy(
                src_ref=small_ref, dst_ref=gath_ref.at[me], send_sem=send_sems.at[j], recv_sem=recv_sems.at[j],
                device_id=flipped(*flip), device_id_type=MESH)
            cp.start()
            sends.append(cp)
        for j, flip in enumerate(flips):
            px, py, pc = flipped(*flip)
            pltpu.make_async_remote_copy(
                src_ref=small_ref, dst_ref=gath_ref.at[4 * px + 2 * py + pc], send_sem=send_sems.at[j],
                recv_sem=recv_sems.at[j], device_id=(px, py, pc), device_id_type=MESH).wait_recv()
        for cp in sends:
            cp.wait_send()
        loc.wait()

    any_spec = pl.BlockSpec(memory_space=pl.ANY)
    return pl.pallas_call(
        body, name="allgather_small", in_specs=[any_spec], out_specs=any_spec,
        out_shape=_sds((N_DEV,) + small.shape, small.dtype),
        scratch_shapes=[pltpu.SemaphoreType.DMA((N_DEV - 1,)), pltpu.SemaphoreType.DMA((N_DEV - 1,)),
                        pltpu.SemaphoreType.DMA])(small)


def _swap_with_sibling(parts):
    n = len(parts)

    def body(*refs):
        ins, outs = refs[:n], refs[n:2 * n]
        send_sems, recv_sems = refs[2 * n:]
        sib = (lax.axis_index("x"), lax.axis_index("y"), 1 - lax.axis_index("c"))
        cps = []
        for t in range(n):
            cp = pltpu.make_async_remote_copy(
                src_ref=ins[t], dst_ref=outs[t], send_sem=send_sems.at[t], recv_sem=recv_sems.at[t],
                device_id=sib, device_id_type=MESH)
            cp.start()
            cps.append(cp)
        for cp in cps:
            cp.wait()

    any_spec = pl.BlockSpec(memory_space=pl.ANY)
    return pl.pallas_call(
        body, name="swap_with_sibling", in_specs=[any_spec] * n, out_specs=[any_spec] * n,
        out_shape=[_sds(p.shape, p.dtype) for p in parts],
        scratch_shapes=[pltpu.SemaphoreType.DMA((n,)), pltpu.SemaphoreType.DMA((n,))])(*parts)


BIG = ("w_in", "w_out", "w_xq", "w_xk", "w_xv", "w_xo", "w_up", "w_down")
COL_SHARDED = ("w_in", "w_up")


def _as_matrix(name, w4):
    if name in COL_SHARDED:
        return w4
    return w4.reshape(1, w4.shape[0] * w4.shape[1], w4.shape[2])


def _transposed(w3):
    nsh, K, n = w3.shape
    return jnp.swapaxes(w3, 1, 2).reshape(1, nsh * n, K)


def _shard_layout(name, g):
    if name in COL_SHARDED:
        return g
    return g.reshape(N_CHIPS, g.shape[0] * g.shape[1] // N_CHIPS, g.shape[2])


def _local_step(x, mem, target, vecs, comm):
    S = x.shape[0]
    tables = _rope_tables(S)

    xn = _rms_fwd(x, vecs["norm_mix_g"], name="rms_mix")
    w_in, conv_w32 = comm["first"]((xn,) + tuple(tables))
    y = _mm_nn(xn, w_in, name="mm_in", tm=2048, tn=640)
    qk, v_perm = _rope_fwd(y, tables)
    v_src = [(y, 2)] + [(v, 0) for v in v_perm[1:]]
    outs, lses = zip(*[_att_fwd(qk[p], v_src[p], d, name=f"att_fwd_d{d}") for p, d in enumerate(DILATIONS)])
    att, lg = _att_combine(outs, lses)
    cv, u1 = _conv_fwd(y, conv_w32, vecs["conv_b"], vecs["conv_ln_g"], vecs["conv_ln_b"])
    Wm = {k: _as_matrix(k, v) for k, v in comm["rest"]((att, cv)).items()}
    Wm["w_in"] = w_in
    h1, hn = _mm_rows((att, cv), Wm["w_out"], _residual_norm_tail, name="mm_out_rms", rows_in=(x,),
                      vecs_in=(vecs["norm_x_g"],), rows_out=(F32, BF16))
    xq = _mm_nn(hn, Wm["w_xq"], name="mm_xq")
    mn = _rms_fwd(mem, vecs["norm_mem_g"], name="rms_mem")
    xk = _mm_nn(mn, Wm["w_xk"], name="mm_xk")
    xv = _mm_nn(mn, Wm["w_xv"], name="mm_xv")
    xo = _xatt_fwd(xq, xk, xv)
    h2, hm = _mm_rows(xo, Wm["w_xo"], _residual_norm_tail, name="mm_xo_rms", rows_in=(h1,),
                      vecs_in=(vecs["norm_mlp_g"],), rows_out=(F32, BF16))
    relu_up = _mm_nn(hm, Wm["w_up"], name="mm_up", relu=True, tm=2048)
    sums = ((8, D_MODEL),)
    dh3, dh3b, dg_final, loss = _mm_rows(
        relu_up, Wm["w_down"], _loss_tail, name="mm_down_loss", rows_in=(h2, target), vecs_in=(vecs["norm_final_g"],),
        rows_out=(F32, BF16), sums_out=sums + ((8, LANES),), a_squared=True, tm=256)
    g = {}
    g["w_down"] = _mm_tn(relu_up, dh3b, 1, name="dw_down", a_squared=True)
    dup = _mm_nt(dh3b, Wm["w_down"], name="d_act", out_dtype=BF16, mul=relu_up, tm=2048)
    g["w_up"] = _mm_tn(hm, dup, N_CHIPS, name="dw_up")
    sent = comm["send_mlp"]({k: _shard_layout(k, g[k]) for k in ("w_down", "w_up")})
    dh2, dh2b, dg_mlp = _mm_rows(
        dup, _transposed(Wm["w_up"]), _rms_bwd_tail(True), name="d_hm_rms", rows_in=(h2, dh3),
        vecs_in=(vecs["norm_mlp_g"] + sent[0:1, 0:1],), rows_out=(F32, BF16), sums_out=sums, tm=256)
    g["w_xo"] = _mm_tn(xo, dh2b, 1, name="dw_xo")
    dxo = _mm_nt(dh2b, Wm["w_xo"], name="d_xo", out_dtype=BF16)
    dxq, dxk, dxv = _xatt_bwd(xq, xk, xv, dxo)
    g["w_xq"] = _mm_tn(hn, dxq, 1, name="dw_xq")
    dh1, dh1b, dg_x = _mm_rows(
        dxq, _transposed(Wm["w_xq"]), _rms_bwd_tail(True), name="d_hn_rms", rows_in=(h1, dh2),
        vecs_in=(vecs["norm_x_g"],), rows_out=(F32, BF16), sums_out=sums)
    dxkb, dxvb = dxk.astype(BF16), dxv.astype(BF16)
    g["w_xk"] = _mm_tn(mn, dxkb, 1, name="dw_xk")
    g["w_xv"] = _mm_tn(mn, dxvb, 1, name="dw_xv")
    dmn = _mm_nt(jnp.concatenate([dxkb, dxvb], axis=1),
                 jnp.concatenate([Wm["w_xk"], Wm["w_xv"]], axis=2), name="d_mn", out_dtype=BF16)
    _, _, dg_mem = _rms_bwd(dmn, mem, vecs["norm_mem_g"], None, name="rms_bwd_mem", bf16_copy=False)
    g["w_out"] = jnp.concatenate([_mm_tn(att, dh1b, 1, name="dw_out_att"), _mm_tn(cv, dh1b, 1, name="dw_out_conv")],
                                 axis=1)
    sent = comm["send_att"]({k: _shard_layout(k, g[k]) for k in ("w_out", "w_xq", "w_xk", "w_xv", "w_xo")})
    dac = _mm_nt(dh1b, Wm["w_out"], name="d_mix", out_dtype=BF16)
    dag, dconv_w, dconv_small = _conv_bwd(dac, u1, y, conv_w32, vecs["conv_ln_g"] + sent[0:1, 0:1],
                                          vecs["conv_ln_b"])
    delta, do_perm = _att_delta(dac, att)
    do_src = [(dac, 0)] + [(t, 0) for t in do_perm[1:]]
    dq, dk, dv = zip(*[_att_bwd(qk[p], v_src[p], do_src[p], lg[p], delta[p], d, name=f"att_bwd_d{d}")
                       for p, d in enumerate(DILATIONS)])
    dy = _assemble_dy(dq, dk, dv, dag, tables)
    sent = comm["send_in"]({"w_in": _mm_tn(xn, dy, N_CHIPS, name="dw_in", tn=640)})
    grad_x, dg_mix = _mm_rows(
        dy, _transposed(Wm["w_in"]), _rms_bwd_tail(False), name="d_xn_rms", rows_in=(x, dh1),
        vecs_in=(vecs["norm_mix_g"] + sent[0:1, 0:1],), rows_out=(F32,), sums_out=sums)

    small = dict(conv_w=dconv_w, conv_small=dconv_small, norm_mix_g=dg_mix, norm_x_g=dg_x, norm_mem_g=dg_mem,
                 norm_mlp_g=dg_mlp, norm_final_g=dg_final, loss=loss)
    return grad_x, small


SMALL_ORDER = ("conv_w", "conv_small", "norm_mix_g", "norm_x_g", "norm_mem_g", "norm_mlp_g", "norm_final_g", "loss")


def _pack_small(small):
    rows, offs, pos = [], {}, 0
    for k in SMALL_ORDER:
        a = small[k]
        a = a.reshape(a.shape[0] * a.shape[1] // SMALL_W, SMALL_W)
        pad = (-a.shape[0]) % 8
        if pad:
            a = jnp.pad(a, ((0, pad), (0, 0)))
        rows.append(a)
        offs[k] = pos
        pos += a.shape[0]
    return jnp.concatenate(rows, axis=0), offs


def kernel(x, mem, norm_mix_g, w_in, conv_w, conv_b, conv_ln_g, conv_ln_b, w_out, norm_x_g, norm_mem_g, w_xq, w_xk, w_xv, w_xo, norm_mlp_g, w_up, w_down, norm_final_g, loss_target, m_norm_mix_g, m_w_in, m_conv_w, m_conv_b, m_conv_ln_g, m_conv_ln_b, m_w_out, m_norm_x_g, m_norm_mem_g, m_w_xq, m_w_xk, m_w_xv, m_w_xo, m_norm_mlp_g, m_w_up, m_w_down, m_norm_final_g, v_norm_mix_g, v_w_in, v_conv_w, v_conv_b, v_conv_ln_g, v_conv_ln_b, v_w_out, v_norm_x_g, v_norm_mem_g, v_w_xq, v_w_xk, v_w_xv, v_w_xo, v_norm_mlp_g, v_w_up, v_w_down, v_norm_final_g):
    names = ("norm_mix_g", "w_in", "conv_w", "conv_b", "conv_ln_g", "conv_ln_b", "w_out", "norm_x_g", "norm_mem_g",
             "w_xq", "w_xk", "w_xv", "w_xo", "norm_mlp_g", "w_up", "w_down", "norm_final_g")
    wts = dict(zip(names, (norm_mix_g, w_in, conv_w, conv_b, conv_ln_g, conv_ln_b, w_out, norm_x_g, norm_mem_g,
                           w_xq, w_xk, w_xv, w_xo, norm_mlp_g, w_up, w_down, norm_final_g)))
    mom = dict(zip(names, (m_norm_mix_g, m_w_in, m_conv_w, m_conv_b, m_conv_ln_g, m_conv_ln_b, m_w_out, m_norm_x_g,
                           m_norm_mem_g, m_w_xq, m_w_xk, m_w_xv, m_w_xo, m_norm_mlp_g, m_w_up, m_w_down, m_norm_final_g)))
    var = dict(zip(names, (v_norm_mix_g, v_w_in, v_conv_w, v_conv_b, v_conv_ln_g, v_conv_ln_b, v_w_out, v_norm_x_g,
                           v_norm_mem_g, v_w_xq, v_w_xk, v_w_xv, v_w_xo, v_norm_mlp_g, v_w_up, v_w_down, v_norm_final_g)))
    chip = 2 * lax.axis_index("x") + lax.axis_index("y")

    def zone(shard):
        return lax.empty((N_CHIPS,) + shard.shape, shard.dtype)

    conv_w_pad = jnp.pad(wts["conv_w"][0], ((0, 1), (0, 0)))
    first_shards = [wts["w_in"][0].astype(BF16), conv_w_pad]
    gathering_first = _exchange_start("gather", first_shards, [zone(s) for s in first_shards],
                                      name="gather_first_start")
    rest = tuple(k for k in BIG if k != "w_in")
    behind_first = gathering_first[4][0, 0]
    rest_shards = [(wts[k][0] + behind_first).astype(BF16) for k in rest]
    gathering = _exchange_start("gather", rest_shards, [zone(s) for s in rest_shards], name="gather_rest_start")
    sending = {}

    def wait_first(after):
        _, (w_in_all, conv_w_all) = _exchange_wait("gather", gathering_first, after, name="gather_first_wait")
        return w_in_all, jnp.transpose(conv_w_all, (1, 0, 2)).reshape(32, D_CONV)

    def wait_rest(after):
        _, zones = _exchange_wait("gather", gathering, after, name="gather_rest_wait")
        return dict(zip(rest, zones))

    def send(group, grads):
        keys = tuple(grads)
        zones = [lax.empty((N_CHIPS - 1,) + grads[k].shape[1:], grads[k].dtype) for k in keys]
        sending[group] = (keys, _exchange_start("scatter", [grads[k] for k in keys], zones,
                                                name=f"scatter_{group}_start"))
        return sending[group][1][4]

    comm = dict(first=wait_first, rest=wait_rest, send_mlp=lambda grads: send("mlp", grads),
                send_att=lambda grads: send("att", grads), send_in=lambda grads: send("in", grads))
    vecs = {k: wts[k] for k in ("conv_b", "conv_ln_g", "conv_ln_b", "norm_x_g", "norm_mem_g", "norm_mlp_g")}
    vecs["norm_mix_g"] = wts["norm_mix_g"] + gathering[4][0:1, 0:1]
    vecs["norm_final_g"] = wts["norm_final_g"].reshape(1, D_MODEL)
    grad_x, small = _local_step(x[0], mem[0], loss_target[0], vecs, comm)

    packed, offs = _pack_small(small)
    gath = _allgather_small(packed)
    big, recv = {}, {}
    for group in ("mlp", "att", "in"):
        keys, started = sending[group]
        srcs, zones = _exchange_wait("scatter", started, gath, name=f"scatter_{group}_wait")
        big.update(zip(keys, srcs))
        recv.update(zip(keys, zones))
    me_arr = jnp.reshape(chip, (1,)).astype(jnp.int32)
    sums = [_sum_partials(big[k], recv[k], me_arr, name=f"sum_{k}") for k in BIG]
    sib = _swap_with_sibling(sums)

    res = {}
    for k, s_mine, s_sib in zip(BIG, sums, sib):
        res[k] = _adamw([s_mine, s_sib], wts[k][0], mom[k][0], var[k][0], name=f"adamw_{k}")

    where = {"conv_w": ("conv_w", offs["conv_w"]), "conv_b": ("row", offs["conv_small"]),
             "conv_ln_g": ("row", offs["conv_small"] + 1), "conv_ln_b": ("row", offs["conv_small"] + 2)}
    where.update({k: ("gain", offs[k]) for k in ("norm_mix_g", "norm_x_g", "norm_mem_g", "norm_mlp_g", "norm_final_g")})
    as_2d = lambda a: a.reshape(a.shape[-2] if a.ndim > 1 else 1, a.shape[-1])
    tot_small, updates = _adamw_small(gath, me_arr, [(where[k], as_2d(wts[k]), as_2d(mom[k]), as_2d(var[k]))
                                                     for k in where])
    res.update(zip(where, updates))
    loss = tot_small[offs["loss"], 0]

    outs = [loss, grad_x[None]]
    for j in range(4):
        outs += [res[k][j].reshape(wts[k].shape) for k in names]
    return tuple(outs)
```

```python
import jax
import jax.numpy as jnp
from jax import lax
from jax.experimental import pallas as pl
from jax.experimental.pallas import tpu as pltpu

F32 = jnp.float32
BF16 = jnp.bfloat16
MESH = pl.DeviceIdType.MESH

D_MODEL = 1024
ATT_HEADS = 8
HEAD_DIM = 64
D_ATT = ATT_HEADS * HEAD_DIM
D_CONV = D_MODEL - D_ATT
DILATIONS = (1, 4, 16)
HALF = 64
ROPE_THETA = 500000.0
ROT_DIM = HEAD_DIM // 4
CONV_WIDTH = 31
CONV_PAD = (CONV_WIDTH - 1) // 2
XATT_HEADS = 4
XATT_HEAD_DIM = D_MODEL // XATT_HEADS
D_FF = 4 * D_MODEL
D_IN = 3 * D_ATT + 2 * D_CONV
EPS = 1e-6
NEG_INF = -1e30
N_CHIPS = 4
N_DEV = 8

ADAM_LR = 0.001
ADAM_B1 = 0.9
ADAM_B2 = 0.999
ADAM_EPS = 1e-08
ADAM_WD = 0.01
ADAM_STEP = 10

VMEM_LIMIT_V7X = 56 * 1024 * 1024
LANES = 128
HALO = 16
CONV_ROWS = 64
ATT_BLOCK = 128
SMALL_W = 512


def _params(*sem):
    return pltpu.CompilerParams(dimension_semantics=sem, vmem_limit_bytes=VMEM_LIMIT_V7X)


def _sds(shape, dtype):
    return jax.ShapeDtypeStruct(shape, dtype)


def _squared(a):
    af = a.astype(F32)
    return (af * af).astype(BF16)


def _mm_nn(a, w3, *, name, out_dtype=BF16, res=None, relu=False, a_squared=False, tm=1024, tn=None, tk=1024):
    M, K = a.shape
    nsh, _, n = w3.shape
    tm, tk = min(tm, M), min(tk, K)
    tn = tn or min(n, 1024)
    npt, nk = n // tn, K // tk
    nj, N = nsh * npt, nsh * n
    n_out = 1

    def body(*refs):
        a_ref, w_ref = refs[0], refs[1]
        pos = 2
        res_ref = None
        if res is not None:
            res_ref = refs[pos]
            pos += 1
        outs = refs[pos:pos + n_out]
        acc_ref = refs[pos + n_out] if nk > 1 else None

        def finish(acc):
            if res_ref is not None:
                acc = acc + res_ref[...]
            if relu:
                acc = jnp.maximum(acc, 0.0)
            outs[0][...] = acc.astype(outs[0].dtype)

        a_val = _squared(a_ref[...]) if a_squared else a_ref[...]
        part = jnp.dot(a_val, w_ref[...], preferred_element_type=F32)
        if nk == 1:
            finish(part)
        else:
            k = pl.program_id(2)

            @pl.when(k == 0)
            def _():
                acc_ref[...] = part

            @pl.when(k > 0)
            def _():
                acc_ref[...] += part

            @pl.when(k == nk - 1)
            def _():
                finish(acc_ref[...])

    in_specs = [pl.BlockSpec((tm, tk), lambda i, j, k: (i, k)),
                pl.BlockSpec((None, tk, tn), lambda i, j, k: (j // npt, k, j % npt))]
    args = [a, w3]
    if res is not None:
        in_specs.append(pl.BlockSpec((tm, tn), lambda i, j, k: (i, j)))
        args.append(res)
    out_spec = pl.BlockSpec((tm, tn), lambda i, j, k: (i, j))
    out = pl.pallas_call(
        body, name=name, grid=(M // tm, nj, nk), in_specs=in_specs,
        out_specs=[out_spec] * n_out, out_shape=[_sds((M, N), out_dtype)] * n_out,
        scratch_shapes=[pltpu.VMEM((tm, tn), F32)] if nk > 1 else [],
        compiler_params=_params("parallel", "parallel", "arbitrary"))(*args)
    return out[0]


def _mm_nt(dy, w3, *, name, out_dtype=F32, mul=None, tm=1024, tn=None, tko=1024):
    M, N = dy.shape
    nsh, K, n = w3.shape
    tm, tko = min(tm, M), min(tko, K)
    tn = tn or min(n, 1024)
    npt = n // tn
    nj = nsh * npt

    def body(*refs):
        dy_ref, w_ref = refs[0], refs[1]
        pos = 2
        mul_ref = None
        if mul is not None:
            mul_ref = refs[pos]
            pos += 1
        out_ref = refs[pos]
        acc_ref = refs[pos + 1] if nj > 1 else None

        def finish(acc):
            if mul_ref is not None:
                acc = acc * (2.0 * mul_ref[...].astype(F32))
            out_ref[...] = acc.astype(out_ref.dtype)

        part = lax.dot_general(dy_ref[...], w_ref[...], (((1,), (1,)), ((), ())), preferred_element_type=F32)
        if nj == 1:
            finish(part)
        else:
            j = pl.program_id(2)

            @pl.when(j == 0)
            def _():
                acc_ref[...] = part

            @pl.when(j > 0)
            def _():
                acc_ref[...] += part

            @pl.when(j == nj - 1)
            def _():
                finish(acc_ref[...])

    in_specs = [pl.BlockSpec((tm, tn), lambda i, ko, j: (i, j)),
                pl.BlockSpec((None, tko, tn), lambda i, ko, j: (j // npt, ko, j % npt))]
    args = [dy, w3]
    if mul is not None:
        in_specs.append(pl.BlockSpec((tm, tko), lambda i, ko, j: (i, ko)))
        args.append(mul)
    return pl.pallas_call(
        body, name=name, grid=(M // tm, K // tko, nj), in_specs=in_specs,
        out_specs=pl.BlockSpec((tm, tko), lambda i, ko, j: (i, ko)), out_shape=_sds((M, K), out_dtype),
        scratch_shapes=[pltpu.VMEM((tm, tko), F32)] if nj > 1 else [],
        compiler_params=_params("parallel", "parallel", "arbitrary"))(*args)


def _mm_tn(a, dy, nsh, *, name, out_dtype=BF16, a_squared=False, tm=2048, tk=1024, tn=None):
    M, K = a.shape
    N = dy.shape[1]
    n = N // nsh
    tm, tk = min(tm, M), min(tk, K)
    tn = tn or min(n, 1024)
    npt = n // tn
    nj, nm = nsh * npt, M // tm

    def body(a_ref, dy_ref, out_ref, acc_ref):
        m = pl.program_id(2)
        a_val = _squared(a_ref[...]) if a_squared else a_ref[...]
        part = lax.dot_general(a_val, dy_ref[...], (((0,), (0,)), ((), ())), preferred_element_type=F32)

        @pl.when(m == 0)
        def _():
            acc_ref[...] = part

        @pl.when(m > 0)
        def _():
            acc_ref[...] += part

        @pl.when(m == nm - 1)
        def _():
            out_ref[...] = acc_ref[...].astype(out_ref.dtype)

    return pl.pallas_call(
        body, name=name, grid=(K // tk, nj, nm),
        in_specs=[pl.BlockSpec((tm, tk), lambda kk, j, m: (m, kk)),
                  pl.BlockSpec((tm, tn), lambda kk, j, m: (m, j))],
        out_specs=pl.BlockSpec((None, tk, tn), lambda kk, j, m: (j // npt, kk, j % npt)),
        out_shape=_sds((nsh, K, n), out_dtype),
        scratch_shapes=[pltpu.VMEM((tk, tn), F32)],
        compiler_params=_params("parallel", "parallel", "arbitrary"))(a, dy)


def _rms_fwd(x, g, *, name, tm=512):
    M, Dm = x.shape
    tm = min(tm, M)

    def body(x_ref, g_ref, o_ref):
        xf = x_ref[...]
        r = lax.rsqrt(jnp.mean(xf * xf, axis=-1, keepdims=True) + EPS)
        o_ref[...] = (xf * r * g_ref[...]).astype(o_ref.dtype)

    return pl.pallas_call(
        body, name=name, grid=(M // tm,),
        in_specs=[pl.BlockSpec((tm, Dm), lambda i: (i, 0)), pl.BlockSpec((1, Dm), lambda i: (0, 0))],
        out_specs=pl.BlockSpec((tm, Dm), lambda i: (i, 0)), out_shape=_sds((M, Dm), BF16),
        compiler_params=_params("parallel"))(x, g)


def _rms_bwd(dxn, x, g, dres, *, name, bf16_copy=True, tm=512):
    M, Dm = x.shape
    tm = min(tm, M)
    has_res = dres is not None

    def body(*refs):
        dxn_ref, x_ref, g_ref = refs[:3]
        dres_ref = refs[3] if has_res else None
        dx_ref, dg_ref = refs[-1 - 1 - bf16_copy], refs[-1]
        dxb_ref = refs[-2] if bf16_copy else None
        i = pl.program_id(0)
        xf = x_ref[...]
        r = lax.rsqrt(jnp.mean(xf * xf, axis=-1, keepdims=True) + EPS)
        nrm = xf * r
        dxn_f = dxn_ref[...].astype(F32)
        dn = dxn_f * g_ref[...]
        dx = r * (dn - nrm * jnp.mean(dn * nrm, axis=-1, keepdims=True))
        if has_res:
            dx = dx + dres_ref[...]
        dx_ref[...] = dx
        if bf16_copy:
            dxb_ref[...] = dx.astype(dxb_ref.dtype)

        @pl.when(i == 0)
        def _():
            dg_ref[...] = jnp.zeros_like(dg_ref)

        dg_ref[0:1, :] += jnp.sum(dxn_f * nrm, axis=0, keepdims=True)

    row = pl.BlockSpec((tm, Dm), lambda i: (i, 0))
    in_specs = [row, row, pl.BlockSpec((1, Dm), lambda i: (0, 0))] + ([row] if has_res else [])
    args = [dxn, x, g] + ([dres] if has_res else [])
    out = pl.pallas_call(
        body, name=name, grid=(M // tm,), in_specs=in_specs,
        out_specs=[row] * (1 + bf16_copy) + [pl.BlockSpec((8, Dm), lambda i: (0, 0))],
        out_shape=[_sds((M, Dm), F32)] + [_sds((M, Dm), BF16)] * bf16_copy + [_sds((8, Dm), F32)],
        compiler_params=_params("arbitrary"))(*args)
    return out[0], (out[1] if bf16_copy else None), out[-1]


def _mm_rows(a, w3, tail, *, name, rows_in=(), vecs_in=(), rows_out=(), sums_out=(), a_squared=False, tm=512):
    parts = a if isinstance(a, (tuple, list)) else (a,)
    M = parts[0].shape[0]
    K, N = w3.shape[1], w3.shape[2]
    tm = min(tm, M)
    n_a, n_ri, n_vi, n_ro = len(parts), len(rows_in), len(vecs_in), len(rows_out)

    def body(*refs):
        a_refs, w_ref, refs = refs[:n_a], refs[n_a], refs[n_a + 1:]
        rin, vin = refs[:n_ri], refs[n_ri:n_ri + n_vi]
        rout, sout = refs[n_ri + n_vi:n_ri + n_vi + n_ro], refs[n_ri + n_vi + n_ro:]

        @pl.when(pl.program_id(0) == 0)
        def _():
            for s in sout:
                s[...] = jnp.zeros_like(s)

        a_val = a_refs[0][...] if n_a == 1 else jnp.concatenate([r[...] for r in a_refs], axis=1)
        if a_squared:
            a_val = _squared(a_val)
        tail(jnp.dot(a_val, w_ref[0], preferred_element_type=F32), rin, vin, rout, sout)

    row = pl.BlockSpec((tm, N), lambda i: (i, 0))
    once = lambda shape: pl.BlockSpec(shape, lambda i: (0,) * len(shape))
    return pl.pallas_call(
        body, name=name, grid=(M // tm,),
        in_specs=[pl.BlockSpec((tm, p.shape[1]), lambda i: (i, 0)) for p in parts] + [once((1, K, N))]
        + [row] * n_ri + [once((1, N))] * n_vi,
        out_specs=[row] * n_ro + [once(s) for s in sums_out],
        out_shape=[_sds((M, N), dt) for dt in rows_out] + [_sds(s, F32) for s in sums_out],
        compiler_params=_params("arbitrary"))(*parts, w3, *rows_in, *vecs_in)


def _residual_norm_tail(prod, rows_in, vecs_in, rows_out, sums_out):
    hf = prod + rows_in[0][...]
    rows_out[0][...] = hf
    r = lax.rsqrt(jnp.mean(hf * hf, axis=-1, keepdims=True) + EPS)
    rows_out[1][...] = (hf * r * vecs_in[0][...]).astype(BF16)


def _rms_bwd_tail(bf16_copy):
    def tail(dxn, rows_in, vecs_in, rows_out, sums_out):
        xf = rows_in[0][...]
        r = lax.rsqrt(jnp.mean(xf * xf, axis=-1, keepdims=True) + EPS)
        nrm = xf * r
        dn = dxn * vecs_in[0][...]
        dx = r * (dn - nrm * jnp.mean(dn * nrm, axis=-1, keepdims=True)) + rows_in[1][...]
        rows_out[0][...] = dx
        if bf16_copy:
            rows_out[1][...] = dx.astype(BF16)
        sums_out[0][0:1, :] += jnp.sum(dxn * nrm, axis=0, keepdims=True)

    return tail


def _loss_tail(prod, rows_in, vecs_in, rows_out, sums_out):
    hf = prod + rows_in[0][...]
    r = lax.rsqrt(jnp.mean(hf * hf, axis=-1, keepdims=True) + EPS)
    nrm = hf * r
    gv = vecs_in[0][...]
    err = nrm * gv - rows_in[1][...]
    dy = err * (1.0 / hf.shape[-1])
    dn = dy * gv
    dh = r * (dn - nrm * jnp.mean(dn * nrm, axis=-1, keepdims=True))
    rows_out[0][...] = dh
    rows_out[1][...] = dh.astype(BF16)
    sums_out[0][0:1, :] += jnp.sum(dy * nrm, axis=0, keepdims=True)
    part = 0.5 * jnp.sum(jnp.mean(err * err, axis=-1, keepdims=True), axis=0, keepdims=True)
    sel = (lax.broadcasted_iota(jnp.int32, (8, 128), 0) == 0) & (lax.broadcasted_iota(jnp.int32, (8, 128), 1) == 0)
    sums_out[1][...] += jnp.where(sel, part, 0.0)


def _class_spec(tm, d, width):
    return pl.BlockSpec((d, tm // d, width), lambda i: (0, i, 0))


def _row_scratch(tm, width):
    return pltpu.VMEM((width // LANES, tm, LANES), F32)


def _fill(scr, val):
    for c in range(scr.shape[0]):
        scr[c] = val[:, c * LANES:(c + 1) * LANES]


def _to_classes(scr, out_ref, d):
    n = scr.shape[1] // d
    for r in range(d):
        for c in range(scr.shape[0]):
            out_ref[r, :, c * LANES:(c + 1) * LANES] = scr[c, pl.ds(r, n, stride=d), :].astype(out_ref.dtype)


def _from_classes(in_ref, scr, d):
    n = scr.shape[1] // d
    for r in range(d):
        blk = in_ref[r].astype(F32)
        for c in range(scr.shape[0]):
            scr[c, pl.ds(r, n, stride=d), :] = blk[:, c * LANES:(c + 1) * LANES]
    return jnp.concatenate([scr[c] for c in range(scr.shape[0])], axis=1)


def _rope_tables(S):
    half = ROT_DIM // 2
    freqs = ROPE_THETA ** (-jnp.arange(0, ROT_DIM, 2, dtype=F32) / ROT_DIM)
    ang = jnp.arange(S, dtype=F32)[:, None] * freqs[None, :]
    cos, sin = jnp.cos(ang), jnp.sin(ang)
    ones = jnp.ones((S, HEAD_DIM - ROT_DIM), F32)
    zeros = jnp.zeros((S, HEAD_DIM - ROT_DIM), F32)
    zh = jnp.zeros((S, half), F32)
    c = jnp.concatenate([cos, cos, ones], axis=1)
    sa = jnp.concatenate([-sin, zh, zeros], axis=1)
    sb = jnp.concatenate([zh, sin, zeros], axis=1)
    return tuple(jnp.tile(t, (1, LANES // HEAD_DIM)) for t in (c, sa, sb))


def _rope_fwd(y, tables, *, tm=512):
    S = y.shape[0]
    W = 2 * D_ATT
    tm = min(tm, S)
    half = ROT_DIM // 2
    dils = [d for d in DILATIONS if d > 1]

    def body(y_ref, c_ref, sa_ref, sb_ref, qk_ref, *rest):
        qk_outs, v_outs = rest[:len(dils)], rest[len(dils):2 * len(dils)]
        scr_qk, scr_v = rest[2 * len(dils):]
        t = y_ref[:, 0:W].astype(F32)
        rep = W // LANES
        c, sa, sb = (jnp.tile(r[...], (1, rep)) for r in (c_ref, sa_ref, sb_ref))
        rot = t * c + pltpu.roll(t, W - half, axis=1) * sa + pltpu.roll(t, half, axis=1) * sb
        qk_ref[...] = rot.astype(qk_ref.dtype)
        _fill(scr_qk, rot)
        _fill(scr_v, y_ref[:, W:W + D_ATT].astype(F32))
        for d, qo, vo in zip(dils, qk_outs, v_outs):
            _to_classes(scr_qk, qo, d)
            _to_classes(scr_v, vo, d)

    tab = pl.BlockSpec((tm, LANES), lambda i: (i, 0))
    out = pl.pallas_call(
        body, name="rope_fwd", grid=(S // tm,),
        in_specs=[pl.BlockSpec((tm, 3 * D_ATT), lambda i: (i, 0)), tab, tab, tab],
        out_specs=[pl.BlockSpec((tm, W), lambda i: (i, 0))] + [_class_spec(tm, d, W) for d in dils]
        + [_class_spec(tm, d, D_ATT) for d in dils],
        out_shape=[_sds((S, W), BF16)] + [_sds((d, S // d, W), BF16) for d in dils]
        + [_sds((d, S // d, D_ATT), BF16) for d in dils],
        scratch_shapes=[_row_scratch(tm, W), _row_scratch(tm, D_ATT)],
        compiler_params=_params("parallel"))(y, *tables)
    qk = [out[0]] + [o.reshape(S, W) for o in out[1:1 + len(dils)]]
    v = [None] + [o.reshape(S, D_ATT) for o in out[1 + len(dils):]]
    return qk, v


def _assemble_dy(dq, dk, dv, dag, tables, *, tm=512):
    S = dag.shape[0]
    tm = min(tm, S)
    half = ROT_DIM // 2
    W = D_ATT
    n_pat = len(DILATIONS)

    def body(*refs):
        groups = [refs[g * n_pat:(g + 1) * n_pat] for g in range(3)]
        dag_ref, c_ref, sa_ref, sb_ref, o_ref, scr = refs[3 * n_pat:]
        rep = W // LANES
        c, sa, sb = (jnp.tile(r[...], (1, rep)) for r in (c_ref, sa_ref, sb_ref))

        def total(rs):
            acc = rs[0][...].astype(F32)
            for d, r in zip(DILATIONS[1:], rs[1:]):
                acc = acc + _from_classes(r, scr, d)
            return acc

        def unrope(dr):
            return dr * c + pltpu.roll(dr * sa, half, axis=1) + pltpu.roll(dr * sb, W - half, axis=1)

        o_ref[:, 0:W] = unrope(total(groups[0])).astype(o_ref.dtype)
        o_ref[:, W:2 * W] = unrope(total(groups[1])).astype(o_ref.dtype)
        o_ref[:, 2 * W:3 * W] = total(groups[2]).astype(o_ref.dtype)
        o_ref[:, 3 * W:] = dag_ref[...]

    specs = [pl.BlockSpec((tm, W), lambda i: (i, 0))] + [_class_spec(tm, d, W) for d in DILATIONS[1:]]
    tab = pl.BlockSpec((tm, LANES), lambda i: (i, 0))
    args = [a if d == 1 else a.reshape(d, S // d, W) for grp in (dq, dk, dv) for d, a in zip(DILATIONS, grp)]
    return pl.pallas_call(
        body, name="assemble_dy", grid=(S // tm,),
        in_specs=specs * 3 + [pl.BlockSpec((tm, 2 * D_CONV), lambda i: (i, 0)), tab, tab, tab],
        out_specs=pl.BlockSpec((tm, D_IN), lambda i: (i, 0)), out_shape=_sds((S, D_IN), BF16),
        scratch_shapes=[_row_scratch(tm, W)],
        compiler_params=_params("parallel"))(*args, dag, *tables)


def _seq_specs(L, tb, col):
    nb, per, nh = L // tb, tb // HALF, L // HALF
    centre = pl.BlockSpec((tb, D_ATT), lambda r, i: (r * nb + i, col))
    prev = pl.BlockSpec((HALF, D_ATT), lambda r, i: (r * nh + jnp.maximum(i * per - 1, 0), col))
    nxt = pl.BlockSpec((HALF, D_ATT), lambda r, i: (r * nh + jnp.minimum((i + 1) * per, nh - 1), col))
    return prev, centre, nxt


def _band_mask(i, tq, L):
    shape = (tq, tq + 2 * HALF)
    c_idx = lax.broadcasted_iota(jnp.int32, shape, 0)
    w_idx = lax.broadcasted_iota(jnp.int32, shape, 1)
    diff = w_idx - c_idx
    wpos = i * tq - HALF + w_idx
    return (diff >= 0) & (diff <= 2 * HALF) & (wpos >= 0) & (wpos < L)


def _lane_groups():
    for c0 in range(0, D_ATT, LANES):
        yield slice(c0, c0 + LANES)


def _first_head(rows):
    return lax.broadcasted_iota(jnp.int32, (rows, LANES), 1) < HEAD_DIM


def _split_pair(x, first):
    zero = jnp.zeros_like(x)
    return jnp.where(first, x, zero), jnp.where(first, zero, x)


def _nt(a, b):
    return lax.dot_general(a, b, (((1,), (1,)), ((), ())), preferred_element_type=F32)


def _tn(a, b):
    return lax.dot_general(a, b, (((0,), (0,)), ((), ())), preferred_element_type=F32)


ATT_SCALE = HEAD_DIM ** -0.5


def _att_fwd(qk, v_src, d, *, name):
    S = qk.shape[0]
    L = S // d
    tq = min(ATT_BLOCK, L)
    v_arr, v_col = v_src

    def body(q_ref, kp_ref, kc_ref, kn_ref, vp_ref, vc_ref, vn_ref, o_ref, lse_ref):
        i = pl.program_id(1)
        valid = _band_mask(i, tq, L)
        q = q_ref[...] * ATT_SCALE
        kwin = jnp.concatenate([kp_ref[...], kc_ref[...], kn_ref[...]], axis=0)
        vwin = jnp.concatenate([vp_ref[...], vc_ref[...], vn_ref[...]], axis=0)
        first = _first_head(tq)
        groups = list(_lane_groups())
        heads = [(ls, t) for ls in groups for t in _split_pair(q[:, ls], first)]
        s = [jnp.where(valid, _nt(t, kwin[:, ls]), NEG_INF) for ls, t in heads]
        m = [jnp.max(t, axis=-1, keepdims=True) for t in s]
        p = [jnp.exp(t - mm) for t, mm in zip(s, m)]
        den = [jnp.sum(t, axis=-1, keepdims=True) for t in p]
        o = [jnp.dot(t.astype(BF16), vwin[:, ls], preferred_element_type=F32) * (1.0 / dd)
             for t, dd, (ls, _) in zip(p, den, heads)]
        lse = [mm + jnp.log(dd) for mm, dd in zip(m, den)]
        for g, ls in enumerate(groups):
            o_ref[:, ls] = jnp.where(first, o[2 * g], o[2 * g + 1]).astype(o_ref.dtype)
            lse_ref[:, ls] = jnp.where(first, lse[2 * g], lse[2 * g + 1])

    _, qc, _ = _seq_specs(L, tq, 0)
    kp, kc, kn = _seq_specs(L, tq, 1)
    vp, vc, vn = _seq_specs(L, tq, v_col)
    out = pl.BlockSpec((tq, D_ATT), lambda r, i: (r * (L // tq) + i, 0))
    return pl.pallas_call(
        body, name=name, grid=(d, L // tq),
        in_specs=[qc, kp, kc, kn, vp, vc, vn], out_specs=[out, out],
        out_shape=[_sds((S, D_ATT), BF16), _sds((S, D_ATT), F32)],
        compiler_params=_params("parallel", "parallel"))(qk, qk, qk, qk, v_arr, v_arr, v_arr)


def _att_combine(outs, lses, *, tm=512):
    S = outs[0].shape[0]
    tm = min(tm, S)
    dils = DILATIONS[1:]
    n_d = len(dils)

    def body(*refs):
        o_refs, l_refs = refs[0:1 + n_d], refs[1 + n_d:2 + 2 * n_d]
        att_ref, lg_ref = refs[2 + 2 * n_d:4 + 2 * n_d]
        lg_outs = refs[4 + 2 * n_d:4 + 3 * n_d]
        scr = refs[4 + 3 * n_d:]
        scr_o, scr_l, scr_lg = scr[:n_d], scr[n_d:2 * n_d], scr[2 * n_d]
        ls = [l_refs[0][...]] + [_from_classes(r, s, d) for r, s, d in zip(l_refs[1:], scr_l, dils)]
        os_ = [o_refs[0][...].astype(F32)] + [_from_classes(r, s, d) for r, s, d in zip(o_refs[1:], scr_o, dils)]
        mx = ls[0]
        for l in ls[1:]:
            mx = jnp.maximum(mx, l)
        es = [jnp.exp(l - mx) for l in ls]
        tot = es[0]
        num = es[0] * os_[0]
        for e, o in zip(es[1:], os_[1:]):
            tot = tot + e
            num = num + e * o
        att_ref[...] = (num / tot).astype(att_ref.dtype)
        lg = mx + jnp.log(tot)
        lg_ref[...] = lg
        _fill(scr_lg, lg)
        for d, out in zip(dils, lg_outs):
            _to_classes(scr_lg, out, d)

    nat = pl.BlockSpec((tm, D_ATT), lambda i: (i, 0))
    specs = [nat] + [_class_spec(tm, d, D_ATT) for d in dils]
    view = lambda arrs: [arrs[0]] + [a.reshape(d, S // d, D_ATT) for a, d in zip(arrs[1:], dils)]
    out = pl.pallas_call(
        body, name="att_combine", grid=(S // tm,), in_specs=specs * 2,
        out_specs=[nat, nat] + specs[1:],
        out_shape=[_sds((S, D_ATT), BF16), _sds((S, D_ATT), F32)] + [_sds((d, S // d, D_ATT), F32) for d in dils],
        scratch_shapes=[_row_scratch(tm, D_ATT)] * (2 * n_d + 1),
        compiler_params=_params("parallel"))(*view(list(outs)), *view(list(lses)))
    return out[0], [out[1]] + [o.reshape(S, D_ATT) for o in out[2:]]


def _att_delta(dac, att, *, tm=512):
    S = att.shape[0]
    tm = min(tm, S)
    dils = DILATIONS[1:]
    n_d = len(dils)

    def body(do_ref, o_ref, dl_ref, *rest):
        dl_outs, do_outs = rest[:n_d], rest[n_d:2 * n_d]
        scr_dl, scr_do = rest[2 * n_d:]
        do = do_ref[...].astype(F32)
        prod = do * o_ref[...].astype(F32)
        per_head = [jnp.broadcast_to(jnp.sum(prod[:, h * HEAD_DIM:(h + 1) * HEAD_DIM], axis=-1, keepdims=True),
                                     (tm, HEAD_DIM)) for h in range(ATT_HEADS)]
        dl = jnp.concatenate(per_head, axis=1)
        dl_ref[...] = dl
        _fill(scr_dl, dl)
        _fill(scr_do, do)
        for d, dlo, doo in zip(dils, dl_outs, do_outs):
            _to_classes(scr_dl, dlo, d)
            _to_classes(scr_do, doo, d)

    blk = pl.BlockSpec((tm, D_ATT), lambda i: (i, 0))
    out = pl.pallas_call(
        body, name="att_delta", grid=(S // tm,), in_specs=[blk, blk],
        out_specs=[blk] + [_class_spec(tm, d, D_ATT) for d in dils] * 2,
        out_shape=[_sds((S, D_ATT), F32)] + [_sds((d, S // d, D_ATT), F32) for d in dils]
        + [_sds((d, S // d, D_ATT), BF16) for d in dils],
        scratch_shapes=[_row_scratch(tm, D_ATT), _row_scratch(tm, D_ATT)],
        compiler_params=_params("parallel"))(dac, att)
    delta = [out[0]] + [o.reshape(S, D_ATT) for o in out[1:1 + n_d]]
    do = [None] + [o.reshape(S, D_ATT) for o in out[1 + n_d:]]
    return delta, do


def _att_bwd(qk, v_src, do_src, lg, delta, d, *, name):
    S = qk.shape[0]
    L = S // d
    tq = min(ATT_BLOCK, L)
    nb, per, nh = L // tq, tq // HALF, L // HALF
    n_blocks = d * nb
    win = tq + 2 * HALF
    lead = tq - HALF
    acc_rows = lead + win
    (v_arr, v_col), (do_arr, do_col) = v_src, do_src

    def body(q_ref, kp_ref, kc_ref, kn_ref, vp_ref, vc_ref, vn_ref, do_ref, lg_ref, dl_ref,
             dq_ref, dk_ref, dv_ref, acc_k, acc_v):
        b = pl.program_id(0)
        i = lax.rem(jnp.minimum(b, n_blocks - 1), nb)

        @pl.when(b == 0)
        def _():
            acc_k[...] = jnp.zeros_like(acc_k)
            acc_v[...] = jnp.zeros_like(acc_v)

        @pl.when(b < n_blocks)
        def _():
            valid = _band_mask(i, tq, L)
            q, do = q_ref[...] * ATT_SCALE, do_ref[...]
            kwin = jnp.concatenate([kp_ref[...], kc_ref[...], kn_ref[...]], axis=0)
            vwin = jnp.concatenate([vp_ref[...], vc_ref[...], vn_ref[...]], axis=0)
            first, first_w = _first_head(tq), _first_head(win)
            groups = list(_lane_groups())
            cols = [c for ls in groups for c in (ls.start, ls.start + HEAD_DIM)]
            lanes = [ls for ls in groups for _ in range(2)]
            qh = [t for ls in groups for t in _split_pair(q[:, ls], first)]
            doh = [t for ls in groups for t in _split_pair(do[:, ls], first)]
            s = [jnp.where(valid, _nt(t, kwin[:, ls]), NEG_INF) for t, ls in zip(qh, lanes)]
            dp = [_nt(t, vwin[:, ls]) for t, ls in zip(doh, lanes)]
            p = [jnp.exp(t - lg_ref[:, c:c + 1]) for t, c in zip(s, cols)]
            ds = [(pp * (t - dl_ref[:, c:c + 1])).astype(BF16) for pp, t, c in zip(p, dp, cols)]
            dq = [jnp.dot(t, kwin[:, ls], preferred_element_type=F32) for t, ls in zip(ds, lanes)]
            dk = [_tn(t, q[:, ls]) for t, ls in zip(ds, lanes)]
            dv = [_tn(pp.astype(BF16), do[:, ls]) for pp, ls in zip(p, lanes)]
            for g, ls in enumerate(groups):
                dq_ref[:, ls] = (jnp.where(first, dq[2 * g], dq[2 * g + 1]) * ATT_SCALE).astype(dq_ref.dtype)
                acc_k[lead:, ls] += jnp.where(first_w, dk[2 * g], dk[2 * g + 1])
                acc_v[lead:, ls] += jnp.where(first_w, dv[2 * g], dv[2 * g + 1])

        for acc, out in ((acc_k, dk_ref), (acc_v, dv_ref)):
            out[...] = acc[0:tq, :].astype(out.dtype)
            kept = acc[tq:, :]
            acc[0:acc_rows - tq, :] = kept
            acc[acc_rows - tq:, :] = jnp.zeros((tq, D_ATT), F32)

    def seq(col):
        blk = lambda b: jnp.minimum(b, n_blocks - 1)
        cls = lambda b: (blk(b) // nb) * nh
        centre = pl.BlockSpec((tq, D_ATT), lambda b: (blk(b), col))
        prev = pl.BlockSpec((HALF, D_ATT), lambda b: (cls(b) + jnp.maximum((blk(b) % nb) * per - 1, 0), col))
        nxt = pl.BlockSpec((HALF, D_ATT), lambda b: (cls(b) + jnp.minimum((blk(b) % nb + 1) * per, nh - 1), col))
        return prev, centre, nxt

    _, qc, _ = seq(0)
    kp, kc, kn = seq(1)
    vp, vc, vn = seq(v_col)
    _, doc, _ = seq(do_col)
    late = pl.BlockSpec((tq, D_ATT), lambda b: (jnp.maximum(b - 1, 0), 0))
    return pl.pallas_call(
        body, name=name, grid=(n_blocks + 1,),
        in_specs=[qc, kp, kc, kn, vp, vc, vn, doc, qc, qc], out_specs=[qc, late, late],
        out_shape=[_sds((S, D_ATT), BF16)] * 3,
        scratch_shapes=[pltpu.VMEM((acc_rows, D_ATT), F32), pltpu.VMEM((acc_rows, D_ATT), F32)],
        compiler_params=_params("arbitrary"))(qk, qk, qk, qk, v_arr, v_arr, v_arr, do_arr, lg, delta)


def _sigmoid(x):
    return 1.0 / (1.0 + jnp.exp(-x))


def _halo_specs(S, T, width, col):
    last = S // HALO - 1
    per = T // HALO
    centre = pl.BlockSpec((T, width), lambda i: (i, col))
    prev = pl.BlockSpec((HALO, width), lambda i: (jnp.maximum(i * per - 1, 0), col))
    nxt = pl.BlockSpec((HALO, width), lambda i: (jnp.minimum((i + 1) * per, last), col))
    return prev, centre, nxt


def _window_scratch(T, C):
    return pltpu.VMEM((8, T + 2 * HALO, C), F32)


def _fill_window(buf, prev, centre, nxt, T):
    buf[0, 0:HALO, :] = prev
    buf[0, HALO:HALO + T, :] = centre
    buf[0, HALO + T:, :] = nxt
    rows = T + 2 * HALO - 8
    for s in range(1, 8):
        buf[s, 0:rows, :] = buf[0, s:s + rows, :]


def _tap_reads(buf, first_off, step, r0, ls):
    by_slab = {}
    for k in range(CONV_WIDTH):
        off = first_off + step * k
        by_slab.setdefault(off % 8, []).append((k, off - off % 8))
    for s, taps in by_slab.items():
        lo = min(a for _, a in taps)
        hi = max(a for _, a in taps)
        rows = buf[s, pl.ds(lo + r0, CONV_ROWS + hi - lo), ls]
        for k, a in taps:
            yield k, rows[a - lo:a - lo + CONV_ROWS]


def _depthwise(buf, w_ref, out_ref, T, C, first_off, step):
    def row_tile(t, carry):
        r0 = pl.multiple_of(t * CONV_ROWS, CONV_ROWS)
        for c0 in range(0, C, LANES):
            ls = slice(c0, c0 + LANES)
            acc = jnp.zeros((CONV_ROWS, LANES), F32)
            for k, rows in _tap_reads(buf, first_off, step, r0, ls):
                acc = acc + rows * w_ref[k:k + 1, ls]
            out_ref[pl.ds(r0, CONV_ROWS), ls] = acc
        return carry

    lax.fori_loop(0, T // CONV_ROWS, row_tile, 0)


def _conv_fwd(y, conv_w32, conv_b, ln_g, ln_b, *, T=512):
    S = y.shape[0]
    T = min(T, S)
    nblk = S // T
    C = D_CONV

    def body(ap, ac, an, gp, gc, gn, w_ref, b_ref, lg_ref, lb_ref, cv_ref, u1_ref, buf):
        i = pl.program_id(0)

        def glu(a_ref, g_ref):
            return a_ref[...].astype(F32) * _sigmoid(g_ref[...].astype(F32))

        _fill_window(buf, jnp.where(i > 0, glu(ap, gp), 0.0), glu(ac, gc),
                     jnp.where(i < nblk - 1, glu(an, gn), 0.0), T)
        _depthwise(buf, w_ref, u1_ref, T, C, HALO - CONV_PAD, 1)
        u1 = u1_ref[...] + b_ref[...]
        u1_ref[...] = u1
        mu = jnp.mean(u1, axis=-1, keepdims=True)
        xc = u1 - mu
        rstd = lax.rsqrt(jnp.mean(xc * xc, axis=-1, keepdims=True) + EPS)
        u2 = xc * rstd * lg_ref[...] + lb_ref[...]
        cv_ref[...] = (u2 * _sigmoid(u2)).astype(cv_ref.dtype)

    ap, ac, an = _halo_specs(S, T, C, 3)
    gp, gc, gn = _halo_specs(S, T, C, 4)
    vec = pl.BlockSpec((1, C), lambda i: (0, 0))
    out = pl.BlockSpec((T, C), lambda i: (i, 0))
    return pl.pallas_call(
        body, name="conv_fwd", grid=(nblk,),
        in_specs=[ap, ac, an, gp, gc, gn, pl.BlockSpec((32, C), lambda i: (0, 0)), vec, vec, vec],
        out_specs=[out, out], out_shape=[_sds((S, C), BF16), _sds((S, C), F32)],
        scratch_shapes=[_window_scratch(T, C)],
        compiler_params=_params("parallel"))(y, y, y, y, y, y, conv_w32, conv_b, ln_g, ln_b)


def _conv_bwd(dac, u1, y, conv_w32, ln_g, ln_b, *, T=512):
    S = y.shape[0]
    T = min(T, S)
    nblk = S // T
    C = D_CONV

    def body(dp, dc, dn, up, uc, un, ap, ac, an, gp, gc, gn, w_ref, lg_ref, lb_ref,
             dag_ref, dw_ref, dsm_ref, bufd, bufu, du0_scr, dw_acc):
        i = pl.program_id(0)
        lg = lg_ref[...]

        def du1_of(dcv_ref, u1_ref):
            u1 = u1_ref[...]
            mu = jnp.mean(u1, axis=-1, keepdims=True)
            xc = u1 - mu
            rstd = lax.rsqrt(jnp.mean(xc * xc, axis=-1, keepdims=True) + EPS)
            xhat = xc * rstd
            u2 = xhat * lg + lb_ref[...]
            sg = _sigmoid(u2)
            du2 = dcv_ref[...].astype(F32) * (sg * (1.0 + u2 * (1.0 - sg)))
            dxh = du2 * lg
            du1 = rstd * (dxh - jnp.mean(dxh, axis=-1, keepdims=True)
                          - xhat * jnp.mean(dxh * xhat, axis=-1, keepdims=True))
            return du1, du2, xhat

        def glu(a_ref, g_ref):
            return a_ref[...].astype(F32) * _sigmoid(g_ref[...].astype(F32))

        @pl.when(i == 0)
        def _():
            dw_ref[...] = jnp.zeros_like(dw_ref)
            dsm_ref[...] = jnp.zeros_like(dsm_ref)

        du1_c, du2_c, xhat_c = du1_of(dc, uc)
        dsm_ref[0:1, :] += jnp.sum(du1_c, axis=0, keepdims=True)
        dsm_ref[1:2, :] += jnp.sum(du2_c * xhat_c, axis=0, keepdims=True)
        dsm_ref[2:3, :] += jnp.sum(du2_c, axis=0, keepdims=True)
        _fill_window(bufd, jnp.where(i > 0, du1_of(dp, up)[0], 0.0), du1_c,
                     jnp.where(i < nblk - 1, du1_of(dn, un)[0], 0.0), T)
        _fill_window(bufu, jnp.where(i > 0, glu(ap, gp), 0.0), glu(ac, gc),
                     jnp.where(i < nblk - 1, glu(an, gn), 0.0), T)

        _depthwise(bufd, w_ref, du0_scr, T, C, HALO + CONV_PAD, -1)
        dw_acc[...] = jnp.zeros_like(dw_acc)

        def dw_tile(t, carry):
            r0 = pl.multiple_of(t * CONV_ROWS, CONV_ROWS)
            for c0 in range(0, C, LANES):
                ls = slice(c0, c0 + LANES)
                d = bufd[0, pl.ds(HALO + r0, CONV_ROWS), ls]
                for k, rows in _tap_reads(bufu, HALO - CONV_PAD, 1, r0, ls):
                    prod = d * rows
                    part = prod[0:8]
                    for j in range(8, CONV_ROWS, 8):
                        part = part + prod[j:j + 8]
                    dw_acc[k, :, ls] += part
            return carry

        lax.fori_loop(0, T // CONV_ROWS, dw_tile, 0)
        for k in range(CONV_WIDTH):
            dw_ref[k:k + 1, :] += jnp.sum(dw_acc[k], axis=0, keepdims=True)
        du0 = du0_scr[...]
        a = ac[...].astype(F32)
        sg = _sigmoid(gc[...].astype(F32))
        dag_ref[:, 0:C] = (du0 * sg).astype(dag_ref.dtype)
        dag_ref[:, C:] = (du0 * a * sg * (1.0 - sg)).astype(dag_ref.dtype)

    dp, dc, dn = _halo_specs(S, T, C, 1)
    up, uc, un = _halo_specs(S, T, C, 0)
    ap, ac, an = _halo_specs(S, T, C, 3)
    gp, gc, gn = _halo_specs(S, T, C, 4)
    vec = pl.BlockSpec((1, C), lambda i: (0, 0))
    return pl.pallas_call(
        body, name="conv_bwd", grid=(nblk,),
        in_specs=[dp, dc, dn, up, uc, un, ap, ac, an, gp, gc, gn,
                  pl.BlockSpec((32, C), lambda i: (0, 0)), vec, vec],
        out_specs=[pl.BlockSpec((T, 2 * C), lambda i: (i, 0)), pl.BlockSpec((32, C), lambda i: (0, 0)),
                   pl.BlockSpec((8, C), lambda i: (0, 0))],
        out_shape=[_sds((S, 2 * C), BF16), _sds((32, C), F32), _sds((8, C), F32)],
        scratch_shapes=[_window_scratch(T, C), _window_scratch(T, C), pltpu.VMEM((T, C), F32),
                        pltpu.VMEM((CONV_WIDTH, 8, C), F32)],
        compiler_params=_params("arbitrary"))(dac, dac, dac, u1, u1, u1, y, y, y, y, y, y, conv_w32, ln_g, ln_b)


def _xatt_fwd(xq, xk, xv, *, tm=512):
    S = xq.shape[0]
    M = xk.shape[0]
    tm = min(tm, S)
    scale = XATT_HEAD_DIM ** -0.5

    def body(q_ref, k_ref, v_ref, o_ref):
        heads = [slice(h * XATT_HEAD_DIM, (h + 1) * XATT_HEAD_DIM) for h in range(XATT_HEADS)]
        s = [_nt(q_ref[:, sl], k_ref[:, sl]) * scale for sl in heads]
        e = [jnp.exp(t - jnp.max(t, axis=-1, keepdims=True)) for t in s]
        p = [t * (1.0 / jnp.sum(t, axis=-1, keepdims=True)) for t in e]
        for sl, t in zip(heads, p):
            o_ref[:, sl] = jnp.dot(t.astype(BF16), v_ref[:, sl], preferred_element_type=F32).astype(o_ref.dtype)

    row = pl.BlockSpec((tm, D_MODEL), lambda i: (i, 0))
    full = pl.BlockSpec((M, D_MODEL), lambda i: (0, 0))
    return pl.pallas_call(
        body, name="xatt_fwd", grid=(S // tm,), in_specs=[row, full, full], out_specs=row,
        out_shape=_sds((S, D_MODEL), BF16), compiler_params=_params("parallel"))(xq, xk, xv)


def _xatt_bwd(xq, xk, xv, dxo, *, tm=512):
    S = xq.shape[0]
    M = xk.shape[0]
    tm = min(tm, S)
    scale = XATT_HEAD_DIM ** -0.5

    def body(q_ref, k_ref, v_ref, do_ref, dq_ref, dk_ref, dv_ref):
        i = pl.program_id(0)

        @pl.when(i == 0)
        def _():
            dk_ref[...] = jnp.zeros_like(dk_ref)
            dv_ref[...] = jnp.zeros_like(dv_ref)

        heads = [slice(h * XATT_HEAD_DIM, (h + 1) * XATT_HEAD_DIM) for h in range(XATT_HEADS)]
        s = [_nt(q_ref[:, sl], k_ref[:, sl]) * scale for sl in heads]
        dp = [_nt(do_ref[:, sl], v_ref[:, sl]) for sl in heads]
        e = [jnp.exp(t - jnp.max(t, axis=-1, keepdims=True)) for t in s]
        p = [t * (1.0 / jnp.sum(t, axis=-1, keepdims=True)) for t in e]
        ds = [(pp * (t - jnp.sum(t * pp, axis=-1, keepdims=True))).astype(BF16) for pp, t in zip(p, dp)]
        for sl, pp, t in zip(heads, p, ds):
            dq_ref[:, sl] = (jnp.dot(t, k_ref[:, sl], preferred_element_type=F32) * scale).astype(dq_ref.dtype)
            dv_ref[:, sl] += _tn(pp.astype(BF16), do_ref[:, sl])
            dk_ref[:, sl] += _tn(t, q_ref[:, sl]) * scale

    row = pl.BlockSpec((tm, D_MODEL), lambda i: (i, 0))
    full = pl.BlockSpec((M, D_MODEL), lambda i: (0, 0))
    return pl.pallas_call(
        body, name="xatt_bwd", grid=(S // tm,), in_specs=[row, full, full, row], out_specs=[row, full, full],
        out_shape=[_sds((S, D_MODEL), BF16), _sds((M, D_MODEL), F32), _sds((M, D_MODEL), F32)],
        compiler_params=_params("arbitrary"))(xq, xk, xv, dxo)


def _row_tile(R):
    for t in (256, 128, 64, 32, 16, 8):
        if R % t == 0:
            return t
    return R


def _sum_partials(own, recv, me, *, name):
    _, R, C = own.shape
    t = _row_tile(R)

    def body(me_ref, own_ref, r_ref, o_ref):
        o_ref[...] = ((own_ref[...].astype(F32) + r_ref[0].astype(F32)) + r_ref[1].astype(F32)) + r_ref[2].astype(F32)

    return pl.pallas_call(
        body, name=name,
        grid_spec=pltpu.PrefetchScalarGridSpec(
            num_scalar_prefetch=1, grid=(R // t,),
            in_specs=[pl.BlockSpec((None, t, C), lambda i, me_ref: (me_ref[0], i, 0)),
                      pl.BlockSpec((3, t, C), lambda i, me_ref: (0, i, 0))],
            out_specs=pl.BlockSpec((t, C), lambda i, me_ref: (i, 0))),
        out_shape=_sds((R, C), F32), compiler_params=_params("parallel"))(me, own, recv)


def _adamw_math(w, g, m, v):
    m2 = ADAM_B1 * m + (1.0 - ADAM_B1) * g
    v2 = ADAM_B2 * v + (1.0 - ADAM_B2) * (g * g)
    m_hat = m2 / (1.0 - ADAM_B1 ** ADAM_STEP)
    v_hat = v2 / (1.0 - ADAM_B2 ** ADAM_STEP)
    delta = -ADAM_LR * (m_hat / (jnp.sqrt(v_hat) + ADAM_EPS) + ADAM_WD * w)
    return delta, m2, v2


def _adamw(parts, w, m, v, *, name):
    R, C = w.shape
    t = _row_tile(R)
    n = len(parts)

    def body(*refs):
        w_ref, m_ref, v_ref = refs[n:n + 3]
        g_ref, d_ref, m2_ref, v2_ref = refs[n + 3:]
        g = refs[0][...]
        for r in refs[1:n]:
            g = g + r[...]
        delta, m2, v2 = _adamw_math(w_ref[...], g, m_ref[...], v_ref[...])
        g_ref[...] = g
        d_ref[...] = delta
        m2_ref[...] = m2
        v2_ref[...] = v2

    blk = pl.BlockSpec((t, C), lambda i: (i, 0))
    return pl.pallas_call(
        body, name=name, grid=(R // t,), in_specs=[blk] * (n + 3), out_specs=[blk] * 4,
        out_shape=[_sds((R, C), F32)] * 4, compiler_params=_params("parallel"))(*parts, w, m, v)


def _adamw_small(gathered, chip, entries):
    _, R, C = gathered.shape
    n = len(entries)
    group = D_CONV // N_CHIPS

    def body(chip_ref, g_ref, *refs):
        ins, outs, tot_ref = refs[:3 * n], refs[3 * n:7 * n], refs[7 * n]
        tot = g_ref[0]
        for k in range(1, N_DEV):
            tot = tot + g_ref[k]
        tot_ref[...] = tot
        for e, ((kind, r), _, _, _) in enumerate(entries):
            if kind == "row":
                g = tot_ref[r:r + 1, :]
            elif kind == "gain":
                g = jnp.concatenate([tot_ref[r:r + 1, :], tot_ref[r + 1:r + 2, :]], axis=1)
            else:
                g = tot_ref[r:r + CONV_WIDTH, 0:group]
                for j in range(1, N_CHIPS):
                    g = jnp.where(chip_ref[0] == j, tot_ref[r:r + CONV_WIDTH, j * group:(j + 1) * group], g)
            delta, m2, v2 = _adamw_math(ins[3 * e][...], g, ins[3 * e + 1][...], ins[3 * e + 2][...])
            for o, val in zip(outs[4 * e:4 * e + 4], (g, delta, m2, v2)):
                o[...] = val

    whole = lambda a: pl.BlockSpec(a.shape, lambda i, c: (0,) * a.ndim)
    arrays = [a for _, w, m, v in entries for a in (w, m, v)]
    out_like = [w for _, w, _, _ in entries for _ in range(4)]
    tot_like = _sds((R, C), F32)
    out = pl.pallas_call(
        body, name="adamw_small",
        grid_spec=pltpu.PrefetchScalarGridSpec(
            num_scalar_prefetch=1, grid=(1,),
            in_specs=[whole(gathered)] + [whole(a) for a in arrays],
            out_specs=[whole(a) for a in out_like] + [whole(tot_like)]),
        out_shape=[_sds(a.shape, F32) for a in out_like] + [tot_like],
        compiler_params=_params("arbitrary"))(chip, gathered, *arrays)
    return out[-1], [tuple(out[4 * e:4 * e + 4]) for e in range(n)]


def _chip_peers():
    x, y = lax.axis_index("x"), lax.axis_index("y")
    return [(1 - x, y), (x, 1 - y), (1 - x, 1 - y)]


HBM_SPEC = pl.BlockSpec(memory_space=pltpu.HBM)
SEM_SPEC = pl.BlockSpec(memory_space=pltpu.SEMAPHORE)


def _exchange_peers(mode):
    x, y, c = lax.axis_index("x"), lax.axis_index("y"), lax.axis_index("c")
    if mode == "swap":
        return [(x, y, 1 - c)]
    if mode == "all":
        flips = [(fx, fy, fc) for fx in (0, 1) for fy in (0, 1) for fc in (0, 1)][1:]
        return [(1 - x if fx else x, 1 - y if fy else y, 1 - c if fc else c) for fx, fy, fc in flips]
    return [(px, py, c) for px, py in _chip_peers()]


def _exchange_start(mode, srcs, zones, *, name):
    n = len(srcs)

    def body(*refs):
        ins, lands = refs[:n], refs[n:2 * n]
        send_sems, recv_sems = refs[2 * n:3 * n], refs[3 * n:4 * n]
        token = refs[-1]
        x, y, c = lax.axis_index("x"), lax.axis_index("y"), lax.axis_index("c")
        mine = 2 * x + y if mode == "gather" else 4 * x + 2 * y + c
        for t in range(n):
            for k, (px, py, pc) in enumerate(_exchange_peers(mode)):
                if mode in ("gather", "all"):
                    s, d = ins[t], lands[t].at[mine]
                elif mode == "scatter":
                    s, d = ins[t].at[2 * px + py], lands[t].at[k]
                else:
                    s, d = ins[t], lands[t]
                pltpu.make_async_remote_copy(src_ref=s, dst_ref=d, send_sem=send_sems[t], recv_sem=recv_sems[t],
                                             device_id=(px, py, pc), device_id_type=MESH).start()
            if mode in ("gather", "all"):
                pltpu.make_async_copy(ins[t], lands[t].at[mine], send_sems[t]).start()
        token[...] = jnp.zeros_like(token)

    hbm = lambda a: pltpu.with_memory_space_constraint(a, pltpu.HBM)
    out = pl.pallas_call(
        body, name=name,
        in_specs=[HBM_SPEC] * (2 * n),
        out_specs=[SEM_SPEC] * (2 * n) + [HBM_SPEC] * (2 * n) + [pl.BlockSpec(memory_space=pltpu.VMEM)],
        out_shape=[pltpu.SemaphoreType.DMA(())] * (2 * n)
        + [pltpu.HBM(a.shape, a.dtype) for a in list(srcs) + list(zones)] + [_sds((8, LANES), F32)],
        input_output_aliases={i: 2 * n + i for i in range(2 * n)},
        compiler_params=pltpu.CompilerParams(has_side_effects=pltpu.SideEffectType.DATAFLOW_SIDE_EFFECTING),
    )(*[hbm(a) for a in list(srcs) + list(zones)])
    return out[:n], out[n:2 * n], out[2 * n:3 * n], out[3 * n:4 * n], out[-1]


def _exchange_wait(mode, started, after, *, name):
    send_sems, recv_sems, srcs, zones, _ = started
    n = len(srcs)
    afters = tuple(after) if isinstance(after, (tuple, list)) else (after,)

    def body(*refs):
        lands = refs[n:2 * n]
        send_refs, recv_refs = refs[2 * n:3 * n], refs[3 * n:4 * n]
        me = (lax.axis_index("x"), lax.axis_index("y"), lax.axis_index("c"))
        n_remote = {"gather": N_CHIPS - 1, "scatter": N_CHIPS - 1, "all": N_DEV - 1, "swap": 1}[mode]
        for t in range(n):
            got = lands[t] if mode == "swap" else lands[t].at[pl.ds(0, n_remote)]
            sent = lands[t] if mode in ("gather", "all") else got
            pltpu.make_async_remote_copy(src_ref=sent, dst_ref=sent, send_sem=send_refs[t], recv_sem=recv_refs[t],
                                         device_id=me, device_id_type=MESH).wait_send()
            pltpu.make_async_remote_copy(src_ref=got, dst_ref=got, send_sem=send_refs[t], recv_sem=recv_refs[t],
                                         device_id=me, device_id_type=MESH).wait_recv()

    out = pl.pallas_call(
        body, name=name,
        in_specs=[HBM_SPEC] * (2 * n) + [SEM_SPEC] * (2 * n) + [pl.BlockSpec(memory_space=pl.ANY)] * len(afters),
        out_specs=[HBM_SPEC] * (2 * n),
        out_shape=[pltpu.HBM(a.shape, a.dtype) for a in list(srcs) + list(zones)],
        input_output_aliases={i: i for i in range(2 * n)},
        compiler_params=pltpu.CompilerParams(has_side_effects=pltpu.SideEffectType.DATAFLOW_SIDE_EFFECTING),
    )(*srcs, *zones, *send_sems, *recv_sems, *afters)
    return out[:n], out[n:]


def _swap_with_sibling(parts):
    n = len(parts)

    def body(*refs):
        ins, outs = refs[:n], refs[n:2 * n]
        send_sems, recv_sems = refs[2 * n:]
        sib = (lax.axis_index("x"), lax.axis_index("y"), 1 - lax.axis_index("c"))
        cps = []
        for t in range(n):
            cp = pltpu.make_async_remote_copy(
                src_ref=ins[t], dst_ref=outs[t], send_sem=send_sems.at[t], recv_sem=recv_sems.at[t],
                device_id=sib, device_id_type=MESH)
            cp.start()
            cps.append(cp)
        for cp in cps:
            cp.wait()

    any_spec = pl.BlockSpec(memory_space=pl.ANY)
    return pl.pallas_call(
        body, name="swap_with_sibling", in_specs=[any_spec] * n, out_specs=[any_spec] * n,
        out_shape=[_sds(p.shape, p.dtype) for p in parts],
        scratch_shapes=[pltpu.SemaphoreType.DMA((n,)), pltpu.SemaphoreType.DMA((n,))])(*parts)


BIG = ("w_in", "w_out", "w_xq", "w_xk", "w_xv", "w_xo", "w_up", "w_down")
COL_SHARDED = ("w_in", "w_up")


def _as_matrix(name, w4):
    if name in COL_SHARDED:
        return w4
    return w4.reshape(1, w4.shape[0] * w4.shape[1], w4.shape[2])


def _transposed(w3):
    nsh, K, n = w3.shape
    return jnp.swapaxes(w3, 1, 2).reshape(1, nsh * n, K)


def _shard_layout(name, g):
    if name in COL_SHARDED:
        return g
    return g.reshape(N_CHIPS, g.shape[0] * g.shape[1] // N_CHIPS, g.shape[2])


def _local_step(x, mem, target, vecs, comm):
    S = x.shape[0]
    tables = _rope_tables(S)

    xn = _rms_fwd(x, vecs["norm_mix_g"], name="rms_mix")
    w_in, conv_w32 = comm["first"]((xn,) + tuple(tables))
    y = _mm_nn(xn, w_in, name="mm_in", tm=2048, tn=640)
    qk, v_perm = _rope_fwd(y, tables)
    v_src = [(y, 2)] + [(v, 0) for v in v_perm[1:]]
    outs, lses = zip(*[_att_fwd(qk[p], v_src[p], d, name=f"att_fwd_d{d}") for p, d in enumerate(DILATIONS)])
    att, lg = _att_combine(outs, lses)
    cv, u1 = _conv_fwd(y, conv_w32, vecs["conv_b"], vecs["conv_ln_g"], vecs["conv_ln_b"])
    Wm = {k: _as_matrix(k, v) for k, v in comm["rest"]((att, cv)).items()}
    Wm["w_in"] = w_in
    h1, hn = _mm_rows((att, cv), Wm["w_out"], _residual_norm_tail, name="mm_out_rms", rows_in=(x,),
                      vecs_in=(vecs["norm_x_g"],), rows_out=(F32, BF16))
    xq = _mm_nn(hn, Wm["w_xq"], name="mm_xq")
    mn = _rms_fwd(mem, vecs["norm_mem_g"], name="rms_mem")
    xk = _mm_nn(mn, Wm["w_xk"], name="mm_xk")
    xv = _mm_nn(mn, Wm["w_xv"], name="mm_xv")
    xo = _xatt_fwd(xq, xk, xv)
    h2, hm = _mm_rows(xo, Wm["w_xo"], _residual_norm_tail, name="mm_xo_rms", rows_in=(h1,),
                      vecs_in=(vecs["norm_mlp_g"],), rows_out=(F32, BF16))
    relu_up = _mm_nn(hm, Wm["w_up"], name="mm_up", relu=True, tm=2048)
    sums = ((8, D_MODEL),)
    dh3, dh3b, dg_final, loss = _mm_rows(
        relu_up, Wm["w_down"], _loss_tail, name="mm_down_loss", rows_in=(h2, target), vecs_in=(vecs["norm_final_g"],),
        rows_out=(F32, BF16), sums_out=sums + ((8, LANES),), a_squared=True, tm=256)
    g = {}
    g["w_down"] = _mm_tn(relu_up, dh3b, 1, name="dw_down", a_squared=True)
    dup = _mm_nt(dh3b, Wm["w_down"], name="d_act", out_dtype=BF16, mul=relu_up, tm=2048)
    g["w_up"] = _mm_tn(hm, dup, N_CHIPS, name="dw_up")
    sent = comm["send_mlp"]({k: _shard_layout(k, g[k]) for k in ("w_down", "w_up")})
    dh2, dh2b, dg_mlp = _mm_rows(
        dup, _transposed(Wm["w_up"]), _rms_bwd_tail(True), name="d_hm_rms", rows_in=(h2, dh3),
        vecs_in=(vecs["norm_mlp_g"] + sent[0:1, 0:1],), rows_out=(F32, BF16), sums_out=sums, tm=256)
    g["w_xo"] = _mm_tn(xo, dh2b, 1, name="dw_xo")
    dxo = _mm_nt(dh2b, Wm["w_xo"], name="d_xo", out_dtype=BF16)
    dxq, dxk, dxv = _xatt_bwd(xq, xk, xv, dxo)
    g["w_xq"] = _mm_tn(hn, dxq, 1, name="dw_xq")
    dh1, dh1b, dg_x = _mm_rows(
        dxq, _transposed(Wm["w_xq"]), _rms_bwd_tail(True), name="d_hn_rms", rows_in=(h1, dh2),
        vecs_in=(vecs["norm_x_g"],), rows_out=(F32, BF16), sums_out=sums)
    dxkb, dxvb = dxk.astype(BF16), dxv.astype(BF16)
    g["w_xk"] = _mm_tn(mn, dxkb, 1, name="dw_xk")
    g["w_xv"] = _mm_tn(mn, dxvb, 1, name="dw_xv")
    dmn = _mm_nt(jnp.concatenate([dxkb, dxvb], axis=1),
                 jnp.concatenate([Wm["w_xk"], Wm["w_xv"]], axis=2), name="d_mn", out_dtype=BF16)
    _, _, dg_mem = _rms_bwd(dmn, mem, vecs["norm_mem_g"], None, name="rms_bwd_mem", bf16_copy=False)
    g["w_out"] = jnp.concatenate([_mm_tn(att, dh1b, 1, name="dw_out_att"), _mm_tn(cv, dh1b, 1, name="dw_out_conv")],
                                 axis=1)
    sent = comm["send_att"]({k: _shard_layout(k, g[k]) for k in ("w_out", "w_xq", "w_xk", "w_xv", "w_xo")})
    dac = _mm_nt(dh1b, Wm["w_out"], name="d_mix", out_dtype=BF16)
    dag, dconv_w, dconv_small = _conv_bwd(dac, u1, y, conv_w32, vecs["conv_ln_g"] + sent[0:1, 0:1],
                                          vecs["conv_ln_b"])
    delta, do_perm = _att_delta(dac, att)
    do_src = [(dac, 0)] + [(t, 0) for t in do_perm[1:]]
    dq, dk, dv = zip(*[_att_bwd(qk[p], v_src[p], do_src[p], lg[p], delta[p], d, name=f"att_bwd_d{d}")
                       for p, d in enumerate(DILATIONS)])
    dy = _assemble_dy(dq, dk, dv, dag, tables)
    sent = comm["send_in"]({"w_in": _mm_tn(xn, dy, N_CHIPS, name="dw_in", tn=640)})
    grad_x, dg_mix = _mm_rows(
        dy, _transposed(Wm["w_in"]), _rms_bwd_tail(False), name="d_xn_rms", rows_in=(x, dh1),
        vecs_in=(vecs["norm_mix_g"] + sent[0:1, 0:1],), rows_out=(F32,), sums_out=sums)

    small = dict(conv_w=dconv_w, conv_small=dconv_small, norm_mix_g=dg_mix, norm_x_g=dg_x, norm_mem_g=dg_mem,
                 norm_mlp_g=dg_mlp, norm_final_g=dg_final, loss=loss)
    return grad_x, small


SMALL_ORDER = ("conv_w", "conv_small", "norm_mix_g", "norm_x_g", "norm_mem_g", "norm_mlp_g", "norm_final_g", "loss")


def _pack_small(small):
    rows, offs, pos = [], {}, 0
    for k in SMALL_ORDER:
        a = small[k]
        a = a.reshape(a.shape[0] * a.shape[1] // SMALL_W, SMALL_W)
        pad = (-a.shape[0]) % 8
        if pad:
            a = jnp.pad(a, ((0, pad), (0, 0)))
        rows.append(a)
        offs[k] = pos
        pos += a.shape[0]
    return jnp.concatenate(rows, axis=0), offs


def kernel(x, mem, norm_mix_g, w_in, conv_w, conv_b, conv_ln_g, conv_ln_b, w_out, norm_x_g, norm_mem_g, w_xq, w_xk, w_xv, w_xo, norm_mlp_g, w_up, w_down, norm_final_g, loss_target, m_norm_mix_g, m_w_in, m_conv_w, m_conv_b, m_conv_ln_g, m_conv_ln_b, m_w_out, m_norm_x_g, m_norm_mem_g, m_w_xq, m_w_xk, m_w_xv, m_w_xo, m_norm_mlp_g, m_w_up, m_w_down, m_norm_final_g, v_norm_mix_g, v_w_in, v_conv_w, v_conv_b, v_conv_ln_g, v_conv_ln_b, v_w_out, v_norm_x_g, v_norm_mem_g, v_w_xq, v_w_xk, v_w_xv, v_w_xo, v_norm_mlp_g, v_w_up, v_w_down, v_norm_final_g):
    names = ("norm_mix_g", "w_in", "conv_w", "conv_b", "conv_ln_g", "conv_ln_b", "w_out", "norm_x_g", "norm_mem_g",
             "w_xq", "w_xk", "w_xv", "w_xo", "norm_mlp_g", "w_up", "w_down", "norm_final_g")
    wts = dict(zip(names, (norm_mix_g, w_in, conv_w, conv_b, conv_ln_g, conv_ln_b, w_out, norm_x_g, norm_mem_g,
                           w_xq, w_xk, w_xv, w_xo, norm_mlp_g, w_up, w_down, norm_final_g)))
    mom = dict(zip(names, (m_norm_mix_g, m_w_in, m_conv_w, m_conv_b, m_conv_ln_g, m_conv_ln_b, m_w_out, m_norm_x_g,
                           m_norm_mem_g, m_w_xq, m_w_xk, m_w_xv, m_w_xo, m_norm_mlp_g, m_w_up, m_w_down, m_norm_final_g)))
    var = dict(zip(names, (v_norm_mix_g, v_w_in, v_conv_w, v_conv_b, v_conv_ln_g, v_conv_ln_b, v_w_out, v_norm_x_g,
                           v_norm_mem_g, v_w_xq, v_w_xk, v_w_xv, v_w_xo, v_norm_mlp_g, v_w_up, v_w_down, v_norm_final_g)))
    chip = 2 * lax.axis_index("x") + lax.axis_index("y")

    def zone(shard):
        return lax.empty((N_CHIPS,) + shard.shape, shard.dtype)

    conv_w_pad = jnp.pad(wts["conv_w"][0], ((0, 1), (0, 0)))
    first_shards = [wts["w_in"][0].astype(BF16), conv_w_pad]
    gathering_first = _exchange_start("gather", first_shards, [zone(s) for s in first_shards],
                                      name="gather_first_start")
    rest = tuple(k for k in BIG if k != "w_in")
    behind_first = gathering_first[4][0, 0]
    rest_shards = [(wts[k][0] + behind_first).astype(BF16) for k in rest]
    gathering = _exchange_start("gather", rest_shards, [zone(s) for s in rest_shards], name="gather_rest_start")
    sending = {}

    def wait_first(after):
        _, (w_in_all, conv_w_all) = _exchange_wait("gather", gathering_first, after, name="gather_first_wait")
        return w_in_all, jnp.transpose(conv_w_all, (1, 0, 2)).reshape(32, D_CONV)

    def wait_rest(after):
        _, zones = _exchange_wait("gather", gathering, after, name="gather_rest_wait")
        return dict(zip(rest, zones))

    def send(group, grads):
        keys = tuple(grads)
        zones = [lax.empty((N_CHIPS - 1,) + grads[k].shape[1:], grads[k].dtype) for k in keys]
        sending[group] = (keys, _exchange_start("scatter", [grads[k] for k in keys], zones,
                                                name=f"scatter_{group}_start"))
        return sending[group][1][4]

    comm = dict(first=wait_first, rest=wait_rest, send_mlp=lambda grads: send("mlp", grads),
                send_att=lambda grads: send("att", grads), send_in=lambda grads: send("in", grads))
    vecs = {k: wts[k] for k in ("conv_b", "conv_ln_g", "conv_ln_b", "norm_x_g", "norm_mem_g", "norm_mlp_g")}
    vecs["norm_mix_g"] = wts["norm_mix_g"] + gathering[4][0:1, 0:1]
    vecs["norm_final_g"] = wts["norm_final_g"].reshape(1, D_MODEL)
    grad_x, small = _local_step(x[0], mem[0], loss_target[0], vecs, comm)

    packed, offs = _pack_small(small)
    me_arr = jnp.reshape(chip, (1,)).astype(jnp.int32)
    gathering_small = _exchange_start("all", [packed], [lax.empty((N_DEV,) + packed.shape, packed.dtype)],
                                      name="allgather_small_start")
    sums = {}

    def settle(group, after):
        keys, started = sending[group]
        srcs, zones = _exchange_wait("scatter", started, after, name=f"scatter_{group}_wait")
        for k, own, got in zip(keys, srcs, zones):
            sums[k] = _sum_partials(own, got, me_arr, name=f"sum_{k}")

    settle("mlp", gathering_small[4])
    settle("att", gathering_small[4])
    early = tuple(sums)
    swapping = _exchange_start("swap", [sums[k] for k in early], [lax.empty(sums[k].shape, F32) for k in early],
                               name="swap_early_start")
    settle("in", swapping[4])
    _, (gath,) = _exchange_wait("all", gathering_small, sums["w_in"], name="allgather_small_wait")

    where = {"conv_w": ("conv_w", offs["conv_w"]), "conv_b": ("row", offs["conv_small"]),
             "conv_ln_g": ("row", offs["conv_small"] + 1), "conv_ln_b": ("row", offs["conv_small"] + 2)}
    where.update({k: ("gain", offs[k]) for k in ("norm_mix_g", "norm_x_g", "norm_mem_g", "norm_mlp_g", "norm_final_g")})
    as_2d = lambda a: a.reshape(a.shape[-2] if a.ndim > 1 else 1, a.shape[-1])
    tot_small, updates = _adamw_small(gath, me_arr, [(where[k], as_2d(wts[k]), as_2d(mom[k]), as_2d(var[k]))
                                                     for k in where])
    res = dict(zip(where, updates))
    loss = tot_small[offs["loss"], 0]

    sib = {"w_in": _swap_with_sibling([sums["w_in"]])[0]}
    sib.update(zip(early, _exchange_wait("swap", swapping, sib["w_in"], name="swap_early_wait")[1]))
    for k in BIG:
        res[k] = _adamw([sums[k], sib[k]], wts[k][0], mom[k][0], var[k][0], name=f"adamw_{k}")

    outs = [loss, grad_x[None]]
    for j in range(4):
        outs += [res[k][j].reshape(wts[k].shape) for k in names]
    return tuple(outs)
```

```python
import jax
import jax.numpy as jnp
from jax import lax
from jax.experimental import pallas as pl
from jax.experimental.pallas import tpu as pltpu

F32 = jnp.float32
BF16 = jnp.bfloat16
MESH = pl.DeviceIdType.MESH

D_MODEL = 1024
ATT_HEADS = 8
HEAD_DIM = 64
D_ATT = ATT_HEADS * HEAD_DIM
D_CONV = D_MODEL - D_ATT
DILATIONS = (1, 4, 16)
HALF = 64
ROPE_THETA = 500000.0
ROT_DIM = HEAD_DIM // 4
CONV_WIDTH = 31
CONV_PAD = (CONV_WIDTH - 1) // 2
XATT_HEADS = 4
XATT_HEAD_DIM = D_MODEL // XATT_HEADS
D_FF = 4 * D_MODEL
D_IN = 3 * D_ATT + 2 * D_CONV
EPS = 1e-6
NEG_INF = -1e30
N_CHIPS = 4
N_DEV = 8

ADAM_LR = 0.001
ADAM_B1 = 0.9
ADAM_B2 = 0.999
ADAM_EPS = 1e-08
ADAM_WD = 0.01
ADAM_STEP = 10

VMEM_LIMIT_V7X = 56 * 1024 * 1024
LANES = 128
HALO = 16
CONV_ROWS = 64
ATT_BLOCK = 128
SMALL_W = 512


def _params(*sem):
    return pltpu.CompilerParams(dimension_semantics=sem, vmem_limit_bytes=VMEM_LIMIT_V7X)


def _sds(shape, dtype):
    return jax.ShapeDtypeStruct(shape, dtype)


def _squared(a):
    af = a.astype(F32)
    return (af * af).astype(BF16)


def _mm_nn(a, w3, *, name, out_dtype=BF16, res=None, relu=False, a_squared=False, tm=1024, tn=None, tk=1024):
    M, K = a.shape
    nsh, _, n = w3.shape
    tm, tk = min(tm, M), min(tk, K)
    tn = tn or min(n, 1024)
    npt, nk = n // tn, K // tk
    nj, N = nsh * npt, nsh * n
    n_out = 1

    def body(*refs):
        a_ref, w_ref = refs[0], refs[1]
        pos = 2
        res_ref = None
        if res is not None:
            res_ref = refs[pos]
            pos += 1
        outs = refs[pos:pos + n_out]
        acc_ref = refs[pos + n_out] if nk > 1 else None

        def finish(acc):
            if res_ref is not None:
                acc = acc + res_ref[...]
            if relu:
                acc = jnp.maximum(acc, 0.0)
            outs[0][...] = acc.astype(outs[0].dtype)

        a_val = _squared(a_ref[...]) if a_squared else a_ref[...]
        part = jnp.dot(a_val, w_ref[...], preferred_element_type=F32)
        if nk == 1:
            finish(part)
        else:
            k = pl.program_id(2)

            @pl.when(k == 0)
            def _():
                acc_ref[...] = part

            @pl.when(k > 0)
            def _():
                acc_ref[...] += part

            @pl.when(k == nk - 1)
            def _():
                finish(acc_ref[...])

    in_specs = [pl.BlockSpec((tm, tk), lambda i, j, k: (i, k)),
                pl.BlockSpec((None, tk, tn), lambda i, j, k: (j // npt, k, j % npt))]
    args = [a, w3]
    if res is not None:
        in_specs.append(pl.BlockSpec((tm, tn), lambda i, j, k: (i, j)))
        args.append(res)
    out_spec = pl.BlockSpec((tm, tn), lambda i, j, k: (i, j))
    out = pl.pallas_call(
        body, name=name, grid=(M // tm, nj, nk), in_specs=in_specs,
        out_specs=[out_spec] * n_out, out_shape=[_sds((M, N), out_dtype)] * n_out,
        scratch_shapes=[pltpu.VMEM((tm, tn), F32)] if nk > 1 else [],
        compiler_params=_params("parallel", "parallel", "arbitrary"))(*args)
    return out[0]


def _mm_nt(dy, w3, *, name, out_dtype=F32, mul=None, tm=1024, tn=None, tko=1024):
    M, N = dy.shape
    nsh, K, n = w3.shape
    tm, tko = min(tm, M), min(tko, K)
    tn = tn or min(n, 1024)
    npt = n // tn
    nj = nsh * npt

    def body(*refs):
        dy_ref, w_ref = refs[0], refs[1]
        pos = 2
        mul_ref = None
        if mul is not None:
            mul_ref = refs[pos]
            pos += 1
        out_ref = refs[pos]
        acc_ref = refs[pos + 1] if nj > 1 else None

        def finish(acc):
            if mul_ref is not None:
                acc = acc * (2.0 * mul_ref[...].astype(F32))
            out_ref[...] = acc.astype(out_ref.dtype)

        part = lax.dot_general(dy_ref[...], w_ref[...], (((1,), (1,)), ((), ())), preferred_element_type=F32)
        if nj == 1:
            finish(part)
        else:
            j = pl.program_id(2)

            @pl.when(j == 0)
            def _():
                acc_ref[...] = part

            @pl.when(j > 0)
            def _():
                acc_ref[...] += part

            @pl.when(j == nj - 1)
            def _():
                finish(acc_ref[...])

    in_specs = [pl.BlockSpec((tm, tn), lambda i, ko, j: (i, j)),
                pl.BlockSpec((None, tko, tn), lambda i, ko, j: (j // npt, ko, j % npt))]
    args = [dy, w3]
    if mul is not None:
        in_specs.append(pl.BlockSpec((tm, tko), lambda i, ko, j: (i, ko)))
        args.append(mul)
    return pl.pallas_call(
        body, name=name, grid=(M // tm, K // tko, nj), in_specs=in_specs,
        out_specs=pl.BlockSpec((tm, tko), lambda i, ko, j: (i, ko)), out_shape=_sds((M, K), out_dtype),
        scratch_shapes=[pltpu.VMEM((tm, tko), F32)] if nj > 1 else [],
        compiler_params=_params("parallel", "parallel", "arbitrary"))(*args)


def _mm_tn(a, dy, nsh, *, name, out_dtype=BF16, a_squared=False, tm=2048, tk=1024, tn=None):
    M, K = a.shape
    N = dy.shape[1]
    n = N // nsh
    tm, tk = min(tm, M), min(tk, K)
    tn = tn or min(n, 1024)
    npt = n // tn
    nj, nm = nsh * npt, M // tm

    def body(a_ref, dy_ref, out_ref, acc_ref):
        m = pl.program_id(2)
        a_val = _squared(a_ref[...]) if a_squared else a_ref[...]
        part = lax.dot_general(a_val, dy_ref[...], (((0,), (0,)), ((), ())), preferred_element_type=F32)

        @pl.when(m == 0)
        def _():
            acc_ref[...] = part

        @pl.when(m > 0)
        def _():
            acc_ref[...] += part

        @pl.when(m == nm - 1)
        def _():
            out_ref[...] = acc_ref[...].astype(out_ref.dtype)

    return pl.pallas_call(
        body, name=name, grid=(K // tk, nj, nm),
        in_specs=[pl.BlockSpec((tm, tk), lambda kk, j, m: (m, kk)),
                  pl.BlockSpec((tm, tn), lambda kk, j, m: (m, j))],
        out_specs=pl.BlockSpec((None, tk, tn), lambda kk, j, m: (j // npt, kk, j % npt)),
        out_shape=_sds((nsh, K, n), out_dtype),
        scratch_shapes=[pltpu.VMEM((tk, tn), F32)],
        compiler_params=_params("parallel", "parallel", "arbitrary"))(a, dy)


def _rms_fwd(x, g, *, name, tm=512):
    M, Dm = x.shape
    tm = min(tm, M)

    def body(x_ref, g_ref, o_ref):
        xf = x_ref[...]
        r = lax.rsqrt(jnp.mean(xf * xf, axis=-1, keepdims=True) + EPS)
        o_ref[...] = (xf * r * g_ref[...]).astype(o_ref.dtype)

    return pl.pallas_call(
        body, name=name, grid=(M // tm,),
        in_specs=[pl.BlockSpec((tm, Dm), lambda i: (i, 0)), pl.BlockSpec((1, Dm), lambda i: (0, 0))],
        out_specs=pl.BlockSpec((tm, Dm), lambda i: (i, 0)), out_shape=_sds((M, Dm), BF16),
        compiler_params=_params("parallel"))(x, g)


def _rms_bwd(dxn, x, g, dres, *, name, bf16_copy=True, tm=512):
    M, Dm = x.shape
    tm = min(tm, M)
    has_res = dres is not None

    def body(*refs):
        dxn_ref, x_ref, g_ref = refs[:3]
        dres_ref = refs[3] if has_res else None
        dx_ref, dg_ref = refs[-1 - 1 - bf16_copy], refs[-1]
        dxb_ref = refs[-2] if bf16_copy else None
        i = pl.program_id(0)
        xf = x_ref[...]
        r = lax.rsqrt(jnp.mean(xf * xf, axis=-1, keepdims=True) + EPS)
        nrm = xf * r
        dxn_f = dxn_ref[...].astype(F32)
        dn = dxn_f * g_ref[...]
        dx = r * (dn - nrm * jnp.mean(dn * nrm, axis=-1, keepdims=True))
        if has_res:
            dx = dx + dres_ref[...]
        dx_ref[...] = dx
        if bf16_copy:
            dxb_ref[...] = dx.astype(dxb_ref.dtype)

        @pl.when(i == 0)
        def _():
            dg_ref[...] = jnp.zeros_like(dg_ref)

        dg_ref[0:1, :] += jnp.sum(dxn_f * nrm, axis=0, keepdims=True)

    row = pl.BlockSpec((tm, Dm), lambda i: (i, 0))
    in_specs = [row, row, pl.BlockSpec((1, Dm), lambda i: (0, 0))] + ([row] if has_res else [])
    args = [dxn, x, g] + ([dres] if has_res else [])
    out = pl.pallas_call(
        body, name=name, grid=(M // tm,), in_specs=in_specs,
        out_specs=[row] * (1 + bf16_copy) + [pl.BlockSpec((8, Dm), lambda i: (0, 0))],
        out_shape=[_sds((M, Dm), F32)] + [_sds((M, Dm), BF16)] * bf16_copy + [_sds((8, Dm), F32)],
        compiler_params=_params("arbitrary"))(*args)
    return out[0], (out[1] if bf16_copy else None), out[-1]


def _mm_rows(a, w3, tail, *, name, rows_in=(), vecs_in=(), rows_out=(), sums_out=(), a_squared=False, tm=512):
    parts = a if isinstance(a, (tuple, list)) else (a,)
    M = parts[0].shape[0]
    K, N = w3.shape[1], w3.shape[2]
    tm = min(tm, M)
    n_a, n_ri, n_vi, n_ro = len(parts), len(rows_in), len(vecs_in), len(rows_out)

    def body(*refs):
        a_refs, w_ref, refs = refs[:n_a], refs[n_a], refs[n_a + 1:]
        rin, vin = refs[:n_ri], refs[n_ri:n_ri + n_vi]
        rout, sout = refs[n_ri + n_vi:n_ri + n_vi + n_ro], refs[n_ri + n_vi + n_ro:]

        @pl.when(pl.program_id(0) == 0)
        def _():
            for s in sout:
                s[...] = jnp.zeros_like(s)

        a_val = a_refs[0][...] if n_a == 1 else jnp.concatenate([r[...] for r in a_refs], axis=1)
        if a_squared:
            a_val = _squared(a_val)
        tail(jnp.dot(a_val, w_ref[0], preferred_element_type=F32), rin, vin, rout, sout)

    row = pl.BlockSpec((tm, N), lambda i: (i, 0))
    once = lambda shape: pl.BlockSpec(shape, lambda i: (0,) * len(shape))
    return pl.pallas_call(
        body, name=name, grid=(M // tm,),
        in_specs=[pl.BlockSpec((tm, p.shape[1]), lambda i: (i, 0)) for p in parts] + [once((1, K, N))]
        + [row] * n_ri + [once((1, N))] * n_vi,
        out_specs=[row] * n_ro + [once(s) for s in sums_out],
        out_shape=[_sds((M, N), dt) for dt in rows_out] + [_sds(s, F32) for s in sums_out],
        compiler_params=_params("arbitrary"))(*parts, w3, *rows_in, *vecs_in)


def _residual_norm_tail(prod, rows_in, vecs_in, rows_out, sums_out):
    hf = prod + rows_in[0][...]
    rows_out[0][...] = hf
    r = lax.rsqrt(jnp.mean(hf * hf, axis=-1, keepdims=True) + EPS)
    rows_out[1][...] = (hf * r * vecs_in[0][...]).astype(BF16)


def _rms_bwd_tail(bf16_copy):
    def tail(dxn, rows_in, vecs_in, rows_out, sums_out):
        xf = rows_in[0][...]
        r = lax.rsqrt(jnp.mean(xf * xf, axis=-1, keepdims=True) + EPS)
        nrm = xf * r
        dn = dxn * vecs_in[0][...]
        dx = r * (dn - nrm * jnp.mean(dn * nrm, axis=-1, keepdims=True)) + rows_in[1][...]
        rows_out[0][...] = dx
        if bf16_copy:
            rows_out[1][...] = dx.astype(BF16)
        sums_out[0][0:1, :] += jnp.sum(dxn * nrm, axis=0, keepdims=True)

    return tail


def _loss_tail(prod, rows_in, vecs_in, rows_out, sums_out):
    hf = prod + rows_in[0][...]
    r = lax.rsqrt(jnp.mean(hf * hf, axis=-1, keepdims=True) + EPS)
    nrm = hf * r
    gv = vecs_in[0][...]
    err = nrm * gv - rows_in[1][...]
    dy = err * (1.0 / hf.shape[-1])
    dn = dy * gv
    dh = r * (dn - nrm * jnp.mean(dn * nrm, axis=-1, keepdims=True))
    rows_out[0][...] = dh
    rows_out[1][...] = dh.astype(BF16)
    sums_out[0][0:1, :] += jnp.sum(dy * nrm, axis=0, keepdims=True)
    part = 0.5 * jnp.sum(jnp.mean(err * err, axis=-1, keepdims=True), axis=0, keepdims=True)
    sel = (lax.broadcasted_iota(jnp.int32, (8, 128), 0) == 0) & (lax.broadcasted_iota(jnp.int32, (8, 128), 1) == 0)
    sums_out[1][...] += jnp.where(sel, part, 0.0)


def _class_spec(tm, d, width):
    return pl.BlockSpec((d, tm // d, width), lambda i: (0, i, 0))


def _row_scratch(tm, width):
    return pltpu.VMEM((width // LANES, tm, LANES), F32)


def _fill(scr, val):
    for c in range(scr.shape[0]):
        scr[c] = val[:, c * LANES:(c + 1) * LANES]


def _to_classes(scr, out_ref, d):
    n = scr.shape[1] // d
    for r in range(d):
        for c in range(scr.shape[0]):
            out_ref[r, :, c * LANES:(c + 1) * LANES] = scr[c, pl.ds(r, n, stride=d), :].astype(out_ref.dtype)


def _from_classes(in_ref, scr, d):
    n = scr.shape[1] // d
    for r in range(d):
        blk = in_ref[r].astype(F32)
        for c in range(scr.shape[0]):
            scr[c, pl.ds(r, n, stride=d), :] = blk[:, c * LANES:(c + 1) * LANES]
    return jnp.concatenate([scr[c] for c in range(scr.shape[0])], axis=1)


def _rope_tables(S):
    half = ROT_DIM // 2
    freqs = ROPE_THETA ** (-jnp.arange(0, ROT_DIM, 2, dtype=F32) / ROT_DIM)
    ang = jnp.arange(S, dtype=F32)[:, None] * freqs[None, :]
    cos, sin = jnp.cos(ang), jnp.sin(ang)
    ones = jnp.ones((S, HEAD_DIM - ROT_DIM), F32)
    zeros = jnp.zeros((S, HEAD_DIM - ROT_DIM), F32)
    zh = jnp.zeros((S, half), F32)
    c = jnp.concatenate([cos, cos, ones], axis=1)
    sa = jnp.concatenate([-sin, zh, zeros], axis=1)
    sb = jnp.concatenate([zh, sin, zeros], axis=1)
    return tuple(jnp.tile(t, (1, LANES // HEAD_DIM)) for t in (c, sa, sb))


def _rope_fwd(y, tables, *, tm=512):
    S = y.shape[0]
    W = 2 * D_ATT
    tm = min(tm, S)
    half = ROT_DIM // 2
    dils = [d for d in DILATIONS if d > 1]

    def body(y_ref, c_ref, sa_ref, sb_ref, qk_ref, *rest):
        qk_outs, v_outs = rest[:len(dils)], rest[len(dils):2 * len(dils)]
        scr_qk, scr_v = rest[2 * len(dils):]
        t = y_ref[:, 0:W].astype(F32)
        rep = W // LANES
        c, sa, sb = (jnp.tile(r[...], (1, rep)) for r in (c_ref, sa_ref, sb_ref))
        rot = t * c + pltpu.roll(t, W - half, axis=1) * sa + pltpu.roll(t, half, axis=1) * sb
        qk_ref[...] = rot.astype(qk_ref.dtype)
        _fill(scr_qk, rot)
        _fill(scr_v, y_ref[:, W:W + D_ATT].astype(F32))
        for d, qo, vo in zip(dils, qk_outs, v_outs):
            _to_classes(scr_qk, qo, d)
            _to_classes(scr_v, vo, d)

    tab = pl.BlockSpec((tm, LANES), lambda i: (i, 0))
    out = pl.pallas_call(
        body, name="rope_fwd", grid=(S // tm,),
        in_specs=[pl.BlockSpec((tm, 3 * D_ATT), lambda i: (i, 0)), tab, tab, tab],
        out_specs=[pl.BlockSpec((tm, W), lambda i: (i, 0))] + [_class_spec(tm, d, W) for d in dils]
        + [_class_spec(tm, d, D_ATT) for d in dils],
        out_shape=[_sds((S, W), BF16)] + [_sds((d, S // d, W), BF16) for d in dils]
        + [_sds((d, S // d, D_ATT), BF16) for d in dils],
        scratch_shapes=[_row_scratch(tm, W), _row_scratch(tm, D_ATT)],
        compiler_params=_params("parallel"))(y, *tables)
    qk = [out[0]] + [o.reshape(S, W) for o in out[1:1 + len(dils)]]
    v = [None] + [o.reshape(S, D_ATT) for o in out[1 + len(dils):]]
    return qk, v


def _assemble_dy(dq, dk, dv, dag, tables, *, tm=512):
    S = dag.shape[0]
    tm = min(tm, S)
    half = ROT_DIM // 2
    W = D_ATT
    n_pat = len(DILATIONS)

    def body(*refs):
        groups = [refs[g * n_pat:(g + 1) * n_pat] for g in range(3)]
        dag_ref, c_ref, sa_ref, sb_ref, o_ref, scr = refs[3 * n_pat:]
        rep = W // LANES
        c, sa, sb = (jnp.tile(r[...], (1, rep)) for r in (c_ref, sa_ref, sb_ref))

        def total(rs):
            acc = rs[0][...].astype(F32)
            for d, r in zip(DILATIONS[1:], rs[1:]):
                acc = acc + _from_classes(r, scr, d)
            return acc

        def unrope(dr):
            return dr * c + pltpu.roll(dr * sa, half, axis=1) + pltpu.roll(dr * sb, W - half, axis=1)

        o_ref[:, 0:W] = unrope(total(groups[0])).astype(o_ref.dtype)
        o_ref[:, W:2 * W] = unrope(total(groups[1])).astype(o_ref.dtype)
        o_ref[:, 2 * W:3 * W] = total(groups[2]).astype(o_ref.dtype)
        o_ref[:, 3 * W:] = dag_ref[...]

    specs = [pl.BlockSpec((tm, W), lambda i: (i, 0))] + [_class_spec(tm, d, W) for d in DILATIONS[1:]]
    tab = pl.BlockSpec((tm, LANES), lambda i: (i, 0))
    args = [a if d == 1 else a.reshape(d, S // d, W) for grp in (dq, dk, dv) for d, a in zip(DILATIONS, grp)]
    return pl.pallas_call(
        body, name="assemble_dy", grid=(S // tm,),
        in_specs=specs * 3 + [pl.BlockSpec((tm, 2 * D_CONV), lambda i: (i, 0)), tab, tab, tab],
        out_specs=pl.BlockSpec((tm, D_IN), lambda i: (i, 0)), out_shape=_sds((S, D_IN), BF16),
        scratch_shapes=[_row_scratch(tm, W)],
        compiler_params=_params("parallel"))(*args, dag, *tables)


def _seq_specs(L, tb, col):
    nb, per, nh = L // tb, tb // HALF, L // HALF
    centre = pl.BlockSpec((tb, D_ATT), lambda r, i: (r * nb + i, col))
    prev = pl.BlockSpec((HALF, D_ATT), lambda r, i: (r * nh + jnp.maximum(i * per - 1, 0), col))
    nxt = pl.BlockSpec((HALF, D_ATT), lambda r, i: (r * nh + jnp.minimum((i + 1) * per, nh - 1), col))
    return prev, centre, nxt


def _band_mask(i, tq, L):
    shape = (tq, tq + 2 * HALF)
    c_idx = lax.broadcasted_iota(jnp.int32, shape, 0)
    w_idx = lax.broadcasted_iota(jnp.int32, shape, 1)
    diff = w_idx - c_idx
    wpos = i * tq - HALF + w_idx
    return (diff >= 0) & (diff <= 2 * HALF) & (wpos >= 0) & (wpos < L)


def _lane_groups():
    for c0 in range(0, D_ATT, LANES):
        yield slice(c0, c0 + LANES)


def _first_head(rows):
    return lax.broadcasted_iota(jnp.int32, (rows, LANES), 1) < HEAD_DIM


def _split_pair(x, first):
    zero = jnp.zeros_like(x)
    return jnp.where(first, x, zero), jnp.where(first, zero, x)


def _nt(a, b):
    return lax.dot_general(a, b, (((1,), (1,)), ((), ())), preferred_element_type=F32)


def _tn(a, b):
    return lax.dot_general(a, b, (((0,), (0,)), ((), ())), preferred_element_type=F32)


ATT_SCALE = HEAD_DIM ** -0.5


def _att_fwd(qk, v_src, d, *, name):
    S = qk.shape[0]
    L = S // d
    tq = min(ATT_BLOCK, L)
    v_arr, v_col = v_src

    def body(q_ref, kp_ref, kc_ref, kn_ref, vp_ref, vc_ref, vn_ref, o_ref, lse_ref):
        i = pl.program_id(1)
        valid = _band_mask(i, tq, L)
        q = q_ref[...] * ATT_SCALE
        kwin = jnp.concatenate([kp_ref[...], kc_ref[...], kn_ref[...]], axis=0)
        vwin = jnp.concatenate([vp_ref[...], vc_ref[...], vn_ref[...]], axis=0)
        first = _first_head(tq)
        groups = list(_lane_groups())
        heads = [(ls, t) for ls in groups for t in _split_pair(q[:, ls], first)]
        s = [jnp.where(valid, _nt(t, kwin[:, ls]), NEG_INF) for ls, t in heads]
        m = [jnp.max(t, axis=-1, keepdims=True) for t in s]
        p = [jnp.exp(t - mm) for t, mm in zip(s, m)]
        den = [jnp.sum(t, axis=-1, keepdims=True) for t in p]
        o = [jnp.dot(t.astype(BF16), vwin[:, ls], preferred_element_type=F32) * (1.0 / dd)
             for t, dd, (ls, _) in zip(p, den, heads)]
        lse = [mm + jnp.log(dd) for mm, dd in zip(m, den)]
        for g, ls in enumerate(groups):
            o_ref[:, ls] = jnp.where(first, o[2 * g], o[2 * g + 1]).astype(o_ref.dtype)
            lse_ref[:, ls] = jnp.where(first, lse[2 * g], lse[2 * g + 1])

    _, qc, _ = _seq_specs(L, tq, 0)
    kp, kc, kn = _seq_specs(L, tq, 1)
    vp, vc, vn = _seq_specs(L, tq, v_col)
    out = pl.BlockSpec((tq, D_ATT), lambda r, i: (r * (L // tq) + i, 0))
    return pl.pallas_call(
        body, name=name, grid=(d, L // tq),
        in_specs=[qc, kp, kc, kn, vp, vc, vn], out_specs=[out, out],
        out_shape=[_sds((S, D_ATT), BF16), _sds((S, D_ATT), F32)],
        compiler_params=_params("parallel", "parallel"))(qk, qk, qk, qk, v_arr, v_arr, v_arr)


def _att_combine(outs, lses, *, tm=512):
    S = outs[0].shape[0]
    tm = min(tm, S)
    dils = DILATIONS[1:]
    n_d = len(dils)

    def body(*refs):
        o_refs, l_refs = refs[0:1 + n_d], refs[1 + n_d:2 + 2 * n_d]
        att_ref, lg_ref = refs[2 + 2 * n_d:4 + 2 * n_d]
        lg_outs = refs[4 + 2 * n_d:4 + 3 * n_d]
        scr = refs[4 + 3 * n_d:]
        scr_o, scr_l, scr_lg = scr[:n_d], scr[n_d:2 * n_d], scr[2 * n_d]
        ls = [l_refs[0][...]] + [_from_classes(r, s, d) for r, s, d in zip(l_refs[1:], scr_l, dils)]
        os_ = [o_refs[0][...].astype(F32)] + [_from_classes(r, s, d) for r, s, d in zip(o_refs[1:], scr_o, dils)]
        mx = ls[0]
        for l in ls[1:]:
            mx = jnp.maximum(mx, l)
        es = [jnp.exp(l - mx) for l in ls]
        tot = es[0]
        num = es[0] * os_[0]
        for e, o in zip(es[1:], os_[1:]):
            tot = tot + e
            num = num + e * o
        att_ref[...] = (num / tot).astype(att_ref.dtype)
        lg = mx + jnp.log(tot)
        lg_ref[...] = lg
        _fill(scr_lg, lg)
        for d, out in zip(dils, lg_outs):
            _to_classes(scr_lg, out, d)

    nat = pl.BlockSpec((tm, D_ATT), lambda i: (i, 0))
    specs = [nat] + [_class_spec(tm, d, D_ATT) for d in dils]
    view = lambda arrs: [arrs[0]] + [a.reshape(d, S // d, D_ATT) for a, d in zip(arrs[1:], dils)]
    out = pl.pallas_call(
        body, name="att_combine", grid=(S // tm,), in_specs=specs * 2,
        out_specs=[nat, nat] + specs[1:],
        out_shape=[_sds((S, D_ATT), BF16), _sds((S, D_ATT), F32)] + [_sds((d, S // d, D_ATT), F32) for d in dils],
        scratch_shapes=[_row_scratch(tm, D_ATT)] * (2 * n_d + 1),
        compiler_params=_params("parallel"))(*view(list(outs)), *view(list(lses)))
    return out[0], [out[1]] + [o.reshape(S, D_ATT) for o in out[2:]]


def _att_delta(dac, att, *, tm=512):
    S = att.shape[0]
    tm = min(tm, S)
    dils = DILATIONS[1:]
    n_d = len(dils)

    def body(do_ref, o_ref, dl_ref, *rest):
        dl_outs, do_outs = rest[:n_d], rest[n_d:2 * n_d]
        scr_dl, scr_do = rest[2 * n_d:]
        do = do_ref[...].astype(F32)
        prod = do * o_ref[...].astype(F32)
        per_head = [jnp.broadcast_to(jnp.sum(prod[:, h * HEAD_DIM:(h + 1) * HEAD_DIM], axis=-1, keepdims=True),
                                     (tm, HEAD_DIM)) for h in range(ATT_HEADS)]
        dl = jnp.concatenate(per_head, axis=1)
        dl_ref[...] = dl
        _fill(scr_dl, dl)
        _fill(scr_do, do)
        for d, dlo, doo in zip(dils, dl_outs, do_outs):
            _to_classes(scr_dl, dlo, d)
            _to_classes(scr_do, doo, d)

    blk = pl.BlockSpec((tm, D_ATT), lambda i: (i, 0))
    out = pl.pallas_call(
        body, name="att_delta", grid=(S // tm,), in_specs=[blk, blk],
        out_specs=[blk] + [_class_spec(tm, d, D_ATT) for d in dils] * 2,
        out_shape=[_sds((S, D_ATT), F32)] + [_sds((d, S // d, D_ATT), F32) for d in dils]
        + [_sds((d, S // d, D_ATT), BF16) for d in dils],
        scratch_shapes=[_row_scratch(tm, D_ATT), _row_scratch(tm, D_ATT)],
        compiler_params=_params("parallel"))(dac, att)
    delta = [out[0]] + [o.reshape(S, D_ATT) for o in out[1:1 + n_d]]
    do = [None] + [o.reshape(S, D_ATT) for o in out[1 + n_d:]]
    return delta, do


def _att_bwd(qk, v_src, do_src, lg, delta, d, *, name):
    S = qk.shape[0]
    L = S // d
    tq = min(ATT_BLOCK, L)
    nb, per, nh = L // tq, tq // HALF, L // HALF
    n_blocks = d * nb
    win = tq + 2 * HALF
    lead = tq - HALF
    acc_rows = lead + win
    (v_arr, v_col), (do_arr, do_col) = v_src, do_src

    def body(q_ref, kp_ref, kc_ref, kn_ref, vp_ref, vc_ref, vn_ref, do_ref, lg_ref, dl_ref,
             dq_ref, dk_ref, dv_ref, acc_k, acc_v):
        b = pl.program_id(0)
        i = lax.rem(jnp.minimum(b, n_blocks - 1), nb)

        @pl.when(b == 0)
        def _():
            acc_k[...] = jnp.zeros_like(acc_k)
            acc_v[...] = jnp.zeros_like(acc_v)

        @pl.when(b < n_blocks)
        def _():
            valid = _band_mask(i, tq, L)
            q, do = q_ref[...] * ATT_SCALE, do_ref[...]
            kwin = jnp.concatenate([kp_ref[...], kc_ref[...], kn_ref[...]], axis=0)
            vwin = jnp.concatenate([vp_ref[...], vc_ref[...], vn_ref[...]], axis=0)
            first, first_w = _first_head(tq), _first_head(win)
            groups = list(_lane_groups())
            cols = [c for ls in groups for c in (ls.start, ls.start + HEAD_DIM)]
            lanes = [ls for ls in groups for _ in range(2)]
            qh = [t for ls in groups for t in _split_pair(q[:, ls], first)]
            doh = [t for ls in groups for t in _split_pair(do[:, ls], first)]
            s = [jnp.where(valid, _nt(t, kwin[:, ls]), NEG_INF) for t, ls in zip(qh, lanes)]
            dp = [_nt(t, vwin[:, ls]) for t, ls in zip(doh, lanes)]
            p = [jnp.exp(t - lg_ref[:, c:c + 1]) for t, c in zip(s, cols)]
            ds = [(pp * (t - dl_ref[:, c:c + 1])).astype(BF16) for pp, t, c in zip(p, dp, cols)]
            dq = [jnp.dot(t, kwin[:, ls], preferred_element_type=F32) for t, ls in zip(ds, lanes)]
            dk = [_tn(t, q[:, ls]) for t, ls in zip(ds, lanes)]
            dv = [_tn(pp.astype(BF16), do[:, ls]) for pp, ls in zip(p, lanes)]
            for g, ls in enumerate(groups):
                dq_ref[:, ls] = (jnp.where(first, dq[2 * g], dq[2 * g + 1]) * ATT_SCALE).astype(dq_ref.dtype)
                acc_k[lead:, ls] += jnp.where(first_w, dk[2 * g], dk[2 * g + 1])
                acc_v[lead:, ls] += jnp.where(first_w, dv[2 * g], dv[2 * g + 1])

        for acc, out in ((acc_k, dk_ref), (acc_v, dv_ref)):
            out[...] = acc[0:tq, :].astype(out.dtype)
            kept = acc[tq:, :]
            acc[0:acc_rows - tq, :] = kept
            acc[acc_rows - tq:, :] = jnp.zeros((tq, D_ATT), F32)

    def seq(col):
        blk = lambda b: jnp.minimum(b, n_blocks - 1)
        cls = lambda b: (blk(b) // nb) * nh
        centre = pl.BlockSpec((tq, D_ATT), lambda b: (blk(b), col))
        prev = pl.BlockSpec((HALF, D_ATT), lambda b: (cls(b) + jnp.maximum((blk(b) % nb) * per - 1, 0), col))
        nxt = pl.BlockSpec((HALF, D_ATT), lambda b: (cls(b) + jnp.minimum((blk(b) % nb + 1) * per, nh - 1), col))
        return prev, centre, nxt

    _, qc, _ = seq(0)
    kp, kc, kn = seq(1)
    vp, vc, vn = seq(v_col)
    _, doc, _ = seq(do_col)
    late = pl.BlockSpec((tq, D_ATT), lambda b: (jnp.maximum(b - 1, 0), 0))
    return pl.pallas_call(
        body, name=name, grid=(n_blocks + 1,),
        in_specs=[qc, kp, kc, kn, vp, vc, vn, doc, qc, qc], out_specs=[qc, late, late],
        out_shape=[_sds((S, D_ATT), BF16)] * 3,
        scratch_shapes=[pltpu.VMEM((acc_rows, D_ATT), F32), pltpu.VMEM((acc_rows, D_ATT), F32)],
        compiler_params=_params("arbitrary"))(qk, qk, qk, qk, v_arr, v_arr, v_arr, do_arr, lg, delta)


def _sigmoid(x):
    return 1.0 / (1.0 + jnp.exp(-x))


def _halo_specs(S, T, width, col):
    last = S // HALO - 1
    per = T // HALO
    centre = pl.BlockSpec((T, width), lambda i: (i, col))
    prev = pl.BlockSpec((HALO, width), lambda i: (jnp.maximum(i * per - 1, 0), col))
    nxt = pl.BlockSpec((HALO, width), lambda i: (jnp.minimum((i + 1) * per, last), col))
    return prev, centre, nxt


def _window_scratch(T, C):
    return pltpu.VMEM((8, T + 2 * HALO, C), F32)


def _fill_window(buf, prev, centre, nxt, T):
    buf[0, 0:HALO, :] = prev
    buf[0, HALO:HALO + T, :] = centre
    buf[0, HALO + T:, :] = nxt
    rows = T + 2 * HALO - 8
    for s in range(1, 8):
        buf[s, 0:rows, :] = buf[0, s:s + rows, :]


def _tap_reads(buf, first_off, step, r0, ls):
    by_slab = {}
    for k in range(CONV_WIDTH):
        off = first_off + step * k
        by_slab.setdefault(off % 8, []).append((k, off - off % 8))
    for s, taps in by_slab.items():
        lo = min(a for _, a in taps)
        hi = max(a for _, a in taps)
        rows = buf[s, pl.ds(lo + r0, CONV_ROWS + hi - lo), ls]
        for k, a in taps:
            yield k, rows[a - lo:a - lo + CONV_ROWS]


def _depthwise(buf, w_ref, out_ref, T, C, first_off, step):
    def row_tile(t, carry):
        r0 = pl.multiple_of(t * CONV_ROWS, CONV_ROWS)
        for c0 in range(0, C, LANES):
            ls = slice(c0, c0 + LANES)
            acc = jnp.zeros((CONV_ROWS, LANES), F32)
            for k, rows in _tap_reads(buf, first_off, step, r0, ls):
                acc = acc + rows * w_ref[k:k + 1, ls]
            out_ref[pl.ds(r0, CONV_ROWS), ls] = acc
        return carry

    lax.fori_loop(0, T // CONV_ROWS, row_tile, 0)


def _conv_fwd(y, conv_w32, conv_b, ln_g, ln_b, *, T=512):
    S = y.shape[0]
    T = min(T, S)
    nblk = S // T
    C = D_CONV

    def body(ap, ac, an, gp, gc, gn, w_ref, b_ref, lg_ref, lb_ref, cv_ref, u1_ref, buf):
        i = pl.program_id(0)

        def glu(a_ref, g_ref):
            return a_ref[...].astype(F32) * _sigmoid(g_ref[...].astype(F32))

        _fill_window(buf, jnp.where(i > 0, glu(ap, gp), 0.0), glu(ac, gc),
                     jnp.where(i < nblk - 1, glu(an, gn), 0.0), T)
        _depthwise(buf, w_ref, u1_ref, T, C, HALO - CONV_PAD, 1)
        u1 = u1_ref[...] + b_ref[...]
        u1_ref[...] = u1
        mu = jnp.mean(u1, axis=-1, keepdims=True)
        xc = u1 - mu
        rstd = lax.rsqrt(jnp.mean(xc * xc, axis=-1, keepdims=True) + EPS)
        u2 = xc * rstd * lg_ref[...] + lb_ref[...]
        cv_ref[...] = (u2 * _sigmoid(u2)).astype(cv_ref.dtype)

    ap, ac, an = _halo_specs(S, T, C, 3)
    gp, gc, gn = _halo_specs(S, T, C, 4)
    vec = pl.BlockSpec((1, C), lambda i: (0, 0))
    out = pl.BlockSpec((T, C), lambda i: (i, 0))
    return pl.pallas_call(
        body, name="conv_fwd", grid=(nblk,),
        in_specs=[ap, ac, an, gp, gc, gn, pl.BlockSpec((32, C), lambda i: (0, 0)), vec, vec, vec],
        out_specs=[out, out], out_shape=[_sds((S, C), BF16), _sds((S, C), F32)],
        scratch_shapes=[_window_scratch(T, C)],
        compiler_params=_params("parallel"))(y, y, y, y, y, y, conv_w32, conv_b, ln_g, ln_b)


def _conv_bwd(dac, u1, y, conv_w32, ln_g, ln_b, *, T=512):
    S = y.shape[0]
    T = min(T, S)
    nblk = S // T
    C = D_CONV

    def body(dp, dc, dn, up, uc, un, ap, ac, an, gp, gc, gn, w_ref, lg_ref, lb_ref,
             dag_ref, dw_ref, dsm_ref, bufd, bufu, du0_scr, dw_acc):
        i = pl.program_id(0)
        lg = lg_ref[...]

        def du1_of(dcv_ref, u1_ref):
            u1 = u1_ref[...]
            mu = jnp.mean(u1, axis=-1, keepdims=True)
            xc = u1 - mu
            rstd = lax.rsqrt(jnp.mean(xc * xc, axis=-1, keepdims=True) + EPS)
            xhat = xc * rstd
            u2 = xhat * lg + lb_ref[...]
            sg = _sigmoid(u2)
            du2 = dcv_ref[...].astype(F32) * (sg * (1.0 + u2 * (1.0 - sg)))
            dxh = du2 * lg
            du1 = rstd * (dxh - jnp.mean(dxh, axis=-1, keepdims=True)
                          - xhat * jnp.mean(dxh * xhat, axis=-1, keepdims=True))
            return du1, du2, xhat

        def glu(a_ref, g_ref):
            return a_ref[...].astype(F32) * _sigmoid(g_ref[...].astype(F32))

        @pl.when(i == 0)
        def _():
            dw_ref[...] = jnp.zeros_like(dw_ref)
            dsm_ref[...] = jnp.zeros_like(dsm_ref)

        du1_c, du2_c, xhat_c = du1_of(dc, uc)
        dsm_ref[0:1, :] += jnp.sum(du1_c, axis=0, keepdims=True)
        dsm_ref[1:2, :] += jnp.sum(du2_c * xhat_c, axis=0, keepdims=True)
        dsm_ref[2:3, :] += jnp.sum(du2_c, axis=0, keepdims=True)
        _fill_window(bufd, jnp.where(i > 0, du1_of(dp, up)[0], 0.0), du1_c,
                     jnp.where(i < nblk - 1, du1_of(dn, un)[0], 0.0), T)
        _fill_window(bufu, jnp.where(i > 0, glu(ap, gp), 0.0), glu(ac, gc),
                     jnp.where(i < nblk - 1, glu(an, gn), 0.0), T)

        _depthwise(bufd, w_ref, du0_scr, T, C, HALO + CONV_PAD, -1)
        dw_acc[...] = jnp.zeros_like(dw_acc)

        def dw_tile(t, carry):
            r0 = pl.multiple_of(t * CONV_ROWS, CONV_ROWS)
            for c0 in range(0, C, LANES):
                ls = slice(c0, c0 + LANES)
                d = bufd[0, pl.ds(HALO + r0, CONV_ROWS), ls]
                for k, rows in _tap_reads(bufu, HALO - CONV_PAD, 1, r0, ls):
                    prod = d * rows
                    part = prod[0:8]
                    for j in range(8, CONV_ROWS, 8):
                        part = part + prod[j:j + 8]
                    dw_acc[k, :, ls] += part
            return carry

        lax.fori_loop(0, T // CONV_ROWS, dw_tile, 0)
        for k in range(CONV_WIDTH):
            dw_ref[k:k + 1, :] += jnp.sum(dw_acc[k], axis=0, keepdims=True)
        du0 = du0_scr[...]
        a = ac[...].astype(F32)
        sg = _sigmoid(gc[...].astype(F32))
        dag_ref[:, 0:C] = (du0 * sg).astype(dag_ref.dtype)
        dag_ref[:, C:] = (du0 * a * sg * (1.0 - sg)).astype(dag_ref.dtype)

    dp, dc, dn = _halo_specs(S, T, C, 1)
    up, uc, un = _halo_specs(S, T, C, 0)
    ap, ac, an = _halo_specs(S, T, C, 3)
    gp, gc, gn = _halo_specs(S, T, C, 4)
    vec = pl.BlockSpec((1, C), lambda i: (0, 0))
    return pl.pallas_call(
        body, name="conv_bwd", grid=(nblk,),
        in_specs=[dp, dc, dn, up, uc, un, ap, ac, an, gp, gc, gn,
                  pl.BlockSpec((32, C), lambda i: (0, 0)), vec, vec],
        out_specs=[pl.BlockSpec((T, 2 * C), lambda i: (i, 0)), pl.BlockSpec((32, C), lambda i: (0, 0)),
                   pl.BlockSpec((8, C), lambda i: (0, 0))],
        out_shape=[_sds((S, 2 * C), BF16), _sds((32, C), F32), _sds((8, C), F32)],
        scratch_shapes=[_window_scratch(T, C), _window_scratch(T, C), pltpu.VMEM((T, C), F32),
                        pltpu.VMEM((CONV_WIDTH, 8, C), F32)],
        compiler_params=_params("arbitrary"))(dac, dac, dac, u1, u1, u1, y, y, y, y, y, y, conv_w32, ln_g, ln_b)


def _xatt_fwd(xq, xk, xv, *, tm=512):
    S = xq.shape[0]
    M = xk.shape[0]
    tm = min(tm, S)
    scale = XATT_HEAD_DIM ** -0.5

    def body(q_ref, k_ref, v_ref, o_ref):
        heads = [slice(h * XATT_HEAD_DIM, (h + 1) * XATT_HEAD_DIM) for h in range(XATT_HEADS)]
        s = [_nt(q_ref[:, sl], k_ref[:, sl]) * scale for sl in heads]
        e = [jnp.exp(t - jnp.max(t, axis=-1, keepdims=True)) for t in s]
        p = [t * (1.0 / jnp.sum(t, axis=-1, keepdims=True)) for t in e]
        for sl, t in zip(heads, p):
            o_ref[:, sl] = jnp.dot(t.astype(BF16), v_ref[:, sl], preferred_element_type=F32).astype(o_ref.dtype)

    row = pl.BlockSpec((tm, D_MODEL), lambda i: (i, 0))
    full = pl.BlockSpec((M, D_MODEL), lambda i: (0, 0))
    return pl.pallas_call(
        body, name="xatt_fwd", grid=(S // tm,), in_specs=[row, full, full], out_specs=row,
        out_shape=_sds((S, D_MODEL), BF16), compiler_params=_params("parallel"))(xq, xk, xv)


def _xatt_bwd(xq, xk, xv, dxo, *, tm=512):
    S = xq.shape[0]
    M = xk.shape[0]
    tm = min(tm, S)
    scale = XATT_HEAD_DIM ** -0.5

    def body(q_ref, k_ref, v_ref, do_ref, dq_ref, dk_ref, dv_ref):
        i = pl.program_id(0)

        @pl.when(i == 0)
        def _():
            dk_ref[...] = jnp.zeros_like(dk_ref)
            dv_ref[...] = jnp.zeros_like(dv_ref)

        heads = [slice(h * XATT_HEAD_DIM, (h + 1) * XATT_HEAD_DIM) for h in range(XATT_HEADS)]
        s = [_nt(q_ref[:, sl], k_ref[:, sl]) * scale for sl in heads]
        dp = [_nt(do_ref[:, sl], v_ref[:, sl]) for sl in heads]
        e = [jnp.exp(t - jnp.max(t, axis=-1, keepdims=True)) for t in s]
        p = [t * (1.0 / jnp.sum(t, axis=-1, keepdims=True)) for t in e]
        ds = [(pp * (t - jnp.sum(t * pp, axis=-1, keepdims=True))).astype(BF16) for pp, t in zip(p, dp)]
        for sl, pp, t in zip(heads, p, ds):
            dq_ref[:, sl] = (jnp.dot(t, k_ref[:, sl], preferred_element_type=F32) * scale).astype(dq_ref.dtype)
            dv_ref[:, sl] += _tn(pp.astype(BF16), do_ref[:, sl])
            dk_ref[:, sl] += _tn(t, q_ref[:, sl]) * scale

    row = pl.BlockSpec((tm, D_MODEL), lambda i: (i, 0))
    full = pl.BlockSpec((M, D_MODEL), lambda i: (0, 0))
    return pl.pallas_call(
        body, name="xatt_bwd", grid=(S // tm,), in_specs=[row, full, full, row], out_specs=[row, full, full],
        out_shape=[_sds((S, D_MODEL), BF16), _sds((M, D_MODEL), F32), _sds((M, D_MODEL), F32)],
        compiler_params=_params("arbitrary"))(xq, xk, xv, dxo)


def _row_tile(R):
    for t in (256, 128, 64, 32, 16, 8):
        if R % t == 0:
            return t
    return R


def _sum_partials(own, recv, me, *, name):
    _, R, C = own.shape
    t = _row_tile(R)

    def body(me_ref, own_ref, r_ref, o_ref):
        o_ref[...] = ((own_ref[...].astype(F32) + r_ref[0].astype(F32)) + r_ref[1].astype(F32)) + r_ref[2].astype(F32)

    return pl.pallas_call(
        body, name=name,
        grid_spec=pltpu.PrefetchScalarGridSpec(
            num_scalar_prefetch=1, grid=(R // t,),
            in_specs=[pl.BlockSpec((None, t, C), lambda i, me_ref: (me_ref[0], i, 0)),
                      pl.BlockSpec((3, t, C), lambda i, me_ref: (0, i, 0))],
            out_specs=pl.BlockSpec((t, C), lambda i, me_ref: (i, 0))),
        out_shape=_sds((R, C), F32), compiler_params=_params("parallel"))(me, own, recv)


def _adamw_math(w, g, m, v):
    m2 = ADAM_B1 * m + (1.0 - ADAM_B1) * g
    v2 = ADAM_B2 * v + (1.0 - ADAM_B2) * (g * g)
    m_hat = m2 / (1.0 - ADAM_B1 ** ADAM_STEP)
    v_hat = v2 / (1.0 - ADAM_B2 ** ADAM_STEP)
    delta = -ADAM_LR * (m_hat / (jnp.sqrt(v_hat) + ADAM_EPS) + ADAM_WD * w)
    return delta, m2, v2


def _adamw(parts, w, m, v, *, name):
    R, C = w.shape
    t = _row_tile(R)
    n = len(parts)

    def body(*refs):
        w_ref, m_ref, v_ref = refs[n:n + 3]
        g_ref, d_ref, m2_ref, v2_ref = refs[n + 3:]
        g = refs[0][...]
        for r in refs[1:n]:
            g = g + r[...]
        delta, m2, v2 = _adamw_math(w_ref[...], g, m_ref[...], v_ref[...])
        g_ref[...] = g
        d_ref[...] = delta
        m2_ref[...] = m2
        v2_ref[...] = v2

    blk = pl.BlockSpec((t, C), lambda i: (i, 0))
    return pl.pallas_call(
        body, name=name, grid=(R // t,), in_specs=[blk] * (n + 3), out_specs=[blk] * 4,
        out_shape=[_sds((R, C), F32)] * 4, compiler_params=_params("parallel"))(*parts, w, m, v)


def _adamw_small(gathered, chip, entries):
    _, R, C = gathered.shape
    n = len(entries)
    group = D_CONV // N_CHIPS

    def body(chip_ref, g_ref, *refs):
        ins, outs, tot_ref = refs[:3 * n], refs[3 * n:7 * n], refs[7 * n]
        tot = g_ref[0]
        for k in range(1, N_DEV):
            tot = tot + g_ref[k]
        tot_ref[...] = tot
        for e, ((kind, r), _, _, _) in enumerate(entries):
            if kind == "row":
                g = tot_ref[r:r + 1, :]
            elif kind == "gain":
                g = jnp.concatenate([tot_ref[r:r + 1, :], tot_ref[r + 1:r + 2, :]], axis=1)
            else:
                g = tot_ref[r:r + CONV_WIDTH, 0:group]
                for j in range(1, N_CHIPS):
                    g = jnp.where(chip_ref[0] == j, tot_ref[r:r + CONV_WIDTH, j * group:(j + 1) * group], g)
            delta, m2, v2 = _adamw_math(ins[3 * e][...], g, ins[3 * e + 1][...], ins[3 * e + 2][...])
            for o, val in zip(outs[4 * e:4 * e + 4], (g, delta, m2, v2)):
                o[...] = val

    whole = lambda a: pl.BlockSpec(a.shape, lambda i, c: (0,) * a.ndim)
    arrays = [a for _, w, m, v in entries for a in (w, m, v)]
    out_like = [w for _, w, _, _ in entries for _ in range(4)]
    tot_like = _sds((R, C), F32)
    out = pl.pallas_call(
        body, name="adamw_small",
        grid_spec=pltpu.PrefetchScalarGridSpec(
            num_scalar_prefetch=1, grid=(1,),
            in_specs=[whole(gathered)] + [whole(a) for a in arrays],
            out_specs=[whole(a) for a in out_like] + [whole(tot_like)]),
        out_shape=[_sds(a.shape, F32) for a in out_like] + [tot_like],
        compiler_params=_params("arbitrary"))(chip, gathered, *arrays)
    return out[-1], [tuple(out[4 * e:4 * e + 4]) for e in range(n)]


def _chip_peers():
    x, y = lax.axis_index("x"), lax.axis_index("y")
    return [(1 - x, y), (x, 1 - y), (1 - x, 1 - y)]


HBM_SPEC = pl.BlockSpec(memory_space=pltpu.HBM)
SEM_SPEC = pl.BlockSpec(memory_space=pltpu.SEMAPHORE)


def _exchange_peers(mode):
    x, y, c = lax.axis_index("x"), lax.axis_index("y"), lax.axis_index("c")
    if mode == "swap":
        return [(x, y, 1 - c)]
    if mode == "all":
        flips = [(fx, fy, fc) for fx in (0, 1) for fy in (0, 1) for fc in (0, 1)][1:]
        return [(1 - x if fx else x, 1 - y if fy else y, 1 - c if fc else c) for fx, fy, fc in flips]
    return [(px, py, c) for px, py in _chip_peers()]


def _exchange_start(mode, srcs, zones, *, name):
    n = len(srcs)

    def body(*refs):
        ins, lands = refs[:n], refs[n:2 * n]
        send_sems, recv_sems = refs[2 * n:3 * n], refs[3 * n:4 * n]
        token = refs[-1]
        x, y, c = lax.axis_index("x"), lax.axis_index("y"), lax.axis_index("c")
        mine = 2 * x + y if mode == "gather" else 4 * x + 2 * y + c
        for t in range(n):
            for k, (px, py, pc) in enumerate(_exchange_peers(mode)):
                if mode in ("gather", "all"):
                    s, d = ins[t], lands[t].at[mine]
                elif mode == "scatter":
                    s, d = ins[t].at[2 * px + py], lands[t].at[k]
                else:
                    s, d = ins[t], lands[t]
                pltpu.make_async_remote_copy(src_ref=s, dst_ref=d, send_sem=send_sems[t], recv_sem=recv_sems[t],
                                             device_id=(px, py, pc), device_id_type=MESH).start()
            if mode in ("gather", "all"):
                pltpu.make_async_copy(ins[t], lands[t].at[mine], send_sems[t]).start()
        token[...] = jnp.zeros_like(token)

    hbm = lambda a: pltpu.with_memory_space_constraint(a, pltpu.HBM)
    out = pl.pallas_call(
        body, name=name,
        in_specs=[HBM_SPEC] * (2 * n),
        out_specs=[SEM_SPEC] * (2 * n) + [HBM_SPEC] * (2 * n) + [pl.BlockSpec(memory_space=pltpu.VMEM)],
        out_shape=[pltpu.SemaphoreType.DMA(())] * (2 * n)
        + [pltpu.HBM(a.shape, a.dtype) for a in list(srcs) + list(zones)] + [_sds((8, LANES), F32)],
        input_output_aliases={i: 2 * n + i for i in range(2 * n)},
        compiler_params=pltpu.CompilerParams(has_side_effects=pltpu.SideEffectType.DATAFLOW_SIDE_EFFECTING),
    )(*[hbm(a) for a in list(srcs) + list(zones)])
    return out[:n], out[n:2 * n], out[2 * n:3 * n], out[3 * n:4 * n], out[-1]


def _exchange_wait(mode, started, after, *, name):
    send_sems, recv_sems, srcs, zones, _ = started
    n = len(srcs)
    afters = tuple(after) if isinstance(after, (tuple, list)) else (after,)

    def body(*refs):
        lands = refs[n:2 * n]
        send_refs, recv_refs = refs[2 * n:3 * n], refs[3 * n:4 * n]
        me = (lax.axis_index("x"), lax.axis_index("y"), lax.axis_index("c"))
        n_remote = {"gather": N_CHIPS - 1, "scatter": N_CHIPS - 1, "all": N_DEV - 1, "swap": 1}[mode]
        for t in range(n):
            got = lands[t] if mode == "swap" else lands[t].at[pl.ds(0, n_remote)]
            sent = lands[t] if mode in ("gather", "all") else got
            pltpu.make_async_remote_copy(src_ref=sent, dst_ref=sent, send_sem=send_refs[t], recv_sem=recv_refs[t],
                                         device_id=me, device_id_type=MESH).wait_send()
            pltpu.make_async_remote_copy(src_ref=got, dst_ref=got, send_sem=send_refs[t], recv_sem=recv_refs[t],
                                         device_id=me, device_id_type=MESH).wait_recv()

    out = pl.pallas_call(
        body, name=name,
        in_specs=[HBM_SPEC] * (2 * n) + [SEM_SPEC] * (2 * n) + [pl.BlockSpec(memory_space=pl.ANY)] * len(afters),
        out_specs=[HBM_SPEC] * (2 * n),
        out_shape=[pltpu.HBM(a.shape, a.dtype) for a in list(srcs) + list(zones)],
        input_output_aliases={i: i for i in range(2 * n)},
        compiler_params=pltpu.CompilerParams(has_side_effects=pltpu.SideEffectType.DATAFLOW_SIDE_EFFECTING),
    )(*srcs, *zones, *send_sems, *recv_sems, *afters)
    return out[:n], out[n:]


def _swap_with_sibling(parts):
    n = len(parts)

    def body(*refs):
        ins, outs = refs[:n], refs[n:2 * n]
        send_sems, recv_sems = refs[2 * n:]
        sib = (lax.axis_index("x"), lax.axis_index("y"), 1 - lax.axis_index("c"))
        cps = []
        for t in range(n):
            cp = pltpu.make_async_remote_copy(
                src_ref=ins[t], dst_ref=outs[t], send_sem=send_sems.at[t], recv_sem=recv_sems.at[t],
                device_id=sib, device_id_type=MESH)
            cp.start()
            cps.append(cp)
        for cp in cps:
            cp.wait()

    any_spec = pl.BlockSpec(memory_space=pl.ANY)
    return pl.pallas_call(
        body, name="swap_with_sibling", in_specs=[any_spec] * n, out_specs=[any_spec] * n,
        out_shape=[_sds(p.shape, p.dtype) for p in parts],
        scratch_shapes=[pltpu.SemaphoreType.DMA((n,)), pltpu.SemaphoreType.DMA((n,))])(*parts)


BIG = ("w_in", "w_out", "w_xq", "w_xk", "w_xv", "w_xo", "w_up", "w_down")
COL_SHARDED = ("w_in", "w_up")


def _as_matrix(name, w4):
    if name in COL_SHARDED:
        return w4
    return w4.reshape(1, w4.shape[0] * w4.shape[1], w4.shape[2])


def _transposed(w3):
    nsh, K, n = w3.shape
    return jnp.swapaxes(w3, 1, 2).reshape(1, nsh * n, K)


def _shard_layout(name, g):
    if name in COL_SHARDED:
        return g
    return g.reshape(N_CHIPS, g.shape[0] * g.shape[1] // N_CHIPS, g.shape[2])


def _local_step(x, mem, target, vecs, comm):
    S = x.shape[0]
    tables = _rope_tables(S)

    xn = _rms_fwd(x, vecs["norm_mix_g"], name="rms_mix")
    w_in, conv_w32 = comm["first"]((xn,) + tuple(tables))
    y = _mm_nn(xn, w_in, name="mm_in", tm=2048, tn=640)
    qk, v_perm = _rope_fwd(y, tables)
    v_src = [(y, 2)] + [(v, 0) for v in v_perm[1:]]
    outs, lses = zip(*[_att_fwd(qk[p], v_src[p], d, name=f"att_fwd_d{d}") for p, d in enumerate(DILATIONS)])
    att, lg = _att_combine(outs, lses)
    cv, u1 = _conv_fwd(y, conv_w32, vecs["conv_b"], vecs["conv_ln_g"], vecs["conv_ln_b"])
    Wm = {k: _as_matrix(k, v) for k, v in comm["rest"]((att, cv)).items()}
    Wm["w_in"] = w_in
    h1, hn = _mm_rows((att, cv), Wm["w_out"], _residual_norm_tail, name="mm_out_rms", rows_in=(x,),
                      vecs_in=(vecs["norm_x_g"],), rows_out=(F32, BF16))
    xq = _mm_nn(hn, Wm["w_xq"], name="mm_xq")
    mn = _rms_fwd(mem, vecs["norm_mem_g"], name="rms_mem")
    xk = _mm_nn(mn, Wm["w_xk"], name="mm_xk")
    xv = _mm_nn(mn, Wm["w_xv"], name="mm_xv")
    xo = _xatt_fwd(xq, xk, xv)
    h2, hm = _mm_rows(xo, Wm["w_xo"], _residual_norm_tail, name="mm_xo_rms", rows_in=(h1,),
                      vecs_in=(vecs["norm_mlp_g"],), rows_out=(F32, BF16))
    relu_up = _mm_nn(hm, Wm["w_up"], name="mm_up", relu=True, tm=2048)
    sums = ((8, D_MODEL),)
    dh3, dh3b, dg_final, loss = _mm_rows(
        relu_up, Wm["w_down"], _loss_tail, name="mm_down_loss", rows_in=(h2, target), vecs_in=(vecs["norm_final_g"],),
        rows_out=(F32, BF16), sums_out=sums + ((8, LANES),), a_squared=True, tm=256)
    g = {}
    g["w_down"] = _mm_tn(relu_up, dh3b, 1, name="dw_down", a_squared=True)
    dup = _mm_nt(dh3b, Wm["w_down"], name="d_act", out_dtype=BF16, mul=relu_up, tm=2048)
    g["w_up"] = _mm_tn(hm, dup, N_CHIPS, name="dw_up")
    sent = comm["send_mlp"]({k: _shard_layout(k, g[k]) for k in ("w_down", "w_up")})
    dh2, dh2b, dg_mlp = _mm_rows(
        dup, _transposed(Wm["w_up"]), _rms_bwd_tail(True), name="d_hm_rms", rows_in=(h2, dh3),
        vecs_in=(vecs["norm_mlp_g"] + sent[0:1, 0:1],), rows_out=(F32, BF16), sums_out=sums, tm=256)
    g["w_xo"] = _mm_tn(xo, dh2b, 1, name="dw_xo")
    dxo = _mm_nt(dh2b, Wm["w_xo"], name="d_xo", out_dtype=BF16)
    dxq, dxk, dxv = _xatt_bwd(xq, xk, xv, dxo)
    g["w_xq"] = _mm_tn(hn, dxq, 1, name="dw_xq")
    dh1, dh1b, dg_x = _mm_rows(
        dxq, _transposed(Wm["w_xq"]), _rms_bwd_tail(True), name="d_hn_rms", rows_in=(h1, dh2),
        vecs_in=(vecs["norm_x_g"],), rows_out=(F32, BF16), sums_out=sums)
    dxkb, dxvb = dxk.astype(BF16), dxv.astype(BF16)
    g["w_xk"] = _mm_tn(mn, dxkb, 1, name="dw_xk")
    g["w_xv"] = _mm_tn(mn, dxvb, 1, name="dw_xv")
    dmn = _mm_nt(jnp.concatenate([dxkb, dxvb], axis=1),
                 jnp.concatenate([Wm["w_xk"], Wm["w_xv"]], axis=2), name="d_mn", out_dtype=BF16)
    _, _, dg_mem = _rms_bwd(dmn, mem, vecs["norm_mem_g"], None, name="rms_bwd_mem", bf16_copy=False)
    g["w_out"] = jnp.concatenate([_mm_tn(att, dh1b, 1, name="dw_out_att"), _mm_tn(cv, dh1b, 1, name="dw_out_conv")],
                                 axis=1)
    sent = comm["send_att"]({k: _shard_layout(k, g[k]) for k in ("w_out", "w_xq", "w_xk", "w_xv", "w_xo")})
    dac = _mm_nt(dh1b, Wm["w_out"], name="d_mix", out_dtype=BF16)
    dag, dconv_w, dconv_small = _conv_bwd(dac, u1, y, conv_w32, vecs["conv_ln_g"] + sent[0:1, 0:1],
                                          vecs["conv_ln_b"])
    delta, do_perm = _att_delta(dac, att)
    do_src = [(dac, 0)] + [(t, 0) for t in do_perm[1:]]
    dq, dk, dv = zip(*[_att_bwd(qk[p], v_src[p], do_src[p], lg[p], delta[p], d, name=f"att_bwd_d{d}")
                       for p, d in enumerate(DILATIONS)])
    dy = _assemble_dy(dq, dk, dv, dag, tables)
    sent = comm["send_in"]({"w_in": _mm_tn(xn, dy, N_CHIPS, name="dw_in", tn=640)})
    grad_x, dg_mix = _mm_rows(
        dy, _transposed(Wm["w_in"]), _rms_bwd_tail(False), name="d_xn_rms", rows_in=(x, dh1),
        vecs_in=(vecs["norm_mix_g"] + sent[0:1, 0:1],), rows_out=(F32,), sums_out=sums)

    small = dict(conv_w=dconv_w, conv_small=dconv_small, norm_mix_g=dg_mix, norm_x_g=dg_x, norm_mem_g=dg_mem,
                 norm_mlp_g=dg_mlp, norm_final_g=dg_final, loss=loss)
    return grad_x, small


SMALL_ORDER = ("conv_w", "conv_small", "norm_mix_g", "norm_x_g", "norm_mem_g", "norm_mlp_g", "norm_final_g", "loss")


def _pack_small(small):
    rows, offs, pos = [], {}, 0
    for k in SMALL_ORDER:
        a = small[k]
        a = a.reshape(a.shape[0] * a.shape[1] // SMALL_W, SMALL_W)
        pad = (-a.shape[0]) % 8
        if pad:
            a = jnp.pad(a, ((0, pad), (0, 0)))
        rows.append(a)
        offs[k] = pos
        pos += a.shape[0]
    return jnp.concatenate(rows, axis=0), offs


def kernel(x, mem, norm_mix_g, w_in, conv_w, conv_b, conv_ln_g, conv_ln_b, w_out, norm_x_g, norm_mem_g, w_xq, w_xk, w_xv, w_xo, norm_mlp_g, w_up, w_down, norm_final_g, loss_target, m_norm_mix_g, m_w_in, m_conv_w, m_conv_b, m_conv_ln_g, m_conv_ln_b, m_w_out, m_norm_x_g, m_norm_mem_g, m_w_xq, m_w_xk, m_w_xv, m_w_xo, m_norm_mlp_g, m_w_up, m_w_down, m_norm_final_g, v_norm_mix_g, v_w_in, v_conv_w, v_conv_b, v_conv_ln_g, v_conv_ln_b, v_w_out, v_norm_x_g, v_norm_mem_g, v_w_xq, v_w_xk, v_w_xv, v_w_xo, v_norm_mlp_g, v_w_up, v_w_down, v_norm_final_g):
    names = ("norm_mix_g", "w_in", "conv_w", "conv_b", "conv_ln_g", "conv_ln_b", "w_out", "norm_x_g", "norm_mem_g",
             "w_xq", "w_xk", "w_xv", "w_xo", "norm_mlp_g", "w_up", "w_down", "norm_final_g")
    wts = dict(zip(names, (norm_mix_g, w_in, conv_w, conv_b, conv_ln_g, conv_ln_b, w_out, norm_x_g, norm_mem_g,
                           w_xq, w_xk, w_xv, w_xo, norm_mlp_g, w_up, w_down, norm_final_g)))
    mom = dict(zip(names, (m_norm_mix_g, m_w_in, m_conv_w, m_conv_b, m_conv_ln_g, m_conv_ln_b, m_w_out, m_norm_x_g,
                           m_norm_mem_g, m_w_xq, m_w_xk, m_w_xv, m_w_xo, m_norm_mlp_g, m_w_up, m_w_down, m_norm_final_g)))
    var = dict(zip(names, (v_norm_mix_g, v_w_in, v_conv_w, v_conv_b, v_conv_ln_g, v_conv_ln_b, v_w_out, v_norm_x_g,
                           v_norm_mem_g, v_w_xq, v_w_xk, v_w_xv, v_w_xo, v_norm_mlp_g, v_w_up, v_w_down, v_norm_final_g)))
    chip = 2 * lax.axis_index("x") + lax.axis_index("y")

    def zone(shard):
        return lax.empty((N_CHIPS,) + shard.shape, shard.dtype)

    conv_w_pad = jnp.pad(wts["conv_w"][0], ((0, 1), (0, 0)))
    rest = tuple(k for k in BIG if k != "w_in")
    shards = [wts["w_in"][0].astype(BF16), conv_w_pad] + [wts[k][0].astype(BF16) for k in rest]
    gathering = _exchange_start("gather", shards, [zone(s) for s in shards], name="gather_start")
    gathering_first = tuple(part[:2] for part in gathering[:4]) + (gathering[4],)
    gathering_rest = tuple(part[2:] for part in gathering[:4]) + (gathering[4],)
    sending = {}

    def wait_first(after):
        _, (w_in_all, conv_w_all) = _exchange_wait("gather", gathering_first, after, name="gather_first_wait")
        return w_in_all, jnp.transpose(conv_w_all, (1, 0, 2)).reshape(32, D_CONV)

    def wait_rest(after):
        _, zones = _exchange_wait("gather", gathering_rest, after, name="gather_rest_wait")
        return dict(zip(rest, zones))

    def send(group, grads):
        keys = tuple(grads)
        zones = [lax.empty((N_CHIPS - 1,) + grads[k].shape[1:], grads[k].dtype) for k in keys]
        sending[group] = (keys, _exchange_start("scatter", [grads[k] for k in keys], zones,
                                                name=f"scatter_{group}_start"))
        return sending[group][1][4]

    comm = dict(first=wait_first, rest=wait_rest, send_mlp=lambda grads: send("mlp", grads),
                send_att=lambda grads: send("att", grads), send_in=lambda grads: send("in", grads))
    vecs = {k: wts[k] for k in ("conv_b", "conv_ln_g", "conv_ln_b", "norm_x_g", "norm_mem_g", "norm_mlp_g")}
    vecs["norm_mix_g"] = wts["norm_mix_g"] + gathering[4][0:1, 0:1]
    vecs["norm_final_g"] = wts["norm_final_g"].reshape(1, D_MODEL)
    grad_x, small = _local_step(x[0], mem[0], loss_target[0], vecs, comm)

    packed, offs = _pack_small(small)
    me_arr = jnp.reshape(chip, (1,)).astype(jnp.int32)
    gathering_small = _exchange_start("all", [packed], [lax.empty((N_DEV,) + packed.shape, packed.dtype)],
                                      name="allgather_small_start")
    sums = {}

    def settle(group, after):
        keys, started = sending[group]
        srcs, zones = _exchange_wait("scatter", started, after, name=f"scatter_{group}_wait")
        for k, own, got in zip(keys, srcs, zones):
            sums[k] = _sum_partials(own, got, me_arr, name=f"sum_{k}")

    settle("mlp", gathering_small[4])
    settle("att", gathering_small[4])
    early = tuple(sums)
    swapping = _exchange_start("swap", [sums[k] for k in early], [lax.empty(sums[k].shape, F32) for k in early],
                               name="swap_early_start")
    settle("in", swapping[4])
    _, (gath,) = _exchange_wait("all", gathering_small, sums["w_in"], name="allgather_small_wait")

    where = {"conv_w": ("conv_w", offs["conv_w"]), "conv_b": ("row", offs["conv_small"]),
             "conv_ln_g": ("row", offs["conv_small"] + 1), "conv_ln_b": ("row", offs["conv_small"] + 2)}
    where.update({k: ("gain", offs[k]) for k in ("norm_mix_g", "norm_x_g", "norm_mem_g", "norm_mlp_g", "norm_final_g")})
    as_2d = lambda a: a.reshape(a.shape[-2] if a.ndim > 1 else 1, a.shape[-1])
    tot_small, updates = _adamw_small(gath, me_arr, [(where[k], as_2d(wts[k]), as_2d(mom[k]), as_2d(var[k]))
                                                     for k in where])
    res = dict(zip(where, updates))
    loss = tot_small[offs["loss"], 0]

    sib = {"w_in": _swap_with_sibling([sums["w_in"]])[0]}
    mine_early, sib_early = _exchange_wait("swap", swapping, sib["w_in"], name="swap_early_wait")
    sums.update(zip(early, mine_early))
    sib.update(zip(early, sib_early))
    for k in BIG:
        res[k] = _adamw([sums[k], sib[k]], wts[k][0], mom[k][0], var[k][0], name=f"adamw_{k}")

    outs = [loss, grad_x[None]]
    for j in range(4):
        outs += [res[k][j].reshape(wts[k].shape) for k in names]
    return tuple(outs)
```

```python
import jax
import jax.numpy as jnp
from jax import lax
from jax.experimental import pallas as pl
from jax.experimental.pallas import tpu as pltpu

F32 = jnp.float32
BF16 = jnp.bfloat16
MESH = pl.DeviceIdType.MESH

D_MODEL = 1024
ATT_HEADS = 8
HEAD_DIM = 64
D_ATT = ATT_HEADS * HEAD_DIM
D_CONV = D_MODEL - D_ATT
DILATIONS = (1, 4, 16)
HALF = 64
ROPE_THETA = 500000.0
ROT_DIM = HEAD_DIM // 4
CONV_WIDTH = 31
CONV_PAD = (CONV_WIDTH - 1) // 2
XATT_HEADS = 4
XATT_HEAD_DIM = D_MODEL // XATT_HEADS
D_FF = 4 * D_MODEL
D_IN = 3 * D_ATT + 2 * D_CONV
EPS = 1e-6
NEG_INF = -1e30
N_CHIPS = 4
N_DEV = 8

ADAM_LR = 0.001
ADAM_B1 = 0.9
ADAM_B2 = 0.999
ADAM_EPS = 1e-08
ADAM_WD = 0.01
ADAM_STEP = 10

VMEM_LIMIT_V7X = 56 * 1024 * 1024
LANES = 128
HALO = 16
CONV_ROWS = 64
ATT_BLOCK = 128
SMALL_W = 512


def _params(*sem):
    return pltpu.CompilerParams(dimension_semantics=sem, vmem_limit_bytes=VMEM_LIMIT_V7X)


def _sds(shape, dtype):
    return jax.ShapeDtypeStruct(shape, dtype)


def _squared(a):
    af = a.astype(F32)
    return (af * af).astype(BF16)


def _mm_nn(a, w3, *, name, out_dtype=BF16, res=None, relu=False, a_squared=False, tm=1024, tn=None, tk=1024):
    M, K = a.shape
    nsh, _, n = w3.shape
    tm, tk = min(tm, M), min(tk, K)
    tn = tn or min(n, 1024)
    npt, nk = n // tn, K // tk
    nj, N = nsh * npt, nsh * n
    n_out = 1

    def body(*refs):
        a_ref, w_ref = refs[0], refs[1]
        pos = 2
        res_ref = None
        if res is not None:
            res_ref = refs[pos]
            pos += 1
        outs = refs[pos:pos + n_out]
        acc_ref = refs[pos + n_out] if nk > 1 else None

        def finish(acc):
            if res_ref is not None:
                acc = acc + res_ref[...]
            if relu:
                acc = jnp.maximum(acc, 0.0)
            outs[0][...] = acc.astype(outs[0].dtype)

        a_val = _squared(a_ref[...]) if a_squared else a_ref[...]
        part = jnp.dot(a_val, w_ref[...], preferred_element_type=F32)
        if nk == 1:
            finish(part)
        else:
            k = pl.program_id(2)

            @pl.when(k == 0)
            def _():
                acc_ref[...] = part

            @pl.when(k > 0)
            def _():
                acc_ref[...] += part

            @pl.when(k == nk - 1)
            def _():
                finish(acc_ref[...])

    in_specs = [pl.BlockSpec((tm, tk), lambda i, j, k: (i, k)),
                pl.BlockSpec((None, tk, tn), lambda i, j, k: (j // npt, k, j % npt))]
    args = [a, w3]
    if res is not None:
        in_specs.append(pl.BlockSpec((tm, tn), lambda i, j, k: (i, j)))
        args.append(res)
    out_spec = pl.BlockSpec((tm, tn), lambda i, j, k: (i, j))
    out = pl.pallas_call(
        body, name=name, grid=(M // tm, nj, nk), in_specs=in_specs,
        out_specs=[out_spec] * n_out, out_shape=[_sds((M, N), out_dtype)] * n_out,
        scratch_shapes=[pltpu.VMEM((tm, tn), F32)] if nk > 1 else [],
        compiler_params=_params("parallel", "parallel", "arbitrary"))(*args)
    return out[0]


def _mm_nt(dy, w3, *, name, out_dtype=F32, mul=None, tm=1024, tn=None, tko=1024):
    M, N = dy.shape
    nsh, K, n = w3.shape
    tm, tko = min(tm, M), min(tko, K)
    tn = tn or min(n, 1024)
    npt = n // tn
    nj = nsh * npt

    def body(*refs):
        dy_ref, w_ref = refs[0], refs[1]
        pos = 2
        mul_ref = None
        if mul is not None:
            mul_ref = refs[pos]
            pos += 1
        out_ref = refs[pos]
        acc_ref = refs[pos + 1] if nj > 1 else None

        def finish(acc):
            if mul_ref is not None:
                acc = acc * (2.0 * mul_ref[...].astype(F32))
            out_ref[...] = acc.astype(out_ref.dtype)

        part = lax.dot_general(dy_ref[...], w_ref[...], (((1,), (1,)), ((), ())), preferred_element_type=F32)
        if nj == 1:
            finish(part)
        else:
            j = pl.program_id(2)

            @pl.when(j == 0)
            def _():
                acc_ref[...] = part

            @pl.when(j > 0)
            def _():
                acc_ref[...] += part

            @pl.when(j == nj - 1)
            def _():
                finish(acc_ref[...])

    in_specs = [pl.BlockSpec((tm, tn), lambda i, ko, j: (i, j)),
                pl.BlockSpec((None, tko, tn), lambda i, ko, j: (j // npt, ko, j % npt))]
    args = [dy, w3]
    if mul is not None:
        in_specs.append(pl.BlockSpec((tm, tko), lambda i, ko, j: (i, ko)))
        args.append(mul)
    return pl.pallas_call(
        body, name=name, grid=(M // tm, K // tko, nj), in_specs=in_specs,
        out_specs=pl.BlockSpec((tm, tko), lambda i, ko, j: (i, ko)), out_shape=_sds((M, K), out_dtype),
        scratch_shapes=[pltpu.VMEM((tm, tko), F32)] if nj > 1 else [],
        compiler_params=_params("parallel", "parallel", "arbitrary"))(*args)


def _mm_tn(a, dy, nsh, *, name, out_dtype=BF16, a_squared=False, tm=2048, tk=1024, tn=None):
    M, K = a.shape
    N = dy.shape[1]
    n = N // nsh
    tm, tk = min(tm, M), min(tk, K)
    tn = tn or min(n, 1024)
    npt = n // tn
    nj, nm = nsh * npt, M // tm

    def body(a_ref, dy_ref, out_ref, acc_ref):
        m = pl.program_id(2)
        a_val = _squared(a_ref[...]) if a_squared else a_ref[...]
        part = lax.dot_general(a_val, dy_ref[...], (((0,), (0,)), ((), ())), preferred_element_type=F32)

        @pl.when(m == 0)
        def _():
            acc_ref[...] = part

        @pl.when(m > 0)
        def _():
            acc_ref[...] += part

        @pl.when(m == nm - 1)
        def _():
            out_ref[...] = acc_ref[...].astype(out_ref.dtype)

    return pl.pallas_call(
        body, name=name, grid=(K // tk, nj, nm),
        in_specs=[pl.BlockSpec((tm, tk), lambda kk, j, m: (m, kk)),
                  pl.BlockSpec((tm, tn), lambda kk, j, m: (m, j))],
        out_specs=pl.BlockSpec((None, tk, tn), lambda kk, j, m: (j // npt, kk, j % npt)),
        out_shape=_sds((nsh, K, n), out_dtype),
        scratch_shapes=[pltpu.VMEM((tk, tn), F32)],
        compiler_params=_params("parallel", "parallel", "arbitrary"))(a, dy)


def _rms_fwd(x, g, *, name, tm=512):
    M, Dm = x.shape
    tm = min(tm, M)

    def body(x_ref, g_ref, o_ref):
        xf = x_ref[...]
        r = lax.rsqrt(jnp.mean(xf * xf, axis=-1, keepdims=True) + EPS)
        o_ref[...] = (xf * r * g_ref[...]).astype(o_ref.dtype)

    return pl.pallas_call(
        body, name=name, grid=(M // tm,),
        in_specs=[pl.BlockSpec((tm, Dm), lambda i: (i, 0)), pl.BlockSpec((1, Dm), lambda i: (0, 0))],
        out_specs=pl.BlockSpec((tm, Dm), lambda i: (i, 0)), out_shape=_sds((M, Dm), BF16),
        compiler_params=_params("parallel"))(x, g)


def _rms_bwd(dxn, x, g, dres, *, name, bf16_copy=True, tm=512):
    M, Dm = x.shape
    tm = min(tm, M)
    has_res = dres is not None

    def body(*refs):
        dxn_ref, x_ref, g_ref = refs[:3]
        dres_ref = refs[3] if has_res else None
        dx_ref, dg_ref = refs[-1 - 1 - bf16_copy], refs[-1]
        dxb_ref = refs[-2] if bf16_copy else None
        i = pl.program_id(0)
        xf = x_ref[...]
        r = lax.rsqrt(jnp.mean(xf * xf, axis=-1, keepdims=True) + EPS)
        nrm = xf * r
        dxn_f = dxn_ref[...].astype(F32)
        dn = dxn_f * g_ref[...]
        dx = r * (dn - nrm * jnp.mean(dn * nrm, axis=-1, keepdims=True))
        if has_res:
            dx = dx + dres_ref[...]
        dx_ref[...] = dx
        if bf16_copy:
            dxb_ref[...] = dx.astype(dxb_ref.dtype)

        @pl.when(i == 0)
        def _():
            dg_ref[...] = jnp.zeros_like(dg_ref)

        dg_ref[0:1, :] += jnp.sum(dxn_f * nrm, axis=0, keepdims=True)

    row = pl.BlockSpec((tm, Dm), lambda i: (i, 0))
    in_specs = [row, row, pl.BlockSpec((1, Dm), lambda i: (0, 0))] + ([row] if has_res else [])
    args = [dxn, x, g] + ([dres] if has_res else [])
    out = pl.pallas_call(
        body, name=name, grid=(M // tm,), in_specs=in_specs,
        out_specs=[row] * (1 + bf16_copy) + [pl.BlockSpec((8, Dm), lambda i: (0, 0))],
        out_shape=[_sds((M, Dm), F32)] + [_sds((M, Dm), BF16)] * bf16_copy + [_sds((8, Dm), F32)],
        compiler_params=_params("arbitrary"))(*args)
    return out[0], (out[1] if bf16_copy else None), out[-1]


def _mm_rows(a, w3, tail, *, name, rows_in=(), vecs_in=(), rows_out=(), sums_out=(), a_squared=False, tm=512):
    parts = a if isinstance(a, (tuple, list)) else (a,)
    M = parts[0].shape[0]
    K, N = w3.shape[1], w3.shape[2]
    tm = min(tm, M)
    n_a, n_ri, n_vi, n_ro = len(parts), len(rows_in), len(vecs_in), len(rows_out)

    def body(*refs):
        a_refs, w_ref, refs = refs[:n_a], refs[n_a], refs[n_a + 1:]
        rin, vin = refs[:n_ri], refs[n_ri:n_ri + n_vi]
        rout, sout = refs[n_ri + n_vi:n_ri + n_vi + n_ro], refs[n_ri + n_vi + n_ro:]

        @pl.when(pl.program_id(0) == 0)
        def _():
            for s in sout:
                s[...] = jnp.zeros_like(s)

        a_val = a_refs[0][...] if n_a == 1 else jnp.concatenate([r[...] for r in a_refs], axis=1)
        if a_squared:
            a_val = _squared(a_val)
        tail(jnp.dot(a_val, w_ref[0], preferred_element_type=F32), rin, vin, rout, sout)

    row = pl.BlockSpec((tm, N), lambda i: (i, 0))
    once = lambda shape: pl.BlockSpec(shape, lambda i: (0,) * len(shape))
    return pl.pallas_call(
        body, name=name, grid=(M // tm,),
        in_specs=[pl.BlockSpec((tm, p.shape[1]), lambda i: (i, 0)) for p in parts] + [once((1, K, N))]
        + [row] * n_ri + [once((1, N))] * n_vi,
        out_specs=[row] * n_ro + [once(s) for s in sums_out],
        out_shape=[_sds((M, N), dt) for dt in rows_out] + [_sds(s, F32) for s in sums_out],
        compiler_params=_params("arbitrary"))(*parts, w3, *rows_in, *vecs_in)


def _residual_norm_tail(prod, rows_in, vecs_in, rows_out, sums_out):
    hf = prod + rows_in[0][...]
    rows_out[0][...] = hf
    r = lax.rsqrt(jnp.mean(hf * hf, axis=-1, keepdims=True) + EPS)
    rows_out[1][...] = (hf * r * vecs_in[0][...]).astype(BF16)


def _rms_bwd_tail(bf16_copy):
    def tail(dxn, rows_in, vecs_in, rows_out, sums_out):
        xf = rows_in[0][...]
        r = lax.rsqrt(jnp.mean(xf * xf, axis=-1, keepdims=True) + EPS)
        nrm = xf * r
        dn = dxn * vecs_in[0][...]
        dx = r * (dn - nrm * jnp.mean(dn * nrm, axis=-1, keepdims=True)) + rows_in[1][...]
        rows_out[0][...] = dx
        if bf16_copy:
            rows_out[1][...] = dx.astype(BF16)
        sums_out[0][0:1, :] += jnp.sum(dxn * nrm, axis=0, keepdims=True)

    return tail


def _loss_tail(prod, rows_in, vecs_in, rows_out, sums_out):
    hf = prod + rows_in[0][...]
    r = lax.rsqrt(jnp.mean(hf * hf, axis=-1, keepdims=True) + EPS)
    nrm = hf * r
    gv = vecs_in[0][...]
    err = nrm * gv - rows_in[1][...]
    dy = err * (1.0 / hf.shape[-1])
    dn = dy * gv
    dh = r * (dn - nrm * jnp.mean(dn * nrm, axis=-1, keepdims=True))
    rows_out[0][...] = dh
    rows_out[1][...] = dh.astype(BF16)
    sums_out[0][0:1, :] += jnp.sum(dy * nrm, axis=0, keepdims=True)
    part = 0.5 * jnp.sum(jnp.mean(err * err, axis=-1, keepdims=True), axis=0, keepdims=True)
    sel = (lax.broadcasted_iota(jnp.int32, (8, 128), 0) == 0) & (lax.broadcasted_iota(jnp.int32, (8, 128), 1) == 0)
    sums_out[1][...] += jnp.where(sel, part, 0.0)


def _class_spec(tm, d, width):
    return pl.BlockSpec((d, tm // d, width), lambda i: (0, i, 0))


def _row_scratch(tm, width):
    return pltpu.VMEM((width // LANES, tm, LANES), F32)


def _fill(scr, val):
    for c in range(scr.shape[0]):
        scr[c] = val[:, c * LANES:(c + 1) * LANES]


def _to_classes(scr, out_ref, d):
    n = scr.shape[1] // d
    for r in range(d):
        for c in range(scr.shape[0]):
            out_ref[r, :, c * LANES:(c + 1) * LANES] = scr[c, pl.ds(r, n, stride=d), :].astype(out_ref.dtype)


def _from_classes(in_ref, scr, d):
    n = scr.shape[1] // d
    for r in range(d):
        blk = in_ref[r].astype(F32)
        for c in range(scr.shape[0]):
            scr[c, pl.ds(r, n, stride=d), :] = blk[:, c * LANES:(c + 1) * LANES]
    return jnp.concatenate([scr[c] for c in range(scr.shape[0])], axis=1)


def _rope_tables(S):
    half = ROT_DIM // 2
    freqs = ROPE_THETA ** (-jnp.arange(0, ROT_DIM, 2, dtype=F32) / ROT_DIM)
    ang = jnp.arange(S, dtype=F32)[:, None] * freqs[None, :]
    cos, sin = jnp.cos(ang), jnp.sin(ang)
    ones = jnp.ones((S, HEAD_DIM - ROT_DIM), F32)
    zeros = jnp.zeros((S, HEAD_DIM - ROT_DIM), F32)
    zh = jnp.zeros((S, half), F32)
    c = jnp.concatenate([cos, cos, ones], axis=1)
    sa = jnp.concatenate([-sin, zh, zeros], axis=1)
    sb = jnp.concatenate([zh, sin, zeros], axis=1)
    return tuple(jnp.tile(t, (1, LANES // HEAD_DIM)) for t in (c, sa, sb))


def _rope_fwd(y, tables, *, tm=512):
    S = y.shape[0]
    W = 2 * D_ATT
    tm = min(tm, S)
    half = ROT_DIM // 2
    dils = [d for d in DILATIONS if d > 1]

    def body(y_ref, c_ref, sa_ref, sb_ref, qk_ref, *rest):
        qk_outs, v_outs = rest[:len(dils)], rest[len(dils):2 * len(dils)]
        scr_qk, scr_v = rest[2 * len(dils):]
        t = y_ref[:, 0:W].astype(F32)
        rep = W // LANES
        c, sa, sb = (jnp.tile(r[...], (1, rep)) for r in (c_ref, sa_ref, sb_ref))
        rot = t * c + pltpu.roll(t, W - half, axis=1) * sa + pltpu.roll(t, half, axis=1) * sb
        qk_ref[...] = rot.astype(qk_ref.dtype)
        _fill(scr_qk, rot)
        _fill(scr_v, y_ref[:, W:W + D_ATT].astype(F32))
        for d, qo, vo in zip(dils, qk_outs, v_outs):
            _to_classes(scr_qk, qo, d)
            _to_classes(scr_v, vo, d)

    tab = pl.BlockSpec((tm, LANES), lambda i: (i, 0))
    out = pl.pallas_call(
        body, name="rope_fwd", grid=(S // tm,),
        in_specs=[pl.BlockSpec((tm, 3 * D_ATT), lambda i: (i, 0)), tab, tab, tab],
        out_specs=[pl.BlockSpec((tm, W), lambda i: (i, 0))] + [_class_spec(tm, d, W) for d in dils]
        + [_class_spec(tm, d, D_ATT) for d in dils],
        out_shape=[_sds((S, W), BF16)] + [_sds((d, S // d, W), BF16) for d in dils]
        + [_sds((d, S // d, D_ATT), BF16) for d in dils],
        scratch_shapes=[_row_scratch(tm, W), _row_scratch(tm, D_ATT)],
        compiler_params=_params("parallel"))(y, *tables)
    qk = [out[0]] + [o.reshape(S, W) for o in out[1:1 + len(dils)]]
    v = [None] + [o.reshape(S, D_ATT) for o in out[1 + len(dils):]]
    return qk, v


def _assemble_dy(dq, dk, dv, dag, tables, *, tm=512):
    S = dag.shape[0]
    tm = min(tm, S)
    half = ROT_DIM // 2
    W = D_ATT
    n_pat = len(DILATIONS)

    def body(*refs):
        groups = [refs[g * n_pat:(g + 1) * n_pat] for g in range(3)]
        dag_ref, c_ref, sa_ref, sb_ref, o_ref, scr = refs[3 * n_pat:]
        rep = W // LANES
        c, sa, sb = (jnp.tile(r[...], (1, rep)) for r in (c_ref, sa_ref, sb_ref))

        def total(rs):
            acc = rs[0][...].astype(F32)
            for d, r in zip(DILATIONS[1:], rs[1:]):
                acc = acc + _from_classes(r, scr, d)
            return acc

        def unrope(dr):
            return dr * c + pltpu.roll(dr * sa, half, axis=1) + pltpu.roll(dr * sb, W - half, axis=1)

        o_ref[:, 0:W] = unrope(total(groups[0])).astype(o_ref.dtype)
        o_ref[:, W:2 * W] = unrope(total(groups[1])).astype(o_ref.dtype)
        o_ref[:, 2 * W:3 * W] = total(groups[2]).astype(o_ref.dtype)
        o_ref[:, 3 * W:] = dag_ref[...]

    specs = [pl.BlockSpec((tm, W), lambda i: (i, 0))] + [_class_spec(tm, d, W) for d in DILATIONS[1:]]
    tab = pl.BlockSpec((tm, LANES), lambda i: (i, 0))
    args = [a if d == 1 else a.reshape(d, S // d, W) for grp in (dq, dk, dv) for d, a in zip(DILATIONS, grp)]
    return pl.pallas_call(
        body, name="assemble_dy", grid=(S // tm,),
        in_specs=specs * 3 + [pl.BlockSpec((tm, 2 * D_CONV), lambda i: (i, 0)), tab, tab, tab],
        out_specs=pl.BlockSpec((tm, D_IN), lambda i: (i, 0)), out_shape=_sds((S, D_IN), BF16),
        scratch_shapes=[_row_scratch(tm, W)],
        compiler_params=_params("parallel"))(*args, dag, *tables)


def _seq_specs(L, tb, col):
    nb, per, nh = L // tb, tb // HALF, L // HALF
    centre = pl.BlockSpec((tb, D_ATT), lambda r, i: (r * nb + i, col))
    prev = pl.BlockSpec((HALF, D_ATT), lambda r, i: (r * nh + jnp.maximum(i * per - 1, 0), col))
    nxt = pl.BlockSpec((HALF, D_ATT), lambda r, i: (r * nh + jnp.minimum((i + 1) * per, nh - 1), col))
    return prev, centre, nxt


def _band_mask(i, tq, L):
    shape = (tq, tq + 2 * HALF)
    c_idx = lax.broadcasted_iota(jnp.int32, shape, 0)
    w_idx = lax.broadcasted_iota(jnp.int32, shape, 1)
    diff = w_idx - c_idx
    wpos = i * tq - HALF + w_idx
    return (diff >= 0) & (diff <= 2 * HALF) & (wpos >= 0) & (wpos < L)


def _lane_groups():
    for c0 in range(0, D_ATT, LANES):
        yield slice(c0, c0 + LANES)


def _first_head(rows):
    return lax.broadcasted_iota(jnp.int32, (rows, LANES), 1) < HEAD_DIM


def _split_pair(x, first):
    zero = jnp.zeros_like(x)
    return jnp.where(first, x, zero), jnp.where(first, zero, x)


def _nt(a, b):
    return lax.dot_general(a, b, (((1,), (1,)), ((), ())), preferred_element_type=F32)


def _tn(a, b):
    return lax.dot_general(a, b, (((0,), (0,)), ((), ())), preferred_element_type=F32)


ATT_SCALE = HEAD_DIM ** -0.5


def _att_fwd(qk, v_src, d, *, name):
    S = qk.shape[0]
    L = S // d
    tq = min(ATT_BLOCK, L)
    v_arr, v_col = v_src

    def body(q_ref, kp_ref, kc_ref, kn_ref, vp_ref, vc_ref, vn_ref, o_ref, lse_ref):
        i = pl.program_id(1)
        valid = _band_mask(i, tq, L)
        q = q_ref[...] * ATT_SCALE
        kwin = jnp.concatenate([kp_ref[...], kc_ref[...], kn_ref[...]], axis=0)
        vwin = jnp.concatenate([vp_ref[...], vc_ref[...], vn_ref[...]], axis=0)
        first = _first_head(tq)
        groups = list(_lane_groups())
        heads = [(ls, t) for ls in groups for t in _split_pair(q[:, ls], first)]
        s = [jnp.where(valid, _nt(t, kwin[:, ls]), NEG_INF) for ls, t in heads]
        m = [jnp.max(t, axis=-1, keepdims=True) for t in s]
        p = [jnp.exp(t - mm) for t, mm in zip(s, m)]
        den = [jnp.sum(t, axis=-1, keepdims=True) for t in p]
        o = [jnp.dot(t.astype(BF16), vwin[:, ls], preferred_element_type=F32) * (1.0 / dd)
             for t, dd, (ls, _) in zip(p, den, heads)]
        lse = [mm + jnp.log(dd) for mm, dd in zip(m, den)]
        for g, ls in enumerate(groups):
            o_ref[:, ls] = jnp.where(first, o[2 * g], o[2 * g + 1]).astype(o_ref.dtype)
            lse_ref[:, ls] = jnp.where(first, lse[2 * g], lse[2 * g + 1])

    _, qc, _ = _seq_specs(L, tq, 0)
    kp, kc, kn = _seq_specs(L, tq, 1)
    vp, vc, vn = _seq_specs(L, tq, v_col)
    out = pl.BlockSpec((tq, D_ATT), lambda r, i: (r * (L // tq) + i, 0))
    return pl.pallas_call(
        body, name=name, grid=(d, L // tq),
        in_specs=[qc, kp, kc, kn, vp, vc, vn], out_specs=[out, out],
        out_shape=[_sds((S, D_ATT), BF16), _sds((S, D_ATT), F32)],
        compiler_params=_params("parallel", "parallel"))(qk, qk, qk, qk, v_arr, v_arr, v_arr)


def _att_combine(outs, lses, *, tm=512):
    S = outs[0].shape[0]
    tm = min(tm, S)
    dils = DILATIONS[1:]
    n_d = len(dils)

    def body(*refs):
        o_refs, l_refs = refs[0:1 + n_d], refs[1 + n_d:2 + 2 * n_d]
        att_ref, lg_ref = refs[2 + 2 * n_d:4 + 2 * n_d]
        lg_outs = refs[4 + 2 * n_d:4 + 3 * n_d]
        scr = refs[4 + 3 * n_d:]
        scr_o, scr_l, scr_lg = scr[:n_d], scr[n_d:2 * n_d], scr[2 * n_d]
        ls = [l_refs[0][...]] + [_from_classes(r, s, d) for r, s, d in zip(l_refs[1:], scr_l, dils)]
        os_ = [o_refs[0][...].astype(F32)] + [_from_classes(r, s, d) for r, s, d in zip(o_refs[1:], scr_o, dils)]
        mx = ls[0]
        for l in ls[1:]:
            mx = jnp.maximum(mx, l)
        es = [jnp.exp(l - mx) for l in ls]
        tot = es[0]
        num = es[0] * os_[0]
        for e, o in zip(es[1:], os_[1:]):
            tot = tot + e
            num = num + e * o
        att_ref[...] = (num / tot).astype(att_ref.dtype)
        lg = mx + jnp.log(tot)
        lg_ref[...] = lg
        _fill(scr_lg, lg)
        for d, out in zip(dils, lg_outs):
            _to_classes(scr_lg, out, d)

    nat = pl.BlockSpec((tm, D_ATT), lambda i: (i, 0))
    specs = [nat] + [_class_spec(tm, d, D_ATT) for d in dils]
    view = lambda arrs: [arrs[0]] + [a.reshape(d, S // d, D_ATT) for a, d in zip(arrs[1:], dils)]
    out = pl.pallas_call(
        body, name="att_combine", grid=(S // tm,), in_specs=specs * 2,
        out_specs=[nat, nat] + specs[1:],
        out_shape=[_sds((S, D_ATT), BF16), _sds((S, D_ATT), F32)] + [_sds((d, S // d, D_ATT), F32) for d in dils],
        scratch_shapes=[_row_scratch(tm, D_ATT)] * (2 * n_d + 1),
        compiler_params=_params("parallel"))(*view(list(outs)), *view(list(lses)))
    return out[0], [out[1]] + [o.reshape(S, D_ATT) for o in out[2:]]


def _att_delta(dac, att, *, tm=512):
    S = att.shape[0]
    tm = min(tm, S)
    dils = DILATIONS[1:]
    n_d = len(dils)

    def body(do_ref, o_ref, dl_ref, *rest):
        dl_outs, do_outs = rest[:n_d], rest[n_d:2 * n_d]
        scr_dl, scr_do = rest[2 * n_d:]
        do = do_ref[...].astype(F32)
        prod = do * o_ref[...].astype(F32)
        per_head = [jnp.broadcast_to(jnp.sum(prod[:, h * HEAD_DIM:(h + 1) * HEAD_DIM], axis=-1, keepdims=True),
                                     (tm, HEAD_DIM)) for h in range(ATT_HEADS)]
        dl = jnp.concatenate(per_head, axis=1)
        dl_ref[...] = dl
        _fill(scr_dl, dl)
        _fill(scr_do, do)
        for d, dlo, doo in zip(dils, dl_outs, do_outs):
            _to_classes(scr_dl, dlo, d)
            _to_classes(scr_do, doo, d)

    blk = pl.BlockSpec((tm, D_ATT), lambda i: (i, 0))
    out = pl.pallas_call(
        body, name="att_delta", grid=(S // tm,), in_specs=[blk, blk],
        out_specs=[blk] + [_class_spec(tm, d, D_ATT) for d in dils] * 2,
        out_shape=[_sds((S, D_ATT), F32)] + [_sds((d, S // d, D_ATT), F32) for d in dils]
        + [_sds((d, S // d, D_ATT), BF16) for d in dils],
        scratch_shapes=[_row_scratch(tm, D_ATT), _row_scratch(tm, D_ATT)],
        compiler_params=_params("parallel"))(dac, att)
    delta = [out[0]] + [o.reshape(S, D_ATT) for o in out[1:1 + n_d]]
    do = [None] + [o.reshape(S, D_ATT) for o in out[1 + n_d:]]
    return delta, do


def _att_bwd(qk, v_src, do_src, lg, delta, d, *, name):
    S = qk.shape[0]
    L = S // d
    tq = min(ATT_BLOCK, L)
    nb, per, nh = L // tq, tq // HALF, L // HALF
    n_blocks = d * nb
    win = tq + 2 * HALF
    lead = tq - HALF
    acc_rows = lead + win
    (v_arr, v_col), (do_arr, do_col) = v_src, do_src

    def body(q_ref, kp_ref, kc_ref, kn_ref, vp_ref, vc_ref, vn_ref, do_ref, lg_ref, dl_ref,
             dq_ref, dk_ref, dv_ref, acc_k, acc_v):
        b = pl.program_id(0)
        i = lax.rem(jnp.minimum(b, n_blocks - 1), nb)

        @pl.when(b == 0)
        def _():
            acc_k[...] = jnp.zeros_like(acc_k)
            acc_v[...] = jnp.zeros_like(acc_v)

        @pl.when(b < n_blocks)
        def _():
            valid = _band_mask(i, tq, L)
            q, do = q_ref[...] * ATT_SCALE, do_ref[...]
            kwin = jnp.concatenate([kp_ref[...], kc_ref[...], kn_ref[...]], axis=0)
            vwin = jnp.concatenate([vp_ref[...], vc_ref[...], vn_ref[...]], axis=0)
            first, first_w = _first_head(tq), _first_head(win)
            groups = list(_lane_groups())
            cols = [c for ls in groups for c in (ls.start, ls.start + HEAD_DIM)]
            lanes = [ls for ls in groups for _ in range(2)]
            qh = [t for ls in groups for t in _split_pair(q[:, ls], first)]
            doh = [t for ls in groups for t in _split_pair(do[:, ls], first)]
            s = [jnp.where(valid, _nt(t, kwin[:, ls]), NEG_INF) for t, ls in zip(qh, lanes)]
            dp = [_nt(t, vwin[:, ls]) for t, ls in zip(doh, lanes)]
            p = [jnp.exp(t - lg_ref[:, c:c + 1]) for t, c in zip(s, cols)]
            ds = [(pp * (t - dl_ref[:, c:c + 1])).astype(BF16) for pp, t, c in zip(p, dp, cols)]
            dq = [jnp.dot(t, kwin[:, ls], preferred_element_type=F32) for t, ls in zip(ds, lanes)]
            dk = [_tn(t, q[:, ls]) for t, ls in zip(ds, lanes)]
            dv = [_tn(pp.astype(BF16), do[:, ls]) for pp, ls in zip(p, lanes)]
            for g, ls in enumerate(groups):
                dq_ref[:, ls] = (jnp.where(first, dq[2 * g], dq[2 * g + 1]) * ATT_SCALE).astype(dq_ref.dtype)
                acc_k[lead:, ls] += jnp.where(first_w, dk[2 * g], dk[2 * g + 1])
                acc_v[lead:, ls] += jnp.where(first_w, dv[2 * g], dv[2 * g + 1])

        for acc, out in ((acc_k, dk_ref), (acc_v, dv_ref)):
            out[...] = acc[0:tq, :].astype(out.dtype)
            kept = acc[tq:, :]
            acc[0:acc_rows - tq, :] = kept
            acc[acc_rows - tq:, :] = jnp.zeros((tq, D_ATT), F32)

    def seq(col):
        blk = lambda b: jnp.minimum(b, n_blocks - 1)
        cls = lambda b: (blk(b) // nb) * nh
        centre = pl.BlockSpec((tq, D_ATT), lambda b: (blk(b), col))
        prev = pl.BlockSpec((HALF, D_ATT), lambda b: (cls(b) + jnp.maximum((blk(b) % nb) * per - 1, 0), col))
        nxt = pl.BlockSpec((HALF, D_ATT), lambda b: (cls(b) + jnp.minimum((blk(b) % nb + 1) * per, nh - 1), col))
        return prev, centre, nxt

    _, qc, _ = seq(0)
    kp, kc, kn = seq(1)
    vp, vc, vn = seq(v_col)
    _, doc, _ = seq(do_col)
    late = pl.BlockSpec((tq, D_ATT), lambda b: (jnp.maximum(b - 1, 0), 0))
    return pl.pallas_call(
        body, name=name, grid=(n_blocks + 1,),
        in_specs=[qc, kp, kc, kn, vp, vc, vn, doc, qc, qc], out_specs=[qc, late, late],
        out_shape=[_sds((S, D_ATT), BF16)] * 3,
        scratch_shapes=[pltpu.VMEM((acc_rows, D_ATT), F32), pltpu.VMEM((acc_rows, D_ATT), F32)],
        compiler_params=_params("arbitrary"))(qk, qk, qk, qk, v_arr, v_arr, v_arr, do_arr, lg, delta)


def _sigmoid(x):
    return 1.0 / (1.0 + jnp.exp(-x))


def _halo_specs(S, T, width, col):
    last = S // HALO - 1
    per = T // HALO
    centre = pl.BlockSpec((T, width), lambda i: (i, col))
    prev = pl.BlockSpec((HALO, width), lambda i: (jnp.maximum(i * per - 1, 0), col))
    nxt = pl.BlockSpec((HALO, width), lambda i: (jnp.minimum((i + 1) * per, last), col))
    return prev, centre, nxt


def _window_scratch(T, C):
    return pltpu.VMEM((8, T + 2 * HALO, C), F32)


def _fill_window(buf, prev, centre, nxt, T):
    buf[0, 0:HALO, :] = prev
    buf[0, HALO:HALO + T, :] = centre
    buf[0, HALO + T:, :] = nxt
    rows = T + 2 * HALO - 8
    for s in range(1, 8):
        buf[s, 0:rows, :] = buf[0, s:s + rows, :]


def _tap_reads(buf, first_off, step, r0, ls):
    by_slab = {}
    for k in range(CONV_WIDTH):
        off = first_off + step * k
        by_slab.setdefault(off % 8, []).append((k, off - off % 8))
    for s, taps in by_slab.items():
        lo = min(a for _, a in taps)
        hi = max(a for _, a in taps)
        rows = buf[s, pl.ds(lo + r0, CONV_ROWS + hi - lo), ls]
        for k, a in taps:
            yield k, rows[a - lo:a - lo + CONV_ROWS]


def _depthwise(buf, w_ref, out_ref, T, C, first_off, step):
    def row_tile(t, carry):
        r0 = pl.multiple_of(t * CONV_ROWS, CONV_ROWS)
        for c0 in range(0, C, LANES):
            ls = slice(c0, c0 + LANES)
            acc = jnp.zeros((CONV_ROWS, LANES), F32)
            for k, rows in _tap_reads(buf, first_off, step, r0, ls):
                acc = acc + rows * w_ref[k:k + 1, ls]
            out_ref[pl.ds(r0, CONV_ROWS), ls] = acc
        return carry

    lax.fori_loop(0, T // CONV_ROWS, row_tile, 0)


def _conv_fwd(y, conv_w32, conv_b, ln_g, ln_b, *, T=512):
    S = y.shape[0]
    T = min(T, S)
    nblk = S // T
    C = D_CONV

    def body(ap, ac, an, gp, gc, gn, w_ref, b_ref, lg_ref, lb_ref, cv_ref, u1_ref, buf):
        i = pl.program_id(0)

        def glu(a_ref, g_ref):
            return a_ref[...].astype(F32) * _sigmoid(g_ref[...].astype(F32))

        _fill_window(buf, jnp.where(i > 0, glu(ap, gp), 0.0), glu(ac, gc),
                     jnp.where(i < nblk - 1, glu(an, gn), 0.0), T)
        _depthwise(buf, w_ref, u1_ref, T, C, HALO - CONV_PAD, 1)
        u1 = u1_ref[...] + b_ref[...]
        u1_ref[...] = u1
        mu = jnp.mean(u1, axis=-1, keepdims=True)
        xc = u1 - mu
        rstd = lax.rsqrt(jnp.mean(xc * xc, axis=-1, keepdims=True) + EPS)
        u2 = xc * rstd * lg_ref[...] + lb_ref[...]
        cv_ref[...] = (u2 * _sigmoid(u2)).astype(cv_ref.dtype)

    ap, ac, an = _halo_specs(S, T, C, 3)
    gp, gc, gn = _halo_specs(S, T, C, 4)
    vec = pl.BlockSpec((1, C), lambda i: (0, 0))
    out = pl.BlockSpec((T, C), lambda i: (i, 0))
    return pl.pallas_call(
        body, name="conv_fwd", grid=(nblk,),
        in_specs=[ap, ac, an, gp, gc, gn, pl.BlockSpec((32, C), lambda i: (0, 0)), vec, vec, vec],
        out_specs=[out, out], out_shape=[_sds((S, C), BF16), _sds((S, C), F32)],
        scratch_shapes=[_window_scratch(T, C)],
        compiler_params=_params("parallel"))(y, y, y, y, y, y, conv_w32, conv_b, ln_g, ln_b)


def _conv_bwd(dac, u1, y, conv_w32, ln_g, ln_b, *, T=512):
    S = y.shape[0]
    T = min(T, S)
    nblk = S // T
    C = D_CONV

    def body(dp, dc, dn, up, uc, un, ap, ac, an, gp, gc, gn, w_ref, lg_ref, lb_ref,
             dag_ref, dw_ref, dsm_ref, bufd, bufu, du0_scr, dw_acc):
        i = pl.program_id(0)
        lg = lg_ref[...]

        def du1_of(dcv_ref, u1_ref):
            u1 = u1_ref[...]
            mu = jnp.mean(u1, axis=-1, keepdims=True)
            xc = u1 - mu
            rstd = lax.rsqrt(jnp.mean(xc * xc, axis=-1, keepdims=True) + EPS)
            xhat = xc * rstd
            u2 = xhat * lg + lb_ref[...]
            sg = _sigmoid(u2)
            du2 = dcv_ref[...].astype(F32) * (sg * (1.0 + u2 * (1.0 - sg)))
            dxh = du2 * lg
            du1 = rstd * (dxh - jnp.mean(dxh, axis=-1, keepdims=True)
                          - xhat * jnp.mean(dxh * xhat, axis=-1, keepdims=True))
            return du1, du2, xhat

        def glu(a_ref, g_ref):
            return a_ref[...].astype(F32) * _sigmoid(g_ref[...].astype(F32))

        @pl.when(i == 0)
        def _():
            dw_ref[...] = jnp.zeros_like(dw_ref)
            dsm_ref[...] = jnp.zeros_like(dsm_ref)

        du1_c, du2_c, xhat_c = du1_of(dc, uc)
        dsm_ref[0:1, :] += jnp.sum(du1_c, axis=0, keepdims=True)
        dsm_ref[1:2, :] += jnp.sum(du2_c * xhat_c, axis=0, keepdims=True)
        dsm_ref[2:3, :] += jnp.sum(du2_c, axis=0, keepdims=True)
        _fill_window(bufd, jnp.where(i > 0, du1_of(dp, up)[0], 0.0), du1_c,
                     jnp.where(i < nblk - 1, du1_of(dn, un)[0], 0.0), T)
        _fill_window(bufu, jnp.where(i > 0, glu(ap, gp), 0.0), glu(ac, gc),
                     jnp.where(i < nblk - 1, glu(an, gn), 0.0), T)

        _depthwise(bufd, w_ref, du0_scr, T, C, HALO + CONV_PAD, -1)
        dw_acc[...] = jnp.zeros_like(dw_acc)

        def dw_tile(t, carry):
            r0 = pl.multiple_of(t * CONV_ROWS, CONV_ROWS)
            for c0 in range(0, C, LANES):
                ls = slice(c0, c0 + LANES)
                d = bufd[0, pl.ds(HALO + r0, CONV_ROWS), ls]
                for k, rows in _tap_reads(bufu, HALO - CONV_PAD, 1, r0, ls):
                    prod = d * rows
                    part = prod[0:8]
                    for j in range(8, CONV_ROWS, 8):
                        part = part + prod[j:j + 8]
                    dw_acc[k, :, ls] += part
            return carry

        lax.fori_loop(0, T // CONV_ROWS, dw_tile, 0)
        for k in range(CONV_WIDTH):
            dw_ref[k:k + 1, :] += jnp.sum(dw_acc[k], axis=0, keepdims=True)
        du0 = du0_scr[...]
        a = ac[...].astype(F32)
        sg = _sigmoid(gc[...].astype(F32))
        dag_ref[:, 0:C] = (du0 * sg).astype(dag_ref.dtype)
        dag_ref[:, C:] = (du0 * a * sg * (1.0 - sg)).astype(dag_ref.dtype)

    dp, dc, dn = _halo_specs(S, T, C, 1)
    up, uc, un = _halo_specs(S, T, C, 0)
    ap, ac, an = _halo_specs(S, T, C, 3)
    gp, gc, gn = _halo_specs(S, T, C, 4)
    vec = pl.BlockSpec((1, C), lambda i: (0, 0))
    return pl.pallas_call(
        body, name="conv_bwd", grid=(nblk,),
        in_specs=[dp, dc, dn, up, uc, un, ap, ac, an, gp, gc, gn,
                  pl.BlockSpec((32, C), lambda i: (0, 0)), vec, vec],
        out_specs=[pl.BlockSpec((T, 2 * C), lambda i: (i, 0)), pl.BlockSpec((32, C), lambda i: (0, 0)),
                   pl.BlockSpec((8, C), lambda i: (0, 0))],
        out_shape=[_sds((S, 2 * C), BF16), _sds((32, C), F32), _sds((8, C), F32)],
        scratch_shapes=[_window_scratch(T, C), _window_scratch(T, C), pltpu.VMEM((T, C), F32),
                        pltpu.VMEM((CONV_WIDTH, 8, C), F32)],
        compiler_params=_params("arbitrary"))(dac, dac, dac, u1, u1, u1, y, y, y, y, y, y, conv_w32, ln_g, ln_b)


def _xatt_fwd(xq, xk, xv, *, tm=512):
    S = xq.shape[0]
    M = xk.shape[0]
    tm = min(tm, S)
    scale = XATT_HEAD_DIM ** -0.5

    def body(q_ref, k_ref, v_ref, o_ref):
        heads = [slice(h * XATT_HEAD_DIM, (h + 1) * XATT_HEAD_DIM) for h in range(XATT_HEADS)]
        s = [_nt(q_ref[:, sl], k_ref[:, sl]) * scale for sl in heads]
        e = [jnp.exp(t - jnp.max(t, axis=-1, keepdims=True)) for t in s]
        p = [t * (1.0 / jnp.sum(t, axis=-1, keepdims=True)) for t in e]
        for sl, t in zip(heads, p):
            o_ref[:, sl] = jnp.dot(t.astype(BF16), v_ref[:, sl], preferred_element_type=F32).astype(o_ref.dtype)

    row = pl.BlockSpec((tm, D_MODEL), lambda i: (i, 0))
    full = pl.BlockSpec((M, D_MODEL), lambda i: (0, 0))
    return pl.pallas_call(
        body, name="xatt_fwd", grid=(S // tm,), in_specs=[row, full, full], out_specs=row,
        out_shape=_sds((S, D_MODEL), BF16), compiler_params=_params("parallel"))(xq, xk, xv)


def _xatt_bwd(xq, xk, xv, dxo, *, tm=512):
    S = xq.shape[0]
    M = xk.shape[0]
    tm = min(tm, S)
    scale = XATT_HEAD_DIM ** -0.5

    def body(q_ref, k_ref, v_ref, do_ref, dq_ref, dk_ref, dv_ref):
        i = pl.program_id(0)

        @pl.when(i == 0)
        def _():
            dk_ref[...] = jnp.zeros_like(dk_ref)
            dv_ref[...] = jnp.zeros_like(dv_ref)

        heads = [slice(h * XATT_HEAD_DIM, (h + 1) * XATT_HEAD_DIM) for h in range(XATT_HEADS)]
        s = [_nt(q_ref[:, sl], k_ref[:, sl]) * scale for sl in heads]
        dp = [_nt(do_ref[:, sl], v_ref[:, sl]) for sl in heads]
        e = [jnp.exp(t - jnp.max(t, axis=-1, keepdims=True)) for t in s]
        p = [t * (1.0 / jnp.sum(t, axis=-1, keepdims=True)) for t in e]
        ds = [(pp * (t - jnp.sum(t * pp, axis=-1, keepdims=True))).astype(BF16) for pp, t in zip(p, dp)]
        for sl, pp, t in zip(heads, p, ds):
            dq_ref[:, sl] = (jnp.dot(t, k_ref[:, sl], preferred_element_type=F32) * scale).astype(dq_ref.dtype)
            dv_ref[:, sl] += _tn(pp.astype(BF16), do_ref[:, sl])
            dk_ref[:, sl] += _tn(t, q_ref[:, sl]) * scale

    row = pl.BlockSpec((tm, D_MODEL), lambda i: (i, 0))
    full = pl.BlockSpec((M, D_MODEL), lambda i: (0, 0))
    return pl.pallas_call(
        body, name="xatt_bwd", grid=(S // tm,), in_specs=[row, full, full, row], out_specs=[row, full, full],
        out_shape=[_sds((S, D_MODEL), BF16), _sds((M, D_MODEL), F32), _sds((M, D_MODEL), F32)],
        compiler_params=_params("arbitrary"))(xq, xk, xv, dxo)


def _row_tile(R):
    for t in (256, 128, 64, 32, 16, 8):
        if R % t == 0:
            return t
    return R


def _sum_partials(own, recv, me, *, name):
    _, R, C = own.shape
    t = _row_tile(R)

    def body(me_ref, own_ref, r_ref, o_ref):
        o_ref[...] = ((own_ref[...].astype(F32) + r_ref[0].astype(F32)) + r_ref[1].astype(F32)) + r_ref[2].astype(F32)

    return pl.pallas_call(
        body, name=name,
        grid_spec=pltpu.PrefetchScalarGridSpec(
            num_scalar_prefetch=1, grid=(R // t,),
            in_specs=[pl.BlockSpec((None, t, C), lambda i, me_ref: (me_ref[0], i, 0)),
                      pl.BlockSpec((3, t, C), lambda i, me_ref: (0, i, 0))],
            out_specs=pl.BlockSpec((t, C), lambda i, me_ref: (i, 0))),
        out_shape=_sds((R, C), F32), compiler_params=_params("parallel"))(me, own, recv)


def _adamw_math(w, g, m, v):
    m2 = ADAM_B1 * m + (1.0 - ADAM_B1) * g
    v2 = ADAM_B2 * v + (1.0 - ADAM_B2) * (g * g)
    m_hat = m2 / (1.0 - ADAM_B1 ** ADAM_STEP)
    v_hat = v2 / (1.0 - ADAM_B2 ** ADAM_STEP)
    delta = -ADAM_LR * (m_hat / (jnp.sqrt(v_hat) + ADAM_EPS) + ADAM_WD * w)
    return delta, m2, v2


def _adamw(parts, w, m, v, *, name):
    R, C = w.shape
    t = _row_tile(R)
    n = len(parts)

    def body(*refs):
        w_ref, m_ref, v_ref = refs[n:n + 3]
        g_ref, d_ref, m2_ref, v2_ref = refs[n + 3:]
        g = refs[0][...]
        for r in refs[1:n]:
            g = g + r[...]
        delta, m2, v2 = _adamw_math(w_ref[...], g, m_ref[...], v_ref[...])
        g_ref[...] = g
        d_ref[...] = delta
        m2_ref[...] = m2
        v2_ref[...] = v2

    blk = pl.BlockSpec((t, C), lambda i: (i, 0))
    return pl.pallas_call(
        body, name=name, grid=(R // t,), in_specs=[blk] * (n + 3), out_specs=[blk] * 4,
        out_shape=[_sds((R, C), F32)] * 4, compiler_params=_params("parallel"))(*parts, w, m, v)


def _adamw_small(gathered, chip, entries):
    _, R, C = gathered.shape
    n = len(entries)
    group = D_CONV // N_CHIPS

    def body(chip_ref, g_ref, *refs):
        ins, outs, tot_ref = refs[:3 * n], refs[3 * n:7 * n], refs[7 * n]
        tot = g_ref[0]
        for k in range(1, N_DEV):
            tot = tot + g_ref[k]
        tot_ref[...] = tot
        for e, ((kind, r), _, _, _) in enumerate(entries):
            if kind == "row":
                g = tot_ref[r:r + 1, :]
            elif kind == "gain":
                g = jnp.concatenate([tot_ref[r:r + 1, :], tot_ref[r + 1:r + 2, :]], axis=1)
            else:
                g = tot_ref[r:r + CONV_WIDTH, 0:group]
                for j in range(1, N_CHIPS):
                    g = jnp.where(chip_ref[0] == j, tot_ref[r:r + CONV_WIDTH, j * group:(j + 1) * group], g)
            delta, m2, v2 = _adamw_math(ins[3 * e][...], g, ins[3 * e + 1][...], ins[3 * e + 2][...])
            for o, val in zip(outs[4 * e:4 * e + 4], (g, delta, m2, v2)):
                o[...] = val

    whole = lambda a: pl.BlockSpec(a.shape, lambda i, c: (0,) * a.ndim)
    arrays = [a for _, w, m, v in entries for a in (w, m, v)]
    out_like = [w for _, w, _, _ in entries for _ in range(4)]
    tot_like = _sds((R, C), F32)
    out = pl.pallas_call(
        body, name="adamw_small",
        grid_spec=pltpu.PrefetchScalarGridSpec(
            num_scalar_prefetch=1, grid=(1,),
            in_specs=[whole(gathered)] + [whole(a) for a in arrays],
            out_specs=[whole(a) for a in out_like] + [whole(tot_like)]),
        out_shape=[_sds(a.shape, F32) for a in out_like] + [tot_like],
        compiler_params=_params("arbitrary"))(chip, gathered, *arrays)
    return out[-1], [tuple(out[4 * e:4 * e + 4]) for e in range(n)]


def _chip_peers():
    x, y = lax.axis_index("x"), lax.axis_index("y")
    return [(1 - x, y), (x, 1 - y), (1 - x, 1 - y)]


HBM_SPEC = pl.BlockSpec(memory_space=pltpu.HBM)
SEM_SPEC = pl.BlockSpec(memory_space=pltpu.SEMAPHORE)


def _exchange_peers(mode):
    x, y, c = lax.axis_index("x"), lax.axis_index("y"), lax.axis_index("c")
    if mode == "swap":
        return [(x, y, 1 - c)]
    if mode == "all":
        flips = [(fx, fy, fc) for fx in (0, 1) for fy in (0, 1) for fc in (0, 1)][1:]
        return [(1 - x if fx else x, 1 - y if fy else y, 1 - c if fc else c) for fx, fy, fc in flips]
    return [(px, py, c) for px, py in _chip_peers()]


def _exchange_start(mode, srcs, zones, *, name):
    n = len(srcs)

    def body(*refs):
        ins, lands = refs[:n], refs[n:2 * n]
        send_sems, recv_sems = refs[2 * n:3 * n], refs[3 * n:4 * n]
        token = refs[-1]
        x, y, c = lax.axis_index("x"), lax.axis_index("y"), lax.axis_index("c")
        mine = 2 * x + y if mode == "gather" else 4 * x + 2 * y + c
        for t in range(n):
            for k, (px, py, pc) in enumerate(_exchange_peers(mode)):
                if mode in ("gather", "all"):
                    s, d = ins[t], lands[t].at[mine]
                elif mode == "scatter":
                    s, d = ins[t].at[2 * px + py], lands[t].at[k]
                else:
                    s, d = ins[t], lands[t]
                pltpu.make_async_remote_copy(src_ref=s, dst_ref=d, send_sem=send_sems[t], recv_sem=recv_sems[t],
                                             device_id=(px, py, pc), device_id_type=MESH).start()
            if mode in ("gather", "all"):
                pltpu.make_async_copy(ins[t], lands[t].at[mine], send_sems[t]).start()
        token[...] = jnp.zeros_like(token)

    hbm = lambda a: pltpu.with_memory_space_constraint(a, pltpu.HBM)
    out = pl.pallas_call(
        body, name=name,
        in_specs=[HBM_SPEC] * (2 * n),
        out_specs=[SEM_SPEC] * (2 * n) + [HBM_SPEC] * (2 * n) + [pl.BlockSpec(memory_space=pltpu.VMEM)],
        out_shape=[pltpu.SemaphoreType.DMA(())] * (2 * n)
        + [pltpu.HBM(a.shape, a.dtype) for a in list(srcs) + list(zones)] + [_sds((8, LANES), F32)],
        input_output_aliases={i: 2 * n + i for i in range(2 * n)},
        compiler_params=pltpu.CompilerParams(has_side_effects=pltpu.SideEffectType.DATAFLOW_SIDE_EFFECTING),
    )(*[hbm(a) for a in list(srcs) + list(zones)])
    return out[:n], out[n:2 * n], out[2 * n:3 * n], out[3 * n:4 * n], out[-1]


def _exchange_wait(mode, started, after, *, name):
    send_sems, recv_sems, srcs, zones, _ = started
    n = len(srcs)
    afters = tuple(after) if isinstance(after, (tuple, list)) else (after,)

    def body(*refs):
        lands = refs[n:2 * n]
        send_refs, recv_refs = refs[2 * n:3 * n], refs[3 * n:4 * n]
        me = (lax.axis_index("x"), lax.axis_index("y"), lax.axis_index("c"))
        n_remote = {"gather": N_CHIPS - 1, "scatter": N_CHIPS - 1, "all": N_DEV - 1, "swap": 1}[mode]
        for t in range(n):
            got = lands[t] if mode == "swap" else lands[t].at[pl.ds(0, n_remote)]
            sent = lands[t] if mode in ("gather", "all") else got
            pltpu.make_async_remote_copy(src_ref=sent, dst_ref=sent, send_sem=send_refs[t], recv_sem=recv_refs[t],
                                         device_id=me, device_id_type=MESH).wait_send()
            pltpu.make_async_remote_copy(src_ref=got, dst_ref=got, send_sem=send_refs[t], recv_sem=recv_refs[t],
                                         device_id=me, device_id_type=MESH).wait_recv()

    out = pl.pallas_call(
        body, name=name,
        in_specs=[HBM_SPEC] * (2 * n) + [SEM_SPEC] * (2 * n) + [pl.BlockSpec(memory_space=pl.ANY)] * len(afters),
        out_specs=[HBM_SPEC] * (2 * n),
        out_shape=[pltpu.HBM(a.shape, a.dtype) for a in list(srcs) + list(zones)],
        input_output_aliases={i: i for i in range(2 * n)},
        compiler_params=pltpu.CompilerParams(has_side_effects=pltpu.SideEffectType.DATAFLOW_SIDE_EFFECTING),
    )(*srcs, *zones, *send_sems, *recv_sems, *afters)
    return out[:n], out[n:]


def _swap_with_sibling(parts):
    n = len(parts)

    def body(*refs):
        ins, outs = refs[:n], refs[n:2 * n]
        send_sems, recv_sems = refs[2 * n:]
        sib = (lax.axis_index("x"), lax.axis_index("y"), 1 - lax.axis_index("c"))
        cps = []
        for t in range(n):
            cp = pltpu.make_async_remote_copy(
                src_ref=ins[t], dst_ref=outs[t], send_sem=send_sems.at[t], recv_sem=recv_sems.at[t],
                device_id=sib, device_id_type=MESH)
            cp.start()
            cps.append(cp)
        for cp in cps:
            cp.wait()

    any_spec = pl.BlockSpec(memory_space=pl.ANY)
    return pl.pallas_call(
        body, name="swap_with_sibling", in_specs=[any_spec] * n, out_specs=[any_spec] * n,
        out_shape=[_sds(p.shape, p.dtype) for p in parts],
        scratch_shapes=[pltpu.SemaphoreType.DMA((n,)), pltpu.SemaphoreType.DMA((n,))])(*parts)


BIG = ("w_in", "w_out", "w_xq", "w_xk", "w_xv", "w_xo", "w_up", "w_down")
COL_SHARDED = ("w_in", "w_up")


def _as_matrix(name, w4):
    if name in COL_SHARDED:
        return w4
    return w4.reshape(1, w4.shape[0] * w4.shape[1], w4.shape[2])


def _transposed(w3):
    nsh, K, n = w3.shape
    return jnp.swapaxes(w3, 1, 2).reshape(1, nsh * n, K)


def _shard_layout(name, g):
    if name in COL_SHARDED:
        return g
    return g.reshape(N_CHIPS, g.shape[0] * g.shape[1] // N_CHIPS, g.shape[2])


def _local_step(x, mem, target, vecs, comm):
    S = x.shape[0]
    tables = _rope_tables(S)

    xn = _rms_fwd(x, vecs["norm_mix_g"], name="rms_mix")
    w_in, conv_w32 = comm["first"]((xn,) + tuple(tables))
    w_in_whole = jnp.swapaxes(w_in, 0, 1).reshape(1, D_MODEL, D_IN)
    y = _mm_nn(xn, w_in_whole, name="mm_in", tm=2048, tn=D_IN // 2)
    qk, v_perm = _rope_fwd(y, tables)
    v_src = [(y, 2)] + [(v, 0) for v in v_perm[1:]]
    outs, lses = zip(*[_att_fwd(qk[p], v_src[p], d, name=f"att_fwd_d{d}") for p, d in enumerate(DILATIONS)])
    att, lg = _att_combine(outs, lses)
    cv, u1 = _conv_fwd(y, conv_w32, vecs["conv_b"], vecs["conv_ln_g"], vecs["conv_ln_b"])
    Wm = {k: _as_matrix(k, v) for k, v in comm["rest"]((att, cv)).items()}
    Wm["w_in"] = w_in
    h1, hn = _mm_rows((att, cv), Wm["w_out"], _residual_norm_tail, name="mm_out_rms", rows_in=(x,),
                      vecs_in=(vecs["norm_x_g"],), rows_out=(F32, BF16))
    xq = _mm_nn(hn, Wm["w_xq"], name="mm_xq")
    mn = _rms_fwd(mem, vecs["norm_mem_g"], name="rms_mem")
    xk = _mm_nn(mn, Wm["w_xk"], name="mm_xk")
    xv = _mm_nn(mn, Wm["w_xv"], name="mm_xv")
    xo = _xatt_fwd(xq, xk, xv)
    h2, hm = _mm_rows(xo, Wm["w_xo"], _residual_norm_tail, name="mm_xo_rms", rows_in=(h1,),
                      vecs_in=(vecs["norm_mlp_g"],), rows_out=(F32, BF16))
    relu_up = _mm_nn(hm, Wm["w_up"], name="mm_up", relu=True, tm=2048)
    sums = ((8, D_MODEL),)
    dh3, dh3b, dg_final, loss = _mm_rows(
        relu_up, Wm["w_down"], _loss_tail, name="mm_down_loss", rows_in=(h2, target), vecs_in=(vecs["norm_final_g"],),
        rows_out=(F32, BF16), sums_out=sums + ((8, LANES),), a_squared=True, tm=256)
    g = {}
    g["w_down"] = _mm_tn(relu_up, dh3b, 1, name="dw_down", a_squared=True)
    dup = _mm_nt(dh3b, Wm["w_down"], name="d_act", out_dtype=BF16, mul=relu_up, tm=2048)
    g["w_up"] = _mm_tn(hm, dup, N_CHIPS, name="dw_up")
    sent = comm["send_mlp"]({k: _shard_layout(k, g[k]) for k in ("w_down", "w_up")})
    dh2, dh2b, dg_mlp = _mm_rows(
        dup, _transposed(Wm["w_up"]), _rms_bwd_tail(True), name="d_hm_rms", rows_in=(h2, dh3),
        vecs_in=(vecs["norm_mlp_g"] + sent[0:1, 0:1],), rows_out=(F32, BF16), sums_out=sums, tm=256)
    g["w_xo"] = _mm_tn(xo, dh2b, 1, name="dw_xo")
    dxo = _mm_nt(dh2b, Wm["w_xo"], name="d_xo", out_dtype=BF16)
    dxq, dxk, dxv = _xatt_bwd(xq, xk, xv, dxo)
    g["w_xq"] = _mm_tn(hn, dxq, 1, name="dw_xq")
    dh1, dh1b, dg_x = _mm_rows(
        dxq, _transposed(Wm["w_xq"]), _rms_bwd_tail(True), name="d_hn_rms", rows_in=(h1, dh2),
        vecs_in=(vecs["norm_x_g"],), rows_out=(F32, BF16), sums_out=sums)
    dxkb, dxvb = dxk.astype(BF16), dxv.astype(BF16)
    g["w_xk"] = _mm_tn(mn, dxkb, 1, name="dw_xk")
    g["w_xv"] = _mm_tn(mn, dxvb, 1, name="dw_xv")
    dmn = _mm_nt(jnp.concatenate([dxkb, dxvb], axis=1),
                 jnp.concatenate([Wm["w_xk"], Wm["w_xv"]], axis=2), name="d_mn", out_dtype=BF16)
    _, _, dg_mem = _rms_bwd(dmn, mem, vecs["norm_mem_g"], None, name="rms_bwd_mem", bf16_copy=False)
    g["w_out"] = jnp.concatenate([_mm_tn(att, dh1b, 1, name="dw_out_att"), _mm_tn(cv, dh1b, 1, name="dw_out_conv")],
                                 axis=1)
    sent = comm["send_att"]({k: _shard_layout(k, g[k]) for k in ("w_out", "w_xq", "w_xk", "w_xv", "w_xo")})
    dac = _mm_nt(dh1b, Wm["w_out"], name="d_mix", out_dtype=BF16)
    dag, dconv_w, dconv_small = _conv_bwd(dac, u1, y, conv_w32, vecs["conv_ln_g"] + sent[0:1, 0:1],
                                          vecs["conv_ln_b"])
    delta, do_perm = _att_delta(dac, att)
    do_src = [(dac, 0)] + [(t, 0) for t in do_perm[1:]]
    dq, dk, dv = zip(*[_att_bwd(qk[p], v_src[p], do_src[p], lg[p], delta[p], d, name=f"att_bwd_d{d}")
                       for p, d in enumerate(DILATIONS)])
    dy = _assemble_dy(dq, dk, dv, dag, tables)
    g_in = _mm_tn(xn, dy, 1, name="dw_in", tn=D_IN // 2)
    g_in = jnp.swapaxes(g_in.reshape(D_MODEL, N_CHIPS, D_IN // N_CHIPS), 0, 1)
    sent = comm["send_in"]({"w_in": g_in})
    grad_x, dg_mix = _mm_rows(
        dy, _transposed(Wm["w_in"]), _rms_bwd_tail(False), name="d_xn_rms", rows_in=(x, dh1),
        vecs_in=(vecs["norm_mix_g"] + sent[0:1, 0:1],), rows_out=(F32,), sums_out=sums)

    small = dict(conv_w=dconv_w, conv_small=dconv_small, norm_mix_g=dg_mix, norm_x_g=dg_x, norm_mem_g=dg_mem,
                 norm_mlp_g=dg_mlp, norm_final_g=dg_final, loss=loss)
    return grad_x, small


SMALL_ORDER = ("conv_w", "conv_small", "norm_mix_g", "norm_x_g", "norm_mem_g", "norm_mlp_g", "norm_final_g", "loss")


def _pack_small(small):
    rows, offs, pos = [], {}, 0
    for k in SMALL_ORDER:
        a = small[k]
        a = a.reshape(a.shape[0] * a.shape[1] // SMALL_W, SMALL_W)
        pad = (-a.shape[0]) % 8
        if pad:
            a = jnp.pad(a, ((0, pad), (0, 0)))
        rows.append(a)
        offs[k] = pos
        pos += a.shape[0]
    return jnp.concatenate(rows, axis=0), offs


def kernel(x, mem, norm_mix_g, w_in, conv_w, conv_b, conv_ln_g, conv_ln_b, w_out, norm_x_g, norm_mem_g, w_xq, w_xk, w_xv, w_xo, norm_mlp_g, w_up, w_down, norm_final_g, loss_target, m_norm_mix_g, m_w_in, m_conv_w, m_conv_b, m_conv_ln_g, m_conv_ln_b, m_w_out, m_norm_x_g, m_norm_mem_g, m_w_xq, m_w_xk, m_w_xv, m_w_xo, m_norm_mlp_g, m_w_up, m_w_down, m_norm_final_g, v_norm_mix_g, v_w_in, v_conv_w, v_conv_b, v_conv_ln_g, v_conv_ln_b, v_w_out, v_norm_x_g, v_norm_mem_g, v_w_xq, v_w_xk, v_w_xv, v_w_xo, v_norm_mlp_g, v_w_up, v_w_down, v_norm_final_g):
    names = ("norm_mix_g", "w_in", "conv_w", "conv_b", "conv_ln_g", "conv_ln_b", "w_out", "norm_x_g", "norm_mem_g",
             "w_xq", "w_xk", "w_xv", "w_xo", "norm_mlp_g", "w_up", "w_down", "norm_final_g")
    wts = dict(zip(names, (norm_mix_g, w_in, conv_w, conv_b, conv_ln_g, conv_ln_b, w_out, norm_x_g, norm_mem_g,
                           w_xq, w_xk, w_xv, w_xo, norm_mlp_g, w_up, w_down, norm_final_g)))
    mom = dict(zip(names, (m_norm_mix_g, m_w_in, m_conv_w, m_conv_b, m_conv_ln_g, m_conv_ln_b, m_w_out, m_norm_x_g,
                           m_norm_mem_g, m_w_xq, m_w_xk, m_w_xv, m_w_xo, m_norm_mlp_g, m_w_up, m_w_down, m_norm_final_g)))
    var = dict(zip(names, (v_norm_mix_g, v_w_in, v_conv_w, v_conv_b, v_conv_ln_g, v_conv_ln_b, v_w_out, v_norm_x_g,
                           v_norm_mem_g, v_w_xq, v_w_xk, v_w_xv, v_w_xo, v_norm_mlp_g, v_w_up, v_w_down, v_norm_final_g)))
    chip = 2 * lax.axis_index("x") + lax.axis_index("y")

    def zone(shard):
        return lax.empty((N_CHIPS,) + shard.shape, shard.dtype)

    conv_w_pad = jnp.pad(wts["conv_w"][0], ((0, 1), (0, 0)))
    first_shards = [wts["w_in"][0].astype(BF16), conv_w_pad]
    gathering_first = _exchange_start("gather", first_shards, [zone(s) for s in first_shards],
                                      name="gather_first_start")
    rest = tuple(k for k in BIG if k != "w_in")
    behind_first = gathering_first[4][0, 0]
    rest_shards = [(wts[k][0] + behind_first).astype(BF16) for k in rest]
    gathering = gathering_rest = _exchange_start("gather", rest_shards, [zone(s) for s in rest_shards],
                                                 name="gather_rest_start")
    sending = {}

    def wait_first(after):
        _, (w_in_all, conv_w_all) = _exchange_wait("gather", gathering_first, after, name="gather_first_wait")
        return w_in_all, jnp.transpose(conv_w_all, (1, 0, 2)).reshape(32, D_CONV)

    def wait_rest(after):
        _, zones = _exchange_wait("gather", gathering_rest, after, name="gather_rest_wait")
        return dict(zip(rest, zones))

    def send(group, grads):
        keys = tuple(grads)
        zones = [lax.empty((N_CHIPS - 1,) + grads[k].shape[1:], grads[k].dtype) for k in keys]
        sending[group] = (keys, _exchange_start("scatter", [grads[k] for k in keys], zones,
                                                name=f"scatter_{group}_start"))
        return sending[group][1][4]

    comm = dict(first=wait_first, rest=wait_rest, send_mlp=lambda grads: send("mlp", grads),
                send_att=lambda grads: send("att", grads), send_in=lambda grads: send("in", grads))
    vecs = {k: wts[k] for k in ("conv_b", "conv_ln_g", "conv_ln_b", "norm_x_g", "norm_mem_g", "norm_mlp_g")}
    vecs["norm_mix_g"] = wts["norm_mix_g"] + gathering[4][0:1, 0:1]
    vecs["norm_final_g"] = wts["norm_final_g"].reshape(1, D_MODEL)
    grad_x, small = _local_step(x[0], mem[0], loss_target[0], vecs, comm)

    packed, offs = _pack_small(small)
    me_arr = jnp.reshape(chip, (1,)).astype(jnp.int32)
    gathering_small = _exchange_start("all", [packed], [lax.empty((N_DEV,) + packed.shape, packed.dtype)],
                                      name="allgather_small_start")
    sums = {}

    def settle(group, after):
        keys, started = sending[group]
        srcs, zones = _exchange_wait("scatter", started, after, name=f"scatter_{group}_wait")
        for k, own, got in zip(keys, srcs, zones):
            sums[k] = _sum_partials(own, got, me_arr, name=f"sum_{k}")

    settle("mlp", gathering_small[4])
    settle("att", gathering_small[4])
    early = tuple(sums)
    swapping = _exchange_start("swap", [sums[k] for k in early], [lax.empty(sums[k].shape, F32) for k in early],
                               name="swap_early_start")
    settle("in", swapping[4])
    _, (gath,) = _exchange_wait("all", gathering_small, sums["w_in"], name="allgather_small_wait")

    where = {"conv_w": ("conv_w", offs["conv_w"]), "conv_b": ("row", offs["conv_small"]),
             "conv_ln_g": ("row", offs["conv_small"] + 1), "conv_ln_b": ("row", offs["conv_small"] + 2)}
    where.update({k: ("gain", offs[k]) for k in ("norm_mix_g", "norm_x_g", "norm_mem_g", "norm_mlp_g", "norm_final_g")})
    as_2d = lambda a: a.reshape(a.shape[-2] if a.ndim > 1 else 1, a.shape[-1])
    tot_small, updates = _adamw_small(gath, me_arr, [(where[k], as_2d(wts[k]), as_2d(mom[k]), as_2d(var[k]))
                                                     for k in where])
    res = dict(zip(where, updates))
    loss = tot_small[offs["loss"], 0]

    sib = {"w_in": _swap_with_sibling([sums["w_in"]])[0]}
    mine_early, sib_early = _exchange_wait("swap", swapping, sib["w_in"], name="swap_early_wait")
    sums.update(zip(early, mine_early))
    sib.update(zip(early, sib_early))
    for k in BIG:
        res[k] = _adamw([sums[k], sib[k]], wts[k][0], mom[k][0], var[k][0], name=f"adamw_{k}")

    outs = [loss, grad_x[None]]
    for j in range(4):
        outs += [res[k][j].reshape(wts[k].shape) for k in names]
    return tuple(outs)
```

```python
import jax
import jax.numpy as jnp
from jax import lax
from jax.experimental import pallas as pl
from jax.experimental.pallas import tpu as pltpu

F32 = jnp.float32
BF16 = jnp.bfloat16
MESH = pl.DeviceIdType.MESH

D_MODEL = 1024
ATT_HEADS = 8
HEAD_DIM = 64
D_ATT = ATT_HEADS * HEAD_DIM
D_CONV = D_MODEL - D_ATT
DILATIONS = (1, 4, 16)
HALF = 64
ROPE_THETA = 500000.0
ROT_DIM = HEAD_DIM // 4
CONV_WIDTH = 31
CONV_PAD = (CONV_WIDTH - 1) // 2
XATT_HEADS = 4
XATT_HEAD_DIM = D_MODEL // XATT_HEADS
D_FF = 4 * D_MODEL
D_IN = 3 * D_ATT + 2 * D_CONV
EPS = 1e-6
NEG_INF = -1e30
N_CHIPS = 4
N_DEV = 8

ADAM_LR = 0.001
ADAM_B1 = 0.9
ADAM_B2 = 0.999
ADAM_EPS = 1e-08
ADAM_WD = 0.01
ADAM_STEP = 10

VMEM_LIMIT_V7X = 56 * 1024 * 1024
LANES = 128
HALO = 16
CONV_ROWS = 64
ATT_BLOCK = 128
SMALL_W = 512


def _params(*sem):
    return pltpu.CompilerParams(dimension_semantics=sem, vmem_limit_bytes=VMEM_LIMIT_V7X)


def _sds(shape, dtype):
    return jax.ShapeDtypeStruct(shape, dtype)


def _squared(a):
    af = a.astype(F32)
    return (af * af).astype(BF16)


def _mm_nn(a, w3, *, name, out_dtype=BF16, res=None, relu=False, a_squared=False, tm=1024, tn=None, tk=1024):
    M, K = a.shape
    nsh, _, n = w3.shape
    tm, tk = min(tm, M), min(tk, K)
    tn = tn or min(n, 1024)
    npt, nk = n // tn, K // tk
    nj, N = nsh * npt, nsh * n
    n_out = 1

    def body(*refs):
        a_ref, w_ref = refs[0], refs[1]
        pos = 2
        res_ref = None
        if res is not None:
            res_ref = refs[pos]
            pos += 1
        outs = refs[pos:pos + n_out]
        acc_ref = refs[pos + n_out] if nk > 1 else None

        def finish(acc):
            if res_ref is not None:
                acc = acc + res_ref[...]
            if relu:
                acc = jnp.maximum(acc, 0.0)
            outs[0][...] = acc.astype(outs[0].dtype)

        a_val = _squared(a_ref[...]) if a_squared else a_ref[...]
        part = jnp.dot(a_val, w_ref[...], preferred_element_type=F32)
        if nk == 1:
            finish(part)
        else:
            k = pl.program_id(2)

            @pl.when(k == 0)
            def _():
                acc_ref[...] = part

            @pl.when(k > 0)
            def _():
                acc_ref[...] += part

            @pl.when(k == nk - 1)
            def _():
                finish(acc_ref[...])

    in_specs = [pl.BlockSpec((tm, tk), lambda i, j, k: (i, k)),
                pl.BlockSpec((None, tk, tn), lambda i, j, k: (j // npt, k, j % npt))]
    args = [a, w3]
    if res is not None:
        in_specs.append(pl.BlockSpec((tm, tn), lambda i, j, k: (i, j)))
        args.append(res)
    out_spec = pl.BlockSpec((tm, tn), lambda i, j, k: (i, j))
    out = pl.pallas_call(
        body, name=name, grid=(M // tm, nj, nk), in_specs=in_specs,
        out_specs=[out_spec] * n_out, out_shape=[_sds((M, N), out_dtype)] * n_out,
        scratch_shapes=[pltpu.VMEM((tm, tn), F32)] if nk > 1 else [],
        compiler_params=_params("parallel", "parallel", "arbitrary"))(*args)
    return out[0]


def _mm_nt(dy, w3, *, name, out_dtype=F32, mul=None, tm=1024, tn=None, tko=1024):
    M, N = dy.shape
    nsh, K, n = w3.shape
    tm, tko = min(tm, M), min(tko, K)
    tn = tn or min(n, 1024)
    npt = n // tn
    nj = nsh * npt

    def body(*refs):
        dy_ref, w_ref = refs[0], refs[1]
        pos = 2
        mul_ref = None
        if mul is not None:
            mul_ref = refs[pos]
            pos += 1
        out_ref = refs[pos]
        acc_ref = refs[pos + 1] if nj > 1 else None

        def finish(acc):
            if mul_ref is not None:
                acc = acc * (2.0 * mul_ref[...].astype(F32))
            out_ref[...] = acc.astype(out_ref.dtype)

        part = lax.dot_general(dy_ref[...], w_ref[...], (((1,), (1,)), ((), ())), preferred_element_type=F32)
        if nj == 1:
            finish(part)
        else:
            j = pl.program_id(2)

            @pl.when(j == 0)
            def _():
                acc_ref[...] = part

            @pl.when(j > 0)
            def _():
                acc_ref[...] += part

            @pl.when(j == nj - 1)
            def _():
                finish(acc_ref[...])

    in_specs = [pl.BlockSpec((tm, tn), lambda i, ko, j: (i, j)),
                pl.BlockSpec((None, tko, tn), lambda i, ko, j: (j // npt, ko, j % npt))]
    args = [dy, w3]
    if mul is not None:
        in_specs.append(pl.BlockSpec((tm, tko), lambda i, ko, j: (i, ko)))
        args.append(mul)
    return pl.pallas_call(
        body, name=name, grid=(M // tm, K // tko, nj), in_specs=in_specs,
        out_specs=pl.BlockSpec((tm, tko), lambda i, ko, j: (i, ko)), out_shape=_sds((M, K), out_dtype),
        scratch_shapes=[pltpu.VMEM((tm, tko), F32)] if nj > 1 else [],
        compiler_params=_params("parallel", "parallel", "arbitrary"))(*args)


def _mm_tn(a, dy, nsh, *, name, out_dtype=BF16, a_squared=False, tm=2048, tk=1024, tn=None):
    M, K = a.shape
    N = dy.shape[1]
    n = N // nsh
    tm, tk = min(tm, M), min(tk, K)
    tn = tn or min(n, 1024)
    npt = n // tn
    nj, nm = nsh * npt, M // tm

    def body(a_ref, dy_ref, out_ref, acc_ref):
        m = pl.program_id(2)
        a_val = _squared(a_ref[...]) if a_squared else a_ref[...]
        part = lax.dot_general(a_val, dy_ref[...], (((0,), (0,)), ((), ())), preferred_element_type=F32)

        @pl.when(m == 0)
        def _():
            acc_ref[...] = part

        @pl.when(m > 0)
        def _():
            acc_ref[...] += part

        @pl.when(m == nm - 1)
        def _():
            out_ref[...] = acc_ref[...].astype(out_ref.dtype)

    return pl.pallas_call(
        body, name=name, grid=(K // tk, nj, nm),
        in_specs=[pl.BlockSpec((tm, tk), lambda kk, j, m: (m, kk)),
                  pl.BlockSpec((tm, tn), lambda kk, j, m: (m, j))],
        out_specs=pl.BlockSpec((None, tk, tn), lambda kk, j, m: (j // npt, kk, j % npt)),
        out_shape=_sds((nsh, K, n), out_dtype),
        scratch_shapes=[pltpu.VMEM((tk, tn), F32)],
        compiler_params=_params("parallel", "parallel", "arbitrary"))(a, dy)


def _rms_fwd(x, g, *, name, tm=512):
    M, Dm = x.shape
    tm = min(tm, M)

    def body(x_ref, g_ref, o_ref):
        xf = x_ref[...]
        r = lax.rsqrt(jnp.mean(xf * xf, axis=-1, keepdims=True) + EPS)
        o_ref[...] = (xf * r * g_ref[...]).astype(o_ref.dtype)

    return pl.pallas_call(
        body, name=name, grid=(M // tm,),
        in_specs=[pl.BlockSpec((tm, Dm), lambda i: (i, 0)), pl.BlockSpec((1, Dm), lambda i: (0, 0))],
        out_specs=pl.BlockSpec((tm, Dm), lambda i: (i, 0)), out_shape=_sds((M, Dm), BF16),
        compiler_params=_params("parallel"))(x, g)


def _rms_bwd(dxn, x, g, dres, *, name, bf16_copy=True, tm=512):
    M, Dm = x.shape
    tm = min(tm, M)
    has_res = dres is not None

    def body(*refs):
        dxn_ref, x_ref, g_ref = refs[:3]
        dres_ref = refs[3] if has_res else None
        dx_ref, dg_ref = refs[-1 - 1 - bf16_copy], refs[-1]
        dxb_ref = refs[-2] if bf16_copy else None
        i = pl.program_id(0)
        xf = x_ref[...]
        r = lax.rsqrt(jnp.mean(xf * xf, axis=-1, keepdims=True) + EPS)
        nrm = xf * r
        dxn_f = dxn_ref[...].astype(F32)
        dn = dxn_f * g_ref[...]
        dx = r * (dn - nrm * jnp.mean(dn * nrm, axis=-1, keepdims=True))
        if has_res:
            dx = dx + dres_ref[...]
        dx_ref[...] = dx
        if bf16_copy:
            dxb_ref[...] = dx.astype(dxb_ref.dtype)

        @pl.when(i == 0)
        def _():
            dg_ref[...] = jnp.zeros_like(dg_ref)

        dg_ref[0:1, :] += jnp.sum(dxn_f * nrm, axis=0, keepdims=True)

    row = pl.BlockSpec((tm, Dm), lambda i: (i, 0))
    in_specs = [row, row, pl.BlockSpec((1, Dm), lambda i: (0, 0))] + ([row] if has_res else [])
    args = [dxn, x, g] + ([dres] if has_res else [])
    out = pl.pallas_call(
        body, name=name, grid=(M // tm,), in_specs=in_specs,
        out_specs=[row] * (1 + bf16_copy) + [pl.BlockSpec((8, Dm), lambda i: (0, 0))],
        out_shape=[_sds((M, Dm), F32)] + [_sds((M, Dm), BF16)] * bf16_copy + [_sds((8, Dm), F32)],
        compiler_params=_params("arbitrary"))(*args)
    return out[0], (out[1] if bf16_copy else None), out[-1]


def _mm_rows(a, w3, tail, *, name, rows_in=(), vecs_in=(), rows_out=(), sums_out=(), a_squared=False,
             w_transposed=False, tm=512):
    parts = a if isinstance(a, (tuple, list)) else (a,)
    M = parts[0].shape[0]
    K, N = (w3.shape[0] * w3.shape[2], w3.shape[1]) if w_transposed else (w3.shape[1], w3.shape[2])
    tm = min(tm, M)
    n_a, n_ri, n_vi, n_ro = len(parts), len(rows_in), len(vecs_in), len(rows_out)

    def body(*refs):
        a_refs, w_ref, refs = refs[:n_a], refs[n_a], refs[n_a + 1:]
        rin, vin = refs[:n_ri], refs[n_ri:n_ri + n_vi]
        rout, sout = refs[n_ri + n_vi:n_ri + n_vi + n_ro], refs[n_ri + n_vi + n_ro:]

        @pl.when(pl.program_id(0) == 0)
        def _():
            for s in sout:
                s[...] = jnp.zeros_like(s)

        a_val = a_refs[0][...] if n_a == 1 else jnp.concatenate([r[...] for r in a_refs], axis=1)
        if a_squared:
            a_val = _squared(a_val)
        if w_transposed:
            n = w3.shape[2]
            prod = _nt(a_val[:, 0:n], w_ref[0])
            for j in range(1, w3.shape[0]):
                prod = prod + _nt(a_val[:, j * n:(j + 1) * n], w_ref[j])
        else:
            prod = jnp.dot(a_val, w_ref[0], preferred_element_type=F32)
        tail(prod, rin, vin, rout, sout)

    row = pl.BlockSpec((tm, N), lambda i: (i, 0))
    once = lambda shape: pl.BlockSpec(shape, lambda i: (0,) * len(shape))
    return pl.pallas_call(
        body, name=name, grid=(M // tm,),
        in_specs=[pl.BlockSpec((tm, p.shape[1]), lambda i: (i, 0)) for p in parts] + [once(w3.shape)]
        + [row] * n_ri + [once((1, N))] * n_vi,
        out_specs=[row] * n_ro + [once(s) for s in sums_out],
        out_shape=[_sds((M, N), dt) for dt in rows_out] + [_sds(s, F32) for s in sums_out],
        compiler_params=_params("arbitrary"))(*parts, w3, *rows_in, *vecs_in)


def _residual_norm_tail(prod, rows_in, vecs_in, rows_out, sums_out):
    hf = prod + rows_in[0][...]
    rows_out[0][...] = hf
    r = lax.rsqrt(jnp.mean(hf * hf, axis=-1, keepdims=True) + EPS)
    rows_out[1][...] = (hf * r * vecs_in[0][...]).astype(BF16)


def _rms_bwd_tail(bf16_copy):
    def tail(dxn, rows_in, vecs_in, rows_out, sums_out):
        xf = rows_in[0][...]
        r = lax.rsqrt(jnp.mean(xf * xf, axis=-1, keepdims=True) + EPS)
        nrm = xf * r
        dn = dxn * vecs_in[0][...]
        dx = r * (dn - nrm * jnp.mean(dn * nrm, axis=-1, keepdims=True)) + rows_in[1][...]
        rows_out[0][...] = dx
        if bf16_copy:
            rows_out[1][...] = dx.astype(BF16)
        sums_out[0][0:1, :] += jnp.sum(dxn * nrm, axis=0, keepdims=True)

    return tail


def _loss_tail(prod, rows_in, vecs_in, rows_out, sums_out):
    hf = prod + rows_in[0][...]
    r = lax.rsqrt(jnp.mean(hf * hf, axis=-1, keepdims=True) + EPS)
    nrm = hf * r
    gv = vecs_in[0][...]
    err = nrm * gv - rows_in[1][...]
    dy = err * (1.0 / hf.shape[-1])
    dn = dy * gv
    dh = r * (dn - nrm * jnp.mean(dn * nrm, axis=-1, keepdims=True))
    rows_out[0][...] = dh
    rows_out[1][...] = dh.astype(BF16)
    sums_out[0][0:1, :] += jnp.sum(dy * nrm, axis=0, keepdims=True)
    part = 0.5 * jnp.sum(jnp.mean(err * err, axis=-1, keepdims=True), axis=0, keepdims=True)
    sel = (lax.broadcasted_iota(jnp.int32, (8, 128), 0) == 0) & (lax.broadcasted_iota(jnp.int32, (8, 128), 1) == 0)
    sums_out[1][...] += jnp.where(sel, part, 0.0)


def _class_spec(tm, d, width):
    return pl.BlockSpec((d, tm // d, width), lambda i: (0, i, 0))


def _row_scratch(tm, width):
    return pltpu.VMEM((width // LANES, tm, LANES), F32)


def _fill(scr, val):
    for c in range(scr.shape[0]):
        scr[c] = val[:, c * LANES:(c + 1) * LANES]


def _to_classes(scr, out_ref, d):
    n = scr.shape[1] // d
    for r in range(d):
        for c in range(scr.shape[0]):
            out_ref[r, :, c * LANES:(c + 1) * LANES] = scr[c, pl.ds(r, n, stride=d), :].astype(out_ref.dtype)


def _from_classes(in_ref, scr, d):
    n = scr.shape[1] // d
    for r in range(d):
        blk = in_ref[r].astype(F32)
        for c in range(scr.shape[0]):
            scr[c, pl.ds(r, n, stride=d), :] = blk[:, c * LANES:(c + 1) * LANES]
    return jnp.concatenate([scr[c] for c in range(scr.shape[0])], axis=1)


def _rope_tables(S):
    half = ROT_DIM // 2
    freqs = ROPE_THETA ** (-jnp.arange(0, ROT_DIM, 2, dtype=F32) / ROT_DIM)
    ang = jnp.arange(S, dtype=F32)[:, None] * freqs[None, :]
    cos, sin = jnp.cos(ang), jnp.sin(ang)
    ones = jnp.ones((S, HEAD_DIM - ROT_DIM), F32)
    zeros = jnp.zeros((S, HEAD_DIM - ROT_DIM), F32)
    zh = jnp.zeros((S, half), F32)
    c = jnp.concatenate([cos, cos, ones], axis=1)
    sa = jnp.concatenate([-sin, zh, zeros], axis=1)
    sb = jnp.concatenate([zh, sin, zeros], axis=1)
    return tuple(jnp.tile(t, (1, LANES // HEAD_DIM)) for t in (c, sa, sb))


def _rope_fwd(y, tables, *, tm=512):
    S = y.shape[0]
    W = 2 * D_ATT
    tm = min(tm, S)
    half = ROT_DIM // 2
    dils = [d for d in DILATIONS if d > 1]

    def body(y_ref, c_ref, sa_ref, sb_ref, qk_ref, *rest):
        qk_outs, v_outs = rest[:len(dils)], rest[len(dils):2 * len(dils)]
        scr_qk, scr_v = rest[2 * len(dils):]
        t = y_ref[:, 0:W].astype(F32)
        rep = W // LANES
        c, sa, sb = (jnp.tile(r[...], (1, rep)) for r in (c_ref, sa_ref, sb_ref))
        rot = t * c + pltpu.roll(t, W - half, axis=1) * sa + pltpu.roll(t, half, axis=1) * sb
        qk_ref[...] = rot.astype(qk_ref.dtype)
        _fill(scr_qk, rot)
        _fill(scr_v, y_ref[:, W:W + D_ATT].astype(F32))
        for d, qo, vo in zip(dils, qk_outs, v_outs):
            _to_classes(scr_qk, qo, d)
            _to_classes(scr_v, vo, d)

    tab = pl.BlockSpec((tm, LANES), lambda i: (i, 0))
    out = pl.pallas_call(
        body, name="rope_fwd", grid=(S // tm,),
        in_specs=[pl.BlockSpec((tm, 3 * D_ATT), lambda i: (i, 0)), tab, tab, tab],
        out_specs=[pl.BlockSpec((tm, W), lambda i: (i, 0))] + [_class_spec(tm, d, W) for d in dils]
        + [_class_spec(tm, d, D_ATT) for d in dils],
        out_shape=[_sds((S, W), BF16)] + [_sds((d, S // d, W), BF16) for d in dils]
        + [_sds((d, S // d, D_ATT), BF16) for d in dils],
        scratch_shapes=[_row_scratch(tm, W), _row_scratch(tm, D_ATT)],
        compiler_params=_params("parallel"))(y, *tables)
    qk = [out[0]] + [o.reshape(S, W) for o in out[1:1 + len(dils)]]
    v = [None] + [o.reshape(S, D_ATT) for o in out[1 + len(dils):]]
    return qk, v


def _assemble_dy(dq, dk, dv, dag, tables, *, tm=512):
    S = dag.shape[0]
    tm = min(tm, S)
    half = ROT_DIM // 2
    W = D_ATT
    n_pat = len(DILATIONS)

    def body(*refs):
        groups = [refs[g * n_pat:(g + 1) * n_pat] for g in range(3)]
        dag_ref, c_ref, sa_ref, sb_ref, o_ref, scr = refs[3 * n_pat:]
        rep = W // LANES
        c, sa, sb = (jnp.tile(r[...], (1, rep)) for r in (c_ref, sa_ref, sb_ref))

        def total(rs):
            acc = rs[0][...].astype(F32)
            for d, r in zip(DILATIONS[1:], rs[1:]):
                acc = acc + _from_classes(r, scr, d)
            return acc

        def unrope(dr):
            return dr * c + pltpu.roll(dr * sa, half, axis=1) + pltpu.roll(dr * sb, W - half, axis=1)

        o_ref[:, 0:W] = unrope(total(groups[0])).astype(o_ref.dtype)
        o_ref[:, W:2 * W] = unrope(total(groups[1])).astype(o_ref.dtype)
        o_ref[:, 2 * W:3 * W] = total(groups[2]).astype(o_ref.dtype)
        o_ref[:, 3 * W:] = dag_ref[...]

    specs = [pl.BlockSpec((tm, W), lambda i: (i, 0))] + [_class_spec(tm, d, W) for d in DILATIONS[1:]]
    tab = pl.BlockSpec((tm, LANES), lambda i: (i, 0))
    args = [a if d == 1 else a.reshape(d, S // d, W) for grp in (dq, dk, dv) for d, a in zip(DILATIONS, grp)]
    return pl.pallas_call(
        body, name="assemble_dy", grid=(S // tm,),
        in_specs=specs * 3 + [pl.BlockSpec((tm, 2 * D_CONV), lambda i: (i, 0)), tab, tab, tab],
        out_specs=pl.BlockSpec((tm, D_IN), lambda i: (i, 0)), out_shape=_sds((S, D_IN), BF16),
        scratch_shapes=[_row_scratch(tm, W)],
        compiler_params=_params("parallel"))(*args, dag, *tables)


def _seq_specs(L, tb, col):
    nb, per, nh = L // tb, tb // HALF, L // HALF
    centre = pl.BlockSpec((tb, D_ATT), lambda r, i: (r * nb + i, col))
    prev = pl.BlockSpec((HALF, D_ATT), lambda r, i: (r * nh + jnp.maximum(i * per - 1, 0), col))
    nxt = pl.BlockSpec((HALF, D_ATT), lambda r, i: (r * nh + jnp.minimum((i + 1) * per, nh - 1), col))
    return prev, centre, nxt


def _band_mask(i, tq, L):
    shape = (tq, tq + 2 * HALF)
    c_idx = lax.broadcasted_iota(jnp.int32, shape, 0)
    w_idx = lax.broadcasted_iota(jnp.int32, shape, 1)
    diff = w_idx - c_idx
    wpos = i * tq - HALF + w_idx
    return (diff >= 0) & (diff <= 2 * HALF) & (wpos >= 0) & (wpos < L)


def _lane_groups():
    for c0 in range(0, D_ATT, LANES):
        yield slice(c0, c0 + LANES)


def _first_head(rows):
    return lax.broadcasted_iota(jnp.int32, (rows, LANES), 1) < HEAD_DIM


def _split_pair(x, first):
    zero = jnp.zeros_like(x)
    return jnp.where(first, x, zero), jnp.where(first, zero, x)


def _nt(a, b):
    return lax.dot_general(a, b, (((1,), (1,)), ((), ())), preferred_element_type=F32)


def _tn(a, b):
    return lax.dot_general(a, b, (((0,), (0,)), ((), ())), preferred_element_type=F32)


ATT_SCALE = HEAD_DIM ** -0.5


def _att_fwd(qk, v_src, d, *, name):
    S = qk.shape[0]
    L = S // d
    tq = min(ATT_BLOCK, L)
    v_arr, v_col = v_src

    def body(q_ref, kp_ref, kc_ref, kn_ref, vp_ref, vc_ref, vn_ref, o_ref, lse_ref):
        i = pl.program_id(1)
        valid = _band_mask(i, tq, L)
        q = q_ref[...] * ATT_SCALE
        kwin = jnp.concatenate([kp_ref[...], kc_ref[...], kn_ref[...]], axis=0)
        vwin = jnp.concatenate([vp_ref[...], vc_ref[...], vn_ref[...]], axis=0)
        first = _first_head(tq)
        groups = list(_lane_groups())
        heads = [(ls, t) for ls in groups for t in _split_pair(q[:, ls], first)]
        s = [jnp.where(valid, _nt(t, kwin[:, ls]), NEG_INF) for ls, t in heads]
        m = [jnp.max(t, axis=-1, keepdims=True) for t in s]
        p = [jnp.exp(t - mm) for t, mm in zip(s, m)]
        den = [jnp.sum(t, axis=-1, keepdims=True) for t in p]
        o = [jnp.dot(t.astype(BF16), vwin[:, ls], preferred_element_type=F32) * (1.0 / dd)
             for t, dd, (ls, _) in zip(p, den, heads)]
        lse = [mm + jnp.log(dd) for mm, dd in zip(m, den)]
        for g, ls in enumerate(groups):
            o_ref[:, ls] = jnp.where(first, o[2 * g], o[2 * g + 1]).astype(o_ref.dtype)
            lse_ref[:, ls] = jnp.where(first, lse[2 * g], lse[2 * g + 1])

    _, qc, _ = _seq_specs(L, tq, 0)
    kp, kc, kn = _seq_specs(L, tq, 1)
    vp, vc, vn = _seq_specs(L, tq, v_col)
    out = pl.BlockSpec((tq, D_ATT), lambda r, i: (r * (L // tq) + i, 0))
    return pl.pallas_call(
        body, name=name, grid=(d, L // tq),
        in_specs=[qc, kp, kc, kn, vp, vc, vn], out_specs=[out, out],
        out_shape=[_sds((S, D_ATT), BF16), _sds((S, D_ATT), F32)],
        compiler_params=_params("parallel", "parallel"))(qk, qk, qk, qk, v_arr, v_arr, v_arr)


def _att_combine(outs, lses, *, tm=512):
    S = outs[0].shape[0]
    tm = min(tm, S)
    dils = DILATIONS[1:]
    n_d = len(dils)

    def body(*refs):
        o_refs, l_refs = refs[0:1 + n_d], refs[1 + n_d:2 + 2 * n_d]
        att_ref, lg_ref = refs[2 + 2 * n_d:4 + 2 * n_d]
        lg_outs = refs[4 + 2 * n_d:4 + 3 * n_d]
        scr = refs[4 + 3 * n_d:]
        scr_o, scr_l, scr_lg = scr[:n_d], scr[n_d:2 * n_d], scr[2 * n_d]
        ls = [l_refs[0][...]] + [_from_classes(r, s, d) for r, s, d in zip(l_refs[1:], scr_l, dils)]
        os_ = [o_refs[0][...].astype(F32)] + [_from_classes(r, s, d) for r, s, d in zip(o_refs[1:], scr_o, dils)]
        mx = ls[0]
        for l in ls[1:]:
            mx = jnp.maximum(mx, l)
        es = [jnp.exp(l - mx) for l in ls]
        tot = es[0]
        num = es[0] * os_[0]
        for e, o in zip(es[1:], os_[1:]):
            tot = tot + e
            num = num + e * o
        att_ref[...] = (num / tot).astype(att_ref.dtype)
        lg = mx + jnp.log(tot)
        lg_ref[...] = lg
        _fill(scr_lg, lg)
        for d, out in zip(dils, lg_outs):
            _to_classes(scr_lg, out, d)

    nat = pl.BlockSpec((tm, D_ATT), lambda i: (i, 0))
    specs = [nat] + [_class_spec(tm, d, D_ATT) for d in dils]
    view = lambda arrs: [arrs[0]] + [a.reshape(d, S // d, D_ATT) for a, d in zip(arrs[1:], dils)]
    out = pl.pallas_call(
        body, name="att_combine", grid=(S // tm,), in_specs=specs * 2,
        out_specs=[nat, nat] + specs[1:],
        out_shape=[_sds((S, D_ATT), BF16), _sds((S, D_ATT), F32)] + [_sds((d, S // d, D_ATT), F32) for d in dils],
        scratch_shapes=[_row_scratch(tm, D_ATT)] * (2 * n_d + 1),
        compiler_params=_params("parallel"))(*view(list(outs)), *view(list(lses)))
    return out[0], [out[1]] + [o.reshape(S, D_ATT) for o in out[2:]]


def _att_delta(dac, att, *, tm=512):
    S = att.shape[0]
    tm = min(tm, S)
    dils = DILATIONS[1:]
    n_d = len(dils)

    def body(do_ref, o_ref, dl_ref, *rest):
        dl_outs, do_outs = rest[:n_d], rest[n_d:2 * n_d]
        scr_dl, scr_do = rest[2 * n_d:]
        do = do_ref[...].astype(F32)
        prod = do * o_ref[...].astype(F32)
        per_head = [jnp.broadcast_to(jnp.sum(prod[:, h * HEAD_DIM:(h + 1) * HEAD_DIM], axis=-1, keepdims=True),
                                     (tm, HEAD_DIM)) for h in range(ATT_HEADS)]
        dl = jnp.concatenate(per_head, axis=1)
        dl_ref[...] = dl
        _fill(scr_dl, dl)
        _fill(scr_do, do)
        for d, dlo, doo in zip(dils, dl_outs, do_outs):
            _to_classes(scr_dl, dlo, d)
            _to_classes(scr_do, doo, d)

    blk = pl.BlockSpec((tm, D_ATT), lambda i: (i, 0))
    out = pl.pallas_call(
        body, name="att_delta", grid=(S // tm,), in_specs=[blk, blk],
        out_specs=[blk] + [_class_spec(tm, d, D_ATT) for d in dils] * 2,
        out_shape=[_sds((S, D_ATT), F32)] + [_sds((d, S // d, D_ATT), F32) for d in dils]
        + [_sds((d, S // d, D_ATT), BF16) for d in dils],
        scratch_shapes=[_row_scratch(tm, D_ATT), _row_scratch(tm, D_ATT)],
        compiler_params=_params("parallel"))(dac, att)
    delta = [out[0]] + [o.reshape(S, D_ATT) for o in out[1:1 + n_d]]
    do = [None] + [o.reshape(S, D_ATT) for o in out[1 + n_d:]]
    return delta, do


def _att_bwd(qk, v_src, do_src, lg, delta, d, *, name):
    S = qk.shape[0]
    L = S // d
    tq = min(ATT_BLOCK, L)
    nb, per, nh = L // tq, tq // HALF, L // HALF
    n_blocks = d * nb
    win = tq + 2 * HALF
    lead = tq - HALF
    acc_rows = lead + win
    (v_arr, v_col), (do_arr, do_col) = v_src, do_src

    def body(q_ref, kp_ref, kc_ref, kn_ref, vp_ref, vc_ref, vn_ref, do_ref, lg_ref, dl_ref,
             dq_ref, dk_ref, dv_ref, acc_k, acc_v):
        b = pl.program_id(0)
        i = lax.rem(jnp.minimum(b, n_blocks - 1), nb)

        @pl.when(b == 0)
        def _():
            acc_k[...] = jnp.zeros_like(acc_k)
            acc_v[...] = jnp.zeros_like(acc_v)

        @pl.when(b < n_blocks)
        def _():
            valid = _band_mask(i, tq, L)
            q, do = q_ref[...] * ATT_SCALE, do_ref[...]
            kwin = jnp.concatenate([kp_ref[...], kc_ref[...], kn_ref[...]], axis=0)
            vwin = jnp.concatenate([vp_ref[...], vc_ref[...], vn_ref[...]], axis=0)
            first, first_w = _first_head(tq), _first_head(win)
            groups = list(_lane_groups())
            cols = [c for ls in groups for c in (ls.start, ls.start + HEAD_DIM)]
            lanes = [ls for ls in groups for _ in range(2)]
            qh = [t for ls in groups for t in _split_pair(q[:, ls], first)]
            doh = [t for ls in groups for t in _split_pair(do[:, ls], first)]
            s = [jnp.where(valid, _nt(t, kwin[:, ls]), NEG_INF) for t, ls in zip(qh, lanes)]
            dp = [_nt(t, vwin[:, ls]) for t, ls in zip(doh, lanes)]
            p = [jnp.exp(t - lg_ref[:, c:c + 1]) for t, c in zip(s, cols)]
            ds = [(pp * (t - dl_ref[:, c:c + 1])).astype(BF16) for pp, t, c in zip(p, dp, cols)]
            dq = [jnp.dot(t, kwin[:, ls], preferred_element_type=F32) for t, ls in zip(ds, lanes)]
            dk = [_tn(t, q[:, ls]) for t, ls in zip(ds, lanes)]
            dv = [_tn(pp.astype(BF16), do[:, ls]) for pp, ls in zip(p, lanes)]
            for g, ls in enumerate(groups):
                dq_ref[:, ls] = (jnp.where(first, dq[2 * g], dq[2 * g + 1]) * ATT_SCALE).astype(dq_ref.dtype)
                acc_k[lead:, ls] += jnp.where(first_w, dk[2 * g], dk[2 * g + 1])
                acc_v[lead:, ls] += jnp.where(first_w, dv[2 * g], dv[2 * g + 1])

        for acc, out in ((acc_k, dk_ref), (acc_v, dv_ref)):
            out[...] = acc[0:tq, :].astype(out.dtype)
            kept = acc[tq:, :]
            acc[0:acc_rows - tq, :] = kept
            acc[acc_rows - tq:, :] = jnp.zeros((tq, D_ATT), F32)

    def seq(col):
        blk = lambda b: jnp.minimum(b, n_blocks - 1)
        cls = lambda b: (blk(b) // nb) * nh
        centre = pl.BlockSpec((tq, D_ATT), lambda b: (blk(b), col))
        prev = pl.BlockSpec((HALF, D_ATT), lambda b: (cls(b) + jnp.maximum((blk(b) % nb) * per - 1, 0), col))
        nxt = pl.BlockSpec((HALF, D_ATT), lambda b: (cls(b) + jnp.minimum((blk(b) % nb + 1) * per, nh - 1), col))
        return prev, centre, nxt

    _, qc, _ = seq(0)
    kp, kc, kn = seq(1)
    vp, vc, vn = seq(v_col)
    _, doc, _ = seq(do_col)
    late = pl.BlockSpec((tq, D_ATT), lambda b: (jnp.maximum(b - 1, 0), 0))
    return pl.pallas_call(
        body, name=name, grid=(n_blocks + 1,),
        in_specs=[qc, kp, kc, kn, vp, vc, vn, doc, qc, qc], out_specs=[qc, late, late],
        out_shape=[_sds((S, D_ATT), BF16)] * 3,
        scratch_shapes=[pltpu.VMEM((acc_rows, D_ATT), F32), pltpu.VMEM((acc_rows, D_ATT), F32)],
        compiler_params=_params("arbitrary"))(qk, qk, qk, qk, v_arr, v_arr, v_arr, do_arr, lg, delta)


def _sigmoid(x):
    return 1.0 / (1.0 + jnp.exp(-x))


def _halo_specs(S, T, width, col):
    last = S // HALO - 1
    per = T // HALO
    centre = pl.BlockSpec((T, width), lambda i: (i, col))
    prev = pl.BlockSpec((HALO, width), lambda i: (jnp.maximum(i * per - 1, 0), col))
    nxt = pl.BlockSpec((HALO, width), lambda i: (jnp.minimum((i + 1) * per, last), col))
    return prev, centre, nxt


def _window_scratch(T, C):
    return pltpu.VMEM((8, T + 2 * HALO, C), F32)


def _fill_window(buf, prev, centre, nxt, T):
    buf[0, 0:HALO, :] = prev
    buf[0, HALO:HALO + T, :] = centre
    buf[0, HALO + T:, :] = nxt
    rows = T + 2 * HALO - 8
    for s in range(1, 8):
        buf[s, 0:rows, :] = buf[0, s:s + rows, :]


def _tap_reads(buf, first_off, step, r0, ls):
    by_slab = {}
    for k in range(CONV_WIDTH):
        off = first_off + step * k
        by_slab.setdefault(off % 8, []).append((k, off - off % 8))
    for s, taps in by_slab.items():
        lo = min(a for _, a in taps)
        hi = max(a for _, a in taps)
        rows = buf[s, pl.ds(lo + r0, CONV_ROWS + hi - lo), ls]
        for k, a in taps:
            yield k, rows[a - lo:a - lo + CONV_ROWS]


def _depthwise(buf, w_ref, out_ref, T, C, first_off, step):
    def row_tile(t, carry):
        r0 = pl.multiple_of(t * CONV_ROWS, CONV_ROWS)
        for c0 in range(0, C, LANES):
            ls = slice(c0, c0 + LANES)
            acc = jnp.zeros((CONV_ROWS, LANES), F32)
            for k, rows in _tap_reads(buf, first_off, step, r0, ls):
                acc = acc + rows * w_ref[k:k + 1, ls]
            out_ref[pl.ds(r0, CONV_ROWS), ls] = acc
        return carry

    lax.fori_loop(0, T // CONV_ROWS, row_tile, 0)


def _conv_fwd(y, conv_w32, conv_b, ln_g, ln_b, *, T=512):
    S = y.shape[0]
    T = min(T, S)
    nblk = S // T
    C = D_CONV

    def body(ap, ac, an, gp, gc, gn, w_ref, b_ref, lg_ref, lb_ref, cv_ref, u1_ref, buf):
        i = pl.program_id(0)

        def glu(a_ref, g_ref):
            return a_ref[...].astype(F32) * _sigmoid(g_ref[...].astype(F32))

        _fill_window(buf, jnp.where(i > 0, glu(ap, gp), 0.0), glu(ac, gc),
                     jnp.where(i < nblk - 1, glu(an, gn), 0.0), T)
        _depthwise(buf, w_ref, u1_ref, T, C, HALO - CONV_PAD, 1)
        u1 = u1_ref[...] + b_ref[...]
        u1_ref[...] = u1
        mu = jnp.mean(u1, axis=-1, keepdims=True)
        xc = u1 - mu
        rstd = lax.rsqrt(jnp.mean(xc * xc, axis=-1, keepdims=True) + EPS)
        u2 = xc * rstd * lg_ref[...] + lb_ref[...]
        cv_ref[...] = (u2 * _sigmoid(u2)).astype(cv_ref.dtype)

    ap, ac, an = _halo_specs(S, T, C, 3)
    gp, gc, gn = _halo_specs(S, T, C, 4)
    vec = pl.BlockSpec((1, C), lambda i: (0, 0))
    out = pl.BlockSpec((T, C), lambda i: (i, 0))
    return pl.pallas_call(
        body, name="conv_fwd", grid=(nblk,),
        in_specs=[ap, ac, an, gp, gc, gn, pl.BlockSpec((32, C), lambda i: (0, 0)), vec, vec, vec],
        out_specs=[out, out], out_shape=[_sds((S, C), BF16), _sds((S, C), F32)],
        scratch_shapes=[_window_scratch(T, C)],
        compiler_params=_params("parallel"))(y, y, y, y, y, y, conv_w32, conv_b, ln_g, ln_b)


def _conv_bwd(dac, u1, y, conv_w32, ln_g, ln_b, *, T=512):
    S = y.shape[0]
    T = min(T, S)
    nblk = S // T
    C = D_CONV

    def body(dp, dc, dn, up, uc, un, ap, ac, an, gp, gc, gn, w_ref, lg_ref, lb_ref,
             dag_ref, dw_ref, dsm_ref, bufd, bufu, du0_scr, dw_acc):
        i = pl.program_id(0)
        lg = lg_ref[...]

        def du1_of(dcv_ref, u1_ref):
            u1 = u1_ref[...]
            mu = jnp.mean(u1, axis=-1, keepdims=True)
            xc = u1 - mu
            rstd = lax.rsqrt(jnp.mean(xc * xc, axis=-1, keepdims=True) + EPS)
            xhat = xc * rstd
            u2 = xhat * lg + lb_ref[...]
            sg = _sigmoid(u2)
            du2 = dcv_ref[...].astype(F32) * (sg * (1.0 + u2 * (1.0 - sg)))
            dxh = du2 * lg
            du1 = rstd * (dxh - jnp.mean(dxh, axis=-1, keepdims=True)
                          - xhat * jnp.mean(dxh * xhat, axis=-1, keepdims=True))
            return du1, du2, xhat

        def glu(a_ref, g_ref):
            return a_ref[...].astype(F32) * _sigmoid(g_ref[...].astype(F32))

        @pl.when(i == 0)
        def _():
            dw_ref[...] = jnp.zeros_like(dw_ref)
            dsm_ref[...] = jnp.zeros_like(dsm_ref)

        du1_c, du2_c, xhat_c = du1_of(dc, uc)
        dsm_ref[0:1, :] += jnp.sum(du1_c, axis=0, keepdims=True)
        dsm_ref[1:2, :] += jnp.sum(du2_c * xhat_c, axis=0, keepdims=True)
        dsm_ref[2:3, :] += jnp.sum(du2_c, axis=0, keepdims=True)
        _fill_window(bufd, jnp.where(i > 0, du1_of(dp, up)[0], 0.0), du1_c,
                     jnp.where(i < nblk - 1, du1_of(dn, un)[0], 0.0), T)
        _fill_window(bufu, jnp.where(i > 0, glu(ap, gp), 0.0), glu(ac, gc),
                     jnp.where(i < nblk - 1, glu(an, gn), 0.0), T)

        _depthwise(bufd, w_ref, du0_scr, T, C, HALO + CONV_PAD, -1)
        dw_acc[...] = jnp.zeros_like(dw_acc)

        def dw_tile(t, carry):
            r0 = pl.multiple_of(t * CONV_ROWS, CONV_ROWS)
            for c0 in range(0, C, LANES):
                ls = slice(c0, c0 + LANES)
                d = bufd[0, pl.ds(HALO + r0, CONV_ROWS), ls]
                for k, rows in _tap_reads(bufu, HALO - CONV_PAD, 1, r0, ls):
                    prod = d * rows
                    part = prod[0:8]
                    for j in range(8, CONV_ROWS, 8):
                        part = part + prod[j:j + 8]
                    dw_acc[k, :, ls] += part
            return carry

        lax.fori_loop(0, T // CONV_ROWS, dw_tile, 0)
        for k in range(CONV_WIDTH):
            dw_ref[k:k + 1, :] += jnp.sum(dw_acc[k], axis=0, keepdims=True)
        du0 = du0_scr[...]
        a = ac[...].astype(F32)
        sg = _sigmoid(gc[...].astype(F32))
        dag_ref[:, 0:C] = (du0 * sg).astype(dag_ref.dtype)
        dag_ref[:, C:] = (du0 * a * sg * (1.0 - sg)).astype(dag_ref.dtype)

    dp, dc, dn = _halo_specs(S, T, C, 1)
    up, uc, un = _halo_specs(S, T, C, 0)
    ap, ac, an = _halo_specs(S, T, C, 3)
    gp, gc, gn = _halo_specs(S, T, C, 4)
    vec = pl.BlockSpec((1, C), lambda i: (0, 0))
    return pl.pallas_call(
        body, name="conv_bwd", grid=(nblk,),
        in_specs=[dp, dc, dn, up, uc, un, ap, ac, an, gp, gc, gn,
                  pl.BlockSpec((32, C), lambda i: (0, 0)), vec, vec],
        out_specs=[pl.BlockSpec((T, 2 * C), lambda i: (i, 0)), pl.BlockSpec((32, C), lambda i: (0, 0)),
                   pl.BlockSpec((8, C), lambda i: (0, 0))],
        out_shape=[_sds((S, 2 * C), BF16), _sds((32, C), F32), _sds((8, C), F32)],
        scratch_shapes=[_window_scratch(T, C), _window_scratch(T, C), pltpu.VMEM((T, C), F32),
                        pltpu.VMEM((CONV_WIDTH, 8, C), F32)],
        compiler_params=_params("arbitrary"))(dac, dac, dac, u1, u1, u1, y, y, y, y, y, y, conv_w32, ln_g, ln_b)


def _xatt_fwd(xq, xk, xv, *, tm=512):
    S = xq.shape[0]
    M = xk.shape[0]
    tm = min(tm, S)
    scale = XATT_HEAD_DIM ** -0.5

    def body(q_ref, k_ref, v_ref, o_ref):
        heads = [slice(h * XATT_HEAD_DIM, (h + 1) * XATT_HEAD_DIM) for h in range(XATT_HEADS)]
        s = [_nt(q_ref[:, sl], k_ref[:, sl]) * scale for sl in heads]
        e = [jnp.exp(t - jnp.max(t, axis=-1, keepdims=True)) for t in s]
        p = [t * (1.0 / jnp.sum(t, axis=-1, keepdims=True)) for t in e]
        for sl, t in zip(heads, p):
            o_ref[:, sl] = jnp.dot(t.astype(BF16), v_ref[:, sl], preferred_element_type=F32).astype(o_ref.dtype)

    row = pl.BlockSpec((tm, D_MODEL), lambda i: (i, 0))
    full = pl.BlockSpec((M, D_MODEL), lambda i: (0, 0))
    return pl.pallas_call(
        body, name="xatt_fwd", grid=(S // tm,), in_specs=[row, full, full], out_specs=row,
        out_shape=_sds((S, D_MODEL), BF16), compiler_params=_params("parallel"))(xq, xk, xv)


def _xatt_bwd(xq, xk, xv, dxo, *, tm=512):
    S = xq.shape[0]
    M = xk.shape[0]
    tm = min(tm, S)
    scale = XATT_HEAD_DIM ** -0.5

    def body(q_ref, k_ref, v_ref, do_ref, dq_ref, dk_ref, dv_ref):
        i = pl.program_id(0)

        @pl.when(i == 0)
        def _():
            dk_ref[...] = jnp.zeros_like(dk_ref)
            dv_ref[...] = jnp.zeros_like(dv_ref)

        heads = [slice(h * XATT_HEAD_DIM, (h + 1) * XATT_HEAD_DIM) for h in range(XATT_HEADS)]
        s = [_nt(q_ref[:, sl], k_ref[:, sl]) * scale for sl in heads]
        dp = [_nt(do_ref[:, sl], v_ref[:, sl]) for sl in heads]
        e = [jnp.exp(t - jnp.max(t, axis=-1, keepdims=True)) for t in s]
        p = [t * (1.0 / jnp.sum(t, axis=-1, keepdims=True)) for t in e]
        ds = [(pp * (t - jnp.sum(t * pp, axis=-1, keepdims=True))).astype(BF16) for pp, t in zip(p, dp)]
        for sl, pp, t in zip(heads, p, ds):
            dq_ref[:, sl] = (jnp.dot(t, k_ref[:, sl], preferred_element_type=F32) * scale).astype(dq_ref.dtype)
            dv_ref[:, sl] += _tn(pp.astype(BF16), do_ref[:, sl])
            dk_ref[:, sl] += _tn(t, q_ref[:, sl]) * scale

    row = pl.BlockSpec((tm, D_MODEL), lambda i: (i, 0))
    full = pl.BlockSpec((M, D_MODEL), lambda i: (0, 0))
    return pl.pallas_call(
        body, name="xatt_bwd", grid=(S // tm,), in_specs=[row, full, full, row], out_specs=[row, full, full],
        out_shape=[_sds((S, D_MODEL), BF16), _sds((M, D_MODEL), F32), _sds((M, D_MODEL), F32)],
        compiler_params=_params("arbitrary"))(xq, xk, xv, dxo)


def _row_tile(R):
    for t in (256, 128, 64, 32, 16, 8):
        if R % t == 0:
            return t
    return R


def _sum_partials(own, recv, me, *, name):
    _, R, C = own.shape
    t = _row_tile(R)

    def body(me_ref, own_ref, r_ref, o_ref):
        o_ref[...] = ((own_ref[...].astype(F32) + r_ref[0].astype(F32)) + r_ref[1].astype(F32)) + r_ref[2].astype(F32)

    return pl.pallas_call(
        body, name=name,
        grid_spec=pltpu.PrefetchScalarGridSpec(
            num_scalar_prefetch=1, grid=(R // t,),
            in_specs=[pl.BlockSpec((None, t, C), lambda i, me_ref: (me_ref[0], i, 0)),
                      pl.BlockSpec((3, t, C), lambda i, me_ref: (0, i, 0))],
            out_specs=pl.BlockSpec((t, C), lambda i, me_ref: (i, 0))),
        out_shape=_sds((R, C), F32), compiler_params=_params("parallel"))(me, own, recv)


def _adamw_math(w, g, m, v):
    m2 = ADAM_B1 * m + (1.0 - ADAM_B1) * g
    v2 = ADAM_B2 * v + (1.0 - ADAM_B2) * (g * g)
    m_hat = m2 / (1.0 - ADAM_B1 ** ADAM_STEP)
    v_hat = v2 / (1.0 - ADAM_B2 ** ADAM_STEP)
    delta = -ADAM_LR * (m_hat / (jnp.sqrt(v_hat) + ADAM_EPS) + ADAM_WD * w)
    return delta, m2, v2


def _adamw(parts, w, m, v, *, name):
    R, C = w.shape
    t = _row_tile(R)
    n = len(parts)

    def body(*refs):
        w_ref, m_ref, v_ref = refs[n:n + 3]
        g_ref, d_ref, m2_ref, v2_ref = refs[n + 3:]
        g = refs[0][...]
        for r in refs[1:n]:
            g = g + r[...]
        delta, m2, v2 = _adamw_math(w_ref[...], g, m_ref[...], v_ref[...])
        g_ref[...] = g
        d_ref[...] = delta
        m2_ref[...] = m2
        v2_ref[...] = v2

    blk = pl.BlockSpec((t, C), lambda i: (i, 0))
    return pl.pallas_call(
        body, name=name, grid=(R // t,), in_specs=[blk] * (n + 3), out_specs=[blk] * 4,
        out_shape=[_sds((R, C), F32)] * 4, compiler_params=_params("parallel"))(*parts, w, m, v)


def _adamw_small(gathered, chip, entries):
    _, R, C = gathered.shape
    n = len(entries)
    group = D_CONV // N_CHIPS

    def body(chip_ref, g_ref, *refs):
        ins, outs, tot_ref = refs[:3 * n], refs[3 * n:7 * n], refs[7 * n]
        tot = g_ref[0]
        for k in range(1, N_DEV):
            tot = tot + g_ref[k]
        tot_ref[...] = tot
        for e, ((kind, r), _, _, _) in enumerate(entries):
            if kind == "row":
                g = tot_ref[r:r + 1, :]
            elif kind == "gain":
                g = jnp.concatenate([tot_ref[r:r + 1, :], tot_ref[r + 1:r + 2, :]], axis=1)
            else:
                g = tot_ref[r:r + CONV_WIDTH, 0:group]
                for j in range(1, N_CHIPS):
                    g = jnp.where(chip_ref[0] == j, tot_ref[r:r + CONV_WIDTH, j * group:(j + 1) * group], g)
            delta, m2, v2 = _adamw_math(ins[3 * e][...], g, ins[3 * e + 1][...], ins[3 * e + 2][...])
            for o, val in zip(outs[4 * e:4 * e + 4], (g, delta, m2, v2)):
                o[...] = val

    whole = lambda a: pl.BlockSpec(a.shape, lambda i, c: (0,) * a.ndim)
    arrays = [a for _, w, m, v in entries for a in (w, m, v)]
    out_like = [w for _, w, _, _ in entries for _ in range(4)]
    tot_like = _sds((R, C), F32)
    out = pl.pallas_call(
        body, name="adamw_small",
        grid_spec=pltpu.PrefetchScalarGridSpec(
            num_scalar_prefetch=1, grid=(1,),
            in_specs=[whole(gathered)] + [whole(a) for a in arrays],
            out_specs=[whole(a) for a in out_like] + [whole(tot_like)]),
        out_shape=[_sds(a.shape, F32) for a in out_like] + [tot_like],
        compiler_params=_params("arbitrary"))(chip, gathered, *arrays)
    return out[-1], [tuple(out[4 * e:4 * e + 4]) for e in range(n)]


def _chip_peers():
    x, y = lax.axis_index("x"), lax.axis_index("y")
    return [(1 - x, y), (x, 1 - y), (1 - x, 1 - y)]


HBM_SPEC = pl.BlockSpec(memory_space=pltpu.HBM)
SEM_SPEC = pl.BlockSpec(memory_space=pltpu.SEMAPHORE)


def _exchange_peers(mode):
    x, y, c = lax.axis_index("x"), lax.axis_index("y"), lax.axis_index("c")
    if mode == "swap":
        return [(x, y, 1 - c)]
    if mode == "all":
        flips = [(fx, fy, fc) for fx in (0, 1) for fy in (0, 1) for fc in (0, 1)][1:]
        return [(1 - x if fx else x, 1 - y if fy else y, 1 - c if fc else c) for fx, fy, fc in flips]
    return [(px, py, c) for px, py in _chip_peers()]


def _exchange_start(mode, srcs, zones, *, name):
    n = len(srcs)

    def body(*refs):
        ins, lands = refs[:n], refs[n:2 * n]
        send_sems, recv_sems = refs[2 * n:3 * n], refs[3 * n:4 * n]
        token = refs[-1]
        x, y, c = lax.axis_index("x"), lax.axis_index("y"), lax.axis_index("c")
        mine = 2 * x + y if mode == "gather" else 4 * x + 2 * y + c
        for t in range(n):
            for k, (px, py, pc) in enumerate(_exchange_peers(mode)):
                if mode in ("gather", "all"):
                    s, d = ins[t], lands[t].at[mine]
                elif mode == "scatter":
                    s, d = ins[t].at[2 * px + py], lands[t].at[k]
                else:
                    s, d = ins[t], lands[t]
                pltpu.make_async_remote_copy(src_ref=s, dst_ref=d, send_sem=send_sems[t], recv_sem=recv_sems[t],
                                             device_id=(px, py, pc), device_id_type=MESH).start()
            if mode in ("gather", "all"):
                pltpu.make_async_copy(ins[t], lands[t].at[mine], send_sems[t]).start()
        token[...] = jnp.zeros_like(token)

    hbm = lambda a: pltpu.with_memory_space_constraint(a, pltpu.HBM)
    out = pl.pallas_call(
        body, name=name,
        in_specs=[HBM_SPEC] * (2 * n),
        out_specs=[SEM_SPEC] * (2 * n) + [HBM_SPEC] * (2 * n) + [pl.BlockSpec(memory_space=pltpu.VMEM)],
        out_shape=[pltpu.SemaphoreType.DMA(())] * (2 * n)
        + [pltpu.HBM(a.shape, a.dtype) for a in list(srcs) + list(zones)] + [_sds((8, LANES), F32)],
        input_output_aliases={i: 2 * n + i for i in range(2 * n)},
        compiler_params=pltpu.CompilerParams(has_side_effects=pltpu.SideEffectType.DATAFLOW_SIDE_EFFECTING),
    )(*[hbm(a) for a in list(srcs) + list(zones)])
    return out[:n], out[n:2 * n], out[2 * n:3 * n], out[3 * n:4 * n], out[-1]


def _exchange_wait(mode, started, after, *, name):
    send_sems, recv_sems, srcs, zones, _ = started
    n = len(srcs)
    afters = tuple(after) if isinstance(after, (tuple, list)) else (after,)

    def body(*refs):
        lands = refs[n:2 * n]
        send_refs, recv_refs = refs[2 * n:3 * n], refs[3 * n:4 * n]
        me = (lax.axis_index("x"), lax.axis_index("y"), lax.axis_index("c"))
        n_remote = {"gather": N_CHIPS - 1, "scatter": N_CHIPS - 1, "all": N_DEV - 1, "swap": 1}[mode]
        for t in range(n):
            got = lands[t] if mode == "swap" else lands[t].at[pl.ds(0, n_remote)]
            sent = lands[t] if mode in ("gather", "all") else got
            pltpu.make_async_remote_copy(src_ref=sent, dst_ref=sent, send_sem=send_refs[t], recv_sem=recv_refs[t],
                                         device_id=me, device_id_type=MESH).wait_send()
            pltpu.make_async_remote_copy(src_ref=got, dst_ref=got, send_sem=send_refs[t], recv_sem=recv_refs[t],
                                         device_id=me, device_id_type=MESH).wait_recv()

    out = pl.pallas_call(
        body, name=name,
        in_specs=[HBM_SPEC] * (2 * n) + [SEM_SPEC] * (2 * n) + [pl.BlockSpec(memory_space=pl.ANY)] * len(afters),
        out_specs=[HBM_SPEC] * (2 * n),
        out_shape=[pltpu.HBM(a.shape, a.dtype) for a in list(srcs) + list(zones)],
        input_output_aliases={i: i for i in range(2 * n)},
        compiler_params=pltpu.CompilerParams(has_side_effects=pltpu.SideEffectType.DATAFLOW_SIDE_EFFECTING),
    )(*srcs, *zones, *send_sems, *recv_sems, *afters)
    return out[:n], out[n:]


def _swap_with_sibling(parts):
    n = len(parts)

    def body(*refs):
        ins, outs = refs[:n], refs[n:2 * n]
        send_sems, recv_sems = refs[2 * n:]
        sib = (lax.axis_index("x"), lax.axis_index("y"), 1 - lax.axis_index("c"))
        cps = []
        for t in range(n):
            cp = pltpu.make_async_remote_copy(
                src_ref=ins[t], dst_ref=outs[t], send_sem=send_sems.at[t], recv_sem=recv_sems.at[t],
                device_id=sib, device_id_type=MESH)
            cp.start()
            cps.append(cp)
        for cp in cps:
            cp.wait()

    any_spec = pl.BlockSpec(memory_space=pl.ANY)
    return pl.pallas_call(
        body, name="swap_with_sibling", in_specs=[any_spec] * n, out_specs=[any_spec] * n,
        out_shape=[_sds(p.shape, p.dtype) for p in parts],
        scratch_shapes=[pltpu.SemaphoreType.DMA((n,)), pltpu.SemaphoreType.DMA((n,))])(*parts)


BIG = ("w_in", "w_out", "w_xq", "w_xk", "w_xv", "w_xo", "w_up", "w_down")
COL_SHARDED = ("w_in", "w_up")


def _as_matrix(name, w4):
    if name in COL_SHARDED:
        return w4
    return w4.reshape(1, w4.shape[0] * w4.shape[1], w4.shape[2])


def _shard_layout(name, g):
    if name in COL_SHARDED:
        return g
    return g.reshape(N_CHIPS, g.shape[0] * g.shape[1] // N_CHIPS, g.shape[2])


def _local_step(x, mem, target, vecs, comm):
    S = x.shape[0]
    tables = _rope_tables(S)

    xn = _rms_fwd(x, vecs["norm_mix_g"], name="rms_mix")
    w_in, conv_w32 = comm["first"]((xn,) + tuple(tables))
    w_in_whole = jnp.swapaxes(w_in, 0, 1).reshape(1, D_MODEL, D_IN)
    y = _mm_nn(xn, w_in_whole, name="mm_in", tm=2048, tn=D_IN // 2)
    qk, v_perm = _rope_fwd(y, tables)
    v_src = [(y, 2)] + [(v, 0) for v in v_perm[1:]]
    outs, lses = zip(*[_att_fwd(qk[p], v_src[p], d, name=f"att_fwd_d{d}") for p, d in enumerate(DILATIONS)])
    att, lg = _att_combine(outs, lses)
    cv, u1 = _conv_fwd(y, conv_w32, vecs["conv_b"], vecs["conv_ln_g"], vecs["conv_ln_b"])
    Wm = {k: _as_matrix(k, v) for k, v in comm["rest"]((att, cv)).items()}
    Wm["w_in"] = w_in
    h1, hn = _mm_rows((att, cv), Wm["w_out"], _residual_norm_tail, name="mm_out_rms", rows_in=(x,),
                      vecs_in=(vecs["norm_x_g"],), rows_out=(F32, BF16))
    xq = _mm_nn(hn, Wm["w_xq"], name="mm_xq")
    mn = _rms_fwd(mem, vecs["norm_mem_g"], name="rms_mem")
    xk = _mm_nn(mn, Wm["w_xk"], name="mm_xk")
    xv = _mm_nn(mn, Wm["w_xv"], name="mm_xv")
    xo = _xatt_fwd(xq, xk, xv)
    h2, hm = _mm_rows(xo, Wm["w_xo"], _residual_norm_tail, name="mm_xo_rms", rows_in=(h1,),
                      vecs_in=(vecs["norm_mlp_g"],), rows_out=(F32, BF16))
    relu_up = _mm_nn(hm, Wm["w_up"], name="mm_up", relu=True, tm=2048)
    sums = ((8, D_MODEL),)
    dh3, dh3b, dg_final, loss = _mm_rows(
        relu_up, Wm["w_down"], _loss_tail, name="mm_down_loss", rows_in=(h2, target), vecs_in=(vecs["norm_final_g"],),
        rows_out=(F32, BF16), sums_out=sums + ((8, LANES),), a_squared=True, tm=256)
    g = {}
    g["w_down"] = _mm_tn(relu_up, dh3b, 1, name="dw_down", a_squared=True)
    dup = _mm_nt(dh3b, Wm["w_down"], name="d_act", out_dtype=BF16, mul=relu_up, tm=2048)
    g["w_up"] = _mm_tn(hm, dup, N_CHIPS, name="dw_up")
    sent = comm["send_mlp"]({k: _shard_layout(k, g[k]) for k in ("w_down", "w_up")})
    dh2, dh2b, dg_mlp = _mm_rows(
        dup, Wm["w_up"], _rms_bwd_tail(True), name="d_hm_rms", w_transposed=True, rows_in=(h2, dh3),
        vecs_in=(vecs["norm_mlp_g"] + sent[0:1, 0:1],), rows_out=(F32, BF16), sums_out=sums, tm=256)
    g["w_xo"] = _mm_tn(xo, dh2b, 1, name="dw_xo")
    dxo = _mm_nt(dh2b, Wm["w_xo"], name="d_xo", out_dtype=BF16)
    dxq, dxk, dxv = _xatt_bwd(xq, xk, xv, dxo)
    g["w_xq"] = _mm_tn(hn, dxq, 1, name="dw_xq")
    dh1, dh1b, dg_x = _mm_rows(
        dxq, Wm["w_xq"], _rms_bwd_tail(True), name="d_hn_rms", w_transposed=True, rows_in=(h1, dh2),
        vecs_in=(vecs["norm_x_g"],), rows_out=(F32, BF16), sums_out=sums)
    dxkb, dxvb = dxk.astype(BF16), dxv.astype(BF16)
    g["w_xk"] = _mm_tn(mn, dxkb, 1, name="dw_xk")
    g["w_xv"] = _mm_tn(mn, dxvb, 1, name="dw_xv")
    dmn = _mm_nt(jnp.concatenate([dxkb, dxvb], axis=1),
                 jnp.concatenate([Wm["w_xk"], Wm["w_xv"]], axis=2), name="d_mn", out_dtype=BF16)
    _, _, dg_mem = _rms_bwd(dmn, mem, vecs["norm_mem_g"], None, name="rms_bwd_mem", bf16_copy=False)
    g["w_out"] = jnp.concatenate([_mm_tn(att, dh1b, 1, name="dw_out_att"), _mm_tn(cv, dh1b, 1, name="dw_out_conv")],
                                 axis=1)
    sent = comm["send_att"]({k: _shard_layout(k, g[k]) for k in ("w_out", "w_xq", "w_xk", "w_xv", "w_xo")})
    dac = _mm_nt(dh1b, Wm["w_out"], name="d_mix", out_dtype=BF16)
    dag, dconv_w, dconv_small = _conv_bwd(dac, u1, y, conv_w32, vecs["conv_ln_g"] + sent[0:1, 0:1],
                                          vecs["conv_ln_b"])
    delta, do_perm = _att_delta(dac, att)
    do_src = [(dac, 0)] + [(t, 0) for t in do_perm[1:]]
    dq, dk, dv = zip(*[_att_bwd(qk[p], v_src[p], do_src[p], lg[p], delta[p], d, name=f"att_bwd_d{d}")
                       for p, d in enumerate(DILATIONS)])
    dy = _assemble_dy(dq, dk, dv, dag, tables)
    g_in = _mm_tn(xn, dy, 1, name="dw_in", tn=D_IN // 2)
    g_in = jnp.swapaxes(g_in.reshape(D_MODEL, N_CHIPS, D_IN // N_CHIPS), 0, 1)
    sent = comm["send_in"]({"w_in": g_in})
    grad_x, dg_mix = _mm_rows(
        dy, Wm["w_in"], _rms_bwd_tail(False), name="d_xn_rms", w_transposed=True, rows_in=(x, dh1),
        vecs_in=(vecs["norm_mix_g"] + sent[0:1, 0:1],), rows_out=(F32,), sums_out=sums)

    small = dict(conv_w=dconv_w, conv_small=dconv_small, norm_mix_g=dg_mix, norm_x_g=dg_x, norm_mem_g=dg_mem,
                 norm_mlp_g=dg_mlp, norm_final_g=dg_final, loss=loss)
    return grad_x, small


SMALL_ORDER = ("conv_w", "conv_small", "norm_mix_g", "norm_x_g", "norm_mem_g", "norm_mlp_g", "norm_final_g", "loss")


def _pack_small(small):
    rows, offs, pos = [], {}, 0
    for k in SMALL_ORDER:
        a = small[k]
        a = a.reshape(a.shape[0] * a.shape[1] // SMALL_W, SMALL_W)
        pad = (-a.shape[0]) % 8
        if pad:
            a = jnp.pad(a, ((0, pad), (0, 0)))
        rows.append(a)
        offs[k] = pos
        pos += a.shape[0]
    return jnp.concatenate(rows, axis=0), offs


def kernel(x, mem, norm_mix_g, w_in, conv_w, conv_b, conv_ln_g, conv_ln_b, w_out, norm_x_g, norm_mem_g, w_xq, w_xk, w_xv, w_xo, norm_mlp_g, w_up, w_down, norm_final_g, loss_target, m_norm_mix_g, m_w_in, m_conv_w, m_conv_b, m_conv_ln_g, m_conv_ln_b, m_w_out, m_norm_x_g, m_norm_mem_g, m_w_xq, m_w_xk, m_w_xv, m_w_xo, m_norm_mlp_g, m_w_up, m_w_down, m_norm_final_g, v_norm_mix_g, v_w_in, v_conv_w, v_conv_b, v_conv_ln_g, v_conv_ln_b, v_w_out, v_norm_x_g, v_norm_mem_g, v_w_xq, v_w_xk, v_w_xv, v_w_xo, v_norm_mlp_g, v_w_up, v_w_down, v_norm_final_g):
    names = ("norm_mix_g", "w_in", "conv_w", "conv_b", "conv_ln_g", "conv_ln_b", "w_out", "norm_x_g", "norm_mem_g",
             "w_xq", "w_xk", "w_xv", "w_xo", "norm_mlp_g", "w_up", "w_down", "norm_final_g")
    wts = dict(zip(names, (norm_mix_g, w_in, conv_w, conv_b, conv_ln_g, conv_ln_b, w_out, norm_x_g, norm_mem_g,
                           w_xq, w_xk, w_xv, w_xo, norm_mlp_g, w_up, w_down, norm_final_g)))
    mom = dict(zip(names, (m_norm_mix_g, m_w_in, m_conv_w, m_conv_b, m_conv_ln_g, m_conv_ln_b, m_w_out, m_norm_x_g,
                           m_norm_mem_g, m_w_xq, m_w_xk, m_w_xv, m_w_xo, m_norm_mlp_g, m_w_up, m_w_down, m_norm_final_g)))
    var = dict(zip(names, (v_norm_mix_g, v_w_in, v_conv_w, v_conv_b, v_conv_ln_g, v_conv_ln_b, v_w_out, v_norm_x_g,
                           v_norm_mem_g, v_w_xq, v_w_xk, v_w_xv, v_w_xo, v_norm_mlp_g, v_w_up, v_w_down, v_norm_final_g)))
    chip = 2 * lax.axis_index("x") + lax.axis_index("y")

    def zone(shard):
        return lax.empty((N_CHIPS,) + shard.shape, shard.dtype)

    conv_w_pad = jnp.pad(wts["conv_w"][0], ((0, 1), (0, 0)))
    first_shards = [wts["w_in"][0].astype(BF16), conv_w_pad]
    gathering_first = _exchange_start("gather", first_shards, [zone(s) for s in first_shards],
                                      name="gather_first_start")
    rest = tuple(k for k in BIG if k != "w_in")
    behind_first = gathering_first[4][0, 0]
    rest_shards = [(wts[k][0] + behind_first).astype(BF16) for k in rest]
    gathering = gathering_rest = _exchange_start("gather", rest_shards, [zone(s) for s in rest_shards],
                                                 name="gather_rest_start")
    sending = {}

    def wait_first(after):
        _, (w_in_all, conv_w_all) = _exchange_wait("gather", gathering_first, after, name="gather_first_wait")
        return w_in_all, jnp.transpose(conv_w_all, (1, 0, 2)).reshape(32, D_CONV)

    def wait_rest(after):
        _, zones = _exchange_wait("gather", gathering_rest, after, name="gather_rest_wait")
        return dict(zip(rest, zones))

    def send(group, grads):
        keys = tuple(grads)
        zones = [lax.empty((N_CHIPS - 1,) + grads[k].shape[1:], grads[k].dtype) for k in keys]
        sending[group] = (keys, _exchange_start("scatter", [grads[k] for k in keys], zones,
                                                name=f"scatter_{group}_start"))
        return sending[group][1][4]

    comm = dict(first=wait_first, rest=wait_rest, send_mlp=lambda grads: send("mlp", grads),
                send_att=lambda grads: send("att", grads), send_in=lambda grads: send("in", grads))
    vecs = {k: wts[k] for k in ("conv_b", "conv_ln_g", "conv_ln_b", "norm_x_g", "norm_mem_g", "norm_mlp_g")}
    vecs["norm_mix_g"] = wts["norm_mix_g"] + gathering[4][0:1, 0:1]
    vecs["norm_final_g"] = wts["norm_final_g"].reshape(1, D_MODEL)
    grad_x, small = _local_step(x[0], mem[0], loss_target[0], vecs, comm)

    packed, offs = _pack_small(small)
    me_arr = jnp.reshape(chip, (1,)).astype(jnp.int32)
    gathering_small = _exchange_start("all", [packed], [lax.empty((N_DEV,) + packed.shape, packed.dtype)],
                                      name="allgather_small_start")
    sums = {}

    def settle(group, after):
        keys, started = sending[group]
        srcs, zones = _exchange_wait("scatter", started, after, name=f"scatter_{group}_wait")
        for k, own, got in zip(keys, srcs, zones):
            sums[k] = _sum_partials(own, got, me_arr, name=f"sum_{k}")

    settle("mlp", gathering_small[4])
    settle("att", gathering_small[4])
    early = tuple(sums)
    swapping = _exchange_start("swap", [sums[k] for k in early], [lax.empty(sums[k].shape, F32) for k in early],
                               name="swap_early_start")
    settle("in", swapping[4])
    _, (gath,) = _exchange_wait("all", gathering_small, sums["w_in"], name="allgather_small_wait")

    where = {"conv_w": ("conv_w", offs["conv_w"]), "conv_b": ("row", offs["conv_small"]),
             "conv_ln_g": ("row", offs["conv_small"] + 1), "conv_ln_b": ("row", offs["conv_small"] + 2)}
    where.update({k: ("gain", offs[k]) for k in ("norm_mix_g", "norm_x_g", "norm_mem_g", "norm_mlp_g", "norm_final_g")})
    as_2d = lambda a: a.reshape(a.shape[-2] if a.ndim > 1 else 1, a.shape[-1])
    tot_small, updates = _adamw_small(gath, me_arr, [(where[k], as_2d(wts[k]), as_2d(mom[k]), as_2d(var[k]))
                                                     for k in where])
    res = dict(zip(where, updates))
    loss = tot_small[offs["loss"], 0]

    sib = {"w_in": _swap_with_sibling([sums["w_in"]])[0]}
    mine_early, sib_early = _exchange_wait("swap", swapping, sib["w_in"], name="swap_early_wait")
    sums.update(zip(early, mine_early))
    sib.update(zip(early, sib_early))
    for k in BIG:
        res[k] = _adamw([sums[k], sib[k]], wts[k][0], mom[k][0], var[k][0], name=f"adamw_{k}")

    outs = [loss, grad_x[None]]
    for j in range(4):
        outs += [res[k][j].reshape(wts[k].shape) for k in names]
    return tuple(outs)
```

```python
import jax
import jax.numpy as jnp
from jax import lax
from jax.experimental import pallas as pl
from jax.experimental.pallas import tpu as pltpu

F32 = jnp.float32
BF16 = jnp.bfloat16
MESH = pl.DeviceIdType.MESH

D_MODEL = 1024
ATT_HEADS = 8
HEAD_DIM = 64
D_ATT = ATT_HEADS * HEAD_DIM
D_CONV = D_MODEL - D_ATT
DILATIONS = (1, 4, 16)
HALF = 64
ROPE_THETA = 500000.0
ROT_DIM = HEAD_DIM // 4
CONV_WIDTH = 31
CONV_PAD = (CONV_WIDTH - 1) // 2
XATT_HEADS = 4
XATT_HEAD_DIM = D_MODEL // XATT_HEADS
D_FF = 4 * D_MODEL
D_IN = 3 * D_ATT + 2 * D_CONV
EPS = 1e-6
NEG_INF = -1e30
N_CHIPS = 4
N_DEV = 8

ADAM_LR = 0.001
ADAM_B1 = 0.9
ADAM_B2 = 0.999
ADAM_EPS = 1e-08
ADAM_WD = 0.01
ADAM_STEP = 10

VMEM_LIMIT_V7X = 56 * 1024 * 1024
LANES = 128
HALO = 16
CONV_ROWS = 64
ATT_BLOCK = 128
SMALL_W = 512


def _params(*sem):
    return pltpu.CompilerParams(dimension_semantics=sem, vmem_limit_bytes=VMEM_LIMIT_V7X)


def _sds(shape, dtype):
    return jax.ShapeDtypeStruct(shape, dtype)


def _squared(a):
    af = a.astype(F32)
    return (af * af).astype(BF16)


def _mm_nn(a, w3, *, name, out_dtype=BF16, res=None, relu=False, a_squared=False, group=1, tm=1024, tn=None,
           tk=1024):
    M, K = a.shape
    nsh, _, n = w3.shape
    tm, tk = min(tm, M), min(tk, K)
    tn = group * n if group > 1 else (tn or min(n, 1024))
    npt, nk = max(n // tn, 1), K // tk
    nj, N = nsh * npt // group, nsh * n
    n_out = 1

    def body(*refs):
        a_ref, w_ref = refs[0], refs[1]
        pos = 2
        res_ref = None
        if res is not None:
            res_ref = refs[pos]
            pos += 1
        outs = refs[pos:pos + n_out]
        acc_ref = refs[pos + n_out] if nk > 1 else None

        def finish(acc):
            if res_ref is not None:
                acc = acc + res_ref[...]
            if relu:
                acc = jnp.maximum(acc, 0.0)
            outs[0][...] = acc.astype(outs[0].dtype)

        a_val = _squared(a_ref[...]) if a_squared else a_ref[...]
        w_val = w_ref[...] if group == 1 else jnp.concatenate([w_ref[s] for s in range(group)], axis=1)
        part = jnp.dot(a_val, w_val, preferred_element_type=F32)
        if nk == 1:
            finish(part)
        else:
            k = pl.program_id(2)

            @pl.when(k == 0)
            def _():
                acc_ref[...] = part

            @pl.when(k > 0)
            def _():
                acc_ref[...] += part

            @pl.when(k == nk - 1)
            def _():
                finish(acc_ref[...])

    w_spec = (pl.BlockSpec((None, tk, tn), lambda i, j, k: (j // npt, k, j % npt)) if group == 1 else
              pl.BlockSpec((group, tk, n), lambda i, j, k: (j, k, 0)))
    in_specs = [pl.BlockSpec((tm, tk), lambda i, j, k: (i, k)), w_spec]
    args = [a, w3]
    if res is not None:
        in_specs.append(pl.BlockSpec((tm, tn), lambda i, j, k: (i, j)))
        args.append(res)
    out_spec = pl.BlockSpec((tm, tn), lambda i, j, k: (i, j))
    out = pl.pallas_call(
        body, name=name, grid=(M // tm, nj, nk), in_specs=in_specs,
        out_specs=[out_spec] * n_out, out_shape=[_sds((M, N), out_dtype)] * n_out,
        scratch_shapes=[pltpu.VMEM((tm, tn), F32)] if nk > 1 else [],
        compiler_params=_params("parallel", "parallel", "arbitrary"))(*args)
    return out[0]


def _mm_nt(dy, w3, *, name, out_dtype=F32, mul=None, tm=1024, tn=None, tko=1024):
    M, N = dy.shape
    nsh, K, n = w3.shape
    tm, tko = min(tm, M), min(tko, K)
    tn = tn or min(n, 1024)
    npt = n // tn
    nj = nsh * npt

    def body(*refs):
        dy_ref, w_ref = refs[0], refs[1]
        pos = 2
        mul_ref = None
        if mul is not None:
            mul_ref = refs[pos]
            pos += 1
        out_ref = refs[pos]
        acc_ref = refs[pos + 1] if nj > 1 else None

        def finish(acc):
            if mul_ref is not None:
                acc = acc * (2.0 * mul_ref[...].astype(F32))
            out_ref[...] = acc.astype(out_ref.dtype)

        part = lax.dot_general(dy_ref[...], w_ref[...], (((1,), (1,)), ((), ())), preferred_element_type=F32)
        if nj == 1:
            finish(part)
        else:
            j = pl.program_id(2)

            @pl.when(j == 0)
            def _():
                acc_ref[...] = part

            @pl.when(j > 0)
            def _():
                acc_ref[...] += part

            @pl.when(j == nj - 1)
            def _():
                finish(acc_ref[...])

    in_specs = [pl.BlockSpec((tm, tn), lambda i, ko, j: (i, j)),
                pl.BlockSpec((None, tko, tn), lambda i, ko, j: (j // npt, ko, j % npt))]
    args = [dy, w3]
    if mul is not None:
        in_specs.append(pl.BlockSpec((tm, tko), lambda i, ko, j: (i, ko)))
        args.append(mul)
    return pl.pallas_call(
        body, name=name, grid=(M // tm, K // tko, nj), in_specs=in_specs,
        out_specs=pl.BlockSpec((tm, tko), lambda i, ko, j: (i, ko)), out_shape=_sds((M, K), out_dtype),
        scratch_shapes=[pltpu.VMEM((tm, tko), F32)] if nj > 1 else [],
        compiler_params=_params("parallel", "parallel", "arbitrary"))(*args)


def _mm_tn(a, dy, nsh, *, name, out_dtype=BF16, a_squared=False, group=1, tm=2048, tk=1024, tn=None):
    M, K = a.shape
    N = dy.shape[1]
    n = N // nsh
    tm, tk = min(tm, M), min(tk, K)
    tn = group * n if group > 1 else (tn or min(n, 1024))
    npt = max(n // tn, 1)
    nj, nm = nsh * npt // group, M // tm

    def body(a_ref, dy_ref, out_ref, acc_ref):
        m = pl.program_id(2)
        a_val = _squared(a_ref[...]) if a_squared else a_ref[...]
        part = lax.dot_general(a_val, dy_ref[...], (((0,), (0,)), ((), ())), preferred_element_type=F32)

        @pl.when(m == 0)
        def _():
            acc_ref[...] = part

        @pl.when(m > 0)
        def _():
            acc_ref[...] += part

        @pl.when(m == nm - 1)
        def _():
            if group == 1:
                out_ref[...] = acc_ref[...].astype(out_ref.dtype)
            else:
                for s in range(group):
                    out_ref[s] = acc_ref[:, s * n:(s + 1) * n].astype(out_ref.dtype)

    out_spec = (pl.BlockSpec((None, tk, tn), lambda kk, j, m: (j // npt, kk, j % npt)) if group == 1 else
                pl.BlockSpec((group, tk, n), lambda kk, j, m: (j, kk, 0)))
    return pl.pallas_call(
        body, name=name, grid=(K // tk, nj, nm),
        in_specs=[pl.BlockSpec((tm, tk), lambda kk, j, m: (m, kk)),
                  pl.BlockSpec((tm, tn), lambda kk, j, m: (m, j))],
        out_specs=out_spec,
        out_shape=_sds((nsh, K, n), out_dtype),
        scratch_shapes=[pltpu.VMEM((tk, tn), F32)],
        compiler_params=_params("parallel", "parallel", "arbitrary"))(a, dy)


def _rms_fwd(x, g, *, name, tm=512):
    M, Dm = x.shape
    tm = min(tm, M)

    def body(x_ref, g_ref, o_ref):
        xf = x_ref[...]
        r = lax.rsqrt(jnp.mean(xf * xf, axis=-1, keepdims=True) + EPS)
        o_ref[...] = (xf * r * g_ref[...]).astype(o_ref.dtype)

    return pl.pallas_call(
        body, name=name, grid=(M // tm,),
        in_specs=[pl.BlockSpec((tm, Dm), lambda i: (i, 0)), pl.BlockSpec((1, Dm), lambda i: (0, 0))],
        out_specs=pl.BlockSpec((tm, Dm), lambda i: (i, 0)), out_shape=_sds((M, Dm), BF16),
        compiler_params=_params("parallel"))(x, g)


def _rms_bwd(dxn, x, g, dres, *, name, bf16_copy=True, tm=512):
    M, Dm = x.shape
    tm = min(tm, M)
    has_res = dres is not None

    def body(*refs):
        dxn_ref, x_ref, g_ref = refs[:3]
        dres_ref = refs[3] if has_res else None
        dx_ref, dg_ref = refs[-1 - 1 - bf16_copy], refs[-1]
        dxb_ref = refs[-2] if bf16_copy else None
        i = pl.program_id(0)
        xf = x_ref[...]
        r = lax.rsqrt(jnp.mean(xf * xf, axis=-1, keepdims=True) + EPS)
        nrm = xf * r
        dxn_f = dxn_ref[...].astype(F32)
        dn = dxn_f * g_ref[...]
        dx = r * (dn - nrm * jnp.mean(dn * nrm, axis=-1, keepdims=True))
        if has_res:
            dx = dx + dres_ref[...]
        dx_ref[...] = dx
        if bf16_copy:
            dxb_ref[...] = dx.astype(dxb_ref.dtype)

        @pl.when(i == 0)
        def _():
            dg_ref[...] = jnp.zeros_like(dg_ref)

        dg_ref[0:1, :] += jnp.sum(dxn_f * nrm, axis=0, keepdims=True)

    row = pl.BlockSpec((tm, Dm), lambda i: (i, 0))
    in_specs = [row, row, pl.BlockSpec((1, Dm), lambda i: (0, 0))] + ([row] if has_res else [])
    args = [dxn, x, g] + ([dres] if has_res else [])
    out = pl.pallas_call(
        body, name=name, grid=(M // tm,), in_specs=in_specs,
        out_specs=[row] * (1 + bf16_copy) + [pl.BlockSpec((8, Dm), lambda i: (0, 0))],
        out_shape=[_sds((M, Dm), F32)] + [_sds((M, Dm), BF16)] * bf16_copy + [_sds((8, Dm), F32)],
        compiler_params=_params("arbitrary"))(*args)
    return out[0], (out[1] if bf16_copy else None), out[-1]


def _mm_rows(a, w3, tail, *, name, rows_in=(), vecs_in=(), rows_out=(), sums_out=(), a_squared=False,
             w_transposed=False, tm=512):
    parts = a if isinstance(a, (tuple, list)) else (a,)
    M = parts[0].shape[0]
    K, N = (w3.shape[0] * w3.shape[2], w3.shape[1]) if w_transposed else (w3.shape[1], w3.shape[2])
    tm = min(tm, M)
    n_a, n_ri, n_vi, n_ro = len(parts), len(rows_in), len(vecs_in), len(rows_out)

    def body(*refs):
        a_refs, w_ref, refs = refs[:n_a], refs[n_a], refs[n_a + 1:]
        rin, vin = refs[:n_ri], refs[n_ri:n_ri + n_vi]
        rout, sout = refs[n_ri + n_vi:n_ri + n_vi + n_ro], refs[n_ri + n_vi + n_ro:]

        @pl.when(pl.program_id(0) == 0)
        def _():
            for s in sout:
                s[...] = jnp.zeros_like(s)

        a_val = a_refs[0][...] if n_a == 1 else jnp.concatenate([r[...] for r in a_refs], axis=1)
        if a_squared:
            a_val = _squared(a_val)
        if w_transposed:
            n = w3.shape[2]
            prod = _nt(a_val[:, 0:n], w_ref[0])
            for j in range(1, w3.shape[0]):
                prod = prod + _nt(a_val[:, j * n:(j + 1) * n], w_ref[j])
        else:
            prod = jnp.dot(a_val, w_ref[0], preferred_element_type=F32)
        tail(prod, rin, vin, rout, sout)

    row = pl.BlockSpec((tm, N), lambda i: (i, 0))
    once = lambda shape: pl.BlockSpec(shape, lambda i: (0,) * len(shape))
    return pl.pallas_call(
        body, name=name, grid=(M // tm,),
        in_specs=[pl.BlockSpec((tm, p.shape[1]), lambda i: (i, 0)) for p in parts] + [once(w3.shape)]
        + [row] * n_ri + [once((1, N))] * n_vi,
        out_specs=[row] * n_ro + [once(s) for s in sums_out],
        out_shape=[_sds((M, N), dt) for dt in rows_out] + [_sds(s, F32) for s in sums_out],
        compiler_params=_params("arbitrary"))(*parts, w3, *rows_in, *vecs_in)


def _residual_norm_tail(prod, rows_in, vecs_in, rows_out, sums_out):
    hf = prod + rows_in[0][...]
    rows_out[0][...] = hf
    r = lax.rsqrt(jnp.mean(hf * hf, axis=-1, keepdims=True) + EPS)
    rows_out[1][...] = (hf * r * vecs_in[0][...]).astype(BF16)


def _rms_bwd_tail(bf16_copy):
    def tail(dxn, rows_in, vecs_in, rows_out, sums_out):
        xf = rows_in[0][...]
        r = lax.rsqrt(jnp.mean(xf * xf, axis=-1, keepdims=True) + EPS)
        nrm = xf * r
        dn = dxn * vecs_in[0][...]
        dx = r * (dn - nrm * jnp.mean(dn * nrm, axis=-1, keepdims=True)) + rows_in[1][...]
        rows_out[0][...] = dx
        if bf16_copy:
            rows_out[1][...] = dx.astype(BF16)
        sums_out[0][0:1, :] += jnp.sum(dxn * nrm, axis=0, keepdims=True)

    return tail


def _loss_tail(prod, rows_in, vecs_in, rows_out, sums_out):
    hf = prod + rows_in[0][...]
    r = lax.rsqrt(jnp.mean(hf * hf, axis=-1, keepdims=True) + EPS)
    nrm = hf * r
    gv = vecs_in[0][...]
    err = nrm * gv - rows_in[1][...]
    dy = err * (1.0 / hf.shape[-1])
    dn = dy * gv
    dh = r * (dn - nrm * jnp.mean(dn * nrm, axis=-1, keepdims=True))
    rows_out[0][...] = dh
    rows_out[1][...] = dh.astype(BF16)
    sums_out[0][0:1, :] += jnp.sum(dy * nrm, axis=0, keepdims=True)
    part = 0.5 * jnp.sum(jnp.mean(err * err, axis=-1, keepdims=True), axis=0, keepdims=True)
    sel = (lax.broadcasted_iota(jnp.int32, (8, 128), 0) == 0) & (lax.broadcasted_iota(jnp.int32, (8, 128), 1) == 0)
    sums_out[1][...] += jnp.where(sel, part, 0.0)


def _class_spec(tm, d, width):
    return pl.BlockSpec((d, tm // d, width), lambda i: (0, i, 0))


def _row_scratch(tm, width):
    return pltpu.VMEM((width // LANES, tm, LANES), F32)


def _fill(scr, val):
    for c in range(scr.shape[0]):
        scr[c] = val[:, c * LANES:(c + 1) * LANES]


def _to_classes(scr, out_ref, d):
    n = scr.shape[1] // d
    for r in range(d):
        for c in range(scr.shape[0]):
            out_ref[r, :, c * LANES:(c + 1) * LANES] = scr[c, pl.ds(r, n, stride=d), :].astype(out_ref.dtype)


def _from_classes(in_ref, scr, d):
    n = scr.shape[1] // d
    for r in range(d):
        blk = in_ref[r].astype(F32)
        for c in range(scr.shape[0]):
            scr[c, pl.ds(r, n, stride=d), :] = blk[:, c * LANES:(c + 1) * LANES]
    return jnp.concatenate([scr[c] for c in range(scr.shape[0])], axis=1)


def _rope_tables(S):
    half = ROT_DIM // 2
    freqs = ROPE_THETA ** (-jnp.arange(0, ROT_DIM, 2, dtype=F32) / ROT_DIM)
    ang = jnp.arange(S, dtype=F32)[:, None] * freqs[None, :]
    cos, sin = jnp.cos(ang), jnp.sin(ang)
    ones = jnp.ones((S, HEAD_DIM - ROT_DIM), F32)
    zeros = jnp.zeros((S, HEAD_DIM - ROT_DIM), F32)
    zh = jnp.zeros((S, half), F32)
    c = jnp.concatenate([cos, cos, ones], axis=1)
    sa = jnp.concatenate([-sin, zh, zeros], axis=1)
    sb = jnp.concatenate([zh, sin, zeros], axis=1)
    return tuple(jnp.tile(t, (1, LANES // HEAD_DIM)) for t in (c, sa, sb))


def _rope_fwd(y, tables, *, tm=512):
    S = y.shape[0]
    W = 2 * D_ATT
    tm = min(tm, S)
    half = ROT_DIM // 2
    dils = [d for d in DILATIONS if d > 1]

    def body(y_ref, c_ref, sa_ref, sb_ref, qk_ref, *rest):
        qk_outs, v_outs = rest[:len(dils)], rest[len(dils):2 * len(dils)]
        scr_qk, scr_v = rest[2 * len(dils):]
        t = y_ref[:, 0:W].astype(F32)
        rep = W // LANES
        c, sa, sb = (jnp.tile(r[...], (1, rep)) for r in (c_ref, sa_ref, sb_ref))
        rot = t * c + pltpu.roll(t, W - half, axis=1) * sa + pltpu.roll(t, half, axis=1) * sb
        qk_ref[...] = rot.astype(qk_ref.dtype)
        _fill(scr_qk, rot)
        _fill(scr_v, y_ref[:, W:W + D_ATT].astype(F32))
        for d, qo, vo in zip(dils, qk_outs, v_outs):
            _to_classes(scr_qk, qo, d)
            _to_classes(scr_v, vo, d)

    tab = pl.BlockSpec((tm, LANES), lambda i: (i, 0))
    out = pl.pallas_call(
        body, name="rope_fwd", grid=(S // tm,),
        in_specs=[pl.BlockSpec((tm, 3 * D_ATT), lambda i: (i, 0)), tab, tab, tab],
        out_specs=[pl.BlockSpec((tm, W), lambda i: (i, 0))] + [_class_spec(tm, d, W) for d in dils]
        + [_class_spec(tm, d, D_ATT) for d in dils],
        out_shape=[_sds((S, W), BF16)] + [_sds((d, S // d, W), BF16) for d in dils]
        + [_sds((d, S // d, D_ATT), BF16) for d in dils],
        scratch_shapes=[_row_scratch(tm, W), _row_scratch(tm, D_ATT)],
        compiler_params=_params("parallel"))(y, *tables)
    qk = [out[0]] + [o.reshape(S, W) for o in out[1:1 + len(dils)]]
    v = [None] + [o.reshape(S, D_ATT) for o in out[1 + len(dils):]]
    return qk, v


def _assemble_dy(dq, dk, dv, dag, tables, *, tm=512):
    S = dag.shape[0]
    tm = min(tm, S)
    half = ROT_DIM // 2
    W = D_ATT
    n_pat = len(DILATIONS)

    def body(*refs):
        groups = [refs[g * n_pat:(g + 1) * n_pat] for g in range(3)]
        dag_ref, c_ref, sa_ref, sb_ref, o_ref, scr = refs[3 * n_pat:]
        rep = W // LANES
        c, sa, sb = (jnp.tile(r[...], (1, rep)) for r in (c_ref, sa_ref, sb_ref))

        def total(rs):
            acc = rs[0][...].astype(F32)
            for d, r in zip(DILATIONS[1:], rs[1:]):
                acc = acc + _from_classes(r, scr, d)
            return acc

        def unrope(dr):
            return dr * c + pltpu.roll(dr * sa, half, axis=1) + pltpu.roll(dr * sb, W - half, axis=1)

        o_ref[:, 0:W] = unrope(total(groups[0])).astype(o_ref.dtype)
        o_ref[:, W:2 * W] = unrope(total(groups[1])).astype(o_ref.dtype)
        o_ref[:, 2 * W:3 * W] = total(groups[2]).astype(o_ref.dtype)
        o_ref[:, 3 * W:] = dag_ref[...]

    specs = [pl.BlockSpec((tm, W), lambda i: (i, 0))] + [_class_spec(tm, d, W) for d in DILATIONS[1:]]
    tab = pl.BlockSpec((tm, LANES), lambda i: (i, 0))
    args = [a if d == 1 else a.reshape(d, S // d, W) for grp in (dq, dk, dv) for d, a in zip(DILATIONS, grp)]
    return pl.pallas_call(
        body, name="assemble_dy", grid=(S // tm,),
        in_specs=specs * 3 + [pl.BlockSpec((tm, 2 * D_CONV), lambda i: (i, 0)), tab, tab, tab],
        out_specs=pl.BlockSpec((tm, D_IN), lambda i: (i, 0)), out_shape=_sds((S, D_IN), BF16),
        scratch_shapes=[_row_scratch(tm, W)],
        compiler_params=_params("parallel"))(*args, dag, *tables)


def _seq_specs(L, tb, col):
    nb, per, nh = L // tb, tb // HALF, L // HALF
    centre = pl.BlockSpec((tb, D_ATT), lambda r, i: (r * nb + i, col))
    prev = pl.BlockSpec((HALF, D_ATT), lambda r, i: (r * nh + jnp.maximum(i * per - 1, 0), col))
    nxt = pl.BlockSpec((HALF, D_ATT), lambda r, i: (r * nh + jnp.minimum((i + 1) * per, nh - 1), col))
    return prev, centre, nxt


def _band_mask(i, tq, L):
    shape = (tq, tq + 2 * HALF)
    c_idx = lax.broadcasted_iota(jnp.int32, shape, 0)
    w_idx = lax.broadcasted_iota(jnp.int32, shape, 1)
    diff = w_idx - c_idx
    wpos = i * tq - HALF + w_idx
    return (diff >= 0) & (diff <= 2 * HALF) & (wpos >= 0) & (wpos < L)


def _lane_groups():
    for c0 in range(0, D_ATT, LANES):
        yield slice(c0, c0 + LANES)


def _first_head(rows):
    return lax.broadcasted_iota(jnp.int32, (rows, LANES), 1) < HEAD_DIM


def _split_pair(x, first):
    zero = jnp.zeros_like(x)
    return jnp.where(first, x, zero), jnp.where(first, zero, x)


def _nt(a, b):
    return lax.dot_general(a, b, (((1,), (1,)), ((), ())), preferred_element_type=F32)


def _tn(a, b):
    return lax.dot_general(a, b, (((0,), (0,)), ((), ())), preferred_element_type=F32)


ATT_SCALE = HEAD_DIM ** -0.5


def _att_fwd(qk, v_src, d, *, name):
    S = qk.shape[0]
    L = S // d
    tq = min(ATT_BLOCK, L)
    v_arr, v_col = v_src

    def body(q_ref, kp_ref, kc_ref, kn_ref, vp_ref, vc_ref, vn_ref, o_ref, lse_ref):
        i = pl.program_id(1)
        valid = _band_mask(i, tq, L)
        q = q_ref[...] * ATT_SCALE
        kwin = jnp.concatenate([kp_ref[...], kc_ref[...], kn_ref[...]], axis=0)
        vwin = jnp.concatenate([vp_ref[...], vc_ref[...], vn_ref[...]], axis=0)
        first = _first_head(tq)
        groups = list(_lane_groups())
        heads = [(ls, t) for ls in groups for t in _split_pair(q[:, ls], first)]
        s = [jnp.where(valid, _nt(t, kwin[:, ls]), NEG_INF) for ls, t in heads]
        m = [jnp.max(t, axis=-1, keepdims=True) for t in s]
        p = [jnp.exp(t - mm) for t, mm in zip(s, m)]
        den = [jnp.sum(t, axis=-1, keepdims=True) for t in p]
        o = [jnp.dot(t.astype(BF16), vwin[:, ls], preferred_element_type=F32) * (1.0 / dd)
             for t, dd, (ls, _) in zip(p, den, heads)]
        lse = [mm + jnp.log(dd) for mm, dd in zip(m, den)]
        for g, ls in enumerate(groups):
            o_ref[:, ls] = jnp.where(first, o[2 * g], o[2 * g + 1]).astype(o_ref.dtype)
            lse_ref[:, ls] = jnp.where(first, lse[2 * g], lse[2 * g + 1])

    _, qc, _ = _seq_specs(L, tq, 0)
    kp, kc, kn = _seq_specs(L, tq, 1)
    vp, vc, vn = _seq_specs(L, tq, v_col)
    out = pl.BlockSpec((tq, D_ATT), lambda r, i: (r * (L // tq) + i, 0))
    return pl.pallas_call(
        body, name=name, grid=(d, L // tq),
        in_specs=[qc, kp, kc, kn, vp, vc, vn], out_specs=[out, out],
        out_shape=[_sds((S, D_ATT), BF16), _sds((S, D_ATT), F32)],
        compiler_params=_params("parallel", "parallel"))(qk, qk, qk, qk, v_arr, v_arr, v_arr)


def _att_combine(outs, lses, *, tm=512):
    S = outs[0].shape[0]
    tm = min(tm, S)
    dils = DILATIONS[1:]
    n_d = len(dils)

    def body(*refs):
        o_refs, l_refs = refs[0:1 + n_d], refs[1 + n_d:2 + 2 * n_d]
        att_ref, lg_ref = refs[2 + 2 * n_d:4 + 2 * n_d]
        lg_outs = refs[4 + 2 * n_d:4 + 3 * n_d]
        scr = refs[4 + 3 * n_d:]
        scr_o, scr_l, scr_lg = scr[:n_d], scr[n_d:2 * n_d], scr[2 * n_d]
        ls = [l_refs[0][...]] + [_from_classes(r, s, d) for r, s, d in zip(l_refs[1:], scr_l, dils)]
        os_ = [o_refs[0][...].astype(F32)] + [_from_classes(r, s, d) for r, s, d in zip(o_refs[1:], scr_o, dils)]
        mx = ls[0]
        for l in ls[1:]:
            mx = jnp.maximum(mx, l)
        es = [jnp.exp(l - mx) for l in ls]
        tot = es[0]
        num = es[0] * os_[0]
        for e, o in zip(es[1:], os_[1:]):
            tot = tot + e
            num = num + e * o
        att_ref[...] = (num / tot).astype(att_ref.dtype)
        lg = mx + jnp.log(tot)
        lg_ref[...] = lg
        _fill(scr_lg, lg)
        for d, out in zip(dils, lg_outs):
            _to_classes(scr_lg, out, d)

    nat = pl.BlockSpec((tm, D_ATT), lambda i: (i, 0))
    specs = [nat] + [_class_spec(tm, d, D_ATT) for d in dils]
    view = lambda arrs: [arrs[0]] + [a.reshape(d, S // d, D_ATT) for a, d in zip(arrs[1:], dils)]
    out = pl.pallas_call(
        body, name="att_combine", grid=(S // tm,), in_specs=specs * 2,
        out_specs=[nat, nat] + specs[1:],
        out_shape=[_sds((S, D_ATT), BF16), _sds((S, D_ATT), F32)] + [_sds((d, S // d, D_ATT), F32) for d in dils],
        scratch_shapes=[_row_scratch(tm, D_ATT)] * (2 * n_d + 1),
        compiler_params=_params("parallel"))(*view(list(outs)), *view(list(lses)))
    return out[0], [out[1]] + [o.reshape(S, D_ATT) for o in out[2:]]


def _att_delta(dac, att, *, tm=512):
    S = att.shape[0]
    tm = min(tm, S)
    dils = DILATIONS[1:]
    n_d = len(dils)

    def body(do_ref, o_ref, dl_ref, *rest):
        dl_outs, do_outs = rest[:n_d], rest[n_d:2 * n_d]
        scr_dl, scr_do = rest[2 * n_d:]
        do = do_ref[...].astype(F32)
        prod = do * o_ref[...].astype(F32)
        per_head = [jnp.broadcast_to(jnp.sum(prod[:, h * HEAD_DIM:(h + 1) * HEAD_DIM], axis=-1, keepdims=True),
                                     (tm, HEAD_DIM)) for h in range(ATT_HEADS)]
        dl = jnp.concatenate(per_head, axis=1)
        dl_ref[...] = dl
        _fill(scr_dl, dl)
        _fill(scr_do, do)
        for d, dlo, doo in zip(dils, dl_outs, do_outs):
            _to_classes(scr_dl, dlo, d)
            _to_classes(scr_do, doo, d)

    blk = pl.BlockSpec((tm, D_ATT), lambda i: (i, 0))
    out = pl.pallas_call(
        body, name="att_delta", grid=(S // tm,), in_specs=[blk, blk],
        out_specs=[blk] + [_class_spec(tm, d, D_ATT) for d in dils] * 2,
        out_shape=[_sds((S, D_ATT), F32)] + [_sds((d, S // d, D_ATT), F32) for d in dils]
        + [_sds((d, S // d, D_ATT), BF16) for d in dils],
        scratch_shapes=[_row_scratch(tm, D_ATT), _row_scratch(tm, D_ATT)],
        compiler_params=_params("parallel"))(dac, att)
    delta = [out[0]] + [o.reshape(S, D_ATT) for o in out[1:1 + n_d]]
    do = [None] + [o.reshape(S, D_ATT) for o in out[1 + n_d:]]
    return delta, do


def _att_bwd(qk, v_src, do_src, lg, delta, d, *, name):
    S = qk.shape[0]
    L = S // d
    tq = min(ATT_BLOCK, L)
    nb, per, nh = L // tq, tq // HALF, L // HALF
    n_blocks = d * nb
    win = tq + 2 * HALF
    lead = tq - HALF
    acc_rows = lead + win
    (v_arr, v_col), (do_arr, do_col) = v_src, do_src

    def body(q_ref, kp_ref, kc_ref, kn_ref, vp_ref, vc_ref, vn_ref, do_ref, lg_ref, dl_ref,
             dq_ref, dk_ref, dv_ref, acc_k, acc_v):
        b = pl.program_id(0)
        i = lax.rem(jnp.minimum(b, n_blocks - 1), nb)

        @pl.when(b == 0)
        def _():
            acc_k[...] = jnp.zeros_like(acc_k)
            acc_v[...] = jnp.zeros_like(acc_v)

        @pl.when(b < n_blocks)
        def _():
            valid = _band_mask(i, tq, L)
            q, do = q_ref[...] * ATT_SCALE, do_ref[...]
            kwin = jnp.concatenate([kp_ref[...], kc_ref[...], kn_ref[...]], axis=0)
            vwin = jnp.concatenate([vp_ref[...], vc_ref[...], vn_ref[...]], axis=0)
            first, first_w = _first_head(tq), _first_head(win)
            groups = list(_lane_groups())
            cols = [c for ls in groups for c in (ls.start, ls.start + HEAD_DIM)]
            lanes = [ls for ls in groups for _ in range(2)]
            qh = [t for ls in groups for t in _split_pair(q[:, ls], first)]
            doh = [t for ls in groups for t in _split_pair(do[:, ls], first)]
            s = [jnp.where(valid, _nt(t, kwin[:, ls]), NEG_INF) for t, ls in zip(qh, lanes)]
            dp = [_nt(t, vwin[:, ls]) for t, ls in zip(doh, lanes)]
            p = [jnp.exp(t - lg_ref[:, c:c + 1]) for t, c in zip(s, cols)]
            ds = [(pp * (t - dl_ref[:, c:c + 1])).astype(BF16) for pp, t, c in zip(p, dp, cols)]
            dq = [jnp.dot(t, kwin[:, ls], preferred_element_type=F32) for t, ls in zip(ds, lanes)]
            dk = [_tn(t, q[:, ls]) for t, ls in zip(ds, lanes)]
            dv = [_tn(pp.astype(BF16), do[:, ls]) for pp, ls in zip(p, lanes)]
            for g, ls in enumerate(groups):
                dq_ref[:, ls] = (jnp.where(first, dq[2 * g], dq[2 * g + 1]) * ATT_SCALE).astype(dq_ref.dtype)
                acc_k[lead:, ls] += jnp.where(first_w, dk[2 * g], dk[2 * g + 1])
                acc_v[lead:, ls] += jnp.where(first_w, dv[2 * g], dv[2 * g + 1])

        for acc, out in ((acc_k, dk_ref), (acc_v, dv_ref)):
            out[...] = acc[0:tq, :].astype(out.dtype)
            kept = acc[tq:, :]
            acc[0:acc_rows - tq, :] = kept
            acc[acc_rows - tq:, :] = jnp.zeros((tq, D_ATT), F32)

    def seq(col):
        blk = lambda b: jnp.minimum(b, n_blocks - 1)
        cls = lambda b: (blk(b) // nb) * nh
        centre = pl.BlockSpec((tq, D_ATT), lambda b: (blk(b), col))
        prev = pl.BlockSpec((HALF, D_ATT), lambda b: (cls(b) + jnp.maximum((blk(b) % nb) * per - 1, 0), col))
        nxt = pl.BlockSpec((HALF, D_ATT), lambda b: (cls(b) + jnp.minimum((blk(b) % nb + 1) * per, nh - 1), col))
        return prev, centre, nxt

    _, qc, _ = seq(0)
    kp, kc, kn = seq(1)
    vp, vc, vn = seq(v_col)
    _, doc, _ = seq(do_col)
    late = pl.BlockSpec((tq, D_ATT), lambda b: (jnp.maximum(b - 1, 0), 0))
    return pl.pallas_call(
        body, name=name, grid=(n_blocks + 1,),
        in_specs=[qc, kp, kc, kn, vp, vc, vn, doc, qc, qc], out_specs=[qc, late, late],
        out_shape=[_sds((S, D_ATT), BF16)] * 3,
        scratch_shapes=[pltpu.VMEM((acc_rows, D_ATT), F32), pltpu.VMEM((acc_rows, D_ATT), F32)],
        compiler_params=_params("arbitrary"))(qk, qk, qk, qk, v_arr, v_arr, v_arr, do_arr, lg, delta)


def _sigmoid(x):
    return 1.0 / (1.0 + jnp.exp(-x))


def _halo_specs(S, T, width, col):
    last = S // HALO - 1
    per = T // HALO
    centre = pl.BlockSpec((T, width), lambda i: (i, col))
    prev = pl.BlockSpec((HALO, width), lambda i: (jnp.maximum(i * per - 1, 0), col))
    nxt = pl.BlockSpec((HALO, width), lambda i: (jnp.minimum((i + 1) * per, last), col))
    return prev, centre, nxt


def _window_scratch(T, C):
    return pltpu.VMEM((8, T + 2 * HALO, C), F32)


def _fill_window(buf, prev, centre, nxt, T):
    buf[0, 0:HALO, :] = prev
    buf[0, HALO:HALO + T, :] = centre
    buf[0, HALO + T:, :] = nxt
    rows = T + 2 * HALO - 8
    for s in range(1, 8):
        buf[s, 0:rows, :] = buf[0, s:s + rows, :]


def _tap_reads(buf, first_off, step, r0, ls):
    by_slab = {}
    for k in range(CONV_WIDTH):
        off = first_off + step * k
        by_slab.setdefault(off % 8, []).append((k, off - off % 8))
    for s, taps in by_slab.items():
        lo = min(a for _, a in taps)
        hi = max(a for _, a in taps)
        rows = buf[s, pl.ds(lo + r0, CONV_ROWS + hi - lo), ls]
        for k, a in taps:
            yield k, rows[a - lo:a - lo + CONV_ROWS]


def _depthwise(buf, w_ref, out_ref, T, C, first_off, step):
    def row_tile(t, carry):
        r0 = pl.multiple_of(t * CONV_ROWS, CONV_ROWS)
        for c0 in range(0, C, LANES):
            ls = slice(c0, c0 + LANES)
            acc = jnp.zeros((CONV_ROWS, LANES), F32)
            for k, rows in _tap_reads(buf, first_off, step, r0, ls):
                acc = acc + rows * w_ref[k:k + 1, ls]
            out_ref[pl.ds(r0, CONV_ROWS), ls] = acc
        return carry

    lax.fori_loop(0, T // CONV_ROWS, row_tile, 0)


def _conv_fwd(y, conv_w32, conv_b, ln_g, ln_b, *, T=512):
    S = y.shape[0]
    T = min(T, S)
    nblk = S // T
    C = D_CONV

    def body(ap, ac, an, gp, gc, gn, w_ref, b_ref, lg_ref, lb_ref, cv_ref, u1_ref, buf):
        i = pl.program_id(0)

        def glu(a_ref, g_ref):
            return a_ref[...].astype(F32) * _sigmoid(g_ref[...].astype(F32))

        _fill_window(buf, jnp.where(i > 0, glu(ap, gp), 0.0), glu(ac, gc),
                     jnp.where(i < nblk - 1, glu(an, gn), 0.0), T)
        _depthwise(buf, w_ref, u1_ref, T, C, HALO - CONV_PAD, 1)
        u1 = u1_ref[...] + b_ref[...]
        u1_ref[...] = u1
        mu = jnp.mean(u1, axis=-1, keepdims=True)
        xc = u1 - mu
        rstd = lax.rsqrt(jnp.mean(xc * xc, axis=-1, keepdims=True) + EPS)
        u2 = xc * rstd * lg_ref[...] + lb_ref[...]
        cv_ref[...] = (u2 * _sigmoid(u2)).astype(cv_ref.dtype)

    ap, ac, an = _halo_specs(S, T, C, 3)
    gp, gc, gn = _halo_specs(S, T, C, 4)
    vec = pl.BlockSpec((1, C), lambda i: (0, 0))
    out = pl.BlockSpec((T, C), lambda i: (i, 0))
    return pl.pallas_call(
        body, name="conv_fwd", grid=(nblk,),
        in_specs=[ap, ac, an, gp, gc, gn, pl.BlockSpec((32, C), lambda i: (0, 0)), vec, vec, vec],
        out_specs=[out, out], out_shape=[_sds((S, C), BF16), _sds((S, C), F32)],
        scratch_shapes=[_window_scratch(T, C)],
        compiler_params=_params("parallel"))(y, y, y, y, y, y, conv_w32, conv_b, ln_g, ln_b)


def _conv_bwd(dac, u1, y, conv_w32, ln_g, ln_b, *, T=512):
    S = y.shape[0]
    T = min(T, S)
    nblk = S // T
    C = D_CONV

    def body(dp, dc, dn, up, uc, un, ap, ac, an, gp, gc, gn, w_ref, lg_ref, lb_ref,
             dag_ref, dw_ref, dsm_ref, bufd, bufu, du0_scr, dw_acc):
        i = pl.program_id(0)
        lg = lg_ref[...]

        def du1_of(dcv_ref, u1_ref):
            u1 = u1_ref[...]
            mu = jnp.mean(u1, axis=-1, keepdims=True)
            xc = u1 - mu
            rstd = lax.rsqrt(jnp.mean(xc * xc, axis=-1, keepdims=True) + EPS)
            xhat = xc * rstd
            u2 = xhat * lg + lb_ref[...]
            sg = _sigmoid(u2)
            du2 = dcv_ref[...].astype(F32) * (sg * (1.0 + u2 * (1.0 - sg)))
            dxh = du2 * lg
            du1 = rstd * (dxh - jnp.mean(dxh, axis=-1, keepdims=True)
                          - xhat * jnp.mean(dxh * xhat, axis=-1, keepdims=True))
            return du1, du2, xhat

        def glu(a_ref, g_ref):
            return a_ref[...].astype(F32) * _sigmoid(g_ref[...].astype(F32))

        @pl.when(i == 0)
        def _():
            dw_ref[...] = jnp.zeros_like(dw_ref)
            dsm_ref[...] = jnp.zeros_like(dsm_ref)

        du1_c, du2_c, xhat_c = du1_of(dc, uc)
        dsm_ref[0:1, :] += jnp.sum(du1_c, axis=0, keepdims=True)
        dsm_ref[1:2, :] += jnp.sum(du2_c * xhat_c, axis=0, keepdims=True)
        dsm_ref[2:3, :] += jnp.sum(du2_c, axis=0, keepdims=True)
        _fill_window(bufd, jnp.where(i > 0, du1_of(dp, up)[0], 0.0), du1_c,
                     jnp.where(i < nblk - 1, du1_of(dn, un)[0], 0.0), T)
        _fill_window(bufu, jnp.where(i > 0, glu(ap, gp), 0.0), glu(ac, gc),
                     jnp.where(i < nblk - 1, glu(an, gn), 0.0), T)

        _depthwise(bufd, w_ref, du0_scr, T, C, HALO + CONV_PAD, -1)
        dw_acc[...] = jnp.zeros_like(dw_acc)

        def dw_tile(t, carry):
            r0 = pl.multiple_of(t * CONV_ROWS, CONV_ROWS)
            for c0 in range(0, C, LANES):
                ls = slice(c0, c0 + LANES)
                d = bufd[0, pl.ds(HALO + r0, CONV_ROWS), ls]
                for k, rows in _tap_reads(bufu, HALO - CONV_PAD, 1, r0, ls):
                    prod = d * rows
                    part = prod[0:8]
                    for j in range(8, CONV_ROWS, 8):
                        part = part + prod[j:j + 8]
                    dw_acc[k, :, ls] += part
            return carry

        lax.fori_loop(0, T // CONV_ROWS, dw_tile, 0)
        for k in range(CONV_WIDTH):
            dw_ref[k:k + 1, :] += jnp.sum(dw_acc[k], axis=0, keepdims=True)
        du0 = du0_scr[...]
        a = ac[...].astype(F32)
        sg = _sigmoid(gc[...].astype(F32))
        dag_ref[:, 0:C] = (du0 * sg).astype(dag_ref.dtype)
        dag_ref[:, C:] = (du0 * a * sg * (1.0 - sg)).astype(dag_ref.dtype)

    dp, dc, dn = _halo_specs(S, T, C, 1)
    up, uc, un = _halo_specs(S, T, C, 0)
    ap, ac, an = _halo_specs(S, T, C, 3)
    gp, gc, gn = _halo_specs(S, T, C, 4)
    vec = pl.BlockSpec((1, C), lambda i: (0, 0))
    return pl.pallas_call(
        body, name="conv_bwd", grid=(nblk,),
        in_specs=[dp, dc, dn, up, uc, un, ap, ac, an, gp, gc, gn,
                  pl.BlockSpec((32, C), lambda i: (0, 0)), vec, vec],
        out_specs=[pl.BlockSpec((T, 2 * C), lambda i: (i, 0)), pl.BlockSpec((32, C), lambda i: (0, 0)),
                   pl.BlockSpec((8, C), lambda i: (0, 0))],
        out_shape=[_sds((S, 2 * C), BF16), _sds((32, C), F32), _sds((8, C), F32)],
        scratch_shapes=[_window_scratch(T, C), _window_scratch(T, C), pltpu.VMEM((T, C), F32),
                        pltpu.VMEM((CONV_WIDTH, 8, C), F32)],
        compiler_params=_params("arbitrary"))(dac, dac, dac, u1, u1, u1, y, y, y, y, y, y, conv_w32, ln_g, ln_b)


def _xatt_fwd(xq, xk, xv, *, tm=512):
    S = xq.shape[0]
    M = xk.shape[0]
    tm = min(tm, S)
    scale = XATT_HEAD_DIM ** -0.5

    def body(q_ref, k_ref, v_ref, o_ref):
        heads = [slice(h * XATT_HEAD_DIM, (h + 1) * XATT_HEAD_DIM) for h in range(XATT_HEADS)]
        s = [_nt(q_ref[:, sl], k_ref[:, sl]) * scale for sl in heads]
        e = [jnp.exp(t - jnp.max(t, axis=-1, keepdims=True)) for t in s]
        p = [t * (1.0 / jnp.sum(t, axis=-1, keepdims=True)) for t in e]
        for sl, t in zip(heads, p):
            o_ref[:, sl] = jnp.dot(t.astype(BF16), v_ref[:, sl], preferred_element_type=F32).astype(o_ref.dtype)

    row = pl.BlockSpec((tm, D_MODEL), lambda i: (i, 0))
    full = pl.BlockSpec((M, D_MODEL), lambda i: (0, 0))
    return pl.pallas_call(
        body, name="xatt_fwd", grid=(S // tm,), in_specs=[row, full, full], out_specs=row,
        out_shape=_sds((S, D_MODEL), BF16), compiler_params=_params("parallel"))(xq, xk, xv)


def _xatt_bwd(xq, xk, xv, dxo, *, tm=512):
    S = xq.shape[0]
    M = xk.shape[0]
    tm = min(tm, S)
    scale = XATT_HEAD_DIM ** -0.5

    def body(q_ref, k_ref, v_ref, do_ref, dq_ref, dk_ref, dv_ref):
        i = pl.program_id(0)

        @pl.when(i == 0)
        def _():
            dk_ref[...] = jnp.zeros_like(dk_ref)
            dv_ref[...] = jnp.zeros_like(dv_ref)

        heads = [slice(h * XATT_HEAD_DIM, (h + 1) * XATT_HEAD_DIM) for h in range(XATT_HEADS)]
        s = [_nt(q_ref[:, sl], k_ref[:, sl]) * scale for sl in heads]
        dp = [_nt(do_ref[:, sl], v_ref[:, sl]) for sl in heads]
        e = [jnp.exp(t - jnp.max(t, axis=-1, keepdims=True)) for t in s]
        p = [t * (1.0 / jnp.sum(t, axis=-1, keepdims=True)) for t in e]
        ds = [(pp * (t - jnp.sum(t * pp, axis=-1, keepdims=True))).astype(BF16) for pp, t in zip(p, dp)]
        for sl, pp, t in zip(heads, p, ds):
            dq_ref[:, sl] = (jnp.dot(t, k_ref[:, sl], preferred_element_type=F32) * scale).astype(dq_ref.dtype)
            dv_ref[:, sl] += _tn(pp.astype(BF16), do_ref[:, sl])
            dk_ref[:, sl] += _tn(t, q_ref[:, sl]) * scale

    row = pl.BlockSpec((tm, D_MODEL), lambda i: (i, 0))
    full = pl.BlockSpec((M, D_MODEL), lambda i: (0, 0))
    return pl.pallas_call(
        body, name="xatt_bwd", grid=(S // tm,), in_specs=[row, full, full, row], out_specs=[row, full, full],
        out_shape=[_sds((S, D_MODEL), BF16), _sds((M, D_MODEL), F32), _sds((M, D_MODEL), F32)],
        compiler_params=_params("arbitrary"))(xq, xk, xv, dxo)


def _row_tile(R):
    for t in (256, 128, 64, 32, 16, 8):
        if R % t == 0:
            return t
    return R


def _sum_partials(own, recv, me, *, name):
    _, R, C = own.shape
    t = _row_tile(R)

    def body(me_ref, own_ref, r_ref, o_ref):
        o_ref[...] = ((own_ref[...].astype(F32) + r_ref[0].astype(F32)) + r_ref[1].astype(F32)) + r_ref[2].astype(F32)

    return pl.pallas_call(
        body, name=name,
        grid_spec=pltpu.PrefetchScalarGridSpec(
            num_scalar_prefetch=1, grid=(R // t,),
            in_specs=[pl.BlockSpec((None, t, C), lambda i, me_ref: (me_ref[0], i, 0)),
                      pl.BlockSpec((3, t, C), lambda i, me_ref: (0, i, 0))],
            out_specs=pl.BlockSpec((t, C), lambda i, me_ref: (i, 0))),
        out_shape=_sds((R, C), F32), compiler_params=_params("parallel"))(me, own, recv)


def _adamw_math(w, g, m, v):
    m2 = ADAM_B1 * m + (1.0 - ADAM_B1) * g
    v2 = ADAM_B2 * v + (1.0 - ADAM_B2) * (g * g)
    m_hat = m2 / (1.0 - ADAM_B1 ** ADAM_STEP)
    v_hat = v2 / (1.0 - ADAM_B2 ** ADAM_STEP)
    delta = -ADAM_LR * (m_hat / (jnp.sqrt(v_hat) + ADAM_EPS) + ADAM_WD * w)
    return delta, m2, v2


def _adamw(parts, w, m, v, *, name):
    R, C = w.shape
    t = _row_tile(R)
    n = len(parts)

    def body(*refs):
        w_ref, m_ref, v_ref = refs[n:n + 3]
        g_ref, d_ref, m2_ref, v2_ref = refs[n + 3:]
        g = refs[0][...]
        for r in refs[1:n]:
            g = g + r[...]
        delta, m2, v2 = _adamw_math(w_ref[...], g, m_ref[...], v_ref[...])
        g_ref[...] = g
        d_ref[...] = delta
        m2_ref[...] = m2
        v2_ref[...] = v2

    blk = pl.BlockSpec((t, C), lambda i: (i, 0))
    return pl.pallas_call(
        body, name=name, grid=(R // t,), in_specs=[blk] * (n + 3), out_specs=[blk] * 4,
        out_shape=[_sds((R, C), F32)] * 4, compiler_params=_params("parallel"))(*parts, w, m, v)


def _adamw_small(gathered, chip, entries):
    _, R, C = gathered.shape
    n = len(entries)
    group = D_CONV // N_CHIPS

    def body(chip_ref, g_ref, *refs):
        ins, outs, tot_ref = refs[:3 * n], refs[3 * n:7 * n], refs[7 * n]
        tot = g_ref[0]
        for k in range(1, N_DEV):
            tot = tot + g_ref[k]
        tot_ref[...] = tot
        for e, ((kind, r), _, _, _) in enumerate(entries):
            if kind == "row":
                g = tot_ref[r:r + 1, :]
            elif kind == "gain":
                g = jnp.concatenate([tot_ref[r:r + 1, :], tot_ref[r + 1:r + 2, :]], axis=1)
            else:
                g = tot_ref[r:r + CONV_WIDTH, 0:group]
                for j in range(1, N_CHIPS):
                    g = jnp.where(chip_ref[0] == j, tot_ref[r:r + CONV_WIDTH, j * group:(j + 1) * group], g)
            delta, m2, v2 = _adamw_math(ins[3 * e][...], g, ins[3 * e + 1][...], ins[3 * e + 2][...])
            for o, val in zip(outs[4 * e:4 * e + 4], (g, delta, m2, v2)):
                o[...] = val

    whole = lambda a: pl.BlockSpec(a.shape, lambda i, c: (0,) * a.ndim)
    arrays = [a for _, w, m, v in entries for a in (w, m, v)]
    out_like = [w for _, w, _, _ in entries for _ in range(4)]
    tot_like = _sds((R, C), F32)
    out = pl.pallas_call(
        body, name="adamw_small",
        grid_spec=pltpu.PrefetchScalarGridSpec(
            num_scalar_prefetch=1, grid=(1,),
            in_specs=[whole(gathered)] + [whole(a) for a in arrays],
            out_specs=[whole(a) for a in out_like] + [whole(tot_like)]),
        out_shape=[_sds(a.shape, F32) for a in out_like] + [tot_like],
        compiler_params=_params("arbitrary"))(chip, gathered, *arrays)
    return out[-1], [tuple(out[4 * e:4 * e + 4]) for e in range(n)]


def _chip_peers():
    x, y = lax.axis_index("x"), lax.axis_index("y")
    return [(1 - x, y), (x, 1 - y), (1 - x, 1 - y)]


HBM_SPEC = pl.BlockSpec(memory_space=pltpu.HBM)
SEM_SPEC = pl.BlockSpec(memory_space=pltpu.SEMAPHORE)


def _exchange_peers(mode):
    x, y, c = lax.axis_index("x"), lax.axis_index("y"), lax.axis_index("c")
    if mode == "swap":
        return [(x, y, 1 - c)]
    if mode == "all":
        flips = [(fx, fy, fc) for fx in (0, 1) for fy in (0, 1) for fc in (0, 1)][1:]
        return [(1 - x if fx else x, 1 - y if fy else y, 1 - c if fc else c) for fx, fy, fc in flips]
    return [(px, py, c) for px, py in _chip_peers()]


def _exchange_start(mode, srcs, zones, *, name):
    n = len(srcs)

    def body(*refs):
        ins, lands = refs[:n], refs[n:2 * n]
        send_sems, recv_sems = refs[2 * n:3 * n], refs[3 * n:4 * n]
        token = refs[-1]
        x, y, c = lax.axis_index("x"), lax.axis_index("y"), lax.axis_index("c")
        mine = 2 * x + y if mode == "gather" else 4 * x + 2 * y + c
        for t in range(n):
            for k, (px, py, pc) in enumerate(_exchange_peers(mode)):
                if mode in ("gather", "all"):
                    s, d = ins[t], lands[t].at[mine]
                elif mode == "scatter":
                    s, d = ins[t].at[2 * px + py], lands[t].at[k]
                else:
                    s, d = ins[t], lands[t]
                pltpu.make_async_remote_copy(src_ref=s, dst_ref=d, send_sem=send_sems[t], recv_sem=recv_sems[t],
                                             device_id=(px, py, pc), device_id_type=MESH).start()
            if mode in ("gather", "all"):
                pltpu.make_async_copy(ins[t], lands[t].at[mine], send_sems[t]).start()
        token[...] = jnp.zeros_like(token)

    hbm = lambda a: pltpu.with_memory_space_constraint(a, pltpu.HBM)
    out = pl.pallas_call(
        body, name=name,
        in_specs=[HBM_SPEC] * (2 * n),
        out_specs=[SEM_SPEC] * (2 * n) + [HBM_SPEC] * (2 * n) + [pl.BlockSpec(memory_space=pltpu.VMEM)],
        out_shape=[pltpu.SemaphoreType.DMA(())] * (2 * n)
        + [pltpu.HBM(a.shape, a.dtype) for a in list(srcs) + list(zones)] + [_sds((8, LANES), F32)],
        input_output_aliases={i: 2 * n + i for i in range(2 * n)},
        compiler_params=pltpu.CompilerParams(has_side_effects=pltpu.SideEffectType.DATAFLOW_SIDE_EFFECTING),
    )(*[hbm(a) for a in list(srcs) + list(zones)])
    return out[:n], out[n:2 * n], out[2 * n:3 * n], out[3 * n:4 * n], out[-1]


def _exchange_wait(mode, started, after, *, name):
    send_sems, recv_sems, srcs, zones, _ = started
    n = len(srcs)
    afters = tuple(after) if isinstance(after, (tuple, list)) else (after,)

    def body(*refs):
        lands = refs[n:2 * n]
        send_refs, recv_refs = refs[2 * n:3 * n], refs[3 * n:4 * n]
        me = (lax.axis_index("x"), lax.axis_index("y"), lax.axis_index("c"))
        n_remote = {"gather": N_CHIPS - 1, "scatter": N_CHIPS - 1, "all": N_DEV - 1, "swap": 1}[mode]
        for t in range(n):
            got = lands[t] if mode == "swap" else lands[t].at[pl.ds(0, n_remote)]
            sent = lands[t] if mode in ("gather", "all") else got
            pltpu.make_async_remote_copy(src_ref=sent, dst_ref=sent, send_sem=send_refs[t], recv_sem=recv_refs[t],
                                         device_id=me, device_id_type=MESH).wait_send()
            pltpu.make_async_remote_copy(src_ref=got, dst_ref=got, send_sem=send_refs[t], recv_sem=recv_refs[t],
                                         device_id=me, device_id_type=MESH).wait_recv()

    out = pl.pallas_call(
        body, name=name,
        in_specs=[HBM_SPEC] * (2 * n) + [SEM_SPEC] * (2 * n) + [pl.BlockSpec(memory_space=pl.ANY)] * len(afters),
        out_specs=[HBM_SPEC] * (2 * n),
        out_shape=[pltpu.HBM(a.shape, a.dtype) for a in list(srcs) + list(zones)],
        input_output_aliases={i: i for i in range(2 * n)},
        compiler_params=pltpu.CompilerParams(has_side_effects=pltpu.SideEffectType.DATAFLOW_SIDE_EFFECTING),
    )(*srcs, *zones, *send_sems, *recv_sems, *afters)
    return out[:n], out[n:]


def _swap_with_sibling(parts):
    n = len(parts)

    def body(*refs):
        ins, outs = refs[:n], refs[n:2 * n]
        send_sems, recv_sems = refs[2 * n:]
        sib = (lax.axis_index("x"), lax.axis_index("y"), 1 - lax.axis_index("c"))
        cps = []
        for t in range(n):
            cp = pltpu.make_async_remote_copy(
                src_ref=ins[t], dst_ref=outs[t], send_sem=send_sems.at[t], recv_sem=recv_sems.at[t],
                device_id=sib, device_id_type=MESH)
            cp.start()
            cps.append(cp)
        for cp in cps:
            cp.wait()

    any_spec = pl.BlockSpec(memory_space=pl.ANY)
    return pl.pallas_call(
        body, name="swap_with_sibling", in_specs=[any_spec] * n, out_specs=[any_spec] * n,
        out_shape=[_sds(p.shape, p.dtype) for p in parts],
        scratch_shapes=[pltpu.SemaphoreType.DMA((n,)), pltpu.SemaphoreType.DMA((n,))])(*parts)


BIG = ("w_in", "w_out", "w_xq", "w_xk", "w_xv", "w_xo", "w_up", "w_down")
COL_SHARDED = ("w_in", "w_up")


def _as_matrix(name, w4):
    if name in COL_SHARDED:
        return w4
    return w4.reshape(1, w4.shape[0] * w4.shape[1], w4.shape[2])


def _shard_layout(name, g):
    if name in COL_SHARDED:
        return g
    return g.reshape(N_CHIPS, g.shape[0] * g.shape[1] // N_CHIPS, g.shape[2])


def _local_step(x, mem, target, vecs, comm):
    S = x.shape[0]
    tables = _rope_tables(S)

    xn = _rms_fwd(x, vecs["norm_mix_g"], name="rms_mix")
    w_in, conv_w32 = comm["first"]((xn,) + tuple(tables))
    y = _mm_nn(xn, w_in, name="mm_in", tm=2048, group=2)
    qk, v_perm = _rope_fwd(y, tables)
    v_src = [(y, 2)] + [(v, 0) for v in v_perm[1:]]
    outs, lses = zip(*[_att_fwd(qk[p], v_src[p], d, name=f"att_fwd_d{d}") for p, d in enumerate(DILATIONS)])
    att, lg = _att_combine(outs, lses)
    cv, u1 = _conv_fwd(y, conv_w32, vecs["conv_b"], vecs["conv_ln_g"], vecs["conv_ln_b"])
    Wm = {k: _as_matrix(k, v) for k, v in comm["rest"]((att, cv)).items()}
    Wm["w_in"] = w_in
    h1, hn = _mm_rows((att, cv), Wm["w_out"], _residual_norm_tail, name="mm_out_rms", rows_in=(x,),
                      vecs_in=(vecs["norm_x_g"],), rows_out=(F32, BF16))
    xq = _mm_nn(hn, Wm["w_xq"], name="mm_xq")
    mn = _rms_fwd(mem, vecs["norm_mem_g"], name="rms_mem")
    xk = _mm_nn(mn, Wm["w_xk"], name="mm_xk")
    xv = _mm_nn(mn, Wm["w_xv"], name="mm_xv")
    xo = _xatt_fwd(xq, xk, xv)
    h2, hm = _mm_rows(xo, Wm["w_xo"], _residual_norm_tail, name="mm_xo_rms", rows_in=(h1,),
                      vecs_in=(vecs["norm_mlp_g"],), rows_out=(F32, BF16))
    relu_up = _mm_nn(hm, Wm["w_up"], name="mm_up", relu=True, tm=2048)
    sums = ((8, D_MODEL),)
    dh3, dh3b, dg_final, loss = _mm_rows(
        relu_up, Wm["w_down"], _loss_tail, name="mm_down_loss", rows_in=(h2, target), vecs_in=(vecs["norm_final_g"],),
        rows_out=(F32, BF16), sums_out=sums + ((8, LANES),), a_squared=True, tm=256)
    g = {}
    g["w_down"] = _mm_tn(relu_up, dh3b, 1, name="dw_down", a_squared=True)
    dup = _mm_nt(dh3b, Wm["w_down"], name="d_act", out_dtype=BF16, mul=relu_up, tm=2048)
    g["w_up"] = _mm_tn(hm, dup, N_CHIPS, name="dw_up")
    sent = comm["send_mlp"]({k: _shard_layout(k, g[k]) for k in ("w_down", "w_up")})
    dh2, dh2b, dg_mlp = _mm_rows(
        dup, Wm["w_up"], _rms_bwd_tail(True), name="d_hm_rms", w_transposed=True, rows_in=(h2, dh3),
        vecs_in=(vecs["norm_mlp_g"] + sent[0:1, 0:1],), rows_out=(F32, BF16), sums_out=sums, tm=256)
    g["w_xo"] = _mm_tn(xo, dh2b, 1, name="dw_xo")
    dxo = _mm_nt(dh2b, Wm["w_xo"], name="d_xo", out_dtype=BF16)
    dxq, dxk, dxv = _xatt_bwd(xq, xk, xv, dxo)
    g["w_xq"] = _mm_tn(hn, dxq, 1, name="dw_xq")
    dh1, dh1b, dg_x = _mm_rows(
        dxq, Wm["w_xq"], _rms_bwd_tail(True), name="d_hn_rms", w_transposed=True, rows_in=(h1, dh2),
        vecs_in=(vecs["norm_x_g"],), rows_out=(F32, BF16), sums_out=sums)
    dxkb, dxvb = dxk.astype(BF16), dxv.astype(BF16)
    g["w_xk"] = _mm_tn(mn, dxkb, 1, name="dw_xk")
    g["w_xv"] = _mm_tn(mn, dxvb, 1, name="dw_xv")
    dmn = _mm_nt(jnp.concatenate([dxkb, dxvb], axis=1),
                 jnp.concatenate([Wm["w_xk"], Wm["w_xv"]], axis=2), name="d_mn", out_dtype=BF16)
    _, _, dg_mem = _rms_bwd(dmn, mem, vecs["norm_mem_g"], None, name="rms_bwd_mem", bf16_copy=False)
    g["w_out"] = jnp.concatenate([_mm_tn(att, dh1b, 1, name="dw_out_att"), _mm_tn(cv, dh1b, 1, name="dw_out_conv")],
                                 axis=1)
    sent = comm["send_att"]({k: _shard_layout(k, g[k]) for k in ("w_out", "w_xq", "w_xk", "w_xv", "w_xo")})
    dac = _mm_nt(dh1b, Wm["w_out"], name="d_mix", out_dtype=BF16)
    dag, dconv_w, dconv_small = _conv_bwd(dac, u1, y, conv_w32, vecs["conv_ln_g"] + sent[0:1, 0:1],
                                          vecs["conv_ln_b"])
    delta, do_perm = _att_delta(dac, att)
    do_src = [(dac, 0)] + [(t, 0) for t in do_perm[1:]]
    dq, dk, dv = zip(*[_att_bwd(qk[p], v_src[p], do_src[p], lg[p], delta[p], d, name=f"att_bwd_d{d}")
                       for p, d in enumerate(DILATIONS)])
    dy = _assemble_dy(dq, dk, dv, dag, tables)
    sent = comm["send_in"]({"w_in": _mm_tn(xn, dy, N_CHIPS, name="dw_in", group=2)})
    grad_x, dg_mix = _mm_rows(
        dy, Wm["w_in"], _rms_bwd_tail(False), name="d_xn_rms", w_transposed=True, rows_in=(x, dh1),
        vecs_in=(vecs["norm_mix_g"] + sent[0:1, 0:1],), rows_out=(F32,), sums_out=sums)

    small = dict(conv_w=dconv_w, conv_small=dconv_small, norm_mix_g=dg_mix, norm_x_g=dg_x, norm_mem_g=dg_mem,
                 norm_mlp_g=dg_mlp, norm_final_g=dg_final, loss=loss)
    return grad_x, small


SMALL_ORDER = ("conv_w", "conv_small", "norm_mix_g", "norm_x_g", "norm_mem_g", "norm_mlp_g", "norm_final_g", "loss")


def _pack_small(small):
    rows, offs, pos = [], {}, 0
    for k in SMALL_ORDER:
        a = small[k]
        a = a.reshape(a.shape[0] * a.shape[1] // SMALL_W, SMALL_W)
        pad = (-a.shape[0]) % 8
        if pad:
            a = jnp.pad(a, ((0, pad), (0, 0)))
        rows.append(a)
        offs[k] = pos
        pos += a.shape[0]
    return jnp.concatenate(rows, axis=0), offs


def kernel(x, mem, norm_mix_g, w_in, conv_w, conv_b, conv_ln_g, conv_ln_b, w_out, norm_x_g, norm_mem_g, w_xq, w_xk, w_xv, w_xo, norm_mlp_g, w_up, w_down, norm_final_g, loss_target, m_norm_mix_g, m_w_in, m_conv_w, m_conv_b, m_conv_ln_g, m_conv_ln_b, m_w_out, m_norm_x_g, m_norm_mem_g, m_w_xq, m_w_xk, m_w_xv, m_w_xo, m_norm_mlp_g, m_w_up, m_w_down, m_norm_final_g, v_norm_mix_g, v_w_in, v_conv_w, v_conv_b, v_conv_ln_g, v_conv_ln_b, v_w_out, v_norm_x_g, v_norm_mem_g, v_w_xq, v_w_xk, v_w_xv, v_w_xo, v_norm_mlp_g, v_w_up, v_w_down, v_norm_final_g):
    names = ("norm_mix_g", "w_in", "conv_w", "conv_b", "conv_ln_g", "conv_ln_b", "w_out", "norm_x_g", "norm_mem_g",
             "w_xq", "w_xk", "w_xv", "w_xo", "norm_mlp_g", "w_up", "w_down", "norm_final_g")
    wts = dict(zip(names, (norm_mix_g, w_in, conv_w, conv_b, conv_ln_g, conv_ln_b, w_out, norm_x_g, norm_mem_g,
                           w_xq, w_xk, w_xv, w_xo, norm_mlp_g, w_up, w_down, norm_final_g)))
    mom = dict(zip(names, (m_norm_mix_g, m_w_in, m_conv_w, m_conv_b, m_conv_ln_g, m_conv_ln_b, m_w_out, m_norm_x_g,
                           m_norm_mem_g, m_w_xq, m_w_xk, m_w_xv, m_w_xo, m_norm_mlp_g, m_w_up, m_w_down, m_norm_final_g)))
    var = dict(zip(names, (v_norm_mix_g, v_w_in, v_conv_w, v_conv_b, v_conv_ln_g, v_conv_ln_b, v_w_out, v_norm_x_g,
                           v_norm_mem_g, v_w_xq, v_w_xk, v_w_xv, v_w_xo, v_norm_mlp_g, v_w_up, v_w_down, v_norm_final_g)))
    chip = 2 * lax.axis_index("x") + lax.axis_index("y")

    def zone(shard):
        return lax.empty((N_CHIPS,) + shard.shape, shard.dtype)

    conv_w_pad = jnp.pad(wts["conv_w"][0], ((0, 1), (0, 0)))
    first_shards = [wts["w_in"][0].astype(BF16), conv_w_pad]
    gathering_first = _exchange_start("gather", first_shards, [zone(s) for s in first_shards],
                                      name="gather_first_start")
    rest = tuple(k for k in BIG if k != "w_in")
    behind_first = gathering_first[4][0, 0]
    rest_shards = [(wts[k][0] + behind_first).astype(BF16) for k in rest]
    gathering = gathering_rest = _exchange_start("gather", rest_shards, [zone(s) for s in rest_shards],
                                                 name="gather_rest_start")
    sending = {}

    def wait_first(after):
        _, (w_in_all, conv_w_all) = _exchange_wait("gather", gathering_first, after, name="gather_first_wait")
        return w_in_all, jnp.transpose(conv_w_all, (1, 0, 2)).reshape(32, D_CONV)

    def wait_rest(after):
        _, zones = _exchange_wait("gather", gathering_rest, after, name="gather_rest_wait")
        return dict(zip(rest, zones))

    def send(group, grads):
        keys = tuple(grads)
        zones = [lax.empty((N_CHIPS - 1,) + grads[k].shape[1:], grads[k].dtype) for k in keys]
        sending[group] = (keys, _exchange_start("scatter", [grads[k] for k in keys], zones,
                                                name=f"scatter_{group}_start"))
        return sending[group][1][4]

    comm = dict(first=wait_first, rest=wait_rest, send_mlp=lambda grads: send("mlp", grads),
                send_att=lambda grads: send("att", grads), send_in=lambda grads: send("in", grads))
    vecs = {k: wts[k] for k in ("conv_b", "conv_ln_g", "conv_ln_b", "norm_x_g", "norm_mem_g", "norm_mlp_g")}
    vecs["norm_mix_g"] = wts["norm_mix_g"] + gathering[4][0:1, 0:1]
    vecs["norm_final_g"] = wts["norm_final_g"].reshape(1, D_MODEL)
    grad_x, small = _local_step(x[0], mem[0], loss_target[0], vecs, comm)

    packed, offs = _pack_small(small)
    me_arr = jnp.reshape(chip, (1,)).astype(jnp.int32)
    gathering_small = _exchange_start("all", [packed], [lax.empty((N_DEV,) + packed.shape, packed.dtype)],
                                      name="allgather_small_start")
    sums = {}

    def settle(group, after):
        keys, started = sending[group]
        srcs, zones = _exchange_wait("scatter", started, after, name=f"scatter_{group}_wait")
        for k, own, got in zip(keys, srcs, zones):
            sums[k] = _sum_partials(own, got, me_arr, name=f"sum_{k}")

    settle("mlp", gathering_small[4])
    settle("att", gathering_small[4])
    early = tuple(sums)
    swapping = _exchange_start("swap", [sums[k] for k in early], [lax.empty(sums[k].shape, F32) for k in early],
                               name="swap_early_start")
    settle("in", swapping[4])
    _, (gath,) = _exchange_wait("all", gathering_small, sums["w_in"], name="allgather_small_wait")

    where = {"conv_w": ("conv_w", offs["conv_w"]), "conv_b": ("row", offs["conv_small"]),
             "conv_ln_g": ("row", offs["conv_small"] + 1), "conv_ln_b": ("row", offs["conv_small"] + 2)}
    where.update({k: ("gain", offs[k]) for k in ("norm_mix_g", "norm_x_g", "norm_mem_g", "norm_mlp_g", "norm_final_g")})
    as_2d = lambda a: a.reshape(a.shape[-2] if a.ndim > 1 else 1, a.shape[-1])
    tot_small, updates = _adamw_small(gath, me_arr, [(where[k], as_2d(wts[k]), as_2d(mom[k]), as_2d(var[k]))
                                                     for k in where])
    res = dict(zip(where, updates))
    loss = tot_small[offs["loss"], 0]

    sib = {"w_in": _swap_with_sibling([sums["w_in"]])[0]}
    mine_early, sib_early = _exchange_wait("swap", swapping, sib["w_in"], name="swap_early_wait")
    sums.update(zip(early, mine_early))
    sib.update(zip(early, sib_early))
    for k in BIG:
        res[k] = _adamw([sums[k], sib[k]], wts[k][0], mom[k][0], var[k][0], name=f"adamw_{k}")

    outs = [loss, grad_x[None]]
    for j in range(4):
        outs += [res[k][j].reshape(wts[k].shape) for k in names]
    return tuple(outs)
```

```python
import jax
import jax.numpy as jnp
from jax import lax
from jax.experimental import pallas as pl
from jax.experimental.pallas import tpu as pltpu

F32 = jnp.float32
BF16 = jnp.bfloat16
MESH = pl.DeviceIdType.MESH

D_MODEL = 1024
ATT_HEADS = 8
HEAD_DIM = 64
D_ATT = ATT_HEADS * HEAD_DIM
D_CONV = D_MODEL - D_ATT
DILATIONS = (1, 4, 16)
HALF = 64
ROPE_THETA = 500000.0
ROT_DIM = HEAD_DIM // 4
CONV_WIDTH = 31
CONV_PAD = (CONV_WIDTH - 1) // 2
XATT_HEADS = 4
XATT_HEAD_DIM = D_MODEL // XATT_HEADS
D_FF = 4 * D_MODEL
D_IN = 3 * D_ATT + 2 * D_CONV
EPS = 1e-6
NEG_INF = -1e30
N_CHIPS = 4
N_DEV = 8

ADAM_LR = 0.001
ADAM_B1 = 0.9
ADAM_B2 = 0.999
ADAM_EPS = 1e-08
ADAM_WD = 0.01
ADAM_STEP = 10

VMEM_LIMIT_V7X = 56 * 1024 * 1024
LANES = 128
HALO = 16
CONV_ROWS = 64
ATT_BLOCK = 128
SMALL_W = 512


def _params(*sem):
    return pltpu.CompilerParams(dimension_semantics=sem, vmem_limit_bytes=VMEM_LIMIT_V7X)


def _sds(shape, dtype):
    return jax.ShapeDtypeStruct(shape, dtype)


def _squared(a):
    af = a.astype(F32)
    return (af * af).astype(BF16)


def _mm_nn(a, w3, *, name, out_dtype=BF16, relu=False, group=1, tm=1024, tn=None, tk=1024):
    M, K = a.shape
    nsh, _, n = w3.shape
    tm, tk = min(tm, M), min(tk, K)
    tn = group * n if group > 1 else (tn or min(n, 1024))
    npt, nk = max(n // tn, 1), K // tk
    nj, N = nsh * npt // group, nsh * n

    def body(a_ref, w_ref, out_ref, *scratch):
        acc_ref = scratch[0] if nk > 1 else None

        def finish(acc):
            if relu:
                acc = jnp.maximum(acc, 0.0)
            out_ref[...] = acc.astype(out_ref.dtype)

        a_val = a_ref[...]
        w_val = w_ref[...] if group == 1 else jnp.concatenate([w_ref[s] for s in range(group)], axis=1)
        part = jnp.dot(a_val, w_val, preferred_element_type=F32)
        if nk == 1:
            finish(part)
        else:
            k = pl.program_id(2)

            @pl.when(k == 0)
            def _():
                acc_ref[...] = part

            @pl.when(k > 0)
            def _():
                acc_ref[...] += part

            @pl.when(k == nk - 1)
            def _():
                finish(acc_ref[...])

    w_spec = (pl.BlockSpec((None, tk, tn), lambda i, j, k: (j // npt, k, j % npt)) if group == 1 else
              pl.BlockSpec((group, tk, n), lambda i, j, k: (j, k, 0)))
    return pl.pallas_call(
        body, name=name, grid=(M // tm, nj, nk),
        in_specs=[pl.BlockSpec((tm, tk), lambda i, j, k: (i, k)), w_spec],
        out_specs=pl.BlockSpec((tm, tn), lambda i, j, k: (i, j)), out_shape=_sds((M, N), out_dtype),
        scratch_shapes=[pltpu.VMEM((tm, tn), F32)] if nk > 1 else [],
        compiler_params=_params("parallel", "parallel", "arbitrary"))(a, w3)


def _mm_nt(dy, w3, *, name, out_dtype=F32, mul=None, tm=1024, tn=None, tko=1024):
    M, N = dy.shape
    nsh, K, n = w3.shape
    tm, tko = min(tm, M), min(tko, K)
    tn = tn or min(n, 1024)
    npt = n // tn
    nj = nsh * npt

    def body(*refs):
        dy_ref, w_ref = refs[0], refs[1]
        pos = 2
        mul_ref = None
        if mul is not None:
            mul_ref = refs[pos]
            pos += 1
        out_ref = refs[pos]
        acc_ref = refs[pos + 1] if nj > 1 else None

        def finish(acc):
            if mul_ref is not None:
                acc = acc * (2.0 * mul_ref[...].astype(F32))
            out_ref[...] = acc.astype(out_ref.dtype)

        part = lax.dot_general(dy_ref[...], w_ref[...], (((1,), (1,)), ((), ())), preferred_element_type=F32)
        if nj == 1:
            finish(part)
        else:
            j = pl.program_id(2)

            @pl.when(j == 0)
            def _():
                acc_ref[...] = part

            @pl.when(j > 0)
            def _():
                acc_ref[...] += part

            @pl.when(j == nj - 1)
            def _():
                finish(acc_ref[...])

    in_specs = [pl.BlockSpec((tm, tn), lambda i, ko, j: (i, j)),
                pl.BlockSpec((None, tko, tn), lambda i, ko, j: (j // npt, ko, j % npt))]
    args = [dy, w3]
    if mul is not None:
        in_specs.append(pl.BlockSpec((tm, tko), lambda i, ko, j: (i, ko)))
        args.append(mul)
    return pl.pallas_call(
        body, name=name, grid=(M // tm, K // tko, nj), in_specs=in_specs,
        out_specs=pl.BlockSpec((tm, tko), lambda i, ko, j: (i, ko)), out_shape=_sds((M, K), out_dtype),
        scratch_shapes=[pltpu.VMEM((tm, tko), F32)] if nj > 1 else [],
        compiler_params=_params("parallel", "parallel", "arbitrary"))(*args)


def _mm_tn(a, dy, nsh, *, name, out_dtype=BF16, a_squared=False, group=1, tm=2048, tk=1024, tn=None):
    M, K = a.shape
    N = dy.shape[1]
    n = N // nsh
    tm, tk = min(tm, M), min(tk, K)
    tn = group * n if group > 1 else (tn or min(n, 1024))
    npt = max(n // tn, 1)
    nj, nm = nsh * npt // group, M // tm

    def body(a_ref, dy_ref, out_ref, acc_ref):
        m = pl.program_id(2)
        a_val = _squared(a_ref[...]) if a_squared else a_ref[...]
        part = lax.dot_general(a_val, dy_ref[...], (((0,), (0,)), ((), ())), preferred_element_type=F32)

        @pl.when(m == 0)
        def _():
            acc_ref[...] = part

        @pl.when(m > 0)
        def _():
            acc_ref[...] += part

        @pl.when(m == nm - 1)
        def _():
            if group == 1:
                out_ref[...] = acc_ref[...].astype(out_ref.dtype)
            else:
                for s in range(group):
                    out_ref[s] = acc_ref[:, s * n:(s + 1) * n].astype(out_ref.dtype)

    out_spec = (pl.BlockSpec((None, tk, tn), lambda kk, j, m: (j // npt, kk, j % npt)) if group == 1 else
                pl.BlockSpec((group, tk, n), lambda kk, j, m: (j, kk, 0)))
    return pl.pallas_call(
        body, name=name, grid=(K // tk, nj, nm),
        in_specs=[pl.BlockSpec((tm, tk), lambda kk, j, m: (m, kk)),
                  pl.BlockSpec((tm, tn), lambda kk, j, m: (m, j))],
        out_specs=out_spec,
        out_shape=_sds((nsh, K, n), out_dtype),
        scratch_shapes=[pltpu.VMEM((tk, tn), F32)],
        compiler_params=_params("parallel", "parallel", "arbitrary"))(a, dy)


def _rms_fwd(x, g, *, name, tm=512):
    M, Dm = x.shape
    tm = min(tm, M)

    def body(x_ref, g_ref, o_ref):
        xf = x_ref[...]
        r = lax.rsqrt(jnp.mean(xf * xf, axis=-1, keepdims=True) + EPS)
        o_ref[...] = (xf * r * g_ref[...]).astype(o_ref.dtype)

    return pl.pallas_call(
        body, name=name, grid=(M // tm,),
        in_specs=[pl.BlockSpec((tm, Dm), lambda i: (i, 0)), pl.BlockSpec((1, Dm), lambda i: (0, 0))],
        out_specs=pl.BlockSpec((tm, Dm), lambda i: (i, 0)), out_shape=_sds((M, Dm), BF16),
        compiler_params=_params("parallel"))(x, g)


def _rms_bwd(dxn, x, g, dres, *, name, bf16_copy=True, tm=512):
    M, Dm = x.shape
    tm = min(tm, M)
    has_res = dres is not None

    def body(*refs):
        dxn_ref, x_ref, g_ref = refs[:3]
        dres_ref = refs[3] if has_res else None
        dx_ref, dg_ref = refs[-1 - 1 - bf16_copy], refs[-1]
        dxb_ref = refs[-2] if bf16_copy else None
        i = pl.program_id(0)
        xf = x_ref[...]
        r = lax.rsqrt(jnp.mean(xf * xf, axis=-1, keepdims=True) + EPS)
        nrm = xf * r
        dxn_f = dxn_ref[...].astype(F32)
        dn = dxn_f * g_ref[...]
        dx = r * (dn - nrm * jnp.mean(dn * nrm, axis=-1, keepdims=True))
        if has_res:
            dx = dx + dres_ref[...]
        dx_ref[...] = dx
        if bf16_copy:
            dxb_ref[...] = dx.astype(dxb_ref.dtype)

        @pl.when(i == 0)
        def _():
            dg_ref[...] = jnp.zeros_like(dg_ref)

        dg_ref[0:1, :] += jnp.sum(dxn_f * nrm, axis=0, keepdims=True)

    row = pl.BlockSpec((tm, Dm), lambda i: (i, 0))
    in_specs = [row, row, pl.BlockSpec((1, Dm), lambda i: (0, 0))] + ([row] if has_res else [])
    args = [dxn, x, g] + ([dres] if has_res else [])
    out = pl.pallas_call(
        body, name=name, grid=(M // tm,), in_specs=in_specs,
        out_specs=[row] * (1 + bf16_copy) + [pl.BlockSpec((8, Dm), lambda i: (0, 0))],
        out_shape=[_sds((M, Dm), F32)] + [_sds((M, Dm), BF16)] * bf16_copy + [_sds((8, Dm), F32)],
        compiler_params=_params("arbitrary"))(*args)
    return out[0], (out[1] if bf16_copy else None), out[-1]


def _mm_rows(a, w3, tail, *, name, rows_in=(), vecs_in=(), rows_out=(), sums_out=(), a_squared=False,
             w_transposed=False, tm=512):
    parts = a if isinstance(a, (tuple, list)) else (a,)
    M = parts[0].shape[0]
    K, N = (w3.shape[0] * w3.shape[2], w3.shape[1]) if w_transposed else (w3.shape[1], w3.shape[2])
    tm = min(tm, M)
    n_a, n_ri, n_vi, n_ro = len(parts), len(rows_in), len(vecs_in), len(rows_out)

    def body(*refs):
        a_refs, w_ref, refs = refs[:n_a], refs[n_a], refs[n_a + 1:]
        rin, vin = refs[:n_ri], refs[n_ri:n_ri + n_vi]
        rout, sout = refs[n_ri + n_vi:n_ri + n_vi + n_ro], refs[n_ri + n_vi + n_ro:]

        @pl.when(pl.program_id(0) == 0)
        def _():
            for s in sout:
                s[...] = jnp.zeros_like(s)

        a_val = a_refs[0][...] if n_a == 1 else jnp.concatenate([r[...] for r in a_refs], axis=1)
        if a_squared:
            a_val = _squared(a_val)
        if w_transposed:
            n = w3.shape[2]
            prod = _nt(a_val[:, 0:n], w_ref[0])
            for j in range(1, w3.shape[0]):
                prod = prod + _nt(a_val[:, j * n:(j + 1) * n], w_ref[j])
        else:
            prod = jnp.dot(a_val, w_ref[0], preferred_element_type=F32)
        tail(prod, rin, vin, rout, sout)

    row = pl.BlockSpec((tm, N), lambda i: (i, 0))
    once = lambda shape: pl.BlockSpec(shape, lambda i: (0,) * len(shape))
    return pl.pallas_call(
        body, name=name, grid=(M // tm,),
        in_specs=[pl.BlockSpec((tm, p.shape[1]), lambda i: (i, 0)) for p in parts] + [once(w3.shape)]
        + [row] * n_ri + [once((1, N))] * n_vi,
        out_specs=[row] * n_ro + [once(s) for s in sums_out],
        out_shape=[_sds((M, N), dt) for dt in rows_out] + [_sds(s, F32) for s in sums_out],
        compiler_params=_params("arbitrary"))(*parts, w3, *rows_in, *vecs_in)


def _residual_norm_tail(prod, rows_in, vecs_in, rows_out, sums_out):
    hf = prod + rows_in[0][...]
    rows_out[0][...] = hf
    r = lax.rsqrt(jnp.mean(hf * hf, axis=-1, keepdims=True) + EPS)
    rows_out[1][...] = (hf * r * vecs_in[0][...]).astype(BF16)


def _rms_bwd_tail(bf16_copy):
    def tail(dxn, rows_in, vecs_in, rows_out, sums_out):
        xf = rows_in[0][...]
        r = lax.rsqrt(jnp.mean(xf * xf, axis=-1, keepdims=True) + EPS)
        nrm = xf * r
        dn = dxn * vecs_in[0][...]
        dx = r * (dn - nrm * jnp.mean(dn * nrm, axis=-1, keepdims=True)) + rows_in[1][...]
        rows_out[0][...] = dx
        if bf16_copy:
            rows_out[1][...] = dx.astype(BF16)
        sums_out[0][0:1, :] += jnp.sum(dxn * nrm, axis=0, keepdims=True)

    return tail


def _loss_tail(prod, rows_in, vecs_in, rows_out, sums_out):
    hf = prod + rows_in[0][...]
    r = lax.rsqrt(jnp.mean(hf * hf, axis=-1, keepdims=True) + EPS)
    nrm = hf * r
    gv = vecs_in[0][...]
    err = nrm * gv - rows_in[1][...]
    dy = err * (1.0 / hf.shape[-1])
    dn = dy * gv
    dh = r * (dn - nrm * jnp.mean(dn * nrm, axis=-1, keepdims=True))
    rows_out[0][...] = dh
    rows_out[1][...] = dh.astype(BF16)
    sums_out[0][0:1, :] += jnp.sum(dy * nrm, axis=0, keepdims=True)
    part = 0.5 * jnp.sum(jnp.mean(err * err, axis=-1, keepdims=True), axis=0, keepdims=True)
    sel = (lax.broadcasted_iota(jnp.int32, (8, 128), 0) == 0) & (lax.broadcasted_iota(jnp.int32, (8, 128), 1) == 0)
    sums_out[1][...] += jnp.where(sel, part, 0.0)


def _class_spec(tm, d, width):
    return pl.BlockSpec((d, tm // d, width), lambda i: (0, i, 0))


def _row_scratch(tm, width):
    return pltpu.VMEM((width // LANES, tm, LANES), F32)


def _fill(scr, val):
    for c in range(scr.shape[0]):
        scr[c] = val[:, c * LANES:(c + 1) * LANES]


def _to_classes(scr, out_ref, d):
    n = scr.shape[1] // d
    for r in range(d):
        for c in range(scr.shape[0]):
            out_ref[r, :, c * LANES:(c + 1) * LANES] = scr[c, pl.ds(r, n, stride=d), :].astype(out_ref.dtype)


def _from_classes(in_ref, scr, d):
    n = scr.shape[1] // d
    for r in range(d):
        blk = in_ref[r].astype(F32)
        for c in range(scr.shape[0]):
            scr[c, pl.ds(r, n, stride=d), :] = blk[:, c * LANES:(c + 1) * LANES]
    return jnp.concatenate([scr[c] for c in range(scr.shape[0])], axis=1)


def _rope_tables(S):
    half = ROT_DIM // 2
    freqs = ROPE_THETA ** (-jnp.arange(0, ROT_DIM, 2, dtype=F32) / ROT_DIM)
    ang = jnp.arange(S, dtype=F32)[:, None] * freqs[None, :]
    cos, sin = jnp.cos(ang), jnp.sin(ang)
    ones = jnp.ones((S, HEAD_DIM - ROT_DIM), F32)
    zeros = jnp.zeros((S, HEAD_DIM - ROT_DIM), F32)
    zh = jnp.zeros((S, half), F32)
    c = jnp.concatenate([cos, cos, ones], axis=1)
    sa = jnp.concatenate([-sin, zh, zeros], axis=1)
    sb = jnp.concatenate([zh, sin, zeros], axis=1)
    return tuple(jnp.tile(t, (1, LANES // HEAD_DIM)) for t in (c, sa, sb))


def _rope_fwd(y, tables, *, tm=512):
    S = y.shape[0]
    W = 2 * D_ATT
    tm = min(tm, S)
    half = ROT_DIM // 2
    dils = [d for d in DILATIONS if d > 1]

    def body(y_ref, c_ref, sa_ref, sb_ref, qk_ref, *rest):
        qk_outs, v_outs = rest[:len(dils)], rest[len(dils):2 * len(dils)]
        scr_qk, scr_v = rest[2 * len(dils):]
        t = y_ref[:, 0:W].astype(F32)
        rep = W // LANES
        c, sa, sb = (jnp.tile(r[...], (1, rep)) for r in (c_ref, sa_ref, sb_ref))
        rot = t * c + pltpu.roll(t, W - half, axis=1) * sa + pltpu.roll(t, half, axis=1) * sb
        qk_ref[...] = rot.astype(qk_ref.dtype)
        _fill(scr_qk, rot)
        _fill(scr_v, y_ref[:, W:W + D_ATT].astype(F32))
        for d, qo, vo in zip(dils, qk_outs, v_outs):
            _to_classes(scr_qk, qo, d)
            _to_classes(scr_v, vo, d)

    tab = pl.BlockSpec((tm, LANES), lambda i: (i, 0))
    out = pl.pallas_call(
        body, name="rope_fwd", grid=(S // tm,),
        in_specs=[pl.BlockSpec((tm, 3 * D_ATT), lambda i: (i, 0)), tab, tab, tab],
        out_specs=[pl.BlockSpec((tm, W), lambda i: (i, 0))] + [_class_spec(tm, d, W) for d in dils]
        + [_class_spec(tm, d, D_ATT) for d in dils],
        out_shape=[_sds((S, W), BF16)] + [_sds((d, S // d, W), BF16) for d in dils]
        + [_sds((d, S // d, D_ATT), BF16) for d in dils],
        scratch_shapes=[_row_scratch(tm, W), _row_scratch(tm, D_ATT)],
        compiler_params=_params("parallel"))(y, *tables)
    qk = [out[0]] + [o.reshape(S, W) for o in out[1:1 + len(dils)]]
    v = [None] + [o.reshape(S, D_ATT) for o in out[1 + len(dils):]]
    return qk, v


def _assemble_dy(dq, dk, dv, dag, tables, *, tm=512):
    S = dag.shape[0]
    tm = min(tm, S)
    half = ROT_DIM // 2
    W = D_ATT
    n_pat = len(DILATIONS)

    def body(*refs):
        groups = [refs[g * n_pat:(g + 1) * n_pat] for g in range(3)]
        dag_ref, c_ref, sa_ref, sb_ref, o_ref, scr = refs[3 * n_pat:]
        rep = W // LANES
        c, sa, sb = (jnp.tile(r[...], (1, rep)) for r in (c_ref, sa_ref, sb_ref))

        def total(rs):
            acc = rs[0][...].astype(F32)
            for d, r in zip(DILATIONS[1:], rs[1:]):
                acc = acc + _from_classes(r, scr, d)
            return acc

        def unrope(dr):
            return dr * c + pltpu.roll(dr * sa, half, axis=1) + pltpu.roll(dr * sb, W - half, axis=1)

        o_ref[:, 0:W] = unrope(total(groups[0])).astype(o_ref.dtype)
        o_ref[:, W:2 * W] = unrope(total(groups[1])).astype(o_ref.dtype)
        o_ref[:, 2 * W:3 * W] = total(groups[2]).astype(o_ref.dtype)
        o_ref[:, 3 * W:] = dag_ref[...]

    specs = [pl.BlockSpec((tm, W), lambda i: (i, 0))] + [_class_spec(tm, d, W) for d in DILATIONS[1:]]
    tab = pl.BlockSpec((tm, LANES), lambda i: (i, 0))
    args = [a if d == 1 else a.reshape(d, S // d, W) for grp in (dq, dk, dv) for d, a in zip(DILATIONS, grp)]
    return pl.pallas_call(
        body, name="assemble_dy", grid=(S // tm,),
        in_specs=specs * 3 + [pl.BlockSpec((tm, 2 * D_CONV), lambda i: (i, 0)), tab, tab, tab],
        out_specs=pl.BlockSpec((tm, D_IN), lambda i: (i, 0)), out_shape=_sds((S, D_IN), BF16),
        scratch_shapes=[_row_scratch(tm, W)],
        compiler_params=_params("parallel"))(*args, dag, *tables)


def _seq_specs(L, tb, col):
    nb, per, nh = L // tb, tb // HALF, L // HALF
    centre = pl.BlockSpec((tb, D_ATT), lambda r, i: (r * nb + i, col))
    prev = pl.BlockSpec((HALF, D_ATT), lambda r, i: (r * nh + jnp.maximum(i * per - 1, 0), col))
    nxt = pl.BlockSpec((HALF, D_ATT), lambda r, i: (r * nh + jnp.minimum((i + 1) * per, nh - 1), col))
    return prev, centre, nxt


def _band_mask(i, tq, L):
    shape = (tq, tq + 2 * HALF)
    c_idx = lax.broadcasted_iota(jnp.int32, shape, 0)
    w_idx = lax.broadcasted_iota(jnp.int32, shape, 1)
    diff = w_idx - c_idx
    wpos = i * tq - HALF + w_idx
    return (diff >= 0) & (diff <= 2 * HALF) & (wpos >= 0) & (wpos < L)


def _lane_groups():
    for c0 in range(0, D_ATT, LANES):
        yield slice(c0, c0 + LANES)


def _first_head(rows):
    return lax.broadcasted_iota(jnp.int32, (rows, LANES), 1) < HEAD_DIM


def _split_pair(x, first):
    zero = jnp.zeros_like(x)
    return jnp.where(first, x, zero), jnp.where(first, zero, x)


def _nt(a, b):
    return lax.dot_general(a, b, (((1,), (1,)), ((), ())), preferred_element_type=F32)


def _tn(a, b):
    return lax.dot_general(a, b, (((0,), (0,)), ((), ())), preferred_element_type=F32)


ATT_SCALE = HEAD_DIM ** -0.5


def _att_fwd(qk, v_src, d, *, name):
    S = qk.shape[0]
    L = S // d
    tq = min(ATT_BLOCK, L)
    v_arr, v_col = v_src

    def body(q_ref, kp_ref, kc_ref, kn_ref, vp_ref, vc_ref, vn_ref, o_ref, lse_ref):
        i = pl.program_id(1)
        valid = _band_mask(i, tq, L)
        q = q_ref[...] * ATT_SCALE
        kwin = jnp.concatenate([kp_ref[...], kc_ref[...], kn_ref[...]], axis=0)
        vwin = jnp.concatenate([vp_ref[...], vc_ref[...], vn_ref[...]], axis=0)
        first = _first_head(tq)
        groups = list(_lane_groups())
        heads = [(ls, t) for ls in groups for t in _split_pair(q[:, ls], first)]
        s = [jnp.where(valid, _nt(t, kwin[:, ls]), NEG_INF) for ls, t in heads]
        m = [jnp.max(t, axis=-1, keepdims=True) for t in s]
        p = [jnp.exp(t - mm) for t, mm in zip(s, m)]
        den = [jnp.sum(t, axis=-1, keepdims=True) for t in p]
        o = [jnp.dot(t.astype(BF16), vwin[:, ls], preferred_element_type=F32) * (1.0 / dd)
             for t, dd, (ls, _) in zip(p, den, heads)]
        lse = [mm + jnp.log(dd) for mm, dd in zip(m, den)]
        for g, ls in enumerate(groups):
            o_ref[:, ls] = jnp.where(first, o[2 * g], o[2 * g + 1]).astype(o_ref.dtype)
            lse_ref[:, ls] = jnp.where(first, lse[2 * g], lse[2 * g + 1])

    _, qc, _ = _seq_specs(L, tq, 0)
    kp, kc, kn = _seq_specs(L, tq, 1)
    vp, vc, vn = _seq_specs(L, tq, v_col)
    out = pl.BlockSpec((tq, D_ATT), lambda r, i: (r * (L // tq) + i, 0))
    return pl.pallas_call(
        body, name=name, grid=(d, L // tq),
        in_specs=[qc, kp, kc, kn, vp, vc, vn], out_specs=[out, out],
        out_shape=[_sds((S, D_ATT), BF16), _sds((S, D_ATT), F32)],
        compiler_params=_params("parallel", "parallel"))(qk, qk, qk, qk, v_arr, v_arr, v_arr)


def _att_combine(outs, lses, *, tm=512):
    S = outs[0].shape[0]
    tm = min(tm, S)
    dils = DILATIONS[1:]
    n_d = len(dils)

    def body(*refs):
        o_refs, l_refs = refs[0:1 + n_d], refs[1 + n_d:2 + 2 * n_d]
        att_ref, lg_ref = refs[2 + 2 * n_d:4 + 2 * n_d]
        lg_outs = refs[4 + 2 * n_d:4 + 3 * n_d]
        scr = refs[4 + 3 * n_d:]
        scr_o, scr_l, scr_lg = scr[:n_d], scr[n_d:2 * n_d], scr[2 * n_d]
        ls = [l_refs[0][...]] + [_from_classes(r, s, d) for r, s, d in zip(l_refs[1:], scr_l, dils)]
        os_ = [o_refs[0][...].astype(F32)] + [_from_classes(r, s, d) for r, s, d in zip(o_refs[1:], scr_o, dils)]
        mx = ls[0]
        for l in ls[1:]:
            mx = jnp.maximum(mx, l)
        es = [jnp.exp(l - mx) for l in ls]
        tot = es[0]
        num = es[0] * os_[0]
        for e, o in zip(es[1:], os_[1:]):
            tot = tot + e
            num = num + e * o
        att_ref[...] = (num / tot).astype(att_ref.dtype)
        lg = mx + jnp.log(tot)
        lg_ref[...] = lg
        _fill(scr_lg, lg)
        for d, out in zip(dils, lg_outs):
            _to_classes(scr_lg, out, d)

    nat = pl.BlockSpec((tm, D_ATT), lambda i: (i, 0))
    specs = [nat] + [_class_spec(tm, d, D_ATT) for d in dils]
    view = lambda arrs: [arrs[0]] + [a.reshape(d, S // d, D_ATT) for a, d in zip(arrs[1:], dils)]
    out = pl.pallas_call(
        body, name="att_combine", grid=(S // tm,), in_specs=specs * 2,
        out_specs=[nat, nat] + specs[1:],
        out_shape=[_sds((S, D_ATT), BF16), _sds((S, D_ATT), F32)] + [_sds((d, S // d, D_ATT), F32) for d in dils],
        scratch_shapes=[_row_scratch(tm, D_ATT)] * (2 * n_d + 1),
        compiler_params=_params("parallel"))(*view(list(outs)), *view(list(lses)))
    return out[0], [out[1]] + [o.reshape(S, D_ATT) for o in out[2:]]


def _att_delta(dac, att, *, tm=512):
    S = att.shape[0]
    tm = min(tm, S)
    dils = DILATIONS[1:]
    n_d = len(dils)

    def body(do_ref, o_ref, dl_ref, *rest):
        dl_outs, do_outs = rest[:n_d], rest[n_d:2 * n_d]
        scr_dl, scr_do = rest[2 * n_d:]
        do = do_ref[...].astype(F32)
        prod = do * o_ref[...].astype(F32)
        per_head = [jnp.broadcast_to(jnp.sum(prod[:, h * HEAD_DIM:(h + 1) * HEAD_DIM], axis=-1, keepdims=True),
                                     (tm, HEAD_DIM)) for h in range(ATT_HEADS)]
        dl = jnp.concatenate(per_head, axis=1)
        dl_ref[...] = dl
        _fill(scr_dl, dl)
        _fill(scr_do, do)
        for d, dlo, doo in zip(dils, dl_outs, do_outs):
            _to_classes(scr_dl, dlo, d)
            _to_classes(scr_do, doo, d)

    blk = pl.BlockSpec((tm, D_ATT), lambda i: (i, 0))
    out = pl.pallas_call(
        body, name="att_delta", grid=(S // tm,), in_specs=[blk, blk],
        out_specs=[blk] + [_class_spec(tm, d, D_ATT) for d in dils] * 2,
        out_shape=[_sds((S, D_ATT), F32)] + [_sds((d, S // d, D_ATT), F32) for d in dils]
        + [_sds((d, S // d, D_ATT), BF16) for d in dils],
        scratch_shapes=[_row_scratch(tm, D_ATT), _row_scratch(tm, D_ATT)],
        compiler_params=_params("parallel"))(dac, att)
    delta = [out[0]] + [o.reshape(S, D_ATT) for o in out[1:1 + n_d]]
    do = [None] + [o.reshape(S, D_ATT) for o in out[1 + n_d:]]
    return delta, do


def _att_bwd(qk, v_src, do_src, lg, delta, d, *, name):
    S = qk.shape[0]
    L = S // d
    tq = min(ATT_BLOCK, L)
    nb, per, nh = L // tq, tq // HALF, L // HALF
    n_blocks = d * nb
    win = tq + 2 * HALF
    lead = tq - HALF
    acc_rows = lead + win
    (v_arr, v_col), (do_arr, do_col) = v_src, do_src

    def body(q_ref, kp_ref, kc_ref, kn_ref, vp_ref, vc_ref, vn_ref, do_ref, lg_ref, dl_ref,
             dq_ref, dk_ref, dv_ref, acc_k, acc_v):
        b = pl.program_id(0)
        i = lax.rem(jnp.minimum(b, n_blocks - 1), nb)

        @pl.when(b == 0)
        def _():
            acc_k[...] = jnp.zeros_like(acc_k)
            acc_v[...] = jnp.zeros_like(acc_v)

        @pl.when(b < n_blocks)
        def _():
            valid = _band_mask(i, tq, L)
            q, do = q_ref[...] * ATT_SCALE, do_ref[...]
            kwin = jnp.concatenate([kp_ref[...], kc_ref[...], kn_ref[...]], axis=0)
            vwin = jnp.concatenate([vp_ref[...], vc_ref[...], vn_ref[...]], axis=0)
            first, first_w = _first_head(tq), _first_head(win)
            groups = list(_lane_groups())
            cols = [c for ls in groups for c in (ls.start, ls.start + HEAD_DIM)]
            lanes = [ls for ls in groups for _ in range(2)]
            qh = [t for ls in groups for t in _split_pair(q[:, ls], first)]
            doh = [t for ls in groups for t in _split_pair(do[:, ls], first)]
            s = [jnp.where(valid, _nt(t, kwin[:, ls]), NEG_INF) for t, ls in zip(qh, lanes)]
            dp = [_nt(t, vwin[:, ls]) for t, ls in zip(doh, lanes)]
            p = [jnp.exp(t - lg_ref[:, c:c + 1]) for t, c in zip(s, cols)]
            ds = [(pp * (t - dl_ref[:, c:c + 1])).astype(BF16) for pp, t, c in zip(p, dp, cols)]
            dq = [jnp.dot(t, kwin[:, ls], preferred_element_type=F32) for t, ls in zip(ds, lanes)]
            dk = [_tn(t, q[:, ls]) for t, ls in zip(ds, lanes)]
            dv = [_tn(pp.astype(BF16), do[:, ls]) for pp, ls in zip(p, lanes)]
            for g, ls in enumerate(groups):
                dq_ref[:, ls] = (jnp.where(first, dq[2 * g], dq[2 * g + 1]) * ATT_SCALE).astype(dq_ref.dtype)
                acc_k[lead:, ls] += jnp.where(first_w, dk[2 * g], dk[2 * g + 1])
                acc_v[lead:, ls] += jnp.where(first_w, dv[2 * g], dv[2 * g + 1])

        for acc, out in ((acc_k, dk_ref), (acc_v, dv_ref)):
            out[...] = acc[0:tq, :].astype(out.dtype)
            kept = acc[tq:, :]
            acc[0:acc_rows - tq, :] = kept
            acc[acc_rows - tq:, :] = jnp.zeros((tq, D_ATT), F32)

    def seq(col):
        blk = lambda b: jnp.minimum(b, n_blocks - 1)
        cls = lambda b: (blk(b) // nb) * nh
        centre = pl.BlockSpec((tq, D_ATT), lambda b: (blk(b), col))
        prev = pl.BlockSpec((HALF, D_ATT), lambda b: (cls(b) + jnp.maximum((blk(b) % nb) * per - 1, 0), col))
        nxt = pl.BlockSpec((HALF, D_ATT), lambda b: (cls(b) + jnp.minimum((blk(b) % nb + 1) * per, nh - 1), col))
        return prev, centre, nxt

    _, qc, _ = seq(0)
    kp, kc, kn = seq(1)
    vp, vc, vn = seq(v_col)
    _, doc, _ = seq(do_col)
    late = pl.BlockSpec((tq, D_ATT), lambda b: (jnp.maximum(b - 1, 0), 0))
    return pl.pallas_call(
        body, name=name, grid=(n_blocks + 1,),
        in_specs=[qc, kp, kc, kn, vp, vc, vn, doc, qc, qc], out_specs=[qc, late, late],
        out_shape=[_sds((S, D_ATT), BF16)] * 3,
        scratch_shapes=[pltpu.VMEM((acc_rows, D_ATT), F32), pltpu.VMEM((acc_rows, D_ATT), F32)],
        compiler_params=_params("arbitrary"))(qk, qk, qk, qk, v_arr, v_arr, v_arr, do_arr, lg, delta)


def _sigmoid(x):
    return 1.0 / (1.0 + jnp.exp(-x))


def _halo_specs(S, T, width, col):
    last = S // HALO - 1
    per = T // HALO
    centre = pl.BlockSpec((T, width), lambda i: (i, col))
    prev = pl.BlockSpec((HALO, width), lambda i: (jnp.maximum(i * per - 1, 0), col))
    nxt = pl.BlockSpec((HALO, width), lambda i: (jnp.minimum((i + 1) * per, last), col))
    return prev, centre, nxt


def _window_scratch(T, C):
    return pltpu.VMEM((8, T + 2 * HALO, C), F32)


def _fill_window(buf, prev, centre, nxt, T):
    buf[0, 0:HALO, :] = prev
    buf[0, HALO:HALO + T, :] = centre
    buf[0, HALO + T:, :] = nxt
    rows = T + 2 * HALO - 8
    for s in range(1, 8):
        buf[s, 0:rows, :] = buf[0, s:s + rows, :]


def _tap_reads(buf, first_off, step, r0, ls):
    by_slab = {}
    for k in range(CONV_WIDTH):
        off = first_off + step * k
        by_slab.setdefault(off % 8, []).append((k, off - off % 8))
    for s, taps in by_slab.items():
        lo = min(a for _, a in taps)
        hi = max(a for _, a in taps)
        rows = buf[s, pl.ds(lo + r0, CONV_ROWS + hi - lo), ls]
        for k, a in taps:
            yield k, rows[a - lo:a - lo + CONV_ROWS]


def _depthwise(buf, w_ref, out_ref, T, C, first_off, step):
    def row_tile(t, carry):
        r0 = pl.multiple_of(t * CONV_ROWS, CONV_ROWS)
        for c0 in range(0, C, LANES):
            ls = slice(c0, c0 + LANES)
            acc = jnp.zeros((CONV_ROWS, LANES), F32)
            for k, rows in _tap_reads(buf, first_off, step, r0, ls):
                acc = acc + rows * w_ref[k:k + 1, ls]
            out_ref[pl.ds(r0, CONV_ROWS), ls] = acc
        return carry

    lax.fori_loop(0, T // CONV_ROWS, row_tile, 0)


def _conv_fwd(y, conv_w32, conv_b, ln_g, ln_b, *, T=512):
    S = y.shape[0]
    T = min(T, S)
    nblk = S // T
    C = D_CONV

    def body(ap, ac, an, gp, gc, gn, w_ref, b_ref, lg_ref, lb_ref, cv_ref, u1_ref, buf):
        i = pl.program_id(0)

        def glu(a_ref, g_ref):
            return a_ref[...].astype(F32) * _sigmoid(g_ref[...].astype(F32))

        _fill_window(buf, jnp.where(i > 0, glu(ap, gp), 0.0), glu(ac, gc),
                     jnp.where(i < nblk - 1, glu(an, gn), 0.0), T)
        _depthwise(buf, w_ref, u1_ref, T, C, HALO - CONV_PAD, 1)
        u1 = u1_ref[...] + b_ref[...]
        u1_ref[...] = u1
        mu = jnp.mean(u1, axis=-1, keepdims=True)
        xc = u1 - mu
        rstd = lax.rsqrt(jnp.mean(xc * xc, axis=-1, keepdims=True) + EPS)
        u2 = xc * rstd * lg_ref[...] + lb_ref[...]
        cv_ref[...] = (u2 * _sigmoid(u2)).astype(cv_ref.dtype)

    ap, ac, an = _halo_specs(S, T, C, 3)
    gp, gc, gn = _halo_specs(S, T, C, 4)
    vec = pl.BlockSpec((1, C), lambda i: (0, 0))
    out = pl.BlockSpec((T, C), lambda i: (i, 0))
    return pl.pallas_call(
        body, name="conv_fwd", grid=(nblk,),
        in_specs=[ap, ac, an, gp, gc, gn, pl.BlockSpec((32, C), lambda i: (0, 0)), vec, vec, vec],
        out_specs=[out, out], out_shape=[_sds((S, C), BF16), _sds((S, C), F32)],
        scratch_shapes=[_window_scratch(T, C)],
        compiler_params=_params("parallel"))(y, y, y, y, y, y, conv_w32, conv_b, ln_g, ln_b)


def _conv_bwd(dac, u1, y, conv_w32, ln_g, ln_b, *, T=512):
    S = y.shape[0]
    T = min(T, S)
    nblk = S // T
    C = D_CONV

    def body(dp, dc, dn, up, uc, un, ap, ac, an, gp, gc, gn, w_ref, lg_ref, lb_ref,
             dag_ref, dw_ref, dsm_ref, bufd, bufu, du0_scr, dw_acc):
        i = pl.program_id(0)
        lg = lg_ref[...]

        def du1_of(dcv_ref, u1_ref):
            u1 = u1_ref[...]
            mu = jnp.mean(u1, axis=-1, keepdims=True)
            xc = u1 - mu
            rstd = lax.rsqrt(jnp.mean(xc * xc, axis=-1, keepdims=True) + EPS)
            xhat = xc * rstd
            u2 = xhat * lg + lb_ref[...]
            sg = _sigmoid(u2)
            du2 = dcv_ref[...].astype(F32) * (sg * (1.0 + u2 * (1.0 - sg)))
            dxh = du2 * lg
            du1 = rstd * (dxh - jnp.mean(dxh, axis=-1, keepdims=True)
                          - xhat * jnp.mean(dxh * xhat, axis=-1, keepdims=True))
            return du1, du2, xhat

        def glu(a_ref, g_ref):
            return a_ref[...].astype(F32) * _sigmoid(g_ref[...].astype(F32))

        @pl.when(i == 0)
        def _():
            dw_ref[...] = jnp.zeros_like(dw_ref)
            dsm_ref[...] = jnp.zeros_like(dsm_ref)

        du1_c, du2_c, xhat_c = du1_of(dc, uc)
        dsm_ref[0:1, :] += jnp.sum(du1_c, axis=0, keepdims=True)
        dsm_ref[1:2, :] += jnp.sum(du2_c * xhat_c, axis=0, keepdims=True)
        dsm_ref[2:3, :] += jnp.sum(du2_c, axis=0, keepdims=True)
        _fill_window(bufd, jnp.where(i > 0, du1_of(dp, up)[0], 0.0), du1_c,
                     jnp.where(i < nblk - 1, du1_of(dn, un)[0], 0.0), T)
        _fill_window(bufu, jnp.where(i > 0, glu(ap, gp), 0.0), glu(ac, gc),
                     jnp.where(i < nblk - 1, glu(an, gn), 0.0), T)

        _depthwise(bufd, w_ref, du0_scr, T, C, HALO + CONV_PAD, -1)
        dw_acc[...] = jnp.zeros_like(dw_acc)

        def dw_tile(t, carry):
            r0 = pl.multiple_of(t * CONV_ROWS, CONV_ROWS)
            for c0 in range(0, C, LANES):
                ls = slice(c0, c0 + LANES)
                d = bufd[0, pl.ds(HALO + r0, CONV_ROWS), ls]
                for k, rows in _tap_reads(bufu, HALO - CONV_PAD, 1, r0, ls):
                    prod = d * rows
                    part = prod[0:8]
                    for j in range(8, CONV_ROWS, 8):
                        part = part + prod[j:j + 8]
                    dw_acc[k, :, ls] += part
            return carry

        lax.fori_loop(0, T // CONV_ROWS, dw_tile, 0)
        for k in range(CONV_WIDTH):
            dw_ref[k:k + 1, :] += jnp.sum(dw_acc[k], axis=0, keepdims=True)
        du0 = du0_scr[...]
        a = ac[...].astype(F32)
        sg = _sigmoid(gc[...].astype(F32))
        dag_ref[:, 0:C] = (du0 * sg).astype(dag_ref.dtype)
        dag_ref[:, C:] = (du0 * a * sg * (1.0 - sg)).astype(dag_ref.dtype)

    dp, dc, dn = _halo_specs(S, T, C, 1)
    up, uc, un = _halo_specs(S, T, C, 0)
    ap, ac, an = _halo_specs(S, T, C, 3)
    gp, gc, gn = _halo_specs(S, T, C, 4)
    vec = pl.BlockSpec((1, C), lambda i: (0, 0))
    return pl.pallas_call(
        body, name="conv_bwd", grid=(nblk,),
        in_specs=[dp, dc, dn, up, uc, un, ap, ac, an, gp, gc, gn,
                  pl.BlockSpec((32, C), lambda i: (0, 0)), vec, vec],
        out_specs=[pl.BlockSpec((T, 2 * C), lambda i: (i, 0)), pl.BlockSpec((32, C), lambda i: (0, 0)),
                   pl.BlockSpec((8, C), lambda i: (0, 0))],
        out_shape=[_sds((S, 2 * C), BF16), _sds((32, C), F32), _sds((8, C), F32)],
        scratch_shapes=[_window_scratch(T, C), _window_scratch(T, C), pltpu.VMEM((T, C), F32),
                        pltpu.VMEM((CONV_WIDTH, 8, C), F32)],
        compiler_params=_params("arbitrary"))(dac, dac, dac, u1, u1, u1, y, y, y, y, y, y, conv_w32, ln_g, ln_b)


def _xatt_fwd(xq, xk, xv, *, tm=512):
    S = xq.shape[0]
    M = xk.shape[0]
    tm = min(tm, S)
    scale = XATT_HEAD_DIM ** -0.5

    def body(q_ref, k_ref, v_ref, o_ref):
        heads = [slice(h * XATT_HEAD_DIM, (h + 1) * XATT_HEAD_DIM) for h in range(XATT_HEADS)]
        s = [_nt(q_ref[:, sl], k_ref[:, sl]) * scale for sl in heads]
        e = [jnp.exp(t - jnp.max(t, axis=-1, keepdims=True)) for t in s]
        p = [t * (1.0 / jnp.sum(t, axis=-1, keepdims=True)) for t in e]
        for sl, t in zip(heads, p):
            o_ref[:, sl] = jnp.dot(t.astype(BF16), v_ref[:, sl], preferred_element_type=F32).astype(o_ref.dtype)

    row = pl.BlockSpec((tm, D_MODEL), lambda i: (i, 0))
    full = pl.BlockSpec((M, D_MODEL), lambda i: (0, 0))
    return pl.pallas_call(
        body, name="xatt_fwd", grid=(S // tm,), in_specs=[row, full, full], out_specs=row,
        out_shape=_sds((S, D_MODEL), BF16), compiler_params=_params("parallel"))(xq, xk, xv)


def _xatt_bwd(xq, xk, xv, dxo, *, tm=512):
    S = xq.shape[0]
    M = xk.shape[0]
    tm = min(tm, S)
    scale = XATT_HEAD_DIM ** -0.5

    def body(q_ref, k_ref, v_ref, do_ref, dq_ref, dk_ref, dv_ref):
        i = pl.program_id(0)

        @pl.when(i == 0)
        def _():
            dk_ref[...] = jnp.zeros_like(dk_ref)
            dv_ref[...] = jnp.zeros_like(dv_ref)

        heads = [slice(h * XATT_HEAD_DIM, (h + 1) * XATT_HEAD_DIM) for h in range(XATT_HEADS)]
        s = [_nt(q_ref[:, sl], k_ref[:, sl]) * scale for sl in heads]
        dp = [_nt(do_ref[:, sl], v_ref[:, sl]) for sl in heads]
        e = [jnp.exp(t - jnp.max(t, axis=-1, keepdims=True)) for t in s]
        p = [t * (1.0 / jnp.sum(t, axis=-1, keepdims=True)) for t in e]
        ds = [(pp * (t - jnp.sum(t * pp, axis=-1, keepdims=True))).astype(BF16) for pp, t in zip(p, dp)]
        for sl, pp, t in zip(heads, p, ds):
            dq_ref[:, sl] = (jnp.dot(t, k_ref[:, sl], preferred_element_type=F32) * scale).astype(dq_ref.dtype)
            dv_ref[:, sl] += _tn(pp.astype(BF16), do_ref[:, sl])
            dk_ref[:, sl] += _tn(t, q_ref[:, sl]) * scale

    row = pl.BlockSpec((tm, D_MODEL), lambda i: (i, 0))
    full = pl.BlockSpec((M, D_MODEL), lambda i: (0, 0))
    return pl.pallas_call(
        body, name="xatt_bwd", grid=(S // tm,), in_specs=[row, full, full, row], out_specs=[row, full, full],
        out_shape=[_sds((S, D_MODEL), BF16), _sds((M, D_MODEL), F32), _sds((M, D_MODEL), F32)],
        compiler_params=_params("arbitrary"))(xq, xk, xv, dxo)


def _row_tile(R):
    for t in (128, 64, 32, 16):
        if R % t == 0 and R // t >= 4:
            return t
    return R


def _sum_partials(own, recv, me, *, name):
    _, R, C = own.shape
    t = _row_tile(R)

    def body(me_ref, own_ref, r_ref, o_ref):
        o_ref[...] = ((own_ref[...].astype(F32) + r_ref[0].astype(F32)) + r_ref[1].astype(F32)) + r_ref[2].astype(F32)

    return pl.pallas_call(
        body, name=name,
        grid_spec=pltpu.PrefetchScalarGridSpec(
            num_scalar_prefetch=1, grid=(R // t,),
            in_specs=[pl.BlockSpec((None, t, C), lambda i, me_ref: (me_ref[0], i, 0)),
                      pl.BlockSpec((3, t, C), lambda i, me_ref: (0, i, 0))],
            out_specs=pl.BlockSpec((t, C), lambda i, me_ref: (i, 0))),
        out_shape=_sds((R, C), F32), compiler_params=_params("parallel"))(me, own, recv)


def _adamw_math(w, g, m, v):
    m2 = ADAM_B1 * m + (1.0 - ADAM_B1) * g
    v2 = ADAM_B2 * v + (1.0 - ADAM_B2) * (g * g)
    m_hat = m2 / (1.0 - ADAM_B1 ** ADAM_STEP)
    v_hat = v2 / (1.0 - ADAM_B2 ** ADAM_STEP)
    delta = -ADAM_LR * (m_hat / (jnp.sqrt(v_hat) + ADAM_EPS) + ADAM_WD * w)
    return delta, m2, v2


def _adamw(parts, w, m, v, *, name):
    R, C = w.shape
    t = _row_tile(R)
    n = len(parts)

    def body(*refs):
        w_ref, m_ref, v_ref = refs[n:n + 3]
        g_ref, d_ref, m2_ref, v2_ref = refs[n + 3:]
        g = refs[0][...]
        for r in refs[1:n]:
            g = g + r[...]
        delta, m2, v2 = _adamw_math(w_ref[...], g, m_ref[...], v_ref[...])
        g_ref[...] = g
        d_ref[...] = delta
        m2_ref[...] = m2
        v2_ref[...] = v2

    blk = pl.BlockSpec((t, C), lambda i: (i, 0))
    return pl.pallas_call(
        body, name=name, grid=(R // t,), in_specs=[blk] * (n + 3), out_specs=[blk] * 4,
        out_shape=[_sds((R, C), F32)] * 4, compiler_params=_params("parallel"))(*parts, w, m, v)


def _adamw_small(gathered, chip, entries):
    _, R, C = gathered.shape
    n = len(entries)
    group = D_CONV // N_CHIPS

    def body(chip_ref, g_ref, *refs):
        ins, outs, tot_ref = refs[:3 * n], refs[3 * n:7 * n], refs[7 * n]
        tot = g_ref[0]
        for k in range(1, N_DEV):
            tot = tot + g_ref[k]
        tot_ref[...] = tot
        for e, ((kind, r), _, _, _) in enumerate(entries):
            if kind == "row":
                g = tot_ref[r:r + 1, :]
            elif kind == "gain":
                g = jnp.concatenate([tot_ref[r:r + 1, :], tot_ref[r + 1:r + 2, :]], axis=1)
            else:
                g = tot_ref[r:r + CONV_WIDTH, 0:group]
                for j in range(1, N_CHIPS):
                    g = jnp.where(chip_ref[0] == j, tot_ref[r:r + CONV_WIDTH, j * group:(j + 1) * group], g)
            delta, m2, v2 = _adamw_math(ins[3 * e][...], g, ins[3 * e + 1][...], ins[3 * e + 2][...])
            for o, val in zip(outs[4 * e:4 * e + 4], (g, delta, m2, v2)):
                o[...] = val

    whole = lambda a: pl.BlockSpec(a.shape, lambda i, c: (0,) * a.ndim)
    arrays = [a for _, w, m, v in entries for a in (w, m, v)]
    out_like = [w for _, w, _, _ in entries for _ in range(4)]
    tot_like = _sds((R, C), F32)
    out = pl.pallas_call(
        body, name="adamw_small",
        grid_spec=pltpu.PrefetchScalarGridSpec(
            num_scalar_prefetch=1, grid=(1,),
            in_specs=[whole(gathered)] + [whole(a) for a in arrays],
            out_specs=[whole(a) for a in out_like] + [whole(tot_like)]),
        out_shape=[_sds(a.shape, F32) for a in out_like] + [tot_like],
        compiler_params=_params("arbitrary"))(chip, gathered, *arrays)
    return out[-1], [tuple(out[4 * e:4 * e + 4]) for e in range(n)]


def _chip_peers():
    x, y = lax.axis_index("x"), lax.axis_index("y")
    return [(1 - x, y), (x, 1 - y), (1 - x, 1 - y)]


HBM_SPEC = pl.BlockSpec(memory_space=pltpu.HBM)
SEM_SPEC = pl.BlockSpec(memory_space=pltpu.SEMAPHORE)


def _exchange_peers(mode):
    x, y, c = lax.axis_index("x"), lax.axis_index("y"), lax.axis_index("c")
    if mode == "swap":
        return [(x, y, 1 - c)]
    if mode == "all":
        flips = [(fx, fy, fc) for fx in (0, 1) for fy in (0, 1) for fc in (0, 1)][1:]
        return [(1 - x if fx else x, 1 - y if fy else y, 1 - c if fc else c) for fx, fy, fc in flips]
    return [(px, py, c) for px, py in _chip_peers()]


def _exchange_start(mode, srcs, zones, *, name):
    n = len(srcs)

    def body(*refs):
        ins, lands = refs[:n], refs[n:2 * n]
        send_sems, recv_sems = refs[2 * n:3 * n], refs[3 * n:4 * n]
        token = refs[-1]
        x, y, c = lax.axis_index("x"), lax.axis_index("y"), lax.axis_index("c")
        mine = 2 * x + y if mode == "gather" else 4 * x + 2 * y + c
        for t in range(n):
            for k, (px, py, pc) in enumerate(_exchange_peers(mode)):
                if mode in ("gather", "all"):
                    s, d = ins[t], lands[t].at[mine]
                elif mode == "scatter":
                    s, d = ins[t].at[2 * px + py], lands[t].at[k]
                else:
                    s, d = ins[t], lands[t]
                pltpu.make_async_remote_copy(src_ref=s, dst_ref=d, send_sem=send_sems[t], recv_sem=recv_sems[t],
                                             device_id=(px, py, pc), device_id_type=MESH).start()
            if mode in ("gather", "all"):
                pltpu.make_async_copy(ins[t], lands[t].at[mine], send_sems[t]).start()
        token[...] = jnp.zeros_like(token)

    hbm = lambda a: pltpu.with_memory_space_constraint(a, pltpu.HBM)
    out = pl.pallas_call(
        body, name=name,
        in_specs=[HBM_SPEC] * (2 * n),
        out_specs=[SEM_SPEC] * (2 * n) + [HBM_SPEC] * (2 * n) + [pl.BlockSpec(memory_space=pltpu.VMEM)],
        out_shape=[pltpu.SemaphoreType.DMA(())] * (2 * n)
        + [pltpu.HBM(a.shape, a.dtype) for a in list(srcs) + list(zones)] + [_sds((8, LANES), F32)],
        input_output_aliases={i: 2 * n + i for i in range(2 * n)},
        compiler_params=pltpu.CompilerParams(has_side_effects=pltpu.SideEffectType.DATAFLOW_SIDE_EFFECTING),
    )(*[hbm(a) for a in list(srcs) + list(zones)])
    return out[:n], out[n:2 * n], out[2 * n:3 * n], out[3 * n:4 * n], out[-1]


def _exchange_wait(mode, started, after, *, name):
    send_sems, recv_sems, srcs, zones, _ = started
    n = len(srcs)
    afters = tuple(after) if isinstance(after, (tuple, list)) else (after,)

    def body(*refs):
        lands = refs[n:2 * n]
        send_refs, recv_refs = refs[2 * n:3 * n], refs[3 * n:4 * n]
        me = (lax.axis_index("x"), lax.axis_index("y"), lax.axis_index("c"))
        n_remote = {"gather": N_CHIPS - 1, "scatter": N_CHIPS - 1, "all": N_DEV - 1, "swap": 1}[mode]
        for t in range(n):
            got = lands[t] if mode == "swap" else lands[t].at[pl.ds(0, n_remote)]
            sent = lands[t] if mode in ("gather", "all") else got
            pltpu.make_async_remote_copy(src_ref=sent, dst_ref=sent, send_sem=send_refs[t], recv_sem=recv_refs[t],
                                         device_id=me, device_id_type=MESH).wait_send()
            pltpu.make_async_remote_copy(src_ref=got, dst_ref=got, send_sem=send_refs[t], recv_sem=recv_refs[t],
                                         device_id=me, device_id_type=MESH).wait_recv()

    out = pl.pallas_call(
        body, name=name,
        in_specs=[HBM_SPEC] * (2 * n) + [SEM_SPEC] * (2 * n) + [pl.BlockSpec(memory_space=pl.ANY)] * len(afters),
        out_specs=[HBM_SPEC] * (2 * n),
        out_shape=[pltpu.HBM(a.shape, a.dtype) for a in list(srcs) + list(zones)],
        input_output_aliases={i: i for i in range(2 * n)},
        compiler_params=pltpu.CompilerParams(has_side_effects=pltpu.SideEffectType.DATAFLOW_SIDE_EFFECTING),
    )(*srcs, *zones, *send_sems, *recv_sems, *afters)
    return out[:n], out[n:]


def _swap_with_sibling(parts):
    n = len(parts)

    def body(*refs):
        ins, outs = refs[:n], refs[n:2 * n]
        send_sems, recv_sems = refs[2 * n:]
        sib = (lax.axis_index("x"), lax.axis_index("y"), 1 - lax.axis_index("c"))
        cps = []
        for t in range(n):
            cp = pltpu.make_async_remote_copy(
                src_ref=ins[t], dst_ref=outs[t], send_sem=send_sems.at[t], recv_sem=recv_sems.at[t],
                device_id=sib, device_id_type=MESH)
            cp.start()
            cps.append(cp)
        for cp in cps:
            cp.wait()

    any_spec = pl.BlockSpec(memory_space=pl.ANY)
    return pl.pallas_call(
        body, name="swap_with_sibling", in_specs=[any_spec] * n, out_specs=[any_spec] * n,
        out_shape=[_sds(p.shape, p.dtype) for p in parts],
        scratch_shapes=[pltpu.SemaphoreType.DMA((n,)), pltpu.SemaphoreType.DMA((n,))])(*parts)


BIG = ("w_in", "w_out", "w_xq", "w_xk", "w_xv", "w_xo", "w_up", "w_down")
COL_SHARDED = ("w_in", "w_up")


def _as_matrix(name, w4):
    if name in COL_SHARDED:
        return w4
    return w4.reshape(1, w4.shape[0] * w4.shape[1], w4.shape[2])


def _shard_layout(name, g):
    if name in COL_SHARDED:
        return g
    return g.reshape(N_CHIPS, g.shape[0] * g.shape[1] // N_CHIPS, g.shape[2])


def _local_step(x, mem, target, vecs, comm):
    S = x.shape[0]
    tables = _rope_tables(S)

    xn = _rms_fwd(x, vecs["norm_mix_g"], name="rms_mix")
    w_in, conv_w32 = comm["first"]((xn,) + tuple(tables))
    y = _mm_nn(xn, w_in, name="mm_in", tm=2048, group=2)
    qk, v_perm = _rope_fwd(y, tables)
    v_src = [(y, 2)] + [(v, 0) for v in v_perm[1:]]
    outs, lses = zip(*[_att_fwd(qk[p], v_src[p], d, name=f"att_fwd_d{d}") for p, d in enumerate(DILATIONS)])
    att, lg = _att_combine(outs, lses)
    cv, u1 = _conv_fwd(y, conv_w32, vecs["conv_b"], vecs["conv_ln_g"], vecs["conv_ln_b"])
    Wm = {k: _as_matrix(k, v) for k, v in comm["rest"]((att, cv)).items()}
    Wm["w_in"] = w_in
    h1, hn = _mm_rows((att, cv), Wm["w_out"], _residual_norm_tail, name="mm_out_rms", rows_in=(x,),
                      vecs_in=(vecs["norm_x_g"],), rows_out=(F32, BF16))
    xq = _mm_nn(hn, Wm["w_xq"], name="mm_xq")
    mn = _rms_fwd(mem, vecs["norm_mem_g"], name="rms_mem")
    xk = _mm_nn(mn, Wm["w_xk"], name="mm_xk")
    xv = _mm_nn(mn, Wm["w_xv"], name="mm_xv")
    xo = _xatt_fwd(xq, xk, xv)
    h2, hm = _mm_rows(xo, Wm["w_xo"], _residual_norm_tail, name="mm_xo_rms", rows_in=(h1,),
                      vecs_in=(vecs["norm_mlp_g"],), rows_out=(F32, BF16))
    relu_up = _mm_nn(hm, Wm["w_up"], name="mm_up", relu=True, tm=2048)
    sums = ((8, D_MODEL),)
    dh3, dh3b, dg_final, loss = _mm_rows(
        relu_up, Wm["w_down"], _loss_tail, name="mm_down_loss", rows_in=(h2, target), vecs_in=(vecs["norm_final_g"],),
        rows_out=(F32, BF16), sums_out=sums + ((8, LANES),), a_squared=True, tm=256)
    g = {}
    g["w_down"] = _mm_tn(relu_up, dh3b, 1, name="dw_down", a_squared=True)
    dup = _mm_nt(dh3b, Wm["w_down"], name="d_act", out_dtype=BF16, mul=relu_up, tm=2048)
    g["w_up"] = _mm_tn(hm, dup, N_CHIPS, name="dw_up")
    sent = comm["send_mlp"]({k: _shard_layout(k, g[k]) for k in ("w_down", "w_up")})
    dh2, dh2b, dg_mlp = _mm_rows(
        dup, Wm["w_up"], _rms_bwd_tail(True), name="d_hm_rms", w_transposed=True, rows_in=(h2, dh3),
        vecs_in=(vecs["norm_mlp_g"] + sent[0:1, 0:1],), rows_out=(F32, BF16), sums_out=sums, tm=256)
    g["w_xo"] = _mm_tn(xo, dh2b, 1, name="dw_xo")
    dxo = _mm_nt(dh2b, Wm["w_xo"], name="d_xo", out_dtype=BF16)
    dxq, dxk, dxv = _xatt_bwd(xq, xk, xv, dxo)
    g["w_xq"] = _mm_tn(hn, dxq, 1, name="dw_xq")
    dh1, dh1b, dg_x = _mm_rows(
        dxq, Wm["w_xq"], _rms_bwd_tail(True), name="d_hn_rms", w_transposed=True, rows_in=(h1, dh2),
        vecs_in=(vecs["norm_x_g"],), rows_out=(F32, BF16), sums_out=sums)
    dxkb, dxvb = dxk.astype(BF16), dxv.astype(BF16)
    g["w_xk"] = _mm_tn(mn, dxkb, 1, name="dw_xk")
    g["w_xv"] = _mm_tn(mn, dxvb, 1, name="dw_xv")
    dmn = _mm_nt(jnp.concatenate([dxkb, dxvb], axis=1),
                 jnp.concatenate([Wm["w_xk"], Wm["w_xv"]], axis=2), name="d_mn", out_dtype=BF16)
    _, _, dg_mem = _rms_bwd(dmn, mem, vecs["norm_mem_g"], None, name="rms_bwd_mem", bf16_copy=False)
    g["w_out"] = jnp.concatenate([_mm_tn(att, dh1b, 1, name="dw_out_att"), _mm_tn(cv, dh1b, 1, name="dw_out_conv")],
                                 axis=1)
    sent = comm["send_att"]({k: _shard_layout(k, g[k]) for k in ("w_out", "w_xq", "w_xk", "w_xv", "w_xo")})
    dac = _mm_nt(dh1b, Wm["w_out"], name="d_mix", out_dtype=BF16)
    dag, dconv_w, dconv_small = _conv_bwd(dac, u1, y, conv_w32, vecs["conv_ln_g"] + sent[0:1, 0:1],
                                          vecs["conv_ln_b"])
    delta, do_perm = _att_delta(dac, att)
    do_src = [(dac, 0)] + [(t, 0) for t in do_perm[1:]]
    dq, dk, dv = zip(*[_att_bwd(qk[p], v_src[p], do_src[p], lg[p], delta[p], d, name=f"att_bwd_d{d}")
                       for p, d in enumerate(DILATIONS)])
    dy = _assemble_dy(dq, dk, dv, dag, tables)
    sent = comm["send_in"]({"w_in": _mm_tn(xn, dy, N_CHIPS, name="dw_in", group=2)})
    grad_x, dg_mix = _mm_rows(
        dy, Wm["w_in"], _rms_bwd_tail(False), name="d_xn_rms", w_transposed=True, rows_in=(x, dh1),
        vecs_in=(vecs["norm_mix_g"] + sent[0:1, 0:1],), rows_out=(F32,), sums_out=sums)

    small = dict(conv_w=dconv_w, conv_small=dconv_small, norm_mix_g=dg_mix, norm_x_g=dg_x, norm_mem_g=dg_mem,
                 norm_mlp_g=dg_mlp, norm_final_g=dg_final, loss=loss)
    return grad_x, small


SMALL_ORDER = ("conv_w", "conv_small", "norm_mix_g", "norm_x_g", "norm_mem_g", "norm_mlp_g", "norm_final_g", "loss")


def _pack_small(small):
    rows, offs, pos = [], {}, 0
    for k in SMALL_ORDER:
        a = small[k]
        a = a.reshape(a.shape[0] * a.shape[1] // SMALL_W, SMALL_W)
        pad = (-a.shape[0]) % 8
        if pad:
            a = jnp.pad(a, ((0, pad), (0, 0)))
        rows.append(a)
        offs[k] = pos
        pos += a.shape[0]
    return jnp.concatenate(rows, axis=0), offs


def kernel(x, mem, norm_mix_g, w_in, conv_w, conv_b, conv_ln_g, conv_ln_b, w_out, norm_x_g, norm_mem_g, w_xq, w_xk, w_xv, w_xo, norm_mlp_g, w_up, w_down, norm_final_g, loss_target, m_norm_mix_g, m_w_in, m_conv_w, m_conv_b, m_conv_ln_g, m_conv_ln_b, m_w_out, m_norm_x_g, m_norm_mem_g, m_w_xq, m_w_xk, m_w_xv, m_w_xo, m_norm_mlp_g, m_w_up, m_w_down, m_norm_final_g, v_norm_mix_g, v_w_in, v_conv_w, v_conv_b, v_conv_ln_g, v_conv_ln_b, v_w_out, v_norm_x_g, v_norm_mem_g, v_w_xq, v_w_xk, v_w_xv, v_w_xo, v_norm_mlp_g, v_w_up, v_w_down, v_norm_final_g):
    names = ("norm_mix_g", "w_in", "conv_w", "conv_b", "conv_ln_g", "conv_ln_b", "w_out", "norm_x_g", "norm_mem_g",
             "w_xq", "w_xk", "w_xv", "w_xo", "norm_mlp_g", "w_up", "w_down", "norm_final_g")
    wts = dict(zip(names, (norm_mix_g, w_in, conv_w, conv_b, conv_ln_g, conv_ln_b, w_out, norm_x_g, norm_mem_g,
                           w_xq, w_xk, w_xv, w_xo, norm_mlp_g, w_up, w_down, norm_final_g)))
    mom = dict(zip(names, (m_norm_mix_g, m_w_in, m_conv_w, m_conv_b, m_conv_ln_g, m_conv_ln_b, m_w_out, m_norm_x_g,
                           m_norm_mem_g, m_w_xq, m_w_xk, m_w_xv, m_w_xo, m_norm_mlp_g, m_w_up, m_w_down, m_norm_final_g)))
    var = dict(zip(names, (v_norm_mix_g, v_w_in, v_conv_w, v_conv_b, v_conv_ln_g, v_conv_ln_b, v_w_out, v_norm_x_g,
                           v_norm_mem_g, v_w_xq, v_w_xk, v_w_xv, v_w_xo, v_norm_mlp_g, v_w_up, v_w_down, v_norm_final_g)))
    chip = 2 * lax.axis_index("x") + lax.axis_index("y")

    def zone(shard):
        return lax.empty((N_CHIPS,) + shard.shape, shard.dtype)

    conv_w_pad = jnp.pad(wts["conv_w"][0], ((0, 1), (0, 0)))
    first_shards = [wts["w_in"][0].astype(BF16), conv_w_pad]
    gathering_first = _exchange_start("gather", first_shards, [zone(s) for s in first_shards],
                                      name="gather_first_start")
    rest = tuple(k for k in BIG if k != "w_in")
    behind_first = gathering_first[4][0, 0]
    rest_shards = [(wts[k][0] + behind_first).astype(BF16) for k in rest]
    gathering = gathering_rest = _exchange_start("gather", rest_shards, [zone(s) for s in rest_shards],
                                                 name="gather_rest_start")
    sending = {}

    def wait_first(after):
        _, (w_in_all, conv_w_all) = _exchange_wait("gather", gathering_first, after, name="gather_first_wait")
        return w_in_all, jnp.transpose(conv_w_all, (1, 0, 2)).reshape(32, D_CONV)

    def wait_rest(after):
        _, zones = _exchange_wait("gather", gathering_rest, after, name="gather_rest_wait")
        return dict(zip(rest, zones))

    def send(group, grads):
        keys = tuple(grads)
        zones = [lax.empty((N_CHIPS - 1,) + grads[k].shape[1:], grads[k].dtype) for k in keys]
        sending[group] = (keys, _exchange_start("scatter", [grads[k] for k in keys], zones,
                                                name=f"scatter_{group}_start"))
        return sending[group][1][4]

    comm = dict(first=wait_first, rest=wait_rest, send_mlp=lambda grads: send("mlp", grads),
                send_att=lambda grads: send("att", grads), send_in=lambda grads: send("in", grads))
    vecs = {k: wts[k] for k in ("conv_b", "conv_ln_g", "conv_ln_b", "norm_x_g", "norm_mem_g", "norm_mlp_g")}
    vecs["norm_mix_g"] = wts["norm_mix_g"] + gathering[4][0:1, 0:1]
    vecs["norm_final_g"] = wts["norm_final_g"].reshape(1, D_MODEL)
    grad_x, small = _local_step(x[0], mem[0], loss_target[0], vecs, comm)

    packed, offs = _pack_small(small)
    me_arr = jnp.reshape(chip, (1,)).astype(jnp.int32)
    gathering_small = _exchange_start("all", [packed], [lax.empty((N_DEV,) + packed.shape, packed.dtype)],
                                      name="allgather_small_start")
    sums = {}

    def settle(group, after):
        keys, started = sending[group]
        srcs, zones = _exchange_wait("scatter", started, after, name=f"scatter_{group}_wait")
        for k, own, got in zip(keys, srcs, zones):
            sums[k] = _sum_partials(own, got, me_arr, name=f"sum_{k}")

    settle("mlp", gathering_small[4])
    settle("att", gathering_small[4])
    early = tuple(sums)
    swapping = _exchange_start("swap", [sums[k] for k in early], [lax.empty(sums[k].shape, F32) for k in early],
                               name="swap_early_start")
    settle("in", swapping[4])
    _, (gath,) = _exchange_wait("all", gathering_small, sums["w_in"], name="allgather_small_wait")

    where = {"conv_w": ("conv_w", offs["conv_w"]), "conv_b": ("row", offs["conv_small"]),
             "conv_ln_g": ("row", offs["conv_small"] + 1), "conv_ln_b": ("row", offs["conv_small"] + 2)}
    where.update({k: ("gain", offs[k]) for k in ("norm_mix_g", "norm_x_g", "norm_mem_g", "norm_mlp_g", "norm_final_g")})
    as_2d = lambda a: a.reshape(a.shape[-2] if a.ndim > 1 else 1, a.shape[-1])
    tot_small, updates = _adamw_small(gath, me_arr, [(where[k], as_2d(wts[k]), as_2d(mom[k]), as_2d(var[k]))
                                                     for k in where])
    res = dict(zip(where, updates))
    loss = tot_small[offs["loss"], 0]

    sib = {"w_in": _swap_with_sibling([sums["w_in"]])[0]}
    mine_early, sib_early = _exchange_wait("swap", swapping, sib["w_in"], name="swap_early_wait")
    sums.update(zip(early, mine_early))
    sib.update(zip(early, sib_early))
    for k in BIG:
        res[k] = _adamw([sums[k], sib[k]], wts[k][0], mom[k][0], var[k][0], name=f"adamw_{k}")

    outs = [loss, grad_x[None]]
    for j in range(4):
        outs += [res[k][j].reshape(wts[k].shape) for k in names]
    return tuple(outs)
```

```python
import jax
import jax.numpy as jnp
from jax import lax
from jax.experimental import pallas as pl
from jax.experimental.pallas import tpu as pltpu

F32 = jnp.float32
BF16 = jnp.bfloat16
MESH = pl.DeviceIdType.MESH

D_MODEL = 1024
ATT_HEADS = 8
HEAD_DIM = 64
D_ATT = ATT_HEADS * HEAD_DIM
D_CONV = D_MODEL - D_ATT
DILATIONS = (1, 4, 16)
HALF = 64
ROPE_THETA = 500000.0
ROT_DIM = HEAD_DIM // 4
CONV_WIDTH = 31
CONV_PAD = (CONV_WIDTH - 1) // 2
XATT_HEADS = 4
XATT_HEAD_DIM = D_MODEL // XATT_HEADS
D_FF = 4 * D_MODEL
D_IN = 3 * D_ATT + 2 * D_CONV
EPS = 1e-6
NEG_INF = -1e30
N_CHIPS = 4
N_DEV = 8

ADAM_LR = 0.001
ADAM_B1 = 0.9
ADAM_B2 = 0.999
ADAM_EPS = 1e-08
ADAM_WD = 0.01
ADAM_STEP = 10

VMEM_LIMIT_V7X = 56 * 1024 * 1024
LANES = 128
HALO = 16
CONV_ROWS = 64
ATT_BLOCK = 128
SMALL_W = 512


def _params(*sem):
    return pltpu.CompilerParams(dimension_semantics=sem, vmem_limit_bytes=VMEM_LIMIT_V7X)


def _sds(shape, dtype):
    return jax.ShapeDtypeStruct(shape, dtype)


def _squared(a):
    af = a.astype(F32)
    return (af * af).astype(BF16)


def _mm_nn(a, w3, *, name, out_dtype=BF16, relu=False, group=1, tm=1024, tn=None, tk=1024):
    M, K = a.shape
    nsh, _, n = w3.shape
    tm, tk = min(tm, M), min(tk, K)
    tn = group * n if group > 1 else (tn or min(n, 1024))
    npt, nk = max(n // tn, 1), K // tk
    nj, N = nsh * npt // group, nsh * n

    def body(a_ref, w_ref, out_ref, *scratch):
        acc_ref = scratch[0] if nk > 1 else None

        def finish(acc):
            if relu:
                acc = jnp.maximum(acc, 0.0)
            out_ref[...] = acc.astype(out_ref.dtype)

        a_val = a_ref[...]
        w_val = w_ref[...] if group == 1 else jnp.concatenate([w_ref[s] for s in range(group)], axis=1)
        part = jnp.dot(a_val, w_val, preferred_element_type=F32)
        if nk == 1:
            finish(part)
        else:
            k = pl.program_id(2)

            @pl.when(k == 0)
            def _():
                acc_ref[...] = part

            @pl.when(k > 0)
            def _():
                acc_ref[...] += part

            @pl.when(k == nk - 1)
            def _():
                finish(acc_ref[...])

    w_spec = (pl.BlockSpec((None, tk, tn), lambda i, j, k: (j // npt, k, j % npt)) if group == 1 else
              pl.BlockSpec((group, tk, n), lambda i, j, k: (j, k, 0)))
    return pl.pallas_call(
        body, name=name, grid=(M // tm, nj, nk),
        in_specs=[pl.BlockSpec((tm, tk), lambda i, j, k: (i, k)), w_spec],
        out_specs=pl.BlockSpec((tm, tn), lambda i, j, k: (i, j)), out_shape=_sds((M, N), out_dtype),
        scratch_shapes=[pltpu.VMEM((tm, tn), F32)] if nk > 1 else [],
        compiler_params=_params("parallel", "parallel", "arbitrary"))(a, w3)


def _mm_nt(dy, w3, *, name, out_dtype=F32, mul=None, tm=1024, tn=None, tko=1024):
    M, N = dy.shape
    nsh, K, n = w3.shape
    tm, tko = min(tm, M), min(tko, K)
    tn = tn or min(n, 1024)
    npt = n // tn
    nj = nsh * npt

    def body(*refs):
        dy_ref, w_ref = refs[0], refs[1]
        pos = 2
        mul_ref = None
        if mul is not None:
            mul_ref = refs[pos]
            pos += 1
        out_ref = refs[pos]
        acc_ref = refs[pos + 1] if nj > 1 else None

        def finish(acc):
            if mul_ref is not None:
                acc = acc * (2.0 * mul_ref[...].astype(F32))
            out_ref[...] = acc.astype(out_ref.dtype)

        part = lax.dot_general(dy_ref[...], w_ref[...], (((1,), (1,)), ((), ())), preferred_element_type=F32)
        if nj == 1:
            finish(part)
        else:
            j = pl.program_id(2)

            @pl.when(j == 0)
            def _():
                acc_ref[...] = part

            @pl.when(j > 0)
            def _():
                acc_ref[...] += part

            @pl.when(j == nj - 1)
            def _():
                finish(acc_ref[...])

    in_specs = [pl.BlockSpec((tm, tn), lambda i, ko, j: (i, j)),
                pl.BlockSpec((None, tko, tn), lambda i, ko, j: (j // npt, ko, j % npt))]
    args = [dy, w3]
    if mul is not None:
        in_specs.append(pl.BlockSpec((tm, tko), lambda i, ko, j: (i, ko)))
        args.append(mul)
    return pl.pallas_call(
        body, name=name, grid=(M // tm, K // tko, nj), in_specs=in_specs,
        out_specs=pl.BlockSpec((tm, tko), lambda i, ko, j: (i, ko)), out_shape=_sds((M, K), out_dtype),
        scratch_shapes=[pltpu.VMEM((tm, tko), F32)] if nj > 1 else [],
        compiler_params=_params("parallel", "parallel", "arbitrary"))(*args)


def _mm_tn(a, dy, nsh, *, name, out_dtype=BF16, a_squared=False, group=1, tm=2048, tk=1024, tn=None):
    M, K = a.shape
    N = dy.shape[1]
    n = N // nsh
    tm, tk = min(tm, M), min(tk, K)
    tn = group * n if group > 1 else (tn or min(n, 1024))
    npt = max(n // tn, 1)
    nj, nm = nsh * npt // group, M // tm

    def body(a_ref, dy_ref, out_ref, acc_ref):
        m = pl.program_id(2)
        a_val = _squared(a_ref[...]) if a_squared else a_ref[...]
        part = lax.dot_general(a_val, dy_ref[...], (((0,), (0,)), ((), ())), preferred_element_type=F32)

        @pl.when(m == 0)
        def _():
            acc_ref[...] = part

        @pl.when(m > 0)
        def _():
            acc_ref[...] += part

        @pl.when(m == nm - 1)
        def _():
            if group == 1:
                out_ref[...] = acc_ref[...].astype(out_ref.dtype)
            else:
                for s in range(group):
                    out_ref[s] = acc_ref[:, s * n:(s + 1) * n].astype(out_ref.dtype)

    out_spec = (pl.BlockSpec((None, tk, tn), lambda kk, j, m: (j // npt, kk, j % npt)) if group == 1 else
                pl.BlockSpec((group, tk, n), lambda kk, j, m: (j, kk, 0)))
    return pl.pallas_call(
        body, name=name, grid=(K // tk, nj, nm),
        in_specs=[pl.BlockSpec((tm, tk), lambda kk, j, m: (m, kk)),
                  pl.BlockSpec((tm, tn), lambda kk, j, m: (m, j))],
        out_specs=out_spec,
        out_shape=_sds((nsh, K, n), out_dtype),
        scratch_shapes=[pltpu.VMEM((tk, tn), F32)],
        compiler_params=_params("parallel", "parallel", "arbitrary"))(a, dy)


def _rms_fwd(x, g, *, name, tm=512):
    M, Dm = x.shape
    tm = min(tm, M)

    def body(x_ref, g_ref, o_ref):
        xf = x_ref[...]
        r = lax.rsqrt(jnp.mean(xf * xf, axis=-1, keepdims=True) + EPS)
        o_ref[...] = (xf * r * g_ref[...]).astype(o_ref.dtype)

    return pl.pallas_call(
        body, name=name, grid=(M // tm,),
        in_specs=[pl.BlockSpec((tm, Dm), lambda i: (i, 0)), pl.BlockSpec((1, Dm), lambda i: (0, 0))],
        out_specs=pl.BlockSpec((tm, Dm), lambda i: (i, 0)), out_shape=_sds((M, Dm), BF16),
        compiler_params=_params("parallel"))(x, g)


def _rms_bwd(dxn, x, g, dres, *, name, bf16_copy=True, tm=512):
    M, Dm = x.shape
    tm = min(tm, M)
    has_res = dres is not None

    def body(*refs):
        dxn_ref, x_ref, g_ref = refs[:3]
        dres_ref = refs[3] if has_res else None
        dx_ref, dg_ref = refs[-1 - 1 - bf16_copy], refs[-1]
        dxb_ref = refs[-2] if bf16_copy else None
        i = pl.program_id(0)
        xf = x_ref[...]
        r = lax.rsqrt(jnp.mean(xf * xf, axis=-1, keepdims=True) + EPS)
        nrm = xf * r
        dxn_f = dxn_ref[...].astype(F32)
        dn = dxn_f * g_ref[...]
        dx = r * (dn - nrm * jnp.mean(dn * nrm, axis=-1, keepdims=True))
        if has_res:
            dx = dx + dres_ref[...]
        dx_ref[...] = dx
        if bf16_copy:
            dxb_ref[...] = dx.astype(dxb_ref.dtype)

        @pl.when(i == 0)
        def _():
            dg_ref[...] = jnp.zeros_like(dg_ref)

        dg_ref[0:1, :] += jnp.sum(dxn_f * nrm, axis=0, keepdims=True)

    row = pl.BlockSpec((tm, Dm), lambda i: (i, 0))
    in_specs = [row, row, pl.BlockSpec((1, Dm), lambda i: (0, 0))] + ([row] if has_res else [])
    args = [dxn, x, g] + ([dres] if has_res else [])
    out = pl.pallas_call(
        body, name=name, grid=(M // tm,), in_specs=in_specs,
        out_specs=[row] * (1 + bf16_copy) + [pl.BlockSpec((8, Dm), lambda i: (0, 0))],
        out_shape=[_sds((M, Dm), F32)] + [_sds((M, Dm), BF16)] * bf16_copy + [_sds((8, Dm), F32)],
        compiler_params=_params("arbitrary"))(*args)
    return out[0], (out[1] if bf16_copy else None), out[-1]


def _mm_rows(a, w3, tail, *, name, rows_in=(), vecs_in=(), rows_out=(), sums_out=(), a_squared=False,
             w_transposed=False, tm=512):
    parts = a if isinstance(a, (tuple, list)) else (a,)
    M = parts[0].shape[0]
    K, N = (w3.shape[0] * w3.shape[2], w3.shape[1]) if w_transposed else (w3.shape[1], w3.shape[2])
    tm = min(tm, M)
    n_a, n_ri, n_vi, n_ro = len(parts), len(rows_in), len(vecs_in), len(rows_out)

    def body(*refs):
        a_refs, w_ref, refs = refs[:n_a], refs[n_a], refs[n_a + 1:]
        rin, vin = refs[:n_ri], refs[n_ri:n_ri + n_vi]
        rout, sout = refs[n_ri + n_vi:n_ri + n_vi + n_ro], refs[n_ri + n_vi + n_ro:]

        @pl.when(pl.program_id(0) == 0)
        def _():
            for s in sout:
                s[...] = jnp.zeros_like(s)

        a_val = a_refs[0][...] if n_a == 1 else jnp.concatenate([r[...] for r in a_refs], axis=1)
        if a_squared:
            a_val = _squared(a_val)
        if w_transposed:
            n = w3.shape[2]
            prod = _nt(a_val[:, 0:n], w_ref[0])
            for j in range(1, w3.shape[0]):
                prod = prod + _nt(a_val[:, j * n:(j + 1) * n], w_ref[j])
        else:
            prod = jnp.dot(a_val, w_ref[0], preferred_element_type=F32)
        tail(prod, rin, vin, rout, sout)

    row = pl.BlockSpec((tm, N), lambda i: (i, 0))
    once = lambda shape: pl.BlockSpec(shape, lambda i: (0,) * len(shape))
    return pl.pallas_call(
        body, name=name, grid=(M // tm,),
        in_specs=[pl.BlockSpec((tm, p.shape[1]), lambda i: (i, 0)) for p in parts] + [once(w3.shape)]
        + [row] * n_ri + [once((1, N))] * n_vi,
        out_specs=[row] * n_ro + [once(s) for s in sums_out],
        out_shape=[_sds((M, N), dt) for dt in rows_out] + [_sds(s, F32) for s in sums_out],
        compiler_params=_params("arbitrary"))(*parts, w3, *rows_in, *vecs_in)


def _residual_norm_tail(prod, rows_in, vecs_in, rows_out, sums_out):
    hf = prod + rows_in[0][...]
    rows_out[0][...] = hf
    r = lax.rsqrt(jnp.mean(hf * hf, axis=-1, keepdims=True) + EPS)
    rows_out[1][...] = (hf * r * vecs_in[0][...]).astype(BF16)


def _rms_bwd_tail(bf16_copy):
    def tail(dxn, rows_in, vecs_in, rows_out, sums_out):
        xf = rows_in[0][...]
        r = lax.rsqrt(jnp.mean(xf * xf, axis=-1, keepdims=True) + EPS)
        nrm = xf * r
        dn = dxn * vecs_in[0][...]
        dx = r * (dn - nrm * jnp.mean(dn * nrm, axis=-1, keepdims=True)) + rows_in[1][...]
        rows_out[0][...] = dx
        if bf16_copy:
            rows_out[1][...] = dx.astype(BF16)
        sums_out[0][0:1, :] += jnp.sum(dxn * nrm, axis=0, keepdims=True)

    return tail


def _loss_tail(prod, rows_in, vecs_in, rows_out, sums_out):
    hf = prod + rows_in[0][...]
    r = lax.rsqrt(jnp.mean(hf * hf, axis=-1, keepdims=True) + EPS)
    nrm = hf * r
    gv = vecs_in[0][...]
    err = nrm * gv - rows_in[1][...]
    dy = err * (1.0 / hf.shape[-1])
    dn = dy * gv
    dh = r * (dn - nrm * jnp.mean(dn * nrm, axis=-1, keepdims=True))
    rows_out[0][...] = dh
    rows_out[1][...] = dh.astype(BF16)
    sums_out[0][0:1, :] += jnp.sum(dy * nrm, axis=0, keepdims=True)
    part = 0.5 * jnp.sum(jnp.mean(err * err, axis=-1, keepdims=True), axis=0, keepdims=True)
    sel = (lax.broadcasted_iota(jnp.int32, (8, 128), 0) == 0) & (lax.broadcasted_iota(jnp.int32, (8, 128), 1) == 0)
    sums_out[1][...] += jnp.where(sel, part, 0.0)


def _class_spec(tm, d, width):
    return pl.BlockSpec((d, tm // d, width), lambda i: (0, i, 0))


def _row_scratch(tm, width):
    return pltpu.VMEM((width // LANES, tm, LANES), F32)


def _fill(scr, val):
    for c in range(scr.shape[0]):
        scr[c] = val[:, c * LANES:(c + 1) * LANES]


def _to_classes(scr, out_ref, d):
    n = scr.shape[1] // d
    for r in range(d):
        for c in range(scr.shape[0]):
            out_ref[r, :, c * LANES:(c + 1) * LANES] = scr[c, pl.ds(r, n, stride=d), :].astype(out_ref.dtype)


def _from_classes(in_ref, scr, d):
    n = scr.shape[1] // d
    for r in range(d):
        blk = in_ref[r].astype(F32)
        for c in range(scr.shape[0]):
            scr[c, pl.ds(r, n, stride=d), :] = blk[:, c * LANES:(c + 1) * LANES]
    return jnp.concatenate([scr[c] for c in range(scr.shape[0])], axis=1)


def _rope_tables(S):
    half = ROT_DIM // 2
    freqs = ROPE_THETA ** (-jnp.arange(0, ROT_DIM, 2, dtype=F32) / ROT_DIM)
    ang = jnp.arange(S, dtype=F32)[:, None] * freqs[None, :]
    cos, sin = jnp.cos(ang), jnp.sin(ang)
    ones = jnp.ones((S, HEAD_DIM - ROT_DIM), F32)
    zeros = jnp.zeros((S, HEAD_DIM - ROT_DIM), F32)
    zh = jnp.zeros((S, half), F32)
    c = jnp.concatenate([cos, cos, ones], axis=1)
    sa = jnp.concatenate([-sin, zh, zeros], axis=1)
    sb = jnp.concatenate([zh, sin, zeros], axis=1)
    return tuple(jnp.tile(t, (1, LANES // HEAD_DIM)) for t in (c, sa, sb))


def _rope_fwd(y, tables, *, tm=512):
    S = y.shape[0]
    W = 2 * D_ATT
    tm = min(tm, S)
    half = ROT_DIM // 2
    dils = [d for d in DILATIONS if d > 1]

    def body(y_ref, c_ref, sa_ref, sb_ref, qk_ref, *rest):
        qk_outs, v_outs = rest[:len(dils)], rest[len(dils):2 * len(dils)]
        scr_qk, scr_v = rest[2 * len(dils):]
        t = y_ref[:, 0:W].astype(F32)
        rep = W // LANES
        c, sa, sb = (jnp.tile(r[...], (1, rep)) for r in (c_ref, sa_ref, sb_ref))
        rot = t * c + pltpu.roll(t, W - half, axis=1) * sa + pltpu.roll(t, half, axis=1) * sb
        qk_ref[...] = rot.astype(qk_ref.dtype)
        _fill(scr_qk, rot)
        _fill(scr_v, y_ref[:, W:W + D_ATT].astype(F32))
        for d, qo, vo in zip(dils, qk_outs, v_outs):
            _to_classes(scr_qk, qo, d)
            _to_classes(scr_v, vo, d)

    tab = pl.BlockSpec((tm, LANES), lambda i: (i, 0))
    out = pl.pallas_call(
        body, name="rope_fwd", grid=(S // tm,),
        in_specs=[pl.BlockSpec((tm, 3 * D_ATT), lambda i: (i, 0)), tab, tab, tab],
        out_specs=[pl.BlockSpec((tm, W), lambda i: (i, 0))] + [_class_spec(tm, d, W) for d in dils]
        + [_class_spec(tm, d, D_ATT) for d in dils],
        out_shape=[_sds((S, W), BF16)] + [_sds((d, S // d, W), BF16) for d in dils]
        + [_sds((d, S // d, D_ATT), BF16) for d in dils],
        scratch_shapes=[_row_scratch(tm, W), _row_scratch(tm, D_ATT)],
        compiler_params=_params("parallel"))(y, *tables)
    qk = [out[0]] + [o.reshape(S, W) for o in out[1:1 + len(dils)]]
    v = [None] + [o.reshape(S, D_ATT) for o in out[1 + len(dils):]]
    return qk, v


def _assemble_dy(dq, dk, dv, dag, tables, *, tm=512):
    S = dag.shape[0]
    tm = min(tm, S)
    half = ROT_DIM // 2
    W = D_ATT
    n_pat = len(DILATIONS)

    def body(*refs):
        groups = [refs[g * n_pat:(g + 1) * n_pat] for g in range(3)]
        dag_ref, c_ref, sa_ref, sb_ref, o_ref, scr = refs[3 * n_pat:]
        rep = W // LANES
        c, sa, sb = (jnp.tile(r[...], (1, rep)) for r in (c_ref, sa_ref, sb_ref))

        def total(rs):
            acc = rs[0][...].astype(F32)
            for d, r in zip(DILATIONS[1:], rs[1:]):
                acc = acc + _from_classes(r, scr, d)
            return acc

        def unrope(dr):
            return dr * c + pltpu.roll(dr * sa, half, axis=1) + pltpu.roll(dr * sb, W - half, axis=1)

        o_ref[:, 0:W] = unrope(total(groups[0])).astype(o_ref.dtype)
        o_ref[:, W:2 * W] = unrope(total(groups[1])).astype(o_ref.dtype)
        o_ref[:, 2 * W:3 * W] = total(groups[2]).astype(o_ref.dtype)
        o_ref[:, 3 * W:] = dag_ref[...]

    specs = [pl.BlockSpec((tm, W), lambda i: (i, 0))] + [_class_spec(tm, d, W) for d in DILATIONS[1:]]
    tab = pl.BlockSpec((tm, LANES), lambda i: (i, 0))
    args = [a if d == 1 else a.reshape(d, S // d, W) for grp in (dq, dk, dv) for d, a in zip(DILATIONS, grp)]
    return pl.pallas_call(
        body, name="assemble_dy", grid=(S // tm,),
        in_specs=specs * 3 + [pl.BlockSpec((tm, 2 * D_CONV), lambda i: (i, 0)), tab, tab, tab],
        out_specs=pl.BlockSpec((tm, D_IN), lambda i: (i, 0)), out_shape=_sds((S, D_IN), BF16),
        scratch_shapes=[_row_scratch(tm, W)],
        compiler_params=_params("parallel"))(*args, dag, *tables)


def _seq_specs(L, tb, col):
    nb, per, nh = L // tb, tb // HALF, L // HALF
    centre = pl.BlockSpec((tb, D_ATT), lambda r, i: (r * nb + i, col))
    prev = pl.BlockSpec((HALF, D_ATT), lambda r, i: (r * nh + jnp.maximum(i * per - 1, 0), col))
    nxt = pl.BlockSpec((HALF, D_ATT), lambda r, i: (r * nh + jnp.minimum((i + 1) * per, nh - 1), col))
    return prev, centre, nxt


def _band_mask(i, tq, L):
    shape = (tq, tq + 2 * HALF)
    c_idx = lax.broadcasted_iota(jnp.int32, shape, 0)
    w_idx = lax.broadcasted_iota(jnp.int32, shape, 1)
    diff = w_idx - c_idx
    wpos = i * tq - HALF + w_idx
    return (diff >= 0) & (diff <= 2 * HALF) & (wpos >= 0) & (wpos < L)


def _lane_groups():
    for c0 in range(0, D_ATT, LANES):
        yield slice(c0, c0 + LANES)


def _first_head(rows):
    return lax.broadcasted_iota(jnp.int32, (rows, LANES), 1) < HEAD_DIM


def _split_pair(x, first):
    zero = jnp.zeros_like(x)
    return jnp.where(first, x, zero), jnp.where(first, zero, x)


def _nt(a, b):
    return lax.dot_general(a, b, (((1,), (1,)), ((), ())), preferred_element_type=F32)


def _tn(a, b):
    return lax.dot_general(a, b, (((0,), (0,)), ((), ())), preferred_element_type=F32)


ATT_SCALE = HEAD_DIM ** -0.5


def _att_fwd(qk, v_src, d, *, name):
    S = qk.shape[0]
    L = S // d
    tq = min(ATT_BLOCK, L)
    v_arr, v_col = v_src

    def body(q_ref, kp_ref, kc_ref, kn_ref, vp_ref, vc_ref, vn_ref, o_ref, lse_ref):
        i = pl.program_id(1)
        valid = _band_mask(i, tq, L)
        q = q_ref[...] * ATT_SCALE
        kwin = jnp.concatenate([kp_ref[...], kc_ref[...], kn_ref[...]], axis=0)
        vwin = jnp.concatenate([vp_ref[...], vc_ref[...], vn_ref[...]], axis=0)
        first = _first_head(tq)
        groups = list(_lane_groups())
        heads = [(ls, t) for ls in groups for t in _split_pair(q[:, ls], first)]
        s = [jnp.where(valid, _nt(t, kwin[:, ls]), NEG_INF) for ls, t in heads]
        m = [jnp.max(t, axis=-1, keepdims=True) for t in s]
        p = [jnp.exp(t - mm) for t, mm in zip(s, m)]
        den = [jnp.sum(t, axis=-1, keepdims=True) for t in p]
        o = [jnp.dot(t.astype(BF16), vwin[:, ls], preferred_element_type=F32) * (1.0 / dd)
             for t, dd, (ls, _) in zip(p, den, heads)]
        lse = [mm + jnp.log(dd) for mm, dd in zip(m, den)]
        for g, ls in enumerate(groups):
            o_ref[:, ls] = jnp.where(first, o[2 * g], o[2 * g + 1]).astype(o_ref.dtype)
            lse_ref[:, ls] = jnp.where(first, lse[2 * g], lse[2 * g + 1])

    _, qc, _ = _seq_specs(L, tq, 0)
    kp, kc, kn = _seq_specs(L, tq, 1)
    vp, vc, vn = _seq_specs(L, tq, v_col)
    out = pl.BlockSpec((tq, D_ATT), lambda r, i: (r * (L // tq) + i, 0))
    return pl.pallas_call(
        body, name=name, grid=(d, L // tq),
        in_specs=[qc, kp, kc, kn, vp, vc, vn], out_specs=[out, out],
        out_shape=[_sds((S, D_ATT), BF16), _sds((S, D_ATT), F32)],
        compiler_params=_params("parallel", "parallel"))(qk, qk, qk, qk, v_arr, v_arr, v_arr)


def _att_combine(outs, lses, *, tm=512):
    S = outs[0].shape[0]
    tm = min(tm, S)
    dils = DILATIONS[1:]
    n_d = len(dils)

    def body(*refs):
        o_refs, l_refs = refs[0:1 + n_d], refs[1 + n_d:2 + 2 * n_d]
        att_ref, lg_ref = refs[2 + 2 * n_d:4 + 2 * n_d]
        lg_outs = refs[4 + 2 * n_d:4 + 3 * n_d]
        scr = refs[4 + 3 * n_d:]
        scr_o, scr_l, scr_lg = scr[:n_d], scr[n_d:2 * n_d], scr[2 * n_d]
        ls = [l_refs[0][...]] + [_from_classes(r, s, d) for r, s, d in zip(l_refs[1:], scr_l, dils)]
        os_ = [o_refs[0][...].astype(F32)] + [_from_classes(r, s, d) for r, s, d in zip(o_refs[1:], scr_o, dils)]
        mx = ls[0]
        for l in ls[1:]:
            mx = jnp.maximum(mx, l)
        es = [jnp.exp(l - mx) for l in ls]
        tot = es[0]
        num = es[0] * os_[0]
        for e, o in zip(es[1:], os_[1:]):
            tot = tot + e
            num = num + e * o
        att_ref[...] = (num / tot).astype(att_ref.dtype)
        lg = mx + jnp.log(tot)
        lg_ref[...] = lg
        _fill(scr_lg, lg)
        for d, out in zip(dils, lg_outs):
            _to_classes(scr_lg, out, d)

    nat = pl.BlockSpec((tm, D_ATT), lambda i: (i, 0))
    specs = [nat] + [_class_spec(tm, d, D_ATT) for d in dils]
    view = lambda arrs: [arrs[0]] + [a.reshape(d, S // d, D_ATT) for a, d in zip(arrs[1:], dils)]
    out = pl.pallas_call(
        body, name="att_combine", grid=(S // tm,), in_specs=specs * 2,
        out_specs=[nat, nat] + specs[1:],
        out_shape=[_sds((S, D_ATT), BF16), _sds((S, D_ATT), F32)] + [_sds((d, S // d, D_ATT), F32) for d in dils],
        scratch_shapes=[_row_scratch(tm, D_ATT)] * (2 * n_d + 1),
        compiler_params=_params("parallel"))(*view(list(outs)), *view(list(lses)))
    return out[0], [out[1]] + [o.reshape(S, D_ATT) for o in out[2:]]


def _att_delta(dac, att, *, tm=512):
    S = att.shape[0]
    tm = min(tm, S)
    dils = DILATIONS[1:]
    n_d = len(dils)

    def body(do_ref, o_ref, dl_ref, *rest):
        dl_outs, do_outs = rest[:n_d], rest[n_d:2 * n_d]
        scr_dl, scr_do = rest[2 * n_d:]
        do = do_ref[...].astype(F32)
        prod = do * o_ref[...].astype(F32)
        per_head = [jnp.broadcast_to(jnp.sum(prod[:, h * HEAD_DIM:(h + 1) * HEAD_DIM], axis=-1, keepdims=True),
                                     (tm, HEAD_DIM)) for h in range(ATT_HEADS)]
        dl = jnp.concatenate(per_head, axis=1)
        dl_ref[...] = dl
        _fill(scr_dl, dl)
        _fill(scr_do, do)
        for d, dlo, doo in zip(dils, dl_outs, do_outs):
            _to_classes(scr_dl, dlo, d)
            _to_classes(scr_do, doo, d)

    blk = pl.BlockSpec((tm, D_ATT), lambda i: (i, 0))
    out = pl.pallas_call(
        body, name="att_delta", grid=(S // tm,), in_specs=[blk, blk],
        out_specs=[blk] + [_class_spec(tm, d, D_ATT) for d in dils] * 2,
        out_shape=[_sds((S, D_ATT), F32)] + [_sds((d, S // d, D_ATT), F32) for d in dils]
        + [_sds((d, S // d, D_ATT), BF16) for d in dils],
        scratch_shapes=[_row_scratch(tm, D_ATT), _row_scratch(tm, D_ATT)],
        compiler_params=_params("parallel"))(dac, att)
    delta = [out[0]] + [o.reshape(S, D_ATT) for o in out[1:1 + n_d]]
    do = [None] + [o.reshape(S, D_ATT) for o in out[1 + n_d:]]
    return delta, do


def _att_bwd(qk, v_src, do_src, lg, delta, d, *, name):
    S = qk.shape[0]
    L = S // d
    tq = min(ATT_BLOCK, L)
    nb, per, nh = L // tq, tq // HALF, L // HALF
    n_blocks = d * nb
    win = tq + 2 * HALF
    lead = tq - HALF
    acc_rows = lead + win
    (v_arr, v_col), (do_arr, do_col) = v_src, do_src

    def body(q_ref, kp_ref, kc_ref, kn_ref, vp_ref, vc_ref, vn_ref, do_ref, lg_ref, dl_ref,
             dq_ref, dk_ref, dv_ref, acc_k, acc_v):
        b = pl.program_id(0)
        i = lax.rem(jnp.minimum(b, n_blocks - 1), nb)

        @pl.when(b == 0)
        def _():
            acc_k[...] = jnp.zeros_like(acc_k)
            acc_v[...] = jnp.zeros_like(acc_v)

        @pl.when(b < n_blocks)
        def _():
            valid = _band_mask(i, tq, L)
            q, do = q_ref[...] * ATT_SCALE, do_ref[...]
            kwin = jnp.concatenate([kp_ref[...], kc_ref[...], kn_ref[...]], axis=0)
            vwin = jnp.concatenate([vp_ref[...], vc_ref[...], vn_ref[...]], axis=0)
            first, first_w = _first_head(tq), _first_head(win)
            groups = list(_lane_groups())
            cols = [c for ls in groups for c in (ls.start, ls.start + HEAD_DIM)]
            lanes = [ls for ls in groups for _ in range(2)]
            qh = [t for ls in groups for t in _split_pair(q[:, ls], first)]
            doh = [t for ls in groups for t in _split_pair(do[:, ls], first)]
            s = [jnp.where(valid, _nt(t, kwin[:, ls]), NEG_INF) for t, ls in zip(qh, lanes)]
            dp = [_nt(t, vwin[:, ls]) for t, ls in zip(doh, lanes)]
            p = [jnp.exp(t - lg_ref[:, c:c + 1]) for t, c in zip(s, cols)]
            ds = [(pp * (t - dl_ref[:, c:c + 1])).astype(BF16) for pp, t, c in zip(p, dp, cols)]
            dq = [jnp.dot(t, kwin[:, ls], preferred_element_type=F32) for t, ls in zip(ds, lanes)]
            dk = [_tn(t, q[:, ls]) for t, ls in zip(ds, lanes)]
            dv = [_tn(pp.astype(BF16), do[:, ls]) for pp, ls in zip(p, lanes)]
            for g, ls in enumerate(groups):
                dq_ref[:, ls] = (jnp.where(first, dq[2 * g], dq[2 * g + 1]) * ATT_SCALE).astype(dq_ref.dtype)
                acc_k[lead:, ls] += jnp.where(first_w, dk[2 * g], dk[2 * g + 1])
                acc_v[lead:, ls] += jnp.where(first_w, dv[2 * g], dv[2 * g + 1])

        for acc, out in ((acc_k, dk_ref), (acc_v, dv_ref)):
            out[...] = acc[0:tq, :].astype(out.dtype)
            kept = acc[tq:, :]
            acc[0:acc_rows - tq, :] = kept
            acc[acc_rows - tq:, :] = jnp.zeros((tq, D_ATT), F32)

    def seq(col):
        blk = lambda b: jnp.minimum(b, n_blocks - 1)
        cls = lambda b: (blk(b) // nb) * nh
        centre = pl.BlockSpec((tq, D_ATT), lambda b: (blk(b), col))
        prev = pl.BlockSpec((HALF, D_ATT), lambda b: (cls(b) + jnp.maximum((blk(b) % nb) * per - 1, 0), col))
        nxt = pl.BlockSpec((HALF, D_ATT), lambda b: (cls(b) + jnp.minimum((blk(b) % nb + 1) * per, nh - 1), col))
        return prev, centre, nxt

    _, qc, _ = seq(0)
    kp, kc, kn = seq(1)
    vp, vc, vn = seq(v_col)
    _, doc, _ = seq(do_col)
    late = pl.BlockSpec((tq, D_ATT), lambda b: (jnp.maximum(b - 1, 0), 0))
    return pl.pallas_call(
        body, name=name, grid=(n_blocks + 1,),
        in_specs=[qc, kp, kc, kn, vp, vc, vn, doc, qc, qc], out_specs=[qc, late, late],
        out_shape=[_sds((S, D_ATT), BF16)] * 3,
        scratch_shapes=[pltpu.VMEM((acc_rows, D_ATT), F32), pltpu.VMEM((acc_rows, D_ATT), F32)],
        compiler_params=_params("arbitrary"))(qk, qk, qk, qk, v_arr, v_arr, v_arr, do_arr, lg, delta)


def _sigmoid(x):
    return 1.0 / (1.0 + jnp.exp(-x))


def _halo_specs(S, T, width, col):
    last = S // HALO - 1
    per = T // HALO
    centre = pl.BlockSpec((T, width), lambda i: (i, col))
    prev = pl.BlockSpec((HALO, width), lambda i: (jnp.maximum(i * per - 1, 0), col))
    nxt = pl.BlockSpec((HALO, width), lambda i: (jnp.minimum((i + 1) * per, last), col))
    return prev, centre, nxt


def _window_scratch(T, C):
    return pltpu.VMEM((8, T + 2 * HALO, C), F32)


def _fill_window(buf, prev, centre, nxt, T):
    buf[0, 0:HALO, :] = prev
    buf[0, HALO:HALO + T, :] = centre
    buf[0, HALO + T:, :] = nxt
    rows = T + 2 * HALO - 8
    for s in range(1, 8):
        buf[s, 0:rows, :] = buf[0, s:s + rows, :]


def _tap_reads(buf, first_off, step, r0, ls):
    by_slab = {}
    for k in range(CONV_WIDTH):
        off = first_off + step * k
        by_slab.setdefault(off % 8, []).append((k, off - off % 8))
    for s, taps in by_slab.items():
        lo = min(a for _, a in taps)
        hi = max(a for _, a in taps)
        rows = buf[s, pl.ds(lo + r0, CONV_ROWS + hi - lo), ls]
        for k, a in taps:
            yield k, rows[a - lo:a - lo + CONV_ROWS]


def _depthwise(buf, w_ref, out_ref, T, C, first_off, step):
    def row_tile(t, carry):
        r0 = pl.multiple_of(t * CONV_ROWS, CONV_ROWS)
        for c0 in range(0, C, LANES):
            ls = slice(c0, c0 + LANES)
            acc = jnp.zeros((CONV_ROWS, LANES), F32)
            for k, rows in _tap_reads(buf, first_off, step, r0, ls):
                acc = acc + rows * w_ref[k:k + 1, ls]
            out_ref[pl.ds(r0, CONV_ROWS), ls] = acc
        return carry

    lax.fori_loop(0, T // CONV_ROWS, row_tile, 0)


def _conv_fwd(y, conv_w32, conv_b, ln_g, ln_b, *, T=512):
    S = y.shape[0]
    T = min(T, S)
    nblk = S // T
    C = D_CONV

    def body(ap, ac, an, gp, gc, gn, w_ref, b_ref, lg_ref, lb_ref, cv_ref, u1_ref, buf):
        i = pl.program_id(0)

        def glu(a_ref, g_ref):
            return a_ref[...].astype(F32) * _sigmoid(g_ref[...].astype(F32))

        _fill_window(buf, jnp.where(i > 0, glu(ap, gp), 0.0), glu(ac, gc),
                     jnp.where(i < nblk - 1, glu(an, gn), 0.0), T)
        _depthwise(buf, w_ref, u1_ref, T, C, HALO - CONV_PAD, 1)
        u1 = u1_ref[...] + b_ref[...]
        u1_ref[...] = u1
        mu = jnp.mean(u1, axis=-1, keepdims=True)
        xc = u1 - mu
        rstd = lax.rsqrt(jnp.mean(xc * xc, axis=-1, keepdims=True) + EPS)
        u2 = xc * rstd * lg_ref[...] + lb_ref[...]
        cv_ref[...] = (u2 * _sigmoid(u2)).astype(cv_ref.dtype)

    ap, ac, an = _halo_specs(S, T, C, 3)
    gp, gc, gn = _halo_specs(S, T, C, 4)
    vec = pl.BlockSpec((1, C), lambda i: (0, 0))
    out = pl.BlockSpec((T, C), lambda i: (i, 0))
    return pl.pallas_call(
        body, name="conv_fwd", grid=(nblk,),
        in_specs=[ap, ac, an, gp, gc, gn, pl.BlockSpec((32, C), lambda i: (0, 0)), vec, vec, vec],
        out_specs=[out, out], out_shape=[_sds((S, C), BF16), _sds((S, C), F32)],
        scratch_shapes=[_window_scratch(T, C)],
        compiler_params=_params("parallel"))(y, y, y, y, y, y, conv_w32, conv_b, ln_g, ln_b)


def _conv_bwd(dac, u1, y, conv_w32, ln_g, ln_b, *, T=512):
    S = y.shape[0]
    T = min(T, S)
    nblk = S // T
    C = D_CONV

    def body(dp, dc, dn, up, uc, un, ap, ac, an, gp, gc, gn, w_ref, lg_ref, lb_ref,
             dag_ref, dw_ref, dsm_ref, bufd, bufu, du0_scr, dw_acc):
        i = pl.program_id(0)
        lg = lg_ref[...]

        def du1_of(dcv_ref, u1_ref):
            u1 = u1_ref[...]
            mu = jnp.mean(u1, axis=-1, keepdims=True)
            xc = u1 - mu
            rstd = lax.rsqrt(jnp.mean(xc * xc, axis=-1, keepdims=True) + EPS)
            xhat = xc * rstd
            u2 = xhat * lg + lb_ref[...]
            sg = _sigmoid(u2)
            du2 = dcv_ref[...].astype(F32) * (sg * (1.0 + u2 * (1.0 - sg)))
            dxh = du2 * lg
            du1 = rstd * (dxh - jnp.mean(dxh, axis=-1, keepdims=True)
                          - xhat * jnp.mean(dxh * xhat, axis=-1, keepdims=True))
            return du1, du2, xhat

        def glu(a_ref, g_ref):
            return a_ref[...].astype(F32) * _sigmoid(g_ref[...].astype(F32))

        @pl.when(i == 0)
        def _():
            dw_ref[...] = jnp.zeros_like(dw_ref)
            dsm_ref[...] = jnp.zeros_like(dsm_ref)

        du1_c, du2_c, xhat_c = du1_of(dc, uc)
        dsm_ref[0:1, :] += jnp.sum(du1_c, axis=0, keepdims=True)
        dsm_ref[1:2, :] += jnp.sum(du2_c * xhat_c, axis=0, keepdims=True)
        dsm_ref[2:3, :] += jnp.sum(du2_c, axis=0, keepdims=True)
        _fill_window(bufd, jnp.where(i > 0, du1_of(dp, up)[0], 0.0), du1_c,
                     jnp.where(i < nblk - 1, du1_of(dn, un)[0], 0.0), T)
        _fill_window(bufu, jnp.where(i > 0, glu(ap, gp), 0.0), glu(ac, gc),
                     jnp.where(i < nblk - 1, glu(an, gn), 0.0), T)

        _depthwise(bufd, w_ref, du0_scr, T, C, HALO + CONV_PAD, -1)
        dw_acc[...] = jnp.zeros_like(dw_acc)

        def dw_tile(t, carry):
            r0 = pl.multiple_of(t * CONV_ROWS, CONV_ROWS)
            for c0 in range(0, C, LANES):
                ls = slice(c0, c0 + LANES)
                d = bufd[0, pl.ds(HALO + r0, CONV_ROWS), ls]
                for k, rows in _tap_reads(bufu, HALO - CONV_PAD, 1, r0, ls):
                    prod = d * rows
                    part = prod[0:8]
                    for j in range(8, CONV_ROWS, 8):
                        part = part + prod[j:j + 8]
                    dw_acc[k, :, ls] += part
            return carry

        lax.fori_loop(0, T // CONV_ROWS, dw_tile, 0)
        for k in range(CONV_WIDTH):
            dw_ref[k:k + 1, :] += jnp.sum(dw_acc[k], axis=0, keepdims=True)
        du0 = du0_scr[...]
        a = ac[...].astype(F32)
        sg = _sigmoid(gc[...].astype(F32))
        dag_ref[:, 0:C] = (du0 * sg).astype(dag_ref.dtype)
        dag_ref[:, C:] = (du0 * a * sg * (1.0 - sg)).astype(dag_ref.dtype)

    dp, dc, dn = _halo_specs(S, T, C, 1)
    up, uc, un = _halo_specs(S, T, C, 0)
    ap, ac, an = _halo_specs(S, T, C, 3)
    gp, gc, gn = _halo_specs(S, T, C, 4)
    vec = pl.BlockSpec((1, C), lambda i: (0, 0))
    return pl.pallas_call(
        body, name="conv_bwd", grid=(nblk,),
        in_specs=[dp, dc, dn, up, uc, un, ap, ac, an, gp, gc, gn,
                  pl.BlockSpec((32, C), lambda i: (0, 0)), vec, vec],
        out_specs=[pl.BlockSpec((T, 2 * C), lambda i: (i, 0)), pl.BlockSpec((32, C), lambda i: (0, 0)),
                   pl.BlockSpec((8, C), lambda i: (0, 0))],
        out_shape=[_sds((S, 2 * C), BF16), _sds((32, C), F32), _sds((8, C), F32)],
        scratch_shapes=[_window_scratch(T, C), _window_scratch(T, C), pltpu.VMEM((T, C), F32),
                        pltpu.VMEM((CONV_WIDTH, 8, C), F32)],
        compiler_params=_params("arbitrary"))(dac, dac, dac, u1, u1, u1, y, y, y, y, y, y, conv_w32, ln_g, ln_b)


def _xatt_fwd(xq, xk, xv, *, tm=512):
    S = xq.shape[0]
    M = xk.shape[0]
    tm = min(tm, S)
    scale = XATT_HEAD_DIM ** -0.5

    def body(q_ref, k_ref, v_ref, o_ref):
        heads = [slice(h * XATT_HEAD_DIM, (h + 1) * XATT_HEAD_DIM) for h in range(XATT_HEADS)]
        s = [_nt(q_ref[:, sl], k_ref[:, sl]) * scale for sl in heads]
        e = [jnp.exp(t - jnp.max(t, axis=-1, keepdims=True)) for t in s]
        p = [t * (1.0 / jnp.sum(t, axis=-1, keepdims=True)) for t in e]
        for sl, t in zip(heads, p):
            o_ref[:, sl] = jnp.dot(t.astype(BF16), v_ref[:, sl], preferred_element_type=F32).astype(o_ref.dtype)

    row = pl.BlockSpec((tm, D_MODEL), lambda i: (i, 0))
    full = pl.BlockSpec((M, D_MODEL), lambda i: (0, 0))
    return pl.pallas_call(
        body, name="xatt_fwd", grid=(S // tm,), in_specs=[row, full, full], out_specs=row,
        out_shape=_sds((S, D_MODEL), BF16), compiler_params=_params("parallel"))(xq, xk, xv)


def _xatt_bwd(xq, xk, xv, dxo, *, tm=512):
    S = xq.shape[0]
    M = xk.shape[0]
    tm = min(tm, S)
    scale = XATT_HEAD_DIM ** -0.5

    def body(q_ref, k_ref, v_ref, do_ref, dq_ref, dk_ref, dv_ref):
        i = pl.program_id(0)

        @pl.when(i == 0)
        def _():
            dk_ref[...] = jnp.zeros_like(dk_ref)
            dv_ref[...] = jnp.zeros_like(dv_ref)

        heads = [slice(h * XATT_HEAD_DIM, (h + 1) * XATT_HEAD_DIM) for h in range(XATT_HEADS)]
        s = [_nt(q_ref[:, sl], k_ref[:, sl]) * scale for sl in heads]
        dp = [_nt(do_ref[:, sl], v_ref[:, sl]) for sl in heads]
        e = [jnp.exp(t - jnp.max(t, axis=-1, keepdims=True)) for t in s]
        p = [t * (1.0 / jnp.sum(t, axis=-1, keepdims=True)) for t in e]
        ds = [(pp * (t - jnp.sum(t * pp, axis=-1, keepdims=True))).astype(BF16) for pp, t in zip(p, dp)]
        for sl, pp, t in zip(heads, p, ds):
            dq_ref[:, sl] = (jnp.dot(t, k_ref[:, sl], preferred_element_type=F32) * scale).astype(dq_ref.dtype)
            dv_ref[:, sl] += _tn(pp.astype(BF16), do_ref[:, sl])
            dk_ref[:, sl] += _tn(t, q_ref[:, sl]) * scale

    row = pl.BlockSpec((tm, D_MODEL), lambda i: (i, 0))
    full = pl.BlockSpec((M, D_MODEL), lambda i: (0, 0))
    return pl.pallas_call(
        body, name="xatt_bwd", grid=(S // tm,), in_specs=[row, full, full, row], out_specs=[row, full, full],
        out_shape=[_sds((S, D_MODEL), BF16), _sds((M, D_MODEL), F32), _sds((M, D_MODEL), F32)],
        compiler_params=_params("arbitrary"))(xq, xk, xv, dxo)


def _row_tile(R):
    for t in (256, 128, 64, 32, 16, 8):
        if R % t == 0:
            return t
    return R


def _sum_partials(own, recv, me, *, name):
    _, R, C = own.shape
    t = _row_tile(R)

    def body(me_ref, own_ref, r_ref, o_ref):
        o_ref[...] = ((own_ref[...].astype(F32) + r_ref[0].astype(F32)) + r_ref[1].astype(F32)) + r_ref[2].astype(F32)

    return pl.pallas_call(
        body, name=name,
        grid_spec=pltpu.PrefetchScalarGridSpec(
            num_scalar_prefetch=1, grid=(R // t,),
            in_specs=[pl.BlockSpec((None, t, C), lambda i, me_ref: (me_ref[0], i, 0)),
                      pl.BlockSpec((3, t, C), lambda i, me_ref: (0, i, 0))],
            out_specs=pl.BlockSpec((t, C), lambda i, me_ref: (i, 0))),
        out_shape=_sds((R, C), F32), compiler_params=_params("parallel"))(me, own, recv)


def _adamw_math(w, g, m, v):
    m2 = ADAM_B1 * m + (1.0 - ADAM_B1) * g
    v2 = ADAM_B2 * v + (1.0 - ADAM_B2) * (g * g)
    m_hat = m2 / (1.0 - ADAM_B1 ** ADAM_STEP)
    v_hat = v2 / (1.0 - ADAM_B2 ** ADAM_STEP)
    delta = -ADAM_LR * (m_hat / (jnp.sqrt(v_hat) + ADAM_EPS) + ADAM_WD * w)
    return delta, m2, v2


def _adamw(parts, w, m, v, *, name):
    R, C = w.shape
    t = _row_tile(R)
    n = len(parts)

    def body(*refs):
        w_ref, m_ref, v_ref = refs[n:n + 3]
        g_ref, d_ref, m2_ref, v2_ref = refs[n + 3:]
        g = refs[0][...]
        for r in refs[1:n]:
            g = g + r[...]
        delta, m2, v2 = _adamw_math(w_ref[...], g, m_ref[...], v_ref[...])
        g_ref[...] = g
        d_ref[...] = delta
        m2_ref[...] = m2
        v2_ref[...] = v2

    blk = pl.BlockSpec((t, C), lambda i: (i, 0))
    return pl.pallas_call(
        body, name=name, grid=(R // t,), in_specs=[blk] * (n + 3), out_specs=[blk] * 4,
        out_shape=[_sds((R, C), F32)] * 4, compiler_params=_params("parallel"))(*parts, w, m, v)


def _adamw_small(gathered, chip, entries):
    _, R, C = gathered.shape
    n = len(entries)
    group = D_CONV // N_CHIPS

    def body(chip_ref, g_ref, *refs):
        ins, outs, tot_ref = refs[:3 * n], refs[3 * n:7 * n], refs[7 * n]
        tot = g_ref[0]
        for k in range(1, N_DEV):
            tot = tot + g_ref[k]
        tot_ref[...] = tot
        for e, ((kind, r), _, _, _) in enumerate(entries):
            if kind == "row":
                g = tot_ref[r:r + 1, :]
            elif kind == "gain":
                g = jnp.concatenate([tot_ref[r:r + 1, :], tot_ref[r + 1:r + 2, :]], axis=1)
            else:
                g = tot_ref[r:r + CONV_WIDTH, 0:group]
                for j in range(1, N_CHIPS):
                    g = jnp.where(chip_ref[0] == j, tot_ref[r:r + CONV_WIDTH, j * group:(j + 1) * group], g)
            delta, m2, v2 = _adamw_math(ins[3 * e][...], g, ins[3 * e + 1][...], ins[3 * e + 2][...])
            for o, val in zip(outs[4 * e:4 * e + 4], (g, delta, m2, v2)):
                o[...] = val

    whole = lambda a: pl.BlockSpec(a.shape, lambda i, c: (0,) * a.ndim)
    arrays = [a for _, w, m, v in entries for a in (w, m, v)]
    out_like = [w for _, w, _, _ in entries for _ in range(4)]
    tot_like = _sds((R, C), F32)
    out = pl.pallas_call(
        body, name="adamw_small",
        grid_spec=pltpu.PrefetchScalarGridSpec(
            num_scalar_prefetch=1, grid=(1,),
            in_specs=[whole(gathered)] + [whole(a) for a in arrays],
            out_specs=[whole(a) for a in out_like] + [whole(tot_like)]),
        out_shape=[_sds(a.shape, F32) for a in out_like] + [tot_like],
        compiler_params=_params("arbitrary"))(chip, gathered, *arrays)
    return out[-1], [tuple(out[4 * e:4 * e + 4]) for e in range(n)]


def _chip_peers():
    x, y = lax.axis_index("x"), lax.axis_index("y")
    return [(1 - x, y), (x, 1 - y), (1 - x, 1 - y)]


HBM_SPEC = pl.BlockSpec(memory_space=pltpu.HBM)
SEM_SPEC = pl.BlockSpec(memory_space=pltpu.SEMAPHORE)


def _exchange_peers(mode):
    x, y, c = lax.axis_index("x"), lax.axis_index("y"), lax.axis_index("c")
    if mode == "swap":
        return [(x, y, 1 - c)]
    if mode == "all":
        flips = [(fx, fy, fc) for fx in (0, 1) for fy in (0, 1) for fc in (0, 1)][1:]
        return [(1 - x if fx else x, 1 - y if fy else y, 1 - c if fc else c) for fx, fy, fc in flips]
    return [(px, py, c) for px, py in _chip_peers()]


def _exchange_start(mode, srcs, zones, *, name):
    n = len(srcs)

    def body(*refs):
        ins, lands = refs[:n], refs[n:2 * n]
        send_sems, recv_sems = refs[2 * n:3 * n], refs[3 * n:4 * n]
        token = refs[-1]
        x, y, c = lax.axis_index("x"), lax.axis_index("y"), lax.axis_index("c")
        mine = 2 * x + y if mode == "gather" else 4 * x + 2 * y + c
        for t in range(n):
            for k, (px, py, pc) in enumerate(_exchange_peers(mode)):
                if mode in ("gather", "all"):
                    s, d = ins[t], lands[t].at[mine]
                elif mode == "scatter":
                    s, d = ins[t].at[2 * px + py], lands[t].at[k]
                else:
                    s, d = ins[t], lands[t]
                pltpu.make_async_remote_copy(src_ref=s, dst_ref=d, send_sem=send_sems[t], recv_sem=recv_sems[t],
                                             device_id=(px, py, pc), device_id_type=MESH).start()
            if mode in ("gather", "all"):
                pltpu.make_async_copy(ins[t], lands[t].at[mine], send_sems[t]).start()
        token[...] = jnp.zeros_like(token)

    hbm = lambda a: pltpu.with_memory_space_constraint(a, pltpu.HBM)
    out = pl.pallas_call(
        body, name=name,
        in_specs=[HBM_SPEC] * (2 * n),
        out_specs=[SEM_SPEC] * (2 * n) + [HBM_SPEC] * (2 * n) + [pl.BlockSpec(memory_space=pltpu.VMEM)],
        out_shape=[pltpu.SemaphoreType.DMA(())] * (2 * n)
        + [pltpu.HBM(a.shape, a.dtype) for a in list(srcs) + list(zones)] + [_sds((8, LANES), F32)],
        input_output_aliases={i: 2 * n + i for i in range(2 * n)},
        compiler_params=pltpu.CompilerParams(has_side_effects=pltpu.SideEffectType.DATAFLOW_SIDE_EFFECTING),
    )(*[hbm(a) for a in list(srcs) + list(zones)])
    return out[:n], out[n:2 * n], out[2 * n:3 * n], out[3 * n:4 * n], out[-1]


def _exchange_wait(mode, started, after, *, name):
    send_sems, recv_sems, srcs, zones, _ = started
    n = len(srcs)
    afters = tuple(after) if isinstance(after, (tuple, list)) else (after,)

    def body(*refs):
        lands = refs[n:2 * n]
        send_refs, recv_refs = refs[2 * n:3 * n], refs[3 * n:4 * n]
        me = (lax.axis_index("x"), lax.axis_index("y"), lax.axis_index("c"))
        n_remote = {"gather": N_CHIPS - 1, "scatter": N_CHIPS - 1, "all": N_DEV - 1, "swap": 1}[mode]
        for t in range(n):
            got = lands[t] if mode == "swap" else lands[t].at[pl.ds(0, n_remote)]
            sent = lands[t] if mode in ("gather", "all") else got
            pltpu.make_async_remote_copy(src_ref=sent, dst_ref=sent, send_sem=send_refs[t], recv_sem=recv_refs[t],
                                         device_id=me, device_id_type=MESH).wait_send()
            pltpu.make_async_remote_copy(src_ref=got, dst_ref=got, send_sem=send_refs[t], recv_sem=recv_refs[t],
                                         device_id=me, device_id_type=MESH).wait_recv()

    out = pl.pallas_call(
        body, name=name,
        in_specs=[HBM_SPEC] * (2 * n) + [SEM_SPEC] * (2 * n) + [pl.BlockSpec(memory_space=pl.ANY)] * len(afters),
        out_specs=[HBM_SPEC] * (2 * n),
        out_shape=[pltpu.HBM(a.shape, a.dtype) for a in list(srcs) + list(zones)],
        input_output_aliases={i: i for i in range(2 * n)},
        compiler_params=pltpu.CompilerParams(has_side_effects=pltpu.SideEffectType.DATAFLOW_SIDE_EFFECTING),
    )(*srcs, *zones, *send_sems, *recv_sems, *afters)
    return out[:n], out[n:]


def _swap_with_sibling(parts):
    n = len(parts)

    def body(*refs):
        ins, outs = refs[:n], refs[n:2 * n]
        send_sems, recv_sems = refs[2 * n:]
        sib = (lax.axis_index("x"), lax.axis_index("y"), 1 - lax.axis_index("c"))
        cps = []
        for t in range(n):
            cp = pltpu.make_async_remote_copy(
                src_ref=ins[t], dst_ref=outs[t], send_sem=send_sems.at[t], recv_sem=recv_sems.at[t],
                device_id=sib, device_id_type=MESH)
            cp.start()
            cps.append(cp)
        for cp in cps:
            cp.wait()

    any_spec = pl.BlockSpec(memory_space=pl.ANY)
    return pl.pallas_call(
        body, name="swap_with_sibling", in_specs=[any_spec] * n, out_specs=[any_spec] * n,
        out_shape=[_sds(p.shape, p.dtype) for p in parts],
        scratch_shapes=[pltpu.SemaphoreType.DMA((n,)), pltpu.SemaphoreType.DMA((n,))])(*parts)


BIG = ("w_in", "w_out", "w_xq", "w_xk", "w_xv", "w_xo", "w_up", "w_down")
COL_SHARDED = ("w_in", "w_up")


def _as_matrix(name, w4):
    if name in COL_SHARDED:
        return w4
    return w4.reshape(1, w4.shape[0] * w4.shape[1], w4.shape[2])


def _shard_layout(name, g):
    if name in COL_SHARDED:
        return g
    return g.reshape(N_CHIPS, g.shape[0] * g.shape[1] // N_CHIPS, g.shape[2])


def _local_step(x, mem, target, vecs, comm):
    S = x.shape[0]
    tables = _rope_tables(S)

    xn = _rms_fwd(x, vecs["norm_mix_g"], name="rms_mix")
    w_in, conv_w32 = comm["first"]((xn,) + tuple(tables))
    y = _mm_nn(xn, w_in, name="mm_in", tm=2048, group=2)
    qk, v_perm = _rope_fwd(y, tables)
    v_src = [(y, 2)] + [(v, 0) for v in v_perm[1:]]
    outs, lses = zip(*[_att_fwd(qk[p], v_src[p], d, name=f"att_fwd_d{d}") for p, d in enumerate(DILATIONS)])
    att, lg = _att_combine(outs, lses)
    cv, u1 = _conv_fwd(y, conv_w32, vecs["conv_b"], vecs["conv_ln_g"], vecs["conv_ln_b"])
    Wm = {k: _as_matrix(k, v) for k, v in comm["rest"]((att, cv)).items()}
    Wm["w_in"] = w_in
    h1, hn = _mm_rows((att, cv), Wm["w_out"], _residual_norm_tail, name="mm_out_rms", rows_in=(x,),
                      vecs_in=(vecs["norm_x_g"],), rows_out=(F32, BF16))
    xq = _mm_nn(hn, Wm["w_xq"], name="mm_xq")
    mn = _rms_fwd(mem, vecs["norm_mem_g"], name="rms_mem")
    xk = _mm_nn(mn, Wm["w_xk"], name="mm_xk")
    xv = _mm_nn(mn, Wm["w_xv"], name="mm_xv")
    xo = _xatt_fwd(xq, xk, xv)
    h2, hm = _mm_rows(xo, Wm["w_xo"], _residual_norm_tail, name="mm_xo_rms", rows_in=(h1,),
                      vecs_in=(vecs["norm_mlp_g"],), rows_out=(F32, BF16))
    relu_up = _mm_nn(hm, Wm["w_up"], name="mm_up", relu=True, tm=2048)
    sums = ((8, D_MODEL),)
    dh3, dh3b, dg_final, loss = _mm_rows(
        relu_up, Wm["w_down"], _loss_tail, name="mm_down_loss", rows_in=(h2, target), vecs_in=(vecs["norm_final_g"],),
        rows_out=(F32, BF16), sums_out=sums + ((8, LANES),), a_squared=True, tm=256)
    g = {}
    g["w_down"] = _mm_tn(relu_up, dh3b, 1, name="dw_down", a_squared=True, tm=4096)
    dup = _mm_nt(dh3b, Wm["w_down"], name="d_act", out_dtype=BF16, mul=relu_up, tm=2048)
    g["w_up"] = _mm_tn(hm, dup, N_CHIPS, name="dw_up", tm=4096)
    sent = comm["send_mlp"]({k: _shard_layout(k, g[k]) for k in ("w_down", "w_up")})
    dh2, dh2b, dg_mlp = _mm_rows(
        dup, Wm["w_up"], _rms_bwd_tail(True), name="d_hm_rms", w_transposed=True, rows_in=(h2, dh3),
        vecs_in=(vecs["norm_mlp_g"] + sent[0:1, 0:1],), rows_out=(F32, BF16), sums_out=sums, tm=256)
    g["w_xo"] = _mm_tn(xo, dh2b, 1, name="dw_xo")
    dxo = _mm_nt(dh2b, Wm["w_xo"], name="d_xo", out_dtype=BF16)
    dxq, dxk, dxv = _xatt_bwd(xq, xk, xv, dxo)
    g["w_xq"] = _mm_tn(hn, dxq, 1, name="dw_xq")
    dh1, dh1b, dg_x = _mm_rows(
        dxq, Wm["w_xq"], _rms_bwd_tail(True), name="d_hn_rms", w_transposed=True, rows_in=(h1, dh2),
        vecs_in=(vecs["norm_x_g"],), rows_out=(F32, BF16), sums_out=sums)
    dxkb, dxvb = dxk.astype(BF16), dxv.astype(BF16)
    g["w_xk"] = _mm_tn(mn, dxkb, 1, name="dw_xk")
    g["w_xv"] = _mm_tn(mn, dxvb, 1, name="dw_xv")
    dmn = _mm_nt(jnp.concatenate([dxkb, dxvb], axis=1),
                 jnp.concatenate([Wm["w_xk"], Wm["w_xv"]], axis=2), name="d_mn", out_dtype=BF16)
    _, _, dg_mem = _rms_bwd(dmn, mem, vecs["norm_mem_g"], None, name="rms_bwd_mem", bf16_copy=False)
    g["w_out"] = jnp.concatenate([_mm_tn(att, dh1b, 1, name="dw_out_att"), _mm_tn(cv, dh1b, 1, name="dw_out_conv")],
                                 axis=1)
    sent = comm["send_att"]({k: _shard_layout(k, g[k]) for k in ("w_out", "w_xq", "w_xk", "w_xv", "w_xo")})
    dac = _mm_nt(dh1b, Wm["w_out"], name="d_mix", out_dtype=BF16)
    dag, dconv_w, dconv_small = _conv_bwd(dac, u1, y, conv_w32, vecs["conv_ln_g"] + sent[0:1, 0:1],
                                          vecs["conv_ln_b"])
    delta, do_perm = _att_delta(dac, att)
    do_src = [(dac, 0)] + [(t, 0) for t in do_perm[1:]]
    dq, dk, dv = zip(*[_att_bwd(qk[p], v_src[p], do_src[p], lg[p], delta[p], d, name=f"att_bwd_d{d}")
                       for p, d in enumerate(DILATIONS)])
    dy = _assemble_dy(dq, dk, dv, dag, tables)
    sent = comm["send_in"]({"w_in": _mm_tn(xn, dy, N_CHIPS, name="dw_in", group=2)})
    grad_x, dg_mix = _mm_rows(
        dy, Wm["w_in"], _rms_bwd_tail(False), name="d_xn_rms", w_transposed=True, rows_in=(x, dh1),
        vecs_in=(vecs["norm_mix_g"] + sent[0:1, 0:1],), rows_out=(F32,), sums_out=sums)

    small = dict(conv_w=dconv_w, conv_small=dconv_small, norm_mix_g=dg_mix, norm_x_g=dg_x, norm_mem_g=dg_mem,
                 norm_mlp_g=dg_mlp, norm_final_g=dg_final, loss=loss)
    return grad_x, small


SMALL_ORDER = ("conv_w", "conv_small", "norm_mix_g", "norm_x_g", "norm_mem_g", "norm_mlp_g", "norm_final_g", "loss")


def _pack_small(small):
    rows, offs, pos = [], {}, 0
    for k in SMALL_ORDER:
        a = small[k]
        a = a.reshape(a.shape[0] * a.shape[1] // SMALL_W, SMALL_W)
        pad = (-a.shape[0]) % 8
        if pad:
            a = jnp.pad(a, ((0, pad), (0, 0)))
        rows.append(a)
        offs[k] = pos
        pos += a.shape[0]
    return jnp.concatenate(rows, axis=0), offs


def kernel(x, mem, norm_mix_g, w_in, conv_w, conv_b, conv_ln_g, conv_ln_b, w_out, norm_x_g, norm_mem_g, w_xq, w_xk, w_xv, w_xo, norm_mlp_g, w_up, w_down, norm_final_g, loss_target, m_norm_mix_g, m_w_in, m_conv_w, m_conv_b, m_conv_ln_g, m_conv_ln_b, m_w_out, m_norm_x_g, m_norm_mem_g, m_w_xq, m_w_xk, m_w_xv, m_w_xo, m_norm_mlp_g, m_w_up, m_w_down, m_norm_final_g, v_norm_mix_g, v_w_in, v_conv_w, v_conv_b, v_conv_ln_g, v_conv_ln_b, v_w_out, v_norm_x_g, v_norm_mem_g, v_w_xq, v_w_xk, v_w_xv, v_w_xo, v_norm_mlp_g, v_w_up, v_w_down, v_norm_final_g):
    names = ("norm_mix_g", "w_in", "conv_w", "conv_b", "conv_ln_g", "conv_ln_b", "w_out", "norm_x_g", "norm_mem_g",
             "w_xq", "w_xk", "w_xv", "w_xo", "norm_mlp_g", "w_up", "w_down", "norm_final_g")
    wts = dict(zip(names, (norm_mix_g, w_in, conv_w, conv_b, conv_ln_g, conv_ln_b, w_out, norm_x_g, norm_mem_g,
                           w_xq, w_xk, w_xv, w_xo, norm_mlp_g, w_up, w_down, norm_final_g)))
    mom = dict(zip(names, (m_norm_mix_g, m_w_in, m_conv_w, m_conv_b, m_conv_ln_g, m_conv_ln_b, m_w_out, m_norm_x_g,
                           m_norm_mem_g, m_w_xq, m_w_xk, m_w_xv, m_w_xo, m_norm_mlp_g, m_w_up, m_w_down, m_norm_final_g)))
    var = dict(zip(names, (v_norm_mix_g, v_w_in, v_conv_w, v_conv_b, v_conv_ln_g, v_conv_ln_b, v_w_out, v_norm_x_g,
                           v_norm_mem_g, v_w_xq, v_w_xk, v_w_xv, v_w_xo, v_norm_mlp_g, v_w_up, v_w_down, v_norm_final_g)))
    chip = 2 * lax.axis_index("x") + lax.axis_index("y")

    def zone(shard):
        return lax.empty((N_CHIPS,) + shard.shape, shard.dtype)

    conv_w_pad = jnp.pad(wts["conv_w"][0], ((0, 1), (0, 0)))
    first_shards = [wts["w_in"][0].astype(BF16), conv_w_pad]
    gathering_first = _exchange_start("gather", first_shards, [zone(s) for s in first_shards],
                                      name="gather_first_start")
    rest = tuple(k for k in BIG if k != "w_in")
    behind_first = gathering_first[4][0, 0]
    rest_shards = [(wts[k][0] + behind_first).astype(BF16) for k in rest]
    gathering = gathering_rest = _exchange_start("gather", rest_shards, [zone(s) for s in rest_shards],
                                                 name="gather_rest_start")
    sending = {}

    def wait_first(after):
        _, (w_in_all, conv_w_all) = _exchange_wait("gather", gathering_first, after, name="gather_first_wait")
        return w_in_all, jnp.transpose(conv_w_all, (1, 0, 2)).reshape(32, D_CONV)

    def wait_rest(after):
        _, zones = _exchange_wait("gather", gathering_rest, after, name="gather_rest_wait")
        return dict(zip(rest, zones))

    def send(group, grads):
        keys = tuple(grads)
        zones = [lax.empty((N_CHIPS - 1,) + grads[k].shape[1:], grads[k].dtype) for k in keys]
        sending[group] = (keys, _exchange_start("scatter", [grads[k] for k in keys], zones,
                                                name=f"scatter_{group}_start"))
        return sending[group][1][4]

    comm = dict(first=wait_first, rest=wait_rest, send_mlp=lambda grads: send("mlp", grads),
                send_att=lambda grads: send("att", grads), send_in=lambda grads: send("in", grads))
    vecs = {k: wts[k] for k in ("conv_b", "conv_ln_g", "conv_ln_b", "norm_x_g", "norm_mem_g", "norm_mlp_g")}
    vecs["norm_mix_g"] = wts["norm_mix_g"] + gathering[4][0:1, 0:1]
    vecs["norm_final_g"] = wts["norm_final_g"].reshape(1, D_MODEL)
    grad_x, small = _local_step(x[0], mem[0], loss_target[0], vecs, comm)

    packed, offs = _pack_small(small)
    me_arr = jnp.reshape(chip, (1,)).astype(jnp.int32)
    gathering_small = _exchange_start("all", [packed], [lax.empty((N_DEV,) + packed.shape, packed.dtype)],
                                      name="allgather_small_start")
    sums = {}

    def settle(group, after):
        keys, started = sending[group]
        srcs, zones = _exchange_wait("scatter", started, after, name=f"scatter_{group}_wait")
        for k, own, got in zip(keys, srcs, zones):
            sums[k] = _sum_partials(own, got, me_arr, name=f"sum_{k}")

    settle("mlp", gathering_small[4])
    settle("att", gathering_small[4])
    early = tuple(sums)
    swapping = _exchange_start("swap", [sums[k] for k in early], [lax.empty(sums[k].shape, F32) for k in early],
                               name="swap_early_start")
    settle("in", swapping[4])
    _, (gath,) = _exchange_wait("all", gathering_small, sums["w_in"], name="allgather_small_wait")

    where = {"conv_w": ("conv_w", offs["conv_w"]), "conv_b": ("row", offs["conv_small"]),
             "conv_ln_g": ("row", offs["conv_small"] + 1), "conv_ln_b": ("row", offs["conv_small"] + 2)}
    where.update({k: ("gain", offs[k]) for k in ("norm_mix_g", "norm_x_g", "norm_mem_g", "norm_mlp_g", "norm_final_g")})
    as_2d = lambda a: a.reshape(a.shape[-2] if a.ndim > 1 else 1, a.shape[-1])
    tot_small, updates = _adamw_small(gath, me_arr, [(where[k], as_2d(wts[k]), as_2d(mom[k]), as_2d(var[k]))
                                                     for k in where])
    res = dict(zip(where, updates))
    loss = tot_small[offs["loss"], 0]

    sib = {"w_in": _swap_with_sibling([sums["w_in"]])[0]}
    mine_early, sib_early = _exchange_wait("swap", swapping, sib["w_in"], name="swap_early_wait")
    sums.update(zip(early, mine_early))
    sib.update(zip(early, sib_early))
    for k in BIG:
        res[k] = _adamw([sums[k], sib[k]], wts[k][0], mom[k][0], var[k][0], name=f"adamw_{k}")

    outs = [loss, grad_x[None]]
    for j in range(4):
        outs += [res[k][j].reshape(wts[k].shape) for k in names]
    return tuple(outs)
```

```python
import jax
import jax.numpy as jnp
from jax import lax
from jax.experimental import pallas as pl
from jax.experimental.pallas import tpu as pltpu

F32 = jnp.float32
BF16 = jnp.bfloat16
MESH = pl.DeviceIdType.MESH

D_MODEL = 1024
ATT_HEADS = 8
HEAD_DIM = 64
D_ATT = ATT_HEADS * HEAD_DIM
D_CONV = D_MODEL - D_ATT
DILATIONS = (1, 4, 16)
HALF = 64
ROPE_THETA = 500000.0
ROT_DIM = HEAD_DIM // 4
CONV_WIDTH = 31
CONV_PAD = (CONV_WIDTH - 1) // 2
XATT_HEADS = 4
XATT_HEAD_DIM = D_MODEL // XATT_HEADS
D_FF = 4 * D_MODEL
D_IN = 3 * D_ATT + 2 * D_CONV
EPS = 1e-6
NEG_INF = -1e30
N_CHIPS = 4
N_DEV = 8

ADAM_LR = 0.001
ADAM_B1 = 0.9
ADAM_B2 = 0.999
ADAM_EPS = 1e-08
ADAM_WD = 0.01
ADAM_STEP = 10

VMEM_LIMIT_V7X = 56 * 1024 * 1024
LANES = 128
HALO = 16
CONV_ROWS = 64
ATT_BLOCK = 128
SMALL_W = 512


def _params(*sem):
    return pltpu.CompilerParams(dimension_semantics=sem, vmem_limit_bytes=VMEM_LIMIT_V7X)


def _sds(shape, dtype):
    return jax.ShapeDtypeStruct(shape, dtype)


def _squared(a):
    af = a.astype(F32)
    return (af * af).astype(BF16)


def _mm_nn(a, w3, *, name, out_dtype=BF16, relu=False, group=1, tm=1024, tn=None, tk=1024):
    M, K = a.shape
    nsh, _, n = w3.shape
    tm, tk = min(tm, M), min(tk, K)
    tn = group * n if group > 1 else (tn or min(n, 1024))
    npt, nk = max(n // tn, 1), K // tk
    nj, N = nsh * npt // group, nsh * n

    def body(a_ref, w_ref, out_ref, *scratch):
        acc_ref = scratch[0] if nk > 1 else None

        def finish(acc):
            if relu:
                acc = jnp.maximum(acc, 0.0)
            out_ref[...] = acc.astype(out_ref.dtype)

        a_val = a_ref[...]
        w_val = w_ref[...] if group == 1 else jnp.concatenate([w_ref[s] for s in range(group)], axis=1)
        part = jnp.dot(a_val, w_val, preferred_element_type=F32)
        if nk == 1:
            finish(part)
        else:
            k = pl.program_id(2)

            @pl.when(k == 0)
            def _():
                acc_ref[...] = part

            @pl.when(k > 0)
            def _():
                acc_ref[...] += part

            @pl.when(k == nk - 1)
            def _():
                finish(acc_ref[...])

    w_spec = (pl.BlockSpec((None, tk, tn), lambda i, j, k: (j // npt, k, j % npt)) if group == 1 else
              pl.BlockSpec((group, tk, n), lambda i, j, k: (j, k, 0)))
    return pl.pallas_call(
        body, name=name, grid=(M // tm, nj, nk),
        in_specs=[pl.BlockSpec((tm, tk), lambda i, j, k: (i, k)), w_spec],
        out_specs=pl.BlockSpec((tm, tn), lambda i, j, k: (i, j)), out_shape=_sds((M, N), out_dtype),
        scratch_shapes=[pltpu.VMEM((tm, tn), F32)] if nk > 1 else [],
        compiler_params=_params("parallel", "parallel", "arbitrary"))(a, w3)


def _mm_nt(dy, w3, *, name, out_dtype=F32, mul=None, tm=1024, tn=None, tko=1024):
    M, N = dy.shape
    nsh, K, n = w3.shape
    tm, tko = min(tm, M), min(tko, K)
    tn = tn or min(n, 1024)
    npt = n // tn
    nj = nsh * npt

    def body(*refs):
        dy_ref, w_ref = refs[0], refs[1]
        pos = 2
        mul_ref = None
        if mul is not None:
            mul_ref = refs[pos]
            pos += 1
        out_ref = refs[pos]
        acc_ref = refs[pos + 1] if nj > 1 else None

        def finish(acc):
            if mul_ref is not None:
                acc = acc * (2.0 * mul_ref[...].astype(F32))
            out_ref[...] = acc.astype(out_ref.dtype)

        part = lax.dot_general(dy_ref[...], w_ref[...], (((1,), (1,)), ((), ())), preferred_element_type=F32)
        if nj == 1:
            finish(part)
        else:
            j = pl.program_id(2)

            @pl.when(j == 0)
            def _():
                acc_ref[...] = part

            @pl.when(j > 0)
            def _():
                acc_ref[...] += part

            @pl.when(j == nj - 1)
            def _():
                finish(acc_ref[...])

    in_specs = [pl.BlockSpec((tm, tn), lambda i, ko, j: (i, j)),
                pl.BlockSpec((None, tko, tn), lambda i, ko, j: (j // npt, ko, j % npt))]
    args = [dy, w3]
    if mul is not None:
        in_specs.append(pl.BlockSpec((tm, tko), lambda i, ko, j: (i, ko)))
        args.append(mul)
    return pl.pallas_call(
        body, name=name, grid=(M // tm, K // tko, nj), in_specs=in_specs,
        out_specs=pl.BlockSpec((tm, tko), lambda i, ko, j: (i, ko)), out_shape=_sds((M, K), out_dtype),
        scratch_shapes=[pltpu.VMEM((tm, tko), F32)] if nj > 1 else [],
        compiler_params=_params("parallel", "parallel", "arbitrary"))(*args)


def _mm_tn(a, dy, nsh, *, name, out_dtype=BF16, a_squared=False, group=1, tm=4096, tk=1024, tn=None):
    M, K = a.shape
    N = dy.shape[1]
    n = N // nsh
    tm, tk = min(tm, M), min(tk, K)
    tn = group * n if group > 1 else (tn or min(n, 1024))
    npt = max(n // tn, 1)
    nj, nm = nsh * npt // group, M // tm

    def body(a_ref, dy_ref, out_ref, acc_ref):
        m = pl.program_id(2)
        a_val = _squared(a_ref[...]) if a_squared else a_ref[...]
        part = lax.dot_general(a_val, dy_ref[...], (((0,), (0,)), ((), ())), preferred_element_type=F32)

        @pl.when(m == 0)
        def _():
            acc_ref[...] = part

        @pl.when(m > 0)
        def _():
            acc_ref[...] += part

        @pl.when(m == nm - 1)
        def _():
            if group == 1:
                out_ref[...] = acc_ref[...].astype(out_ref.dtype)
            else:
                for s in range(group):
                    out_ref[s] = acc_ref[:, s * n:(s + 1) * n].astype(out_ref.dtype)

    out_spec = (pl.BlockSpec((None, tk, tn), lambda kk, j, m: (j // npt, kk, j % npt)) if group == 1 else
                pl.BlockSpec((group, tk, n), lambda kk, j, m: (j, kk, 0)))
    return pl.pallas_call(
        body, name=name, grid=(K // tk, nj, nm),
        in_specs=[pl.BlockSpec((tm, tk), lambda kk, j, m: (m, kk)),
                  pl.BlockSpec((tm, tn), lambda kk, j, m: (m, j))],
        out_specs=out_spec,
        out_shape=_sds((nsh, K, n), out_dtype),
        scratch_shapes=[pltpu.VMEM((tk, tn), F32)],
        compiler_params=_params("parallel", "parallel", "arbitrary"))(a, dy)


def _rms_fwd(x, g, *, name, tm=512):
    M, Dm = x.shape
    tm = min(tm, M)

    def body(x_ref, g_ref, o_ref):
        xf = x_ref[...]
        r = lax.rsqrt(jnp.mean(xf * xf, axis=-1, keepdims=True) + EPS)
        o_ref[...] = (xf * r * g_ref[...]).astype(o_ref.dtype)

    return pl.pallas_call(
        body, name=name, grid=(M // tm,),
        in_specs=[pl.BlockSpec((tm, Dm), lambda i: (i, 0)), pl.BlockSpec((1, Dm), lambda i: (0, 0))],
        out_specs=pl.BlockSpec((tm, Dm), lambda i: (i, 0)), out_shape=_sds((M, Dm), BF16),
        compiler_params=_params("parallel"))(x, g)


def _rms_bwd(dxn, x, g, dres, *, name, bf16_copy=True, tm=512):
    M, Dm = x.shape
    tm = min(tm, M)
    has_res = dres is not None

    def body(*refs):
        dxn_ref, x_ref, g_ref = refs[:3]
        dres_ref = refs[3] if has_res else None
        dx_ref, dg_ref = refs[-1 - 1 - bf16_copy], refs[-1]
        dxb_ref = refs[-2] if bf16_copy else None
        i = pl.program_id(0)
        xf = x_ref[...]
        r = lax.rsqrt(jnp.mean(xf * xf, axis=-1, keepdims=True) + EPS)
        nrm = xf * r
        dxn_f = dxn_ref[...].astype(F32)
        dn = dxn_f * g_ref[...]
        dx = r * (dn - nrm * jnp.mean(dn * nrm, axis=-1, keepdims=True))
        if has_res:
            dx = dx + dres_ref[...]
        dx_ref[...] = dx
        if bf16_copy:
            dxb_ref[...] = dx.astype(dxb_ref.dtype)

        @pl.when(i == 0)
        def _():
            dg_ref[...] = jnp.zeros_like(dg_ref)

        dg_ref[0:1, :] += jnp.sum(dxn_f * nrm, axis=0, keepdims=True)

    row = pl.BlockSpec((tm, Dm), lambda i: (i, 0))
    in_specs = [row, row, pl.BlockSpec((1, Dm), lambda i: (0, 0))] + ([row] if has_res else [])
    args = [dxn, x, g] + ([dres] if has_res else [])
    out = pl.pallas_call(
        body, name=name, grid=(M // tm,), in_specs=in_specs,
        out_specs=[row] * (1 + bf16_copy) + [pl.BlockSpec((8, Dm), lambda i: (0, 0))],
        out_shape=[_sds((M, Dm), F32)] + [_sds((M, Dm), BF16)] * bf16_copy + [_sds((8, Dm), F32)],
        compiler_params=_params("arbitrary"))(*args)
    return out[0], (out[1] if bf16_copy else None), out[-1]


def _mm_rows(a, w3, tail, *, name, rows_in=(), vecs_in=(), rows_out=(), sums_out=(), a_squared=False,
             w_transposed=False, tm=512):
    parts = a if isinstance(a, (tuple, list)) else (a,)
    M = parts[0].shape[0]
    K, N = (w3.shape[0] * w3.shape[2], w3.shape[1]) if w_transposed else (w3.shape[1], w3.shape[2])
    tm = min(tm, M)
    n_a, n_ri, n_vi, n_ro = len(parts), len(rows_in), len(vecs_in), len(rows_out)

    def body(*refs):
        a_refs, w_ref, refs = refs[:n_a], refs[n_a], refs[n_a + 1:]
        rin, vin = refs[:n_ri], refs[n_ri:n_ri + n_vi]
        rout, sout = refs[n_ri + n_vi:n_ri + n_vi + n_ro], refs[n_ri + n_vi + n_ro:]

        @pl.when(pl.program_id(0) == 0)
        def _():
            for s in sout:
                s[...] = jnp.zeros_like(s)

        a_val = a_refs[0][...] if n_a == 1 else jnp.concatenate([r[...] for r in a_refs], axis=1)
        if a_squared:
            a_val = _squared(a_val)
        if w_transposed:
            n = w3.shape[2]
            prod = _nt(a_val[:, 0:n], w_ref[0])
            for j in range(1, w3.shape[0]):
                prod = prod + _nt(a_val[:, j * n:(j + 1) * n], w_ref[j])
        else:
            prod = jnp.dot(a_val, w_ref[0], preferred_element_type=F32)
        tail(prod, rin, vin, rout, sout)

    row = pl.BlockSpec((tm, N), lambda i: (i, 0))
    once = lambda shape: pl.BlockSpec(shape, lambda i: (0,) * len(shape))
    return pl.pallas_call(
        body, name=name, grid=(M // tm,),
        in_specs=[pl.BlockSpec((tm, p.shape[1]), lambda i: (i, 0)) for p in parts] + [once(w3.shape)]
        + [row] * n_ri + [once((1, N))] * n_vi,
        out_specs=[row] * n_ro + [once(s) for s in sums_out],
        out_shape=[_sds((M, N), dt) for dt in rows_out] + [_sds(s, F32) for s in sums_out],
        compiler_params=_params("arbitrary"))(*parts, w3, *rows_in, *vecs_in)


def _residual_norm_tail(prod, rows_in, vecs_in, rows_out, sums_out):
    hf = prod + rows_in[0][...]
    rows_out[0][...] = hf
    r = lax.rsqrt(jnp.mean(hf * hf, axis=-1, keepdims=True) + EPS)
    rows_out[1][...] = (hf * r * vecs_in[0][...]).astype(BF16)


def _rms_bwd_tail(bf16_copy):
    def tail(dxn, rows_in, vecs_in, rows_out, sums_out):
        xf = rows_in[0][...]
        r = lax.rsqrt(jnp.mean(xf * xf, axis=-1, keepdims=True) + EPS)
        nrm = xf * r
        dn = dxn * vecs_in[0][...]
        dx = r * (dn - nrm * jnp.mean(dn * nrm, axis=-1, keepdims=True)) + rows_in[1][...]
        rows_out[0][...] = dx
        if bf16_copy:
            rows_out[1][...] = dx.astype(BF16)
        sums_out[0][0:1, :] += jnp.sum(dxn * nrm, axis=0, keepdims=True)

    return tail


def _loss_tail(prod, rows_in, vecs_in, rows_out, sums_out):
    hf = prod + rows_in[0][...]
    r = lax.rsqrt(jnp.mean(hf * hf, axis=-1, keepdims=True) + EPS)
    nrm = hf * r
    gv = vecs_in[0][...]
    err = nrm * gv - rows_in[1][...]
    dy = err * (1.0 / hf.shape[-1])
    dn = dy * gv
    dh = r * (dn - nrm * jnp.mean(dn * nrm, axis=-1, keepdims=True))
    rows_out[0][...] = dh
    rows_out[1][...] = dh.astype(BF16)
    sums_out[0][0:1, :] += jnp.sum(dy * nrm, axis=0, keepdims=True)
    part = 0.5 * jnp.sum(jnp.mean(err * err, axis=-1, keepdims=True), axis=0, keepdims=True)
    sel = (lax.broadcasted_iota(jnp.int32, (8, 128), 0) == 0) & (lax.broadcasted_iota(jnp.int32, (8, 128), 1) == 0)
    sums_out[1][...] += jnp.where(sel, part, 0.0)


def _class_spec(tm, d, width):
    return pl.BlockSpec((d, tm // d, width), lambda i: (0, i, 0))


def _row_scratch(tm, width):
    return pltpu.VMEM((width // LANES, tm, LANES), F32)


def _fill(scr, val):
    for c in range(scr.shape[0]):
        scr[c] = val[:, c * LANES:(c + 1) * LANES]


def _to_classes(scr, out_ref, d):
    n = scr.shape[1] // d
    for r in range(d):
        for c in range(scr.shape[0]):
            out_ref[r, :, c * LANES:(c + 1) * LANES] = scr[c, pl.ds(r, n, stride=d), :].astype(out_ref.dtype)


def _from_classes(in_ref, scr, d):
    n = scr.shape[1] // d
    for r in range(d):
        blk = in_ref[r].astype(F32)
        for c in range(scr.shape[0]):
            scr[c, pl.ds(r, n, stride=d), :] = blk[:, c * LANES:(c + 1) * LANES]
    return jnp.concatenate([scr[c] for c in range(scr.shape[0])], axis=1)


def _rope_tables(S):
    half = ROT_DIM // 2
    freqs = ROPE_THETA ** (-jnp.arange(0, ROT_DIM, 2, dtype=F32) / ROT_DIM)
    ang = jnp.arange(S, dtype=F32)[:, None] * freqs[None, :]
    cos, sin = jnp.cos(ang), jnp.sin(ang)
    ones = jnp.ones((S, HEAD_DIM - ROT_DIM), F32)
    zeros = jnp.zeros((S, HEAD_DIM - ROT_DIM), F32)
    zh = jnp.zeros((S, half), F32)
    c = jnp.concatenate([cos, cos, ones], axis=1)
    sa = jnp.concatenate([-sin, zh, zeros], axis=1)
    sb = jnp.concatenate([zh, sin, zeros], axis=1)
    return tuple(jnp.tile(t, (1, LANES // HEAD_DIM)) for t in (c, sa, sb))


def _rope_fwd(y, tables, *, tm=512):
    S = y.shape[0]
    W = 2 * D_ATT
    tm = min(tm, S)
    half = ROT_DIM // 2
    dils = [d for d in DILATIONS if d > 1]

    def body(y_ref, c_ref, sa_ref, sb_ref, qk_ref, *rest):
        qk_outs, v_outs = rest[:len(dils)], rest[len(dils):2 * len(dils)]
        scr_qk, scr_v = rest[2 * len(dils):]
        t = y_ref[:, 0:W].astype(F32)
        rep = W // LANES
        c, sa, sb = (jnp.tile(r[...], (1, rep)) for r in (c_ref, sa_ref, sb_ref))
        rot = t * c + pltpu.roll(t, W - half, axis=1) * sa + pltpu.roll(t, half, axis=1) * sb
        qk_ref[...] = rot.astype(qk_ref.dtype)
        _fill(scr_qk, rot)
        _fill(scr_v, y_ref[:, W:W + D_ATT].astype(F32))
        for d, qo, vo in zip(dils, qk_outs, v_outs):
            _to_classes(scr_qk, qo, d)
            _to_classes(scr_v, vo, d)

    tab = pl.BlockSpec((tm, LANES), lambda i: (i, 0))
    out = pl.pallas_call(
        body, name="rope_fwd", grid=(S // tm,),
        in_specs=[pl.BlockSpec((tm, 3 * D_ATT), lambda i: (i, 0)), tab, tab, tab],
        out_specs=[pl.BlockSpec((tm, W), lambda i: (i, 0))] + [_class_spec(tm, d, W) for d in dils]
        + [_class_spec(tm, d, D_ATT) for d in dils],
        out_shape=[_sds((S, W), BF16)] + [_sds((d, S // d, W), BF16) for d in dils]
        + [_sds((d, S // d, D_ATT), BF16) for d in dils],
        scratch_shapes=[_row_scratch(tm, W), _row_scratch(tm, D_ATT)],
        compiler_params=_params("parallel"))(y, *tables)
    qk = [out[0]] + [o.reshape(S, W) for o in out[1:1 + len(dils)]]
    v = [None] + [o.reshape(S, D_ATT) for o in out[1 + len(dils):]]
    return qk, v


def _assemble_dy(dq, dk, dv, dag, tables, *, tm=512):
    S = dag.shape[0]
    tm = min(tm, S)
    half = ROT_DIM // 2
    W = D_ATT
    n_pat = len(DILATIONS)

    def body(*refs):
        groups = [refs[g * n_pat:(g + 1) * n_pat] for g in range(3)]
        dag_ref, c_ref, sa_ref, sb_ref, o_ref, scr = refs[3 * n_pat:]
        rep = W // LANES
        c, sa, sb = (jnp.tile(r[...], (1, rep)) for r in (c_ref, sa_ref, sb_ref))

        def total(rs):
            acc = rs[0][...].astype(F32)
            for d, r in zip(DILATIONS[1:], rs[1:]):
                acc = acc + _from_classes(r, scr, d)
            return acc

        def unrope(dr):
            return dr * c + pltpu.roll(dr * sa, half, axis=1) + pltpu.roll(dr * sb, W - half, axis=1)

        o_ref[:, 0:W] = unrope(total(groups[0])).astype(o_ref.dtype)
        o_ref[:, W:2 * W] = unrope(total(groups[1])).astype(o_ref.dtype)
        o_ref[:, 2 * W:3 * W] = total(groups[2]).astype(o_ref.dtype)
        o_ref[:, 3 * W:] = dag_ref[...]

    specs = [pl.BlockSpec((tm, W), lambda i: (i, 0))] + [_class_spec(tm, d, W) for d in DILATIONS[1:]]
    tab = pl.BlockSpec((tm, LANES), lambda i: (i, 0))
    args = [a if d == 1 else a.reshape(d, S // d, W) for grp in (dq, dk, dv) for d, a in zip(DILATIONS, grp)]
    return pl.pallas_call(
        body, name="assemble_dy", grid=(S // tm,),
        in_specs=specs * 3 + [pl.BlockSpec((tm, 2 * D_CONV), lambda i: (i, 0)), tab, tab, tab],
        out_specs=pl.BlockSpec((tm, D_IN), lambda i: (i, 0)), out_shape=_sds((S, D_IN), BF16),
        scratch_shapes=[_row_scratch(tm, W)],
        compiler_params=_params("parallel"))(*args, dag, *tables)


def _seq_specs(L, tb, col):
    nb, per, nh = L // tb, tb // HALF, L // HALF
    centre = pl.BlockSpec((tb, D_ATT), lambda r, i: (r * nb + i, col))
    prev = pl.BlockSpec((HALF, D_ATT), lambda r, i: (r * nh + jnp.maximum(i * per - 1, 0), col))
    nxt = pl.BlockSpec((HALF, D_ATT), lambda r, i: (r * nh + jnp.minimum((i + 1) * per, nh - 1), col))
    return prev, centre, nxt


def _band_mask(i, tq, L):
    shape = (tq, tq + 2 * HALF)
    c_idx = lax.broadcasted_iota(jnp.int32, shape, 0)
    w_idx = lax.broadcasted_iota(jnp.int32, shape, 1)
    diff = w_idx - c_idx
    wpos = i * tq - HALF + w_idx
    return (diff >= 0) & (diff <= 2 * HALF) & (wpos >= 0) & (wpos < L)


def _lane_groups():
    for c0 in range(0, D_ATT, LANES):
        yield slice(c0, c0 + LANES)


def _first_head(rows):
    return lax.broadcasted_iota(jnp.int32, (rows, LANES), 1) < HEAD_DIM


def _split_pair(x, first):
    zero = jnp.zeros_like(x)
    return jnp.where(first, x, zero), jnp.where(first, zero, x)


def _nt(a, b):
    return lax.dot_general(a, b, (((1,), (1,)), ((), ())), preferred_element_type=F32)


def _tn(a, b):
    return lax.dot_general(a, b, (((0,), (0,)), ((), ())), preferred_element_type=F32)


ATT_SCALE = HEAD_DIM ** -0.5


def _att_fwd(qk, v_src, d, *, name):
    S = qk.shape[0]
    L = S // d
    tq = min(ATT_BLOCK, L)
    v_arr, v_col = v_src

    def body(q_ref, kp_ref, kc_ref, kn_ref, vp_ref, vc_ref, vn_ref, o_ref, lse_ref):
        i = pl.program_id(1)
        valid = _band_mask(i, tq, L)
        q = q_ref[...] * ATT_SCALE
        kwin = jnp.concatenate([kp_ref[...], kc_ref[...], kn_ref[...]], axis=0)
        vwin = jnp.concatenate([vp_ref[...], vc_ref[...], vn_ref[...]], axis=0)
        first = _first_head(tq)
        groups = list(_lane_groups())
        heads = [(ls, t) for ls in groups for t in _split_pair(q[:, ls], first)]
        s = [jnp.where(valid, _nt(t, kwin[:, ls]), NEG_INF) for ls, t in heads]
        m = [jnp.max(t, axis=-1, keepdims=True) for t in s]
        p = [jnp.exp(t - mm) for t, mm in zip(s, m)]
        den = [jnp.sum(t, axis=-1, keepdims=True) for t in p]
        o = [jnp.dot(t.astype(BF16), vwin[:, ls], preferred_element_type=F32) * (1.0 / dd)
             for t, dd, (ls, _) in zip(p, den, heads)]
        lse = [mm + jnp.log(dd) for mm, dd in zip(m, den)]
        for g, ls in enumerate(groups):
            o_ref[:, ls] = jnp.where(first, o[2 * g], o[2 * g + 1]).astype(o_ref.dtype)
            lse_ref[:, ls] = jnp.where(first, lse[2 * g], lse[2 * g + 1])

    _, qc, _ = _seq_specs(L, tq, 0)
    kp, kc, kn = _seq_specs(L, tq, 1)
    vp, vc, vn = _seq_specs(L, tq, v_col)
    out = pl.BlockSpec((tq, D_ATT), lambda r, i: (r * (L // tq) + i, 0))
    return pl.pallas_call(
        body, name=name, grid=(d, L // tq),
        in_specs=[qc, kp, kc, kn, vp, vc, vn], out_specs=[out, out],
        out_shape=[_sds((S, D_ATT), BF16), _sds((S, D_ATT), F32)],
        compiler_params=_params("parallel", "parallel"))(qk, qk, qk, qk, v_arr, v_arr, v_arr)


def _att_combine(outs, lses, *, tm=512):
    S = outs[0].shape[0]
    tm = min(tm, S)
    dils = DILATIONS[1:]
    n_d = len(dils)

    def body(*refs):
        o_refs, l_refs = refs[0:1 + n_d], refs[1 + n_d:2 + 2 * n_d]
        att_ref, lg_ref = refs[2 + 2 * n_d:4 + 2 * n_d]
        lg_outs = refs[4 + 2 * n_d:4 + 3 * n_d]
        scr = refs[4 + 3 * n_d:]
        scr_o, scr_l, scr_lg = scr[:n_d], scr[n_d:2 * n_d], scr[2 * n_d]
        ls = [l_refs[0][...]] + [_from_classes(r, s, d) for r, s, d in zip(l_refs[1:], scr_l, dils)]
        os_ = [o_refs[0][...].astype(F32)] + [_from_classes(r, s, d) for r, s, d in zip(o_refs[1:], scr_o, dils)]
        mx = ls[0]
        for l in ls[1:]:
            mx = jnp.maximum(mx, l)
        es = [jnp.exp(l - mx) for l in ls]
        tot = es[0]
        num = es[0] * os_[0]
        for e, o in zip(es[1:], os_[1:]):
            tot = tot + e
            num = num + e * o
        att_ref[...] = (num / tot).astype(att_ref.dtype)
        lg = mx + jnp.log(tot)
        lg_ref[...] = lg
        _fill(scr_lg, lg)
        for d, out in zip(dils, lg_outs):
            _to_classes(scr_lg, out, d)

    nat = pl.BlockSpec((tm, D_ATT), lambda i: (i, 0))
    specs = [nat] + [_class_spec(tm, d, D_ATT) for d in dils]
    view = lambda arrs: [arrs[0]] + [a.reshape(d, S // d, D_ATT) for a, d in zip(arrs[1:], dils)]
    out = pl.pallas_call(
        body, name="att_combine", grid=(S // tm,), in_specs=specs * 2,
        out_specs=[nat, nat] + specs[1:],
        out_shape=[_sds((S, D_ATT), BF16), _sds((S, D_ATT), F32)] + [_sds((d, S // d, D_ATT), F32) for d in dils],
        scratch_shapes=[_row_scratch(tm, D_ATT)] * (2 * n_d + 1),
        compiler_params=_params("parallel"))(*view(list(outs)), *view(list(lses)))
    return out[0], [out[1]] + [o.reshape(S, D_ATT) for o in out[2:]]


def _att_delta(dac, att, *, tm=512):
    S = att.shape[0]
    tm = min(tm, S)
    dils = DILATIONS[1:]
    n_d = len(dils)

    def body(do_ref, o_ref, dl_ref, *rest):
        dl_outs, do_outs = rest[:n_d], rest[n_d:2 * n_d]
        scr_dl, scr_do = rest[2 * n_d:]
        do = do_ref[...].astype(F32)
        prod = do * o_ref[...].astype(F32)
        per_head = [jnp.broadcast_to(jnp.sum(prod[:, h * HEAD_DIM:(h + 1) * HEAD_DIM], axis=-1, keepdims=True),
                                     (tm, HEAD_DIM)) for h in range(ATT_HEADS)]
        dl = jnp.concatenate(per_head, axis=1)
        dl_ref[...] = dl
        _fill(scr_dl, dl)
        _fill(scr_do, do)
        for d, dlo, doo in zip(dils, dl_outs, do_outs):
            _to_classes(scr_dl, dlo, d)
            _to_classes(scr_do, doo, d)

    blk = pl.BlockSpec((tm, D_ATT), lambda i: (i, 0))
    out = pl.pallas_call(
        body, name="att_delta", grid=(S // tm,), in_specs=[blk, blk],
        out_specs=[blk] + [_class_spec(tm, d, D_ATT) for d in dils] * 2,
        out_shape=[_sds((S, D_ATT), F32)] + [_sds((d, S // d, D_ATT), F32) for d in dils]
        + [_sds((d, S // d, D_ATT), BF16) for d in dils],
        scratch_shapes=[_row_scratch(tm, D_ATT), _row_scratch(tm, D_ATT)],
        compiler_params=_params("parallel"))(dac, att)
    delta = [out[0]] + [o.reshape(S, D_ATT) for o in out[1:1 + n_d]]
    do = [None] + [o.reshape(S, D_ATT) for o in out[1 + n_d:]]
    return delta, do


def _att_bwd(qk, v_src, do_src, lg, delta, d, *, name):
    S = qk.shape[0]
    L = S // d
    tq = min(ATT_BLOCK, L)
    nb, per, nh = L // tq, tq // HALF, L // HALF
    n_blocks = d * nb
    win = tq + 2 * HALF
    lead = tq - HALF
    acc_rows = lead + win
    (v_arr, v_col), (do_arr, do_col) = v_src, do_src

    def body(q_ref, kp_ref, kc_ref, kn_ref, vp_ref, vc_ref, vn_ref, do_ref, lg_ref, dl_ref,
             dq_ref, dk_ref, dv_ref, acc_k, acc_v):
        b = pl.program_id(0)
        i = lax.rem(jnp.minimum(b, n_blocks - 1), nb)

        @pl.when(b == 0)
        def _():
            acc_k[...] = jnp.zeros_like(acc_k)
            acc_v[...] = jnp.zeros_like(acc_v)

        @pl.when(b < n_blocks)
        def _():
            valid = _band_mask(i, tq, L)
            q, do = q_ref[...] * ATT_SCALE, do_ref[...]
            kwin = jnp.concatenate([kp_ref[...], kc_ref[...], kn_ref[...]], axis=0)
            vwin = jnp.concatenate([vp_ref[...], vc_ref[...], vn_ref[...]], axis=0)
            first, first_w = _first_head(tq), _first_head(win)
            groups = list(_lane_groups())
            cols = [c for ls in groups for c in (ls.start, ls.start + HEAD_DIM)]
            lanes = [ls for ls in groups for _ in range(2)]
            qh = [t for ls in groups for t in _split_pair(q[:, ls], first)]
            doh = [t for ls in groups for t in _split_pair(do[:, ls], first)]
            s = [jnp.where(valid, _nt(t, kwin[:, ls]), NEG_INF) for t, ls in zip(qh, lanes)]
            dp = [_nt(t, vwin[:, ls]) for t, ls in zip(doh, lanes)]
            p = [jnp.exp(t - lg_ref[:, c:c + 1]) for t, c in zip(s, cols)]
            ds = [(pp * (t - dl_ref[:, c:c + 1])).astype(BF16) for pp, t, c in zip(p, dp, cols)]
            dq = [jnp.dot(t, kwin[:, ls], preferred_element_type=F32) for t, ls in zip(ds, lanes)]
            dk = [_tn(t, q[:, ls]) for t, ls in zip(ds, lanes)]
            dv = [_tn(pp.astype(BF16), do[:, ls]) for pp, ls in zip(p, lanes)]
            for g, ls in enumerate(groups):
                dq_ref[:, ls] = (jnp.where(first, dq[2 * g], dq[2 * g + 1]) * ATT_SCALE).astype(dq_ref.dtype)
                acc_k[lead:, ls] += jnp.where(first_w, dk[2 * g], dk[2 * g + 1])
                acc_v[lead:, ls] += jnp.where(first_w, dv[2 * g], dv[2 * g + 1])

        for acc, out in ((acc_k, dk_ref), (acc_v, dv_ref)):
            out[...] = acc[0:tq, :].astype(out.dtype)
            kept = acc[tq:, :]
            acc[0:acc_rows - tq, :] = kept
            acc[acc_rows - tq:, :] = jnp.zeros((tq, D_ATT), F32)

    def seq(col):
        blk = lambda b: jnp.minimum(b, n_blocks - 1)
        cls = lambda b: (blk(b) // nb) * nh
        centre = pl.BlockSpec((tq, D_ATT), lambda b: (blk(b), col))
        prev = pl.BlockSpec((HALF, D_ATT), lambda b: (cls(b) + jnp.maximum((blk(b) % nb) * per - 1, 0), col))
        nxt = pl.BlockSpec((HALF, D_ATT), lambda b: (cls(b) + jnp.minimum((blk(b) % nb + 1) * per, nh - 1), col))
        return prev, centre, nxt

    _, qc, _ = seq(0)
    kp, kc, kn = seq(1)
    vp, vc, vn = seq(v_col)
    _, doc, _ = seq(do_col)
    late = pl.BlockSpec((tq, D_ATT), lambda b: (jnp.maximum(b - 1, 0), 0))
    return pl.pallas_call(
        body, name=name, grid=(n_blocks + 1,),
        in_specs=[qc, kp, kc, kn, vp, vc, vn, doc, qc, qc], out_specs=[qc, late, late],
        out_shape=[_sds((S, D_ATT), BF16)] * 3,
        scratch_shapes=[pltpu.VMEM((acc_rows, D_ATT), F32), pltpu.VMEM((acc_rows, D_ATT), F32)],
        compiler_params=_params("arbitrary"))(qk, qk, qk, qk, v_arr, v_arr, v_arr, do_arr, lg, delta)


def _sigmoid(x):
    return 1.0 / (1.0 + jnp.exp(-x))


def _halo_specs(S, T, width, col):
    last = S // HALO - 1
    per = T // HALO
    centre = pl.BlockSpec((T, width), lambda i: (i, col))
    prev = pl.BlockSpec((HALO, width), lambda i: (jnp.maximum(i * per - 1, 0), col))
    nxt = pl.BlockSpec((HALO, width), lambda i: (jnp.minimum((i + 1) * per, last), col))
    return prev, centre, nxt


def _window_scratch(T, C):
    return pltpu.VMEM((8, T + 2 * HALO, C), F32)


def _fill_window(buf, prev, centre, nxt, T):
    buf[0, 0:HALO, :] = prev
    buf[0, HALO:HALO + T, :] = centre
    buf[0, HALO + T:, :] = nxt
    rows = T + 2 * HALO - 8
    for s in range(1, 8):
        buf[s, 0:rows, :] = buf[0, s:s + rows, :]


def _tap_reads(buf, first_off, step, r0, ls):
    by_slab = {}
    for k in range(CONV_WIDTH):
        off = first_off + step * k
        by_slab.setdefault(off % 8, []).append((k, off - off % 8))
    for s, taps in by_slab.items():
        lo = min(a for _, a in taps)
        hi = max(a for _, a in taps)
        rows = buf[s, pl.ds(lo + r0, CONV_ROWS + hi - lo), ls]
        for k, a in taps:
            yield k, rows[a - lo:a - lo + CONV_ROWS]


def _depthwise(buf, w_ref, out_ref, T, C, first_off, step):
    def row_tile(t, carry):
        r0 = pl.multiple_of(t * CONV_ROWS, CONV_ROWS)
        for c0 in range(0, C, LANES):
            ls = slice(c0, c0 + LANES)
            acc = jnp.zeros((CONV_ROWS, LANES), F32)
            for k, rows in _tap_reads(buf, first_off, step, r0, ls):
                acc = acc + rows * w_ref[k:k + 1, ls]
            out_ref[pl.ds(r0, CONV_ROWS), ls] = acc
        return carry

    lax.fori_loop(0, T // CONV_ROWS, row_tile, 0)


def _conv_fwd(y, conv_w32, conv_b, ln_g, ln_b, *, T=512):
    S = y.shape[0]
    T = min(T, S)
    nblk = S // T
    C = D_CONV

    def body(ap, ac, an, gp, gc, gn, w_ref, b_ref, lg_ref, lb_ref, cv_ref, u1_ref, buf):
        i = pl.program_id(0)

        def glu(a_ref, g_ref):
            return a_ref[...].astype(F32) * _sigmoid(g_ref[...].astype(F32))

        _fill_window(buf, jnp.where(i > 0, glu(ap, gp), 0.0), glu(ac, gc),
                     jnp.where(i < nblk - 1, glu(an, gn), 0.0), T)
        _depthwise(buf, w_ref, u1_ref, T, C, HALO - CONV_PAD, 1)
        u1 = u1_ref[...] + b_ref[...]
        u1_ref[...] = u1
        mu = jnp.mean(u1, axis=-1, keepdims=True)
        xc = u1 - mu
        rstd = lax.rsqrt(jnp.mean(xc * xc, axis=-1, keepdims=True) + EPS)
        u2 = xc * rstd * lg_ref[...] + lb_ref[...]
        cv_ref[...] = (u2 * _sigmoid(u2)).astype(cv_ref.dtype)

    ap, ac, an = _halo_specs(S, T, C, 3)
    gp, gc, gn = _halo_specs(S, T, C, 4)
    vec = pl.BlockSpec((1, C), lambda i: (0, 0))
    out = pl.BlockSpec((T, C), lambda i: (i, 0))
    return pl.pallas_call(
        body, name="conv_fwd", grid=(nblk,),
        in_specs=[ap, ac, an, gp, gc, gn, pl.BlockSpec((32, C), lambda i: (0, 0)), vec, vec, vec],
        out_specs=[out, out], out_shape=[_sds((S, C), BF16), _sds((S, C), F32)],
        scratch_shapes=[_window_scratch(T, C)],
        compiler_params=_params("parallel"))(y, y, y, y, y, y, conv_w32, conv_b, ln_g, ln_b)


def _conv_bwd(dac, u1, y, conv_w32, ln_g, ln_b, *, T=512):
    S = y.shape[0]
    T = min(T, S)
    nblk = S // T
    C = D_CONV

    def body(dp, dc, dn, up, uc, un, ap, ac, an, gp, gc, gn, w_ref, lg_ref, lb_ref,
             dag_ref, dw_ref, dsm_ref, bufd, bufu, du0_scr, dw_acc):
        i = pl.program_id(0)
        lg = lg_ref[...]

        def du1_of(dcv_ref, u1_ref):
            u1 = u1_ref[...]
            mu = jnp.mean(u1, axis=-1, keepdims=True)
            xc = u1 - mu
            rstd = lax.rsqrt(jnp.mean(xc * xc, axis=-1, keepdims=True) + EPS)
            xhat = xc * rstd
            u2 = xhat * lg + lb_ref[...]
            sg = _sigmoid(u2)
            du2 = dcv_ref[...].astype(F32) * (sg * (1.0 + u2 * (1.0 - sg)))
            dxh = du2 * lg
            du1 = rstd * (dxh - jnp.mean(dxh, axis=-1, keepdims=True)
                          - xhat * jnp.mean(dxh * xhat, axis=-1, keepdims=True))
            return du1, du2, xhat

        def glu(a_ref, g_ref):
            return a_ref[...].astype(F32) * _sigmoid(g_ref[...].astype(F32))

        @pl.when(i == 0)
        def _():
            dw_ref[...] = jnp.zeros_like(dw_ref)
            dsm_ref[...] = jnp.zeros_like(dsm_ref)

        du1_c, du2_c, xhat_c = du1_of(dc, uc)
        dsm_ref[0:1, :] += jnp.sum(du1_c, axis=0, keepdims=True)
        dsm_ref[1:2, :] += jnp.sum(du2_c * xhat_c, axis=0, keepdims=True)
        dsm_ref[2:3, :] += jnp.sum(du2_c, axis=0, keepdims=True)
        _fill_window(bufd, jnp.where(i > 0, du1_of(dp, up)[0], 0.0), du1_c,
                     jnp.where(i < nblk - 1, du1_of(dn, un)[0], 0.0), T)
        _fill_window(bufu, jnp.where(i > 0, glu(ap, gp), 0.0), glu(ac, gc),
                     jnp.where(i < nblk - 1, glu(an, gn), 0.0), T)

        _depthwise(bufd, w_ref, du0_scr, T, C, HALO + CONV_PAD, -1)
        dw_acc[...] = jnp.zeros_like(dw_acc)

        def dw_tile(t, carry):
            r0 = pl.multiple_of(t * CONV_ROWS, CONV_ROWS)
            for c0 in range(0, C, LANES):
                ls = slice(c0, c0 + LANES)
                d = bufd[0, pl.ds(HALO + r0, CONV_ROWS), ls]
                for k, rows in _tap_reads(bufu, HALO - CONV_PAD, 1, r0, ls):
                    prod = d * rows
                    part = prod[0:8]
                    for j in range(8, CONV_ROWS, 8):
                        part = part + prod[j:j + 8]
                    dw_acc[k, :, ls] += part
            return carry

        lax.fori_loop(0, T // CONV_ROWS, dw_tile, 0)
        for k in range(CONV_WIDTH):
            dw_ref[k:k + 1, :] += jnp.sum(dw_acc[k], axis=0, keepdims=True)
        du0 = du0_scr[...]
        a = ac[...].astype(F32)
        sg = _sigmoid(gc[...].astype(F32))
        dag_ref[:, 0:C] = (du0 * sg).astype(dag_ref.dtype)
        dag_ref[:, C:] = (du0 * a * sg * (1.0 - sg)).astype(dag_ref.dtype)

    dp, dc, dn = _halo_specs(S, T, C, 1)
    up, uc, un = _halo_specs(S, T, C, 0)
    ap, ac, an = _halo_specs(S, T, C, 3)
    gp, gc, gn = _halo_specs(S, T, C, 4)
    vec = pl.BlockSpec((1, C), lambda i: (0, 0))
    return pl.pallas_call(
        body, name="conv_bwd", grid=(nblk,),
        in_specs=[dp, dc, dn, up, uc, un, ap, ac, an, gp, gc, gn,
                  pl.BlockSpec((32, C), lambda i: (0, 0)), vec, vec],
        out_specs=[pl.BlockSpec((T, 2 * C), lambda i: (i, 0)), pl.BlockSpec((32, C), lambda i: (0, 0)),
                   pl.BlockSpec((8, C), lambda i: (0, 0))],
        out_shape=[_sds((S, 2 * C), BF16), _sds((32, C), F32), _sds((8, C), F32)],
        scratch_shapes=[_window_scratch(T, C), _window_scratch(T, C), pltpu.VMEM((T, C), F32),
                        pltpu.VMEM((CONV_WIDTH, 8, C), F32)],
        compiler_params=_params("arbitrary"))(dac, dac, dac, u1, u1, u1, y, y, y, y, y, y, conv_w32, ln_g, ln_b)


def _xatt_fwd(xq, xk, xv, *, tm=512):
    S = xq.shape[0]
    M = xk.shape[0]
    tm = min(tm, S)
    scale = XATT_HEAD_DIM ** -0.5

    def body(q_ref, k_ref, v_ref, o_ref):
        heads = [slice(h * XATT_HEAD_DIM, (h + 1) * XATT_HEAD_DIM) for h in range(XATT_HEADS)]
        s = [_nt(q_ref[:, sl], k_ref[:, sl]) * scale for sl in heads]
        e = [jnp.exp(t - jnp.max(t, axis=-1, keepdims=True)) for t in s]
        p = [t * (1.0 / jnp.sum(t, axis=-1, keepdims=True)) for t in e]
        for sl, t in zip(heads, p):
            o_ref[:, sl] = jnp.dot(t.astype(BF16), v_ref[:, sl], preferred_element_type=F32).astype(o_ref.dtype)

    row = pl.BlockSpec((tm, D_MODEL), lambda i: (i, 0))
    full = pl.BlockSpec((M, D_MODEL), lambda i: (0, 0))
    return pl.pallas_call(
        body, name="xatt_fwd", grid=(S // tm,), in_specs=[row, full, full], out_specs=row,
        out_shape=_sds((S, D_MODEL), BF16), compiler_params=_params("parallel"))(xq, xk, xv)


def _xatt_bwd(xq, xk, xv, dxo, *, tm=512):
    S = xq.shape[0]
    M = xk.shape[0]
    tm = min(tm, S)
    scale = XATT_HEAD_DIM ** -0.5

    def body(q_ref, k_ref, v_ref, do_ref, dq_ref, dk_ref, dv_ref):
        i = pl.program_id(0)

        @pl.when(i == 0)
        def _():
            dk_ref[...] = jnp.zeros_like(dk_ref)
            dv_ref[...] = jnp.zeros_like(dv_ref)

        heads = [slice(h * XATT_HEAD_DIM, (h + 1) * XATT_HEAD_DIM) for h in range(XATT_HEADS)]
        s = [_nt(q_ref[:, sl], k_ref[:, sl]) * scale for sl in heads]
        dp = [_nt(do_ref[:, sl], v_ref[:, sl]) for sl in heads]
        e = [jnp.exp(t - jnp.max(t, axis=-1, keepdims=True)) for t in s]
        p = [t * (1.0 / jnp.sum(t, axis=-1, keepdims=True)) for t in e]
        ds = [(pp * (t - jnp.sum(t * pp, axis=-1, keepdims=True))).astype(BF16) for pp, t in zip(p, dp)]
        for sl, pp, t in zip(heads, p, ds):
            dq_ref[:, sl] = (jnp.dot(t, k_ref[:, sl], preferred_element_type=F32) * scale).astype(dq_ref.dtype)
            dv_ref[:, sl] += _tn(pp.astype(BF16), do_ref[:, sl])
            dk_ref[:, sl] += _tn(t, q_ref[:, sl]) * scale

    row = pl.BlockSpec((tm, D_MODEL), lambda i: (i, 0))
    full = pl.BlockSpec((M, D_MODEL), lambda i: (0, 0))
    return pl.pallas_call(
        body, name="xatt_bwd", grid=(S // tm,), in_specs=[row, full, full, row], out_specs=[row, full, full],
        out_shape=[_sds((S, D_MODEL), BF16), _sds((M, D_MODEL), F32), _sds((M, D_MODEL), F32)],
        compiler_params=_params("arbitrary"))(xq, xk, xv, dxo)


def _row_tile(R):
    for t in (256, 128, 64, 32, 16, 8):
        if R % t == 0:
            return t
    return R


def _sum_partials(own, recv, me, *, name):
    _, R, C = own.shape
    t = _row_tile(R)

    def body(me_ref, own_ref, r_ref, o_ref):
        o_ref[...] = ((own_ref[...].astype(F32) + r_ref[0].astype(F32)) + r_ref[1].astype(F32)) + r_ref[2].astype(F32)

    return pl.pallas_call(
        body, name=name,
        grid_spec=pltpu.PrefetchScalarGridSpec(
            num_scalar_prefetch=1, grid=(R // t,),
            in_specs=[pl.BlockSpec((None, t, C), lambda i, me_ref: (me_ref[0], i, 0)),
                      pl.BlockSpec((3, t, C), lambda i, me_ref: (0, i, 0))],
            out_specs=pl.BlockSpec((t, C), lambda i, me_ref: (i, 0))),
        out_shape=_sds((R, C), F32), compiler_params=_params("parallel"))(me, own, recv)


def _adamw_math(w, g, m, v):
    m2 = ADAM_B1 * m + (1.0 - ADAM_B1) * g
    v2 = ADAM_B2 * v + (1.0 - ADAM_B2) * (g * g)
    m_hat = m2 / (1.0 - ADAM_B1 ** ADAM_STEP)
    v_hat = v2 / (1.0 - ADAM_B2 ** ADAM_STEP)
    delta = -ADAM_LR * (m_hat / (jnp.sqrt(v_hat) + ADAM_EPS) + ADAM_WD * w)
    return delta, m2, v2


def _adamw(parts, w, m, v, *, name):
    R, C = w.shape
    t = _row_tile(R)
    n = len(parts)

    def body(*refs):
        w_ref, m_ref, v_ref = refs[n:n + 3]
        g_ref, d_ref, m2_ref, v2_ref = refs[n + 3:]
        g = refs[0][...]
        for r in refs[1:n]:
            g = g + r[...]
        delta, m2, v2 = _adamw_math(w_ref[...], g, m_ref[...], v_ref[...])
        g_ref[...] = g
        d_ref[...] = delta
        m2_ref[...] = m2
        v2_ref[...] = v2

    blk = pl.BlockSpec((t, C), lambda i: (i, 0))
    return pl.pallas_call(
        body, name=name, grid=(R // t,), in_specs=[blk] * (n + 3), out_specs=[blk] * 4,
        out_shape=[_sds((R, C), F32)] * 4, compiler_params=_params("parallel"))(*parts, w, m, v)


def _adamw_small(gathered, chip, entries):
    _, R, C = gathered.shape
    n = len(entries)
    group = D_CONV // N_CHIPS

    def body(chip_ref, g_ref, *refs):
        ins, outs, tot_ref = refs[:3 * n], refs[3 * n:7 * n], refs[7 * n]
        tot = g_ref[0]
        for k in range(1, N_DEV):
            tot = tot + g_ref[k]
        tot_ref[...] = tot
        for e, ((kind, r), _, _, _) in enumerate(entries):
            if kind == "row":
                g = tot_ref[r:r + 1, :]
            elif kind == "gain":
                g = jnp.concatenate([tot_ref[r:r + 1, :], tot_ref[r + 1:r + 2, :]], axis=1)
            else:
                g = tot_ref[r:r + CONV_WIDTH, 0:group]
                for j in range(1, N_CHIPS):
                    g = jnp.where(chip_ref[0] == j, tot_ref[r:r + CONV_WIDTH, j * group:(j + 1) * group], g)
            delta, m2, v2 = _adamw_math(ins[3 * e][...], g, ins[3 * e + 1][...], ins[3 * e + 2][...])
            for o, val in zip(outs[4 * e:4 * e + 4], (g, delta, m2, v2)):
                o[...] = val

    whole = lambda a: pl.BlockSpec(a.shape, lambda i, c: (0,) * a.ndim)
    arrays = [a for _, w, m, v in entries for a in (w, m, v)]
    out_like = [w for _, w, _, _ in entries for _ in range(4)]
    tot_like = _sds((R, C), F32)
    out = pl.pallas_call(
        body, name="adamw_small",
        grid_spec=pltpu.PrefetchScalarGridSpec(
            num_scalar_prefetch=1, grid=(1,),
            in_specs=[whole(gathered)] + [whole(a) for a in arrays],
            out_specs=[whole(a) for a in out_like] + [whole(tot_like)]),
        out_shape=[_sds(a.shape, F32) for a in out_like] + [tot_like],
        compiler_params=_params("arbitrary"))(chip, gathered, *arrays)
    return out[-1], [tuple(out[4 * e:4 * e + 4]) for e in range(n)]


def _chip_peers():
    x, y = lax.axis_index("x"), lax.axis_index("y")
    return [(1 - x, y), (x, 1 - y), (1 - x, 1 - y)]


HBM_SPEC = pl.BlockSpec(memory_space=pltpu.HBM)
SEM_SPEC = pl.BlockSpec(memory_space=pltpu.SEMAPHORE)


def _exchange_peers(mode):
    x, y, c = lax.axis_index("x"), lax.axis_index("y"), lax.axis_index("c")
    if mode == "swap":
        return [(x, y, 1 - c)]
    if mode == "all":
        flips = [(fx, fy, fc) for fx in (0, 1) for fy in (0, 1) for fc in (0, 1)][1:]
        return [(1 - x if fx else x, 1 - y if fy else y, 1 - c if fc else c) for fx, fy, fc in flips]
    return [(px, py, c) for px, py in _chip_peers()]


def _exchange_start(mode, srcs, zones, *, name):
    n = len(srcs)

    def body(*refs):
        ins, lands = refs[:n], refs[n:2 * n]
        send_sems, recv_sems = refs[2 * n:3 * n], refs[3 * n:4 * n]
        token = refs[-1]
        x, y, c = lax.axis_index("x"), lax.axis_index("y"), lax.axis_index("c")
        mine = 2 * x + y if mode == "gather" else 4 * x + 2 * y + c
        for t in range(n):
            for k, (px, py, pc) in enumerate(_exchange_peers(mode)):
                if mode in ("gather", "all"):
                    s, d = ins[t], lands[t].at[mine]
                elif mode == "scatter":
                    s, d = ins[t].at[2 * px + py], lands[t].at[k]
                else:
                    s, d = ins[t], lands[t]
                pltpu.make_async_remote_copy(src_ref=s, dst_ref=d, send_sem=send_sems[t], recv_sem=recv_sems[t],
                                             device_id=(px, py, pc), device_id_type=MESH).start()
            if mode in ("gather", "all"):
                pltpu.make_async_copy(ins[t], lands[t].at[mine], send_sems[t]).start()
        token[...] = jnp.zeros_like(token)

    hbm = lambda a: pltpu.with_memory_space_constraint(a, pltpu.HBM)
    out = pl.pallas_call(
        body, name=name,
        in_specs=[HBM_SPEC] * (2 * n),
        out_specs=[SEM_SPEC] * (2 * n) + [HBM_SPEC] * (2 * n) + [pl.BlockSpec(memory_space=pltpu.VMEM)],
        out_shape=[pltpu.SemaphoreType.DMA(())] * (2 * n)
        + [pltpu.HBM(a.shape, a.dtype) for a in list(srcs) + list(zones)] + [_sds((8, LANES), F32)],
        input_output_aliases={i: 2 * n + i for i in range(2 * n)},
        compiler_params=pltpu.CompilerParams(has_side_effects=pltpu.SideEffectType.DATAFLOW_SIDE_EFFECTING),
    )(*[hbm(a) for a in list(srcs) + list(zones)])
    return out[:n], out[n:2 * n], out[2 * n:3 * n], out[3 * n:4 * n], out[-1]


def _exchange_wait(mode, started, after, *, name):
    send_sems, recv_sems, srcs, zones, _ = started
    n = len(srcs)
    afters = tuple(after) if isinstance(after, (tuple, list)) else (after,)

    def body(*refs):
        lands = refs[n:2 * n]
        send_refs, recv_refs = refs[2 * n:3 * n], refs[3 * n:4 * n]
        me = (lax.axis_index("x"), lax.axis_index("y"), lax.axis_index("c"))
        n_remote = {"gather": N_CHIPS - 1, "scatter": N_CHIPS - 1, "all": N_DEV - 1, "swap": 1}[mode]
        for t in range(n):
            got = lands[t] if mode == "swap" else lands[t].at[pl.ds(0, n_remote)]
            sent = lands[t] if mode in ("gather", "all") else got
            pltpu.make_async_remote_copy(src_ref=sent, dst_ref=sent, send_sem=send_refs[t], recv_sem=recv_refs[t],
                                         device_id=me, device_id_type=MESH).wait_send()
            pltpu.make_async_remote_copy(src_ref=got, dst_ref=got, send_sem=send_refs[t], recv_sem=recv_refs[t],
                                         device_id=me, device_id_type=MESH).wait_recv()

    out = pl.pallas_call(
        body, name=name,
        in_specs=[HBM_SPEC] * (2 * n) + [SEM_SPEC] * (2 * n) + [pl.BlockSpec(memory_space=pl.ANY)] * len(afters),
        out_specs=[HBM_SPEC] * (2 * n),
        out_shape=[pltpu.HBM(a.shape, a.dtype) for a in list(srcs) + list(zones)],
        input_output_aliases={i: i for i in range(2 * n)},
        compiler_params=pltpu.CompilerParams(has_side_effects=pltpu.SideEffectType.DATAFLOW_SIDE_EFFECTING),
    )(*srcs, *zones, *send_sems, *recv_sems, *afters)
    return out[:n], out[n:]


def _swap_with_sibling(parts):
    n = len(parts)

    def body(*refs):
        ins, outs = refs[:n], refs[n:2 * n]
        send_sems, recv_sems = refs[2 * n:]
        sib = (lax.axis_index("x"), lax.axis_index("y"), 1 - lax.axis_index("c"))
        cps = []
        for t in range(n):
            cp = pltpu.make_async_remote_copy(
                src_ref=ins[t], dst_ref=outs[t], send_sem=send_sems.at[t], recv_sem=recv_sems.at[t],
                device_id=sib, device_id_type=MESH)
            cp.start()
            cps.append(cp)
        for cp in cps:
            cp.wait()

    any_spec = pl.BlockSpec(memory_space=pl.ANY)
    return pl.pallas_call(
        body, name="swap_with_sibling", in_specs=[any_spec] * n, out_specs=[any_spec] * n,
        out_shape=[_sds(p.shape, p.dtype) for p in parts],
        scratch_shapes=[pltpu.SemaphoreType.DMA((n,)), pltpu.SemaphoreType.DMA((n,))])(*parts)


BIG = ("w_in", "w_out", "w_xq", "w_xk", "w_xv", "w_xo", "w_up", "w_down")
COL_SHARDED = ("w_in", "w_up")


def _as_matrix(name, w4):
    if name in COL_SHARDED:
        return w4
    return w4.reshape(1, w4.shape[0] * w4.shape[1], w4.shape[2])


def _shard_layout(name, g):
    if name in COL_SHARDED:
        return g
    return g.reshape(N_CHIPS, g.shape[0] * g.shape[1] // N_CHIPS, g.shape[2])


def _local_step(x, mem, target, vecs, comm):
    S = x.shape[0]
    tables = _rope_tables(S)

    xn = _rms_fwd(x, vecs["norm_mix_g"], name="rms_mix")
    w_in, conv_w32 = comm["first"]((xn,) + tuple(tables))
    y = _mm_nn(xn, w_in, name="mm_in", tm=2048, group=2)
    qk, v_perm = _rope_fwd(y, tables)
    v_src = [(y, 2)] + [(v, 0) for v in v_perm[1:]]
    outs, lses = zip(*[_att_fwd(qk[p], v_src[p], d, name=f"att_fwd_d{d}") for p, d in enumerate(DILATIONS)])
    att, lg = _att_combine(outs, lses)
    cv, u1 = _conv_fwd(y, conv_w32, vecs["conv_b"], vecs["conv_ln_g"], vecs["conv_ln_b"])
    Wm = {k: _as_matrix(k, v) for k, v in comm["rest"]((att, cv)).items()}
    Wm["w_in"] = w_in
    h1, hn = _mm_rows((att, cv), Wm["w_out"], _residual_norm_tail, name="mm_out_rms", rows_in=(x,),
                      vecs_in=(vecs["norm_x_g"],), rows_out=(F32, BF16))
    xq = _mm_nn(hn, Wm["w_xq"], name="mm_xq")
    mn = _rms_fwd(mem, vecs["norm_mem_g"], name="rms_mem")
    xk = _mm_nn(mn, Wm["w_xk"], name="mm_xk")
    xv = _mm_nn(mn, Wm["w_xv"], name="mm_xv")
    xo = _xatt_fwd(xq, xk, xv)
    h2, hm = _mm_rows(xo, Wm["w_xo"], _residual_norm_tail, name="mm_xo_rms", rows_in=(h1,),
                      vecs_in=(vecs["norm_mlp_g"],), rows_out=(F32, BF16))
    relu_up = _mm_nn(hm, Wm["w_up"], name="mm_up", relu=True, tm=2048)
    sums = ((8, D_MODEL),)
    dh3, dh3b, dg_final, loss = _mm_rows(
        relu_up, Wm["w_down"], _loss_tail, name="mm_down_loss", rows_in=(h2, target), vecs_in=(vecs["norm_final_g"],),
        rows_out=(F32, BF16), sums_out=sums + ((8, LANES),), a_squared=True, tm=256)
    g = {}
    g["w_down"] = _mm_tn(relu_up, dh3b, 1, name="dw_down", a_squared=True)
    dup = _mm_nt(dh3b, Wm["w_down"], name="d_act", out_dtype=BF16, mul=relu_up, tm=2048)
    g["w_up"] = _mm_tn(hm, dup, N_CHIPS, name="dw_up")
    sent = comm["send_mlp"]({k: _shard_layout(k, g[k]) for k in ("w_down", "w_up")})
    dh2, dh2b, dg_mlp = _mm_rows(
        dup, Wm["w_up"], _rms_bwd_tail(True), name="d_hm_rms", w_transposed=True, rows_in=(h2, dh3),
        vecs_in=(vecs["norm_mlp_g"] + sent[0:1, 0:1],), rows_out=(F32, BF16), sums_out=sums, tm=256)
    g["w_xo"] = _mm_tn(xo, dh2b, 1, name="dw_xo")
    dxo = _mm_nt(dh2b, Wm["w_xo"], name="d_xo", out_dtype=BF16)
    dxq, dxk, dxv = _xatt_bwd(xq, xk, xv, dxo)
    g["w_xq"] = _mm_tn(hn, dxq, 1, name="dw_xq")
    dh1, dh1b, dg_x = _mm_rows(
        dxq, Wm["w_xq"], _rms_bwd_tail(True), name="d_hn_rms", w_transposed=True, rows_in=(h1, dh2),
        vecs_in=(vecs["norm_x_g"],), rows_out=(F32, BF16), sums_out=sums)
    dxkb, dxvb = dxk.astype(BF16), dxv.astype(BF16)
    g["w_xk"] = _mm_tn(mn, dxkb, 1, name="dw_xk")
    g["w_xv"] = _mm_tn(mn, dxvb, 1, name="dw_xv")
    dmn = _mm_nt(jnp.concatenate([dxkb, dxvb], axis=1),
                 jnp.concatenate([Wm["w_xk"], Wm["w_xv"]], axis=2), name="d_mn", out_dtype=BF16)
    _, _, dg_mem = _rms_bwd(dmn, mem, vecs["norm_mem_g"], None, name="rms_bwd_mem", bf16_copy=False)
    g["w_out"] = jnp.concatenate([_mm_tn(att, dh1b, 1, name="dw_out_att"), _mm_tn(cv, dh1b, 1, name="dw_out_conv")],
                                 axis=1)
    sent = comm["send_att"]({k: _shard_layout(k, g[k]) for k in ("w_out", "w_xq", "w_xk", "w_xv", "w_xo")})
    dac = _mm_nt(dh1b, Wm["w_out"], name="d_mix", out_dtype=BF16)
    dag, dconv_w, dconv_small = _conv_bwd(dac, u1, y, conv_w32, vecs["conv_ln_g"] + sent[0:1, 0:1],
                                          vecs["conv_ln_b"])
    delta, do_perm = _att_delta(dac, att)
    do_src = [(dac, 0)] + [(t, 0) for t in do_perm[1:]]
    dq, dk, dv = zip(*[_att_bwd(qk[p], v_src[p], do_src[p], lg[p], delta[p], d, name=f"att_bwd_d{d}")
                       for p, d in enumerate(DILATIONS)])
    dy = _assemble_dy(dq, dk, dv, dag, tables)
    sent = comm["send_in"]({"w_in": _mm_tn(xn, dy, N_CHIPS, name="dw_in", group=2, tm=2048)})
    grad_x, dg_mix = _mm_rows(
        dy, Wm["w_in"], _rms_bwd_tail(False), name="d_xn_rms", w_transposed=True, rows_in=(x, dh1),
        vecs_in=(vecs["norm_mix_g"] + sent[0:1, 0:1],), rows_out=(F32,), sums_out=sums)

    small = dict(conv_w=dconv_w, conv_small=dconv_small, norm_mix_g=dg_mix, norm_x_g=dg_x, norm_mem_g=dg_mem,
                 norm_mlp_g=dg_mlp, norm_final_g=dg_final, loss=loss)
    return grad_x, small


SMALL_ORDER = ("conv_w", "conv_small", "norm_mix_g", "norm_x_g", "norm_mem_g", "norm_mlp_g", "norm_final_g", "loss")


def _pack_small(small):
    rows, offs, pos = [], {}, 0
    for k in SMALL_ORDER:
        a = small[k]
        a = a.reshape(a.shape[0] * a.shape[1] // SMALL_W, SMALL_W)
        pad = (-a.shape[0]) % 8
        if pad:
            a = jnp.pad(a, ((0, pad), (0, 0)))
        rows.append(a)
        offs[k] = pos
        pos += a.shape[0]
    return jnp.concatenate(rows, axis=0), offs


def kernel(x, mem, norm_mix_g, w_in, conv_w, conv_b, conv_ln_g, conv_ln_b, w_out, norm_x_g, norm_mem_g, w_xq, w_xk, w_xv, w_xo, norm_mlp_g, w_up, w_down, norm_final_g, loss_target, m_norm_mix_g, m_w_in, m_conv_w, m_conv_b, m_conv_ln_g, m_conv_ln_b, m_w_out, m_norm_x_g, m_norm_mem_g, m_w_xq, m_w_xk, m_w_xv, m_w_xo, m_norm_mlp_g, m_w_up, m_w_down, m_norm_final_g, v_norm_mix_g, v_w_in, v_conv_w, v_conv_b, v_conv_ln_g, v_conv_ln_b, v_w_out, v_norm_x_g, v_norm_mem_g, v_w_xq, v_w_xk, v_w_xv, v_w_xo, v_norm_mlp_g, v_w_up, v_w_down, v_norm_final_g):
    names = ("norm_mix_g", "w_in", "conv_w", "conv_b", "conv_ln_g", "conv_ln_b", "w_out", "norm_x_g", "norm_mem_g",
             "w_xq", "w_xk", "w_xv", "w_xo", "norm_mlp_g", "w_up", "w_down", "norm_final_g")
    wts = dict(zip(names, (norm_mix_g, w_in, conv_w, conv_b, conv_ln_g, conv_ln_b, w_out, norm_x_g, norm_mem_g,
                           w_xq, w_xk, w_xv, w_xo, norm_mlp_g, w_up, w_down, norm_final_g)))
    mom = dict(zip(names, (m_norm_mix_g, m_w_in, m_conv_w, m_conv_b, m_conv_ln_g, m_conv_ln_b, m_w_out, m_norm_x_g,
                           m_norm_mem_g, m_w_xq, m_w_xk, m_w_xv, m_w_xo, m_norm_mlp_g, m_w_up, m_w_down, m_norm_final_g)))
    var = dict(zip(names, (v_norm_mix_g, v_w_in, v_conv_w, v_conv_b, v_conv_ln_g, v_conv_ln_b, v_w_out, v_norm_x_g,
                           v_norm_mem_g, v_w_xq, v_w_xk, v_w_xv, v_w_xo, v_norm_mlp_g, v_w_up, v_w_down, v_norm_final_g)))
    chip = 2 * lax.axis_index("x") + lax.axis_index("y")

    def zone(shard):
        return lax.empty((N_CHIPS,) + shard.shape, shard.dtype)

    conv_w_pad = jnp.pad(wts["conv_w"][0], ((0, 1), (0, 0)))
    first_shards = [wts["w_in"][0].astype(BF16), conv_w_pad]
    gathering_first = _exchange_start("gather", first_shards, [zone(s) for s in first_shards],
                                      name="gather_first_start")
    rest = tuple(k for k in BIG if k != "w_in")
    behind_first = gathering_first[4][0, 0]
    rest_shards = [(wts[k][0] + behind_first).astype(BF16) for k in rest]
    gathering = gathering_rest = _exchange_start("gather", rest_shards, [zone(s) for s in rest_shards],
                                                 name="gather_rest_start")
    sending = {}

    def wait_first(after):
        _, (w_in_all, conv_w_all) = _exchange_wait("gather", gathering_first, after, name="gather_first_wait")
        return w_in_all, jnp.transpose(conv_w_all, (1, 0, 2)).reshape(32, D_CONV)

    def wait_rest(after):
        _, zones = _exchange_wait("gather", gathering_rest, after, name="gather_rest_wait")
        return dict(zip(rest, zones))

    def send(group, grads):
        keys = tuple(grads)
        zones = [lax.empty((N_CHIPS - 1,) + grads[k].shape[1:], grads[k].dtype) for k in keys]
        sending[group] = (keys, _exchange_start("scatter", [grads[k] for k in keys], zones,
                                                name=f"scatter_{group}_start"))
        return sending[group][1][4]

    comm = dict(first=wait_first, rest=wait_rest, send_mlp=lambda grads: send("mlp", grads),
                send_att=lambda grads: send("att", grads), send_in=lambda grads: send("in", grads))
    vecs = {k: wts[k] for k in ("conv_b", "conv_ln_g", "conv_ln_b", "norm_x_g", "norm_mem_g", "norm_mlp_g")}
    vecs["norm_mix_g"] = wts["norm_mix_g"] + gathering[4][0:1, 0:1]
    vecs["norm_final_g"] = wts["norm_final_g"].reshape(1, D_MODEL)
    grad_x, small = _local_step(x[0], mem[0], loss_target[0], vecs, comm)

    packed, offs = _pack_small(small)
    me_arr = jnp.reshape(chip, (1,)).astype(jnp.int32)
    gathering_small = _exchange_start("all", [packed], [lax.empty((N_DEV,) + packed.shape, packed.dtype)],
                                      name="allgather_small_start")
    sums = {}

    def settle(group, after):
        keys, started = sending[group]
        srcs, zones = _exchange_wait("scatter", started, after, name=f"scatter_{group}_wait")
        for k, own, got in zip(keys, srcs, zones):
            sums[k] = _sum_partials(own, got, me_arr, name=f"sum_{k}")

    settle("mlp", gathering_small[4])
    settle("att", gathering_small[4])
    early = tuple(sums)
    swapping = _exchange_start("swap", [sums[k] for k in early], [lax.empty(sums[k].shape, F32) for k in early],
                               name="swap_early_start")
    settle("in", swapping[4])
    _, (gath,) = _exchange_wait("all", gathering_small, sums["w_in"], name="allgather_small_wait")

    where = {"conv_w": ("conv_w", offs["conv_w"]), "conv_b": ("row", offs["conv_small"]),
             "conv_ln_g": ("row", offs["conv_small"] + 1), "conv_ln_b": ("row", offs["conv_small"] + 2)}
    where.update({k: ("gain", offs[k]) for k in ("norm_mix_g", "norm_x_g", "norm_mem_g", "norm_mlp_g", "norm_final_g")})
    as_2d = lambda a: a.reshape(a.shape[-2] if a.ndim > 1 else 1, a.shape[-1])
    tot_small, updates = _adamw_small(gath, me_arr, [(where[k], as_2d(wts[k]), as_2d(mom[k]), as_2d(var[k]))
                                                     for k in where])
    res = dict(zip(where, updates))
    loss = tot_small[offs["loss"], 0]

    sib = {"w_in": _swap_with_sibling([sums["w_in"]])[0]}
    mine_early, sib_early = _exchange_wait("swap", swapping, sib["w_in"], name="swap_early_wait")
    sums.update(zip(early, mine_early))
    sib.update(zip(early, sib_early))
    for k in BIG:
        res[k] = _adamw([sums[k], sib[k]], wts[k][0], mom[k][0], var[k][0], name=f"adamw_{k}")

    outs = [loss, grad_x[None]]
    for j in range(4):
        outs += [res[k][j].reshape(wts[k].shape) for k in names]
    return tuple(outs)
```

```python
import jax
import jax.numpy as jnp
from jax import lax
from jax.experimental import pallas as pl
from jax.experimental.pallas import tpu as pltpu

F32 = jnp.float32
BF16 = jnp.bfloat16
MESH = pl.DeviceIdType.MESH

D_MODEL = 1024
ATT_HEADS = 8
HEAD_DIM = 64
D_ATT = ATT_HEADS * HEAD_DIM
D_CONV = D_MODEL - D_ATT
DILATIONS = (1, 4, 16)
HALF = 64
ROPE_THETA = 500000.0
ROT_DIM = HEAD_DIM // 4
CONV_WIDTH = 31
CONV_PAD = (CONV_WIDTH - 1) // 2
XATT_HEADS = 4
XATT_HEAD_DIM = D_MODEL // XATT_HEADS
D_FF = 4 * D_MODEL
D_IN = 3 * D_ATT + 2 * D_CONV
EPS = 1e-6
NEG_INF = -1e30
N_CHIPS = 4
N_DEV = 8

ADAM_LR = 0.001
ADAM_B1 = 0.9
ADAM_B2 = 0.999
ADAM_EPS = 1e-08
ADAM_WD = 0.01
ADAM_STEP = 10

VMEM_LIMIT_V7X = 56 * 1024 * 1024
LANES = 128
HALO = 16
CONV_ROWS = 64
ATT_BLOCK = 128
SMALL_W = 512


def _params(*sem):
    return pltpu.CompilerParams(dimension_semantics=sem, vmem_limit_bytes=VMEM_LIMIT_V7X)


def _sds(shape, dtype):
    return jax.ShapeDtypeStruct(shape, dtype)


def _squared(a):
    af = a.astype(F32)
    return (af * af).astype(BF16)


def _mm_nn(a, w3, *, name, out_dtype=BF16, relu=False, group=1, tm=1024, tn=None, tk=1024):
    M, K = a.shape
    nsh, _, n = w3.shape
    tm, tk = min(tm, M), min(tk, K)
    tn = group * n if group > 1 else (tn or min(n, 1024))
    npt, nk = max(n // tn, 1), K // tk
    nj, N = nsh * npt // group, nsh * n

    def body(a_ref, w_ref, out_ref, *scratch):
        acc_ref = scratch[0] if nk > 1 else None

        def finish(acc):
            if relu:
                acc = jnp.maximum(acc, 0.0)
            out_ref[...] = acc.astype(out_ref.dtype)

        a_val = a_ref[...]
        w_val = w_ref[...] if group == 1 else jnp.concatenate([w_ref[s] for s in range(group)], axis=1)
        part = jnp.dot(a_val, w_val, preferred_element_type=F32)
        if nk == 1:
            finish(part)
        else:
            k = pl.program_id(2)

            @pl.when(k == 0)
            def _():
                acc_ref[...] = part

            @pl.when(k > 0)
            def _():
                acc_ref[...] += part

            @pl.when(k == nk - 1)
            def _():
                finish(acc_ref[...])

    w_spec = (pl.BlockSpec((None, tk, tn), lambda i, j, k: (j // npt, k, j % npt)) if group == 1 else
              pl.BlockSpec((group, tk, n), lambda i, j, k: (j, k, 0)))
    return pl.pallas_call(
        body, name=name, grid=(M // tm, nj, nk),
        in_specs=[pl.BlockSpec((tm, tk), lambda i, j, k: (i, k)), w_spec],
        out_specs=pl.BlockSpec((tm, tn), lambda i, j, k: (i, j)), out_shape=_sds((M, N), out_dtype),
        scratch_shapes=[pltpu.VMEM((tm, tn), F32)] if nk > 1 else [],
        compiler_params=_params("parallel", "parallel", "arbitrary"))(a, w3)


def _mm_nt(dy, w3, *, name, out_dtype=F32, mul=None, tm=1024, tn=None, tko=1024):
    M, N = dy.shape
    nsh, K, n = w3.shape
    tm, tko = min(tm, M), min(tko, K)
    tn = tn or min(n, 1024)
    npt = n // tn
    nj = nsh * npt

    def body(*refs):
        dy_ref, w_ref = refs[0], refs[1]
        pos = 2
        mul_ref = None
        if mul is not None:
            mul_ref = refs[pos]
            pos += 1
        out_ref = refs[pos]
        acc_ref = refs[pos + 1] if nj > 1 else None

        def finish(acc):
            if mul_ref is not None:
                acc = acc * (2.0 * mul_ref[...].astype(F32))
            out_ref[...] = acc.astype(out_ref.dtype)

        part = lax.dot_general(dy_ref[...], w_ref[...], (((1,), (1,)), ((), ())), preferred_element_type=F32)
        if nj == 1:
            finish(part)
        else:
            j = pl.program_id(2)

            @pl.when(j == 0)
            def _():
                acc_ref[...] = part

            @pl.when(j > 0)
            def _():
                acc_ref[...] += part

            @pl.when(j == nj - 1)
            def _():
                finish(acc_ref[...])

    in_specs = [pl.BlockSpec((tm, tn), lambda i, ko, j: (i, j)),
                pl.BlockSpec((None, tko, tn), lambda i, ko, j: (j // npt, ko, j % npt))]
    args = [dy, w3]
    if mul is not None:
        in_specs.append(pl.BlockSpec((tm, tko), lambda i, ko, j: (i, ko)))
        args.append(mul)
    return pl.pallas_call(
        body, name=name, grid=(M // tm, K // tko, nj), in_specs=in_specs,
        out_specs=pl.BlockSpec((tm, tko), lambda i, ko, j: (i, ko)), out_shape=_sds((M, K), out_dtype),
        scratch_shapes=[pltpu.VMEM((tm, tko), F32)] if nj > 1 else [],
        compiler_params=_params("parallel", "parallel", "arbitrary"))(*args)


def _mm_tn(a, dy, nsh, *, name, out_dtype=BF16, a_squared=False, group=1, tm=4096, tk=1024, tn=None):
    M, K = a.shape
    N = dy.shape[1]
    n = N // nsh
    tm, tk = min(tm, M), min(tk, K)
    tn = group * n if group > 1 else (tn or min(n, 1024))
    npt = max(n // tn, 1)
    nj, nm = nsh * npt // group, M // tm

    def body(a_ref, dy_ref, out_ref, acc_ref):
        m = pl.program_id(2)
        a_val = _squared(a_ref[...]) if a_squared else a_ref[...]
        part = lax.dot_general(a_val, dy_ref[...], (((0,), (0,)), ((), ())), preferred_element_type=F32)

        @pl.when(m == 0)
        def _():
            acc_ref[...] = part

        @pl.when(m > 0)
        def _():
            acc_ref[...] += part

        @pl.when(m == nm - 1)
        def _():
            if group == 1:
                out_ref[...] = acc_ref[...].astype(out_ref.dtype)
            else:
                for s in range(group):
                    out_ref[s] = acc_ref[:, s * n:(s + 1) * n].astype(out_ref.dtype)

    out_spec = (pl.BlockSpec((None, tk, tn), lambda kk, j, m: (j // npt, kk, j % npt)) if group == 1 else
                pl.BlockSpec((group, tk, n), lambda kk, j, m: (j, kk, 0)))
    return pl.pallas_call(
        body, name=name, grid=(K // tk, nj, nm),
        in_specs=[pl.BlockSpec((tm, tk), lambda kk, j, m: (m, kk)),
                  pl.BlockSpec((tm, tn), lambda kk, j, m: (m, j))],
        out_specs=out_spec,
        out_shape=_sds((nsh, K, n), out_dtype),
        scratch_shapes=[pltpu.VMEM((tk, tn), F32)],
        compiler_params=_params("parallel", "parallel", "arbitrary"))(a, dy)


def _rms_fwd(x, g, *, name, tm=512):
    M, Dm = x.shape
    tm = min(tm, M)

    def body(x_ref, g_ref, o_ref):
        xf = x_ref[...]
        r = lax.rsqrt(jnp.mean(xf * xf, axis=-1, keepdims=True) + EPS)
        o_ref[...] = (xf * r * g_ref[...]).astype(o_ref.dtype)

    return pl.pallas_call(
        body, name=name, grid=(M // tm,),
        in_specs=[pl.BlockSpec((tm, Dm), lambda i: (i, 0)), pl.BlockSpec((1, Dm), lambda i: (0, 0))],
        out_specs=pl.BlockSpec((tm, Dm), lambda i: (i, 0)), out_shape=_sds((M, Dm), BF16),
        compiler_params=_params("parallel"))(x, g)


def _rms_bwd(dxn, x, g, dres, *, name, bf16_copy=True, tm=512):
    M, Dm = x.shape
    tm = min(tm, M)
    has_res = dres is not None

    def body(*refs):
        dxn_ref, x_ref, g_ref = refs[:3]
        dres_ref = refs[3] if has_res else None
        dx_ref, dg_ref = refs[-1 - 1 - bf16_copy], refs[-1]
        dxb_ref = refs[-2] if bf16_copy else None
        i = pl.program_id(0)
        xf = x_ref[...]
        r = lax.rsqrt(jnp.mean(xf * xf, axis=-1, keepdims=True) + EPS)
        nrm = xf * r
        dxn_f = dxn_ref[...].astype(F32)
        dn = dxn_f * g_ref[...]
        dx = r * (dn - nrm * jnp.mean(dn * nrm, axis=-1, keepdims=True))
        if has_res:
            dx = dx + dres_ref[...]
        dx_ref[...] = dx
        if bf16_copy:
            dxb_ref[...] = dx.astype(dxb_ref.dtype)

        @pl.when(i == 0)
        def _():
            dg_ref[...] = jnp.zeros_like(dg_ref)

        dg_ref[0:1, :] += jnp.sum(dxn_f * nrm, axis=0, keepdims=True)

    row = pl.BlockSpec((tm, Dm), lambda i: (i, 0))
    in_specs = [row, row, pl.BlockSpec((1, Dm), lambda i: (0, 0))] + ([row] if has_res else [])
    args = [dxn, x, g] + ([dres] if has_res else [])
    out = pl.pallas_call(
        body, name=name, grid=(M // tm,), in_specs=in_specs,
        out_specs=[row] * (1 + bf16_copy) + [pl.BlockSpec((8, Dm), lambda i: (0, 0))],
        out_shape=[_sds((M, Dm), F32)] + [_sds((M, Dm), BF16)] * bf16_copy + [_sds((8, Dm), F32)],
        compiler_params=_params("arbitrary"))(*args)
    return out[0], (out[1] if bf16_copy else None), out[-1]


def _mm_rows(a, w3, tail, *, name, rows_in=(), vecs_in=(), rows_out=(), sums_out=(), a_squared=False,
             w_transposed=False, tm=512):
    parts = a if isinstance(a, (tuple, list)) else (a,)
    M = parts[0].shape[0]
    K, N = (w3.shape[0] * w3.shape[2], w3.shape[1]) if w_transposed else (w3.shape[1], w3.shape[2])
    tm = min(tm, M)
    n_a, n_ri, n_vi, n_ro = len(parts), len(rows_in), len(vecs_in), len(rows_out)

    def body(*refs):
        a_refs, w_ref, refs = refs[:n_a], refs[n_a], refs[n_a + 1:]
        rin, vin = refs[:n_ri], refs[n_ri:n_ri + n_vi]
        rout, sout = refs[n_ri + n_vi:n_ri + n_vi + n_ro], refs[n_ri + n_vi + n_ro:]

        @pl.when(pl.program_id(0) == 0)
        def _():
            for s in sout:
                s[...] = jnp.zeros_like(s)

        a_val = a_refs[0][...] if n_a == 1 else jnp.concatenate([r[...] for r in a_refs], axis=1)
        if a_squared:
            a_val = _squared(a_val)
        if w_transposed:
            n = w3.shape[2]
            prod = _nt(a_val[:, 0:n], w_ref[0])
            for j in range(1, w3.shape[0]):
                prod = prod + _nt(a_val[:, j * n:(j + 1) * n], w_ref[j])
        else:
            prod = jnp.dot(a_val, w_ref[0], preferred_element_type=F32)
        tail(prod, rin, vin, rout, sout)

    row = pl.BlockSpec((tm, N), lambda i: (i, 0))
    once = lambda shape: pl.BlockSpec(shape, lambda i: (0,) * len(shape))
    return pl.pallas_call(
        body, name=name, grid=(M // tm,),
        in_specs=[pl.BlockSpec((tm, p.shape[1]), lambda i: (i, 0)) for p in parts] + [once(w3.shape)]
        + [row] * n_ri + [once((1, N))] * n_vi,
        out_specs=[row] * n_ro + [once(s) for s in sums_out],
        out_shape=[_sds((M, N), dt) for dt in rows_out] + [_sds(s, F32) for s in sums_out],
        compiler_params=_params("arbitrary"))(*parts, w3, *rows_in, *vecs_in)


def _residual_norm_tail(prod, rows_in, vecs_in, rows_out, sums_out):
    hf = prod + rows_in[0][...]
    rows_out[0][...] = hf
    r = lax.rsqrt(jnp.mean(hf * hf, axis=-1, keepdims=True) + EPS)
    rows_out[1][...] = (hf * r * vecs_in[0][...]).astype(BF16)


def _rms_bwd_tail(bf16_copy):
    def tail(dxn, rows_in, vecs_in, rows_out, sums_out):
        xf = rows_in[0][...]
        r = lax.rsqrt(jnp.mean(xf * xf, axis=-1, keepdims=True) + EPS)
        nrm = xf * r
        dn = dxn * vecs_in[0][...]
        dx = r * (dn - nrm * jnp.mean(dn * nrm, axis=-1, keepdims=True)) + rows_in[1][...]
        rows_out[0][...] = dx
        if bf16_copy:
            rows_out[1][...] = dx.astype(BF16)
        sums_out[0][0:1, :] += jnp.sum(dxn * nrm, axis=0, keepdims=True)

    return tail


def _loss_tail(prod, rows_in, vecs_in, rows_out, sums_out):
    hf = prod + rows_in[0][...]
    r = lax.rsqrt(jnp.mean(hf * hf, axis=-1, keepdims=True) + EPS)
    nrm = hf * r
    gv = vecs_in[0][...]
    err = nrm * gv - rows_in[1][...]
    dy = err * (1.0 / hf.shape[-1])
    dn = dy * gv
    dh = r * (dn - nrm * jnp.mean(dn * nrm, axis=-1, keepdims=True))
    rows_out[0][...] = dh
    rows_out[1][...] = dh.astype(BF16)
    sums_out[0][0:1, :] += jnp.sum(dy * nrm, axis=0, keepdims=True)
    part = 0.5 * jnp.sum(jnp.mean(err * err, axis=-1, keepdims=True), axis=0, keepdims=True)
    sel = (lax.broadcasted_iota(jnp.int32, (8, 128), 0) == 0) & (lax.broadcasted_iota(jnp.int32, (8, 128), 1) == 0)
    sums_out[1][...] += jnp.where(sel, part, 0.0)


def _class_spec(tm, d, width):
    return pl.BlockSpec((d, tm // d, width), lambda i: (0, i, 0))


def _row_scratch(tm, width):
    return pltpu.VMEM((width // LANES, tm, LANES), F32)


def _fill(scr, val):
    for c in range(scr.shape[0]):
        scr[c] = val[:, c * LANES:(c + 1) * LANES]


def _to_classes(scr, out_ref, d):
    n = scr.shape[1] // d
    for r in range(d):
        for c in range(scr.shape[0]):
            out_ref[r, :, c * LANES:(c + 1) * LANES] = scr[c, pl.ds(r, n, stride=d), :].astype(out_ref.dtype)


def _from_classes(in_ref, scr, d):
    n = scr.shape[1] // d
    for r in range(d):
        blk = in_ref[r].astype(F32)
        for c in range(scr.shape[0]):
            scr[c, pl.ds(r, n, stride=d), :] = blk[:, c * LANES:(c + 1) * LANES]
    return jnp.concatenate([scr[c] for c in range(scr.shape[0])], axis=1)


def _rope_tables(S):
    half = ROT_DIM // 2
    freqs = ROPE_THETA ** (-jnp.arange(0, ROT_DIM, 2, dtype=F32) / ROT_DIM)
    ang = jnp.arange(S, dtype=F32)[:, None] * freqs[None, :]
    cos, sin = jnp.cos(ang), jnp.sin(ang)
    ones = jnp.ones((S, HEAD_DIM - ROT_DIM), F32)
    zeros = jnp.zeros((S, HEAD_DIM - ROT_DIM), F32)
    zh = jnp.zeros((S, half), F32)
    c = jnp.concatenate([cos, cos, ones], axis=1)
    sa = jnp.concatenate([-sin, zh, zeros], axis=1)
    sb = jnp.concatenate([zh, sin, zeros], axis=1)
    return tuple(jnp.tile(t, (1, LANES // HEAD_DIM)) for t in (c, sa, sb))


def _rope_fwd(y, tables, *, tm=512):
    S = y.shape[0]
    W = 2 * D_ATT
    tm = min(tm, S)
    half = ROT_DIM // 2
    dils = [d for d in DILATIONS if d > 1]

    def body(y_ref, c_ref, sa_ref, sb_ref, qk_ref, *rest):
        qk_outs, v_outs = rest[:len(dils)], rest[len(dils):2 * len(dils)]
        scr_qk, scr_v = rest[2 * len(dils):]
        t = y_ref[:, 0:W].astype(F32)
        rep = W // LANES
        c, sa, sb = (jnp.tile(r[...], (1, rep)) for r in (c_ref, sa_ref, sb_ref))
        rot = t * c + pltpu.roll(t, W - half, axis=1) * sa + pltpu.roll(t, half, axis=1) * sb
        qk_ref[...] = rot.astype(qk_ref.dtype)
        _fill(scr_qk, rot)
        _fill(scr_v, y_ref[:, W:W + D_ATT].astype(F32))
        for d, qo, vo in zip(dils, qk_outs, v_outs):
            _to_classes(scr_qk, qo, d)
            _to_classes(scr_v, vo, d)

    tab = pl.BlockSpec((tm, LANES), lambda i: (i, 0))
    out = pl.pallas_call(
        body, name="rope_fwd", grid=(S // tm,),
        in_specs=[pl.BlockSpec((tm, 3 * D_ATT), lambda i: (i, 0)), tab, tab, tab],
        out_specs=[pl.BlockSpec((tm, W), lambda i: (i, 0))] + [_class_spec(tm, d, W) for d in dils]
        + [_class_spec(tm, d, D_ATT) for d in dils],
        out_shape=[_sds((S, W), BF16)] + [_sds((d, S // d, W), BF16) for d in dils]
        + [_sds((d, S // d, D_ATT), BF16) for d in dils],
        scratch_shapes=[_row_scratch(tm, W), _row_scratch(tm, D_ATT)],
        compiler_params=_params("parallel"))(y, *tables)
    qk = [out[0]] + [o.reshape(S, W) for o in out[1:1 + len(dils)]]
    v = [None] + [o.reshape(S, D_ATT) for o in out[1 + len(dils):]]
    return qk, v


def _assemble_dy(dq, dk, dv, dag, tables, *, tm=512):
    S = dag.shape[0]
    tm = min(tm, S)
    half = ROT_DIM // 2
    W = D_ATT
    n_pat = len(DILATIONS)

    def body(*refs):
        groups = [refs[g * n_pat:(g + 1) * n_pat] for g in range(3)]
        dag_ref, c_ref, sa_ref, sb_ref, o_ref, scr = refs[3 * n_pat:]
        rep = W // LANES
        c, sa, sb = (jnp.tile(r[...], (1, rep)) for r in (c_ref, sa_ref, sb_ref))

        def total(rs):
            acc = rs[0][...].astype(F32)
            for d, r in zip(DILATIONS[1:], rs[1:]):
                acc = acc + _from_classes(r, scr, d)
            return acc

        def unrope(dr):
            return dr * c + pltpu.roll(dr * sa, half, axis=1) + pltpu.roll(dr * sb, W - half, axis=1)

        o_ref[:, 0:W] = unrope(total(groups[0])).astype(o_ref.dtype)
        o_ref[:, W:2 * W] = unrope(total(groups[1])).astype(o_ref.dtype)
        o_ref[:, 2 * W:3 * W] = total(groups[2]).astype(o_ref.dtype)
        o_ref[:, 3 * W:] = dag_ref[...]

    specs = [pl.BlockSpec((tm, W), lambda i: (i, 0))] + [_class_spec(tm, d, W) for d in DILATIONS[1:]]
    tab = pl.BlockSpec((tm, LANES), lambda i: (i, 0))
    args = [a if d == 1 else a.reshape(d, S // d, W) for grp in (dq, dk, dv) for d, a in zip(DILATIONS, grp)]
    return pl.pallas_call(
        body, name="assemble_dy", grid=(S // tm,),
        in_specs=specs * 3 + [pl.BlockSpec((tm, 2 * D_CONV), lambda i: (i, 0)), tab, tab, tab],
        out_specs=pl.BlockSpec((tm, D_IN), lambda i: (i, 0)), out_shape=_sds((S, D_IN), BF16),
        scratch_shapes=[_row_scratch(tm, W)],
        compiler_params=_params("parallel"))(*args, dag, *tables)


def _seq_specs(L, tb, col):
    nb, per, nh = L // tb, tb // HALF, L // HALF
    centre = pl.BlockSpec((tb, D_ATT), lambda r, i: (r * nb + i, col))
    prev = pl.BlockSpec((HALF, D_ATT), lambda r, i: (r * nh + jnp.maximum(i * per - 1, 0), col))
    nxt = pl.BlockSpec((HALF, D_ATT), lambda r, i: (r * nh + jnp.minimum((i + 1) * per, nh - 1), col))
    return prev, centre, nxt


def _band_mask(i, tq, L):
    shape = (tq, tq + 2 * HALF)
    c_idx = lax.broadcasted_iota(jnp.int32, shape, 0)
    w_idx = lax.broadcasted_iota(jnp.int32, shape, 1)
    diff = w_idx - c_idx
    wpos = i * tq - HALF + w_idx
    return (diff >= 0) & (diff <= 2 * HALF) & (wpos >= 0) & (wpos < L)


def _lane_groups():
    for c0 in range(0, D_ATT, LANES):
        yield slice(c0, c0 + LANES)


def _first_head(rows):
    return lax.broadcasted_iota(jnp.int32, (rows, LANES), 1) < HEAD_DIM


def _split_pair(x, first):
    zero = jnp.zeros_like(x)
    return jnp.where(first, x, zero), jnp.where(first, zero, x)


def _nt(a, b):
    return lax.dot_general(a, b, (((1,), (1,)), ((), ())), preferred_element_type=F32)


def _tn(a, b):
    return lax.dot_general(a, b, (((0,), (0,)), ((), ())), preferred_element_type=F32)


ATT_SCALE = HEAD_DIM ** -0.5


def _att_fwd(qk, v_src, d, *, name):
    S = qk.shape[0]
    L = S // d
    tq = min(ATT_BLOCK, L)
    n_sub = 2 if L % (2 * tq) == 0 else 1
    tb = n_sub * tq
    v_arr, v_col = v_src

    def body(q_ref, kp_ref, kc_ref, kn_ref, vp_ref, vc_ref, vn_ref, o_ref, lse_ref):
        i = pl.program_id(1)
        q_all = q_ref[...] * ATT_SCALE
        k_all = jnp.concatenate([kp_ref[...], kc_ref[...], kn_ref[...]], axis=0)
        v_all = jnp.concatenate([vp_ref[...], vc_ref[...], vn_ref[...]], axis=0)
        first = _first_head(tq)
        groups = list(_lane_groups())
        for b in range(n_sub):
            rows = slice(b * tq, (b + 1) * tq)
            valid = _band_mask(i * n_sub + b, tq, L)
            q = q_all[rows]
            kwin, vwin = k_all[b * tq:(b + 1) * tq + 2 * HALF], v_all[b * tq:(b + 1) * tq + 2 * HALF]
            heads = [(ls, t) for ls in groups for t in _split_pair(q[:, ls], first)]
            s = [jnp.where(valid, _nt(t, kwin[:, ls]), NEG_INF) for ls, t in heads]
            m = [jnp.max(t, axis=-1, keepdims=True) for t in s]
            p = [jnp.exp(t - mm) for t, mm in zip(s, m)]
            den = [jnp.sum(t, axis=-1, keepdims=True) for t in p]
            o = [jnp.dot(t.astype(BF16), vwin[:, ls], preferred_element_type=F32) * (1.0 / dd)
                 for t, dd, (ls, _) in zip(p, den, heads)]
            lse = [mm + jnp.log(dd) for mm, dd in zip(m, den)]
            for g, ls in enumerate(groups):
                o_ref[rows, ls] = jnp.where(first, o[2 * g], o[2 * g + 1]).astype(o_ref.dtype)
                lse_ref[rows, ls] = jnp.where(first, lse[2 * g], lse[2 * g + 1])

    _, qc, _ = _seq_specs(L, tb, 0)
    kp, kc, kn = _seq_specs(L, tb, 1)
    vp, vc, vn = _seq_specs(L, tb, v_col)
    out = pl.BlockSpec((tb, D_ATT), lambda r, i: (r * (L // tb) + i, 0))
    return pl.pallas_call(
        body, name=name, grid=(d, L // tb),
        in_specs=[qc, kp, kc, kn, vp, vc, vn], out_specs=[out, out],
        out_shape=[_sds((S, D_ATT), BF16), _sds((S, D_ATT), F32)],
        compiler_params=_params("parallel", "parallel"))(qk, qk, qk, qk, v_arr, v_arr, v_arr)


def _att_combine(outs, lses, *, tm=512):
    S = outs[0].shape[0]
    tm = min(tm, S)
    dils = DILATIONS[1:]
    n_d = len(dils)

    def body(*refs):
        o_refs, l_refs = refs[0:1 + n_d], refs[1 + n_d:2 + 2 * n_d]
        att_ref, lg_ref = refs[2 + 2 * n_d:4 + 2 * n_d]
        lg_outs = refs[4 + 2 * n_d:4 + 3 * n_d]
        scr = refs[4 + 3 * n_d:]
        scr_o, scr_l, scr_lg = scr[:n_d], scr[n_d:2 * n_d], scr[2 * n_d]
        ls = [l_refs[0][...]] + [_from_classes(r, s, d) for r, s, d in zip(l_refs[1:], scr_l, dils)]
        os_ = [o_refs[0][...].astype(F32)] + [_from_classes(r, s, d) for r, s, d in zip(o_refs[1:], scr_o, dils)]
        mx = ls[0]
        for l in ls[1:]:
            mx = jnp.maximum(mx, l)
        es = [jnp.exp(l - mx) for l in ls]
        tot = es[0]
        num = es[0] * os_[0]
        for e, o in zip(es[1:], os_[1:]):
            tot = tot + e
            num = num + e * o
        att_ref[...] = (num / tot).astype(att_ref.dtype)
        lg = mx + jnp.log(tot)
        lg_ref[...] = lg
        _fill(scr_lg, lg)
        for d, out in zip(dils, lg_outs):
            _to_classes(scr_lg, out, d)

    nat = pl.BlockSpec((tm, D_ATT), lambda i: (i, 0))
    specs = [nat] + [_class_spec(tm, d, D_ATT) for d in dils]
    view = lambda arrs: [arrs[0]] + [a.reshape(d, S // d, D_ATT) for a, d in zip(arrs[1:], dils)]
    out = pl.pallas_call(
        body, name="att_combine", grid=(S // tm,), in_specs=specs * 2,
        out_specs=[nat, nat] + specs[1:],
        out_shape=[_sds((S, D_ATT), BF16), _sds((S, D_ATT), F32)] + [_sds((d, S // d, D_ATT), F32) for d in dils],
        scratch_shapes=[_row_scratch(tm, D_ATT)] * (2 * n_d + 1),
        compiler_params=_params("parallel"))(*view(list(outs)), *view(list(lses)))
    return out[0], [out[1]] + [o.reshape(S, D_ATT) for o in out[2:]]


def _att_delta(dac, att, *, tm=512):
    S = att.shape[0]
    tm = min(tm, S)
    dils = DILATIONS[1:]
    n_d = len(dils)

    def body(do_ref, o_ref, dl_ref, *rest):
        dl_outs, do_outs = rest[:n_d], rest[n_d:2 * n_d]
        scr_dl, scr_do = rest[2 * n_d:]
        do = do_ref[...].astype(F32)
        prod = do * o_ref[...].astype(F32)
        per_head = [jnp.broadcast_to(jnp.sum(prod[:, h * HEAD_DIM:(h + 1) * HEAD_DIM], axis=-1, keepdims=True),
                                     (tm, HEAD_DIM)) for h in range(ATT_HEADS)]
        dl = jnp.concatenate(per_head, axis=1)
        dl_ref[...] = dl
        _fill(scr_dl, dl)
        _fill(scr_do, do)
        for d, dlo, doo in zip(dils, dl_outs, do_outs):
            _to_classes(scr_dl, dlo, d)
            _to_classes(scr_do, doo, d)

    blk = pl.BlockSpec((tm, D_ATT), lambda i: (i, 0))
    out = pl.pallas_call(
        body, name="att_delta", grid=(S // tm,), in_specs=[blk, blk],
        out_specs=[blk] + [_class_spec(tm, d, D_ATT) for d in dils] * 2,
        out_shape=[_sds((S, D_ATT), F32)] + [_sds((d, S // d, D_ATT), F32) for d in dils]
        + [_sds((d, S // d, D_ATT), BF16) for d in dils],
        scratch_shapes=[_row_scratch(tm, D_ATT), _row_scratch(tm, D_ATT)],
        compiler_params=_params("parallel"))(dac, att)
    delta = [out[0]] + [o.reshape(S, D_ATT) for o in out[1:1 + n_d]]
    do = [None] + [o.reshape(S, D_ATT) for o in out[1 + n_d:]]
    return delta, do


def _att_bwd(qk, v_src, do_src, lg, delta, d, *, name):
    S = qk.shape[0]
    L = S // d
    tq = min(ATT_BLOCK, L)
    nb, per, nh = L // tq, tq // HALF, L // HALF
    n_blocks = d * nb
    win = tq + 2 * HALF
    lead = tq - HALF
    acc_rows = lead + win
    (v_arr, v_col), (do_arr, do_col) = v_src, do_src

    def body(q_ref, kp_ref, kc_ref, kn_ref, vp_ref, vc_ref, vn_ref, do_ref, lg_ref, dl_ref,
             dq_ref, dk_ref, dv_ref, acc_k, acc_v):
        b = pl.program_id(0)
        i = lax.rem(jnp.minimum(b, n_blocks - 1), nb)

        @pl.when(b == 0)
        def _():
            acc_k[...] = jnp.zeros_like(acc_k)
            acc_v[...] = jnp.zeros_like(acc_v)

        @pl.when(b < n_blocks)
        def _():
            valid = _band_mask(i, tq, L)
            q, do = q_ref[...] * ATT_SCALE, do_ref[...]
            kwin = jnp.concatenate([kp_ref[...], kc_ref[...], kn_ref[...]], axis=0)
            vwin = jnp.concatenate([vp_ref[...], vc_ref[...], vn_ref[...]], axis=0)
            first, first_w = _first_head(tq), _first_head(win)
            groups = list(_lane_groups())
            cols = [c for ls in groups for c in (ls.start, ls.start + HEAD_DIM)]
            lanes = [ls for ls in groups for _ in range(2)]
            qh = [t for ls in groups for t in _split_pair(q[:, ls], first)]
            doh = [t for ls in groups for t in _split_pair(do[:, ls], first)]
            s = [jnp.where(valid, _nt(t, kwin[:, ls]), NEG_INF) for t, ls in zip(qh, lanes)]
            dp = [_nt(t, vwin[:, ls]) for t, ls in zip(doh, lanes)]
            p = [jnp.exp(t - lg_ref[:, c:c + 1]) for t, c in zip(s, cols)]
            ds = [(pp * (t - dl_ref[:, c:c + 1])).astype(BF16) for pp, t, c in zip(p, dp, cols)]
            dq = [jnp.dot(t, kwin[:, ls], preferred_element_type=F32) for t, ls in zip(ds, lanes)]
            dk = [_tn(t, q[:, ls]) for t, ls in zip(ds, lanes)]
            dv = [_tn(pp.astype(BF16), do[:, ls]) for pp, ls in zip(p, lanes)]
            for g, ls in enumerate(groups):
                dq_ref[:, ls] = (jnp.where(first, dq[2 * g], dq[2 * g + 1]) * ATT_SCALE).astype(dq_ref.dtype)
                acc_k[lead:, ls] += jnp.where(first_w, dk[2 * g], dk[2 * g + 1])
                acc_v[lead:, ls] += jnp.where(first_w, dv[2 * g], dv[2 * g + 1])

        for acc, out in ((acc_k, dk_ref), (acc_v, dv_ref)):
            out[...] = acc[0:tq, :].astype(out.dtype)
            kept = acc[tq:, :]
            acc[0:acc_rows - tq, :] = kept
            acc[acc_rows - tq:, :] = jnp.zeros((tq, D_ATT), F32)

    def seq(col):
        blk = lambda b: jnp.minimum(b, n_blocks - 1)
        cls = lambda b: (blk(b) // nb) * nh
        centre = pl.BlockSpec((tq, D_ATT), lambda b: (blk(b), col))
        prev = pl.BlockSpec((HALF, D_ATT), lambda b: (cls(b) + jnp.maximum((blk(b) % nb) * per - 1, 0), col))
        nxt = pl.BlockSpec((HALF, D_ATT), lambda b: (cls(b) + jnp.minimum((blk(b) % nb + 1) * per, nh - 1), col))
        return prev, centre, nxt

    _, qc, _ = seq(0)
    kp, kc, kn = seq(1)
    vp, vc, vn = seq(v_col)
    _, doc, _ = seq(do_col)
    late = pl.BlockSpec((tq, D_ATT), lambda b: (jnp.maximum(b - 1, 0), 0))
    return pl.pallas_call(
        body, name=name, grid=(n_blocks + 1,),
        in_specs=[qc, kp, kc, kn, vp, vc, vn, doc, qc, qc], out_specs=[qc, late, late],
        out_shape=[_sds((S, D_ATT), BF16)] * 3,
        scratch_shapes=[pltpu.VMEM((acc_rows, D_ATT), F32), pltpu.VMEM((acc_rows, D_ATT), F32)],
        compiler_params=_params("arbitrary"))(qk, qk, qk, qk, v_arr, v_arr, v_arr, do_arr, lg, delta)


def _sigmoid(x):
    return 1.0 / (1.0 + jnp.exp(-x))


def _halo_specs(S, T, width, col):
    last = S // HALO - 1
    per = T // HALO
    centre = pl.BlockSpec((T, width), lambda i: (i, col))
    prev = pl.BlockSpec((HALO, width), lambda i: (jnp.maximum(i * per - 1, 0), col))
    nxt = pl.BlockSpec((HALO, width), lambda i: (jnp.minimum((i + 1) * per, last), col))
    return prev, centre, nxt


def _window_scratch(T, C):
    return pltpu.VMEM((8, T + 2 * HALO, C), F32)


def _fill_window(buf, prev, centre, nxt, T):
    buf[0, 0:HALO, :] = prev
    buf[0, HALO:HALO + T, :] = centre
    buf[0, HALO + T:, :] = nxt
    rows = T + 2 * HALO - 8
    for s in range(1, 8):
        buf[s, 0:rows, :] = buf[0, s:s + rows, :]


def _tap_reads(buf, first_off, step, r0, ls):
    by_slab = {}
    for k in range(CONV_WIDTH):
        off = first_off + step * k
        by_slab.setdefault(off % 8, []).append((k, off - off % 8))
    for s, taps in by_slab.items():
        lo = min(a for _, a in taps)
        hi = max(a for _, a in taps)
        rows = buf[s, pl.ds(lo + r0, CONV_ROWS + hi - lo), ls]
        for k, a in taps:
            yield k, rows[a - lo:a - lo + CONV_ROWS]


def _depthwise(buf, w_ref, out_ref, T, C, first_off, step):
    def row_tile(t, carry):
        r0 = pl.multiple_of(t * CONV_ROWS, CONV_ROWS)
        for c0 in range(0, C, LANES):
            ls = slice(c0, c0 + LANES)
            acc = jnp.zeros((CONV_ROWS, LANES), F32)
            for k, rows in _tap_reads(buf, first_off, step, r0, ls):
                acc = acc + rows * w_ref[k:k + 1, ls]
            out_ref[pl.ds(r0, CONV_ROWS), ls] = acc
        return carry

    lax.fori_loop(0, T // CONV_ROWS, row_tile, 0)


def _conv_fwd(y, conv_w32, conv_b, ln_g, ln_b, *, T=512):
    S = y.shape[0]
    T = min(T, S)
    nblk = S // T
    C = D_CONV

    def body(ap, ac, an, gp, gc, gn, w_ref, b_ref, lg_ref, lb_ref, cv_ref, u1_ref, buf):
        i = pl.program_id(0)

        def glu(a_ref, g_ref):
            return a_ref[...].astype(F32) * _sigmoid(g_ref[...].astype(F32))

        _fill_window(buf, jnp.where(i > 0, glu(ap, gp), 0.0), glu(ac, gc),
                     jnp.where(i < nblk - 1, glu(an, gn), 0.0), T)
        _depthwise(buf, w_ref, u1_ref, T, C, HALO - CONV_PAD, 1)
        u1 = u1_ref[...] + b_ref[...]
        u1_ref[...] = u1
        mu = jnp.mean(u1, axis=-1, keepdims=True)
        xc = u1 - mu
        rstd = lax.rsqrt(jnp.mean(xc * xc, axis=-1, keepdims=True) + EPS)
        u2 = xc * rstd * lg_ref[...] + lb_ref[...]
        cv_ref[...] = (u2 * _sigmoid(u2)).astype(cv_ref.dtype)

    ap, ac, an = _halo_specs(S, T, C, 3)
    gp, gc, gn = _halo_specs(S, T, C, 4)
    vec = pl.BlockSpec((1, C), lambda i: (0, 0))
    out = pl.BlockSpec((T, C), lambda i: (i, 0))
    return pl.pallas_call(
        body, name="conv_fwd", grid=(nblk,),
        in_specs=[ap, ac, an, gp, gc, gn, pl.BlockSpec((32, C), lambda i: (0, 0)), vec, vec, vec],
        out_specs=[out, out], out_shape=[_sds((S, C), BF16), _sds((S, C), F32)],
        scratch_shapes=[_window_scratch(T, C)],
        compiler_params=_params("parallel"))(y, y, y, y, y, y, conv_w32, conv_b, ln_g, ln_b)


def _conv_bwd(dac, u1, y, conv_w32, ln_g, ln_b, *, T=512):
    S = y.shape[0]
    T = min(T, S)
    nblk = S // T
    C = D_CONV

    def body(dp, dc, dn, up, uc, un, ap, ac, an, gp, gc, gn, w_ref, lg_ref, lb_ref,
             dag_ref, dw_ref, dsm_ref, bufd, bufu, du0_scr, dw_acc):
        i = pl.program_id(0)
        lg = lg_ref[...]

        def du1_of(dcv_ref, u1_ref):
            u1 = u1_ref[...]
            mu = jnp.mean(u1, axis=-1, keepdims=True)
            xc = u1 - mu
            rstd = lax.rsqrt(jnp.mean(xc * xc, axis=-1, keepdims=True) + EPS)
            xhat = xc * rstd
            u2 = xhat * lg + lb_ref[...]
            sg = _sigmoid(u2)
            du2 = dcv_ref[...].astype(F32) * (sg * (1.0 + u2 * (1.0 - sg)))
            dxh = du2 * lg
            du1 = rstd * (dxh - jnp.mean(dxh, axis=-1, keepdims=True)
                          - xhat * jnp.mean(dxh * xhat, axis=-1, keepdims=True))
            return du1, du2, xhat

        def glu(a_ref, g_ref):
            return a_ref[...].astype(F32) * _sigmoid(g_ref[...].astype(F32))

        @pl.when(i == 0)
        def _():
            dw_ref[...] = jnp.zeros_like(dw_ref)
            dsm_ref[...] = jnp.zeros_like(dsm_ref)

        du1_c, du2_c, xhat_c = du1_of(dc, uc)
        dsm_ref[0:1, :] += jnp.sum(du1_c, axis=0, keepdims=True)
        dsm_ref[1:2, :] += jnp.sum(du2_c * xhat_c, axis=0, keepdims=True)
        dsm_ref[2:3, :] += jnp.sum(du2_c, axis=0, keepdims=True)
        _fill_window(bufd, jnp.where(i > 0, du1_of(dp, up)[0], 0.0), du1_c,
                     jnp.where(i < nblk - 1, du1_of(dn, un)[0], 0.0), T)
        _fill_window(bufu, jnp.where(i > 0, glu(ap, gp), 0.0), glu(ac, gc),
                     jnp.where(i < nblk - 1, glu(an, gn), 0.0), T)

        _depthwise(bufd, w_ref, du0_scr, T, C, HALO + CONV_PAD, -1)
        dw_acc[...] = jnp.zeros_like(dw_acc)

        def dw_tile(t, carry):
            r0 = pl.multiple_of(t * CONV_ROWS, CONV_ROWS)
            for c0 in range(0, C, LANES):
                ls = slice(c0, c0 + LANES)
                d = bufd[0, pl.ds(HALO + r0, CONV_ROWS), ls]
                for k, rows in _tap_reads(bufu, HALO - CONV_PAD, 1, r0, ls):
                    prod = d * rows
                    part = prod[0:8]
                    for j in range(8, CONV_ROWS, 8):
                        part = part + prod[j:j + 8]
                    dw_acc[k, :, ls] += part
            return carry

        lax.fori_loop(0, T // CONV_ROWS, dw_tile, 0)
        for k in range(CONV_WIDTH):
            dw_ref[k:k + 1, :] += jnp.sum(dw_acc[k], axis=0, keepdims=True)
        du0 = du0_scr[...]
        a = ac[...].astype(F32)
        sg = _sigmoid(gc[...].astype(F32))
        dag_ref[:, 0:C] = (du0 * sg).astype(dag_ref.dtype)
        dag_ref[:, C:] = (du0 * a * sg * (1.0 - sg)).astype(dag_ref.dtype)

    dp, dc, dn = _halo_specs(S, T, C, 1)
    up, uc, un = _halo_specs(S, T, C, 0)
    ap, ac, an = _halo_specs(S, T, C, 3)
    gp, gc, gn = _halo_specs(S, T, C, 4)
    vec = pl.BlockSpec((1, C), lambda i: (0, 0))
    return pl.pallas_call(
        body, name="conv_bwd", grid=(nblk,),
        in_specs=[dp, dc, dn, up, uc, un, ap, ac, an, gp, gc, gn,
                  pl.BlockSpec((32, C), lambda i: (0, 0)), vec, vec],
        out_specs=[pl.BlockSpec((T, 2 * C), lambda i: (i, 0)), pl.BlockSpec((32, C), lambda i: (0, 0)),
                   pl.BlockSpec((8, C), lambda i: (0, 0))],
        out_shape=[_sds((S, 2 * C), BF16), _sds((32, C), F32), _sds((8, C), F32)],
        scratch_shapes=[_window_scratch(T, C), _window_scratch(T, C), pltpu.VMEM((T, C), F32),
                        pltpu.VMEM((CONV_WIDTH, 8, C), F32)],
        compiler_params=_params("arbitrary"))(dac, dac, dac, u1, u1, u1, y, y, y, y, y, y, conv_w32, ln_g, ln_b)


def _xatt_fwd(xq, xk, xv, *, tm=512):
    S = xq.shape[0]
    M = xk.shape[0]
    tm = min(tm, S)
    scale = XATT_HEAD_DIM ** -0.5

    def body(q_ref, k_ref, v_ref, o_ref):
        heads = [slice(h * XATT_HEAD_DIM, (h + 1) * XATT_HEAD_DIM) for h in range(XATT_HEADS)]
        s = [_nt(q_ref[:, sl], k_ref[:, sl]) * scale for sl in heads]
        e = [jnp.exp(t - jnp.max(t, axis=-1, keepdims=True)) for t in s]
        p = [t * (1.0 / jnp.sum(t, axis=-1, keepdims=True)) for t in e]
        for sl, t in zip(heads, p):
            o_ref[:, sl] = jnp.dot(t.astype(BF16), v_ref[:, sl], preferred_element_type=F32).astype(o_ref.dtype)

    row = pl.BlockSpec((tm, D_MODEL), lambda i: (i, 0))
    full = pl.BlockSpec((M, D_MODEL), lambda i: (0, 0))
    return pl.pallas_call(
        body, name="xatt_fwd", grid=(S // tm,), in_specs=[row, full, full], out_specs=row,
        out_shape=_sds((S, D_MODEL), BF16), compiler_params=_params("parallel"))(xq, xk, xv)


def _xatt_bwd(xq, xk, xv, dxo, *, tm=512):
    S = xq.shape[0]
    M = xk.shape[0]
    tm = min(tm, S)
    scale = XATT_HEAD_DIM ** -0.5

    def body(q_ref, k_ref, v_ref, do_ref, dq_ref, dk_ref, dv_ref):
        i = pl.program_id(0)

        @pl.when(i == 0)
        def _():
            dk_ref[...] = jnp.zeros_like(dk_ref)
            dv_ref[...] = jnp.zeros_like(dv_ref)

        heads = [slice(h * XATT_HEAD_DIM, (h + 1) * XATT_HEAD_DIM) for h in range(XATT_HEADS)]
        s = [_nt(q_ref[:, sl], k_ref[:, sl]) * scale for sl in heads]
        dp = [_nt(do_ref[:, sl], v_ref[:, sl]) for sl in heads]
        e = [jnp.exp(t - jnp.max(t, axis=-1, keepdims=True)) for t in s]
        p = [t * (1.0 / jnp.sum(t, axis=-1, keepdims=True)) for t in e]
        ds = [(pp * (t - jnp.sum(t * pp, axis=-1, keepdims=True))).astype(BF16) for pp, t in zip(p, dp)]
        for sl, pp, t in zip(heads, p, ds):
            dq_ref[:, sl] = (jnp.dot(t, k_ref[:, sl], preferred_element_type=F32) * scale).astype(dq_ref.dtype)
            dv_ref[:, sl] += _tn(pp.astype(BF16), do_ref[:, sl])
            dk_ref[:, sl] += _tn(t, q_ref[:, sl]) * scale

    row = pl.BlockSpec((tm, D_MODEL), lambda i: (i, 0))
    full = pl.BlockSpec((M, D_MODEL), lambda i: (0, 0))
    return pl.pallas_call(
        body, name="xatt_bwd", grid=(S // tm,), in_specs=[row, full, full, row], out_specs=[row, full, full],
        out_shape=[_sds((S, D_MODEL), BF16), _sds((M, D_MODEL), F32), _sds((M, D_MODEL), F32)],
        compiler_params=_params("arbitrary"))(xq, xk, xv, dxo)


def _row_tile(R):
    for t in (256, 128, 64, 32, 16, 8):
        if R % t == 0:
            return t
    return R


def _sum_partials(own, recv, me, *, name):
    _, R, C = own.shape
    t = _row_tile(R)

    def body(me_ref, own_ref, r_ref, o_ref):
        o_ref[...] = ((own_ref[...].astype(F32) + r_ref[0].astype(F32)) + r_ref[1].astype(F32)) + r_ref[2].astype(F32)

    return pl.pallas_call(
        body, name=name,
        grid_spec=pltpu.PrefetchScalarGridSpec(
            num_scalar_prefetch=1, grid=(R // t,),
            in_specs=[pl.BlockSpec((None, t, C), lambda i, me_ref: (me_ref[0], i, 0)),
                      pl.BlockSpec((3, t, C), lambda i, me_ref: (0, i, 0))],
            out_specs=pl.BlockSpec((t, C), lambda i, me_ref: (i, 0))),
        out_shape=_sds((R, C), F32), compiler_params=_params("parallel"))(me, own, recv)


def _adamw_math(w, g, m, v):
    m2 = ADAM_B1 * m + (1.0 - ADAM_B1) * g
    v2 = ADAM_B2 * v + (1.0 - ADAM_B2) * (g * g)
    m_hat = m2 / (1.0 - ADAM_B1 ** ADAM_STEP)
    v_hat = v2 / (1.0 - ADAM_B2 ** ADAM_STEP)
    delta = -ADAM_LR * (m_hat / (jnp.sqrt(v_hat) + ADAM_EPS) + ADAM_WD * w)
    return delta, m2, v2


def _adamw(parts, w, m, v, *, name):
    R, C = w.shape
    t = _row_tile(R)
    n = len(parts)

    def body(*refs):
        w_ref, m_ref, v_ref = refs[n:n + 3]
        g_ref, d_ref, m2_ref, v2_ref = refs[n + 3:]
        g = refs[0][...]
        for r in refs[1:n]:
            g = g + r[...]
        delta, m2, v2 = _adamw_math(w_ref[...], g, m_ref[...], v_ref[...])
        g_ref[...] = g
        d_ref[...] = delta
        m2_ref[...] = m2
        v2_ref[...] = v2

    blk = pl.BlockSpec((t, C), lambda i: (i, 0))
    return pl.pallas_call(
        body, name=name, grid=(R // t,), in_specs=[blk] * (n + 3), out_specs=[blk] * 4,
        out_shape=[_sds((R, C), F32)] * 4, compiler_params=_params("parallel"))(*parts, w, m, v)


def _adamw_small(gathered, chip, entries):
    _, R, C = gathered.shape
    n = len(entries)
    group = D_CONV // N_CHIPS

    def body(chip_ref, g_ref, *refs):
        ins, outs, tot_ref = refs[:3 * n], refs[3 * n:7 * n], refs[7 * n]
        tot = g_ref[0]
        for k in range(1, N_DEV):
            tot = tot + g_ref[k]
        tot_ref[...] = tot
        for e, ((kind, r), _, _, _) in enumerate(entries):
            if kind == "row":
                g = tot_ref[r:r + 1, :]
            elif kind == "gain":
                g = jnp.concatenate([tot_ref[r:r + 1, :], tot_ref[r + 1:r + 2, :]], axis=1)
            else:
                g = tot_ref[r:r + CONV_WIDTH, 0:group]
                for j in range(1, N_CHIPS):
                    g = jnp.where(chip_ref[0] == j, tot_ref[r:r + CONV_WIDTH, j * group:(j + 1) * group], g)
            delta, m2, v2 = _adamw_math(ins[3 * e][...], g, ins[3 * e + 1][...], ins[3 * e + 2][...])
            for o, val in zip(outs[4 * e:4 * e + 4], (g, delta, m2, v2)):
                o[...] = val

    whole = lambda a: pl.BlockSpec(a.shape, lambda i, c: (0,) * a.ndim)
    arrays = [a for _, w, m, v in entries for a in (w, m, v)]
    out_like = [w for _, w, _, _ in entries for _ in range(4)]
    tot_like = _sds((R, C), F32)
    out = pl.pallas_call(
        body, name="adamw_small",
        grid_spec=pltpu.PrefetchScalarGridSpec(
            num_scalar_prefetch=1, grid=(1,),
            in_specs=[whole(gathered)] + [whole(a) for a in arrays],
            out_specs=[whole(a) for a in out_like] + [whole(tot_like)]),
        out_shape=[_sds(a.shape, F32) for a in out_like] + [tot_like],
        compiler_params=_params("arbitrary"))(chip, gathered, *arrays)
    return out[-1], [tuple(out[4 * e:4 * e + 4]) for e in range(n)]


def _chip_peers():
    x, y = lax.axis_index("x"), lax.axis_index("y")
    return [(1 - x, y), (x, 1 - y), (1 - x, 1 - y)]


HBM_SPEC = pl.BlockSpec(memory_space=pltpu.HBM)
SEM_SPEC = pl.BlockSpec(memory_space=pltpu.SEMAPHORE)


def _exchange_peers(mode):
    x, y, c = lax.axis_index("x"), lax.axis_index("y"), lax.axis_index("c")
    if mode == "swap":
        return [(x, y, 1 - c)]
    if mode == "all":
        flips = [(fx, fy, fc) for fx in (0, 1) for fy in (0, 1) for fc in (0, 1)][1:]
        return [(1 - x if fx else x, 1 - y if fy else y, 1 - c if fc else c) for fx, fy, fc in flips]
    return [(px, py, c) for px, py in _chip_peers()]


def _exchange_start(mode, srcs, zones, *, name):
    n = len(srcs)

    def body(*refs):
        ins, lands = refs[:n], refs[n:2 * n]
        send_sems, recv_sems = refs[2 * n:3 * n], refs[3 * n:4 * n]
        token = refs[-1]
        x, y, c = lax.axis_index("x"), lax.axis_index("y"), lax.axis_index("c")
        mine = 2 * x + y if mode == "gather" else 4 * x + 2 * y + c
        for t in range(n):
            for k, (px, py, pc) in enumerate(_exchange_peers(mode)):
                if mode in ("gather", "all"):
                    s, d = ins[t], lands[t].at[mine]
                elif mode == "scatter":
                    s, d = ins[t].at[2 * px + py], lands[t].at[k]
                else:
                    s, d = ins[t], lands[t]
                pltpu.make_async_remote_copy(src_ref=s, dst_ref=d, send_sem=send_sems[t], recv_sem=recv_sems[t],
                                             device_id=(px, py, pc), device_id_type=MESH).start()
            if mode in ("gather", "all"):
                pltpu.make_async_copy(ins[t], lands[t].at[mine], send_sems[t]).start()
        token[...] = jnp.zeros_like(token)

    hbm = lambda a: pltpu.with_memory_space_constraint(a, pltpu.HBM)
    out = pl.pallas_call(
        body, name=name,
        in_specs=[HBM_SPEC] * (2 * n),
        out_specs=[SEM_SPEC] * (2 * n) + [HBM_SPEC] * (2 * n) + [pl.BlockSpec(memory_space=pltpu.VMEM)],
        out_shape=[pltpu.SemaphoreType.DMA(())] * (2 * n)
        + [pltpu.HBM(a.shape, a.dtype) for a in list(srcs) + list(zones)] + [_sds((8, LANES), F32)],
        input_output_aliases={i: 2 * n + i for i in range(2 * n)},
        compiler_params=pltpu.CompilerParams(has_side_effects=pltpu.SideEffectType.DATAFLOW_SIDE_EFFECTING),
    )(*[hbm(a) for a in list(srcs) + list(zones)])
    return out[:n], out[n:2 * n], out[2 * n:3 * n], out[3 * n:4 * n], out[-1]


def _exchange_wait(mode, started, after, *, name):
    send_sems, recv_sems, srcs, zones, _ = started
    n = len(srcs)
    afters = tuple(after) if isinstance(after, (tuple, list)) else (after,)

    def body(*refs):
        lands = refs[n:2 * n]
        send_refs, recv_refs = refs[2 * n:3 * n], refs[3 * n:4 * n]
        me = (lax.axis_index("x"), lax.axis_index("y"), lax.axis_index("c"))
        n_remote = {"gather": N_CHIPS - 1, "scatter": N_CHIPS - 1, "all": N_DEV - 1, "swap": 1}[mode]
        for t in range(n):
            got = lands[t] if mode == "swap" else lands[t].at[pl.ds(0, n_remote)]
            sent = lands[t] if mode in ("gather", "all") else got
            pltpu.make_async_remote_copy(src_ref=sent, dst_ref=sent, send_sem=send_refs[t], recv_sem=recv_refs[t],
                                         device_id=me, device_id_type=MESH).wait_send()
            pltpu.make_async_remote_copy(src_ref=got, dst_ref=got, send_sem=send_refs[t], recv_sem=recv_refs[t],
                                         device_id=me, device_id_type=MESH).wait_recv()

    out = pl.pallas_call(
        body, name=name,
        in_specs=[HBM_SPEC] * (2 * n) + [SEM_SPEC] * (2 * n) + [pl.BlockSpec(memory_space=pl.ANY)] * len(afters),
        out_specs=[HBM_SPEC] * (2 * n),
        out_shape=[pltpu.HBM(a.shape, a.dtype) for a in list(srcs) + list(zones)],
        input_output_aliases={i: i for i in range(2 * n)},
        compiler_params=pltpu.CompilerParams(has_side_effects=pltpu.SideEffectType.DATAFLOW_SIDE_EFFECTING),
    )(*srcs, *zones, *send_sems, *recv_sems, *afters)
    return out[:n], out[n:]


def _swap_with_sibling(parts):
    n = len(parts)

    def body(*refs):
        ins, outs = refs[:n], refs[n:2 * n]
        send_sems, recv_sems = refs[2 * n:]
        sib = (lax.axis_index("x"), lax.axis_index("y"), 1 - lax.axis_index("c"))
        cps = []
        for t in range(n):
            cp = pltpu.make_async_remote_copy(
                src_ref=ins[t], dst_ref=outs[t], send_sem=send_sems.at[t], recv_sem=recv_sems.at[t],
                device_id=sib, device_id_type=MESH)
            cp.start()
            cps.append(cp)
        for cp in cps:
            cp.wait()

    any_spec = pl.BlockSpec(memory_space=pl.ANY)
    return pl.pallas_call(
        body, name="swap_with_sibling", in_specs=[any_spec] * n, out_specs=[any_spec] * n,
        out_shape=[_sds(p.shape, p.dtype) for p in parts],
        scratch_shapes=[pltpu.SemaphoreType.DMA((n,)), pltpu.SemaphoreType.DMA((n,))])(*parts)


BIG = ("w_in", "w_out", "w_xq", "w_xk", "w_xv", "w_xo", "w_up", "w_down")
COL_SHARDED = ("w_in", "w_up")


def _as_matrix(name, w4):
    if name in COL_SHARDED:
        return w4
    return w4.reshape(1, w4.shape[0] * w4.shape[1], w4.shape[2])


def _shard_layout(name, g):
    if name in COL_SHARDED:
        return g
    return g.reshape(N_CHIPS, g.shape[0] * g.shape[1] // N_CHIPS, g.shape[2])


def _local_step(x, mem, target, vecs, comm):
    S = x.shape[0]
    tables = _rope_tables(S)

    xn = _rms_fwd(x, vecs["norm_mix_g"], name="rms_mix")
    w_in, conv_w32 = comm["first"]((xn,) + tuple(tables))
    y = _mm_nn(xn, w_in, name="mm_in", tm=2048, group=2)
    qk, v_perm = _rope_fwd(y, tables)
    v_src = [(y, 2)] + [(v, 0) for v in v_perm[1:]]
    outs, lses = zip(*[_att_fwd(qk[p], v_src[p], d, name=f"att_fwd_d{d}") for p, d in enumerate(DILATIONS)])
    att, lg = _att_combine(outs, lses)
    cv, u1 = _conv_fwd(y, conv_w32, vecs["conv_b"], vecs["conv_ln_g"], vecs["conv_ln_b"])
    Wm = {k: _as_matrix(k, v) for k, v in comm["rest"]((att, cv)).items()}
    Wm["w_in"] = w_in
    h1, hn = _mm_rows((att, cv), Wm["w_out"], _residual_norm_tail, name="mm_out_rms", rows_in=(x,),
                      vecs_in=(vecs["norm_x_g"],), rows_out=(F32, BF16))
    xq = _mm_nn(hn, Wm["w_xq"], name="mm_xq")
    mn = _rms_fwd(mem, vecs["norm_mem_g"], name="rms_mem")
    xk = _mm_nn(mn, Wm["w_xk"], name="mm_xk")
    xv = _mm_nn(mn, Wm["w_xv"], name="mm_xv")
    xo = _xatt_fwd(xq, xk, xv)
    h2, hm = _mm_rows(xo, Wm["w_xo"], _residual_norm_tail, name="mm_xo_rms", rows_in=(h1,),
                      vecs_in=(vecs["norm_mlp_g"],), rows_out=(F32, BF16))
    relu_up = _mm_nn(hm, Wm["w_up"], name="mm_up", relu=True, tm=2048)
    sums = ((8, D_MODEL),)
    dh3, dh3b, dg_final, loss = _mm_rows(
        relu_up, Wm["w_down"], _loss_tail, name="mm_down_loss", rows_in=(h2, target), vecs_in=(vecs["norm_final_g"],),
        rows_out=(F32, BF16), sums_out=sums + ((8, LANES),), a_squared=True, tm=256)
    g = {}
    g["w_down"] = _mm_tn(relu_up, dh3b, 1, name="dw_down", a_squared=True)
    dup = _mm_nt(dh3b, Wm["w_down"], name="d_act", out_dtype=BF16, mul=relu_up, tm=2048)
    g["w_up"] = _mm_tn(hm, dup, N_CHIPS, name="dw_up")
    sent = comm["send_mlp"]({k: _shard_layout(k, g[k]) for k in ("w_down", "w_up")})
    dh2, dh2b, dg_mlp = _mm_rows(
        dup, Wm["w_up"], _rms_bwd_tail(True), name="d_hm_rms", w_transposed=True, rows_in=(h2, dh3),
        vecs_in=(vecs["norm_mlp_g"] + sent[0:1, 0:1],), rows_out=(F32, BF16), sums_out=sums, tm=256)
    g["w_xo"] = _mm_tn(xo, dh2b, 1, name="dw_xo")
    dxo = _mm_nt(dh2b, Wm["w_xo"], name="d_xo", out_dtype=BF16)
    dxq, dxk, dxv = _xatt_bwd(xq, xk, xv, dxo)
    g["w_xq"] = _mm_tn(hn, dxq, 1, name="dw_xq")
    dh1, dh1b, dg_x = _mm_rows(
        dxq, Wm["w_xq"], _rms_bwd_tail(True), name="d_hn_rms", w_transposed=True, rows_in=(h1, dh2),
        vecs_in=(vecs["norm_x_g"],), rows_out=(F32, BF16), sums_out=sums)
    dxkb, dxvb = dxk.astype(BF16), dxv.astype(BF16)
    g["w_xk"] = _mm_tn(mn, dxkb, 1, name="dw_xk")
    g["w_xv"] = _mm_tn(mn, dxvb, 1, name="dw_xv")
    dmn = _mm_nt(jnp.concatenate([dxkb, dxvb], axis=1),
                 jnp.concatenate([Wm["w_xk"], Wm["w_xv"]], axis=2), name="d_mn", out_dtype=BF16)
    _, _, dg_mem = _rms_bwd(dmn, mem, vecs["norm_mem_g"], None, name="rms_bwd_mem", bf16_copy=False)
    g["w_out"] = jnp.concatenate([_mm_tn(att, dh1b, 1, name="dw_out_att"), _mm_tn(cv, dh1b, 1, name="dw_out_conv")],
                                 axis=1)
    sent = comm["send_att"]({k: _shard_layout(k, g[k]) for k in ("w_out", "w_xq", "w_xk", "w_xv", "w_xo")})
    dac = _mm_nt(dh1b, Wm["w_out"], name="d_mix", out_dtype=BF16)
    dag, dconv_w, dconv_small = _conv_bwd(dac, u1, y, conv_w32, vecs["conv_ln_g"] + sent[0:1, 0:1],
                                          vecs["conv_ln_b"])
    delta, do_perm = _att_delta(dac, att)
    do_src = [(dac, 0)] + [(t, 0) for t in do_perm[1:]]
    dq, dk, dv = zip(*[_att_bwd(qk[p], v_src[p], do_src[p], lg[p], delta[p], d, name=f"att_bwd_d{d}")
                       for p, d in enumerate(DILATIONS)])
    dy = _assemble_dy(dq, dk, dv, dag, tables)
    sent = comm["send_in"]({"w_in": _mm_tn(xn, dy, N_CHIPS, name="dw_in", group=2, tm=2048)})
    grad_x, dg_mix = _mm_rows(
        dy, Wm["w_in"], _rms_bwd_tail(False), name="d_xn_rms", w_transposed=True, rows_in=(x, dh1),
        vecs_in=(vecs["norm_mix_g"] + sent[0:1, 0:1],), rows_out=(F32,), sums_out=sums)

    small = dict(conv_w=dconv_w, conv_small=dconv_small, norm_mix_g=dg_mix, norm_x_g=dg_x, norm_mem_g=dg_mem,
                 norm_mlp_g=dg_mlp, norm_final_g=dg_final, loss=loss)
    return grad_x, small


SMALL_ORDER = ("conv_w", "conv_small", "norm_mix_g", "norm_x_g", "norm_mem_g", "norm_mlp_g", "norm_final_g", "loss")


def _pack_small(small):
    rows, offs, pos = [], {}, 0
    for k in SMALL_ORDER:
        a = small[k]
        a = a.reshape(a.shape[0] * a.shape[1] // SMALL_W, SMALL_W)
        pad = (-a.shape[0]) % 8
        if pad:
            a = jnp.pad(a, ((0, pad), (0, 0)))
        rows.append(a)
        offs[k] = pos
        pos += a.shape[0]
    return jnp.concatenate(rows, axis=0), offs


def kernel(x, mem, norm_mix_g, w_in, conv_w, conv_b, conv_ln_g, conv_ln_b, w_out, norm_x_g, norm_mem_g, w_xq, w_xk, w_xv, w_xo, norm_mlp_g, w_up, w_down, norm_final_g, loss_target, m_norm_mix_g, m_w_in, m_conv_w, m_conv_b, m_conv_ln_g, m_conv_ln_b, m_w_out, m_norm_x_g, m_norm_mem_g, m_w_xq, m_w_xk, m_w_xv, m_w_xo, m_norm_mlp_g, m_w_up, m_w_down, m_norm_final_g, v_norm_mix_g, v_w_in, v_conv_w, v_conv_b, v_conv_ln_g, v_conv_ln_b, v_w_out, v_norm_x_g, v_norm_mem_g, v_w_xq, v_w_xk, v_w_xv, v_w_xo, v_norm_mlp_g, v_w_up, v_w_down, v_norm_final_g):
    names = ("norm_mix_g", "w_in", "conv_w", "conv_b", "conv_ln_g", "conv_ln_b", "w_out", "norm_x_g", "norm_mem_g",
             "w_xq", "w_xk", "w_xv", "w_xo", "norm_mlp_g", "w_up", "w_down", "norm_final_g")
    wts = dict(zip(names, (norm_mix_g, w_in, conv_w, conv_b, conv_ln_g, conv_ln_b, w_out, norm_x_g, norm_mem_g,
                           w_xq, w_xk, w_xv, w_xo, norm_mlp_g, w_up, w_down, norm_final_g)))
    mom = dict(zip(names, (m_norm_mix_g, m_w_in, m_conv_w, m_conv_b, m_conv_ln_g, m_conv_ln_b, m_w_out, m_norm_x_g,
                           m_norm_mem_g, m_w_xq, m_w_xk, m_w_xv, m_w_xo, m_norm_mlp_g, m_w_up, m_w_down, m_norm_final_g)))
    var = dict(zip(names, (v_norm_mix_g, v_w_in, v_conv_w, v_conv_b, v_conv_ln_g, v_conv_ln_b, v_w_out, v_norm_x_g,
                           v_norm_mem_g, v_w_xq, v_w_xk, v_w_xv, v_w_xo, v_norm_mlp_g, v_w_up, v_w_down, v_norm_final_g)))
    chip = 2 * lax.axis_index("x") + lax.axis_index("y")

    def zone(shard):
        return lax.empty((N_CHIPS,) + shard.shape, shard.dtype)

    conv_w_pad = jnp.pad(wts["conv_w"][0], ((0, 1), (0, 0)))
    first_shards = [wts["w_in"][0].astype(BF16), conv_w_pad]
    gathering_first = _exchange_start("gather", first_shards, [zone(s) for s in first_shards],
                                      name="gather_first_start")
    rest = tuple(k for k in BIG if k != "w_in")
    behind_first = gathering_first[4][0, 0]
    rest_shards = [(wts[k][0] + behind_first).astype(BF16) for k in rest]
    gathering = gathering_rest = _exchange_start("gather", rest_shards, [zone(s) for s in rest_shards],
                                                 name="gather_rest_start")
    sending = {}

    def wait_first(after):
        _, (w_in_all, conv_w_all) = _exchange_wait("gather", gathering_first, after, name="gather_first_wait")
        return w_in_all, jnp.transpose(conv_w_all, (1, 0, 2)).reshape(32, D_CONV)

    def wait_rest(after):
        _, zones = _exchange_wait("gather", gathering_rest, after, name="gather_rest_wait")
        return dict(zip(rest, zones))

    def send(group, grads):
        keys = tuple(grads)
        zones = [lax.empty((N_CHIPS - 1,) + grads[k].shape[1:], grads[k].dtype) for k in keys]
        sending[group] = (keys, _exchange_start("scatter", [grads[k] for k in keys], zones,
                                                name=f"scatter_{group}_start"))
        return sending[group][1][4]

    comm = dict(first=wait_first, rest=wait_rest, send_mlp=lambda grads: send("mlp", grads),
                send_att=lambda grads: send("att", grads), send_in=lambda grads: send("in", grads))
    vecs = {k: wts[k] for k in ("conv_b", "conv_ln_g", "conv_ln_b", "norm_x_g", "norm_mem_g", "norm_mlp_g")}
    vecs["norm_mix_g"] = wts["norm_mix_g"] + gathering[4][0:1, 0:1]
    vecs["norm_final_g"] = wts["norm_final_g"].reshape(1, D_MODEL)
    grad_x, small = _local_step(x[0], mem[0], loss_target[0], vecs, comm)

    packed, offs = _pack_small(small)
    me_arr = jnp.reshape(chip, (1,)).astype(jnp.int32)
    gathering_small = _exchange_start("all", [packed], [lax.empty((N_DEV,) + packed.shape, packed.dtype)],
                                      name="allgather_small_start")
    sums = {}

    def settle(group, after):
        keys, started = sending[group]
        srcs, zones = _exchange_wait("scatter", started, after, name=f"scatter_{group}_wait")
        for k, own, got in zip(keys, srcs, zones):
            sums[k] = _sum_partials(own, got, me_arr, name=f"sum_{k}")

    settle("mlp", gathering_small[4])
    settle("att", gathering_small[4])
    early = tuple(sums)
    swapping = _exchange_start("swap", [sums[k] for k in early], [lax.empty(sums[k].shape, F32) for k in early],
                               name="swap_early_start")
    settle("in", swapping[4])
    _, (gath,) = _exchange_wait("all", gathering_small, sums["w_in"], name="allgather_small_wait")

    where = {"conv_w": ("conv_w", offs["conv_w"]), "conv_b": ("row", offs["conv_small"]),
             "conv_ln_g": ("row", offs["conv_small"] + 1), "conv_ln_b": ("row", offs["conv_small"] + 2)}
    where.update({k: ("gain", offs[k]) for k in ("norm_mix_g", "norm_x_g", "norm_mem_g", "norm_mlp_g", "norm_final_g")})
    as_2d = lambda a: a.reshape(a.shape[-2] if a.ndim > 1 else 1, a.shape[-1])
    tot_small, updates = _adamw_small(gath, me_arr, [(where[k], as_2d(wts[k]), as_2d(mom[k]), as_2d(var[k]))
                                                     for k in where])
    res = dict(zip(where, updates))
    loss = tot_small[offs["loss"], 0]

    sib = {"w_in": _swap_with_sibling([sums["w_in"]])[0]}
    mine_early, sib_early = _exchange_wait("swap", swapping, sib["w_in"], name="swap_early_wait")
    sums.update(zip(early, mine_early))
    sib.update(zip(early, sib_early))
    for k in BIG:
        res[k] = _adamw([sums[k], sib[k]], wts[k][0], mom[k][0], var[k][0], name=f"adamw_{k}")

    outs = [loss, grad_x[None]]
    for j in range(4):
        outs += [res[k][j].reshape(wts[k].shape) for k in names]
    return tuple(outs)
```
